```python
import jax, jax.numpy as jnp
from jax import lax
import numpy as np

D_MODEL = 1024
BATCH = 16
SEQ = 2048
DEPTH = 1

CTX_LEN = 256
GRID_W = 64
D_MIX = 2 * D_MODEL
D_SSD = D_MIX // 2
D_SC = D_MIX - D_SSD
SSD_HEAD_DIM = 64
SSD_HEADS = D_SSD // SSD_HEAD_DIM
SSD_GROUPS = 2
SSD_STATE = 128
SSD_CONV = 3
SSD_CHUNK = 128
SC_CONV = 3
SC_GROUPS = D_SC // 64
N_EXPERTS = 32
TOP_K = 4
D_FF = D_MODEL
SWIGLU_LIMIT = 7.0
SWIGLU_ALPHA = 1.702
MOE_BLOCK = 256
NORM_EPS = 1e-6
N_MOD = 6

Z0 = 0
X0 = Z0 + D_SSD
B0 = X0 + D_SSD
C0 = B0 + SSD_GROUPS * SSD_STATE
DT0 = C0 + SSD_GROUPS * SSD_STATE
SC0 = DT0 + 2 * SSD_HEADS
IN_W = SC0 + 3 * D_SC
XBC_W = DT0 - X0

kernel_name = "hybrid_ssd_shortconv_moe_dit_layer"


def rmsnorm(x, g):
    xf = x.astype(jnp.float32)
    y = xf * lax.rsqrt(jnp.mean(xf * xf, axis=-1, keepdims=True) + NORM_EPS)
    return (y * g.astype(jnp.float32)).astype(x.dtype)


def group_rmsnorm(x, g, groups):
    shp = x.shape
    xf = x.astype(jnp.float32).reshape(*shp[:-1], groups, shp[-1] // groups)
    y = xf * lax.rsqrt(jnp.mean(xf * xf, axis=-1, keepdims=True) + NORM_EPS)
    return (y.reshape(shp) * g.astype(jnp.float32)).astype(x.dtype)


def modulate(h, shift, scale):
    return h * (1 + scale) + shift


def flip(t):
    return t[:, ::-1]


def conv_seq(u, w):
    k_w = w.shape[0]
    pad = k_w // 2
    l = u.shape[1]
    up = jnp.pad(u, ((0, 0), (pad, pad), (0, 0)))
    return sum(up[:, k:k + l] * w[k] for k in range(k_w))


def conv_grid_cols(u, w):
    b, l, ch = u.shape
    rows = l // GRID_W
    k_w = w.shape[0]
    pad = k_w // 2
    ug = jnp.pad(u.reshape(b, rows, GRID_W, ch), ((0, 0), (pad, pad), (0, 0), (0, 0)))
    out = sum(ug[:, k:k + rows] * w[k] for k in range(k_w))
    return out.reshape(b, l, ch)


def ssd_decays(dt_raw, dt_bias, a_log):
    b, l, _ = dt_raw.shape
    dt = jax.nn.softplus(dt_raw.astype(jnp.float32).reshape(b, l, 2, SSD_HEADS) + dt_bias.astype(jnp.float32))
    a = -jnp.exp(a_log.astype(jnp.float32))
    return dt[:, :, 0], dt[:, :, 1], a[0], a[1]


def ssd_chunked(xs, dt, a, bm, cm, h0):
    b, l, h, p = xs.shape
    g, n = bm.shape[2], bm.shape[3]
    r = h // g
    nc = l // SSD_CHUNK
    q = SSD_CHUNK
    xdt = (xs.astype(jnp.float32) * dt[..., None]).reshape(b, nc, q, g, r, p)
    acs = jnp.cumsum((dt * a).reshape(b, nc, q, g, r), axis=2)
    bc = bm.astype(jnp.float32).reshape(b, nc, q, g, n)
    cc = cm.astype(jnp.float32).reshape(b, nc, q, g, n)
    seg = acs[:, :, :, None] - acs[:, :, None, :]
    lower = jnp.tril(jnp.ones((q, q), dtype=bool))[:, :, None, None]
    decay_ij = jnp.exp(jnp.where(lower, seg, -jnp.inf))
    cb = jnp.einsum('bcign,bcjgn->bcijg', cc, bc)
    y_diag = jnp.einsum('bcijg,bcijgr,bcjgrp->bcigrp', cb, decay_ij, xdt)
    decay_to_end = jnp.exp(acs[:, :, -1:] - acs)
    states = jnp.einsum('bcqgn,bcqgr,bcqgrp->bcgrpn', bc, decay_to_end, xdt)
    chunk_decay = jnp.exp(acs[:, :, -1])

    def step(hc, inp):
        dec, st = inp
        return hc * dec[..., None, None] + st, hc

    h_final, starts = lax.scan(step, h0.astype(jnp.float32).reshape(b, g, r, p, n),
                               (jnp.moveaxis(chunk_decay, 1, 0), jnp.moveaxis(states, 1, 0)))
    starts = jnp.moveaxis(starts, 0, 1)
    y_off = jnp.einsum('bcign,bcgrpn,bcigr->bcigrp', cc, starts, jnp.exp(acs))
    y = (y_diag + y_off).reshape(b, l, h, p).astype(xs.dtype)
    return y, h_final.reshape(b, h, p, n)


def ssd_final_state(xs, dt, a, bm):
    b, l, h, p = xs.shape
    g, n = bm.shape[2], bm.shape[3]
    r = h // g
    acs = jnp.cumsum(dt * a, axis=1)
    w = (jnp.exp(acs[:, -1:] - acs) * dt).reshape(b, l, g, r)
    st = jnp.einsum('blgn,blgr,blgrp->bgrpn', bm.astype(jnp.float32), w,
                    xs.astype(jnp.float32).reshape(b, l, g, r, p))
    return st.reshape(b, h, p, n)


def ssd_bidir(xs, bm, cm, dt_f, dt_b, a_f, a_b, d_skip, h0_f, h0_b):
    y_f, hf = ssd_chunked(xs, dt_f, a_f, bm, cm, h0_f)
    y_b, hb = ssd_chunked(flip(xs), flip(dt_b), a_b, flip(bm), flip(cm), h0_b)
    y = y_f + flip(y_b) + d_skip[:, None].astype(xs.dtype) * xs
    return y, hf, hb


def token_mixers(h, w_in, ssd_conv_w, ssd_conv_b, dt_bias, a_log, d_skip, ssd_norm_g, sc_conv_w, w_out,
                 sc_conv, h0_f, h0_b):
    b, l, _ = h.shape
    proj = h @ w_in
    z = proj[..., Z0:X0]
    xbc = jax.nn.silu(conv_seq(proj[..., X0:DT0], ssd_conv_w) + ssd_conv_b)
    xs = xbc[..., :B0 - X0].reshape(b, l, SSD_HEADS, SSD_HEAD_DIM)
    bm = xbc[..., B0 - X0:C0 - X0].reshape(b, l, SSD_GROUPS, SSD_STATE)
    cm = xbc[..., C0 - X0:].reshape(b, l, SSD_GROUPS, SSD_STATE)
    dt_f, dt_b, a_f, a_b = ssd_decays(proj[..., DT0:SC0], dt_bias, a_log)
    y, hf, hb = ssd_bidir(xs, bm, cm, dt_f, dt_b, a_f, a_b, d_skip, h0_f, h0_b)
    y_ssd = group_rmsnorm(y.reshape(b, l, D_SSD) * jax.nn.silu(z), ssd_norm_g, SSD_GROUPS)
    sc_b = proj[..., SC0:SC0 + D_SC]
    sc_c = proj[..., SC0 + D_SC:SC0 + 2 * D_SC]
    sc_u = proj[..., SC0 + 2 * D_SC:]
    y_sc = sc_b * sc_conv(sc_c * sc_u, sc_conv_w)
    out = jnp.concatenate([y_ssd, y_sc], axis=-1) @ w_out
    return out, hf, hb


def context_ssd_states(hc, w_in, ssd_conv_w, ssd_conv_b, dt_bias, a_log):
    b, l, _ = hc.shape
    xb = jax.nn.silu(conv_seq(hc @ w_in[:, X0:C0], ssd_conv_w[:, :C0 - X0]) + ssd_conv_b[:C0 - X0])
    xs = xb[..., :B0 - X0].reshape(b, l, SSD_HEADS, SSD_HEAD_DIM)
    bm = xb[..., B0 - X0:].reshape(b, l, SSD_GROUPS, SSD_STATE)
    dt_f, dt_b, a_f, a_b = ssd_decays(hc @ w_in[:, DT0:SC0], dt_bias, a_log)
    h_f = ssd_final_state(xs, dt_f, a_f, bm)
    h_b = ssd_final_state(flip(xs), flip(dt_b), a_b, flip(bm))
    return h_f, h_b


def moe(h, w_router, b_router, w_gate_up, b_gate_up, w_down, b_down):
    shp = h.shape
    d = shp[-1]
    t = h.reshape(-1, d)
    n_tok = t.shape[0]
    n_assign = n_tok * TOP_K
    logits = (t @ w_router + b_router).astype(jnp.float32)
    top_logits, top_idx = lax.top_k(logits, TOP_K)
    gates = jax.nn.softmax(top_logits, axis=-1).astype(h.dtype)
    expert = top_idx.reshape(-1).astype(jnp.int32)
    order = jnp.argsort(expert)
    expert_sorted = expert[order]
    counts = jnp.bincount(expert, length=N_EXPERTS).astype(jnp.int32)
    padded = (counts + MOE_BLOCK - 1) // MOE_BLOCK * MOE_BLOCK
    group_start = jnp.cumsum(counts) - counts
    pad_end = jnp.cumsum(padded)
    pad_start = pad_end - padded
    dest_sorted = (pad_start[expert_sorted] + jnp.arange(n_assign, dtype=jnp.int32)
                   - group_start[expert_sorted]).astype(jnp.int32)
    dest = jnp.zeros((n_assign,), jnp.int32).at[order].set(dest_sorted)
    n_blocks = -(-n_assign // MOE_BLOCK) + N_EXPERTS
    row_token = jnp.full((n_blocks * MOE_BLOCK,), n_tok, jnp.int32).at[dest].set(
        jnp.arange(n_assign, dtype=jnp.int32) // TOP_K)
    t_ext = jnp.concatenate([t, jnp.zeros((1, d), t.dtype)], axis=0)
    blocks = t_ext[row_token].reshape(n_blocks, MOE_BLOCK, d)
    block_expert = jnp.minimum(
        jnp.searchsorted(pad_end, jnp.arange(n_blocks, dtype=jnp.int32) * MOE_BLOCK, side='right'),
        N_EXPERTS - 1)

    def expert_block(args):
        xb, e = args
        gu = xb @ w_gate_up[e] + b_gate_up[e]
        glu = jnp.minimum(gu[:, :D_FF], SWIGLU_LIMIT)
        lin = jnp.clip(gu[:, D_FF:], -SWIGLU_LIMIT, SWIGLU_LIMIT)
        act = glu * jax.nn.sigmoid(SWIGLU_ALPHA * glu) * (lin + 1)
        return act @ w_down[e] + b_down[e]

    y_rows = lax.map(expert_block, (blocks, block_expert)).reshape(-1, d)
    y = y_rows[dest].reshape(n_tok, TOP_K, d)
    return jnp.einsum('tk,tkd->td', gates, y).reshape(shp)


def setup_inputs(seed: int = 0) -> dict:
    key = jax.random.key(seed)
    ks = jax.random.split(key, 26)

    def nrm(k, shape, scale):
        return jax.random.normal(k, shape, jnp.float32) * scale

    dt0 = jnp.exp(jax.random.uniform(ks[10], (DEPTH, 2, SSD_HEADS), jnp.float32,
                                     minval=float(np.log(1e-3)), maxval=float(np.log(1e-1))))
    dt_bias = dt0 + jnp.log(-jnp.expm1(-dt0))
    a_log = jnp.log(jax.random.uniform(ks[11], (DEPTH, 2, SSD_HEADS), jnp.float32, minval=1.0, maxval=16.0))
    return {
        "x": nrm(ks[0], (BATCH, SEQ, D_MODEL), 1.0),
        "c": nrm(ks[1], (BATCH, D_MODEL), 1.0),
        "ctx": nrm(ks[2], (BATCH, CTX_LEN, D_MODEL), 1.0),
        "c_ctx": nrm(ks[3], (D_MODEL,), 1.0),
        "w_mod": nrm(ks[4], (DEPTH, D_MODEL, N_MOD * D_MODEL), 0.5 * D_MODEL ** -0.5),
        "b_mod": nrm(ks[5], (DEPTH, N_MOD * D_MODEL), 0.02),
        "norm1_g": 1.0 + nrm(ks[6], (DEPTH, D_MODEL), 0.02),
        "w_in": nrm(ks[7], (DEPTH, D_MODEL, IN_W), D_MODEL ** -0.5),
        "ssd_conv_w": nrm(ks[8], (DEPTH, SSD_CONV, XBC_W), SSD_CONV ** -0.5),
        "ssd_conv_b": nrm(ks[9], (DEPTH, XBC_W), 0.02),
        "ssd_dt_bias": dt_bias,
        "ssd_a_log": a_log,
        "ssd_d": 1.0 + nrm(ks[12], (DEPTH, SSD_HEADS), 0.1),
        "ssd_norm_g": 1.0 + nrm(ks[13], (DEPTH, D_SSD), 0.02),
        "sc_conv_w": nrm(ks[14], (DEPTH, SC_CONV, D_SC), SC_CONV ** -0.5),
        "w_out": nrm(ks[15], (DEPTH, D_MIX, D_MODEL), D_MIX ** -0.5),
        "norm2_g": 1.0 + nrm(ks[16], (DEPTH, D_MODEL), 0.02),
        "w_router": nrm(ks[17], (DEPTH, D_MODEL, N_EXPERTS), D_MODEL ** -0.5),
        "b_router": nrm(ks[18], (DEPTH, N_EXPERTS), 0.01),
        "w_gate_up": nrm(ks[19], (DEPTH, N_EXPERTS, D_MODEL, 2 * D_FF), D_MODEL ** -0.5),
        "b_gate_up": nrm(ks[20], (DEPTH, N_EXPERTS, 2 * D_FF), 0.02),
        "w_down": nrm(ks[21], (DEPTH, N_EXPERTS, D_FF, D_MODEL), D_FF ** -0.5),
        "b_down": nrm(ks[22], (DEPTH, N_EXPERTS, D_MODEL), 0.02),
        "final_g": 1.0 + nrm(ks[23], (D_MODEL,), 0.02),
    }


def reference(x, c, ctx, c_ctx, w_mod, b_mod, norm1_g, w_in, ssd_conv_w, ssd_conv_b, ssd_dt_bias, ssd_a_log,
              ssd_d, ssd_norm_g, sc_conv_w, w_out, norm2_g, w_router, b_router, w_gate_up, b_gate_up, w_down,
              b_down, final_g):
    b = x.shape[0]
    ctx_s = ctx
    for i in range(DEPTH):
        mod_x = (jax.nn.silu(c) @ w_mod[i] + b_mod[i]).reshape(b, N_MOD, 1, D_MODEL)
        mod_c = (jax.nn.silu(c_ctx) @ w_mod[i] + b_mod[i]).reshape(N_MOD, 1, 1, D_MODEL)
        mix_w = (w_in[i], ssd_conv_w[i], ssd_conv_b[i], ssd_dt_bias[i], ssd_a_log[i], ssd_d[i],
                 ssd_norm_g[i], sc_conv_w[i], w_out[i])
        moe_w = (w_router[i], b_router[i], w_gate_up[i], b_gate_up[i], w_down[i], b_down[i])
        hc = modulate(rmsnorm(ctx_s, norm1_g[i]), mod_c[0], mod_c[1])
        if i == DEPTH - 1:
            h0_f, h0_b = context_ssd_states(hc, w_in[i], ssd_conv_w[i], ssd_conv_b[i],
                                            ssd_dt_bias[i], ssd_a_log[i])
        else:
            zero = jnp.zeros((ctx_s.shape[0], SSD_HEADS, SSD_HEAD_DIM, SSD_STATE), jnp.float32)
            oc, h0_f, h0_b = token_mixers(hc, *mix_w, conv_seq, zero, zero)
            ctx_s = ctx_s + mod_c[2] * oc
            ctx_s = ctx_s + mod_c[5] * moe(modulate(rmsnorm(ctx_s, norm2_g[i]), mod_c[3], mod_c[4]), *moe_w)
        hx = modulate(rmsnorm(x, norm1_g[i]), mod_x[:, 0], mod_x[:, 1])
        ox, _, _ = token_mixers(hx, *mix_w, conv_grid_cols, h0_f, h0_b)
        x = x + mod_x[:, 2] * ox
        x = x + mod_x[:, 5] * moe(modulate(rmsnorm(x, norm2_g[i]), mod_x[:, 3], mod_x[:, 4]), *moe_w)
    return rmsnorm(x, final_g)
```

```python
import functools

import jax
import jax.numpy as jnp
from jax import lax
from jax.experimental import pallas as pl
from jax.experimental.pallas import tpu as pltpu

F32 = jnp.float32
BF16 = jnp.bfloat16
HIGHEST = lax.Precision.HIGHEST

D_MODEL = 1024
SEQ = 2048
CTX_LEN = 256
GRID_W = 64
D_SSD = 1024
D_SC = 1024
HEAD_DIM = 64
HEADS = 16
GROUPS = 2
STATE = 128
CHUNK = 128
N_EXPERTS = 32
TOP_K = 4
D_FF = 1024
SWIGLU_LIMIT = 7.0
SWIGLU_ALPHA = 1.702
NORM_EPS = 1e-6
N_MOD = 6
XBC_W = D_SSD + 2 * GROUPS * STATE
XB_W = D_SSD + GROUPS * STATE
LANES = 128

Z0 = 0
X0 = Z0 + D_SSD
B0 = X0 + D_SSD
C0 = B0 + GROUPS * STATE
DT0 = C0 + GROUPS * STATE
SC0 = DT0 + 2 * HEADS

TOK_TILE = 512
MOE_BM = 256
RT_TILE = 512
DISP_TILE = 256
COMB_TILE = 128
VMEM_LIMIT = 56 * 1024 * 1024


def _silu(v):
    return v * jax.nn.sigmoid(v)


def _softplus(v):
    return jnp.maximum(v, 0.0) + jnp.log1p(jnp.exp(-jnp.abs(v)))


def _rms(v):
    return v * lax.rsqrt(jnp.mean(v * v, axis=-1, keepdims=True) + NORM_EPS)


def _dot(a, b):
    return jnp.dot(a, b, preferred_element_type=F32)


def _expand(v, e, pieces):
    acc = None
    rem = v
    for _ in range(pieces):
        p = rem.astype(BF16)
        rem = rem - p.astype(F32)
        t = _dot(p, e)
        acc = t if acc is None else acc + t
    return acc


def _mod_kernel(c_ref, w_ref, b_ref, o_ref):
    o_ref[...] = jnp.dot(_silu(c_ref[...]), w_ref[...], precision=HIGHEST,
                         preferred_element_type=F32) + b_ref[...]


def _mod(cvec, w_mod, b_mod):
    rows = cvec.shape[0]
    n = w_mod.shape[1]
    tn = 1536
    return pl.pallas_call(
        _mod_kernel,
        out_shape=jax.ShapeDtypeStruct((rows, n), F32),
        grid=(n // tn,),
        in_specs=[pl.BlockSpec((rows, D_MODEL), lambda j: (0, 0)),
                  pl.BlockSpec((D_MODEL, tn), lambda j: (0, j)),
                  pl.BlockSpec((1, tn), lambda j: (0, j))],
        out_specs=pl.BlockSpec((rows, tn), lambda j: (0, j)),
        compiler_params=pltpu.CompilerParams(dimension_semantics=("arbitrary",),
                                             vmem_limit_bytes=VMEM_LIMIT),
        name="mod",
    )(cvec, w_mod, b_mod)


def _ctx_kernel(ctx_ref, mod_ref, g1_ref, wxb_ref, wdt_ref, cw_ref, cb_ref, dtb_ref, alog_ref, e64_ref,
                h0_ref):
    L = CTX_LEN
    m = mod_ref[0]
    hc = _rms(ctx_ref[0]) * g1_ref[...] * (1.0 + m[1:2]) + m[0:1]
    hb = hc.astype(BF16)
    pxb = _dot(hb, wxb_ref[...])
    dtr = _dot(hb, wdt_ref[...])
    rowi = lax.broadcasted_iota(jnp.int32, (L, XB_W), 0)
    dn = jnp.where(rowi == 0, 0.0, pltpu.roll(pxb, 1, 0))
    up = jnp.where(rowi == L - 1, 0.0, pltpu.roll(pxb, L - 1, 0))
    cw = cw_ref[...]
    xb = _silu(cw[0:1] * dn + cw[1:2] * pxb + cw[2:3] * up + cb_ref[...])
    xs = xb[:, :D_SSD]
    bm = xb[:, D_SSD:].astype(BF16)
    dt = _softplus(dtr + dtb_ref[...])
    da = dt * (-jnp.exp(alog_ref[...]))
    ri = lax.broadcasted_iota(jnp.int32, (L, L), 0)
    ci = lax.broadcasted_iota(jnp.int32, (L, L), 1)
    for d in range(2):
        tri = (ci <= ri) if d == 0 else (ci >= ri)
        cum = jnp.dot(tri.astype(F32), da, precision=HIGHEST, preferred_element_type=F32)
        last = cum[L - 1:L] if d == 0 else cum[0:1]
        w_e = _expand(jnp.exp(last - cum) * dt, e64_ref[d], 2)
        xw = (xs * w_e).astype(BF16)
        for g in range(GROUPS):
            gw = D_SSD // GROUPS
            st = lax.dot_general(bm[:, g * STATE:(g + 1) * STATE], xw[:, g * gw:(g + 1) * gw],
                                 (((0,), (0,)), ((), ())), preferred_element_type=F32)
            h0_ref[0, d, :, g * gw:(g + 1) * gw] = st


def _ctx_states(ctx, mod3, g1, wxb, wdt, cw, cb, dtb, alog, e64):
    bsz = ctx.shape[0]
    mod_row = bsz
    const = lambda *shape: pl.BlockSpec(shape, lambda b: (0,) * len(shape))
    return pl.pallas_call(
        _ctx_kernel,
        out_shape=jax.ShapeDtypeStruct((bsz, 2, STATE, D_SSD), F32),
        grid=(bsz,),
        in_specs=[pl.BlockSpec((1, CTX_LEN, D_MODEL), lambda b: (b, 0, 0)),
                  pl.BlockSpec((1, N_MOD, D_MODEL), lambda b: (mod_row, 0, 0)),
                  const(1, D_MODEL), const(D_MODEL, XB_W), const(D_MODEL, LANES),
                  const(3, XB_W), const(1, XB_W), const(1, LANES), const(1, LANES),
                  const(2, LANES, D_SSD)],
        out_specs=pl.BlockSpec((1, 2, STATE, D_SSD), lambda b: (b, 0, 0, 0)),
        compiler_params=pltpu.CompilerParams(dimension_semantics=("arbitrary",),
                                             vmem_limit_bytes=VMEM_LIMIT),
        name="ctx_states",
    )(ctx, mod3, g1, wxb, wdt, cw, cb, dtb, alog, e64)


def _inproj_kernel(x_ref, mod_ref, g1_ref, wz_ref, wxbc_ref, wdt_ref, wb_ref, wc_ref, wu_ref,
                   z_ref, xbc_ref, dt_ref, scb_ref, v_ref):
    m = mod_ref[0]
    hx = _rms(x_ref[...]) * g1_ref[...] * (1.0 + m[1:2]) + m[0:1]
    hb = hx.astype(BF16)
    z_ref[...] = _dot(hb, wz_ref[...]).astype(BF16)
    xbc_ref[...] = _dot(hb, wxbc_ref[...]).astype(BF16)
    dt_ref[...] = _dot(hb, wdt_ref[...])
    scb_ref[...] = _dot(hb, wb_ref[...]).astype(BF16)
    v_ref[...] = (_dot(hb, wc_ref[...]) * _dot(hb, wu_ref[...])).astype(BF16)


def _inproj(x2, mod3, g1, wz, wxbc, wdt, wb, wc, wu):
    t = x2.shape[0]
    tm = TOK_TILE
    per_b = SEQ // tm
    const = lambda *shape: pl.BlockSpec(shape, lambda i: (0,) * len(shape))
    tile = lambda w: pl.BlockSpec((tm, w), lambda i: (i, 0))
    return pl.pallas_call(
        _inproj_kernel,
        out_shape=(jax.ShapeDtypeStruct((t, D_SSD), BF16), jax.ShapeDtypeStruct((t, XBC_W), BF16),
                   jax.ShapeDtypeStruct((t, LANES), F32), jax.ShapeDtypeStruct((t, D_SC), BF16),
                   jax.ShapeDtypeStruct((t, D_SC), BF16)),
        grid=(t // tm,),
        in_specs=[tile(D_MODEL),
                  pl.BlockSpec((1, N_MOD, D_MODEL), lambda i: (i // per_b, 0, 0)),
                  const(1, D_MODEL), const(D_MODEL, D_SSD), const(D_MODEL, XBC_W), const(D_MODEL, LANES),
                  const(D_MODEL, D_SC), const(D_MODEL, D_SC), const(D_MODEL, D_SC)],
        out_specs=(tile(D_SSD), tile(XBC_W), tile(LANES), tile(D_SC), tile(D_SC)),
        compiler_params=pltpu.CompilerParams(dimension_semantics=("arbitrary",),
                                             vmem_limit_bytes=VMEM_LIMIT),
        name="inproj",
    )(x2, mod3, g1, wz, wxbc, wdt, wb, wc, wu)


def _ssd_kernel(xbc_ref, z_ref, dt_ref, h0_ref, cw_ref, cb_ref, dtb_ref, alog_ref, dsk_ref, g_ref,
                e64_ref, e128_ref, o_ref, xc_ref, y_ref, s_ref):
    Q = CHUNK
    nck = SEQ // Q
    gw = D_SSD // GROUPS

    rowi = lax.broadcasted_iota(jnp.int32, (Q, XBC_W), 0)

    def conv_body(c, carry):
        r0 = pl.multiple_of(c * Q, Q)
        main = xbc_ref[0, pl.ds(r0, Q), :].astype(F32)
        pstart = pl.multiple_of(jnp.maximum(r0 - 16, 0), 16)
        nstart = pl.multiple_of(jnp.minimum(r0 + Q, SEQ - 16), 16)
        prev = xbc_ref[0, pl.ds(pstart, 16), :].astype(F32)[15:16]
        nxt = xbc_ref[0, pl.ds(nstart, 16), :].astype(F32)[0:1]
        prev = jnp.where(c > 0, prev, 0.0)
        nxt = jnp.where(c < nck - 1, nxt, 0.0)
        dn = jnp.where(rowi == 0, prev, pltpu.roll(main, 1, 0))
        up = jnp.where(rowi == Q - 1, nxt, pltpu.roll(main, Q - 1, 0))
        cw = cw_ref[...]
        conv = cw[0:1] * dn + cw[1:2] * main + cw[2:3] * up + cb_ref[...]
        xc_ref[pl.ds(r0, Q), :] = _silu(conv).astype(BF16)
        return carry

    lax.fori_loop(0, nck, conv_body, 0)

    ri = lax.broadcasted_iota(jnp.int32, (Q, Q), 0)
    ci = lax.broadcasted_iota(jnp.int32, (Q, Q), 1)
    lane = lax.broadcasted_iota(jnp.int32, (Q, LANES), 1)
    a_neg = -jnp.exp(alog_ref[...])

    def chunk(c, d):
        r0 = pl.multiple_of(c * Q, Q)
        rows = pl.ds(r0, Q)
        xs = xc_ref[rows, 0:D_SSD].astype(F32)
        bm = xc_ref[rows, D_SSD:D_SSD + GROUPS * STATE]
        cm = xc_ref[rows, D_SSD + GROUPS * STATE:XBC_W]
        dt = _softplus(dt_ref[0, rows, :] + dtb_ref[...])
        da = dt * a_neg
        tri = (ci <= ri) if d == 0 else (ci >= ri)
        cum = jnp.dot(tri.astype(F32), da, precision=HIGHEST, preferred_element_type=F32)
        cum_t = cum.T
        last = cum[Q - 1:Q] if d == 0 else cum[0:1]
        ecum_e = _expand(jnp.exp(cum), e64_ref[d], 2)
        dt_e = _expand(dt, e64_ref[d], 2)
        w_e = _expand(jnp.exp(last - cum) * dt, e64_ref[d], 2)
        colb = _expand(cum, e128_ref[d], 3)
        decay_e = ecum_e[Q - 1:Q] if d == 0 else ecum_e[0:1]

        gmat = [lax.dot_general(cm[:, g * STATE:(g + 1) * STATE], bm[:, g * STATE:(g + 1) * STATE],
                                (((1,), (1,)), ((), ())), preferred_element_type=F32)
                for g in range(GROUPS)]
        xdt = xs * dt_e
        y_parts = []
        for p in range(HEADS // 2):
            g = (2 * p) // (HEADS // GROUPS)
            ms = []
            for hh in (2 * p, 2 * p + 1):
                seg = colb[:, hh * LANES:(hh + 1) * LANES] - cum_t[HEADS * d + hh:HEADS * d + hh + 1, :]
                mm = jnp.where(tri, jnp.exp(jnp.where(tri, seg, 0.0)), 0.0) * gmat[g]
                ms.append(mm.astype(BF16))
            mcat = jnp.concatenate(ms, axis=1)
            xp = xdt[:, p * LANES:(p + 1) * LANES]
            rhs = jnp.concatenate([jnp.where(lane < HEAD_DIM, xp, 0.0).astype(BF16),
                                   jnp.where(lane >= HEAD_DIM, xp, 0.0).astype(BF16)], axis=0)
            y_parts.append(_dot(mcat, rhs))
        y_diag = jnp.concatenate(y_parts, axis=1)

        s_old = s_ref[...]
        s_bf = s_old.astype(BF16)
        y_off = jnp.concatenate(
            [_dot(cm[:, g * STATE:(g + 1) * STATE], s_bf[:, g * gw:(g + 1) * gw]) for g in range(GROUPS)],
            axis=1)
        y = y_diag + y_off * ecum_e

        xw = (xs * w_e).astype(BF16)
        upd = jnp.concatenate(
            [lax.dot_general(bm[:, g * STATE:(g + 1) * STATE], xw[:, g * gw:(g + 1) * gw],
                             (((0,), (0,)), ((), ())), preferred_element_type=F32) for g in range(GROUPS)],
            axis=1)
        s_ref[...] = s_old * decay_e + upd

        if d == 0:
            y_ref[rows, :] = y + dsk_ref[...] * xs
        else:
            tot = y_ref[rows, :] + y
            zz = z_ref[0, rows, :].astype(F32)
            gz = tot * _silu(zz)
            outs = []
            for g in range(GROUPS):
                gg = gz[:, g * gw:(g + 1) * gw]
                outs.append(gg * lax.rsqrt(jnp.mean(gg * gg, axis=-1, keepdims=True) + NORM_EPS))
            o_ref[0, rows, :] = (jnp.concatenate(outs, axis=1) * g_ref[...]).astype(BF16)

    s_ref[...] = h0_ref[0, 0]

    def fwd_body(i, carry):
        chunk(i, 0)
        return carry

    lax.fori_loop(0, nck, fwd_body, 0)

    s_ref[...] = h0_ref[0, 1]

    def bwd_body(i, carry):
        chunk(nck - 1 - i, 1)
        return carry

    lax.fori_loop(0, nck, bwd_body, 0)


def _ssd(xbc3, z3, dt3, h0, cw, cb, dtb, alog, dsk, g, e64, e128):
    bsz = xbc3.shape[0]
    const = lambda *shape: pl.BlockSpec(shape, lambda b: (0,) * len(shape))
    seq = lambda w: pl.BlockSpec((1, SEQ, w), lambda b: (b, 0, 0))
    return pl.pallas_call(
        _ssd_kernel,
        out_shape=jax.ShapeDtypeStruct((bsz, SEQ, D_SSD), BF16),
        grid=(bsz,),
        in_specs=[seq(XBC_W), seq(D_SSD), seq(LANES),
                  pl.BlockSpec((1, 2, STATE, D_SSD), lambda b: (b, 0, 0, 0)),
                  const(3, XBC_W), const(1, XBC_W), const(1, LANES), const(1, LANES),
                  const(1, D_SSD), const(1, D_SSD), const(2, LANES, D_SSD), const(2, LANES, HEADS * LANES)],
        out_specs=seq(D_SSD),
        scratch_shapes=[pltpu.VMEM((SEQ, XBC_W), BF16), pltpu.VMEM((SEQ, D_SSD), F32),
                        pltpu.VMEM((STATE, D_SSD), F32)],
        compiler_params=pltpu.CompilerParams(dimension_semantics=("arbitrary",),
                                             vmem_limit_bytes=VMEM_LIMIT),
        name="ssd",
    )(xbc3, z3, dt3, h0, cw, cb, dtb, alog, dsk, g, e64, e128)


def _outproj_kernel(x_ref, yssd_ref, scb_ref, v_ref, vp_ref, vn_ref, mod_ref, scw_ref, wo1_ref, wo2_ref,
                    g2_ref, wrt_ref, br_ref, x1_ref, h2_ref, lg_ref):
    tm = TOK_TILE
    per_b = SEQ // tm
    i = pl.program_id(0)
    first = (i % per_b) == 0
    last = (i % per_b) == per_b - 1
    m = mod_ref[0]
    v = v_ref[...].astype(F32)
    vp = jnp.where(first, 0.0, vp_ref[...].astype(F32))
    vn = jnp.where(last, 0.0, vn_ref[...].astype(F32))
    dn = jnp.concatenate([vp, v[:tm - GRID_W]], axis=0)
    up = jnp.concatenate([v[GRID_W:], vn], axis=0)
    scw = scw_ref[...]
    ysc = scb_ref[...].astype(F32) * (scw[0:1] * dn + scw[1:2] * v + scw[2:3] * up)
    out = _dot(yssd_ref[...], wo1_ref[...]) + _dot(ysc.astype(BF16), wo2_ref[...])
    x1 = x_ref[...] + m[2:3] * out
    x1_ref[...] = x1
    h2 = _rms(x1) * g2_ref[...] * (1.0 + m[4:5]) + m[3:4]
    h2_ref[...] = h2
    lg_ref[...] = lax.dot_general(wrt_ref[...], h2, (((1,), (1,)), ((), ())), precision=HIGHEST,
                                  preferred_element_type=F32) + br_ref[...]


def _outproj(x2, yssd, scb, v, mod3, scw, wo1, wo2, g2, wrt, br):
    t = x2.shape[0]
    tm = TOK_TILE
    per_b = SEQ // tm
    r = tm // GRID_W
    nrow = t // GRID_W
    const = lambda *shape: pl.BlockSpec(shape, lambda i: (0,) * len(shape))
    tile = lambda w: pl.BlockSpec((tm, w), lambda i: (i, 0))
    return pl.pallas_call(
        _outproj_kernel,
        out_shape=(jax.ShapeDtypeStruct((t, D_MODEL), F32), jax.ShapeDtypeStruct((t, D_MODEL), F32),
                   jax.ShapeDtypeStruct((N_EXPERTS, t), F32)),
        grid=(t // tm,),
        in_specs=[tile(D_MODEL), tile(D_SSD), tile(D_SC), tile(D_SC),
                  pl.BlockSpec((GRID_W, D_SC), lambda i: (jnp.maximum(i * r - 1, 0), 0)),
                  pl.BlockSpec((GRID_W, D_SC), lambda i: (jnp.minimum((i + 1) * r, nrow - 1), 0)),
                  pl.BlockSpec((1, N_MOD, D_MODEL), lambda i: (i // per_b, 0, 0)),
                  const(3, D_SC), const(D_SSD, D_MODEL), const(D_SC, D_MODEL), const(1, D_MODEL),
                  const(N_EXPERTS, D_MODEL), const(N_EXPERTS, 1)],
        out_specs=(tile(D_MODEL), tile(D_MODEL), pl.BlockSpec((N_EXPERTS, tm), lambda i: (0, i))),
        compiler_params=pltpu.CompilerParams(dimension_semantics=("arbitrary",),
                                             vmem_limit_bytes=VMEM_LIMIT),
        name="outproj",
    )(x2, yssd, scb, v, v, v, mod3, scw, wo1, wo2, g2, wrt, br)


def _route_kernel(lg_ref, dest_ref, gate_ref, meta_ref, idx_ref, rank_ref, carry_ref, *, n_tok, n_blocks):
    tt = RT_TILE
    ne = N_EXPERTS
    eio = lax.broadcasted_iota(jnp.int32, (ne, tt), 0)
    si = lax.broadcasted_iota(jnp.int32, (tt, tt), 0)
    ti = lax.broadcasted_iota(jnp.int32, (tt, tt), 1)
    before = (si < ti).astype(BF16)
    carry_ref[...] = jnp.zeros_like(carry_ref)

    def tile_body(j, c):
        t0 = pl.multiple_of(j * tt, tt)
        l = lg_ref[:, pl.ds(t0, tt)]
        onehot = jnp.zeros((ne, tt), F32)
        tops, sels = [], []
        for _ in range(TOP_K):
            mx = jnp.max(l, axis=0, keepdims=True)
            idx = jnp.min(jnp.where(l == mx, eio, ne), axis=0, keepdims=True)
            sel = eio == idx
            l = jnp.where(sel, -jnp.inf, l)
            onehot = onehot + sel.astype(F32)
            tops.append(mx)
            sels.append(sel)
            idx_ref[pl.ds(len(tops) - 1, 1), pl.ds(t0, tt)] = idx
        ex = [jnp.exp(tv - tops[0]) for tv in tops]
        den = ex[0] + ex[1] + ex[2] + ex[3]
        prefix = _dot(onehot.astype(BF16), before) + carry_ref[:, 0:1]
        for k in range(TOP_K):
            gate_ref[pl.ds(k, 1), pl.ds(t0, tt)] = ex[k] / den
            rk = jnp.sum(jnp.where(sels[k], prefix, 0.0), axis=0, keepdims=True)
            rank_ref[pl.ds(k, 1), pl.ds(t0, tt)] = rk.astype(jnp.int32)
        carry_ref[...] = carry_ref[...] + jnp.sum(onehot, axis=1, keepdims=True)
        return c

    lax.fori_loop(0, n_tok // tt, tile_body, 0)

    counts = carry_ref[...]
    padded = jnp.floor((counts + (MOE_BM - 1)) * (1.0 / MOE_BM)) * MOE_BM
    er = lax.broadcasted_iota(jnp.int32, (ne, ne), 0)
    ec = lax.broadcasted_iota(jnp.int32, (ne, ne), 1)
    pad_start = jnp.dot((ec < er).astype(F32), padded, precision=HIGHEST, preferred_element_type=F32)
    pad_end = pad_start + padded

    def dest_body(j, c):
        t0 = pl.multiple_of(j * tt, tt)
        for k in range(TOP_K):
            idx = idx_ref[pl.ds(k, 1), pl.ds(t0, tt)]
            base = jnp.sum(jnp.where(eio == idx, pad_start[:, 0:1], 0.0), axis=0, keepdims=True)
            dest_ref[pl.ds(k, 1), pl.ds(t0, tt)] = base.astype(jnp.int32) + rank_ref[pl.ds(k, 1), pl.ds(t0, tt)]
        return c

    lax.fori_loop(0, n_tok // tt, dest_body, 0)

    width = meta_ref.shape[1]
    sub = lax.broadcasted_iota(jnp.int32, (ne, width), 0)
    lan = lax.broadcasted_iota(jnp.int32, (ne, width), 1)
    diag = sub == lan
    cnt_row = jnp.sum(jnp.where(diag, counts[:, 0:1], 0.0), axis=0, keepdims=True)
    start_row = jnp.sum(jnp.where(diag, pad_start[:, 0:1], 0.0), axis=0, keepdims=True)
    blk_start = (lan * MOE_BM).astype(F32)
    blk_exp = jnp.sum((pad_end[:, 0:1] <= blk_start).astype(F32), axis=0, keepdims=True)
    blk_exp = jnp.minimum(blk_exp, float(ne - 1))
    used = jnp.sum(padded[:, 0:1], axis=0, keepdims=True) * (1.0 / MOE_BM)
    meta_ref[0:1, :] = cnt_row.astype(jnp.int32)
    meta_ref[1:2, :] = start_row.astype(jnp.int32)
    meta_ref[2:3, :] = blk_exp.astype(jnp.int32)
    meta_ref[3:4, :] = jnp.broadcast_to(used, (1, width)).astype(jnp.int32)
    meta_ref[4:8, :] = jnp.zeros((4, width), jnp.int32)


def _route(lgt, n_blocks):
    ne, n_tok = lgt.shape
    width = -(-n_blocks // LANES) * LANES
    full = lambda *shape: pl.BlockSpec(shape, lambda: (0,) * len(shape))
    return pl.pallas_call(
        functools.partial(_route_kernel, n_tok=n_tok, n_blocks=n_blocks),
        out_shape=(jax.ShapeDtypeStruct((TOP_K, n_tok), jnp.int32),
                   jax.ShapeDtypeStruct((TOP_K, n_tok), F32),
                   jax.ShapeDtypeStruct((8, width), jnp.int32)),
        in_specs=[full(ne, n_tok)],
        out_specs=(full(TOP_K, n_tok), full(TOP_K, n_tok), full(8, width)),
        scratch_shapes=[pltpu.VMEM((TOP_K, n_tok), jnp.int32), pltpu.VMEM((TOP_K, n_tok), jnp.int32),
                        pltpu.VMEM((ne, LANES), F32)],
        compiler_params=pltpu.CompilerParams(vmem_limit_bytes=VMEM_LIMIT),
        name="route",
    )(lgt)


def _dispatch_kernel(dest_ref, cnt_ref, start_ref, nu_ref, h2_ref, zero_ref, xs_ref, sem, zsem):
    i = pl.program_id(0)
    tl = DISP_TILE

    @pl.when(i == 0)
    def _():
        def per_expert(e, c):
            cnt = cnt_ref[e]
            base = start_ref[e]
            pad = (-cnt) & (MOE_BM - 1)

            def per_row(r, c2):
                cp = pltpu.make_async_copy(zero_ref.at[pl.ds(0, 1)], xs_ref.at[pl.ds(base + cnt + r, 1)], zsem)
                cp.start()
                cp.wait()
                return c2

            return lax.fori_loop(0, pad, per_row, c)

        lax.fori_loop(0, N_EXPERTS, per_expert, 0)

        def per_block(b, c):
            cp = pltpu.make_async_copy(zero_ref, xs_ref.at[pl.ds(b * MOE_BM, MOE_BM)], zsem)
            cp.start()
            cp.wait()
            return c

        lax.fori_loop(nu_ref[0], xs_ref.shape[0] // MOE_BM, per_block, 0)

    def per_tok(t, c):
        tok = i * tl + t
        for k in range(TOP_K):
            d = dest_ref[tok * TOP_K + k]
            pltpu.make_async_copy(h2_ref.at[pl.ds(tok, 1)], xs_ref.at[pl.ds(d, 1)], sem).start()
        return c

    lax.fori_loop(0, tl, per_tok, 0)
    pltpu.make_async_copy(h2_ref.at[pl.ds(0, tl * TOP_K)], xs_ref.at[pl.ds(0, tl * TOP_K)], sem).wait()


def _dispatch(dest_flat, cnt, start, n_used, h2, n_rows):
    n_tok = h2.shape[0]
    zero = jnp.zeros((MOE_BM, D_MODEL), F32)
    return pl.pallas_call(
        _dispatch_kernel,
        out_shape=jax.ShapeDtypeStruct((n_rows, D_MODEL), F32),
        grid_spec=pltpu.PrefetchScalarGridSpec(
            num_scalar_prefetch=4,
            grid=(n_tok // DISP_TILE,),
            in_specs=[pl.BlockSpec(memory_space=pl.ANY), pl.BlockSpec(memory_space=pl.ANY)],
            out_specs=pl.BlockSpec(memory_space=pl.ANY),
            scratch_shapes=[pltpu.SemaphoreType.DMA, pltpu.SemaphoreType.DMA]),
        compiler_params=pltpu.CompilerParams(dimension_semantics=("arbitrary",)),
        name="dispatch",
    )(dest_flat, cnt, start, n_used, h2, zero)


def _expert_kernel(be_ref, nu_ref, xs_ref, wgu_ref, bgu_ref, wd_ref, bd_ref, y_ref):
    j = pl.program_id(0)

    @pl.when(j < nu_ref[0])
    def _():
        xb = xs_ref[...].astype(BF16)
        gu = _dot(xb, wgu_ref[0]) + bgu_ref[0]
        glu = jnp.minimum(gu[:, :D_FF], SWIGLU_LIMIT)
        lin = jnp.clip(gu[:, D_FF:], -SWIGLU_LIMIT, SWIGLU_LIMIT)
        act = glu * jax.nn.sigmoid(SWIGLU_ALPHA * glu) * (lin + 1.0)
        y_ref[...] = _dot(act.astype(BF16), wd_ref[0]) + bd_ref[0]

    @pl.when(j >= nu_ref[0])
    def _():
        y_ref[...] = jnp.zeros_like(y_ref)


def _experts(blk_exp, n_used, xs, wgu, bgu, wd, bd):
    n_rows = xs.shape[0]
    nb = n_rows // MOE_BM
    row_blk = lambda j, be, nu: (jnp.minimum(j, nu[0] - 1), 0)
    out_blk = lambda j, be, nu: (j, 0)
    per_e = lambda j, be, nu: (be[j], 0, 0)
    return pl.pallas_call(
        _expert_kernel,
        out_shape=jax.ShapeDtypeStruct((n_rows, D_MODEL), F32),
        grid_spec=pltpu.PrefetchScalarGridSpec(
            num_scalar_prefetch=2,
            grid=(nb,),
            in_specs=[pl.BlockSpec((MOE_BM, D_MODEL), row_blk),
                      pl.BlockSpec((1, D_MODEL, 2 * D_FF), per_e),
                      pl.BlockSpec((1, 1, 2 * D_FF), per_e),
                      pl.BlockSpec((1, D_FF, D_MODEL), per_e),
                      pl.BlockSpec((1, 1, D_MODEL), per_e)],
            out_specs=pl.BlockSpec((MOE_BM, D_MODEL), out_blk)),
        compiler_params=pltpu.CompilerParams(dimension_semantics=("arbitrary",),
                                             vmem_limit_bytes=VMEM_LIMIT),
        name="experts",
    )(blk_exp, n_used, xs, wgu, bgu, wd, bd)


def _combine_kernel(dest_ref, ys_ref, x1_ref, gate_ref, mod_ref, fg_ref, o_ref, buf_ref, sem):
    i = pl.program_id(0)
    n = pl.num_programs(0)
    tc = COMB_TILE
    slot = i % 2

    def issue(tile, sl):
        def per_tok(t, c):
            tok = tile * tc + t
            for k in range(TOP_K):
                d = dest_ref[tok * TOP_K + k]
                pltpu.make_async_copy(ys_ref.at[pl.ds(d, 1)], buf_ref.at[sl, pl.ds(k * tc + t, 1)],
                                      sem.at[sl]).start()
            return c

        lax.fori_loop(0, tc, per_tok, 0)

    @pl.when(i == 0)
    def _():
        issue(0, 0)

    @pl.when(i + 1 < n)
    def _():
        issue(i + 1, 1 - slot)

    pltpu.make_async_copy(ys_ref.at[pl.ds(0, tc * TOP_K)], buf_ref.at[slot], sem.at[slot]).wait()

    gate = gate_ref[...]
    acc = None
    for k in range(TOP_K):
        term = gate[:, k:k + 1] * buf_ref[slot, pl.ds(k * tc, tc), :]
        acc = term if acc is None else acc + term
    m = mod_ref[0]
    x2 = x1_ref[...] + m[5:6] * acc
    o_ref[...] = _rms(x2) * fg_ref[...]


def _combine(dest_flat, ys, x1, gates, mod3, fg):
    n_tok = x1.shape[0]
    tc = COMB_TILE
    per_b = SEQ // tc
    return pl.pallas_call(
        _combine_kernel,
        out_shape=jax.ShapeDtypeStruct((n_tok, D_MODEL), F32),
        grid_spec=pltpu.PrefetchScalarGridSpec(
            num_scalar_prefetch=1,
            grid=(n_tok // tc,),
            in_specs=[pl.BlockSpec(memory_space=pl.ANY),
                      pl.BlockSpec((tc, D_MODEL), lambda i, d: (i, 0)),
                      pl.BlockSpec((tc, TOP_K), lambda i, d: (i, 0)),
                      pl.BlockSpec((1, N_MOD, D_MODEL), lambda i, d: (i // per_b, 0, 0)),
                      pl.BlockSpec((1, D_MODEL), lambda i, d: (0, 0))],
            out_specs=pl.BlockSpec((tc, D_MODEL), lambda i, d: (i, 0)),
            scratch_shapes=[pltpu.VMEM((2, tc * TOP_K, D_MODEL), F32), pltpu.SemaphoreType.DMA((2,))]),
        compiler_params=pltpu.CompilerParams(dimension_semantics=("arbitrary",),
                                             vmem_limit_bytes=VMEM_LIMIT),
        name="combine",
    )(dest_flat, ys, x1, gates, mod3, fg)


def _expansion_matrices():
    r = jnp.arange(LANES)[:, None]
    out64, out128 = [], []
    for d in range(2):
        l64 = jnp.arange(D_SSD)[None, :]
        l128 = jnp.arange(HEADS * LANES)[None, :]
        out64.append((l64 // HEAD_DIM == r - HEADS * d).astype(BF16))
        out128.append((l128 // LANES == r - HEADS * d).astype(BF16))
    return jnp.stack(out64), jnp.stack(out128)


def _pad_lanes(v):
    return jnp.pad(v, [(0, 0)] * (v.ndim - 1) + [(0, LANES - v.shape[-1])])


def kernel(x, c, ctx, c_ctx, w_mod, b_mod, norm1_g, w_in, ssd_conv_w, ssd_conv_b, ssd_dt_bias, ssd_a_log,
           ssd_d, ssd_norm_g, sc_conv_w, w_out, norm2_g, w_router, b_router, w_gate_up, b_gate_up, w_down,
           b_down, final_g):
    bsz = x.shape[0]
    n_tok = bsz * SEQ
    n_assign = n_tok * TOP_K
    n_blocks = n_assign // MOE_BM + N_EXPERTS
    n_rows = n_blocks * MOE_BM
    li = 0

    cvec = jnp.concatenate([c, c_ctx[None, :], jnp.zeros((7, D_MODEL), F32)], axis=0)
    mod3 = _mod(cvec, w_mod[li], b_mod[li][None, :]).reshape(bsz + 8, N_MOD, D_MODEL)

    w = w_in[li]
    wz = w[:, Z0:X0].astype(BF16)
    wxbc = w[:, X0:DT0].astype(BF16)
    wdt = _pad_lanes(w[:, DT0:SC0]).astype(BF16)
    wb = w[:, SC0:SC0 + D_SC].astype(BF16)
    wc = w[:, SC0 + D_SC:SC0 + 2 * D_SC].astype(BF16)
    wu = w[:, SC0 + 2 * D_SC:].astype(BF16)
    g1 = norm1_g[li][None, :]
    cw = ssd_conv_w[li]
    cb = ssd_conv_b[li][None, :]
    dtb = _pad_lanes(ssd_dt_bias[li].reshape(1, 2 * HEADS))
    alog = _pad_lanes(ssd_a_log[li].reshape(1, 2 * HEADS))
    e64, e128 = _expansion_matrices()

    h0 = _ctx_states(ctx, mod3, g1, wxbc[:, :XB_W], wdt, cw[:, :XB_W], cb[:, :XB_W], dtb, alog, e64)

    x2 = x.reshape(n_tok, D_MODEL)
    z, xbc, dtr, scb, v = _inproj(x2, mod3, g1, wz, wxbc, wdt, wb, wc, wu)

    dsk = jnp.repeat(ssd_d[li], HEAD_DIM)[None, :]
    yssd = _ssd(xbc.reshape(bsz, SEQ, XBC_W), z.reshape(bsz, SEQ, D_SSD), dtr.reshape(bsz, SEQ, LANES), h0,
                cw, cb, dtb, alog, dsk, ssd_norm_g[li][None, :], e64, e128)

    wo = w_out[li].astype(BF16)
    x1, h2, lgt = _outproj(x2, yssd.reshape(n_tok, D_SSD), scb, v, mod3, sc_conv_w[li], wo[:D_SSD], wo[D_SSD:],
                           norm2_g[li][None, :], w_router[li].T, b_router[li][:, None])

    dest_t, gate_t, meta = _route(lgt, n_blocks)
    dest_flat = dest_t.T.reshape(n_assign)
    gates = gate_t.T
    cnt = meta[0, :N_EXPERTS]
    start = meta[1, :N_EXPERTS]
    blk_exp = meta[2, :n_blocks]
    n_used = meta[3, :1]

    xs = _dispatch(dest_flat, cnt, start, n_used, h2, n_rows)
    ys = _experts(blk_exp, n_used, xs, w_gate_up[li].astype(BF16), b_gate_up[li][:, None, :],
                  w_down[li].astype(BF16), b_down[li][:, None, :])
    out = _combine(dest_flat, ys, x1, gates, mod3, final_g[None, :])
    return out.reshape(bsz, SEQ, D_MODEL)
```

```python
import functools

import jax
import jax.numpy as jnp
from jax import lax
from jax.experimental import pallas as pl
from jax.experimental.pallas import tpu as pltpu

F32 = jnp.float32
BF16 = jnp.bfloat16
HIGHEST = lax.Precision.HIGHEST

D_MODEL = 1024
SEQ = 2048
CTX_LEN = 256
GRID_W = 64
D_SSD = 1024
D_SC = 1024
HEAD_DIM = 64
HEADS = 16
GROUPS = 2
STATE = 128
CHUNK = 128
N_EXPERTS = 32
TOP_K = 4
D_FF = 1024
SWIGLU_LIMIT = 7.0
SWIGLU_ALPHA = 1.702
NORM_EPS = 1e-6
N_MOD = 6
XBC_W = D_SSD + 2 * GROUPS * STATE
XB_W = D_SSD + GROUPS * STATE
LANES = 128

Z0 = 0
X0 = Z0 + D_SSD
B0 = X0 + D_SSD
C0 = B0 + GROUPS * STATE
DT0 = C0 + GROUPS * STATE
SC0 = DT0 + 2 * HEADS

TOK_TILE = 512
MOE_BM = 256
RT_TILE = 512
DISP_TILE = 256
COMB_TILE = 128
VMEM_LIMIT = 56 * 1024 * 1024


def _silu(v):
    return v * jax.nn.sigmoid(v)


def _softplus(v):
    return jnp.maximum(v, 0.0) + jnp.log1p(jnp.exp(-jnp.abs(v)))


def _rms(v):
    return v * lax.rsqrt(jnp.mean(v * v, axis=-1, keepdims=True) + NORM_EPS)


def _dot(a, b):
    return jnp.dot(a, b, preferred_element_type=F32)


def _expand(v, e, pieces):
    acc = None
    rem = v
    for _ in range(pieces):
        p = rem.astype(BF16)
        rem = rem - p.astype(F32)
        t = _dot(p, e)
        acc = t if acc is None else acc + t
    return acc


def _mod_kernel(c_ref, w_ref, b_ref, o_ref):
    o_ref[...] = jnp.dot(_silu(c_ref[...]), w_ref[...], precision=HIGHEST,
                         preferred_element_type=F32) + b_ref[...]


def _mod(cvec, w_mod, b_mod):
    rows = cvec.shape[0]
    n = w_mod.shape[1]
    tn = 1536
    return pl.pallas_call(
        _mod_kernel,
        out_shape=jax.ShapeDtypeStruct((rows, n), F32),
        grid=(n // tn,),
        in_specs=[pl.BlockSpec((rows, D_MODEL), lambda j: (0, 0)),
                  pl.BlockSpec((D_MODEL, tn), lambda j: (0, j)),
                  pl.BlockSpec((1, tn), lambda j: (0, j))],
        out_specs=pl.BlockSpec((rows, tn), lambda j: (0, j)),
        compiler_params=pltpu.CompilerParams(dimension_semantics=("arbitrary",),
                                             vmem_limit_bytes=VMEM_LIMIT),
        name="mod",
    )(cvec, w_mod, b_mod)


def _ctx_kernel(ctx_ref, mod_ref, g1_ref, wxb_ref, wdt_ref, cw_ref, cb_ref, dtb_ref, alog_ref, e64_ref,
                h0_ref):
    L = CTX_LEN
    m = mod_ref[0]
    hc = _rms(ctx_ref[0]) * g1_ref[...] * (1.0 + m[1:2]) + m[0:1]
    hb = hc.astype(BF16)
    pxb = _dot(hb, wxb_ref[...])
    dtr = _dot(hb, wdt_ref[...])
    rowi = lax.broadcasted_iota(jnp.int32, (L, XB_W), 0)
    dn = jnp.where(rowi == 0, 0.0, pltpu.roll(pxb, 1, 0))
    up = jnp.where(rowi == L - 1, 0.0, pltpu.roll(pxb, L - 1, 0))
    cw = cw_ref[...]
    xb = _silu(cw[0:1] * dn + cw[1:2] * pxb + cw[2:3] * up + cb_ref[...])
    xs = xb[:, :D_SSD]
    bm = xb[:, D_SSD:].astype(BF16)
    dt = _softplus(dtr + dtb_ref[...])
    da = dt * (-jnp.exp(alog_ref[...]))
    ri = lax.broadcasted_iota(jnp.int32, (L, L), 0)
    ci = lax.broadcasted_iota(jnp.int32, (L, L), 1)
    for d in range(2):
        tri = (ci <= ri) if d == 0 else (ci >= ri)
        cum = jnp.dot(tri.astype(F32), da, precision=HIGHEST, preferred_element_type=F32)
        last = cum[L - 1:L] if d == 0 else cum[0:1]
        w_e = _expand(jnp.exp(last - cum) * dt, e64_ref[d], 2)
        xw = (xs * w_e).astype(BF16)
        for g in range(GROUPS):
            gw = D_SSD // GROUPS
            st = lax.dot_general(bm[:, g * STATE:(g + 1) * STATE], xw[:, g * gw:(g + 1) * gw],
                                 (((0,), (0,)), ((), ())), preferred_element_type=F32)
            h0_ref[0, d, :, g * gw:(g + 1) * gw] = st


def _ctx_states(ctx, mod3, g1, wxb, wdt, cw, cb, dtb, alog, e64):
    bsz = ctx.shape[0]
    mod_row = bsz
    const = lambda *shape: pl.BlockSpec(shape, lambda b: (0,) * len(shape))
    return pl.pallas_call(
        _ctx_kernel,
        out_shape=jax.ShapeDtypeStruct((bsz, 2, STATE, D_SSD), F32),
        grid=(bsz,),
        in_specs=[pl.BlockSpec((1, CTX_LEN, D_MODEL), lambda b: (b, 0, 0)),
                  pl.BlockSpec((1, N_MOD, D_MODEL), lambda b: (mod_row, 0, 0)),
                  const(1, D_MODEL), const(D_MODEL, XB_W), const(D_MODEL, LANES),
                  const(3, XB_W), const(1, XB_W), const(1, LANES), const(1, LANES),
                  const(2, LANES, D_SSD)],
        out_specs=pl.BlockSpec((1, 2, STATE, D_SSD), lambda b: (b, 0, 0, 0)),
        compiler_params=pltpu.CompilerParams(dimension_semantics=("arbitrary",),
                                             vmem_limit_bytes=VMEM_LIMIT),
        name="ctx_states",
    )(ctx, mod3, g1, wxb, wdt, cw, cb, dtb, alog, e64)


def _inproj_kernel(x_ref, mod_ref, g1_ref, wz_ref, wxbc_ref, wdt_ref, wb_ref, wc_ref, wu_ref,
                   z_ref, xbc_ref, dt_ref, scb_ref, v_ref):
    m = mod_ref[0]
    hx = _rms(x_ref[...]) * g1_ref[...] * (1.0 + m[1:2]) + m[0:1]
    hb = hx.astype(BF16)
    z_ref[...] = _dot(hb, wz_ref[...]).astype(BF16)
    xbc_ref[...] = _dot(hb, wxbc_ref[...]).astype(BF16)
    dt_ref[...] = _dot(hb, wdt_ref[...])
    scb_ref[...] = _dot(hb, wb_ref[...]).astype(BF16)
    v_ref[...] = (_dot(hb, wc_ref[...]) * _dot(hb, wu_ref[...])).astype(BF16)


def _inproj(x2, mod3, g1, wz, wxbc, wdt, wb, wc, wu):
    t = x2.shape[0]
    tm = TOK_TILE
    per_b = SEQ // tm
    const = lambda *shape: pl.BlockSpec(shape, lambda i: (0,) * len(shape))
    tile = lambda w: pl.BlockSpec((tm, w), lambda i: (i, 0))
    return pl.pallas_call(
        _inproj_kernel,
        out_shape=(jax.ShapeDtypeStruct((t, D_SSD), BF16), jax.ShapeDtypeStruct((t, XBC_W), BF16),
                   jax.ShapeDtypeStruct((t, LANES), F32), jax.ShapeDtypeStruct((t, D_SC), BF16),
                   jax.ShapeDtypeStruct((t, D_SC), BF16)),
        grid=(t // tm,),
        in_specs=[tile(D_MODEL),
                  pl.BlockSpec((1, N_MOD, D_MODEL), lambda i: (i // per_b, 0, 0)),
                  const(1, D_MODEL), const(D_MODEL, D_SSD), const(D_MODEL, XBC_W), const(D_MODEL, LANES),
                  const(D_MODEL, D_SC), const(D_MODEL, D_SC), const(D_MODEL, D_SC)],
        out_specs=(tile(D_SSD), tile(XBC_W), tile(LANES), tile(D_SC), tile(D_SC)),
        compiler_params=pltpu.CompilerParams(dimension_semantics=("arbitrary",),
                                             vmem_limit_bytes=VMEM_LIMIT),
        name="inproj",
    )(x2, mod3, g1, wz, wxbc, wdt, wb, wc, wu)


def _ssd_kernel(xbc_ref, z_ref, dt_ref, h0_ref, cw_ref, cb_ref, dtb_ref, alog_ref, dsk_ref, g_ref,
                e64_ref, e128_ref, o_ref, xc_ref, y_ref, s_ref):
    Q = CHUNK
    nck = SEQ // Q
    gw = D_SSD // GROUPS

    rowi = lax.broadcasted_iota(jnp.int32, (Q, XBC_W), 0)

    def conv_body(c, carry):
        r0 = pl.multiple_of(c * Q, Q)
        main = xbc_ref[0, pl.ds(r0, Q), :].astype(F32)
        pstart = pl.multiple_of(jnp.maximum(r0 - 16, 0), 16)
        nstart = pl.multiple_of(jnp.minimum(r0 + Q, SEQ - 16), 16)
        prev = xbc_ref[0, pl.ds(pstart, 16), :].astype(F32)[15:16]
        nxt = xbc_ref[0, pl.ds(nstart, 16), :].astype(F32)[0:1]
        prev = jnp.where(c > 0, prev, 0.0)
        nxt = jnp.where(c < nck - 1, nxt, 0.0)
        dn = jnp.where(rowi == 0, prev, pltpu.roll(main, 1, 0))
        up = jnp.where(rowi == Q - 1, nxt, pltpu.roll(main, Q - 1, 0))
        cw = cw_ref[...]
        conv = cw[0:1] * dn + cw[1:2] * main + cw[2:3] * up + cb_ref[...]
        xc_ref[pl.ds(r0, Q), :] = _silu(conv).astype(BF16)
        return carry

    lax.fori_loop(0, nck, conv_body, 0)

    ri = lax.broadcasted_iota(jnp.int32, (Q, Q), 0)
    ci = lax.broadcasted_iota(jnp.int32, (Q, Q), 1)
    lane = lax.broadcasted_iota(jnp.int32, (Q, LANES), 1)
    a_neg = -jnp.exp(alog_ref[...])

    def chunk(c, d):
        r0 = pl.multiple_of(c * Q, Q)
        rows = pl.ds(r0, Q)
        xs = xc_ref[rows, 0:D_SSD].astype(F32)
        bm = xc_ref[rows, D_SSD:D_SSD + GROUPS * STATE]
        cm = xc_ref[rows, D_SSD + GROUPS * STATE:XBC_W]
        dt = _softplus(dt_ref[0, rows, :] + dtb_ref[...])
        da = dt * a_neg
        tri = (ci <= ri) if d == 0 else (ci >= ri)
        cum = jnp.dot(tri.astype(F32), da, precision=HIGHEST, preferred_element_type=F32)
        cum_t = cum.T
        last = cum[Q - 1:Q] if d == 0 else cum[0:1]
        ecum_e = _expand(jnp.exp(cum), e64_ref[d], 2)
        dt_e = _expand(dt, e64_ref[d], 2)
        w_e = _expand(jnp.exp(last - cum) * dt, e64_ref[d], 2)
        colb = _expand(cum, e128_ref[d], 3)
        decay_e = ecum_e[Q - 1:Q] if d == 0 else ecum_e[0:1]

        gmat = [lax.dot_general(cm[:, g * STATE:(g + 1) * STATE], bm[:, g * STATE:(g + 1) * STATE],
                                (((1,), (1,)), ((), ())), preferred_element_type=F32)
                for g in range(GROUPS)]
        xdt = xs * dt_e
        y_parts = []
        for p in range(HEADS // 2):
            g = (2 * p) // (HEADS // GROUPS)
            ms = []
            for hh in (2 * p, 2 * p + 1):
                seg = colb[:, hh * LANES:(hh + 1) * LANES] - cum_t[HEADS * d + hh:HEADS * d + hh + 1, :]
                mm = jnp.where(tri, jnp.exp(jnp.where(tri, seg, 0.0)), 0.0) * gmat[g]
                ms.append(mm.astype(BF16))
            mcat = jnp.concatenate(ms, axis=1)
            xp = xdt[:, p * LANES:(p + 1) * LANES]
            rhs = jnp.concatenate([jnp.where(lane < HEAD_DIM, xp, 0.0).astype(BF16),
                                   jnp.where(lane >= HEAD_DIM, xp, 0.0).astype(BF16)], axis=0)
            y_parts.append(_dot(mcat, rhs))
        y_diag = jnp.concatenate(y_parts, axis=1)

        s_old = s_ref[...]
        s_bf = s_old.astype(BF16)
        y_off = jnp.concatenate(
            [_dot(cm[:, g * STATE:(g + 1) * STATE], s_bf[:, g * gw:(g + 1) * gw]) for g in range(GROUPS)],
            axis=1)
        y = y_diag + y_off * ecum_e

        xw = (xs * w_e).astype(BF16)
        upd = jnp.concatenate(
            [lax.dot_general(bm[:, g * STATE:(g + 1) * STATE], xw[:, g * gw:(g + 1) * gw],
                             (((0,), (0,)), ((), ())), preferred_element_type=F32) for g in range(GROUPS)],
            axis=1)
        s_ref[...] = s_old * decay_e + upd

        if d == 0:
            y_ref[rows, :] = y + dsk_ref[...] * xs
        else:
            tot = y_ref[rows, :] + y
            zz = z_ref[0, rows, :].astype(F32)
            gz = tot * _silu(zz)
            outs = []
            for g in range(GROUPS):
                gg = gz[:, g * gw:(g + 1) * gw]
                outs.append(gg * lax.rsqrt(jnp.mean(gg * gg, axis=-1, keepdims=True) + NORM_EPS))
            o_ref[0, rows, :] = (jnp.concatenate(outs, axis=1) * g_ref[...]).astype(BF16)

    s_ref[...] = h0_ref[0, 0]

    def fwd_body(i, carry):
        chunk(i, 0)
        return carry

    lax.fori_loop(0, nck, fwd_body, 0)

    s_ref[...] = h0_ref[0, 1]

    def bwd_body(i, carry):
        chunk(nck - 1 - i, 1)
        return carry

    lax.fori_loop(0, nck, bwd_body, 0)


def _ssd(xbc3, z3, dt3, h0, cw, cb, dtb, alog, dsk, g, e64, e128):
    bsz = xbc3.shape[0]
    const = lambda *shape: pl.BlockSpec(shape, lambda b: (0,) * len(shape))
    seq = lambda w: pl.BlockSpec((1, SEQ, w), lambda b: (b, 0, 0))
    return pl.pallas_call(
        _ssd_kernel,
        out_shape=jax.ShapeDtypeStruct((bsz, SEQ, D_SSD), BF16),
        grid=(bsz,),
        in_specs=[seq(XBC_W), seq(D_SSD), seq(LANES),
                  pl.BlockSpec((1, 2, STATE, D_SSD), lambda b: (b, 0, 0, 0)),
                  const(3, XBC_W), const(1, XBC_W), const(1, LANES), const(1, LANES),
                  const(1, D_SSD), const(1, D_SSD), const(2, LANES, D_SSD), const(2, LANES, HEADS * LANES)],
        out_specs=seq(D_SSD),
        scratch_shapes=[pltpu.VMEM((SEQ, XBC_W), BF16), pltpu.VMEM((SEQ, D_SSD), F32),
                        pltpu.VMEM((STATE, D_SSD), F32)],
        compiler_params=pltpu.CompilerParams(dimension_semantics=("arbitrary",),
                                             vmem_limit_bytes=VMEM_LIMIT),
        name="ssd",
    )(xbc3, z3, dt3, h0, cw, cb, dtb, alog, dsk, g, e64, e128)


def _outproj_kernel(x_ref, yssd_ref, scb_ref, v_ref, vp_ref, vn_ref, mod_ref, scw_ref, wo1_ref, wo2_ref,
                    g2_ref, wrt_ref, br_ref, x1_ref, lg_ref):
    tm = TOK_TILE
    per_b = SEQ // tm
    i = pl.program_id(0)
    first = (i % per_b) == 0
    last = (i % per_b) == per_b - 1
    m = mod_ref[0]
    v = v_ref[...].astype(F32)
    vp = jnp.where(first, 0.0, vp_ref[...].astype(F32))
    vn = jnp.where(last, 0.0, vn_ref[...].astype(F32))
    dn = jnp.concatenate([vp, v[:tm - GRID_W]], axis=0)
    up = jnp.concatenate([v[GRID_W:], vn], axis=0)
    scw = scw_ref[...]
    ysc = scb_ref[...].astype(F32) * (scw[0:1] * dn + scw[1:2] * v + scw[2:3] * up)
    out = _dot(yssd_ref[...], wo1_ref[...]) + _dot(ysc.astype(BF16), wo2_ref[...])
    x1 = x_ref[...] + m[2:3] * out
    x1_ref[...] = x1
    h2 = _rms(x1) * g2_ref[...] * (1.0 + m[4:5]) + m[3:4]
    lg_ref[...] = lax.dot_general(wrt_ref[...], h2, (((1,), (1,)), ((), ())), precision=HIGHEST,
                                  preferred_element_type=F32) + br_ref[...]


def _outproj(x2, yssd, scb, v, mod3, scw, wo1, wo2, g2, wrt, br):
    t = x2.shape[0]
    tm = TOK_TILE
    per_b = SEQ // tm
    r = tm // GRID_W
    nrow = t // GRID_W
    const = lambda *shape: pl.BlockSpec(shape, lambda i: (0,) * len(shape))
    tile = lambda w: pl.BlockSpec((tm, w), lambda i: (i, 0))
    return pl.pallas_call(
        _outproj_kernel,
        out_shape=(jax.ShapeDtypeStruct((t, D_MODEL), F32), jax.ShapeDtypeStruct((N_EXPERTS, t), F32)),
        grid=(t // tm,),
        in_specs=[tile(D_MODEL), tile(D_SSD), tile(D_SC), tile(D_SC),
                  pl.BlockSpec((GRID_W, D_SC), lambda i: (jnp.maximum(i * r - 1, 0), 0)),
                  pl.BlockSpec((GRID_W, D_SC), lambda i: (jnp.minimum((i + 1) * r, nrow - 1), 0)),
                  pl.BlockSpec((1, N_MOD, D_MODEL), lambda i: (i // per_b, 0, 0)),
                  const(3, D_SC), const(D_SSD, D_MODEL), const(D_SC, D_MODEL), const(1, D_MODEL),
                  const(N_EXPERTS, D_MODEL), const(N_EXPERTS, 1)],
        out_specs=(tile(D_MODEL), pl.BlockSpec((N_EXPERTS, tm), lambda i: (0, i))),
        compiler_params=pltpu.CompilerParams(dimension_semantics=("arbitrary",),
                                             vmem_limit_bytes=VMEM_LIMIT),
        name="outproj",
    )(x2, yssd, scb, v, v, v, mod3, scw, wo1, wo2, g2, wrt, br)


def _route_kernel(lg_ref, dest_ref, gate_ref, meta_ref, idx_ref, rank_ref, carry_ref, *, n_tok, n_blocks):
    tt = RT_TILE
    ne = N_EXPERTS
    eio = lax.broadcasted_iota(jnp.int32, (ne, tt), 0)
    si = lax.broadcasted_iota(jnp.int32, (tt, tt), 0)
    ti = lax.broadcasted_iota(jnp.int32, (tt, tt), 1)
    before = (si < ti).astype(BF16)
    carry_ref[...] = jnp.zeros_like(carry_ref)

    def tile_body(j, c):
        t0 = pl.multiple_of(j * tt, tt)
        l = lg_ref[:, pl.ds(t0, tt)]
        onehot = jnp.zeros((ne, tt), F32)
        tops, sels = [], []
        for _ in range(TOP_K):
            mx = jnp.max(l, axis=0, keepdims=True)
            idx = jnp.min(jnp.where(l == mx, eio, ne), axis=0, keepdims=True)
            sel = eio == idx
            l = jnp.where(sel, -jnp.inf, l)
            onehot = onehot + sel.astype(F32)
            tops.append(mx)
            sels.append(sel)
            idx_ref[pl.ds(len(tops) - 1, 1), pl.ds(t0, tt)] = idx
        ex = [jnp.exp(tv - tops[0]) for tv in tops]
        den = ex[0] + ex[1] + ex[2] + ex[3]
        prefix = _dot(onehot.astype(BF16), before) + carry_ref[:, 0:1]
        for k in range(TOP_K):
            gate_ref[pl.ds(k, 1), pl.ds(t0, tt)] = ex[k] / den
            rk = jnp.sum(jnp.where(sels[k], prefix, 0.0), axis=0, keepdims=True)
            rank_ref[pl.ds(k, 1), pl.ds(t0, tt)] = rk.astype(jnp.int32)
        carry_ref[...] = carry_ref[...] + jnp.sum(onehot, axis=1, keepdims=True)
        return c

    lax.fori_loop(0, n_tok // tt, tile_body, 0)

    counts = carry_ref[...]
    padded = jnp.floor((counts + (MOE_BM - 1)) * (1.0 / MOE_BM)) * MOE_BM
    er = lax.broadcasted_iota(jnp.int32, (ne, ne), 0)
    ec = lax.broadcasted_iota(jnp.int32, (ne, ne), 1)
    pad_start = jnp.dot((ec < er).astype(F32), padded, precision=HIGHEST, preferred_element_type=F32)
    pad_end = pad_start + padded

    def dest_body(j, c):
        t0 = pl.multiple_of(j * tt, tt)
        for k in range(TOP_K):
            idx = idx_ref[pl.ds(k, 1), pl.ds(t0, tt)]
            base = jnp.sum(jnp.where(eio == idx, pad_start[:, 0:1], 0.0), axis=0, keepdims=True)
            dest_ref[pl.ds(k, 1), pl.ds(t0, tt)] = base.astype(jnp.int32) + rank_ref[pl.ds(k, 1), pl.ds(t0, tt)]
        return c

    lax.fori_loop(0, n_tok // tt, dest_body, 0)

    width = meta_ref.shape[1]
    sub = lax.broadcasted_iota(jnp.int32, (ne, width), 0)
    lan = lax.broadcasted_iota(jnp.int32, (ne, width), 1)
    diag = sub == lan
    cnt_row = jnp.sum(jnp.where(diag, counts[:, 0:1], 0.0), axis=0, keepdims=True)
    start_row = jnp.sum(jnp.where(diag, pad_start[:, 0:1], 0.0), axis=0, keepdims=True)
    blk_start = (lan * MOE_BM).astype(F32)
    blk_exp = jnp.sum((pad_end[:, 0:1] <= blk_start).astype(F32), axis=0, keepdims=True)
    blk_exp = jnp.minimum(blk_exp, float(ne - 1))
    used = jnp.sum(padded[:, 0:1], axis=0, keepdims=True) * (1.0 / MOE_BM)
    meta_ref[0:1, :] = cnt_row.astype(jnp.int32)
    meta_ref[1:2, :] = start_row.astype(jnp.int32)
    meta_ref[2:3, :] = blk_exp.astype(jnp.int32)
    meta_ref[3:4, :] = jnp.broadcast_to(used, (1, width)).astype(jnp.int32)
    meta_ref[4:8, :] = jnp.zeros((4, width), jnp.int32)


def _route(lgt, n_blocks):
    ne, n_tok = lgt.shape
    width = -(-n_blocks // LANES) * LANES
    full = lambda *shape: pl.BlockSpec(shape, lambda: (0,) * len(shape))
    return pl.pallas_call(
        functools.partial(_route_kernel, n_tok=n_tok, n_blocks=n_blocks),
        out_shape=(jax.ShapeDtypeStruct((TOP_K, n_tok), jnp.int32),
                   jax.ShapeDtypeStruct((TOP_K, n_tok), F32),
                   jax.ShapeDtypeStruct((8, width), jnp.int32)),
        in_specs=[full(ne, n_tok)],
        out_specs=(full(TOP_K, n_tok), full(TOP_K, n_tok), full(8, width)),
        scratch_shapes=[pltpu.VMEM((TOP_K, n_tok), jnp.int32), pltpu.VMEM((TOP_K, n_tok), jnp.int32),
                        pltpu.VMEM((ne, LANES), F32)],
        compiler_params=pltpu.CompilerParams(vmem_limit_bytes=VMEM_LIMIT),
        name="route",
    )(lgt)


def _dispatch_kernel(dest_ref, cnt_ref, start_ref, nu_ref, x1_ref, mod_ref, g2_ref, xs_ref,
                     hbuf, zbuf, sem, zsem):
    i = pl.program_id(0)
    n = pl.num_programs(0)
    tl = DISP_TILE
    slot = i % 2
    nb = xs_ref.shape[0] // MOE_BM

    def zero_block(b):
        return pltpu.make_async_copy(zbuf, xs_ref.at[pl.ds(b * MOE_BM, MOE_BM)], zsem)

    @pl.when(i == 0)
    def _():
        zbuf[...] = jnp.zeros_like(zbuf)

        def start_e(e, c):
            @pl.when(cnt_ref[e] > 0)
            def _():
                zero_block((start_ref[e] + cnt_ref[e] - 1) // MOE_BM).start()
            return c

        def wait_e(e, c):
            @pl.when(cnt_ref[e] > 0)
            def _():
                zero_block(0).wait()
            return c

        def start_t(b, c):
            zero_block(b).start()
            return c

        def wait_t(b, c):
            zero_block(0).wait()
            return c

        lax.fori_loop(0, N_EXPERTS, start_e, 0)
        lax.fori_loop(nu_ref[0], nb, start_t, 0)
        lax.fori_loop(0, N_EXPERTS, wait_e, 0)
        lax.fori_loop(nu_ref[0], nb, wait_t, 0)

    m = mod_ref[0]
    hbuf[slot] = _rms(x1_ref[...]) * g2_ref[...] * (1.0 + m[4:5]) + m[3:4]

    def per_tok(t, c):
        tok = i * tl + t
        for k in range(TOP_K):
            d = dest_ref[tok * TOP_K + k]
            pltpu.make_async_copy(hbuf.at[slot, pl.ds(t, 1)], xs_ref.at[pl.ds(d, 1)], sem.at[slot]).start()
        return c

    lax.fori_loop(0, tl, per_tok, 0)

    def wait_slot(sl):
        for _ in range(TOP_K):
            pltpu.make_async_copy(hbuf.at[sl], xs_ref.at[pl.ds(0, tl)], sem.at[sl]).wait()

    @pl.when(i > 0)
    def _():
        wait_slot(1 - slot)

    @pl.when(i == n - 1)
    def _():
        wait_slot(slot)


def _dispatch(dest_flat, cnt, start, n_used, x1, mod3, g2, n_rows):
    n_tok = x1.shape[0]
    tl = DISP_TILE
    per_b = SEQ // tl
    return pl.pallas_call(
        _dispatch_kernel,
        out_shape=jax.ShapeDtypeStruct((n_rows, D_MODEL), F32),
        grid_spec=pltpu.PrefetchScalarGridSpec(
            num_scalar_prefetch=4,
            grid=(n_tok // tl,),
            in_specs=[pl.BlockSpec((tl, D_MODEL), lambda i, *_: (i, 0)),
                      pl.BlockSpec((1, N_MOD, D_MODEL), lambda i, *_: (i // per_b, 0, 0)),
                      pl.BlockSpec((1, D_MODEL), lambda i, *_: (0, 0))],
            out_specs=pl.BlockSpec(memory_space=pl.ANY),
            scratch_shapes=[pltpu.VMEM((2, tl, D_MODEL), F32), pltpu.VMEM((MOE_BM, D_MODEL), F32),
                            pltpu.SemaphoreType.DMA((2,)), pltpu.SemaphoreType.DMA]),
        compiler_params=pltpu.CompilerParams(dimension_semantics=("arbitrary",),
                                             vmem_limit_bytes=VMEM_LIMIT),
        name="dispatch",
    )(dest_flat, cnt, start, n_used, x1, mod3, g2)


def _expert_kernel(be_ref, nu_ref, xs_ref, wgu_ref, bgu_ref, wd_ref, bd_ref, y_ref):
    j = pl.program_id(0)

    @pl.when(j < nu_ref[0])
    def _():
        xb = xs_ref[...].astype(BF16)
        gu = _dot(xb, wgu_ref[0]) + bgu_ref[0]
        glu = jnp.minimum(gu[:, :D_FF], SWIGLU_LIMIT)
        lin = jnp.clip(gu[:, D_FF:], -SWIGLU_LIMIT, SWIGLU_LIMIT)
        act = glu * jax.nn.sigmoid(SWIGLU_ALPHA * glu) * (lin + 1.0)
        y_ref[...] = _dot(act.astype(BF16), wd_ref[0]) + bd_ref[0]

    @pl.when(j >= nu_ref[0])
    def _():
        y_ref[...] = jnp.zeros_like(y_ref)


def _experts(blk_exp, n_used, xs, wgu, bgu, wd, bd):
    n_rows = xs.shape[0]
    nb = n_rows // MOE_BM
    row_blk = lambda j, be, nu: (jnp.minimum(j, nu[0] - 1), 0)
    out_blk = lambda j, be, nu: (j, 0)
    per_e = lambda j, be, nu: (be[j], 0, 0)
    return pl.pallas_call(
        _expert_kernel,
        out_shape=jax.ShapeDtypeStruct((n_rows, D_MODEL), F32),
        grid_spec=pltpu.PrefetchScalarGridSpec(
            num_scalar_prefetch=2,
            grid=(nb,),
            in_specs=[pl.BlockSpec((MOE_BM, D_MODEL), row_blk),
                      pl.BlockSpec((1, D_MODEL, 2 * D_FF), per_e),
                      pl.BlockSpec((1, 1, 2 * D_FF), per_e),
                      pl.BlockSpec((1, D_FF, D_MODEL), per_e),
                      pl.BlockSpec((1, 1, D_MODEL), per_e)],
            out_specs=pl.BlockSpec((MOE_BM, D_MODEL), out_blk)),
        compiler_params=pltpu.CompilerParams(dimension_semantics=("arbitrary",),
                                             vmem_limit_bytes=VMEM_LIMIT),
        name="experts",
    )(blk_exp, n_used, xs, wgu, bgu, wd, bd)


def _combine_kernel(dest_ref, ys_ref, x1_ref, gate_ref, mod_ref, fg_ref, o_ref, buf_ref, sem):
    i = pl.program_id(0)
    n = pl.num_programs(0)
    tc = COMB_TILE
    slot = i % 2

    def issue(tile, sl):
        def per_tok(t, c):
            tok = tile * tc + t
            for k in range(TOP_K):
                d = dest_ref[tok * TOP_K + k]
                pltpu.make_async_copy(ys_ref.at[pl.ds(d, 1)], buf_ref.at[sl, pl.ds(k * tc + t, 1)],
                                      sem.at[sl]).start()
            return c

        lax.fori_loop(0, tc, per_tok, 0)

    @pl.when(i == 0)
    def _():
        issue(0, 0)

    @pl.when(i + 1 < n)
    def _():
        issue(i + 1, 1 - slot)

    pltpu.make_async_copy(ys_ref.at[pl.ds(0, tc * TOP_K)], buf_ref.at[slot], sem.at[slot]).wait()

    gate = gate_ref[...]
    acc = None
    for k in range(TOP_K):
        term = gate[:, k:k + 1] * buf_ref[slot, pl.ds(k * tc, tc), :]
        acc = term if acc is None else acc + term
    m = mod_ref[0]
    x2 = x1_ref[...] + m[5:6] * acc
    o_ref[...] = _rms(x2) * fg_ref[...]


def _combine(dest_flat, ys, x1, gates, mod3, fg):
    n_tok = x1.shape[0]
    tc = COMB_TILE
    per_b = SEQ // tc
    return pl.pallas_call(
        _combine_kernel,
        out_shape=jax.ShapeDtypeStruct((n_tok, D_MODEL), F32),
        grid_spec=pltpu.PrefetchScalarGridSpec(
            num_scalar_prefetch=1,
            grid=(n_tok // tc,),
            in_specs=[pl.BlockSpec(memory_space=pl.ANY),
                      pl.BlockSpec((tc, D_MODEL), lambda i, d: (i, 0)),
                      pl.BlockSpec((tc, TOP_K), lambda i, d: (i, 0)),
                      pl.BlockSpec((1, N_MOD, D_MODEL), lambda i, d: (i // per_b, 0, 0)),
                      pl.BlockSpec((1, D_MODEL), lambda i, d: (0, 0))],
            out_specs=pl.BlockSpec((tc, D_MODEL), lambda i, d: (i, 0)),
            scratch_shapes=[pltpu.VMEM((2, tc * TOP_K, D_MODEL), F32), pltpu.SemaphoreType.DMA((2,))]),
        compiler_params=pltpu.CompilerParams(dimension_semantics=("arbitrary",),
                                             vmem_limit_bytes=VMEM_LIMIT),
        name="combine",
    )(dest_flat, ys, x1, gates, mod3, fg)


def _expansion_matrices():
    r = jnp.arange(LANES)[:, None]
    out64, out128 = [], []
    for d in range(2):
        l64 = jnp.arange(D_SSD)[None, :]
        l128 = jnp.arange(HEADS * LANES)[None, :]
        out64.append((l64 // HEAD_DIM == r - HEADS * d).astype(BF16))
        out128.append((l128 // LANES == r - HEADS * d).astype(BF16))
    return jnp.stack(out64), jnp.stack(out128)


def _pad_lanes(v):
    return jnp.pad(v, [(0, 0)] * (v.ndim - 1) + [(0, LANES - v.shape[-1])])


def kernel(x, c, ctx, c_ctx, w_mod, b_mod, norm1_g, w_in, ssd_conv_w, ssd_conv_b, ssd_dt_bias, ssd_a_log,
           ssd_d, ssd_norm_g, sc_conv_w, w_out, norm2_g, w_router, b_router, w_gate_up, b_gate_up, w_down,
           b_down, final_g):
    bsz = x.shape[0]
    n_tok = bsz * SEQ
    n_assign = n_tok * TOP_K
    n_blocks = n_assign // MOE_BM + N_EXPERTS
    n_rows = n_blocks * MOE_BM
    li = 0

    cvec = jnp.concatenate([c, c_ctx[None, :], jnp.zeros((7, D_MODEL), F32)], axis=0)
    mod3 = _mod(cvec, w_mod[li], b_mod[li][None, :]).reshape(bsz + 8, N_MOD, D_MODEL)

    w = w_in[li]
    wz = w[:, Z0:X0].astype(BF16)
    wxbc = w[:, X0:DT0].astype(BF16)
    wdt = _pad_lanes(w[:, DT0:SC0]).astype(BF16)
    wb = w[:, SC0:SC0 + D_SC].astype(BF16)
    wc = w[:, SC0 + D_SC:SC0 + 2 * D_SC].astype(BF16)
    wu = w[:, SC0 + 2 * D_SC:].astype(BF16)
    g1 = norm1_g[li][None, :]
    cw = ssd_conv_w[li]
    cb = ssd_conv_b[li][None, :]
    dtb = _pad_lanes(ssd_dt_bias[li].reshape(1, 2 * HEADS))
    alog = _pad_lanes(ssd_a_log[li].reshape(1, 2 * HEADS))
    e64, e128 = _expansion_matrices()

    h0 = _ctx_states(ctx, mod3, g1, wxbc[:, :XB_W], wdt, cw[:, :XB_W], cb[:, :XB_W], dtb, alog, e64)

    x2 = x.reshape(n_tok, D_MODEL)
    z, xbc, dtr, scb, v = _inproj(x2, mod3, g1, wz, wxbc, wdt, wb, wc, wu)

    dsk = jnp.repeat(ssd_d[li], HEAD_DIM)[None, :]
    yssd = _ssd(xbc.reshape(bsz, SEQ, XBC_W), z.reshape(bsz, SEQ, D_SSD), dtr.reshape(bsz, SEQ, LANES), h0,
                cw, cb, dtb, alog, dsk, ssd_norm_g[li][None, :], e64, e128)

    wo = w_out[li].astype(BF16)
    g2 = norm2_g[li][None, :]
    x1, lgt = _outproj(x2, yssd.reshape(n_tok, D_SSD), scb, v, mod3, sc_conv_w[li], wo[:D_SSD], wo[D_SSD:],
                       g2, w_router[li].T, b_router[li][:, None])

    dest_t, gate_t, meta = _route(lgt, n_blocks)
    dest_flat = dest_t.T.reshape(n_assign)
    gates = gate_t.T
    cnt = meta[0, :N_EXPERTS]
    start = meta[1, :N_EXPERTS]
    blk_exp = meta[2, :n_blocks]
    n_used = meta[3, :1]

    xs = _dispatch(dest_flat, cnt, start, n_used, x1, mod3, g2, n_rows)
    ys = _experts(blk_exp, n_used, xs, w_gate_up[li].astype(BF16), b_gate_up[li][:, None, :],
                  w_down[li].astype(BF16), b_down[li][:, None, :])
    out = _combine(dest_flat, ys, x1, gates, mod3, final_g[None, :])
    return out.reshape(bsz, SEQ, D_MODEL)
```

```python
import functools

import jax
import jax.numpy as jnp
from jax import lax
from jax.experimental import pallas as pl
from jax.experimental.pallas import tpu as pltpu

F32 = jnp.float32
BF16 = jnp.bfloat16
HIGHEST = lax.Precision.HIGHEST

D_MODEL = 1024
SEQ = 2048
CTX_LEN = 256
GRID_W = 64
D_SSD = 1024
D_SC = 1024
HEAD_DIM = 64
HEADS = 16
GROUPS = 2
STATE = 128
CHUNK = 128
N_EXPERTS = 32
TOP_K = 4
D_FF = 1024
SWIGLU_LIMIT = 7.0
SWIGLU_ALPHA = 1.702
NORM_EPS = 1e-6
N_MOD = 6
XBC_W = D_SSD + 2 * GROUPS * STATE
XB_W = D_SSD + GROUPS * STATE
LANES = 128

Z0 = 0
X0 = Z0 + D_SSD
B0 = X0 + D_SSD
C0 = B0 + GROUPS * STATE
DT0 = C0 + GROUPS * STATE
SC0 = DT0 + 2 * HEADS

TOK_TILE = 512
MOE_BM = 256
RT_TILE = 512
DISP_TILE = 256
COMB_TILE = 512
ROW_W = D_MODEL + LANES
META_IDX = 0
META_GATE = TOP_K
META_TOK = 2 * TOP_K
VMEM_LIMIT = 56 * 1024 * 1024


def _silu(v):
    return v * jax.nn.sigmoid(v)


def _softplus(v):
    return jnp.maximum(v, 0.0) + jnp.log1p(jnp.exp(-jnp.abs(v)))


def _rms(v):
    return v * lax.rsqrt(jnp.mean(v * v, axis=-1, keepdims=True) + NORM_EPS)


def _dot(a, b):
    return jnp.dot(a, b, preferred_element_type=F32)


def _expand(v, e, pieces):
    acc = None
    rem = v
    for _ in range(pieces):
        p = rem.astype(BF16)
        rem = rem - p.astype(F32)
        t = _dot(p, e)
        acc = t if acc is None else acc + t
    return acc


def _mod_kernel(c_ref, w_ref, b_ref, o_ref):
    o_ref[...] = jnp.dot(_silu(c_ref[...]), w_ref[...], precision=HIGHEST,
                         preferred_element_type=F32) + b_ref[...]


def _mod(cvec, w_mod, b_mod):
    rows = cvec.shape[0]
    n = w_mod.shape[1]
    tn = 1536
    return pl.pallas_call(
        _mod_kernel,
        out_shape=jax.ShapeDtypeStruct((rows, n), F32),
        grid=(n // tn,),
        in_specs=[pl.BlockSpec((rows, D_MODEL), lambda j: (0, 0)),
                  pl.BlockSpec((D_MODEL, tn), lambda j: (0, j)),
                  pl.BlockSpec((1, tn), lambda j: (0, j))],
        out_specs=pl.BlockSpec((rows, tn), lambda j: (0, j)),
        compiler_params=pltpu.CompilerParams(dimension_semantics=("arbitrary",),
                                             vmem_limit_bytes=VMEM_LIMIT),
        name="mod",
    )(cvec, w_mod, b_mod)


def _ctx_kernel(ctx_ref, mod_ref, g1_ref, wxb_ref, wdt_ref, cw_ref, cb_ref, dtb_ref, alog_ref, e64_ref,
                h0_ref):
    L = CTX_LEN
    m = mod_ref[0]
    hc = _rms(ctx_ref[0]) * g1_ref[...] * (1.0 + m[1:2]) + m[0:1]
    hb = hc.astype(BF16)
    pxb = _dot(hb, wxb_ref[...])
    dtr = _dot(hb, wdt_ref[...])
    rowi = lax.broadcasted_iota(jnp.int32, (L, XB_W), 0)
    dn = jnp.where(rowi == 0, 0.0, pltpu.roll(pxb, 1, 0))
    up = jnp.where(rowi == L - 1, 0.0, pltpu.roll(pxb, L - 1, 0))
    cw = cw_ref[...]
    xb = _silu(cw[0:1] * dn + cw[1:2] * pxb + cw[2:3] * up + cb_ref[...])
    xs = xb[:, :D_SSD]
    bm = xb[:, D_SSD:].astype(BF16)
    dt = _softplus(dtr + dtb_ref[...])
    da = dt * (-jnp.exp(alog_ref[...]))
    ri = lax.broadcasted_iota(jnp.int32, (L, L), 0)
    ci = lax.broadcasted_iota(jnp.int32, (L, L), 1)
    for d in range(2):
        tri = (ci <= ri) if d == 0 else (ci >= ri)
        cum = jnp.dot(tri.astype(F32), da, precision=HIGHEST, preferred_element_type=F32)
        last = cum[L - 1:L] if d == 0 else cum[0:1]
        w_e = _expand(jnp.exp(last - cum) * dt, e64_ref[d], 2)
        xw = (xs * w_e).astype(BF16)
        for g in range(GROUPS):
            gw = D_SSD // GROUPS
            st = lax.dot_general(bm[:, g * STATE:(g + 1) * STATE], xw[:, g * gw:(g + 1) * gw],
                                 (((0,), (0,)), ((), ())), preferred_element_type=F32)
            h0_ref[0, d, :, g * gw:(g + 1) * gw] = st


def _ctx_states(ctx, mod3, g1, wxb, wdt, cw, cb, dtb, alog, e64):
    bsz = ctx.shape[0]
    mod_row = bsz
    const = lambda *shape: pl.BlockSpec(shape, lambda b: (0,) * len(shape))
    return pl.pallas_call(
        _ctx_kernel,
        out_shape=jax.ShapeDtypeStruct((bsz, 2, STATE, D_SSD), F32),
        grid=(bsz,),
        in_specs=[pl.BlockSpec((1, CTX_LEN, D_MODEL), lambda b: (b, 0, 0)),
                  pl.BlockSpec((1, N_MOD, D_MODEL), lambda b: (mod_row, 0, 0)),
                  const(1, D_MODEL), const(D_MODEL, XB_W), const(D_MODEL, LANES),
                  const(3, XB_W), const(1, XB_W), const(1, LANES), const(1, LANES),
                  const(2, LANES, D_SSD)],
        out_specs=pl.BlockSpec((1, 2, STATE, D_SSD), lambda b: (b, 0, 0, 0)),
        compiler_params=pltpu.CompilerParams(dimension_semantics=("arbitrary",),
                                             vmem_limit_bytes=VMEM_LIMIT),
        name="ctx_states",
    )(ctx, mod3, g1, wxb, wdt, cw, cb, dtb, alog, e64)


def _inproj_kernel(x_ref, mod_ref, g1_ref, wz_ref, wxbc_ref, wdt_ref, wb_ref, wc_ref, wu_ref,
                   z_ref, xbc_ref, dt_ref, scb_ref, v_ref):
    m = mod_ref[0]
    hx = _rms(x_ref[...]) * g1_ref[...] * (1.0 + m[1:2]) + m[0:1]
    hb = hx.astype(BF16)
    z_ref[...] = _dot(hb, wz_ref[...]).astype(BF16)
    xbc_ref[...] = _dot(hb, wxbc_ref[...]).astype(BF16)
    dt_ref[...] = _dot(hb, wdt_ref[...])
    scb_ref[...] = _dot(hb, wb_ref[...]).astype(BF16)
    v_ref[...] = (_dot(hb, wc_ref[...]) * _dot(hb, wu_ref[...])).astype(BF16)


def _inproj(x2, mod3, g1, wz, wxbc, wdt, wb, wc, wu):
    t = x2.shape[0]
    tm = TOK_TILE
    per_b = SEQ // tm
    const = lambda *shape: pl.BlockSpec(shape, lambda i: (0,) * len(shape))
    tile = lambda w: pl.BlockSpec((tm, w), lambda i: (i, 0))
    return pl.pallas_call(
        _inproj_kernel,
        out_shape=(jax.ShapeDtypeStruct((t, D_SSD), BF16), jax.ShapeDtypeStruct((t, XBC_W), BF16),
                   jax.ShapeDtypeStruct((t, LANES), F32), jax.ShapeDtypeStruct((t, D_SC), BF16),
                   jax.ShapeDtypeStruct((t, D_SC), BF16)),
        grid=(t // tm,),
        in_specs=[tile(D_MODEL),
                  pl.BlockSpec((1, N_MOD, D_MODEL), lambda i: (i // per_b, 0, 0)),
                  const(1, D_MODEL), const(D_MODEL, D_SSD), const(D_MODEL, XBC_W), const(D_MODEL, LANES),
                  const(D_MODEL, D_SC), const(D_MODEL, D_SC), const(D_MODEL, D_SC)],
        out_specs=(tile(D_SSD), tile(XBC_W), tile(LANES), tile(D_SC), tile(D_SC)),
        compiler_params=pltpu.CompilerParams(dimension_semantics=("arbitrary",),
                                             vmem_limit_bytes=VMEM_LIMIT),
        name="inproj",
    )(x2, mod3, g1, wz, wxbc, wdt, wb, wc, wu)


def _ssd_kernel(xbc_ref, z_ref, dt_ref, h0_ref, cw_ref, cb_ref, dtb_ref, alog_ref, dsk_ref, g_ref,
                e64_ref, e128_ref, o_ref, xc_ref, y_ref, s_ref):
    Q = CHUNK
    nck = SEQ // Q
    gw = D_SSD // GROUPS

    rowi = lax.broadcasted_iota(jnp.int32, (Q, XBC_W), 0)

    def conv_body(c, carry):
        r0 = pl.multiple_of(c * Q, Q)
        main = xbc_ref[0, pl.ds(r0, Q), :].astype(F32)
        pstart = pl.multiple_of(jnp.maximum(r0 - 16, 0), 16)
        nstart = pl.multiple_of(jnp.minimum(r0 + Q, SEQ - 16), 16)
        prev = xbc_ref[0, pl.ds(pstart, 16), :].astype(F32)[15:16]
        nxt = xbc_ref[0, pl.ds(nstart, 16), :].astype(F32)[0:1]
        prev = jnp.where(c > 0, prev, 0.0)
        nxt = jnp.where(c < nck - 1, nxt, 0.0)
        dn = jnp.where(rowi == 0, prev, pltpu.roll(main, 1, 0))
        up = jnp.where(rowi == Q - 1, nxt, pltpu.roll(main, Q - 1, 0))
        cw = cw_ref[...]
        conv = cw[0:1] * dn + cw[1:2] * main + cw[2:3] * up + cb_ref[...]
        xc_ref[pl.ds(r0, Q), :] = _silu(conv).astype(BF16)
        return carry

    lax.fori_loop(0, nck, conv_body, 0)

    ri = lax.broadcasted_iota(jnp.int32, (Q, Q), 0)
    ci = lax.broadcasted_iota(jnp.int32, (Q, Q), 1)
    lane = lax.broadcasted_iota(jnp.int32, (Q, LANES), 1)
    a_neg = -jnp.exp(alog_ref[...])

    def chunk(c, d):
        r0 = pl.multiple_of(c * Q, Q)
        rows = pl.ds(r0, Q)
        xs = xc_ref[rows, 0:D_SSD].astype(F32)
        bm = xc_ref[rows, D_SSD:D_SSD + GROUPS * STATE]
        cm = xc_ref[rows, D_SSD + GROUPS * STATE:XBC_W]
        dt = _softplus(dt_ref[0, rows, :] + dtb_ref[...])
        da = dt * a_neg
        tri = (ci <= ri) if d == 0 else (ci >= ri)
        cum = jnp.dot(tri.astype(F32), da, precision=HIGHEST, preferred_element_type=F32)
        cum_t = cum.T
        last = cum[Q - 1:Q] if d == 0 else cum[0:1]
        ecum_e = _expand(jnp.exp(cum), e64_ref[d], 2)
        dt_e = _expand(dt, e64_ref[d], 2)
        w_e = _expand(jnp.exp(last - cum) * dt, e64_ref[d], 2)
        colb = _expand(cum, e128_ref[d], 3)
        decay_e = ecum_e[Q - 1:Q] if d == 0 else ecum_e[0:1]

        gmat = [lax.dot_general(cm[:, g * STATE:(g + 1) * STATE], bm[:, g * STATE:(g + 1) * STATE],
                                (((1,), (1,)), ((), ())), preferred_element_type=F32)
                for g in range(GROUPS)]
        xdt = xs * dt_e
        y_parts = []
        for p in range(HEADS // 2):
            g = (2 * p) // (HEADS // GROUPS)
            ms = []
            for hh in (2 * p, 2 * p + 1):
                seg = colb[:, hh * LANES:(hh + 1) * LANES] - cum_t[HEADS * d + hh:HEADS * d + hh + 1, :]
                mm = jnp.where(tri, jnp.exp(jnp.where(tri, seg, 0.0)), 0.0) * gmat[g]
                ms.append(mm.astype(BF16))
            mcat = jnp.concatenate(ms, axis=1)
            xp = xdt[:, p * LANES:(p + 1) * LANES]
            rhs = jnp.concatenate([jnp.where(lane < HEAD_DIM, xp, 0.0).astype(BF16),
                                   jnp.where(lane >= HEAD_DIM, xp, 0.0).astype(BF16)], axis=0)
            y_parts.append(_dot(mcat, rhs))
        y_diag = jnp.concatenate(y_parts, axis=1)

        s_old = s_ref[...]
        s_bf = s_old.astype(BF16)
        y_off = jnp.concatenate(
            [_dot(cm[:, g * STATE:(g + 1) * STATE], s_bf[:, g * gw:(g + 1) * gw]) for g in range(GROUPS)],
            axis=1)
        y = y_diag + y_off * ecum_e

        xw = (xs * w_e).astype(BF16)
        upd = jnp.concatenate(
            [lax.dot_general(bm[:, g * STATE:(g + 1) * STATE], xw[:, g * gw:(g + 1) * gw],
                             (((0,), (0,)), ((), ())), preferred_element_type=F32) for g in range(GROUPS)],
            axis=1)
        s_ref[...] = s_old * decay_e + upd

        if d == 0:
            y_ref[rows, :] = y + dsk_ref[...] * xs
        else:
            tot = y_ref[rows, :] + y
            zz = z_ref[0, rows, :].astype(F32)
            gz = tot * _silu(zz)
            outs = []
            for g in range(GROUPS):
                gg = gz[:, g * gw:(g + 1) * gw]
                outs.append(gg * lax.rsqrt(jnp.mean(gg * gg, axis=-1, keepdims=True) + NORM_EPS))
            o_ref[0, rows, :] = (jnp.concatenate(outs, axis=1) * g_ref[...]).astype(BF16)

    s_ref[...] = h0_ref[0, 0]

    def fwd_body(i, carry):
        chunk(i, 0)
        return carry

    lax.fori_loop(0, nck, fwd_body, 0)

    s_ref[...] = h0_ref[0, 1]

    def bwd_body(i, carry):
        chunk(nck - 1 - i, 1)
        return carry

    lax.fori_loop(0, nck, bwd_body, 0)


def _ssd(xbc3, z3, dt3, h0, cw, cb, dtb, alog, dsk, g, e64, e128):
    bsz = xbc3.shape[0]
    const = lambda *shape: pl.BlockSpec(shape, lambda b: (0,) * len(shape))
    seq = lambda w: pl.BlockSpec((1, SEQ, w), lambda b: (b, 0, 0))
    return pl.pallas_call(
        _ssd_kernel,
        out_shape=jax.ShapeDtypeStruct((bsz, SEQ, D_SSD), BF16),
        grid=(bsz,),
        in_specs=[seq(XBC_W), seq(D_SSD), seq(LANES),
                  pl.BlockSpec((1, 2, STATE, D_SSD), lambda b: (b, 0, 0, 0)),
                  const(3, XBC_W), const(1, XBC_W), const(1, LANES), const(1, LANES),
                  const(1, D_SSD), const(1, D_SSD), const(2, LANES, D_SSD), const(2, LANES, HEADS * LANES)],
        out_specs=seq(D_SSD),
        scratch_shapes=[pltpu.VMEM((SEQ, XBC_W), BF16), pltpu.VMEM((SEQ, D_SSD), F32),
                        pltpu.VMEM((STATE, D_SSD), F32)],
        compiler_params=pltpu.CompilerParams(dimension_semantics=("arbitrary",),
                                             vmem_limit_bytes=VMEM_LIMIT),
        name="ssd",
    )(xbc3, z3, dt3, h0, cw, cb, dtb, alog, dsk, g, e64, e128)


def _outproj_kernel(x_ref, yssd_ref, scb_ref, v_ref, vp_ref, vn_ref, mod_ref, scw_ref, wo1_ref, wo2_ref,
                    g2_ref, wrt_ref, br_ref, x1_ref, lg_ref):
    tm = TOK_TILE
    per_b = SEQ // tm
    i = pl.program_id(0)
    first = (i % per_b) == 0
    last = (i % per_b) == per_b - 1
    m = mod_ref[0]
    v = v_ref[...].astype(F32)
    vp = jnp.where(first, 0.0, vp_ref[...].astype(F32))
    vn = jnp.where(last, 0.0, vn_ref[...].astype(F32))
    dn = jnp.concatenate([vp, v[:tm - GRID_W]], axis=0)
    up = jnp.concatenate([v[GRID_W:], vn], axis=0)
    scw = scw_ref[...]
    ysc = scb_ref[...].astype(F32) * (scw[0:1] * dn + scw[1:2] * v + scw[2:3] * up)
    out = _dot(yssd_ref[...], wo1_ref[...]) + _dot(ysc.astype(BF16), wo2_ref[...])
    x1 = x_ref[...] + m[2:3] * out
    x1_ref[...] = x1
    h2 = _rms(x1) * g2_ref[...] * (1.0 + m[4:5]) + m[3:4]
    lg_ref[...] = lax.dot_general(wrt_ref[...], h2, (((1,), (1,)), ((), ())), precision=HIGHEST,
                                  preferred_element_type=F32) + br_ref[...]


def _outproj(x2, yssd, scb, v, mod3, scw, wo1, wo2, g2, wrt, br):
    t = x2.shape[0]
    tm = TOK_TILE
    per_b = SEQ // tm
    r = tm // GRID_W
    nrow = t // GRID_W
    const = lambda *shape: pl.BlockSpec(shape, lambda i: (0,) * len(shape))
    tile = lambda w: pl.BlockSpec((tm, w), lambda i: (i, 0))
    return pl.pallas_call(
        _outproj_kernel,
        out_shape=(jax.ShapeDtypeStruct((t, D_MODEL), F32), jax.ShapeDtypeStruct((N_EXPERTS, t), F32)),
        grid=(t // tm,),
        in_specs=[tile(D_MODEL), tile(D_SSD), tile(D_SC), tile(D_SC),
                  pl.BlockSpec((GRID_W, D_SC), lambda i: (jnp.maximum(i * r - 1, 0), 0)),
                  pl.BlockSpec((GRID_W, D_SC), lambda i: (jnp.minimum((i + 1) * r, nrow - 1), 0)),
                  pl.BlockSpec((1, N_MOD, D_MODEL), lambda i: (i // per_b, 0, 0)),
                  const(3, D_SC), const(D_SSD, D_MODEL), const(D_SC, D_MODEL), const(1, D_MODEL),
                  const(N_EXPERTS, D_MODEL), const(N_EXPERTS, 1)],
        out_specs=(tile(D_MODEL), pl.BlockSpec((N_EXPERTS, tm), lambda i: (0, i))),
        compiler_params=pltpu.CompilerParams(dimension_semantics=("arbitrary",),
                                             vmem_limit_bytes=VMEM_LIMIT),
        name="outproj",
    )(x2, yssd, scb, v, v, v, mod3, scw, wo1, wo2, g2, wrt, br)


def _route_kernel(lg_ref, dest_ref, gate_ref, idx_ref, meta_ref, rank_ref, carry_ref, *, n_tok, n_blocks):
    tt = RT_TILE
    ne = N_EXPERTS
    eio = lax.broadcasted_iota(jnp.int32, (ne, tt), 0)
    si = lax.broadcasted_iota(jnp.int32, (tt, tt), 0)
    ti = lax.broadcasted_iota(jnp.int32, (tt, tt), 1)
    before = (si < ti).astype(BF16)
    carry_ref[...] = jnp.zeros_like(carry_ref)

    def tile_body(j, c):
        t0 = pl.multiple_of(j * tt, tt)
        l = lg_ref[:, pl.ds(t0, tt)]
        onehot = jnp.zeros((ne, tt), F32)
        tops, sels = [], []
        for _ in range(TOP_K):
            mx = jnp.max(l, axis=0, keepdims=True)
            idx = jnp.min(jnp.where(l == mx, eio, ne), axis=0, keepdims=True)
            sel = eio == idx
            l = jnp.where(sel, -jnp.inf, l)
            onehot = onehot + sel.astype(F32)
            tops.append(mx)
            sels.append(sel)
            idx_ref[pl.ds(len(tops) - 1, 1), pl.ds(t0, tt)] = idx
        ex = [jnp.exp(tv - tops[0]) for tv in tops]
        den = ex[0] + ex[1] + ex[2] + ex[3]
        prefix = _dot(onehot.astype(BF16), before) + carry_ref[:, 0:1]
        for k in range(TOP_K):
            gate_ref[pl.ds(k, 1), pl.ds(t0, tt)] = ex[k] / den
            rk = jnp.sum(jnp.where(sels[k], prefix, 0.0), axis=0, keepdims=True)
            rank_ref[pl.ds(k, 1), pl.ds(t0, tt)] = rk.astype(jnp.int32)
        carry_ref[...] = carry_ref[...] + jnp.sum(onehot, axis=1, keepdims=True)
        return c

    lax.fori_loop(0, n_tok // tt, tile_body, 0)

    counts = carry_ref[...]
    padded = jnp.floor((counts + (MOE_BM - 1)) * (1.0 / MOE_BM)) * MOE_BM
    er = lax.broadcasted_iota(jnp.int32, (ne, ne), 0)
    ec = lax.broadcasted_iota(jnp.int32, (ne, ne), 1)
    pad_start = jnp.dot((ec < er).astype(F32), padded, precision=HIGHEST, preferred_element_type=F32)
    pad_end = pad_start + padded

    def dest_body(j, c):
        t0 = pl.multiple_of(j * tt, tt)
        for k in range(TOP_K):
            idx = idx_ref[pl.ds(k, 1), pl.ds(t0, tt)]
            base = jnp.sum(jnp.where(eio == idx, pad_start[:, 0:1], 0.0), axis=0, keepdims=True)
            dest_ref[pl.ds(k, 1), pl.ds(t0, tt)] = base.astype(jnp.int32) + rank_ref[pl.ds(k, 1), pl.ds(t0, tt)]
        return c

    lax.fori_loop(0, n_tok // tt, dest_body, 0)

    width = meta_ref.shape[1]
    sub = lax.broadcasted_iota(jnp.int32, (ne, width), 0)
    lan = lax.broadcasted_iota(jnp.int32, (ne, width), 1)
    diag = sub == lan
    cnt_row = jnp.sum(jnp.where(diag, counts[:, 0:1], 0.0), axis=0, keepdims=True)
    start_row = jnp.sum(jnp.where(diag, pad_start[:, 0:1], 0.0), axis=0, keepdims=True)
    blk_start = (lan * MOE_BM).astype(F32)
    blk_exp = jnp.sum((pad_end[:, 0:1] <= blk_start).astype(F32), axis=0, keepdims=True)
    blk_exp = jnp.minimum(blk_exp, float(ne - 1))
    used = jnp.sum(padded[:, 0:1], axis=0, keepdims=True) * (1.0 / MOE_BM)
    meta_ref[0:1, :] = cnt_row.astype(jnp.int32)
    meta_ref[1:2, :] = start_row.astype(jnp.int32)
    meta_ref[2:3, :] = blk_exp.astype(jnp.int32)
    meta_ref[3:4, :] = jnp.broadcast_to(used, (1, width)).astype(jnp.int32)
    meta_ref[4:8, :] = jnp.zeros((4, width), jnp.int32)


def _route(lgt, n_blocks):
    ne, n_tok = lgt.shape
    width = -(-n_blocks // LANES) * LANES
    full = lambda *shape: pl.BlockSpec(shape, lambda: (0,) * len(shape))
    return pl.pallas_call(
        functools.partial(_route_kernel, n_tok=n_tok, n_blocks=n_blocks),
        out_shape=(jax.ShapeDtypeStruct((TOP_K, n_tok), jnp.int32),
                   jax.ShapeDtypeStruct((TOP_K, n_tok), F32),
                   jax.ShapeDtypeStruct((TOP_K, n_tok), jnp.int32),
                   jax.ShapeDtypeStruct((8, width), jnp.int32)),
        in_specs=[full(ne, n_tok)],
        out_specs=(full(TOP_K, n_tok), full(TOP_K, n_tok), full(TOP_K, n_tok), full(8, width)),
        scratch_shapes=[pltpu.VMEM((TOP_K, n_tok), jnp.int32), pltpu.VMEM((ne, LANES), F32)],
        compiler_params=pltpu.CompilerParams(vmem_limit_bytes=VMEM_LIMIT),
        name="route",
    )(lgt)


def _dispatch_kernel(dest_ref, cnt_ref, start_ref, nu_ref, x1_ref, meta_ref, mod_ref, g2_ref, zsrc_ref, xs_ref,
                     hbuf, sem, zsem):
    i = pl.program_id(0)
    n = pl.num_programs(0)
    tl = DISP_TILE
    slot = i % 2
    nb = xs_ref.shape[0] // MOE_BM

    def zero_block(b):
        return pltpu.make_async_copy(zsrc_ref, xs_ref.at[pl.ds(b * MOE_BM, MOE_BM)], zsem)

    @pl.when(i == 0)
    def _():
        def start_e(e, c):
            @pl.when(cnt_ref[e] > 0)
            def _():
                zero_block((start_ref[e] + cnt_ref[e] - 1) // MOE_BM).start()
            return c

        def wait_e(e, c):
            @pl.when(cnt_ref[e] > 0)
            def _():
                zero_block(0).wait()
            return c

        def start_t(b, c):
            zero_block(b).start()
            return c

        def wait_t(b, c):
            zero_block(0).wait()
            return c

        lax.fori_loop(0, N_EXPERTS, start_e, 0)
        lax.fori_loop(nu_ref[0], nb, start_t, 0)
        lax.fori_loop(0, N_EXPERTS, wait_e, 0)
        lax.fori_loop(nu_ref[0], nb, wait_t, 0)

    m = mod_ref[0]
    hbuf[slot, :, 0:D_MODEL] = _rms(x1_ref[...]) * g2_ref[...] * (1.0 + m[4:5]) + m[3:4]
    hbuf[slot, :, D_MODEL:ROW_W] = meta_ref[...]

    def per_tok(t, c):
        tok = i * tl + t
        for k in range(TOP_K):
            d = dest_ref[tok * TOP_K + k]
            pltpu.make_async_copy(hbuf.at[slot, pl.ds(t, 1)], xs_ref.at[pl.ds(d, 1)], sem.at[slot]).start()
        return c

    lax.fori_loop(0, tl, per_tok, 0)

    def wait_slot(sl):
        for _ in range(TOP_K):
            pltpu.make_async_copy(hbuf.at[sl], xs_ref.at[pl.ds(0, tl)], sem.at[sl]).wait()

    @pl.when(i > 0)
    def _():
        wait_slot(1 - slot)

    @pl.when(i == n - 1)
    def _():
        wait_slot(slot)


def _dispatch(dest_flat, cnt, start, n_used, x1, meta_rows, mod3, g2, zsrc, n_rows):
    n_tok = x1.shape[0]
    tl = DISP_TILE
    per_b = SEQ // tl
    return pl.pallas_call(
        _dispatch_kernel,
        out_shape=jax.ShapeDtypeStruct((n_rows, ROW_W), F32),
        grid_spec=pltpu.PrefetchScalarGridSpec(
            num_scalar_prefetch=4,
            grid=(n_tok // tl,),
            in_specs=[pl.BlockSpec((tl, D_MODEL), lambda i, *_: (i, 0)),
                      pl.BlockSpec((tl, LANES), lambda i, *_: (i, 0)),
                      pl.BlockSpec((1, N_MOD, D_MODEL), lambda i, *_: (i // per_b, 0, 0)),
                      pl.BlockSpec((1, D_MODEL), lambda i, *_: (0, 0)),
                      pl.BlockSpec((MOE_BM, ROW_W), lambda i, *_: (0, 0))],
            out_specs=pl.BlockSpec(memory_space=pl.ANY),
            scratch_shapes=[pltpu.VMEM((2, tl, ROW_W), F32),
                            pltpu.SemaphoreType.DMA((2,)), pltpu.SemaphoreType.DMA]),
        compiler_params=pltpu.CompilerParams(dimension_semantics=("arbitrary",),
                                             vmem_limit_bytes=VMEM_LIMIT),
        name="dispatch",
    )(dest_flat, cnt, start, n_used, x1, meta_rows, mod3, g2, zsrc)


def _expert_kernel(be_ref, nu_ref, xs_ref, wgu_ref, bgu_ref, wd_ref, bd_ref, zero_ref, ytm_ref,
                   ybuf, idv, ids, sc_sem, id_sem, z_sem, *, n_tok):
    j = pl.program_id(0)
    nbk = pl.num_programs(0) - 1
    nu = nu_ref[0]
    slot = j % 2
    prev = 1 - slot

    @pl.when(j == 0)
    def _():
        cp = pltpu.make_async_copy(zero_ref, ytm_ref.at[pl.ds(TOP_K * n_tok, MOE_BM)], z_sem)
        cp.start()
        cp.wait()

    def compute():
        e_f = be_ref[jnp.minimum(j, nbk - 1)].astype(F32)
        xb = xs_ref[:, 0:D_MODEL].astype(BF16)
        meta = xs_ref[:, D_MODEL:ROW_W]
        gu = _dot(xb, wgu_ref[0]) + bgu_ref[0]
        glu = jnp.minimum(gu[:, :D_FF], SWIGLU_LIMIT)
        lin = jnp.clip(gu[:, D_FF:], -SWIGLU_LIMIT, SWIGLU_LIMIT)
        act = glu * jax.nn.sigmoid(SWIGLU_ALPHA * glu) * (lin + 1.0)
        y = _dot(act.astype(BF16), wd_ref[0]) + bd_ref[0]
        gate = jnp.zeros((MOE_BM, 1), F32)
        kk = jnp.zeros((MOE_BM, 1), F32)
        for k in range(TOP_K):
            mk = meta[:, META_IDX + k:META_IDX + k + 1] == e_f
            gate = gate + jnp.where(mk, meta[:, META_GATE + k:META_GATE + k + 1], 0.0)
            kk = kk + jnp.where(mk, float(k), 0.0)
        ybuf[slot] = y * gate
        row = kk * float(n_tok) + meta[:, META_TOK:META_TOK + 1]
        row_t = jnp.broadcast_to(row, (MOE_BM, LANES)).T
        idv[slot] = row_t[0:8].astype(jnp.int32)
        pltpu.make_async_copy(idv.at[slot], ids.at[slot], id_sem.at[slot]).start()

    def scatter_issue():
        pltpu.make_async_copy(idv.at[prev], ids.at[prev], id_sem.at[prev]).wait()
        for r in range(MOE_BM):
            pltpu.make_async_copy(ybuf.at[prev, pl.ds(r, 1)], ytm_ref.at[pl.ds(ids[prev, 0, r], 1)],
                                  sc_sem).start()

    def scatter_wait():
        pltpu.make_async_copy(ybuf.at[prev], ytm_ref.at[pl.ds(0, MOE_BM)], sc_sem).wait()

    @pl.when(j == 0)
    def _():
        compute()

    @pl.when(jnp.logical_and(j >= 1, j < nu))
    def _():
        scatter_issue()
        compute()
        scatter_wait()

    @pl.when(j == nu)
    def _():
        scatter_issue()
        scatter_wait()


def _experts(blk_exp, n_used, xs, wgu, bgu, wd, bd, n_tok):
    n_rows = xs.shape[0]
    nb = n_rows // MOE_BM
    row_blk = lambda j, be, nu: (jnp.minimum(j, nu[0] - 1), 0)
    per_e = lambda j, be, nu: (be[jnp.minimum(j, nb - 1)], 0, 0)
    zero = jnp.zeros((MOE_BM, D_MODEL), F32)
    return pl.pallas_call(
        functools.partial(_expert_kernel, n_tok=n_tok),
        out_shape=jax.ShapeDtypeStruct((TOP_K * n_tok + MOE_BM, D_MODEL), F32),
        grid_spec=pltpu.PrefetchScalarGridSpec(
            num_scalar_prefetch=2,
            grid=(nb + 1,),
            in_specs=[pl.BlockSpec((MOE_BM, ROW_W), row_blk),
                      pl.BlockSpec((1, D_MODEL, 2 * D_FF), per_e),
                      pl.BlockSpec((1, 1, 2 * D_FF), per_e),
                      pl.BlockSpec((1, D_FF, D_MODEL), per_e),
                      pl.BlockSpec((1, 1, D_MODEL), per_e),
                      pl.BlockSpec((MOE_BM, D_MODEL), lambda j, be, nu: (0, 0))],
            out_specs=pl.BlockSpec(memory_space=pl.ANY),
            scratch_shapes=[pltpu.VMEM((2, MOE_BM, D_MODEL), F32), pltpu.VMEM((2, 8, MOE_BM), jnp.int32),
                            pltpu.SMEM((2, 8, MOE_BM), jnp.int32), pltpu.SemaphoreType.DMA,
                            pltpu.SemaphoreType.DMA((2,)), pltpu.SemaphoreType.DMA]),
        compiler_params=pltpu.CompilerParams(dimension_semantics=("arbitrary",),
                                             vmem_limit_bytes=VMEM_LIMIT),
        name="experts",
    )(blk_exp, n_used, xs, wgu, bgu, wd, bd, zero)


def _combine_kernel(y0_ref, y1_ref, y2_ref, y3_ref, x1_ref, mod_ref, fg_ref, o_ref):
    m = mod_ref[0]
    moe = (y0_ref[...] + y1_ref[...]) + (y2_ref[...] + y3_ref[...])
    x2 = x1_ref[...] + m[5:6] * moe
    o_ref[...] = _rms(x2) * fg_ref[...]


def _combine(ytm, x1, mod3, fg):
    n_tok = x1.shape[0]
    tc = COMB_TILE
    per_b = SEQ // tc
    nt = n_tok // tc
    slot_spec = lambda k: pl.BlockSpec((tc, D_MODEL), lambda i: (i + k * nt, 0))
    return pl.pallas_call(
        _combine_kernel,
        out_shape=jax.ShapeDtypeStruct((n_tok, D_MODEL), F32),
        grid=(nt,),
        in_specs=[slot_spec(0), slot_spec(1), slot_spec(2), slot_spec(3),
                  pl.BlockSpec((tc, D_MODEL), lambda i: (i, 0)),
                  pl.BlockSpec((1, N_MOD, D_MODEL), lambda i: (i // per_b, 0, 0)),
                  pl.BlockSpec((1, D_MODEL), lambda i: (0, 0))],
        out_specs=pl.BlockSpec((tc, D_MODEL), lambda i: (i, 0)),
        compiler_params=pltpu.CompilerParams(dimension_semantics=("arbitrary",),
                                             vmem_limit_bytes=VMEM_LIMIT),
        name="combine",
    )(ytm, ytm, ytm, ytm, x1, mod3, fg)


def _expansion_matrices():
    r = jnp.arange(LANES)[:, None]
    out64, out128 = [], []
    for d in range(2):
        l64 = jnp.arange(D_SSD)[None, :]
        l128 = jnp.arange(HEADS * LANES)[None, :]
        out64.append((l64 // HEAD_DIM == r - HEADS * d).astype(BF16))
        out128.append((l128 // LANES == r - HEADS * d).astype(BF16))
    return jnp.stack(out64), jnp.stack(out128)


def _pad_lanes(v):
    return jnp.pad(v, [(0, 0)] * (v.ndim - 1) + [(0, LANES - v.shape[-1])])


def kernel(x, c, ctx, c_ctx, w_mod, b_mod, norm1_g, w_in, ssd_conv_w, ssd_conv_b, ssd_dt_bias, ssd_a_log,
           ssd_d, ssd_norm_g, sc_conv_w, w_out, norm2_g, w_router, b_router, w_gate_up, b_gate_up, w_down,
           b_down, final_g):
    bsz = x.shape[0]
    n_tok = bsz * SEQ
    n_assign = n_tok * TOP_K
    n_blocks = n_assign // MOE_BM + N_EXPERTS
    n_rows = n_blocks * MOE_BM
    li = 0

    cvec = jnp.concatenate([c, c_ctx[None, :], jnp.zeros((7, D_MODEL), F32)], axis=0)
    mod3 = _mod(cvec, w_mod[li], b_mod[li][None, :]).reshape(bsz + 8, N_MOD, D_MODEL)

    w = w_in[li]
    wz = w[:, Z0:X0].astype(BF16)
    wxbc = w[:, X0:DT0].astype(BF16)
    wdt = _pad_lanes(w[:, DT0:SC0]).astype(BF16)
    wb = w[:, SC0:SC0 + D_SC].astype(BF16)
    wc = w[:, SC0 + D_SC:SC0 + 2 * D_SC].astype(BF16)
    wu = w[:, SC0 + 2 * D_SC:].astype(BF16)
    g1 = norm1_g[li][None, :]
    cw = ssd_conv_w[li]
    cb = ssd_conv_b[li][None, :]
    dtb = _pad_lanes(ssd_dt_bias[li].reshape(1, 2 * HEADS))
    alog = _pad_lanes(ssd_a_log[li].reshape(1, 2 * HEADS))
    e64, e128 = _expansion_matrices()

    h0 = _ctx_states(ctx, mod3, g1, wxbc[:, :XB_W], wdt, cw[:, :XB_W], cb[:, :XB_W], dtb, alog, e64)

    x2 = x.reshape(n_tok, D_MODEL)
    z, xbc, dtr, scb, v = _inproj(x2, mod3, g1, wz, wxbc, wdt, wb, wc, wu)

    dsk = jnp.repeat(ssd_d[li], HEAD_DIM)[None, :]
    yssd = _ssd(xbc.reshape(bsz, SEQ, XBC_W), z.reshape(bsz, SEQ, D_SSD), dtr.reshape(bsz, SEQ, LANES), h0,
                cw, cb, dtb, alog, dsk, ssd_norm_g[li][None, :], e64, e128)

    wo = w_out[li].astype(BF16)
    g2 = norm2_g[li][None, :]
    x1, lgt = _outproj(x2, yssd.reshape(n_tok, D_SSD), scb, v, mod3, sc_conv_w[li], wo[:D_SSD], wo[D_SSD:],
                       g2, w_router[li].T, b_router[li][:, None])

    dest_t, gate_t, idx_t, meta = _route(lgt, n_blocks)
    dest_flat = dest_t.T.reshape(n_assign)
    cnt = meta[0, :N_EXPERTS]
    start = meta[1, :N_EXPERTS]
    blk_exp = meta[2, :n_blocks]
    n_used = meta[3, :1]

    meta_rows = _pad_lanes(jnp.concatenate(
        [idx_t.T.astype(F32), gate_t.T, jnp.arange(n_tok, dtype=F32)[:, None]], axis=1))
    pad_meta = _pad_lanes(jnp.concatenate(
        [jnp.full((MOE_BM, TOP_K), -1.0, F32), jnp.zeros((MOE_BM, TOP_K), F32),
         (TOP_K * n_tok + jnp.arange(MOE_BM, dtype=F32))[:, None]], axis=1))
    zsrc = jnp.concatenate([jnp.zeros((MOE_BM, D_MODEL), F32), pad_meta], axis=1)

    xs = _dispatch(dest_flat, cnt, start, n_used, x1, meta_rows, mod3, g2, zsrc, n_rows)
    ytm = _experts(blk_exp, n_used, xs, w_gate_up[li].astype(BF16), b_gate_up[li][:, None, :],
                   w_down[li].astype(BF16), b_down[li][:, None, :], n_tok)
    out = _combine(ytm, x1, mod3, final_g[None, :])
    return out.reshape(bsz, SEQ, D_MODEL)
```

```python
import functools

import jax
import jax.numpy as jnp
from jax import lax
from jax.experimental import pallas as pl
from jax.experimental.pallas import tpu as pltpu

F32 = jnp.float32
BF16 = jnp.bfloat16
HIGHEST = lax.Precision.HIGHEST

D_MODEL = 1024
SEQ = 2048
CTX_LEN = 256
GRID_W = 64
D_SSD = 1024
D_SC = 1024
HEAD_DIM = 64
HEADS = 16
GROUPS = 2
STATE = 128
CHUNK = 128
N_EXPERTS = 32
TOP_K = 4
D_FF = 1024
SWIGLU_LIMIT = 7.0
SWIGLU_ALPHA = 1.702
NORM_EPS = 1e-6
N_MOD = 6
XBC_W = D_SSD + 2 * GROUPS * STATE
XB_W = D_SSD + GROUPS * STATE
LANES = 128

Z0 = 0
X0 = Z0 + D_SSD
B0 = X0 + D_SSD
C0 = B0 + GROUPS * STATE
DT0 = C0 + GROUPS * STATE
SC0 = DT0 + 2 * HEADS

TOK_TILE = 512
MOE_BM = 256
RT_TILE = 512
DISP_TILE = 256
COMB_TILE = 512
ROW_W = D_MODEL + LANES
META_IDX = 0
META_GATE = TOP_K
META_TOK = 2 * TOP_K
VMEM_LIMIT = 56 * 1024 * 1024


def _silu(v):
    return v * jax.nn.sigmoid(v)


def _softplus(v):
    return jnp.maximum(v, 0.0) + jnp.log1p(jnp.exp(-jnp.abs(v)))


def _rms(v):
    return v * lax.rsqrt(jnp.mean(v * v, axis=-1, keepdims=True) + NORM_EPS)


def _dot(a, b):
    return jnp.dot(a, b, preferred_element_type=F32)


def _expand(v, e, pieces):
    acc = None
    rem = v
    for _ in range(pieces):
        p = rem.astype(BF16)
        rem = rem - p.astype(F32)
        t = _dot(p, e)
        acc = t if acc is None else acc + t
    return acc


def _mod_kernel(c_ref, w_ref, b_ref, o_ref):
    o_ref[...] = jnp.dot(_silu(c_ref[...]), w_ref[...], precision=HIGHEST,
                         preferred_element_type=F32) + b_ref[...]


def _mod(cvec, w_mod, b_mod):
    rows = cvec.shape[0]
    n = w_mod.shape[1]
    tn = 1536
    return pl.pallas_call(
        _mod_kernel,
        out_shape=jax.ShapeDtypeStruct((rows, n), F32),
        grid=(n // tn,),
        in_specs=[pl.BlockSpec((rows, D_MODEL), lambda j: (0, 0)),
                  pl.BlockSpec((D_MODEL, tn), lambda j: (0, j)),
                  pl.BlockSpec((1, tn), lambda j: (0, j))],
        out_specs=pl.BlockSpec((rows, tn), lambda j: (0, j)),
        compiler_params=pltpu.CompilerParams(dimension_semantics=("arbitrary",),
                                             vmem_limit_bytes=VMEM_LIMIT),
        name="mod",
    )(cvec, w_mod, b_mod)


def _ctx_kernel(ctx_ref, mod_ref, g1_ref, wxb_ref, wdt_ref, cw_ref, cb_ref, dtb_ref, alog_ref, e64_ref,
                h0_ref):
    L = CTX_LEN
    m = mod_ref[0]
    hc = _rms(ctx_ref[0]) * g1_ref[...] * (1.0 + m[1:2]) + m[0:1]
    hb = hc.astype(BF16)
    pxb = _dot(hb, wxb_ref[...])
    dtr = _dot(hb, wdt_ref[...])
    rowi = lax.broadcasted_iota(jnp.int32, (L, XB_W), 0)
    dn = jnp.where(rowi == 0, 0.0, pltpu.roll(pxb, 1, 0))
    up = jnp.where(rowi == L - 1, 0.0, pltpu.roll(pxb, L - 1, 0))
    cw = cw_ref[...]
    xb = _silu(cw[0:1] * dn + cw[1:2] * pxb + cw[2:3] * up + cb_ref[...])
    xs = xb[:, :D_SSD]
    bm = xb[:, D_SSD:].astype(BF16)
    dt = _softplus(dtr + dtb_ref[...])
    da = dt * (-jnp.exp(alog_ref[...]))
    ri = lax.broadcasted_iota(jnp.int32, (L, L), 0)
    ci = lax.broadcasted_iota(jnp.int32, (L, L), 1)
    for d in range(2):
        tri = (ci <= ri) if d == 0 else (ci >= ri)
        cum = jnp.dot(tri.astype(F32), da, precision=HIGHEST, preferred_element_type=F32)
        last = cum[L - 1:L] if d == 0 else cum[0:1]
        w_e = _expand(jnp.exp(last - cum) * dt, e64_ref[d], 2)
        xw = (xs * w_e).astype(BF16)
        for g in range(GROUPS):
            gw = D_SSD // GROUPS
            st = lax.dot_general(bm[:, g * STATE:(g + 1) * STATE], xw[:, g * gw:(g + 1) * gw],
                                 (((0,), (0,)), ((), ())), preferred_element_type=F32)
            h0_ref[0, d, :, g * gw:(g + 1) * gw] = st


def _ctx_states(ctx, mod3, g1, wxb, wdt, cw, cb, dtb, alog, e64):
    bsz = ctx.shape[0]
    mod_row = bsz
    const = lambda *shape: pl.BlockSpec(shape, lambda b: (0,) * len(shape))
    return pl.pallas_call(
        _ctx_kernel,
        out_shape=jax.ShapeDtypeStruct((bsz, 2, STATE, D_SSD), F32),
        grid=(bsz,),
        in_specs=[pl.BlockSpec((1, CTX_LEN, D_MODEL), lambda b: (b, 0, 0)),
                  pl.BlockSpec((1, N_MOD, D_MODEL), lambda b: (mod_row, 0, 0)),
                  const(1, D_MODEL), const(D_MODEL, XB_W), const(D_MODEL, LANES),
                  const(3, XB_W), const(1, XB_W), const(1, LANES), const(1, LANES),
                  const(2, LANES, D_SSD)],
        out_specs=pl.BlockSpec((1, 2, STATE, D_SSD), lambda b: (b, 0, 0, 0)),
        compiler_params=pltpu.CompilerParams(dimension_semantics=("arbitrary",),
                                             vmem_limit_bytes=VMEM_LIMIT),
        name="ctx_states",
    )(ctx, mod3, g1, wxb, wdt, cw, cb, dtb, alog, e64)


def _inproj_kernel(x_ref, mod_ref, g1_ref, wz_ref, wxbc_ref, wdt_ref, wb_ref, wc_ref, wu_ref,
                   z_ref, xbc_ref, dt_ref, scb_ref, v_ref):
    m = mod_ref[0]
    hx = _rms(x_ref[...]) * g1_ref[...] * (1.0 + m[1:2]) + m[0:1]
    hb = hx.astype(BF16)
    z_ref[...] = _dot(hb, wz_ref[...]).astype(BF16)
    xbc_ref[...] = _dot(hb, wxbc_ref[...]).astype(BF16)
    dt_ref[...] = _dot(hb, wdt_ref[...])
    scb_ref[...] = _dot(hb, wb_ref[...]).astype(BF16)
    v_ref[...] = (_dot(hb, wc_ref[...]) * _dot(hb, wu_ref[...])).astype(BF16)


def _inproj(x2, mod3, g1, wz, wxbc, wdt, wb, wc, wu):
    t = x2.shape[0]
    tm = TOK_TILE
    per_b = SEQ // tm
    const = lambda *shape: pl.BlockSpec(shape, lambda i: (0,) * len(shape))
    tile = lambda w: pl.BlockSpec((tm, w), lambda i: (i, 0))
    return pl.pallas_call(
        _inproj_kernel,
        out_shape=(jax.ShapeDtypeStruct((t, D_SSD), BF16), jax.ShapeDtypeStruct((t, XBC_W), BF16),
                   jax.ShapeDtypeStruct((t, LANES), F32), jax.ShapeDtypeStruct((t, D_SC), BF16),
                   jax.ShapeDtypeStruct((t, D_SC), BF16)),
        grid=(t // tm,),
        in_specs=[tile(D_MODEL),
                  pl.BlockSpec((1, N_MOD, D_MODEL), lambda i: (i // per_b, 0, 0)),
                  const(1, D_MODEL), const(D_MODEL, D_SSD), const(D_MODEL, XBC_W), const(D_MODEL, LANES),
                  const(D_MODEL, D_SC), const(D_MODEL, D_SC), const(D_MODEL, D_SC)],
        out_specs=(tile(D_SSD), tile(XBC_W), tile(LANES), tile(D_SC), tile(D_SC)),
        compiler_params=pltpu.CompilerParams(dimension_semantics=("arbitrary",),
                                             vmem_limit_bytes=VMEM_LIMIT),
        name="inproj",
    )(x2, mod3, g1, wz, wxbc, wdt, wb, wc, wu)


def _ssd_kernel(xbc_ref, z_ref, dt_ref, h0_ref, cw_ref, cb_ref, dtb_ref, alog_ref, dsk_ref, g_ref,
                e64_ref, e128_ref, o_ref, xc_ref, y_ref, s_ref):
    Q = CHUNK
    nck = SEQ // Q
    gw = D_SSD // GROUPS

    rowi = lax.broadcasted_iota(jnp.int32, (Q, XBC_W), 0)

    def conv_body(c, carry):
        r0 = pl.multiple_of(c * Q, Q)
        main = xbc_ref[0, pl.ds(r0, Q), :].astype(F32)
        pstart = pl.multiple_of(jnp.maximum(r0 - 16, 0), 16)
        nstart = pl.multiple_of(jnp.minimum(r0 + Q, SEQ - 16), 16)
        prev = xbc_ref[0, pl.ds(pstart, 16), :].astype(F32)[15:16]
        nxt = xbc_ref[0, pl.ds(nstart, 16), :].astype(F32)[0:1]
        prev = jnp.where(c > 0, prev, 0.0)
        nxt = jnp.where(c < nck - 1, nxt, 0.0)
        dn = jnp.where(rowi == 0, prev, pltpu.roll(main, 1, 0))
        up = jnp.where(rowi == Q - 1, nxt, pltpu.roll(main, Q - 1, 0))
        cw = cw_ref[...]
        conv = cw[0:1] * dn + cw[1:2] * main + cw[2:3] * up + cb_ref[...]
        xc_ref[pl.ds(r0, Q), :] = _silu(conv).astype(BF16)
        return carry

    lax.fori_loop(0, nck, conv_body, 0)

    ri = lax.broadcasted_iota(jnp.int32, (Q, Q), 0)
    ci = lax.broadcasted_iota(jnp.int32, (Q, Q), 1)
    lane = lax.broadcasted_iota(jnp.int32, (Q, LANES), 1)
    a_neg = -jnp.exp(alog_ref[...])

    def chunk(c, d):
        r0 = pl.multiple_of(c * Q, Q)
        rows = pl.ds(r0, Q)
        xs = xc_ref[rows, 0:D_SSD].astype(F32)
        bm = xc_ref[rows, D_SSD:D_SSD + GROUPS * STATE]
        cm = xc_ref[rows, D_SSD + GROUPS * STATE:XBC_W]
        dt = _softplus(dt_ref[0, rows, :] + dtb_ref[...])
        da = dt * a_neg
        tri = (ci <= ri) if d == 0 else (ci >= ri)
        cum = jnp.dot(tri.astype(F32), da, precision=HIGHEST, preferred_element_type=F32)
        cum_t = cum.T
        last = cum[Q - 1:Q] if d == 0 else cum[0:1]
        ecum_e = _expand(jnp.exp(cum), e64_ref[d], 2)
        dt_e = _expand(dt, e64_ref[d], 2)
        w_e = _expand(jnp.exp(last - cum) * dt, e64_ref[d], 2)
        colb = _expand(cum, e128_ref[d], 3)
        decay_e = ecum_e[Q - 1:Q] if d == 0 else ecum_e[0:1]

        gmat = [lax.dot_general(cm[:, g * STATE:(g + 1) * STATE], bm[:, g * STATE:(g + 1) * STATE],
                                (((1,), (1,)), ((), ())), preferred_element_type=F32)
                for g in range(GROUPS)]
        xdt = xs * dt_e
        y_parts = []
        for p in range(HEADS // 2):
            g = (2 * p) // (HEADS // GROUPS)
            ms = []
            for hh in (2 * p, 2 * p + 1):
                seg = colb[:, hh * LANES:(hh + 1) * LANES] - cum_t[HEADS * d + hh:HEADS * d + hh + 1, :]
                mm = jnp.where(tri, jnp.exp(jnp.where(tri, seg, 0.0)), 0.0) * gmat[g]
                ms.append(mm.astype(BF16))
            mcat = jnp.concatenate(ms, axis=1)
            xp = xdt[:, p * LANES:(p + 1) * LANES]
            rhs = jnp.concatenate([jnp.where(lane < HEAD_DIM, xp, 0.0).astype(BF16),
                                   jnp.where(lane >= HEAD_DIM, xp, 0.0).astype(BF16)], axis=0)
            y_parts.append(_dot(mcat, rhs))
        y_diag = jnp.concatenate(y_parts, axis=1)

        s_old = s_ref[...]
        s_bf = s_old.astype(BF16)
        y_off = jnp.concatenate(
            [_dot(cm[:, g * STATE:(g + 1) * STATE], s_bf[:, g * gw:(g + 1) * gw]) for g in range(GROUPS)],
            axis=1)
        y = y_diag + y_off * ecum_e

        xw = (xs * w_e).astype(BF16)
        upd = jnp.concatenate(
            [lax.dot_general(bm[:, g * STATE:(g + 1) * STATE], xw[:, g * gw:(g + 1) * gw],
                             (((0,), (0,)), ((), ())), preferred_element_type=F32) for g in range(GROUPS)],
            axis=1)
        s_ref[...] = s_old * decay_e + upd

        if d == 0:
            y_ref[rows, :] = y + dsk_ref[...] * xs
        else:
            tot = y_ref[rows, :] + y
            zz = z_ref[0, rows, :].astype(F32)
            gz = tot * _silu(zz)
            outs = []
            for g in range(GROUPS):
                gg = gz[:, g * gw:(g + 1) * gw]
                outs.append(gg * lax.rsqrt(jnp.mean(gg * gg, axis=-1, keepdims=True) + NORM_EPS))
            o_ref[0, rows, :] = (jnp.concatenate(outs, axis=1) * g_ref[...]).astype(BF16)

    s_ref[...] = h0_ref[0, 0]

    def fwd_body(i, carry):
        chunk(i, 0)
        return carry

    lax.fori_loop(0, nck, fwd_body, 0)

    s_ref[...] = h0_ref[0, 1]

    def bwd_body(i, carry):
        chunk(nck - 1 - i, 1)
        return carry

    lax.fori_loop(0, nck, bwd_body, 0)


def _ssd(xbc3, z3, dt3, h0, cw, cb, dtb, alog, dsk, g, e64, e128):
    bsz = xbc3.shape[0]
    const = lambda *shape: pl.BlockSpec(shape, lambda b: (0,) * len(shape))
    seq = lambda w: pl.BlockSpec((1, SEQ, w), lambda b: (b, 0, 0))
    return pl.pallas_call(
        _ssd_kernel,
        out_shape=jax.ShapeDtypeStruct((bsz, SEQ, D_SSD), BF16),
        grid=(bsz,),
        in_specs=[seq(XBC_W), seq(D_SSD), seq(LANES),
                  pl.BlockSpec((1, 2, STATE, D_SSD), lambda b: (b, 0, 0, 0)),
                  const(3, XBC_W), const(1, XBC_W), const(1, LANES), const(1, LANES),
                  const(1, D_SSD), const(1, D_SSD), const(2, LANES, D_SSD), const(2, LANES, HEADS * LANES)],
        out_specs=seq(D_SSD),
        scratch_shapes=[pltpu.VMEM((SEQ, XBC_W), BF16), pltpu.VMEM((SEQ, D_SSD), F32),
                        pltpu.VMEM((STATE, D_SSD), F32)],
        compiler_params=pltpu.CompilerParams(dimension_semantics=("arbitrary",),
                                             vmem_limit_bytes=VMEM_LIMIT),
        name="ssd",
    )(xbc3, z3, dt3, h0, cw, cb, dtb, alog, dsk, g, e64, e128)


def _outproj_kernel(x_ref, yssd_ref, scb_ref, v_ref, vp_ref, vn_ref, mod_ref, scw_ref, wo1_ref, wo2_ref,
                    g2_ref, wrt_ref, br_ref, x1_ref, lg_ref):
    tm = TOK_TILE
    per_b = SEQ // tm
    i = pl.program_id(0)
    first = (i % per_b) == 0
    last = (i % per_b) == per_b - 1
    m = mod_ref[0]
    v = v_ref[...].astype(F32)
    vp = jnp.where(first, 0.0, vp_ref[...].astype(F32))
    vn = jnp.where(last, 0.0, vn_ref[...].astype(F32))
    dn = jnp.concatenate([vp, v[:tm - GRID_W]], axis=0)
    up = jnp.concatenate([v[GRID_W:], vn], axis=0)
    scw = scw_ref[...]
    ysc = scb_ref[...].astype(F32) * (scw[0:1] * dn + scw[1:2] * v + scw[2:3] * up)
    out = _dot(yssd_ref[...], wo1_ref[...]) + _dot(ysc.astype(BF16), wo2_ref[...])
    x1 = x_ref[...] + m[2:3] * out
    x1_ref[...] = x1
    h2 = _rms(x1) * g2_ref[...] * (1.0 + m[4:5]) + m[3:4]
    lg_ref[...] = lax.dot_general(wrt_ref[...], h2, (((1,), (1,)), ((), ())), precision=HIGHEST,
                                  preferred_element_type=F32) + br_ref[...]


def _outproj(x2, yssd, scb, v, mod3, scw, wo1, wo2, g2, wrt, br):
    t = x2.shape[0]
    tm = TOK_TILE
    per_b = SEQ // tm
    r = tm // GRID_W
    nrow = t // GRID_W
    const = lambda *shape: pl.BlockSpec(shape, lambda i: (0,) * len(shape))
    tile = lambda w: pl.BlockSpec((tm, w), lambda i: (i, 0))
    return pl.pallas_call(
        _outproj_kernel,
        out_shape=(jax.ShapeDtypeStruct((t, D_MODEL), F32), jax.ShapeDtypeStruct((N_EXPERTS, t), F32)),
        grid=(t // tm,),
        in_specs=[tile(D_MODEL), tile(D_SSD), tile(D_SC), tile(D_SC),
                  pl.BlockSpec((GRID_W, D_SC), lambda i: (jnp.maximum(i * r - 1, 0), 0)),
                  pl.BlockSpec((GRID_W, D_SC), lambda i: (jnp.minimum((i + 1) * r, nrow - 1), 0)),
                  pl.BlockSpec((1, N_MOD, D_MODEL), lambda i: (i // per_b, 0, 0)),
                  const(3, D_SC), const(D_SSD, D_MODEL), const(D_SC, D_MODEL), const(1, D_MODEL),
                  const(N_EXPERTS, D_MODEL), const(N_EXPERTS, 1)],
        out_specs=(tile(D_MODEL), pl.BlockSpec((N_EXPERTS, tm), lambda i: (0, i))),
        compiler_params=pltpu.CompilerParams(dimension_semantics=("arbitrary",),
                                             vmem_limit_bytes=VMEM_LIMIT),
        name="outproj",
    )(x2, yssd, scb, v, v, v, mod3, scw, wo1, wo2, g2, wrt, br)


def _route_kernel(lg_ref, dest_ref, gate_ref, idx_ref, meta_ref, rank_ref, carry_ref, *, n_tok, n_blocks):
    tt = RT_TILE
    ne = N_EXPERTS
    eio = lax.broadcasted_iota(jnp.int32, (ne, tt), 0)
    si = lax.broadcasted_iota(jnp.int32, (tt, tt), 0)
    ti = lax.broadcasted_iota(jnp.int32, (tt, tt), 1)
    before = (si < ti).astype(BF16)
    carry_ref[...] = jnp.zeros_like(carry_ref)

    def tile_body(j, c):
        t0 = pl.multiple_of(j * tt, tt)
        l = lg_ref[:, pl.ds(t0, tt)]
        onehot = jnp.zeros((ne, tt), F32)
        tops, sels = [], []
        for _ in range(TOP_K):
            mx = jnp.max(l, axis=0, keepdims=True)
            idx = jnp.min(jnp.where(l == mx, eio, ne), axis=0, keepdims=True)
            sel = eio == idx
            l = jnp.where(sel, -jnp.inf, l)
            onehot = onehot + sel.astype(F32)
            tops.append(mx)
            sels.append(sel)
            idx_ref[pl.ds(len(tops) - 1, 1), pl.ds(t0, tt)] = idx
        ex = [jnp.exp(tv - tops[0]) for tv in tops]
        den = ex[0] + ex[1] + ex[2] + ex[3]
        prefix = _dot(onehot.astype(BF16), before) + carry_ref[:, 0:1]
        for k in range(TOP_K):
            gate_ref[pl.ds(k, 1), pl.ds(t0, tt)] = ex[k] / den
            rk = jnp.sum(jnp.where(sels[k], prefix, 0.0), axis=0, keepdims=True)
            rank_ref[pl.ds(k, 1), pl.ds(t0, tt)] = rk.astype(jnp.int32)
        carry_ref[...] = carry_ref[...] + jnp.sum(onehot, axis=1, keepdims=True)
        return c

    lax.fori_loop(0, n_tok // tt, tile_body, 0)

    counts = carry_ref[...]
    padded = jnp.floor((counts + (MOE_BM - 1)) * (1.0 / MOE_BM)) * MOE_BM
    er = lax.broadcasted_iota(jnp.int32, (ne, ne), 0)
    ec = lax.broadcasted_iota(jnp.int32, (ne, ne), 1)
    pad_start = jnp.dot((ec < er).astype(F32), padded, precision=HIGHEST, preferred_element_type=F32)
    pad_end = pad_start + padded

    def dest_body(j, c):
        t0 = pl.multiple_of(j * tt, tt)
        for k in range(TOP_K):
            idx = idx_ref[pl.ds(k, 1), pl.ds(t0, tt)]
            base = jnp.sum(jnp.where(eio == idx, pad_start[:, 0:1], 0.0), axis=0, keepdims=True)
            dest_ref[pl.ds(k, 1), pl.ds(t0, tt)] = base.astype(jnp.int32) + rank_ref[pl.ds(k, 1), pl.ds(t0, tt)]
        return c

    lax.fori_loop(0, n_tok // tt, dest_body, 0)

    width = meta_ref.shape[1]
    sub = lax.broadcasted_iota(jnp.int32, (ne, width), 0)
    lan = lax.broadcasted_iota(jnp.int32, (ne, width), 1)
    diag = sub == lan
    cnt_row = jnp.sum(jnp.where(diag, counts[:, 0:1], 0.0), axis=0, keepdims=True)
    start_row = jnp.sum(jnp.where(diag, pad_start[:, 0:1], 0.0), axis=0, keepdims=True)
    blk_start = (lan * MOE_BM).astype(F32)
    blk_exp = jnp.sum((pad_end[:, 0:1] <= blk_start).astype(F32), axis=0, keepdims=True)
    blk_exp = jnp.minimum(blk_exp, float(ne - 1))
    used = jnp.sum(padded[:, 0:1], axis=0, keepdims=True) * (1.0 / MOE_BM)
    meta_ref[0:1, :] = cnt_row.astype(jnp.int32)
    meta_ref[1:2, :] = start_row.astype(jnp.int32)
    meta_ref[2:3, :] = blk_exp.astype(jnp.int32)
    meta_ref[3:4, :] = jnp.broadcast_to(used, (1, width)).astype(jnp.int32)
    meta_ref[4:8, :] = jnp.zeros((4, width), jnp.int32)


def _route(lgt, n_blocks):
    ne, n_tok = lgt.shape
    width = -(-n_blocks // LANES) * LANES
    full = lambda *shape: pl.BlockSpec(shape, lambda: (0,) * len(shape))
    return pl.pallas_call(
        functools.partial(_route_kernel, n_tok=n_tok, n_blocks=n_blocks),
        out_shape=(jax.ShapeDtypeStruct((TOP_K, n_tok), jnp.int32),
                   jax.ShapeDtypeStruct((TOP_K, n_tok), F32),
                   jax.ShapeDtypeStruct((TOP_K, n_tok), jnp.int32),
                   jax.ShapeDtypeStruct((8, width), jnp.int32)),
        in_specs=[full(ne, n_tok)],
        out_specs=(full(TOP_K, n_tok), full(TOP_K, n_tok), full(TOP_K, n_tok), full(8, width)),
        scratch_shapes=[pltpu.VMEM((TOP_K, n_tok), jnp.int32), pltpu.VMEM((ne, LANES), F32)],
        compiler_params=pltpu.CompilerParams(vmem_limit_bytes=VMEM_LIMIT),
        name="route",
    )(lgt)


def _dispatch_kernel(dest_ref, cnt_ref, start_ref, nu_ref, x1_ref, meta_ref, mod_ref, g2_ref, zsrc_ref, xs_ref,
                     hbuf, sem, zsem):
    i = pl.program_id(0)
    n = pl.num_programs(0)
    tl = DISP_TILE
    slot = i % 2
    nb = xs_ref.shape[0] // MOE_BM

    def zero_block(b):
        return pltpu.make_async_copy(zsrc_ref, xs_ref.at[pl.ds(b * MOE_BM, MOE_BM)], zsem)

    @pl.when(i == 0)
    def _():
        def start_e(e, c):
            @pl.when(cnt_ref[e] > 0)
            def _():
                zero_block((start_ref[e] + cnt_ref[e] - 1) // MOE_BM).start()
            return c

        def wait_e(e, c):
            @pl.when(cnt_ref[e] > 0)
            def _():
                zero_block(0).wait()
            return c

        def start_t(b, c):
            zero_block(b).start()
            return c

        def wait_t(b, c):
            zero_block(0).wait()
            return c

        lax.fori_loop(0, N_EXPERTS, start_e, 0)
        lax.fori_loop(nu_ref[0], nb, start_t, 0)
        lax.fori_loop(0, N_EXPERTS, wait_e, 0)
        lax.fori_loop(nu_ref[0], nb, wait_t, 0)

    m = mod_ref[0]
    hbuf[slot, :, 0:D_MODEL] = _rms(x1_ref[...]) * g2_ref[...] * (1.0 + m[4:5]) + m[3:4]
    hbuf[slot, :, D_MODEL:ROW_W] = meta_ref[...]

    def per_tok(t, c):
        tok = i * tl + t
        for k in range(TOP_K):
            d = dest_ref[tok * TOP_K + k]
            pltpu.make_async_copy(hbuf.at[slot, pl.ds(t, 1)], xs_ref.at[pl.ds(d, 1)], sem.at[slot]).start()
        return c

    lax.fori_loop(0, tl, per_tok, 0)

    def wait_slot(sl):
        for _ in range(TOP_K):
            pltpu.make_async_copy(hbuf.at[sl], xs_ref.at[pl.ds(0, tl)], sem.at[sl]).wait()

    @pl.when(i > 0)
    def _():
        wait_slot(1 - slot)

    @pl.when(i == n - 1)
    def _():
        wait_slot(slot)


def _dispatch(dest_flat, cnt, start, n_used, x1, meta_rows, mod3, g2, zsrc, n_rows):
    n_tok = x1.shape[0]
    tl = DISP_TILE
    per_b = SEQ // tl
    return pl.pallas_call(
        _dispatch_kernel,
        out_shape=jax.ShapeDtypeStruct((n_rows, ROW_W), F32),
        grid_spec=pltpu.PrefetchScalarGridSpec(
            num_scalar_prefetch=4,
            grid=(n_tok // tl,),
            in_specs=[pl.BlockSpec((tl, D_MODEL), lambda i, *_: (i, 0)),
                      pl.BlockSpec((tl, LANES), lambda i, *_: (i, 0)),
                      pl.BlockSpec((1, N_MOD, D_MODEL), lambda i, *_: (i // per_b, 0, 0)),
                      pl.BlockSpec((1, D_MODEL), lambda i, *_: (0, 0)),
                      pl.BlockSpec((MOE_BM, ROW_W), lambda i, *_: (0, 0))],
            out_specs=pl.BlockSpec(memory_space=pl.ANY),
            scratch_shapes=[pltpu.VMEM((2, tl, ROW_W), F32),
                            pltpu.SemaphoreType.DMA((2,)), pltpu.SemaphoreType.DMA]),
        compiler_params=pltpu.CompilerParams(dimension_semantics=("arbitrary",),
                                             vmem_limit_bytes=VMEM_LIMIT),
        name="dispatch",
    )(dest_flat, cnt, start, n_used, x1, meta_rows, mod3, g2, zsrc)


def _expert_kernel(be_ref, nu_ref, xs_ref, wgu_ref, bgu_ref, wd_ref, bd_ref, zero_ref, ytm_ref,
                   ybuf, idv, ids, sc_sem, id_sem, z_sem, *, n_tok):
    j = pl.program_id(0)
    nbk = pl.num_programs(0) - 1
    nu = nu_ref[0]
    slot = j % 2
    prev = 1 - slot

    @pl.when(j == 0)
    def _():
        cp = pltpu.make_async_copy(zero_ref, ytm_ref.at[pl.ds(TOP_K * n_tok, MOE_BM)], z_sem)
        cp.start()
        cp.wait()

    n_pc = 4
    pw1 = D_FF // n_pc
    pw2 = D_MODEL // n_pc
    groups = [48] * n_pc + [16] * n_pc
    assert sum(groups) == MOE_BM

    def scatter_group(g):
        lo = sum(groups[:g])
        for r in range(lo, lo + groups[g]):
            pltpu.make_async_copy(ybuf.at[prev, pl.ds(r, 1)], ytm_ref.at[pl.ds(ids[prev, 0, r], 1)],
                                  sc_sem).start()

    def compute(with_scatter):
        e_f = be_ref[jnp.minimum(j, nbk - 1)].astype(F32)
        xb = xs_ref[:, 0:D_MODEL].astype(BF16)
        meta = xs_ref[:, D_MODEL:ROW_W]
        gate = jnp.zeros((MOE_BM, 1), F32)
        kk = jnp.zeros((MOE_BM, 1), F32)
        for k in range(TOP_K):
            mk = meta[:, META_IDX + k:META_IDX + k + 1] == e_f
            gate = gate + jnp.where(mk, meta[:, META_GATE + k:META_GATE + k + 1], 0.0)
            kk = kk + jnp.where(mk, float(k), 0.0)
        for c in range(n_pc):
            if with_scatter:
                scatter_group(c)
            cg = slice(c * pw1, (c + 1) * pw1)
            cl = slice(D_FF + c * pw1, D_FF + (c + 1) * pw1)
            glu = jnp.minimum(_dot(xb, wgu_ref[0, :, cg]) + bgu_ref[0, :, cg], SWIGLU_LIMIT)
            lin = jnp.clip(_dot(xb, wgu_ref[0, :, cl]) + bgu_ref[0, :, cl], -SWIGLU_LIMIT, SWIGLU_LIMIT)
            ybuf[2, :, cg] = glu * jax.nn.sigmoid(SWIGLU_ALPHA * glu) * (lin + 1.0)
        act = ybuf[2].astype(BF16)
        for c in range(n_pc):
            if with_scatter:
                scatter_group(n_pc + c)
            cs = slice(c * pw2, (c + 1) * pw2)
            ybuf[slot, :, cs] = (_dot(act, wd_ref[0, :, cs]) + bd_ref[0, :, cs]) * gate
        row = kk * float(n_tok) + meta[:, META_TOK:META_TOK + 1]
        row_t = jnp.broadcast_to(row, (MOE_BM, LANES)).T
        idv[slot] = row_t[0:8].astype(jnp.int32)
        pltpu.make_async_copy(idv.at[slot], ids.at[slot], id_sem.at[slot]).start()

    def ids_wait():
        pltpu.make_async_copy(idv.at[prev], ids.at[prev], id_sem.at[prev]).wait()

    def scatter_wait():
        pltpu.make_async_copy(ybuf.at[prev], ytm_ref.at[pl.ds(0, MOE_BM)], sc_sem).wait()

    @pl.when(j == 0)
    def _():
        compute(False)

    @pl.when(jnp.logical_and(j >= 1, j < nu))
    def _():
        ids_wait()
        compute(True)
        scatter_wait()

    @pl.when(j == nu)
    def _():
        ids_wait()
        for g in range(len(groups)):
            scatter_group(g)
        scatter_wait()


def _experts(blk_exp, n_used, xs, wgu, bgu, wd, bd, n_tok):
    n_rows = xs.shape[0]
    nb = n_rows // MOE_BM
    row_blk = lambda j, be, nu: (jnp.minimum(j, nu[0] - 1), 0)
    per_e = lambda j, be, nu: (be[jnp.minimum(j, nb - 1)], 0, 0)
    zero = jnp.zeros((MOE_BM, D_MODEL), F32)
    return pl.pallas_call(
        functools.partial(_expert_kernel, n_tok=n_tok),
        out_shape=jax.ShapeDtypeStruct((TOP_K * n_tok + MOE_BM, D_MODEL), F32),
        grid_spec=pltpu.PrefetchScalarGridSpec(
            num_scalar_prefetch=2,
            grid=(nb + 1,),
            in_specs=[pl.BlockSpec((MOE_BM, ROW_W), row_blk),
                      pl.BlockSpec((1, D_MODEL, 2 * D_FF), per_e),
                      pl.BlockSpec((1, 1, 2 * D_FF), per_e),
                      pl.BlockSpec((1, D_FF, D_MODEL), per_e),
                      pl.BlockSpec((1, 1, D_MODEL), per_e),
                      pl.BlockSpec((MOE_BM, D_MODEL), lambda j, be, nu: (0, 0))],
            out_specs=pl.BlockSpec(memory_space=pl.ANY),
            scratch_shapes=[pltpu.VMEM((3, MOE_BM, D_MODEL), F32), pltpu.VMEM((2, 8, MOE_BM), jnp.int32),
                            pltpu.SMEM((2, 8, MOE_BM), jnp.int32), pltpu.SemaphoreType.DMA,
                            pltpu.SemaphoreType.DMA((2,)), pltpu.SemaphoreType.DMA]),
        compiler_params=pltpu.CompilerParams(dimension_semantics=("arbitrary",),
                                             vmem_limit_bytes=VMEM_LIMIT),
        name="experts",
    )(blk_exp, n_used, xs, wgu, bgu, wd, bd, zero)


def _combine_kernel(y0_ref, y1_ref, y2_ref, y3_ref, x1_ref, mod_ref, fg_ref, o_ref):
    m = mod_ref[0]
    moe = (y0_ref[...] + y1_ref[...]) + (y2_ref[...] + y3_ref[...])
    x2 = x1_ref[...] + m[5:6] * moe
    o_ref[...] = _rms(x2) * fg_ref[...]


def _combine(ytm, x1, mod3, fg):
    n_tok = x1.shape[0]
    tc = COMB_TILE
    per_b = SEQ // tc
    nt = n_tok // tc
    slot_spec = lambda k: pl.BlockSpec((tc, D_MODEL), lambda i: (i + k * nt, 0))
    return pl.pallas_call(
        _combine_kernel,
        out_shape=jax.ShapeDtypeStruct((n_tok, D_MODEL), F32),
        grid=(nt,),
        in_specs=[slot_spec(0), slot_spec(1), slot_spec(2), slot_spec(3),
                  pl.BlockSpec((tc, D_MODEL), lambda i: (i, 0)),
                  pl.BlockSpec((1, N_MOD, D_MODEL), lambda i: (i // per_b, 0, 0)),
                  pl.BlockSpec((1, D_MODEL), lambda i: (0, 0))],
        out_specs=pl.BlockSpec((tc, D_MODEL), lambda i: (i, 0)),
        compiler_params=pltpu.CompilerParams(dimension_semantics=("arbitrary",),
                                             vmem_limit_bytes=VMEM_LIMIT),
        name="combine",
    )(ytm, ytm, ytm, ytm, x1, mod3, fg)


def _expansion_matrices():
    r = jnp.arange(LANES)[:, None]
    out64, out128 = [], []
    for d in range(2):
        l64 = jnp.arange(D_SSD)[None, :]
        l128 = jnp.arange(HEADS * LANES)[None, :]
        out64.append((l64 // HEAD_DIM == r - HEADS * d).astype(BF16))
        out128.append((l128 // LANES == r - HEADS * d).astype(BF16))
    return jnp.stack(out64), jnp.stack(out128)


def _pad_lanes(v):
    return jnp.pad(v, [(0, 0)] * (v.ndim - 1) + [(0, LANES - v.shape[-1])])


def kernel(x, c, ctx, c_ctx, w_mod, b_mod, norm1_g, w_in, ssd_conv_w, ssd_conv_b, ssd_dt_bias, ssd_a_log,
           ssd_d, ssd_norm_g, sc_conv_w, w_out, norm2_g, w_router, b_router, w_gate_up, b_gate_up, w_down,
           b_down, final_g):
    bsz = x.shape[0]
    n_tok = bsz * SEQ
    n_assign = n_tok * TOP_K
    n_blocks = n_assign // MOE_BM + N_EXPERTS
    n_rows = n_blocks * MOE_BM
    li = 0

    cvec = jnp.concatenate([c, c_ctx[None, :], jnp.zeros((7, D_MODEL), F32)], axis=0)
    mod3 = _mod(cvec, w_mod[li], b_mod[li][None, :]).reshape(bsz + 8, N_MOD, D_MODEL)

    w = w_in[li]
    wz = w[:, Z0:X0].astype(BF16)
    wxbc = w[:, X0:DT0].astype(BF16)
    wdt = _pad_lanes(w[:, DT0:SC0]).astype(BF16)
    wb = w[:, SC0:SC0 + D_SC].astype(BF16)
    wc = w[:, SC0 + D_SC:SC0 + 2 * D_SC].astype(BF16)
    wu = w[:, SC0 + 2 * D_SC:].astype(BF16)
    g1 = norm1_g[li][None, :]
    cw = ssd_conv_w[li]
    cb = ssd_conv_b[li][None, :]
    dtb = _pad_lanes(ssd_dt_bias[li].reshape(1, 2 * HEADS))
    alog = _pad_lanes(ssd_a_log[li].reshape(1, 2 * HEADS))
    e64, e128 = _expansion_matrices()

    h0 = _ctx_states(ctx, mod3, g1, wxbc[:, :XB_W], wdt, cw[:, :XB_W], cb[:, :XB_W], dtb, alog, e64)

    x2 = x.reshape(n_tok, D_MODEL)
    z, xbc, dtr, scb, v = _inproj(x2, mod3, g1, wz, wxbc, wdt, wb, wc, wu)

    dsk = jnp.repeat(ssd_d[li], HEAD_DIM)[None, :]
    yssd = _ssd(xbc.reshape(bsz, SEQ, XBC_W), z.reshape(bsz, SEQ, D_SSD), dtr.reshape(bsz, SEQ, LANES), h0,
                cw, cb, dtb, alog, dsk, ssd_norm_g[li][None, :], e64, e128)

    wo = w_out[li].astype(BF16)
    g2 = norm2_g[li][None, :]
    x1, lgt = _outproj(x2, yssd.reshape(n_tok, D_SSD), scb, v, mod3, sc_conv_w[li], wo[:D_SSD], wo[D_SSD:],
                       g2, w_router[li].T, b_router[li][:, None])

    dest_t, gate_t, idx_t, meta = _route(lgt, n_blocks)
    dest_flat = dest_t.T.reshape(n_assign)
    cnt = meta[0, :N_EXPERTS]
    start = meta[1, :N_EXPERTS]
    blk_exp = meta[2, :n_blocks]
    n_used = meta[3, :1]

    meta_rows = _pad_lanes(jnp.concatenate(
        [idx_t.T.astype(F32), gate_t.T, jnp.arange(n_tok, dtype=F32)[:, None]], axis=1))
    pad_meta = _pad_lanes(jnp.concatenate(
        [jnp.full((MOE_BM, TOP_K), -1.0, F32), jnp.zeros((MOE_BM, TOP_K), F32),
         (TOP_K * n_tok + jnp.arange(MOE_BM, dtype=F32))[:, None]], axis=1))
    zsrc = jnp.concatenate([jnp.zeros((MOE_BM, D_MODEL), F32), pad_meta], axis=1)

    xs = _dispatch(dest_flat, cnt, start, n_used, x1, meta_rows, mod3, g2, zsrc, n_rows)
    ytm = _experts(blk_exp, n_used, xs, w_gate_up[li].astype(BF16), b_gate_up[li][:, None, :],
                   w_down[li].astype(BF16), b_down[li][:, None, :], n_tok)
    out = _combine(ytm, x1, mod3, final_g[None, :])
    return out.reshape(bsz, SEQ, D_MODEL)
```

```python
import functools

import jax
import jax.numpy as jnp
from jax import lax
from jax.experimental import pallas as pl
from jax.experimental.pallas import tpu as pltpu

F32 = jnp.float32
BF16 = jnp.bfloat16
HIGHEST = lax.Precision.HIGHEST

D_MODEL = 1024
SEQ = 2048
CTX_LEN = 256
GRID_W = 64
D_SSD = 1024
D_SC = 1024
HEAD_DIM = 64
HEADS = 16
GROUPS = 2
STATE = 128
CHUNK = 128
N_EXPERTS = 32
TOP_K = 4
D_FF = 1024
SWIGLU_LIMIT = 7.0
SWIGLU_ALPHA = 1.702
NORM_EPS = 1e-6
N_MOD = 6
XBC_W = D_SSD + 2 * GROUPS * STATE
XB_W = D_SSD + GROUPS * STATE
LANES = 128

Z0 = 0
X0 = Z0 + D_SSD
B0 = X0 + D_SSD
C0 = B0 + GROUPS * STATE
DT0 = C0 + GROUPS * STATE
SC0 = DT0 + 2 * HEADS

TOK_TILE = 512
MOE_BM = 256
RT_TILE = 512
DISP_TILE = 256
COMB_TILE = 512
ROW_W = D_MODEL + LANES
META_IDX = 0
META_GATE = TOP_K
META_TOK = 2 * TOP_K
VMEM_LIMIT = 56 * 1024 * 1024


def _silu(v):
    return v * jax.nn.sigmoid(v)


def _softplus(v):
    return jnp.maximum(v, 0.0) + jnp.log1p(jnp.exp(-jnp.abs(v)))


def _rms(v):
    return v * lax.rsqrt(jnp.mean(v * v, axis=-1, keepdims=True) + NORM_EPS)


def _dot(a, b):
    return jnp.dot(a, b, preferred_element_type=F32)


def _expand(v, e, pieces):
    acc = None
    rem = v
    for _ in range(pieces):
        p = rem.astype(BF16)
        rem = rem - p.astype(F32)
        t = _dot(p, e)
        acc = t if acc is None else acc + t
    return acc


def _mod_kernel(c_ref, w_ref, b_ref, o_ref):
    o_ref[...] = jnp.dot(_silu(c_ref[...]), w_ref[...], precision=HIGHEST,
                         preferred_element_type=F32) + b_ref[...]


def _mod(cvec, w_mod, b_mod):
    rows = cvec.shape[0]
    n = w_mod.shape[1]
    tn = 1536
    return pl.pallas_call(
        _mod_kernel,
        out_shape=jax.ShapeDtypeStruct((rows, n), F32),
        grid=(n // tn,),
        in_specs=[pl.BlockSpec((rows, D_MODEL), lambda j: (0, 0)),
                  pl.BlockSpec((D_MODEL, tn), lambda j: (0, j)),
                  pl.BlockSpec((1, tn), lambda j: (0, j))],
        out_specs=pl.BlockSpec((rows, tn), lambda j: (0, j)),
        compiler_params=pltpu.CompilerParams(dimension_semantics=("arbitrary",),
                                             vmem_limit_bytes=VMEM_LIMIT),
        name="mod",
    )(cvec, w_mod, b_mod)


def _ctx_kernel(ctx_ref, mod_ref, g1_ref, wxb_ref, wdt_ref, cw_ref, cb_ref, dtb_ref, alog_ref, e64_ref,
                h0_ref):
    L = CTX_LEN
    m = mod_ref[0]
    hc = _rms(ctx_ref[0]) * g1_ref[...] * (1.0 + m[1:2]) + m[0:1]
    hb = hc.astype(BF16)
    pxb = _dot(hb, wxb_ref[...])
    dtr = _dot(hb, wdt_ref[...])
    rowi = lax.broadcasted_iota(jnp.int32, (L, XB_W), 0)
    dn = jnp.where(rowi == 0, 0.0, pltpu.roll(pxb, 1, 0))
    up = jnp.where(rowi == L - 1, 0.0, pltpu.roll(pxb, L - 1, 0))
    cw = cw_ref[...]
    xb = _silu(cw[0:1] * dn + cw[1:2] * pxb + cw[2:3] * up + cb_ref[...])
    xs = xb[:, :D_SSD]
    bm = xb[:, D_SSD:].astype(BF16)
    dt = _softplus(dtr + dtb_ref[...])
    da = dt * (-jnp.exp(alog_ref[...]))
    ri = lax.broadcasted_iota(jnp.int32, (L, L), 0)
    ci = lax.broadcasted_iota(jnp.int32, (L, L), 1)
    for d in range(2):
        tri = (ci <= ri) if d == 0 else (ci >= ri)
        cum = jnp.dot(tri.astype(F32), da, precision=HIGHEST, preferred_element_type=F32)
        last = cum[L - 1:L] if d == 0 else cum[0:1]
        w_e = _expand(jnp.exp(last - cum) * dt, e64_ref[d], 2)
        xw = (xs * w_e).astype(BF16)
        for g in range(GROUPS):
            gw = D_SSD // GROUPS
            st = lax.dot_general(bm[:, g * STATE:(g + 1) * STATE], xw[:, g * gw:(g + 1) * gw],
                                 (((0,), (0,)), ((), ())), preferred_element_type=F32)
            h0_ref[0, d, :, g * gw:(g + 1) * gw] = st


def _ctx_states(ctx, mod3, g1, wxb, wdt, cw, cb, dtb, alog, e64):
    bsz = ctx.shape[0]
    mod_row = bsz
    const = lambda *shape: pl.BlockSpec(shape, lambda b: (0,) * len(shape))
    return pl.pallas_call(
        _ctx_kernel,
        out_shape=jax.ShapeDtypeStruct((bsz, 2, STATE, D_SSD), F32),
        grid=(bsz,),
        in_specs=[pl.BlockSpec((1, CTX_LEN, D_MODEL), lambda b: (b, 0, 0)),
                  pl.BlockSpec((1, N_MOD, D_MODEL), lambda b: (mod_row, 0, 0)),
                  const(1, D_MODEL), const(D_MODEL, XB_W), const(D_MODEL, LANES),
                  const(3, XB_W), const(1, XB_W), const(1, LANES), const(1, LANES),
                  const(2, LANES, D_SSD)],
        out_specs=pl.BlockSpec((1, 2, STATE, D_SSD), lambda b: (b, 0, 0, 0)),
        compiler_params=pltpu.CompilerParams(dimension_semantics=("arbitrary",),
                                             vmem_limit_bytes=VMEM_LIMIT),
        name="ctx_states",
    )(ctx, mod3, g1, wxb, wdt, cw, cb, dtb, alog, e64)


def _inproj_kernel(x_ref, mod_ref, g1_ref, wz_ref, wxbc_ref, wdt_ref, wb_ref, wc_ref, wu_ref,
                   z_ref, xbc_ref, dt_ref, scb_ref, v_ref):
    m = mod_ref[0]
    hx = _rms(x_ref[...]) * g1_ref[...] * (1.0 + m[1:2]) + m[0:1]
    hb = hx.astype(BF16)
    z_ref[...] = _dot(hb, wz_ref[...]).astype(BF16)
    xbc_ref[...] = _dot(hb, wxbc_ref[...]).astype(BF16)
    dt_ref[...] = _dot(hb, wdt_ref[...])
    scb_ref[...] = _dot(hb, wb_ref[...]).astype(BF16)
    v_ref[...] = (_dot(hb, wc_ref[...]) * _dot(hb, wu_ref[...])).astype(BF16)


def _inproj(x2, mod3, g1, wz, wxbc, wdt, wb, wc, wu):
    t = x2.shape[0]
    tm = TOK_TILE
    per_b = SEQ // tm
    const = lambda *shape: pl.BlockSpec(shape, lambda i: (0,) * len(shape))
    tile = lambda w: pl.BlockSpec((tm, w), lambda i: (i, 0))
    return pl.pallas_call(
        _inproj_kernel,
        out_shape=(jax.ShapeDtypeStruct((t, D_SSD), BF16), jax.ShapeDtypeStruct((t, XBC_W), BF16),
                   jax.ShapeDtypeStruct((t, LANES), F32), jax.ShapeDtypeStruct((t, D_SC), BF16),
                   jax.ShapeDtypeStruct((t, D_SC), BF16)),
        grid=(t // tm,),
        in_specs=[tile(D_MODEL),
                  pl.BlockSpec((1, N_MOD, D_MODEL), lambda i: (i // per_b, 0, 0)),
                  const(1, D_MODEL), const(D_MODEL, D_SSD), const(D_MODEL, XBC_W), const(D_MODEL, LANES),
                  const(D_MODEL, D_SC), const(D_MODEL, D_SC), const(D_MODEL, D_SC)],
        out_specs=(tile(D_SSD), tile(XBC_W), tile(LANES), tile(D_SC), tile(D_SC)),
        compiler_params=pltpu.CompilerParams(dimension_semantics=("arbitrary",),
                                             vmem_limit_bytes=VMEM_LIMIT),
        name="inproj",
    )(x2, mod3, g1, wz, wxbc, wdt, wb, wc, wu)


def _ssd_kernel(xbc_ref, z_ref, dt_ref, h0_ref, cw_ref, cb_ref, dtb_ref, alog_ref, dsk_ref, g_ref,
                e64_ref, e128_ref, o_ref, xc_ref, y_ref, s_ref):
    Q = CHUNK
    nck = SEQ // Q
    gw = D_SSD // GROUPS

    rowi = lax.broadcasted_iota(jnp.int32, (Q, XBC_W), 0)

    def conv_body(c, carry):
        r0 = pl.multiple_of(c * Q, Q)
        main = xbc_ref[0, pl.ds(r0, Q), :].astype(F32)
        pstart = pl.multiple_of(jnp.maximum(r0 - 16, 0), 16)
        nstart = pl.multiple_of(jnp.minimum(r0 + Q, SEQ - 16), 16)
        prev = xbc_ref[0, pl.ds(pstart, 16), :].astype(F32)[15:16]
        nxt = xbc_ref[0, pl.ds(nstart, 16), :].astype(F32)[0:1]
        prev = jnp.where(c > 0, prev, 0.0)
        nxt = jnp.where(c < nck - 1, nxt, 0.0)
        dn = jnp.where(rowi == 0, prev, pltpu.roll(main, 1, 0))
        up = jnp.where(rowi == Q - 1, nxt, pltpu.roll(main, Q - 1, 0))
        cw = cw_ref[...]
        conv = cw[0:1] * dn + cw[1:2] * main + cw[2:3] * up + cb_ref[...]
        xc_ref[pl.ds(r0, Q), :] = _silu(conv).astype(BF16)
        return carry

    lax.fori_loop(0, nck, conv_body, 0)

    ri = lax.broadcasted_iota(jnp.int32, (Q, Q), 0)
    ci = lax.broadcasted_iota(jnp.int32, (Q, Q), 1)
    lane = lax.broadcasted_iota(jnp.int32, (Q, LANES), 1)
    a_neg = -jnp.exp(alog_ref[...])

    def chunk(c, d):
        r0 = pl.multiple_of(c * Q, Q)
        rows = pl.ds(r0, Q)
        xs = xc_ref[rows, 0:D_SSD].astype(F32)
        bm = xc_ref[rows, D_SSD:D_SSD + GROUPS * STATE]
        cm = xc_ref[rows, D_SSD + GROUPS * STATE:XBC_W]
        dt = _softplus(dt_ref[0, rows, :] + dtb_ref[...])
        da = dt * a_neg
        tri = (ci <= ri) if d == 0 else (ci >= ri)
        cum = jnp.dot(tri.astype(F32), da, precision=HIGHEST, preferred_element_type=F32)
        cum_t = cum.T
        last = cum[Q - 1:Q] if d == 0 else cum[0:1]
        ecum_e = _expand(jnp.exp(cum), e64_ref[d], 2)
        dt_e = _expand(dt, e64_ref[d], 2)
        w_e = _expand(jnp.exp(last - cum) * dt, e64_ref[d], 2)
        colb = _expand(cum, e128_ref[d], 3)
        decay_e = ecum_e[Q - 1:Q] if d == 0 else ecum_e[0:1]

        gmat = [lax.dot_general(cm[:, g * STATE:(g + 1) * STATE], bm[:, g * STATE:(g + 1) * STATE],
                                (((1,), (1,)), ((), ())), preferred_element_type=F32)
                for g in range(GROUPS)]
        xdt = xs * dt_e
        y_parts = []
        for p in range(HEADS // 2):
            g = (2 * p) // (HEADS // GROUPS)
            ms = []
            for hh in (2 * p, 2 * p + 1):
                seg = colb[:, hh * LANES:(hh + 1) * LANES] - cum_t[HEADS * d + hh:HEADS * d + hh + 1, :]
                mm = jnp.where(tri, jnp.exp(jnp.where(tri, seg, 0.0)), 0.0) * gmat[g]
                ms.append(mm.astype(BF16))
            mcat = jnp.concatenate(ms, axis=1)
            xp = xdt[:, p * LANES:(p + 1) * LANES]
            rhs = jnp.concatenate([jnp.where(lane < HEAD_DIM, xp, 0.0).astype(BF16),
                                   jnp.where(lane >= HEAD_DIM, xp, 0.0).astype(BF16)], axis=0)
            y_parts.append(_dot(mcat, rhs))
        y_diag = jnp.concatenate(y_parts, axis=1)

        s_old = s_ref[...]
        s_bf = s_old.astype(BF16)
        y_off = jnp.concatenate(
            [_dot(cm[:, g * STATE:(g + 1) * STATE], s_bf[:, g * gw:(g + 1) * gw]) for g in range(GROUPS)],
            axis=1)
        y = y_diag + y_off * ecum_e

        xw = (xs * w_e).astype(BF16)
        upd = jnp.concatenate(
            [lax.dot_general(bm[:, g * STATE:(g + 1) * STATE], xw[:, g * gw:(g + 1) * gw],
                             (((0,), (0,)), ((), ())), preferred_element_type=F32) for g in range(GROUPS)],
            axis=1)
        s_ref[...] = s_old * decay_e + upd

        if d == 0:
            y_ref[rows, :] = y + dsk_ref[...] * xs
        else:
            tot = y_ref[rows, :] + y
            zz = z_ref[0, rows, :].astype(F32)
            gz = tot * _silu(zz)
            outs = []
            for g in range(GROUPS):
                gg = gz[:, g * gw:(g + 1) * gw]
                outs.append(gg * lax.rsqrt(jnp.mean(gg * gg, axis=-1, keepdims=True) + NORM_EPS))
            o_ref[0, rows, :] = (jnp.concatenate(outs, axis=1) * g_ref[...]).astype(BF16)

    s_ref[...] = h0_ref[0, 0]

    def fwd_body(i, carry):
        chunk(i, 0)
        return carry

    lax.fori_loop(0, nck, fwd_body, 0)

    s_ref[...] = h0_ref[0, 1]

    def bwd_body(i, carry):
        chunk(nck - 1 - i, 1)
        return carry

    lax.fori_loop(0, nck, bwd_body, 0)


def _ssd(xbc3, z3, dt3, h0, cw, cb, dtb, alog, dsk, g, e64, e128):
    bsz = xbc3.shape[0]
    const = lambda *shape: pl.BlockSpec(shape, lambda b: (0,) * len(shape))
    seq = lambda w: pl.BlockSpec((1, SEQ, w), lambda b: (b, 0, 0))
    return pl.pallas_call(
        _ssd_kernel,
        out_shape=jax.ShapeDtypeStruct((bsz, SEQ, D_SSD), BF16),
        grid=(bsz,),
        in_specs=[seq(XBC_W), seq(D_SSD), seq(LANES),
                  pl.BlockSpec((1, 2, STATE, D_SSD), lambda b: (b, 0, 0, 0)),
                  const(3, XBC_W), const(1, XBC_W), const(1, LANES), const(1, LANES),
                  const(1, D_SSD), const(1, D_SSD), const(2, LANES, D_SSD), const(2, LANES, HEADS * LANES)],
        out_specs=seq(D_SSD),
        scratch_shapes=[pltpu.VMEM((SEQ, XBC_W), BF16), pltpu.VMEM((SEQ, D_SSD), F32),
                        pltpu.VMEM((STATE, D_SSD), F32)],
        compiler_params=pltpu.CompilerParams(dimension_semantics=("arbitrary",),
                                             vmem_limit_bytes=VMEM_LIMIT),
        name="ssd",
    )(xbc3, z3, dt3, h0, cw, cb, dtb, alog, dsk, g, e64, e128)


def _outproj_kernel(x_ref, yssd_ref, scb_ref, v_ref, vp_ref, vn_ref, mod_ref, scw_ref, wo1_ref, wo2_ref,
                    g2_ref, wrt_ref, br_ref, x1_ref, lg_ref):
    tm = TOK_TILE
    per_b = SEQ // tm
    i = pl.program_id(0)
    first = (i % per_b) == 0
    last = (i % per_b) == per_b - 1
    m = mod_ref[0]
    v = v_ref[...].astype(F32)
    vp = jnp.where(first, 0.0, vp_ref[...].astype(F32))
    vn = jnp.where(last, 0.0, vn_ref[...].astype(F32))
    dn = jnp.concatenate([vp, v[:tm - GRID_W]], axis=0)
    up = jnp.concatenate([v[GRID_W:], vn], axis=0)
    scw = scw_ref[...]
    ysc = scb_ref[...].astype(F32) * (scw[0:1] * dn + scw[1:2] * v + scw[2:3] * up)
    out = _dot(yssd_ref[...], wo1_ref[...]) + _dot(ysc.astype(BF16), wo2_ref[...])
    x1 = x_ref[...] + m[2:3] * out
    x1_ref[...] = x1
    h2 = _rms(x1) * g2_ref[...] * (1.0 + m[4:5]) + m[3:4]
    lg_ref[...] = lax.dot_general(wrt_ref[...], h2, (((1,), (1,)), ((), ())), precision=HIGHEST,
                                  preferred_element_type=F32) + br_ref[...]


def _outproj(x2, yssd, scb, v, mod3, scw, wo1, wo2, g2, wrt, br):
    t = x2.shape[0]
    tm = TOK_TILE
    per_b = SEQ // tm
    r = tm // GRID_W
    nrow = t // GRID_W
    const = lambda *shape: pl.BlockSpec(shape, lambda i: (0,) * len(shape))
    tile = lambda w: pl.BlockSpec((tm, w), lambda i: (i, 0))
    return pl.pallas_call(
        _outproj_kernel,
        out_shape=(jax.ShapeDtypeStruct((t, D_MODEL), F32), jax.ShapeDtypeStruct((N_EXPERTS, t), F32)),
        grid=(t // tm,),
        in_specs=[tile(D_MODEL), tile(D_SSD), tile(D_SC), tile(D_SC),
                  pl.BlockSpec((GRID_W, D_SC), lambda i: (jnp.maximum(i * r - 1, 0), 0)),
                  pl.BlockSpec((GRID_W, D_SC), lambda i: (jnp.minimum((i + 1) * r, nrow - 1), 0)),
                  pl.BlockSpec((1, N_MOD, D_MODEL), lambda i: (i // per_b, 0, 0)),
                  const(3, D_SC), const(D_SSD, D_MODEL), const(D_SC, D_MODEL), const(1, D_MODEL),
                  const(N_EXPERTS, D_MODEL), const(N_EXPERTS, 1)],
        out_specs=(tile(D_MODEL), pl.BlockSpec((N_EXPERTS, tm), lambda i: (0, i))),
        compiler_params=pltpu.CompilerParams(dimension_semantics=("arbitrary",),
                                             vmem_limit_bytes=VMEM_LIMIT),
        name="outproj",
    )(x2, yssd, scb, v, v, v, mod3, scw, wo1, wo2, g2, wrt, br)


def _route_kernel(lg_ref, dest_ref, gate_ref, idx_ref, meta_ref, rank_ref, carry_ref, *, n_tok, n_blocks):
    tt = RT_TILE
    ne = N_EXPERTS
    eio = lax.broadcasted_iota(jnp.int32, (ne, tt), 0)
    si = lax.broadcasted_iota(jnp.int32, (tt, tt), 0)
    ti = lax.broadcasted_iota(jnp.int32, (tt, tt), 1)
    before = (si < ti).astype(BF16)
    carry_ref[...] = jnp.zeros_like(carry_ref)

    def tile_body(j, c):
        t0 = pl.multiple_of(j * tt, tt)
        l = lg_ref[:, pl.ds(t0, tt)]
        onehot = jnp.zeros((ne, tt), F32)
        tops, sels = [], []
        for _ in range(TOP_K):
            mx = jnp.max(l, axis=0, keepdims=True)
            idx = jnp.min(jnp.where(l == mx, eio, ne), axis=0, keepdims=True)
            sel = eio == idx
            l = jnp.where(sel, -jnp.inf, l)
            onehot = onehot + sel.astype(F32)
            tops.append(mx)
            sels.append(sel)
            idx_ref[pl.ds(len(tops) - 1, 1), pl.ds(t0, tt)] = idx
        ex = [jnp.exp(tv - tops[0]) for tv in tops]
        den = ex[0] + ex[1] + ex[2] + ex[3]
        prefix = _dot(onehot.astype(BF16), before) + carry_ref[:, 0:1]
        for k in range(TOP_K):
            gate_ref[pl.ds(k, 1), pl.ds(t0, tt)] = ex[k] / den
            rk = jnp.sum(jnp.where(sels[k], prefix, 0.0), axis=0, keepdims=True)
            rank_ref[pl.ds(k, 1), pl.ds(t0, tt)] = rk.astype(jnp.int32)
        carry_ref[...] = carry_ref[...] + jnp.sum(onehot, axis=1, keepdims=True)
        return c

    lax.fori_loop(0, n_tok // tt, tile_body, 0)

    counts = carry_ref[...]
    padded = jnp.floor((counts + (MOE_BM - 1)) * (1.0 / MOE_BM)) * MOE_BM
    er = lax.broadcasted_iota(jnp.int32, (ne, ne), 0)
    ec = lax.broadcasted_iota(jnp.int32, (ne, ne), 1)
    pad_start = jnp.dot((ec < er).astype(F32), padded, precision=HIGHEST, preferred_element_type=F32)
    pad_end = pad_start + padded

    def dest_body(j, c):
        t0 = pl.multiple_of(j * tt, tt)
        for k in range(TOP_K):
            idx = idx_ref[pl.ds(k, 1), pl.ds(t0, tt)]
            base = jnp.sum(jnp.where(eio == idx, pad_start[:, 0:1], 0.0), axis=0, keepdims=True)
            dest_ref[pl.ds(k, 1), pl.ds(t0, tt)] = base.astype(jnp.int32) + rank_ref[pl.ds(k, 1), pl.ds(t0, tt)]
        return c

    lax.fori_loop(0, n_tok // tt, dest_body, 0)

    width = meta_ref.shape[1]
    sub = lax.broadcasted_iota(jnp.int32, (ne, width), 0)
    lan = lax.broadcasted_iota(jnp.int32, (ne, width), 1)
    diag = sub == lan
    cnt_row = jnp.sum(jnp.where(diag, counts[:, 0:1], 0.0), axis=0, keepdims=True)
    start_row = jnp.sum(jnp.where(diag, pad_start[:, 0:1], 0.0), axis=0, keepdims=True)
    blk_start = (lan * MOE_BM).astype(F32)
    blk_exp = jnp.sum((pad_end[:, 0:1] <= blk_start).astype(F32), axis=0, keepdims=True)
    blk_exp = jnp.minimum(blk_exp, float(ne - 1))
    used = jnp.sum(padded[:, 0:1], axis=0, keepdims=True) * (1.0 / MOE_BM)
    meta_ref[0:1, :] = cnt_row.astype(jnp.int32)
    meta_ref[1:2, :] = start_row.astype(jnp.int32)
    meta_ref[2:3, :] = blk_exp.astype(jnp.int32)
    meta_ref[3:4, :] = jnp.broadcast_to(used, (1, width)).astype(jnp.int32)
    meta_ref[4:8, :] = jnp.zeros((4, width), jnp.int32)


def _route(lgt, n_blocks):
    ne, n_tok = lgt.shape
    width = -(-n_blocks // LANES) * LANES
    full = lambda *shape: pl.BlockSpec(shape, lambda: (0,) * len(shape))
    return pl.pallas_call(
        functools.partial(_route_kernel, n_tok=n_tok, n_blocks=n_blocks),
        out_shape=(jax.ShapeDtypeStruct((TOP_K, n_tok), jnp.int32),
                   jax.ShapeDtypeStruct((TOP_K, n_tok), F32),
                   jax.ShapeDtypeStruct((TOP_K, n_tok), jnp.int32),
                   jax.ShapeDtypeStruct((8, width), jnp.int32)),
        in_specs=[full(ne, n_tok)],
        out_specs=(full(TOP_K, n_tok), full(TOP_K, n_tok), full(TOP_K, n_tok), full(8, width)),
        scratch_shapes=[pltpu.VMEM((TOP_K, n_tok), jnp.int32), pltpu.VMEM((ne, LANES), F32)],
        compiler_params=pltpu.CompilerParams(vmem_limit_bytes=VMEM_LIMIT),
        name="route",
    )(lgt)


def _dispatch_kernel(dest_ref, cnt_ref, start_ref, nu_ref, x1_ref, meta_ref, mod_ref, g2_ref, zsrc_ref, xs_ref,
                     hbuf, sem, zsem):
    i = pl.program_id(0)
    n = pl.num_programs(0)
    tl = DISP_TILE
    slot = i % 2
    nb = xs_ref.shape[0] // MOE_BM

    def zero_block(b):
        return pltpu.make_async_copy(zsrc_ref, xs_ref.at[pl.ds(b * MOE_BM, MOE_BM)], zsem)

    @pl.when(i == 0)
    def _():
        def start_e(e, c):
            @pl.when(cnt_ref[e] > 0)
            def _():
                zero_block((start_ref[e] + cnt_ref[e] - 1) // MOE_BM).start()
            return c

        def wait_e(e, c):
            @pl.when(cnt_ref[e] > 0)
            def _():
                zero_block(0).wait()
            return c

        def start_t(b, c):
            zero_block(b).start()
            return c

        def wait_t(b, c):
            zero_block(0).wait()
            return c

        lax.fori_loop(0, N_EXPERTS, start_e, 0)
        lax.fori_loop(nu_ref[0], nb, start_t, 0)
        lax.fori_loop(0, N_EXPERTS, wait_e, 0)
        lax.fori_loop(nu_ref[0], nb, wait_t, 0)

    m = mod_ref[0]
    hbuf[slot, :, 0:D_MODEL] = _rms(x1_ref[...]) * g2_ref[...] * (1.0 + m[4:5]) + m[3:4]
    hbuf[slot, :, D_MODEL:ROW_W] = meta_ref[...]

    def per_tok(t, c):
        tok = i * tl + t
        for k in range(TOP_K):
            d = dest_ref[tok * TOP_K + k]
            pltpu.make_async_copy(hbuf.at[slot, pl.ds(t, 1)], xs_ref.at[pl.ds(d, 1)], sem.at[slot]).start()
        return c

    lax.fori_loop(0, tl, per_tok, 0)

    def wait_slot(sl):
        for _ in range(TOP_K):
            pltpu.make_async_copy(hbuf.at[sl], xs_ref.at[pl.ds(0, tl)], sem.at[sl]).wait()

    @pl.when(i > 0)
    def _():
        wait_slot(1 - slot)

    @pl.when(i == n - 1)
    def _():
        wait_slot(slot)


def _dispatch(dest_flat, cnt, start, n_used, x1, meta_rows, mod3, g2, zsrc, n_rows):
    n_tok = x1.shape[0]
    tl = DISP_TILE
    per_b = SEQ // tl
    return pl.pallas_call(
        _dispatch_kernel,
        out_shape=jax.ShapeDtypeStruct((n_rows, ROW_W), F32),
        grid_spec=pltpu.PrefetchScalarGridSpec(
            num_scalar_prefetch=4,
            grid=(n_tok // tl,),
            in_specs=[pl.BlockSpec((tl, D_MODEL), lambda i, *_: (i, 0)),
                      pl.BlockSpec((tl, LANES), lambda i, *_: (i, 0)),
                      pl.BlockSpec((1, N_MOD, D_MODEL), lambda i, *_: (i // per_b, 0, 0)),
                      pl.BlockSpec((1, D_MODEL), lambda i, *_: (0, 0)),
                      pl.BlockSpec((MOE_BM, ROW_W), lambda i, *_: (0, 0))],
            out_specs=pl.BlockSpec(memory_space=pl.ANY),
            scratch_shapes=[pltpu.VMEM((2, tl, ROW_W), F32),
                            pltpu.SemaphoreType.DMA((2,)), pltpu.SemaphoreType.DMA]),
        compiler_params=pltpu.CompilerParams(dimension_semantics=("arbitrary",),
                                             vmem_limit_bytes=VMEM_LIMIT),
        name="dispatch",
    )(dest_flat, cnt, start, n_used, x1, meta_rows, mod3, g2, zsrc)


def _expert_kernel(be_ref, nu_ref, xs_ref, wgu_ref, bgu_ref, wd_ref, bd_ref, zero_ref, ytm_ref,
                   ybuf, idv, ids, sc_sem, id_sem, z_sem, *, n_tok):
    j = pl.program_id(0)
    nbk = pl.num_programs(0) - 1
    nu = nu_ref[0]
    slot = j % 2
    prev = 1 - slot

    @pl.when(j == 0)
    def _():
        cp = pltpu.make_async_copy(zero_ref, ytm_ref.at[pl.ds(TOP_K * n_tok, MOE_BM)], z_sem)
        cp.start()
        cp.wait()

    n_pc = 4
    pw1 = D_FF // n_pc
    pw2 = D_MODEL // n_pc
    groups = [48] * n_pc + [16] * n_pc
    assert sum(groups) == MOE_BM

    def scatter_group(g):
        lo = sum(groups[:g])
        for r in range(lo, lo + groups[g]):
            pltpu.make_async_copy(ybuf.at[prev, pl.ds(r, 1)], ytm_ref.at[pl.ds(ids[prev, 0, r], 1)],
                                  sc_sem.at[prev]).start()

    def scatter_wait(sl):
        pltpu.make_async_copy(ybuf.at[sl], ytm_ref.at[pl.ds(0, MOE_BM)], sc_sem.at[sl]).wait()

    def compute(with_scatter, wait_older):
        e_f = be_ref[jnp.minimum(j, nbk - 1)].astype(F32)
        xb = xs_ref[:, 0:D_MODEL].astype(BF16)
        meta = xs_ref[:, D_MODEL:ROW_W]
        gate = jnp.zeros((MOE_BM, 1), F32)
        kk = jnp.zeros((MOE_BM, 1), F32)
        for k in range(TOP_K):
            mk = meta[:, META_IDX + k:META_IDX + k + 1] == e_f
            gate = gate + jnp.where(mk, meta[:, META_GATE + k:META_GATE + k + 1], 0.0)
            kk = kk + jnp.where(mk, float(k), 0.0)
        for c in range(n_pc):
            if with_scatter:
                scatter_group(c)
            cg = slice(c * pw1, (c + 1) * pw1)
            cl = slice(D_FF + c * pw1, D_FF + (c + 1) * pw1)
            glu = jnp.minimum(_dot(xb, wgu_ref[0, :, cg]) + bgu_ref[0, :, cg], SWIGLU_LIMIT)
            lin = jnp.clip(_dot(xb, wgu_ref[0, :, cl]) + bgu_ref[0, :, cl], -SWIGLU_LIMIT, SWIGLU_LIMIT)
            ybuf[2, :, cg] = glu * jax.nn.sigmoid(SWIGLU_ALPHA * glu) * (lin + 1.0)
        act = ybuf[2].astype(BF16)
        if wait_older:
            scatter_wait(slot)
        for c in range(n_pc):
            if with_scatter:
                scatter_group(n_pc + c)
            cs = slice(c * pw2, (c + 1) * pw2)
            ybuf[slot, :, cs] = (_dot(act, wd_ref[0, :, cs]) + bd_ref[0, :, cs]) * gate
        row = kk * float(n_tok) + meta[:, META_TOK:META_TOK + 1]
        row_t = jnp.broadcast_to(row, (MOE_BM, LANES)).T
        idv[slot] = row_t[0:8].astype(jnp.int32)
        pltpu.make_async_copy(idv.at[slot], ids.at[slot], id_sem.at[slot]).start()

    def ids_wait():
        pltpu.make_async_copy(idv.at[prev], ids.at[prev], id_sem.at[prev]).wait()

    @pl.when(j == 0)
    def _():
        compute(False, False)

    @pl.when(jnp.logical_and(j == 1, j < nu))
    def _():
        ids_wait()
        compute(True, False)

    @pl.when(jnp.logical_and(j >= 2, j < nu))
    def _():
        ids_wait()
        compute(True, True)

    @pl.when(j == nu)
    def _():
        ids_wait()
        for g in range(len(groups)):
            scatter_group(g)
        scatter_wait(prev)

    @pl.when(jnp.logical_and(j == nu, j >= 2))
    def _():
        scatter_wait(slot)


def _experts(blk_exp, n_used, xs, wgu, bgu, wd, bd, n_tok):
    n_rows = xs.shape[0]
    nb = n_rows // MOE_BM
    row_blk = lambda j, be, nu: (jnp.minimum(j, nu[0] - 1), 0)
    per_e = lambda j, be, nu: (be[jnp.minimum(j, nb - 1)], 0, 0)
    zero = jnp.zeros((MOE_BM, D_MODEL), F32)
    return pl.pallas_call(
        functools.partial(_expert_kernel, n_tok=n_tok),
        out_shape=jax.ShapeDtypeStruct((TOP_K * n_tok + MOE_BM, D_MODEL), F32),
        grid_spec=pltpu.PrefetchScalarGridSpec(
            num_scalar_prefetch=2,
            grid=(nb + 1,),
            in_specs=[pl.BlockSpec((MOE_BM, ROW_W), row_blk),
                      pl.BlockSpec((1, D_MODEL, 2 * D_FF), per_e),
                      pl.BlockSpec((1, 1, 2 * D_FF), per_e),
                      pl.BlockSpec((1, D_FF, D_MODEL), per_e),
                      pl.BlockSpec((1, 1, D_MODEL), per_e),
                      pl.BlockSpec((MOE_BM, D_MODEL), lambda j, be, nu: (0, 0))],
            out_specs=pl.BlockSpec(memory_space=pl.ANY),
            scratch_shapes=[pltpu.VMEM((3, MOE_BM, D_MODEL), F32), pltpu.VMEM((2, 8, MOE_BM), jnp.int32),
                            pltpu.SMEM((2, 8, MOE_BM), jnp.int32), pltpu.SemaphoreType.DMA((2,)),
                            pltpu.SemaphoreType.DMA((2,)), pltpu.SemaphoreType.DMA]),
        compiler_params=pltpu.CompilerParams(dimension_semantics=("arbitrary",),
                                             vmem_limit_bytes=VMEM_LIMIT),
        name="experts",
    )(blk_exp, n_used, xs, wgu, bgu, wd, bd, zero)


def _combine_kernel(y0_ref, y1_ref, y2_ref, y3_ref, x1_ref, mod_ref, fg_ref, o_ref):
    m = mod_ref[0]
    moe = (y0_ref[...] + y1_ref[...]) + (y2_ref[...] + y3_ref[...])
    x2 = x1_ref[...] + m[5:6] * moe
    o_ref[...] = _rms(x2) * fg_ref[...]


def _combine(ytm, x1, mod3, fg):
    n_tok = x1.shape[0]
    tc = COMB_TILE
    per_b = SEQ // tc
    nt = n_tok // tc
    slot_spec = lambda k: pl.BlockSpec((tc, D_MODEL), lambda i: (i + k * nt, 0))
    return pl.pallas_call(
        _combine_kernel,
        out_shape=jax.ShapeDtypeStruct((n_tok, D_MODEL), F32),
        grid=(nt,),
        in_specs=[slot_spec(0), slot_spec(1), slot_spec(2), slot_spec(3),
                  pl.BlockSpec((tc, D_MODEL), lambda i: (i, 0)),
                  pl.BlockSpec((1, N_MOD, D_MODEL), lambda i: (i // per_b, 0, 0)),
                  pl.BlockSpec((1, D_MODEL), lambda i: (0, 0))],
        out_specs=pl.BlockSpec((tc, D_MODEL), lambda i: (i, 0)),
        compiler_params=pltpu.CompilerParams(dimension_semantics=("arbitrary",),
                                             vmem_limit_bytes=VMEM_LIMIT),
        name="combine",
    )(ytm, ytm, ytm, ytm, x1, mod3, fg)


def _expansion_matrices():
    r = jnp.arange(LANES)[:, None]
    out64, out128 = [], []
    for d in range(2):
        l64 = jnp.arange(D_SSD)[None, :]
        l128 = jnp.arange(HEADS * LANES)[None, :]
        out64.append((l64 // HEAD_DIM == r - HEADS * d).astype(BF16))
        out128.append((l128 // LANES == r - HEADS * d).astype(BF16))
    return jnp.stack(out64), jnp.stack(out128)


def _pad_lanes(v):
    return jnp.pad(v, [(0, 0)] * (v.ndim - 1) + [(0, LANES - v.shape[-1])])


def kernel(x, c, ctx, c_ctx, w_mod, b_mod, norm1_g, w_in, ssd_conv_w, ssd_conv_b, ssd_dt_bias, ssd_a_log,
           ssd_d, ssd_norm_g, sc_conv_w, w_out, norm2_g, w_router, b_router, w_gate_up, b_gate_up, w_down,
           b_down, final_g):
    bsz = x.shape[0]
    n_tok = bsz * SEQ
    n_assign = n_tok * TOP_K
    n_blocks = n_assign // MOE_BM + N_EXPERTS
    n_rows = n_blocks * MOE_BM
    li = 0

    cvec = jnp.concatenate([c, c_ctx[None, :], jnp.zeros((7, D_MODEL), F32)], axis=0)
    mod3 = _mod(cvec, w_mod[li], b_mod[li][None, :]).reshape(bsz + 8, N_MOD, D_MODEL)

    w = w_in[li]
    wz = w[:, Z0:X0].astype(BF16)
    wxbc = w[:, X0:DT0].astype(BF16)
    wdt = _pad_lanes(w[:, DT0:SC0]).astype(BF16)
    wb = w[:, SC0:SC0 + D_SC].astype(BF16)
    wc = w[:, SC0 + D_SC:SC0 + 2 * D_SC].astype(BF16)
    wu = w[:, SC0 + 2 * D_SC:].astype(BF16)
    g1 = norm1_g[li][None, :]
    cw = ssd_conv_w[li]
    cb = ssd_conv_b[li][None, :]
    dtb = _pad_lanes(ssd_dt_bias[li].reshape(1, 2 * HEADS))
    alog = _pad_lanes(ssd_a_log[li].reshape(1, 2 * HEADS))
    e64, e128 = _expansion_matrices()

    h0 = _ctx_states(ctx, mod3, g1, wxbc[:, :XB_W], wdt, cw[:, :XB_W], cb[:, :XB_W], dtb, alog, e64)

    x2 = x.reshape(n_tok, D_MODEL)
    z, xbc, dtr, scb, v = _inproj(x2, mod3, g1, wz, wxbc, wdt, wb, wc, wu)

    dsk = jnp.repeat(ssd_d[li], HEAD_DIM)[None, :]
    yssd = _ssd(xbc.reshape(bsz, SEQ, XBC_W), z.reshape(bsz, SEQ, D_SSD), dtr.reshape(bsz, SEQ, LANES), h0,
                cw, cb, dtb, alog, dsk, ssd_norm_g[li][None, :], e64, e128)

    wo = w_out[li].astype(BF16)
    g2 = norm2_g[li][None, :]
    x1, lgt = _outproj(x2, yssd.reshape(n_tok, D_SSD), scb, v, mod3, sc_conv_w[li], wo[:D_SSD], wo[D_SSD:],
                       g2, w_router[li].T, b_router[li][:, None])

    dest_t, gate_t, idx_t, meta = _route(lgt, n_blocks)
    dest_flat = dest_t.T.reshape(n_assign)
    cnt = meta[0, :N_EXPERTS]
    start = meta[1, :N_EXPERTS]
    blk_exp = meta[2, :n_blocks]
    n_used = meta[3, :1]

    meta_rows = _pad_lanes(jnp.concatenate(
        [idx_t.T.astype(F32), gate_t.T, jnp.arange(n_tok, dtype=F32)[:, None]], axis=1))
    pad_meta = _pad_lanes(jnp.concatenate(
        [jnp.full((MOE_BM, TOP_K), -1.0, F32), jnp.zeros((MOE_BM, TOP_K), F32),
         (TOP_K * n_tok + jnp.arange(MOE_BM, dtype=F32))[:, None]], axis=1))
    zsrc = jnp.concatenate([jnp.zeros((MOE_BM, D_MODEL), F32), pad_meta], axis=1)

    xs = _dispatch(dest_flat, cnt, start, n_used, x1, meta_rows, mod3, g2, zsrc, n_rows)
    ytm = _experts(blk_exp, n_used, xs, w_gate_up[li].astype(BF16), b_gate_up[li][:, None, :],
                   w_down[li].astype(BF16), b_down[li][:, None, :], n_tok)
    out = _combine(ytm, x1, mod3, final_g[None, :])
    return out.reshape(bsz, SEQ, D_MODEL)
```

```python
import functools

import jax
import jax.numpy as jnp
from jax import lax
from jax.experimental import pallas as pl
from jax.experimental.pallas import tpu as pltpu

F32 = jnp.float32
BF16 = jnp.bfloat16
HIGHEST = lax.Precision.HIGHEST

D_MODEL = 1024
SEQ = 2048
CTX_LEN = 256
GRID_W = 64
D_SSD = 1024
D_SC = 1024
HEAD_DIM = 64
HEADS = 16
GROUPS = 2
STATE = 128
CHUNK = 128
N_EXPERTS = 32
TOP_K = 4
D_FF = 1024
SWIGLU_LIMIT = 7.0
SWIGLU_ALPHA = 1.702
NORM_EPS = 1e-6
N_MOD = 6
XBC_W = D_SSD + 2 * GROUPS * STATE
XB_W = D_SSD + GROUPS * STATE
LANES = 128

Z0 = 0
X0 = Z0 + D_SSD
B0 = X0 + D_SSD
C0 = B0 + GROUPS * STATE
DT0 = C0 + GROUPS * STATE
SC0 = DT0 + 2 * HEADS

TOK_TILE = 512
MOE_BM = 256
RT_TILE = 512
DISP_TILE = 256
COMB_TILE = 512
ROW_W = D_MODEL + LANES
META_IDX = 0
META_GATE = TOP_K
META_TOK = 2 * TOP_K
VMEM_LIMIT = 56 * 1024 * 1024


def _silu(v):
    return v * jax.nn.sigmoid(v)


def _softplus(v):
    return jnp.maximum(v, 0.0) + jnp.log1p(jnp.exp(-jnp.abs(v)))


def _rms(v):
    return v * lax.rsqrt(jnp.mean(v * v, axis=-1, keepdims=True) + NORM_EPS)


def _dot(a, b):
    return jnp.dot(a, b, preferred_element_type=F32)


def _expand(v, e, pieces):
    acc = None
    rem = v
    for _ in range(pieces):
        p = rem.astype(BF16)
        rem = rem - p.astype(F32)
        t = _dot(p, e)
        acc = t if acc is None else acc + t
    return acc


def _mod_kernel(c_ref, w_ref, b_ref, o_ref):
    o_ref[...] = jnp.dot(_silu(c_ref[...]), w_ref[...], precision=HIGHEST,
                         preferred_element_type=F32) + b_ref[...]


def _mod(cvec, w_mod, b_mod):
    rows = cvec.shape[0]
    n = w_mod.shape[1]
    tn = 1536
    return pl.pallas_call(
        _mod_kernel,
        out_shape=jax.ShapeDtypeStruct((rows, n), F32),
        grid=(n // tn,),
        in_specs=[pl.BlockSpec((rows, D_MODEL), lambda j: (0, 0)),
                  pl.BlockSpec((D_MODEL, tn), lambda j: (0, j)),
                  pl.BlockSpec((1, tn), lambda j: (0, j))],
        out_specs=pl.BlockSpec((rows, tn), lambda j: (0, j)),
        compiler_params=pltpu.CompilerParams(dimension_semantics=("arbitrary",),
                                             vmem_limit_bytes=VMEM_LIMIT),
        name="mod",
    )(cvec, w_mod, b_mod)


def _ctx_kernel(ctx_ref, mod_ref, g1_ref, wxb_ref, wdt_ref, cw_ref, cb_ref, dtb_ref, alog_ref, e64_ref,
                h0_ref):
    L = CTX_LEN
    m = mod_ref[0]
    hc = _rms(ctx_ref[0]) * g1_ref[...] * (1.0 + m[1:2]) + m[0:1]
    hb = hc.astype(BF16)
    pxb = _dot(hb, wxb_ref[...])
    dtr = _dot(hb, wdt_ref[...])
    rowi = lax.broadcasted_iota(jnp.int32, (L, XB_W), 0)
    dn = jnp.where(rowi == 0, 0.0, pltpu.roll(pxb, 1, 0))
    up = jnp.where(rowi == L - 1, 0.0, pltpu.roll(pxb, L - 1, 0))
    cw = cw_ref[...]
    xb = _silu(cw[0:1] * dn + cw[1:2] * pxb + cw[2:3] * up + cb_ref[...])
    xs = xb[:, :D_SSD]
    bm = xb[:, D_SSD:].astype(BF16)
    dt = _softplus(dtr + dtb_ref[...])
    da = dt * (-jnp.exp(alog_ref[...]))
    ri = lax.broadcasted_iota(jnp.int32, (L, L), 0)
    ci = lax.broadcasted_iota(jnp.int32, (L, L), 1)
    for d in range(2):
        tri = (ci <= ri) if d == 0 else (ci >= ri)
        cum = jnp.dot(tri.astype(F32), da, precision=HIGHEST, preferred_element_type=F32)
        last = cum[L - 1:L] if d == 0 else cum[0:1]
        w_e = _expand(jnp.exp(last - cum) * dt, e64_ref[d], 2)
        xw = (xs * w_e).astype(BF16)
        for g in range(GROUPS):
            gw = D_SSD // GROUPS
            st = lax.dot_general(bm[:, g * STATE:(g + 1) * STATE], xw[:, g * gw:(g + 1) * gw],
                                 (((0,), (0,)), ((), ())), preferred_element_type=F32)
            h0_ref[0, d, :, g * gw:(g + 1) * gw] = st


def _ctx_states(ctx, mod3, g1, wxb, wdt, cw, cb, dtb, alog, e64):
    bsz = ctx.shape[0]
    mod_row = bsz
    const = lambda *shape: pl.BlockSpec(shape, lambda b: (0,) * len(shape))
    return pl.pallas_call(
        _ctx_kernel,
        out_shape=jax.ShapeDtypeStruct((bsz, 2, STATE, D_SSD), F32),
        grid=(bsz,),
        in_specs=[pl.BlockSpec((1, CTX_LEN, D_MODEL), lambda b: (b, 0, 0)),
                  pl.BlockSpec((1, N_MOD, D_MODEL), lambda b: (mod_row, 0, 0)),
                  const(1, D_MODEL), const(D_MODEL, XB_W), const(D_MODEL, LANES),
                  const(3, XB_W), const(1, XB_W), const(1, LANES), const(1, LANES),
                  const(2, LANES, D_SSD)],
        out_specs=pl.BlockSpec((1, 2, STATE, D_SSD), lambda b: (b, 0, 0, 0)),
        compiler_params=pltpu.CompilerParams(dimension_semantics=("arbitrary",),
                                             vmem_limit_bytes=VMEM_LIMIT),
        name="ctx_states",
    )(ctx, mod3, g1, wxb, wdt, cw, cb, dtb, alog, e64)


def _inproj_kernel(x_ref, mod_ref, g1_ref, wz_ref, wxbc_ref, wdt_ref, wb_ref, wc_ref, wu_ref,
                   z_ref, xbc_ref, dt_ref, scb_ref, v_ref):
    m = mod_ref[0]
    hx = _rms(x_ref[...]) * g1_ref[...] * (1.0 + m[1:2]) + m[0:1]
    hb = hx.astype(BF16)
    z_ref[...] = _dot(hb, wz_ref[...]).astype(BF16)
    xbc_ref[...] = _dot(hb, wxbc_ref[...]).astype(BF16)
    dt_ref[...] = _dot(hb, wdt_ref[...])
    scb_ref[...] = _dot(hb, wb_ref[...]).astype(BF16)
    v_ref[...] = (_dot(hb, wc_ref[...]) * _dot(hb, wu_ref[...])).astype(BF16)


def _inproj(x2, mod3, g1, wz, wxbc, wdt, wb, wc, wu):
    t = x2.shape[0]
    tm = TOK_TILE
    per_b = SEQ // tm
    const = lambda *shape: pl.BlockSpec(shape, lambda i: (0,) * len(shape))
    tile = lambda w: pl.BlockSpec((tm, w), lambda i: (i, 0))
    return pl.pallas_call(
        _inproj_kernel,
        out_shape=(jax.ShapeDtypeStruct((t, D_SSD), BF16), jax.ShapeDtypeStruct((t, XBC_W), BF16),
                   jax.ShapeDtypeStruct((t, LANES), F32), jax.ShapeDtypeStruct((t, D_SC), BF16),
                   jax.ShapeDtypeStruct((t, D_SC), BF16)),
        grid=(t // tm,),
        in_specs=[tile(D_MODEL),
                  pl.BlockSpec((1, N_MOD, D_MODEL), lambda i: (i // per_b, 0, 0)),
                  const(1, D_MODEL), const(D_MODEL, D_SSD), const(D_MODEL, XBC_W), const(D_MODEL, LANES),
                  const(D_MODEL, D_SC), const(D_MODEL, D_SC), const(D_MODEL, D_SC)],
        out_specs=(tile(D_SSD), tile(XBC_W), tile(LANES), tile(D_SC), tile(D_SC)),
        compiler_params=pltpu.CompilerParams(dimension_semantics=("arbitrary",),
                                             vmem_limit_bytes=VMEM_LIMIT),
        name="inproj",
    )(x2, mod3, g1, wz, wxbc, wdt, wb, wc, wu)


def _ssd_kernel(xbc_ref, z_ref, dt_ref, h0_ref, cw_ref, cb_ref, dtb_ref, alog_ref, dsk_ref, g_ref,
                e64_ref, e128_ref, o_ref, xc_ref, y_ref, s_ref):
    Q = CHUNK
    nck = SEQ // Q
    gw = D_SSD // GROUPS

    rowi = lax.broadcasted_iota(jnp.int32, (Q, XBC_W), 0)

    def conv_body(c, carry):
        r0 = pl.multiple_of(c * Q, Q)
        main = xbc_ref[0, pl.ds(r0, Q), :].astype(F32)
        pstart = pl.multiple_of(jnp.maximum(r0 - 16, 0), 16)
        nstart = pl.multiple_of(jnp.minimum(r0 + Q, SEQ - 16), 16)
        prev = xbc_ref[0, pl.ds(pstart, 16), :].astype(F32)[15:16]
        nxt = xbc_ref[0, pl.ds(nstart, 16), :].astype(F32)[0:1]
        prev = jnp.where(c > 0, prev, 0.0)
        nxt = jnp.where(c < nck - 1, nxt, 0.0)
        dn = jnp.where(rowi == 0, prev, pltpu.roll(main, 1, 0))
        up = jnp.where(rowi == Q - 1, nxt, pltpu.roll(main, Q - 1, 0))
        cw = cw_ref[...]
        conv = cw[0:1] * dn + cw[1:2] * main + cw[2:3] * up + cb_ref[...]
        xc_ref[pl.ds(r0, Q), :] = _silu(conv).astype(BF16)
        return carry

    lax.fori_loop(0, nck, conv_body, 0)

    ri = lax.broadcasted_iota(jnp.int32, (Q, Q), 0)
    ci = lax.broadcasted_iota(jnp.int32, (Q, Q), 1)
    lane = lax.broadcasted_iota(jnp.int32, (Q, LANES), 1)
    a_neg = -jnp.exp(alog_ref[...])

    def chunk(c, d):
        r0 = pl.multiple_of(c * Q, Q)
        rows = pl.ds(r0, Q)
        xs = xc_ref[rows, 0:D_SSD].astype(F32)
        bm = xc_ref[rows, D_SSD:D_SSD + GROUPS * STATE]
        cm = xc_ref[rows, D_SSD + GROUPS * STATE:XBC_W]
        dt = _softplus(dt_ref[0, rows, :] + dtb_ref[...])
        da = dt * a_neg
        tri = (ci <= ri) if d == 0 else (ci >= ri)
        cum = jnp.dot(tri.astype(F32), da, precision=HIGHEST, preferred_element_type=F32)
        cum_t = cum.T
        last = cum[Q - 1:Q] if d == 0 else cum[0:1]
        ecum_e = _expand(jnp.exp(cum), e64_ref[d], 2)
        dt_e = _expand(dt, e64_ref[d], 2)
        w_e = _expand(jnp.exp(last - cum) * dt, e64_ref[d], 2)
        colb = _expand(cum, e128_ref[d], 3)
        decay_e = ecum_e[Q - 1:Q] if d == 0 else ecum_e[0:1]

        gmat = [lax.dot_general(cm[:, g * STATE:(g + 1) * STATE], bm[:, g * STATE:(g + 1) * STATE],
                                (((1,), (1,)), ((), ())), preferred_element_type=F32)
                for g in range(GROUPS)]
        xdt = xs * dt_e
        y_parts = []
        for p in range(HEADS // 2):
            g = (2 * p) // (HEADS // GROUPS)
            ms = []
            for hh in (2 * p, 2 * p + 1):
                seg = colb[:, hh * LANES:(hh + 1) * LANES] - cum_t[HEADS * d + hh:HEADS * d + hh + 1, :]
                mm = jnp.where(tri, jnp.exp(jnp.where(tri, seg, 0.0)), 0.0) * gmat[g]
                ms.append(mm.astype(BF16))
            mcat = jnp.concatenate(ms, axis=1)
            xp = xdt[:, p * LANES:(p + 1) * LANES]
            rhs = jnp.concatenate([jnp.where(lane < HEAD_DIM, xp, 0.0).astype(BF16),
                                   jnp.where(lane >= HEAD_DIM, xp, 0.0).astype(BF16)], axis=0)
            y_parts.append(_dot(mcat, rhs))
        y_diag = jnp.concatenate(y_parts, axis=1)

        s_old = s_ref[...]
        s_bf = s_old.astype(BF16)
        y_off = jnp.concatenate(
            [_dot(cm[:, g * STATE:(g + 1) * STATE], s_bf[:, g * gw:(g + 1) * gw]) for g in range(GROUPS)],
            axis=1)
        y = y_diag + y_off * ecum_e

        xw = (xs * w_e).astype(BF16)
        upd = jnp.concatenate(
            [lax.dot_general(bm[:, g * STATE:(g + 1) * STATE], xw[:, g * gw:(g + 1) * gw],
                             (((0,), (0,)), ((), ())), preferred_element_type=F32) for g in range(GROUPS)],
            axis=1)
        s_ref[...] = s_old * decay_e + upd

        if d == 0:
            y_ref[rows, :] = y + dsk_ref[...] * xs
        else:
            tot = y_ref[rows, :] + y
            zz = z_ref[0, rows, :].astype(F32)
            gz = tot * _silu(zz)
            outs = []
            for g in range(GROUPS):
                gg = gz[:, g * gw:(g + 1) * gw]
                outs.append(gg * lax.rsqrt(jnp.mean(gg * gg, axis=-1, keepdims=True) + NORM_EPS))
            o_ref[0, rows, :] = (jnp.concatenate(outs, axis=1) * g_ref[...]).astype(BF16)

    s_ref[...] = h0_ref[0, 0]

    def fwd_body(i, carry):
        chunk(i, 0)
        return carry

    lax.fori_loop(0, nck, fwd_body, 0)

    s_ref[...] = h0_ref[0, 1]

    def bwd_body(i, carry):
        chunk(nck - 1 - i, 1)
        return carry

    lax.fori_loop(0, nck, bwd_body, 0)


def _ssd(xbc3, z3, dt3, h0, cw, cb, dtb, alog, dsk, g, e64, e128):
    bsz = xbc3.shape[0]
    const = lambda *shape: pl.BlockSpec(shape, lambda b: (0,) * len(shape))
    seq = lambda w: pl.BlockSpec((1, SEQ, w), lambda b: (b, 0, 0))
    return pl.pallas_call(
        _ssd_kernel,
        out_shape=jax.ShapeDtypeStruct((bsz, SEQ, D_SSD), BF16),
        grid=(bsz,),
        in_specs=[seq(XBC_W), seq(D_SSD), seq(LANES),
                  pl.BlockSpec((1, 2, STATE, D_SSD), lambda b: (b, 0, 0, 0)),
                  const(3, XBC_W), const(1, XBC_W), const(1, LANES), const(1, LANES),
                  const(1, D_SSD), const(1, D_SSD), const(2, LANES, D_SSD), const(2, LANES, HEADS * LANES)],
        out_specs=seq(D_SSD),
        scratch_shapes=[pltpu.VMEM((SEQ, XBC_W), BF16), pltpu.VMEM((SEQ, D_SSD), F32),
                        pltpu.VMEM((STATE, D_SSD), F32)],
        compiler_params=pltpu.CompilerParams(dimension_semantics=("arbitrary",),
                                             vmem_limit_bytes=VMEM_LIMIT),
        name="ssd",
    )(xbc3, z3, dt3, h0, cw, cb, dtb, alog, dsk, g, e64, e128)


def _outproj_kernel(x_ref, yssd_ref, scb_ref, v_ref, vp_ref, vn_ref, mod_ref, scw_ref, wo1_ref, wo2_ref,
                    g2_ref, wrt_ref, br_ref, x1_ref, lg_ref):
    tm = TOK_TILE
    per_b = SEQ // tm
    i = pl.program_id(0)
    first = (i % per_b) == 0
    last = (i % per_b) == per_b - 1
    m = mod_ref[0]
    v = v_ref[...].astype(F32)
    vp = jnp.where(first, 0.0, vp_ref[...].astype(F32))
    vn = jnp.where(last, 0.0, vn_ref[...].astype(F32))
    dn = jnp.concatenate([vp, v[:tm - GRID_W]], axis=0)
    up = jnp.concatenate([v[GRID_W:], vn], axis=0)
    scw = scw_ref[...]
    ysc = scb_ref[...].astype(F32) * (scw[0:1] * dn + scw[1:2] * v + scw[2:3] * up)
    out = _dot(yssd_ref[...], wo1_ref[...]) + _dot(ysc.astype(BF16), wo2_ref[...])
    x1 = x_ref[...] + m[2:3] * out
    x1_ref[...] = x1
    h2 = _rms(x1) * g2_ref[...] * (1.0 + m[4:5]) + m[3:4]
    lg_ref[...] = lax.dot_general(wrt_ref[...], h2, (((1,), (1,)), ((), ())), precision=HIGHEST,
                                  preferred_element_type=F32) + br_ref[...]


def _outproj(x2, yssd, scb, v, mod3, scw, wo1, wo2, g2, wrt, br):
    t = x2.shape[0]
    tm = TOK_TILE
    per_b = SEQ // tm
    r = tm // GRID_W
    nrow = t // GRID_W
    const = lambda *shape: pl.BlockSpec(shape, lambda i: (0,) * len(shape))
    tile = lambda w: pl.BlockSpec((tm, w), lambda i: (i, 0))
    return pl.pallas_call(
        _outproj_kernel,
        out_shape=(jax.ShapeDtypeStruct((t, D_MODEL), F32), jax.ShapeDtypeStruct((N_EXPERTS, t), F32)),
        grid=(t // tm,),
        in_specs=[tile(D_MODEL), tile(D_SSD), tile(D_SC), tile(D_SC),
                  pl.BlockSpec((GRID_W, D_SC), lambda i: (jnp.maximum(i * r - 1, 0), 0)),
                  pl.BlockSpec((GRID_W, D_SC), lambda i: (jnp.minimum((i + 1) * r, nrow - 1), 0)),
                  pl.BlockSpec((1, N_MOD, D_MODEL), lambda i: (i // per_b, 0, 0)),
                  const(3, D_SC), const(D_SSD, D_MODEL), const(D_SC, D_MODEL), const(1, D_MODEL),
                  const(N_EXPERTS, D_MODEL), const(N_EXPERTS, 1)],
        out_specs=(tile(D_MODEL), pl.BlockSpec((N_EXPERTS, tm), lambda i: (0, i))),
        compiler_params=pltpu.CompilerParams(dimension_semantics=("arbitrary",),
                                             vmem_limit_bytes=VMEM_LIMIT),
        name="outproj",
    )(x2, yssd, scb, v, v, v, mod3, scw, wo1, wo2, g2, wrt, br)


def _route_kernel(lg_ref, dest_ref, gate_ref, idx_ref, meta_ref, rank_ref, carry_ref, *, n_tok, n_blocks):
    tt = RT_TILE
    ne = N_EXPERTS
    eio = lax.broadcasted_iota(jnp.int32, (ne, tt), 0)
    si = lax.broadcasted_iota(jnp.int32, (tt, tt), 0)
    ti = lax.broadcasted_iota(jnp.int32, (tt, tt), 1)
    before = (si < ti).astype(BF16)
    carry_ref[...] = jnp.zeros_like(carry_ref)

    def tile_body(j, c):
        t0 = pl.multiple_of(j * tt, tt)
        l = lg_ref[:, pl.ds(t0, tt)]
        onehot = jnp.zeros((ne, tt), F32)
        tops, sels = [], []
        for _ in range(TOP_K):
            mx = jnp.max(l, axis=0, keepdims=True)
            idx = jnp.min(jnp.where(l == mx, eio, ne), axis=0, keepdims=True)
            sel = eio == idx
            l = jnp.where(sel, -jnp.inf, l)
            onehot = onehot + sel.astype(F32)
            tops.append(mx)
            sels.append(sel)
            idx_ref[pl.ds(len(tops) - 1, 1), pl.ds(t0, tt)] = idx
        ex = [jnp.exp(tv - tops[0]) for tv in tops]
        den = ex[0] + ex[1] + ex[2] + ex[3]
        prefix = _dot(onehot.astype(BF16), before) + carry_ref[:, 0:1]
        for k in range(TOP_K):
            gate_ref[pl.ds(k, 1), pl.ds(t0, tt)] = ex[k] / den
            rk = jnp.sum(jnp.where(sels[k], prefix, 0.0), axis=0, keepdims=True)
            rank_ref[pl.ds(k, 1), pl.ds(t0, tt)] = rk.astype(jnp.int32)
        carry_ref[...] = carry_ref[...] + jnp.sum(onehot, axis=1, keepdims=True)
        return c

    lax.fori_loop(0, n_tok // tt, tile_body, 0)

    counts = carry_ref[...]
    padded = jnp.floor((counts + (MOE_BM - 1)) * (1.0 / MOE_BM)) * MOE_BM
    er = lax.broadcasted_iota(jnp.int32, (ne, ne), 0)
    ec = lax.broadcasted_iota(jnp.int32, (ne, ne), 1)
    pad_start = jnp.dot((ec < er).astype(F32), padded, precision=HIGHEST, preferred_element_type=F32)
    pad_end = pad_start + padded

    def dest_body(j, c):
        t0 = pl.multiple_of(j * tt, tt)
        for k in range(TOP_K):
            idx = idx_ref[pl.ds(k, 1), pl.ds(t0, tt)]
            base = jnp.sum(jnp.where(eio == idx, pad_start[:, 0:1], 0.0), axis=0, keepdims=True)
            dest_ref[pl.ds(k, 1), pl.ds(t0, tt)] = base.astype(jnp.int32) + rank_ref[pl.ds(k, 1), pl.ds(t0, tt)]
        return c

    lax.fori_loop(0, n_tok // tt, dest_body, 0)

    width = meta_ref.shape[1]
    sub = lax.broadcasted_iota(jnp.int32, (ne, width), 0)
    lan = lax.broadcasted_iota(jnp.int32, (ne, width), 1)
    diag = sub == lan
    cnt_row = jnp.sum(jnp.where(diag, counts[:, 0:1], 0.0), axis=0, keepdims=True)
    start_row = jnp.sum(jnp.where(diag, pad_start[:, 0:1], 0.0), axis=0, keepdims=True)
    blk_start = (lan * MOE_BM).astype(F32)
    blk_exp = jnp.sum((pad_end[:, 0:1] <= blk_start).astype(F32), axis=0, keepdims=True)
    blk_exp = jnp.minimum(blk_exp, float(ne - 1))
    used = jnp.sum(padded[:, 0:1], axis=0, keepdims=True) * (1.0 / MOE_BM)
    meta_ref[0:1, :] = cnt_row.astype(jnp.int32)
    meta_ref[1:2, :] = start_row.astype(jnp.int32)
    meta_ref[2:3, :] = blk_exp.astype(jnp.int32)
    meta_ref[3:4, :] = jnp.broadcast_to(used, (1, width)).astype(jnp.int32)
    meta_ref[4:8, :] = jnp.zeros((4, width), jnp.int32)


def _route(lgt, n_blocks):
    ne, n_tok = lgt.shape
    width = -(-n_blocks // LANES) * LANES
    full = lambda *shape: pl.BlockSpec(shape, lambda: (0,) * len(shape))
    return pl.pallas_call(
        functools.partial(_route_kernel, n_tok=n_tok, n_blocks=n_blocks),
        out_shape=(jax.ShapeDtypeStruct((TOP_K, n_tok), jnp.int32),
                   jax.ShapeDtypeStruct((TOP_K, n_tok), F32),
                   jax.ShapeDtypeStruct((TOP_K, n_tok), jnp.int32),
                   jax.ShapeDtypeStruct((8, width), jnp.int32)),
        in_specs=[full(ne, n_tok)],
        out_specs=(full(TOP_K, n_tok), full(TOP_K, n_tok), full(TOP_K, n_tok), full(8, width)),
        scratch_shapes=[pltpu.VMEM((TOP_K, n_tok), jnp.int32), pltpu.VMEM((ne, LANES), F32)],
        compiler_params=pltpu.CompilerParams(vmem_limit_bytes=VMEM_LIMIT),
        name="route",
    )(lgt)


def _dispatch_kernel(dest_ref, cnt_ref, start_ref, nu_ref, x1_ref, meta_ref, mod_ref, g2_ref, zsrc_ref, xs_ref,
                     hbuf, sem, zsem):
    i = pl.program_id(0)
    n = pl.num_programs(0)
    tl = DISP_TILE
    slot = i % 2
    nb = xs_ref.shape[0] // MOE_BM

    def zero_block(b):
        return pltpu.make_async_copy(zsrc_ref, xs_ref.at[pl.ds(b * MOE_BM, MOE_BM)], zsem)

    @pl.when(i == 0)
    def _():
        def start_e(e, c):
            @pl.when(cnt_ref[e] > 0)
            def _():
                zero_block((start_ref[e] + cnt_ref[e] - 1) // MOE_BM).start()
            return c

        def wait_e(e, c):
            @pl.when(cnt_ref[e] > 0)
            def _():
                zero_block(0).wait()
            return c

        def start_t(b, c):
            zero_block(b).start()
            return c

        def wait_t(b, c):
            zero_block(0).wait()
            return c

        lax.fori_loop(0, N_EXPERTS, start_e, 0)
        lax.fori_loop(nu_ref[0], nb, start_t, 0)
        lax.fori_loop(0, N_EXPERTS, wait_e, 0)
        lax.fori_loop(nu_ref[0], nb, wait_t, 0)

    m = mod_ref[0]
    hbuf[slot, :, 0:D_MODEL] = _rms(x1_ref[...]) * g2_ref[...] * (1.0 + m[4:5]) + m[3:4]
    hbuf[slot, :, D_MODEL:ROW_W] = meta_ref[...]

    def per_tok(t, c):
        tok = i * tl + t
        for k in range(TOP_K):
            d = dest_ref[tok * TOP_K + k]
            pltpu.make_async_copy(hbuf.at[slot, pl.ds(t, 1)], xs_ref.at[pl.ds(d, 1)], sem.at[slot]).start()
        return c

    lax.fori_loop(0, tl, per_tok, 0)

    def wait_slot(sl):
        for _ in range(TOP_K):
            pltpu.make_async_copy(hbuf.at[sl], xs_ref.at[pl.ds(0, tl)], sem.at[sl]).wait()

    @pl.when(i > 0)
    def _():
        wait_slot(1 - slot)

    @pl.when(i == n - 1)
    def _():
        wait_slot(slot)


def _dispatch(dest_flat, cnt, start, n_used, x1, meta_rows, mod3, g2, zsrc, n_rows):
    n_tok = x1.shape[0]
    tl = DISP_TILE
    per_b = SEQ // tl
    return pl.pallas_call(
        _dispatch_kernel,
        out_shape=jax.ShapeDtypeStruct((n_rows, ROW_W), F32),
        grid_spec=pltpu.PrefetchScalarGridSpec(
            num_scalar_prefetch=4,
            grid=(n_tok // tl,),
            in_specs=[pl.BlockSpec((tl, D_MODEL), lambda i, *_: (i, 0)),
                      pl.BlockSpec((tl, LANES), lambda i, *_: (i, 0)),
                      pl.BlockSpec((1, N_MOD, D_MODEL), lambda i, *_: (i // per_b, 0, 0)),
                      pl.BlockSpec((1, D_MODEL), lambda i, *_: (0, 0)),
                      pl.BlockSpec((MOE_BM, ROW_W), lambda i, *_: (0, 0))],
            out_specs=pl.BlockSpec(memory_space=pl.ANY),
            scratch_shapes=[pltpu.VMEM((2, tl, ROW_W), F32),
                            pltpu.SemaphoreType.DMA((2,)), pltpu.SemaphoreType.DMA]),
        compiler_params=pltpu.CompilerParams(dimension_semantics=("arbitrary",),
                                             vmem_limit_bytes=VMEM_LIMIT),
        name="dispatch",
    )(dest_flat, cnt, start, n_used, x1, meta_rows, mod3, g2, zsrc)


def _expert_kernel(be_ref, nu_ref, xs_ref, wgu_ref, bgu_ref, wd_ref, bd_ref, zero_ref, ytm_ref,
                   ybuf, idv, ids, sc_sem, id_sem, z_sem, *, n_tok):
    j = pl.program_id(0)
    nbk = pl.num_programs(0) - 1
    nu = nu_ref[0]
    slot = j % 2
    prev = 1 - slot

    @pl.when(j == 0)
    def _():
        cp = pltpu.make_async_copy(zero_ref, ytm_ref.at[pl.ds(TOP_K * n_tok, MOE_BM)], z_sem)
        cp.start()
        cp.wait()

    n_pc = 4
    pw1 = D_FF // n_pc
    pw2 = D_MODEL // n_pc
    groups = [48] * n_pc + [16] * n_pc
    assert sum(groups) == MOE_BM

    def scatter_group(g):
        lo = sum(groups[:g])
        for r in range(lo, lo + groups[g]):
            pltpu.make_async_copy(ybuf.at[prev, pl.ds(r, 1)], ytm_ref.at[pl.ds(ids[prev, 0, r], 1)],
                                  sc_sem.at[prev]).start()

    def scatter_wait(sl):
        pltpu.make_async_copy(ybuf.at[sl], ytm_ref.at[pl.ds(0, MOE_BM)], sc_sem.at[sl]).wait()

    def compute(with_scatter, wait_older):
        e_f = be_ref[jnp.minimum(j, nbk - 1)].astype(F32)
        xb = xs_ref[:, 0:D_MODEL].astype(BF16)
        meta = xs_ref[:, D_MODEL:ROW_W]
        gate = jnp.zeros((MOE_BM, 1), F32)
        kk = jnp.zeros((MOE_BM, 1), F32)
        for k in range(TOP_K):
            mk = meta[:, META_IDX + k:META_IDX + k + 1] == e_f
            gate = gate + jnp.where(mk, meta[:, META_GATE + k:META_GATE + k + 1], 0.0)
            kk = kk + jnp.where(mk, float(k), 0.0)
        row = kk * float(n_tok) + meta[:, META_TOK:META_TOK + 1]
        row_t = jnp.broadcast_to(row, (MOE_BM, LANES)).T
        idv[slot] = row_t[0:8].astype(jnp.int32)
        pltpu.make_async_copy(idv.at[slot], ids.at[slot], id_sem.at[slot]).start()
        for c in range(n_pc):
            if with_scatter:
                scatter_group(c)
            cg = slice(c * pw1, (c + 1) * pw1)
            cl = slice(D_FF + c * pw1, D_FF + (c + 1) * pw1)
            glu = jnp.minimum(_dot(xb, wgu_ref[0, :, cg]) + bgu_ref[0, :, cg], SWIGLU_LIMIT)
            lin = jnp.clip(_dot(xb, wgu_ref[0, :, cl]) + bgu_ref[0, :, cl], -SWIGLU_LIMIT, SWIGLU_LIMIT)
            ybuf[2, :, cg] = glu * jax.nn.sigmoid(SWIGLU_ALPHA * glu) * (lin + 1.0)
        act = ybuf[2].astype(BF16)
        if wait_older:
            scatter_wait(slot)
        for c in range(n_pc):
            if with_scatter:
                scatter_group(n_pc + c)
            cs = slice(c * pw2, (c + 1) * pw2)
            ybuf[slot, :, cs] = (_dot(act, wd_ref[0, :, cs]) + bd_ref[0, :, cs]) * gate

    def ids_wait():
        pltpu.make_async_copy(idv.at[prev], ids.at[prev], id_sem.at[prev]).wait()

    @pl.when(j == 0)
    def _():
        compute(False, False)

    @pl.when(jnp.logical_and(j == 1, j < nu))
    def _():
        ids_wait()
        compute(True, False)

    @pl.when(jnp.logical_and(j >= 2, j < nu))
    def _():
        ids_wait()
        compute(True, True)

    @pl.when(j == nu)
    def _():
        ids_wait()
        for g in range(len(groups)):
            scatter_group(g)
        scatter_wait(prev)

    @pl.when(jnp.logical_and(j == nu, j >= 2))
    def _():
        scatter_wait(slot)


def _experts(blk_exp, n_used, xs, wgu, bgu, wd, bd, n_tok):
    n_rows = xs.shape[0]
    nb = n_rows // MOE_BM
    row_blk = lambda j, be, nu: (jnp.minimum(j, nu[0] - 1), 0)
    per_e = lambda j, be, nu: (be[jnp.minimum(j, nb - 1)], 0, 0)
    zero = jnp.zeros((MOE_BM, D_MODEL), F32)
    return pl.pallas_call(
        functools.partial(_expert_kernel, n_tok=n_tok),
        out_shape=jax.ShapeDtypeStruct((TOP_K * n_tok + MOE_BM, D_MODEL), F32),
        grid_spec=pltpu.PrefetchScalarGridSpec(
            num_scalar_prefetch=2,
            grid=(nb + 1,),
            in_specs=[pl.BlockSpec((MOE_BM, ROW_W), row_blk),
                      pl.BlockSpec((1, D_MODEL, 2 * D_FF), per_e),
                      pl.BlockSpec((1, 1, 2 * D_FF), per_e),
                      pl.BlockSpec((1, D_FF, D_MODEL), per_e),
                      pl.BlockSpec((1, 1, D_MODEL), per_e),
                      pl.BlockSpec((MOE_BM, D_MODEL), lambda j, be, nu: (0, 0))],
            out_specs=pl.BlockSpec(memory_space=pl.ANY),
            scratch_shapes=[pltpu.VMEM((3, MOE_BM, D_MODEL), F32), pltpu.VMEM((2, 8, MOE_BM), jnp.int32),
                            pltpu.SMEM((2, 8, MOE_BM), jnp.int32), pltpu.SemaphoreType.DMA((2,)),
                            pltpu.SemaphoreType.DMA((2,)), pltpu.SemaphoreType.DMA]),
        compiler_params=pltpu.CompilerParams(dimension_semantics=("arbitrary",),
                                             vmem_limit_bytes=VMEM_LIMIT),
        name="experts",
    )(blk_exp, n_used, xs, wgu, bgu, wd, bd, zero)


def _combine_kernel(y0_ref, y1_ref, y2_ref, y3_ref, x1_ref, mod_ref, fg_ref, o_ref):
    m = mod_ref[0]
    moe = (y0_ref[...] + y1_ref[...]) + (y2_ref[...] + y3_ref[...])
    x2 = x1_ref[...] + m[5:6] * moe
    o_ref[...] = _rms(x2) * fg_ref[...]


def _combine(ytm, x1, mod3, fg):
    n_tok = x1.shape[0]
    tc = COMB_TILE
    per_b = SEQ // tc
    nt = n_tok // tc
    slot_spec = lambda k: pl.BlockSpec((tc, D_MODEL), lambda i: (i + k * nt, 0))
    return pl.pallas_call(
        _combine_kernel,
        out_shape=jax.ShapeDtypeStruct((n_tok, D_MODEL), F32),
        grid=(nt,),
        in_specs=[slot_spec(0), slot_spec(1), slot_spec(2), slot_spec(3),
                  pl.BlockSpec((tc, D_MODEL), lambda i: (i, 0)),
                  pl.BlockSpec((1, N_MOD, D_MODEL), lambda i: (i // per_b, 0, 0)),
                  pl.BlockSpec((1, D_MODEL), lambda i: (0, 0))],
        out_specs=pl.BlockSpec((tc, D_MODEL), lambda i: (i, 0)),
        compiler_params=pltpu.CompilerParams(dimension_semantics=("arbitrary",),
                                             vmem_limit_bytes=VMEM_LIMIT),
        name="combine",
    )(ytm, ytm, ytm, ytm, x1, mod3, fg)


def _expansion_matrices():
    r = jnp.arange(LANES)[:, None]
    out64, out128 = [], []
    for d in range(2):
        l64 = jnp.arange(D_SSD)[None, :]
        l128 = jnp.arange(HEADS * LANES)[None, :]
        out64.append((l64 // HEAD_DIM == r - HEADS * d).astype(BF16))
        out128.append((l128 // LANES == r - HEADS * d).astype(BF16))
    return jnp.stack(out64), jnp.stack(out128)


def _pad_lanes(v):
    return jnp.pad(v, [(0, 0)] * (v.ndim - 1) + [(0, LANES - v.shape[-1])])


def kernel(x, c, ctx, c_ctx, w_mod, b_mod, norm1_g, w_in, ssd_conv_w, ssd_conv_b, ssd_dt_bias, ssd_a_log,
           ssd_d, ssd_norm_g, sc_conv_w, w_out, norm2_g, w_router, b_router, w_gate_up, b_gate_up, w_down,
           b_down, final_g):
    bsz = x.shape[0]
    n_tok = bsz * SEQ
    n_assign = n_tok * TOP_K
    n_blocks = n_assign // MOE_BM + N_EXPERTS
    n_rows = n_blocks * MOE_BM
    li = 0

    cvec = jnp.concatenate([c, c_ctx[None, :], jnp.zeros((7, D_MODEL), F32)], axis=0)
    mod3 = _mod(cvec, w_mod[li], b_mod[li][None, :]).reshape(bsz + 8, N_MOD, D_MODEL)

    w = w_in[li]
    wz = w[:, Z0:X0].astype(BF16)
    wxbc = w[:, X0:DT0].astype(BF16)
    wdt = _pad_lanes(w[:, DT0:SC0]).astype(BF16)
    wb = w[:, SC0:SC0 + D_SC].astype(BF16)
    wc = w[:, SC0 + D_SC:SC0 + 2 * D_SC].astype(BF16)
    wu = w[:, SC0 + 2 * D_SC:].astype(BF16)
    g1 = norm1_g[li][None, :]
    cw = ssd_conv_w[li]
    cb = ssd_conv_b[li][None, :]
    dtb = _pad_lanes(ssd_dt_bias[li].reshape(1, 2 * HEADS))
    alog = _pad_lanes(ssd_a_log[li].reshape(1, 2 * HEADS))
    e64, e128 = _expansion_matrices()

    h0 = _ctx_states(ctx, mod3, g1, wxbc[:, :XB_W], wdt, cw[:, :XB_W], cb[:, :XB_W], dtb, alog, e64)

    x2 = x.reshape(n_tok, D_MODEL)
    z, xbc, dtr, scb, v = _inproj(x2, mod3, g1, wz, wxbc, wdt, wb, wc, wu)

    dsk = jnp.repeat(ssd_d[li], HEAD_DIM)[None, :]
    yssd = _ssd(xbc.reshape(bsz, SEQ, XBC_W), z.reshape(bsz, SEQ, D_SSD), dtr.reshape(bsz, SEQ, LANES), h0,
                cw, cb, dtb, alog, dsk, ssd_norm_g[li][None, :], e64, e128)

    wo = w_out[li].astype(BF16)
    g2 = norm2_g[li][None, :]
    x1, lgt = _outproj(x2, yssd.reshape(n_tok, D_SSD), scb, v, mod3, sc_conv_w[li], wo[:D_SSD], wo[D_SSD:],
                       g2, w_router[li].T, b_router[li][:, None])

    dest_t, gate_t, idx_t, meta = _route(lgt, n_blocks)
    dest_flat = dest_t.T.reshape(n_assign)
    cnt = meta[0, :N_EXPERTS]
    start = meta[1, :N_EXPERTS]
    blk_exp = meta[2, :n_blocks]
    n_used = meta[3, :1]

    meta_rows = _pad_lanes(jnp.concatenate(
        [idx_t.T.astype(F32), gate_t.T, jnp.arange(n_tok, dtype=F32)[:, None]], axis=1))
    pad_meta = _pad_lanes(jnp.concatenate(
        [jnp.full((MOE_BM, TOP_K), -1.0, F32), jnp.zeros((MOE_BM, TOP_K), F32),
         (TOP_K * n_tok + jnp.arange(MOE_BM, dtype=F32))[:, None]], axis=1))
    zsrc = jnp.concatenate([jnp.zeros((MOE_BM, D_MODEL), F32), pad_meta], axis=1)

    xs = _dispatch(dest_flat, cnt, start, n_used, x1, meta_rows, mod3, g2, zsrc, n_rows)
    ytm = _experts(blk_exp, n_used, xs, w_gate_up[li].astype(BF16), b_gate_up[li][:, None, :],
                   w_down[li].astype(BF16), b_down[li][:, None, :], n_tok)
    out = _combine(ytm, x1, mod3, final_g[None, :])
    return out.reshape(bsz, SEQ, D_MODEL)
```

```python
import functools

import jax
import jax.numpy as jnp
from jax import lax
from jax.experimental import pallas as pl
from jax.experimental.pallas import tpu as pltpu

F32 = jnp.float32
BF16 = jnp.bfloat16
HIGHEST = lax.Precision.HIGHEST

D_MODEL = 1024
SEQ = 2048
CTX_LEN = 256
GRID_W = 64
D_SSD = 1024
D_SC = 1024
HEAD_DIM = 64
HEADS = 16
GROUPS = 2
STATE = 128
CHUNK = 128
N_EXPERTS = 32
TOP_K = 4
D_FF = 1024
SWIGLU_LIMIT = 7.0
SWIGLU_ALPHA = 1.702
NORM_EPS = 1e-6
N_MOD = 6
XBC_W = D_SSD + 2 * GROUPS * STATE
XB_W = D_SSD + GROUPS * STATE
LANES = 128

Z0 = 0
X0 = Z0 + D_SSD
B0 = X0 + D_SSD
C0 = B0 + GROUPS * STATE
DT0 = C0 + GROUPS * STATE
SC0 = DT0 + 2 * HEADS

TOK_TILE = 512
MOE_BM = 256
RT_TILE = 512
DISP_TILE = 256
COMB_TILE = 512
ROW_W = D_MODEL + LANES
META_IDX = 0
META_GATE = TOP_K
META_TOK = 2 * TOP_K
VMEM_LIMIT = 56 * 1024 * 1024


def _silu(v):
    return v * jax.nn.sigmoid(v)


def _softplus(v):
    return jnp.maximum(v, 0.0) + jnp.log1p(jnp.exp(-jnp.abs(v)))


def _rms(v):
    return v * lax.rsqrt(jnp.mean(v * v, axis=-1, keepdims=True) + NORM_EPS)


def _dot(a, b):
    return jnp.dot(a, b, preferred_element_type=F32)


def _expand(v, e, pieces):
    acc = None
    rem = v
    for _ in range(pieces):
        p = rem.astype(BF16)
        rem = rem - p.astype(F32)
        t = _dot(p, e)
        acc = t if acc is None else acc + t
    return acc


def _mod_kernel(c_ref, w_ref, b_ref, o_ref):
    o_ref[...] = jnp.dot(_silu(c_ref[...]), w_ref[...], precision=HIGHEST,
                         preferred_element_type=F32) + b_ref[...]


def _mod(cvec, w_mod, b_mod):
    rows = cvec.shape[0]
    n = w_mod.shape[1]
    tn = 1536
    return pl.pallas_call(
        _mod_kernel,
        out_shape=jax.ShapeDtypeStruct((rows, n), F32),
        grid=(n // tn,),
        in_specs=[pl.BlockSpec((rows, D_MODEL), lambda j: (0, 0)),
                  pl.BlockSpec((D_MODEL, tn), lambda j: (0, j)),
                  pl.BlockSpec((1, tn), lambda j: (0, j))],
        out_specs=pl.BlockSpec((rows, tn), lambda j: (0, j)),
        compiler_params=pltpu.CompilerParams(dimension_semantics=("arbitrary",),
                                             vmem_limit_bytes=VMEM_LIMIT),
        name="mod",
    )(cvec, w_mod, b_mod)


def _ctx_kernel(ctx_ref, mod_ref, g1_ref, wxb_ref, wdt_ref, cw_ref, cb_ref, dtb_ref, alog_ref, e64_ref,
                h0_ref):
    L = CTX_LEN
    m = mod_ref[0]
    hc = _rms(ctx_ref[0]) * g1_ref[...] * (1.0 + m[1:2]) + m[0:1]
    hb = hc.astype(BF16)
    pxb = _dot(hb, wxb_ref[...])
    dtr = _dot(hb, wdt_ref[...])
    rowi = lax.broadcasted_iota(jnp.int32, (L, XB_W), 0)
    dn = jnp.where(rowi == 0, 0.0, pltpu.roll(pxb, 1, 0))
    up = jnp.where(rowi == L - 1, 0.0, pltpu.roll(pxb, L - 1, 0))
    cw = cw_ref[...]
    xb = _silu(cw[0:1] * dn + cw[1:2] * pxb + cw[2:3] * up + cb_ref[...])
    xs = xb[:, :D_SSD]
    bm = xb[:, D_SSD:].astype(BF16)
    dt = _softplus(dtr + dtb_ref[...])
    da = dt * (-jnp.exp(alog_ref[...]))
    ri = lax.broadcasted_iota(jnp.int32, (L, L), 0)
    ci = lax.broadcasted_iota(jnp.int32, (L, L), 1)
    for d in range(2):
        tri = (ci <= ri) if d == 0 else (ci >= ri)
        cum = jnp.dot(tri.astype(F32), da, precision=HIGHEST, preferred_element_type=F32)
        last = cum[L - 1:L] if d == 0 else cum[0:1]
        w_e = _expand(jnp.exp(last - cum) * dt, e64_ref[d], 2)
        xw = (xs * w_e).astype(BF16)
        for g in range(GROUPS):
            gw = D_SSD // GROUPS
            st = lax.dot_general(bm[:, g * STATE:(g + 1) * STATE], xw[:, g * gw:(g + 1) * gw],
                                 (((0,), (0,)), ((), ())), preferred_element_type=F32)
            h0_ref[0, d, :, g * gw:(g + 1) * gw] = st


def _ctx_states(ctx, mod3, g1, wxb, wdt, cw, cb, dtb, alog, e64):
    bsz = ctx.shape[0]
    mod_row = bsz
    const = lambda *shape: pl.BlockSpec(shape, lambda b: (0,) * len(shape))
    return pl.pallas_call(
        _ctx_kernel,
        out_shape=jax.ShapeDtypeStruct((bsz, 2, STATE, D_SSD), F32),
        grid=(bsz,),
        in_specs=[pl.BlockSpec((1, CTX_LEN, D_MODEL), lambda b: (b, 0, 0)),
                  pl.BlockSpec((1, N_MOD, D_MODEL), lambda b: (mod_row, 0, 0)),
                  const(1, D_MODEL), const(D_MODEL, XB_W), const(D_MODEL, LANES),
                  const(3, XB_W), const(1, XB_W), const(1, LANES), const(1, LANES),
                  const(2, LANES, D_SSD)],
        out_specs=pl.BlockSpec((1, 2, STATE, D_SSD), lambda b: (b, 0, 0, 0)),
        compiler_params=pltpu.CompilerParams(dimension_semantics=("arbitrary",),
                                             vmem_limit_bytes=VMEM_LIMIT),
        name="ctx_states",
    )(ctx, mod3, g1, wxb, wdt, cw, cb, dtb, alog, e64)


def _inproj_kernel(x_ref, mod_ref, g1_ref, wz_ref, wxbc_ref, wdt_ref, wb_ref, wc_ref, wu_ref,
                   z_ref, xbc_ref, dt_ref, scb_ref, v_ref):
    m = mod_ref[0]
    hx = _rms(x_ref[...]) * g1_ref[...] * (1.0 + m[1:2]) + m[0:1]
    hb = hx.astype(BF16)
    z_ref[...] = _dot(hb, wz_ref[...]).astype(BF16)
    xbc_ref[...] = _dot(hb, wxbc_ref[...]).astype(BF16)
    dt_ref[...] = _dot(hb, wdt_ref[...])
    scb_ref[...] = _dot(hb, wb_ref[...]).astype(BF16)
    v_ref[...] = (_dot(hb, wc_ref[...]) * _dot(hb, wu_ref[...])).astype(BF16)


def _inproj(x2, mod3, g1, wz, wxbc, wdt, wb, wc, wu):
    t = x2.shape[0]
    tm = TOK_TILE
    per_b = SEQ // tm
    const = lambda *shape: pl.BlockSpec(shape, lambda i: (0,) * len(shape))
    tile = lambda w: pl.BlockSpec((tm, w), lambda i: (i, 0))
    return pl.pallas_call(
        _inproj_kernel,
        out_shape=(jax.ShapeDtypeStruct((t, D_SSD), BF16), jax.ShapeDtypeStruct((t, XBC_W), BF16),
                   jax.ShapeDtypeStruct((t, LANES), F32), jax.ShapeDtypeStruct((t, D_SC), BF16),
                   jax.ShapeDtypeStruct((t, D_SC), BF16)),
        grid=(t // tm,),
        in_specs=[tile(D_MODEL),
                  pl.BlockSpec((1, N_MOD, D_MODEL), lambda i: (i // per_b, 0, 0)),
                  const(1, D_MODEL), const(D_MODEL, D_SSD), const(D_MODEL, XBC_W), const(D_MODEL, LANES),
                  const(D_MODEL, D_SC), const(D_MODEL, D_SC), const(D_MODEL, D_SC)],
        out_specs=(tile(D_SSD), tile(XBC_W), tile(LANES), tile(D_SC), tile(D_SC)),
        compiler_params=pltpu.CompilerParams(dimension_semantics=("arbitrary",),
                                             vmem_limit_bytes=VMEM_LIMIT),
        name="inproj",
    )(x2, mod3, g1, wz, wxbc, wdt, wb, wc, wu)


def _ssd_kernel(xbc_ref, z_ref, dt_ref, h0_ref, cw_ref, cb_ref, dtb_ref, alog_ref, dsk_ref, g_ref,
                e64_ref, e128_ref, o_ref, xc_ref, y_ref, s_ref):
    Q = CHUNK
    nck = SEQ // Q
    gw = D_SSD // GROUPS

    rowi = lax.broadcasted_iota(jnp.int32, (Q, XBC_W), 0)

    def conv_body(c, carry):
        r0 = pl.multiple_of(c * Q, Q)
        main = xbc_ref[0, pl.ds(r0, Q), :].astype(F32)
        pstart = pl.multiple_of(jnp.maximum(r0 - 16, 0), 16)
        nstart = pl.multiple_of(jnp.minimum(r0 + Q, SEQ - 16), 16)
        prev = xbc_ref[0, pl.ds(pstart, 16), :].astype(F32)[15:16]
        nxt = xbc_ref[0, pl.ds(nstart, 16), :].astype(F32)[0:1]
        prev = jnp.where(c > 0, prev, 0.0)
        nxt = jnp.where(c < nck - 1, nxt, 0.0)
        dn = jnp.where(rowi == 0, prev, pltpu.roll(main, 1, 0))
        up = jnp.where(rowi == Q - 1, nxt, pltpu.roll(main, Q - 1, 0))
        cw = cw_ref[...]
        conv = cw[0:1] * dn + cw[1:2] * main + cw[2:3] * up + cb_ref[...]
        xc_ref[pl.ds(r0, Q), :] = _silu(conv).astype(BF16)
        return carry

    lax.fori_loop(0, nck, conv_body, 0)

    ri = lax.broadcasted_iota(jnp.int32, (Q, Q), 0)
    ci = lax.broadcasted_iota(jnp.int32, (Q, Q), 1)
    lane = lax.broadcasted_iota(jnp.int32, (Q, LANES), 1)
    a_neg = -jnp.exp(alog_ref[...])

    def chunk(c, d):
        r0 = pl.multiple_of(c * Q, Q)
        rows = pl.ds(r0, Q)
        xs = xc_ref[rows, 0:D_SSD].astype(F32)
        bm = xc_ref[rows, D_SSD:D_SSD + GROUPS * STATE]
        cm = xc_ref[rows, D_SSD + GROUPS * STATE:XBC_W]
        dt = _softplus(dt_ref[0, rows, :] + dtb_ref[...])
        da = dt * a_neg
        tri = (ci <= ri) if d == 0 else (ci >= ri)
        cum = jnp.dot(tri.astype(F32), da, precision=HIGHEST, preferred_element_type=F32)
        cum_t = cum.T
        last = cum[Q - 1:Q] if d == 0 else cum[0:1]
        ecum_e = _expand(jnp.exp(cum), e64_ref[d], 2)
        dt_e = _expand(dt, e64_ref[d], 2)
        w_e = _expand(jnp.exp(last - cum) * dt, e64_ref[d], 2)
        colb = _expand(cum, e128_ref[d], 3)
        decay_e = ecum_e[Q - 1:Q] if d == 0 else ecum_e[0:1]

        gmat = [lax.dot_general(cm[:, g * STATE:(g + 1) * STATE], bm[:, g * STATE:(g + 1) * STATE],
                                (((1,), (1,)), ((), ())), preferred_element_type=F32)
                for g in range(GROUPS)]
        xdt = xs * dt_e
        y_parts = []
        for p in range(HEADS // 2):
            g = (2 * p) // (HEADS // GROUPS)
            ms = []
            for hh in (2 * p, 2 * p + 1):
                seg = colb[:, hh * LANES:(hh + 1) * LANES] - cum_t[HEADS * d + hh:HEADS * d + hh + 1, :]
                mm = jnp.where(tri, jnp.exp(jnp.where(tri, seg, 0.0)), 0.0) * gmat[g]
                ms.append(mm.astype(BF16))
            mcat = jnp.concatenate(ms, axis=1)
            xp = xdt[:, p * LANES:(p + 1) * LANES]
            rhs = jnp.concatenate([jnp.where(lane < HEAD_DIM, xp, 0.0).astype(BF16),
                                   jnp.where(lane >= HEAD_DIM, xp, 0.0).astype(BF16)], axis=0)
            y_parts.append(_dot(mcat, rhs))
        y_diag = jnp.concatenate(y_parts, axis=1)

        s_old = s_ref[...]
        s_bf = s_old.astype(BF16)
        y_off = jnp.concatenate(
            [_dot(cm[:, g * STATE:(g + 1) * STATE], s_bf[:, g * gw:(g + 1) * gw]) for g in range(GROUPS)],
            axis=1)
        y = y_diag + y_off * ecum_e

        xw = (xs * w_e).astype(BF16)
        upd = jnp.concatenate(
            [lax.dot_general(bm[:, g * STATE:(g + 1) * STATE], xw[:, g * gw:(g + 1) * gw],
                             (((0,), (0,)), ((), ())), preferred_element_type=F32) for g in range(GROUPS)],
            axis=1)
        s_ref[...] = s_old * decay_e + upd

        if d == 0:
            y_ref[rows, :] = y + dsk_ref[...] * xs
        else:
            tot = y_ref[rows, :] + y
            zz = z_ref[0, rows, :].astype(F32)
            gz = tot * _silu(zz)
            outs = []
            for g in range(GROUPS):
                gg = gz[:, g * gw:(g + 1) * gw]
                outs.append(gg * lax.rsqrt(jnp.mean(gg * gg, axis=-1, keepdims=True) + NORM_EPS))
            o_ref[0, rows, :] = (jnp.concatenate(outs, axis=1) * g_ref[...]).astype(BF16)

    s_ref[...] = h0_ref[0, 0]

    def fwd_body(i, carry):
        chunk(i, 0)
        return carry

    lax.fori_loop(0, nck, fwd_body, 0)

    s_ref[...] = h0_ref[0, 1]

    def bwd_body(i, carry):
        chunk(nck - 1 - i, 1)
        return carry

    lax.fori_loop(0, nck, bwd_body, 0)


def _ssd(xbc3, z3, dt3, h0, cw, cb, dtb, alog, dsk, g, e64, e128):
    bsz = xbc3.shape[0]
    const = lambda *shape: pl.BlockSpec(shape, lambda b: (0,) * len(shape))
    seq = lambda w: pl.BlockSpec((1, SEQ, w), lambda b: (b, 0, 0))
    return pl.pallas_call(
        _ssd_kernel,
        out_shape=jax.ShapeDtypeStruct((bsz, SEQ, D_SSD), BF16),
        grid=(bsz,),
        in_specs=[seq(XBC_W), seq(D_SSD), seq(LANES),
                  pl.BlockSpec((1, 2, STATE, D_SSD), lambda b: (b, 0, 0, 0)),
                  const(3, XBC_W), const(1, XBC_W), const(1, LANES), const(1, LANES),
                  const(1, D_SSD), const(1, D_SSD), const(2, LANES, D_SSD), const(2, LANES, HEADS * LANES)],
        out_specs=seq(D_SSD),
        scratch_shapes=[pltpu.VMEM((SEQ, XBC_W), BF16), pltpu.VMEM((SEQ, D_SSD), F32),
                        pltpu.VMEM((STATE, D_SSD), F32)],
        compiler_params=pltpu.CompilerParams(dimension_semantics=("arbitrary",),
                                             vmem_limit_bytes=VMEM_LIMIT),
        name="ssd",
    )(xbc3, z3, dt3, h0, cw, cb, dtb, alog, dsk, g, e64, e128)


def _outproj_kernel(x_ref, yssd_ref, scb_ref, v_ref, vp_ref, vn_ref, mod_ref, scw_ref, wo1_ref, wo2_ref,
                    g2_ref, wrt_ref, br_ref, x1_ref, lg_ref):
    tm = TOK_TILE
    per_b = SEQ // tm
    i = pl.program_id(0)
    first = (i % per_b) == 0
    last = (i % per_b) == per_b - 1
    m = mod_ref[0]
    v = v_ref[...].astype(F32)
    vp = jnp.where(first, 0.0, vp_ref[...].astype(F32))
    vn = jnp.where(last, 0.0, vn_ref[...].astype(F32))
    dn = jnp.concatenate([vp, v[:tm - GRID_W]], axis=0)
    up = jnp.concatenate([v[GRID_W:], vn], axis=0)
    scw = scw_ref[...]
    ysc = scb_ref[...].astype(F32) * (scw[0:1] * dn + scw[1:2] * v + scw[2:3] * up)
    out = _dot(yssd_ref[...], wo1_ref[...]) + _dot(ysc.astype(BF16), wo2_ref[...])
    x1 = x_ref[...] + m[2:3] * out
    x1_ref[...] = x1
    h2 = _rms(x1) * g2_ref[...] * (1.0 + m[4:5]) + m[3:4]
    lg_ref[...] = lax.dot_general(wrt_ref[...], h2, (((1,), (1,)), ((), ())), precision=HIGHEST,
                                  preferred_element_type=F32) + br_ref[...]


def _outproj(x2, yssd, scb, v, mod3, scw, wo1, wo2, g2, wrt, br):
    t = x2.shape[0]
    tm = TOK_TILE
    per_b = SEQ // tm
    r = tm // GRID_W
    nrow = t // GRID_W
    const = lambda *shape: pl.BlockSpec(shape, lambda i: (0,) * len(shape))
    tile = lambda w: pl.BlockSpec((tm, w), lambda i: (i, 0))
    return pl.pallas_call(
        _outproj_kernel,
        out_shape=(jax.ShapeDtypeStruct((t, D_MODEL), F32), jax.ShapeDtypeStruct((N_EXPERTS, t), F32)),
        grid=(t // tm,),
        in_specs=[tile(D_MODEL), tile(D_SSD), tile(D_SC), tile(D_SC),
                  pl.BlockSpec((GRID_W, D_SC), lambda i: (jnp.maximum(i * r - 1, 0), 0)),
                  pl.BlockSpec((GRID_W, D_SC), lambda i: (jnp.minimum((i + 1) * r, nrow - 1), 0)),
                  pl.BlockSpec((1, N_MOD, D_MODEL), lambda i: (i // per_b, 0, 0)),
                  const(3, D_SC), const(D_SSD, D_MODEL), const(D_SC, D_MODEL), const(1, D_MODEL),
                  const(N_EXPERTS, D_MODEL), const(N_EXPERTS, 1)],
        out_specs=(tile(D_MODEL), pl.BlockSpec((N_EXPERTS, tm), lambda i: (0, i))),
        compiler_params=pltpu.CompilerParams(dimension_semantics=("arbitrary",),
                                             vmem_limit_bytes=VMEM_LIMIT),
        name="outproj",
    )(x2, yssd, scb, v, v, v, mod3, scw, wo1, wo2, g2, wrt, br)


def _route_kernel(lg_ref, dest_ref, gate_ref, idx_ref, meta_ref, rank_ref, carry_ref, *, n_tok, n_blocks):
    tt = RT_TILE
    ne = N_EXPERTS
    eio = lax.broadcasted_iota(jnp.int32, (ne, tt), 0)
    si = lax.broadcasted_iota(jnp.int32, (tt, tt), 0)
    ti = lax.broadcasted_iota(jnp.int32, (tt, tt), 1)
    before = (si < ti).astype(BF16)
    carry_ref[...] = jnp.zeros_like(carry_ref)

    def tile_body(j, c):
        t0 = pl.multiple_of(j * tt, tt)
        l = lg_ref[:, pl.ds(t0, tt)]
        onehot = jnp.zeros((ne, tt), F32)
        tops, sels = [], []
        for _ in range(TOP_K):
            mx = jnp.max(l, axis=0, keepdims=True)
            idx = jnp.min(jnp.where(l == mx, eio, ne), axis=0, keepdims=True)
            sel = eio == idx
            l = jnp.where(sel, -jnp.inf, l)
            onehot = onehot + sel.astype(F32)
            tops.append(mx)
            sels.append(sel)
            idx_ref[pl.ds(len(tops) - 1, 1), pl.ds(t0, tt)] = idx
        ex = [jnp.exp(tv - tops[0]) for tv in tops]
        den = ex[0] + ex[1] + ex[2] + ex[3]
        prefix = _dot(onehot.astype(BF16), before) + carry_ref[:, 0:1]
        for k in range(TOP_K):
            gate_ref[pl.ds(k, 1), pl.ds(t0, tt)] = ex[k] / den
            rk = jnp.sum(jnp.where(sels[k], prefix, 0.0), axis=0, keepdims=True)
            rank_ref[pl.ds(k, 1), pl.ds(t0, tt)] = rk.astype(jnp.int32)
        carry_ref[...] = carry_ref[...] + jnp.sum(onehot, axis=1, keepdims=True)
        return c

    lax.fori_loop(0, n_tok // tt, tile_body, 0)

    counts = carry_ref[...]
    padded = jnp.floor((counts + (MOE_BM - 1)) * (1.0 / MOE_BM)) * MOE_BM
    er = lax.broadcasted_iota(jnp.int32, (ne, ne), 0)
    ec = lax.broadcasted_iota(jnp.int32, (ne, ne), 1)
    pad_start = jnp.dot((ec < er).astype(F32), padded, precision=HIGHEST, preferred_element_type=F32)
    pad_end = pad_start + padded

    def dest_body(j, c):
        t0 = pl.multiple_of(j * tt, tt)
        for k in range(TOP_K):
            idx = idx_ref[pl.ds(k, 1), pl.ds(t0, tt)]
            base = jnp.sum(jnp.where(eio == idx, pad_start[:, 0:1], 0.0), axis=0, keepdims=True)
            dest_ref[pl.ds(k, 1), pl.ds(t0, tt)] = base.astype(jnp.int32) + rank_ref[pl.ds(k, 1), pl.ds(t0, tt)]
        return c

    lax.fori_loop(0, n_tok // tt, dest_body, 0)

    width = meta_ref.shape[1]
    sub = lax.broadcasted_iota(jnp.int32, (ne, width), 0)
    lan = lax.broadcasted_iota(jnp.int32, (ne, width), 1)
    diag = sub == lan
    cnt_row = jnp.sum(jnp.where(diag, counts[:, 0:1], 0.0), axis=0, keepdims=True)
    start_row = jnp.sum(jnp.where(diag, pad_start[:, 0:1], 0.0), axis=0, keepdims=True)
    blk_start = (lan * MOE_BM).astype(F32)
    blk_exp = jnp.sum((pad_end[:, 0:1] <= blk_start).astype(F32), axis=0, keepdims=True)
    blk_exp = jnp.minimum(blk_exp, float(ne - 1))
    used = jnp.sum(padded[:, 0:1], axis=0, keepdims=True) * (1.0 / MOE_BM)
    meta_ref[0:1, :] = cnt_row.astype(jnp.int32)
    meta_ref[1:2, :] = start_row.astype(jnp.int32)
    meta_ref[2:3, :] = blk_exp.astype(jnp.int32)
    meta_ref[3:4, :] = jnp.broadcast_to(used, (1, width)).astype(jnp.int32)
    meta_ref[4:8, :] = jnp.zeros((4, width), jnp.int32)


def _route(lgt, n_blocks):
    ne, n_tok = lgt.shape
    width = -(-n_blocks // LANES) * LANES
    full = lambda *shape: pl.BlockSpec(shape, lambda: (0,) * len(shape))
    return pl.pallas_call(
        functools.partial(_route_kernel, n_tok=n_tok, n_blocks=n_blocks),
        out_shape=(jax.ShapeDtypeStruct((TOP_K, n_tok), jnp.int32),
                   jax.ShapeDtypeStruct((TOP_K, n_tok), F32),
                   jax.ShapeDtypeStruct((TOP_K, n_tok), jnp.int32),
                   jax.ShapeDtypeStruct((8, width), jnp.int32)),
        in_specs=[full(ne, n_tok)],
        out_specs=(full(TOP_K, n_tok), full(TOP_K, n_tok), full(TOP_K, n_tok), full(8, width)),
        scratch_shapes=[pltpu.VMEM((TOP_K, n_tok), jnp.int32), pltpu.VMEM((ne, LANES), F32)],
        compiler_params=pltpu.CompilerParams(vmem_limit_bytes=VMEM_LIMIT),
        name="route",
    )(lgt)


def _dispatch_kernel(dest_ref, cnt_ref, start_ref, nu_ref, x1_ref, meta_ref, mod_ref, g2_ref, zsrc_ref, xs_ref,
                     hbuf, sem, zsem):
    i = pl.program_id(0)
    n = pl.num_programs(0)
    tl = DISP_TILE
    slot = i % 2
    nb = xs_ref.shape[0] // MOE_BM

    def zero_block(b):
        return pltpu.make_async_copy(zsrc_ref, xs_ref.at[pl.ds(b * MOE_BM, MOE_BM)], zsem)

    @pl.when(i == 0)
    def _():
        def start_e(e, c):
            @pl.when(cnt_ref[e] > 0)
            def _():
                zero_block((start_ref[e] + cnt_ref[e] - 1) // MOE_BM).start()
            return c

        def wait_e(e, c):
            @pl.when(cnt_ref[e] > 0)
            def _():
                zero_block(0).wait()
            return c

        def start_t(b, c):
            zero_block(b).start()
            return c

        def wait_t(b, c):
            zero_block(0).wait()
            return c

        lax.fori_loop(0, N_EXPERTS, start_e, 0)
        lax.fori_loop(nu_ref[0], nb, start_t, 0)
        lax.fori_loop(0, N_EXPERTS, wait_e, 0)
        lax.fori_loop(nu_ref[0], nb, wait_t, 0)

    m = mod_ref[0]
    hbuf[slot, :, 0:D_MODEL] = _rms(x1_ref[...]) * g2_ref[...] * (1.0 + m[4:5]) + m[3:4]
    hbuf[slot, :, D_MODEL:ROW_W] = meta_ref[...]

    def per_tok(t, c):
        tok = i * tl + t
        for k in range(TOP_K):
            d = dest_ref[tok * TOP_K + k]
            pltpu.make_async_copy(hbuf.at[slot, pl.ds(t, 1)], xs_ref.at[pl.ds(d, 1)], sem.at[slot]).start()
        return c

    lax.fori_loop(0, tl, per_tok, 0)

    def wait_slot(sl):
        for _ in range(TOP_K):
            pltpu.make_async_copy(hbuf.at[sl], xs_ref.at[pl.ds(0, tl)], sem.at[sl]).wait()

    @pl.when(i > 0)
    def _():
        wait_slot(1 - slot)

    @pl.when(i == n - 1)
    def _():
        wait_slot(slot)


def _dispatch(dest_flat, cnt, start, n_used, x1, meta_rows, mod3, g2, zsrc, n_rows):
    n_tok = x1.shape[0]
    tl = DISP_TILE
    per_b = SEQ // tl
    return pl.pallas_call(
        _dispatch_kernel,
        out_shape=jax.ShapeDtypeStruct((n_rows, ROW_W), F32),
        grid_spec=pltpu.PrefetchScalarGridSpec(
            num_scalar_prefetch=4,
            grid=(n_tok // tl,),
            in_specs=[pl.BlockSpec((tl, D_MODEL), lambda i, *_: (i, 0)),
                      pl.BlockSpec((tl, LANES), lambda i, *_: (i, 0)),
                      pl.BlockSpec((1, N_MOD, D_MODEL), lambda i, *_: (i // per_b, 0, 0)),
                      pl.BlockSpec((1, D_MODEL), lambda i, *_: (0, 0)),
                      pl.BlockSpec((MOE_BM, ROW_W), lambda i, *_: (0, 0))],
            out_specs=pl.BlockSpec(memory_space=pl.ANY),
            scratch_shapes=[pltpu.VMEM((2, tl, ROW_W), F32),
                            pltpu.SemaphoreType.DMA((2,)), pltpu.SemaphoreType.DMA]),
        compiler_params=pltpu.CompilerParams(dimension_semantics=("arbitrary",),
                                             vmem_limit_bytes=VMEM_LIMIT),
        name="dispatch",
    )(dest_flat, cnt, start, n_used, x1, meta_rows, mod3, g2, zsrc)


def _expert_kernel(be_ref, nu_ref, xs_ref, wgu_ref, bgu_ref, wd_ref, bd_ref, zero_ref, ytm_ref,
                   ybuf, idv, ids, sc_sem, id_sem, z_sem, *, n_tok):
    j = pl.program_id(0)
    nbk = pl.num_programs(0) - 1
    nu = nu_ref[0]
    slot = j % 2
    prev = 1 - slot

    @pl.when(j == 0)
    def _():
        cp = pltpu.make_async_copy(zero_ref, ytm_ref.at[pl.ds(TOP_K * n_tok, MOE_BM)], z_sem)
        cp.start()
        cp.wait()

    n_pc = 4
    pw1 = D_FF // n_pc
    pw2 = D_MODEL // n_pc
    groups = [48] * n_pc + [16] * n_pc
    assert sum(groups) == MOE_BM

    def scatter_group(g):
        lo = sum(groups[:g])
        for r in range(lo, lo + groups[g]):
            pltpu.make_async_copy(ybuf.at[prev, pl.ds(r, 1)], ytm_ref.at[pl.ds(ids[prev, 0, r], 1)],
                                  sc_sem.at[prev]).start()

    def scatter_wait(sl):
        pltpu.make_async_copy(ybuf.at[sl], ytm_ref.at[pl.ds(0, MOE_BM)], sc_sem.at[sl]).wait()

    def compute(with_scatter):
        e_f = be_ref[jnp.minimum(j, nbk - 1)].astype(F32)
        xb = xs_ref[:, 0:D_MODEL].astype(BF16)
        meta = xs_ref[:, D_MODEL:ROW_W]
        gate = jnp.zeros((MOE_BM, 1), F32)
        kk = jnp.zeros((MOE_BM, 1), F32)
        for k in range(TOP_K):
            mk = meta[:, META_IDX + k:META_IDX + k + 1] == e_f
            gate = gate + jnp.where(mk, meta[:, META_GATE + k:META_GATE + k + 1], 0.0)
            kk = kk + jnp.where(mk, float(k), 0.0)
        row = kk * float(n_tok) + meta[:, META_TOK:META_TOK + 1]
        row_t = jnp.broadcast_to(row, (MOE_BM, LANES)).T
        idv[slot] = row_t[0:8].astype(jnp.int32)
        pltpu.make_async_copy(idv.at[slot], ids.at[slot], id_sem.at[slot]).start()
        for c in range(n_pc):
            if with_scatter:
                scatter_group(c)
            cg = slice(c * pw1, (c + 1) * pw1)
            cl = slice(D_FF + c * pw1, D_FF + (c + 1) * pw1)
            glu = jnp.minimum(_dot(xb, wgu_ref[0, :, cg]) + bgu_ref[0, :, cg], SWIGLU_LIMIT)
            lin = jnp.clip(_dot(xb, wgu_ref[0, :, cl]) + bgu_ref[0, :, cl], -SWIGLU_LIMIT, SWIGLU_LIMIT)
            ybuf[2, :, cg] = glu * jax.nn.sigmoid(SWIGLU_ALPHA * glu) * (lin + 1.0)
        act = ybuf[2].astype(BF16)
        for c in range(n_pc):
            if with_scatter:
                scatter_group(n_pc + c)
            cs = slice(c * pw2, (c + 1) * pw2)
            ybuf[slot, :, cs] = (_dot(act, wd_ref[0, :, cs]) + bd_ref[0, :, cs]) * gate

    def ids_wait():
        pltpu.make_async_copy(idv.at[prev], ids.at[prev], id_sem.at[prev]).wait()

    @pl.when(jnp.logical_and(j >= 2, j < nu))
    def _():
        scatter_wait(slot)

    @pl.when(j == 0)
    def _():
        compute(False)

    @pl.when(jnp.logical_and(j >= 1, j < nu))
    def _():
        ids_wait()
        compute(True)

    @pl.when(j == nu)
    def _():
        ids_wait()
        for g in range(len(groups)):
            scatter_group(g)
        scatter_wait(prev)

    @pl.when(jnp.logical_and(j == nu, j >= 2))
    def _():
        scatter_wait(slot)


def _experts(blk_exp, n_used, xs, wgu, bgu, wd, bd, n_tok):
    n_rows = xs.shape[0]
    nb = n_rows // MOE_BM
    row_blk = lambda j, be, nu: (jnp.minimum(j, nu[0] - 1), 0)
    per_e = lambda j, be, nu: (be[jnp.minimum(j, nb - 1)], 0, 0)
    zero = jnp.zeros((MOE_BM, D_MODEL), F32)
    return pl.pallas_call(
        functools.partial(_expert_kernel, n_tok=n_tok),
        out_shape=jax.ShapeDtypeStruct((TOP_K * n_tok + MOE_BM, D_MODEL), F32),
        grid_spec=pltpu.PrefetchScalarGridSpec(
            num_scalar_prefetch=2,
            grid=(nb + 1,),
            in_specs=[pl.BlockSpec((MOE_BM, ROW_W), row_blk),
                      pl.BlockSpec((1, D_MODEL, 2 * D_FF), per_e),
                      pl.BlockSpec((1, 1, 2 * D_FF), per_e),
                      pl.BlockSpec((1, D_FF, D_MODEL), per_e),
                      pl.BlockSpec((1, 1, D_MODEL), per_e),
                      pl.BlockSpec((MOE_BM, D_MODEL), lambda j, be, nu: (0, 0))],
            out_specs=pl.BlockSpec(memory_space=pl.ANY),
            scratch_shapes=[pltpu.VMEM((3, MOE_BM, D_MODEL), F32), pltpu.VMEM((2, 8, MOE_BM), jnp.int32),
                            pltpu.SMEM((2, 8, MOE_BM), jnp.int32), pltpu.SemaphoreType.DMA((2,)),
                            pltpu.SemaphoreType.DMA((2,)), pltpu.SemaphoreType.DMA]),
        compiler_params=pltpu.CompilerParams(dimension_semantics=("arbitrary",),
                                             vmem_limit_bytes=VMEM_LIMIT),
        name="experts",
    )(blk_exp, n_used, xs, wgu, bgu, wd, bd, zero)


def _combine_kernel(y0_ref, y1_ref, y2_ref, y3_ref, x1_ref, mod_ref, fg_ref, o_ref):
    m = mod_ref[0]
    moe = (y0_ref[...] + y1_ref[...]) + (y2_ref[...] + y3_ref[...])
    x2 = x1_ref[...] + m[5:6] * moe
    o_ref[...] = _rms(x2) * fg_ref[...]


def _combine(ytm, x1, mod3, fg):
    n_tok = x1.shape[0]
    tc = COMB_TILE
    per_b = SEQ // tc
    nt = n_tok // tc
    slot_spec = lambda k: pl.BlockSpec((tc, D_MODEL), lambda i: (i + k * nt, 0))
    return pl.pallas_call(
        _combine_kernel,
        out_shape=jax.ShapeDtypeStruct((n_tok, D_MODEL), F32),
        grid=(nt,),
        in_specs=[slot_spec(0), slot_spec(1), slot_spec(2), slot_spec(3),
                  pl.BlockSpec((tc, D_MODEL), lambda i: (i, 0)),
                  pl.BlockSpec((1, N_MOD, D_MODEL), lambda i: (i // per_b, 0, 0)),
                  pl.BlockSpec((1, D_MODEL), lambda i: (0, 0))],
        out_specs=pl.BlockSpec((tc, D_MODEL), lambda i: (i, 0)),
        compiler_params=pltpu.CompilerParams(dimension_semantics=("arbitrary",),
                                             vmem_limit_bytes=VMEM_LIMIT),
        name="combine",
    )(ytm, ytm, ytm, ytm, x1, mod3, fg)


def _expansion_matrices():
    r = jnp.arange(LANES)[:, None]
    out64, out128 = [], []
    for d in range(2):
        l64 = jnp.arange(D_SSD)[None, :]
        l128 = jnp.arange(HEADS * LANES)[None, :]
        out64.append((l64 // HEAD_DIM == r - HEADS * d).astype(BF16))
        out128.append((l128 // LANES == r - HEADS * d).astype(BF16))
    return jnp.stack(out64), jnp.stack(out128)


def _pad_lanes(v):
    return jnp.pad(v, [(0, 0)] * (v.ndim - 1) + [(0, LANES - v.shape[-1])])


def kernel(x, c, ctx, c_ctx, w_mod, b_mod, norm1_g, w_in, ssd_conv_w, ssd_conv_b, ssd_dt_bias, ssd_a_log,
           ssd_d, ssd_norm_g, sc_conv_w, w_out, norm2_g, w_router, b_router, w_gate_up, b_gate_up, w_down,
           b_down, final_g):
    bsz = x.shape[0]
    n_tok = bsz * SEQ
    n_assign = n_tok * TOP_K
    n_blocks = n_assign // MOE_BM + N_EXPERTS
    n_rows = n_blocks * MOE_BM
    li = 0

    cvec = jnp.concatenate([c, c_ctx[None, :], jnp.zeros((7, D_MODEL), F32)], axis=0)
    mod3 = _mod(cvec, w_mod[li], b_mod[li][None, :]).reshape(bsz + 8, N_MOD, D_MODEL)

    w = w_in[li]
    wz = w[:, Z0:X0].astype(BF16)
    wxbc = w[:, X0:DT0].astype(BF16)
    wdt = _pad_lanes(w[:, DT0:SC0]).astype(BF16)
    wb = w[:, SC0:SC0 + D_SC].astype(BF16)
    wc = w[:, SC0 + D_SC:SC0 + 2 * D_SC].astype(BF16)
    wu = w[:, SC0 + 2 * D_SC:].astype(BF16)
    g1 = norm1_g[li][None, :]
    cw = ssd_conv_w[li]
    cb = ssd_conv_b[li][None, :]
    dtb = _pad_lanes(ssd_dt_bias[li].reshape(1, 2 * HEADS))
    alog = _pad_lanes(ssd_a_log[li].reshape(1, 2 * HEADS))
    e64, e128 = _expansion_matrices()

    h0 = _ctx_states(ctx, mod3, g1, wxbc[:, :XB_W], wdt, cw[:, :XB_W], cb[:, :XB_W], dtb, alog, e64)

    x2 = x.reshape(n_tok, D_MODEL)
    z, xbc, dtr, scb, v = _inproj(x2, mod3, g1, wz, wxbc, wdt, wb, wc, wu)

    dsk = jnp.repeat(ssd_d[li], HEAD_DIM)[None, :]
    yssd = _ssd(xbc.reshape(bsz, SEQ, XBC_W), z.reshape(bsz, SEQ, D_SSD), dtr.reshape(bsz, SEQ, LANES), h0,
                cw, cb, dtb, alog, dsk, ssd_norm_g[li][None, :], e64, e128)

    wo = w_out[li].astype(BF16)
    g2 = norm2_g[li][None, :]
    x1, lgt = _outproj(x2, yssd.reshape(n_tok, D_SSD), scb, v, mod3, sc_conv_w[li], wo[:D_SSD], wo[D_SSD:],
                       g2, w_router[li].T, b_router[li][:, None])

    dest_t, gate_t, idx_t, meta = _route(lgt, n_blocks)
    dest_flat = dest_t.T.reshape(n_assign)
    cnt = meta[0, :N_EXPERTS]
    start = meta[1, :N_EXPERTS]
    blk_exp = meta[2, :n_blocks]
    n_used = meta[3, :1]

    meta_rows = _pad_lanes(jnp.concatenate(
        [idx_t.T.astype(F32), gate_t.T, jnp.arange(n_tok, dtype=F32)[:, None]], axis=1))
    pad_meta = _pad_lanes(jnp.concatenate(
        [jnp.full((MOE_BM, TOP_K), -1.0, F32), jnp.zeros((MOE_BM, TOP_K), F32),
         (TOP_K * n_tok + jnp.arange(MOE_BM, dtype=F32))[:, None]], axis=1))
    zsrc = jnp.concatenate([jnp.zeros((MOE_BM, D_MODEL), F32), pad_meta], axis=1)

    xs = _dispatch(dest_flat, cnt, start, n_used, x1, meta_rows, mod3, g2, zsrc, n_rows)
    ytm = _experts(blk_exp, n_used, xs, w_gate_up[li].astype(BF16), b_gate_up[li][:, None, :],
                   w_down[li].astype(BF16), b_down[li][:, None, :], n_tok)
    out = _combine(ytm, x1, mod3, final_g[None, :])
    return out.reshape(bsz, SEQ, D_MODEL)
```

```python
import functools

import jax
import jax.numpy as jnp
from jax import lax
from jax.experimental import pallas as pl
from jax.experimental.pallas import tpu as pltpu

F32 = jnp.float32
BF16 = jnp.bfloat16
HIGHEST = lax.Precision.HIGHEST

D_MODEL = 1024
SEQ = 2048
CTX_LEN = 256
GRID_W = 64
D_SSD = 1024
D_SC = 1024
HEAD_DIM = 64
HEADS = 16
GROUPS = 2
STATE = 128
CHUNK = 128
N_EXPERTS = 32
TOP_K = 4
D_FF = 1024
SWIGLU_LIMIT = 7.0
SWIGLU_ALPHA = 1.702
NORM_EPS = 1e-6
N_MOD = 6
XBC_W = D_SSD + 2 * GROUPS * STATE
XB_W = D_SSD + GROUPS * STATE
LANES = 128

Z0 = 0
X0 = Z0 + D_SSD
B0 = X0 + D_SSD
C0 = B0 + GROUPS * STATE
DT0 = C0 + GROUPS * STATE
SC0 = DT0 + 2 * HEADS

TOK_TILE = 512
MOE_BM = 256
RT_TILE = 512
DISP_TILE = 256
COMB_TILE = 512
ROW_W = D_MODEL + LANES
META_IDX = 0
META_GATE = TOP_K
META_TOK = 2 * TOP_K
VMEM_LIMIT = 56 * 1024 * 1024


def _silu(v):
    return v * jax.nn.sigmoid(v)


def _softplus(v):
    return jnp.maximum(v, 0.0) + jnp.log1p(jnp.exp(-jnp.abs(v)))


def _rms(v):
    return v * lax.rsqrt(jnp.mean(v * v, axis=-1, keepdims=True) + NORM_EPS)


def _dot(a, b):
    return jnp.dot(a, b, preferred_element_type=F32)


def _expand(v, e, pieces):
    acc = None
    rem = v
    for _ in range(pieces):
        p = rem.astype(BF16)
        rem = rem - p.astype(F32)
        t = _dot(p, e)
        acc = t if acc is None else acc + t
    return acc


def _mod_kernel(c_ref, w_ref, b_ref, o_ref):
    o_ref[...] = jnp.dot(_silu(c_ref[...]), w_ref[...], precision=HIGHEST,
                         preferred_element_type=F32) + b_ref[...]


def _mod(cvec, w_mod, b_mod):
    rows = cvec.shape[0]
    n = w_mod.shape[1]
    tn = 1536
    return pl.pallas_call(
        _mod_kernel,
        out_shape=jax.ShapeDtypeStruct((rows, n), F32),
        grid=(n // tn,),
        in_specs=[pl.BlockSpec((rows, D_MODEL), lambda j: (0, 0)),
                  pl.BlockSpec((D_MODEL, tn), lambda j: (0, j)),
                  pl.BlockSpec((1, tn), lambda j: (0, j))],
        out_specs=pl.BlockSpec((rows, tn), lambda j: (0, j)),
        compiler_params=pltpu.CompilerParams(dimension_semantics=("arbitrary",),
                                             vmem_limit_bytes=VMEM_LIMIT),
        name="mod",
    )(cvec, w_mod, b_mod)


def _ctx_kernel(ctx_ref, mod_ref, g1_ref, wxb_ref, wdt_ref, cw_ref, cb_ref, dtb_ref, alog_ref, e64_ref,
                h0_ref):
    L = CTX_LEN
    m = mod_ref[0]
    hc = _rms(ctx_ref[0]) * g1_ref[...] * (1.0 + m[1:2]) + m[0:1]
    hb = hc.astype(BF16)
    pxb = _dot(hb, wxb_ref[...])
    dtr = _dot(hb, wdt_ref[...])
    rowi = lax.broadcasted_iota(jnp.int32, (L, XB_W), 0)
    dn = jnp.where(rowi == 0, 0.0, pltpu.roll(pxb, 1, 0))
    up = jnp.where(rowi == L - 1, 0.0, pltpu.roll(pxb, L - 1, 0))
    cw = cw_ref[...]
    xb = _silu(cw[0:1] * dn + cw[1:2] * pxb + cw[2:3] * up + cb_ref[...])
    xs = xb[:, :D_SSD]
    bm = xb[:, D_SSD:].astype(BF16)
    dt = _softplus(dtr + dtb_ref[...])
    da = dt * (-jnp.exp(alog_ref[...]))
    ri = lax.broadcasted_iota(jnp.int32, (L, L), 0)
    ci = lax.broadcasted_iota(jnp.int32, (L, L), 1)
    for d in range(2):
        tri = (ci <= ri) if d == 0 else (ci >= ri)
        cum = jnp.dot(tri.astype(F32), da, precision=HIGHEST, preferred_element_type=F32)
        last = cum[L - 1:L] if d == 0 else cum[0:1]
        w_e = _expand(jnp.exp(last - cum) * dt, e64_ref[d], 2)
        xw = (xs * w_e).astype(BF16)
        for g in range(GROUPS):
            gw = D_SSD // GROUPS
            st = lax.dot_general(bm[:, g * STATE:(g + 1) * STATE], xw[:, g * gw:(g + 1) * gw],
                                 (((0,), (0,)), ((), ())), preferred_element_type=F32)
            h0_ref[0, d, :, g * gw:(g + 1) * gw] = st


def _ctx_states(ctx, mod3, g1, wxb, wdt, cw, cb, dtb, alog, e64):
    bsz = ctx.shape[0]
    mod_row = bsz
    const = lambda *shape: pl.BlockSpec(shape, lambda b: (0,) * len(shape))
    return pl.pallas_call(
        _ctx_kernel,
        out_shape=jax.ShapeDtypeStruct((bsz, 2, STATE, D_SSD), F32),
        grid=(bsz,),
        in_specs=[pl.BlockSpec((1, CTX_LEN, D_MODEL), lambda b: (b, 0, 0)),
                  pl.BlockSpec((1, N_MOD, D_MODEL), lambda b: (mod_row, 0, 0)),
                  const(1, D_MODEL), const(D_MODEL, XB_W), const(D_MODEL, LANES),
                  const(3, XB_W), const(1, XB_W), const(1, LANES), const(1, LANES),
                  const(2, LANES, D_SSD)],
        out_specs=pl.BlockSpec((1, 2, STATE, D_SSD), lambda b: (b, 0, 0, 0)),
        compiler_params=pltpu.CompilerParams(dimension_semantics=("arbitrary",),
                                             vmem_limit_bytes=VMEM_LIMIT),
        name="ctx_states",
    )(ctx, mod3, g1, wxb, wdt, cw, cb, dtb, alog, e64)


def _inproj_kernel(x_ref, mod_ref, g1_ref, wz_ref, wxbc_ref, wdt_ref, wb_ref, wc_ref, wu_ref,
                   z_ref, xbc_ref, dt_ref, scb_ref, v_ref):
    m = mod_ref[0]
    hx = _rms(x_ref[...]) * g1_ref[...] * (1.0 + m[1:2]) + m[0:1]
    hb = hx.astype(BF16)
    z_ref[...] = _dot(hb, wz_ref[...]).astype(BF16)
    xbc_ref[...] = _dot(hb, wxbc_ref[...]).astype(BF16)
    dt_ref[...] = _dot(hb, wdt_ref[...])
    scb_ref[...] = _dot(hb, wb_ref[...]).astype(BF16)
    v_ref[...] = (_dot(hb, wc_ref[...]) * _dot(hb, wu_ref[...])).astype(BF16)


def _inproj(x2, mod3, g1, wz, wxbc, wdt, wb, wc, wu):
    t = x2.shape[0]
    tm = TOK_TILE
    per_b = SEQ // tm
    const = lambda *shape: pl.BlockSpec(shape, lambda i: (0,) * len(shape))
    tile = lambda w: pl.BlockSpec((tm, w), lambda i: (i, 0))
    return pl.pallas_call(
        _inproj_kernel,
        out_shape=(jax.ShapeDtypeStruct((t, D_SSD), BF16), jax.ShapeDtypeStruct((t, XBC_W), BF16),
                   jax.ShapeDtypeStruct((t, LANES), F32), jax.ShapeDtypeStruct((t, D_SC), BF16),
                   jax.ShapeDtypeStruct((t, D_SC), BF16)),
        grid=(t // tm,),
        in_specs=[tile(D_MODEL),
                  pl.BlockSpec((1, N_MOD, D_MODEL), lambda i: (i // per_b, 0, 0)),
                  const(1, D_MODEL), const(D_MODEL, D_SSD), const(D_MODEL, XBC_W), const(D_MODEL, LANES),
                  const(D_MODEL, D_SC), const(D_MODEL, D_SC), const(D_MODEL, D_SC)],
        out_specs=(tile(D_SSD), tile(XBC_W), tile(LANES), tile(D_SC), tile(D_SC)),
        compiler_params=pltpu.CompilerParams(dimension_semantics=("arbitrary",),
                                             vmem_limit_bytes=VMEM_LIMIT),
        name="inproj",
    )(x2, mod3, g1, wz, wxbc, wdt, wb, wc, wu)


def _ssd_kernel(xbc_ref, z_ref, dt_ref, h0_ref, cw_ref, cb_ref, dtb_ref, alog_ref, dsk_ref, g_ref,
                e64_ref, e128_ref, o_ref, xc_ref, y_ref, s_ref):
    Q = CHUNK
    nck = SEQ // Q
    gw = D_SSD // GROUPS

    rowi = lax.broadcasted_iota(jnp.int32, (Q, XBC_W), 0)

    def conv_body(c, carry):
        r0 = pl.multiple_of(c * Q, Q)
        main = xbc_ref[0, pl.ds(r0, Q), :].astype(F32)
        pstart = pl.multiple_of(jnp.maximum(r0 - 16, 0), 16)
        nstart = pl.multiple_of(jnp.minimum(r0 + Q, SEQ - 16), 16)
        prev = xbc_ref[0, pl.ds(pstart, 16), :].astype(F32)[15:16]
        nxt = xbc_ref[0, pl.ds(nstart, 16), :].astype(F32)[0:1]
        prev = jnp.where(c > 0, prev, 0.0)
        nxt = jnp.where(c < nck - 1, nxt, 0.0)
        dn = jnp.where(rowi == 0, prev, pltpu.roll(main, 1, 0))
        up = jnp.where(rowi == Q - 1, nxt, pltpu.roll(main, Q - 1, 0))
        cw = cw_ref[...]
        conv = cw[0:1] * dn + cw[1:2] * main + cw[2:3] * up + cb_ref[...]
        xc_ref[pl.ds(r0, Q), :] = _silu(conv).astype(BF16)
        return carry

    lax.fori_loop(0, nck, conv_body, 0)

    ri = lax.broadcasted_iota(jnp.int32, (Q, Q), 0)
    ci = lax.broadcasted_iota(jnp.int32, (Q, Q), 1)
    lane = lax.broadcasted_iota(jnp.int32, (Q, LANES), 1)
    a_neg = -jnp.exp(alog_ref[...])

    def chunk(c, d):
        r0 = pl.multiple_of(c * Q, Q)
        rows = pl.ds(r0, Q)
        xs = xc_ref[rows, 0:D_SSD].astype(F32)
        bm = xc_ref[rows, D_SSD:D_SSD + GROUPS * STATE]
        cm = xc_ref[rows, D_SSD + GROUPS * STATE:XBC_W]
        dt = _softplus(dt_ref[0, rows, :] + dtb_ref[...])
        da = dt * a_neg
        tri = (ci <= ri) if d == 0 else (ci >= ri)
        cum = jnp.dot(tri.astype(F32), da, precision=HIGHEST, preferred_element_type=F32)
        cum_t = cum.T
        last = cum[Q - 1:Q] if d == 0 else cum[0:1]
        ecum_e = _expand(jnp.exp(cum), e64_ref[d], 2)
        dt_e = _expand(dt, e64_ref[d], 2)
        w_e = _expand(jnp.exp(last - cum) * dt, e64_ref[d], 2)
        colb = _expand(cum, e128_ref[d], 3)
        decay_e = ecum_e[Q - 1:Q] if d == 0 else ecum_e[0:1]

        gmat = [lax.dot_general(cm[:, g * STATE:(g + 1) * STATE], bm[:, g * STATE:(g + 1) * STATE],
                                (((1,), (1,)), ((), ())), preferred_element_type=F32)
                for g in range(GROUPS)]
        xdt = xs * dt_e
        y_parts = []
        for p in range(HEADS // 2):
            g = (2 * p) // (HEADS // GROUPS)
            ms = []
            for hh in (2 * p, 2 * p + 1):
                seg = colb[:, hh * LANES:(hh + 1) * LANES] - cum_t[HEADS * d + hh:HEADS * d + hh + 1, :]
                mm = jnp.where(tri, jnp.exp(jnp.where(tri, seg, 0.0)), 0.0) * gmat[g]
                ms.append(mm.astype(BF16))
            mcat = jnp.concatenate(ms, axis=1)
            xp = xdt[:, p * LANES:(p + 1) * LANES]
            rhs = jnp.concatenate([jnp.where(lane < HEAD_DIM, xp, 0.0).astype(BF16),
                                   jnp.where(lane >= HEAD_DIM, xp, 0.0).astype(BF16)], axis=0)
            y_parts.append(_dot(mcat, rhs))
        y_diag = jnp.concatenate(y_parts, axis=1)

        s_old = s_ref[...]
        s_bf = s_old.astype(BF16)
        y_off = jnp.concatenate(
            [_dot(cm[:, g * STATE:(g + 1) * STATE], s_bf[:, g * gw:(g + 1) * gw]) for g in range(GROUPS)],
            axis=1)
        y = y_diag + y_off * ecum_e

        xw = (xs * w_e).astype(BF16)
        upd = jnp.concatenate(
            [lax.dot_general(bm[:, g * STATE:(g + 1) * STATE], xw[:, g * gw:(g + 1) * gw],
                             (((0,), (0,)), ((), ())), preferred_element_type=F32) for g in range(GROUPS)],
            axis=1)
        s_ref[...] = s_old * decay_e + upd

        if d == 0:
            y_ref[rows, :] = y + dsk_ref[...] * xs
        else:
            tot = y_ref[rows, :] + y
            zz = z_ref[0, rows, :].astype(F32)
            gz = tot * _silu(zz)
            outs = []
            for g in range(GROUPS):
                gg = gz[:, g * gw:(g + 1) * gw]
                outs.append(gg * lax.rsqrt(jnp.mean(gg * gg, axis=-1, keepdims=True) + NORM_EPS))
            o_ref[0, rows, :] = (jnp.concatenate(outs, axis=1) * g_ref[...]).astype(BF16)

    s_ref[...] = h0_ref[0, 0]

    def fwd_body(i, carry):
        chunk(i, 0)
        return carry

    lax.fori_loop(0, nck, fwd_body, 0)

    s_ref[...] = h0_ref[0, 1]

    def bwd_body(i, carry):
        chunk(nck - 1 - i, 1)
        return carry

    lax.fori_loop(0, nck, bwd_body, 0)


def _ssd(xbc3, z3, dt3, h0, cw, cb, dtb, alog, dsk, g, e64, e128):
    bsz = xbc3.shape[0]
    const = lambda *shape: pl.BlockSpec(shape, lambda b: (0,) * len(shape))
    seq = lambda w: pl.BlockSpec((1, SEQ, w), lambda b: (b, 0, 0))
    return pl.pallas_call(
        _ssd_kernel,
        out_shape=jax.ShapeDtypeStruct((bsz, SEQ, D_SSD), BF16),
        grid=(bsz,),
        in_specs=[seq(XBC_W), seq(D_SSD), seq(LANES),
                  pl.BlockSpec((1, 2, STATE, D_SSD), lambda b: (b, 0, 0, 0)),
                  const(3, XBC_W), const(1, XBC_W), const(1, LANES), const(1, LANES),
                  const(1, D_SSD), const(1, D_SSD), const(2, LANES, D_SSD), const(2, LANES, HEADS * LANES)],
        out_specs=seq(D_SSD),
        scratch_shapes=[pltpu.VMEM((SEQ, XBC_W), BF16), pltpu.VMEM((SEQ, D_SSD), F32),
                        pltpu.VMEM((STATE, D_SSD), F32)],
        compiler_params=pltpu.CompilerParams(dimension_semantics=("arbitrary",),
                                             vmem_limit_bytes=VMEM_LIMIT),
        name="ssd",
    )(xbc3, z3, dt3, h0, cw, cb, dtb, alog, dsk, g, e64, e128)


def _outproj_kernel(x_ref, yssd_ref, scb_ref, v_ref, vp_ref, vn_ref, mod_ref, scw_ref, wo1_ref, wo2_ref,
                    g2_ref, wrt_ref, br_ref, x1_ref, lg_ref):
    tm = TOK_TILE
    per_b = SEQ // tm
    i = pl.program_id(0)
    first = (i % per_b) == 0
    last = (i % per_b) == per_b - 1
    m = mod_ref[0]
    v = v_ref[...].astype(F32)
    vp = jnp.where(first, 0.0, vp_ref[...].astype(F32))
    vn = jnp.where(last, 0.0, vn_ref[...].astype(F32))
    dn = jnp.concatenate([vp, v[:tm - GRID_W]], axis=0)
    up = jnp.concatenate([v[GRID_W:], vn], axis=0)
    scw = scw_ref[...]
    ysc = scb_ref[...].astype(F32) * (scw[0:1] * dn + scw[1:2] * v + scw[2:3] * up)
    out = _dot(yssd_ref[...], wo1_ref[...]) + _dot(ysc.astype(BF16), wo2_ref[...])
    x1 = x_ref[...] + m[2:3] * out
    x1_ref[...] = x1
    h2 = _rms(x1) * g2_ref[...] * (1.0 + m[4:5]) + m[3:4]
    lg_ref[...] = lax.dot_general(wrt_ref[...], h2, (((1,), (1,)), ((), ())), precision=HIGHEST,
                                  preferred_element_type=F32) + br_ref[...]


def _outproj(x2, yssd, scb, v, mod3, scw, wo1, wo2, g2, wrt, br):
    t = x2.shape[0]
    tm = TOK_TILE
    per_b = SEQ // tm
    r = tm // GRID_W
    nrow = t // GRID_W
    const = lambda *shape: pl.BlockSpec(shape, lambda i: (0,) * len(shape))
    tile = lambda w: pl.BlockSpec((tm, w), lambda i: (i, 0))
    return pl.pallas_call(
        _outproj_kernel,
        out_shape=(jax.ShapeDtypeStruct((t, D_MODEL), F32), jax.ShapeDtypeStruct((N_EXPERTS, t), F32)),
        grid=(t // tm,),
        in_specs=[tile(D_MODEL), tile(D_SSD), tile(D_SC), tile(D_SC),
                  pl.BlockSpec((GRID_W, D_SC), lambda i: (jnp.maximum(i * r - 1, 0), 0)),
                  pl.BlockSpec((GRID_W, D_SC), lambda i: (jnp.minimum((i + 1) * r, nrow - 1), 0)),
                  pl.BlockSpec((1, N_MOD, D_MODEL), lambda i: (i // per_b, 0, 0)),
                  const(3, D_SC), const(D_SSD, D_MODEL), const(D_SC, D_MODEL), const(1, D_MODEL),
                  const(N_EXPERTS, D_MODEL), const(N_EXPERTS, 1)],
        out_specs=(tile(D_MODEL), pl.BlockSpec((N_EXPERTS, tm), lambda i: (0, i))),
        compiler_params=pltpu.CompilerParams(dimension_semantics=("arbitrary",),
                                             vmem_limit_bytes=VMEM_LIMIT),
        name="outproj",
    )(x2, yssd, scb, v, v, v, mod3, scw, wo1, wo2, g2, wrt, br)


def _route_kernel(lg_ref, dest_ref, gate_ref, idx_ref, meta_ref, rank_ref, carry_ref, *, n_tok, n_blocks):
    tt = RT_TILE
    ne = N_EXPERTS
    eio = lax.broadcasted_iota(jnp.int32, (ne, tt), 0)
    si = lax.broadcasted_iota(jnp.int32, (tt, tt), 0)
    ti = lax.broadcasted_iota(jnp.int32, (tt, tt), 1)
    before = (si < ti).astype(BF16)
    carry_ref[...] = jnp.zeros_like(carry_ref)

    def tile_body(j, c):
        t0 = pl.multiple_of(j * tt, tt)
        l = lg_ref[:, pl.ds(t0, tt)]
        onehot = jnp.zeros((ne, tt), F32)
        tops, sels = [], []
        for _ in range(TOP_K):
            mx = jnp.max(l, axis=0, keepdims=True)
            idx = jnp.min(jnp.where(l == mx, eio, ne), axis=0, keepdims=True)
            sel = eio == idx
            l = jnp.where(sel, -jnp.inf, l)
            onehot = onehot + sel.astype(F32)
            tops.append(mx)
            sels.append(sel)
            idx_ref[pl.ds(len(tops) - 1, 1), pl.ds(t0, tt)] = idx
        ex = [jnp.exp(tv - tops[0]) for tv in tops]
        den = ex[0] + ex[1] + ex[2] + ex[3]
        prefix = _dot(onehot.astype(BF16), before) + carry_ref[:, 0:1]
        for k in range(TOP_K):
            gate_ref[pl.ds(k, 1), pl.ds(t0, tt)] = ex[k] / den
            rk = jnp.sum(jnp.where(sels[k], prefix, 0.0), axis=0, keepdims=True)
            rank_ref[pl.ds(k, 1), pl.ds(t0, tt)] = rk.astype(jnp.int32)
        carry_ref[...] = carry_ref[...] + jnp.sum(onehot, axis=1, keepdims=True)
        return c

    lax.fori_loop(0, n_tok // tt, tile_body, 0)

    counts = carry_ref[...]
    padded = jnp.floor((counts + (MOE_BM - 1)) * (1.0 / MOE_BM)) * MOE_BM
    er = lax.broadcasted_iota(jnp.int32, (ne, ne), 0)
    ec = lax.broadcasted_iota(jnp.int32, (ne, ne), 1)
    pad_start = jnp.dot((ec < er).astype(F32), padded, precision=HIGHEST, preferred_element_type=F32)
    pad_end = pad_start + padded

    def dest_body(j, c):
        t0 = pl.multiple_of(j * tt, tt)
        for k in range(TOP_K):
            idx = idx_ref[pl.ds(k, 1), pl.ds(t0, tt)]
            base = jnp.sum(jnp.where(eio == idx, pad_start[:, 0:1], 0.0), axis=0, keepdims=True)
            dest_ref[pl.ds(k, 1), pl.ds(t0, tt)] = base.astype(jnp.int32) + rank_ref[pl.ds(k, 1), pl.ds(t0, tt)]
        return c

    lax.fori_loop(0, n_tok // tt, dest_body, 0)

    width = meta_ref.shape[1]
    sub = lax.broadcasted_iota(jnp.int32, (ne, width), 0)
    lan = lax.broadcasted_iota(jnp.int32, (ne, width), 1)
    diag = sub == lan
    cnt_row = jnp.sum(jnp.where(diag, counts[:, 0:1], 0.0), axis=0, keepdims=True)
    start_row = jnp.sum(jnp.where(diag, pad_start[:, 0:1], 0.0), axis=0, keepdims=True)
    blk_start = (lan * MOE_BM).astype(F32)
    blk_exp = jnp.sum((pad_end[:, 0:1] <= blk_start).astype(F32), axis=0, keepdims=True)
    blk_exp = jnp.minimum(blk_exp, float(ne - 1))
    used = jnp.sum(padded[:, 0:1], axis=0, keepdims=True) * (1.0 / MOE_BM)
    meta_ref[0:1, :] = cnt_row.astype(jnp.int32)
    meta_ref[1:2, :] = start_row.astype(jnp.int32)
    meta_ref[2:3, :] = blk_exp.astype(jnp.int32)
    meta_ref[3:4, :] = jnp.broadcast_to(used, (1, width)).astype(jnp.int32)
    meta_ref[4:8, :] = jnp.zeros((4, width), jnp.int32)


def _route(lgt, n_blocks):
    ne, n_tok = lgt.shape
    width = -(-n_blocks // LANES) * LANES
    full = lambda *shape: pl.BlockSpec(shape, lambda: (0,) * len(shape))
    return pl.pallas_call(
        functools.partial(_route_kernel, n_tok=n_tok, n_blocks=n_blocks),
        out_shape=(jax.ShapeDtypeStruct((TOP_K, n_tok), jnp.int32),
                   jax.ShapeDtypeStruct((TOP_K, n_tok), F32),
                   jax.ShapeDtypeStruct((TOP_K, n_tok), jnp.int32),
                   jax.ShapeDtypeStruct((8, width), jnp.int32)),
        in_specs=[full(ne, n_tok)],
        out_specs=(full(TOP_K, n_tok), full(TOP_K, n_tok), full(TOP_K, n_tok), full(8, width)),
        scratch_shapes=[pltpu.VMEM((TOP_K, n_tok), jnp.int32), pltpu.VMEM((ne, LANES), F32)],
        compiler_params=pltpu.CompilerParams(vmem_limit_bytes=VMEM_LIMIT),
        name="route",
    )(lgt)


def _dispatch_kernel(dest_ref, cnt_ref, start_ref, nu_ref, x1_ref, meta_ref, mod_ref, g2_ref, zsrc_ref, xs_ref,
                     hbuf, sem, zsem):
    i = pl.program_id(0)
    n = pl.num_programs(0)
    tl = DISP_TILE
    slot = i % 2
    nb = xs_ref.shape[0] // MOE_BM

    def zero_block(b):
        return pltpu.make_async_copy(zsrc_ref, xs_ref.at[pl.ds(b * MOE_BM, MOE_BM)], zsem)

    @pl.when(i == 0)
    def _():
        def start_e(e, c):
            @pl.when(cnt_ref[e] > 0)
            def _():
                zero_block((start_ref[e] + cnt_ref[e] - 1) // MOE_BM).start()
            return c

        def wait_e(e, c):
            @pl.when(cnt_ref[e] > 0)
            def _():
                zero_block(0).wait()
            return c

        def start_t(b, c):
            zero_block(b).start()
            return c

        def wait_t(b, c):
            zero_block(0).wait()
            return c

        lax.fori_loop(0, N_EXPERTS, start_e, 0)
        lax.fori_loop(nu_ref[0], nb, start_t, 0)
        lax.fori_loop(0, N_EXPERTS, wait_e, 0)
        lax.fori_loop(nu_ref[0], nb, wait_t, 0)

    m = mod_ref[0]
    hbuf[slot, :, 0:D_MODEL] = _rms(x1_ref[...]) * g2_ref[...] * (1.0 + m[4:5]) + m[3:4]
    hbuf[slot, :, D_MODEL:ROW_W] = meta_ref[...]

    def per_tok(t, c):
        tok = i * tl + t
        for k in range(TOP_K):
            d = dest_ref[tok * TOP_K + k]
            pltpu.make_async_copy(hbuf.at[slot, pl.ds(t, 1)], xs_ref.at[pl.ds(d, 1)], sem.at[slot]).start()
        return c

    lax.fori_loop(0, tl, per_tok, 0)

    def wait_slot(sl):
        for _ in range(TOP_K):
            pltpu.make_async_copy(hbuf.at[sl], xs_ref.at[pl.ds(0, tl)], sem.at[sl]).wait()

    @pl.when(i > 0)
    def _():
        wait_slot(1 - slot)

    @pl.when(i == n - 1)
    def _():
        wait_slot(slot)


def _dispatch(dest_flat, cnt, start, n_used, x1, meta_rows, mod3, g2, zsrc, n_rows):
    n_tok = x1.shape[0]
    tl = DISP_TILE
    per_b = SEQ // tl
    return pl.pallas_call(
        _dispatch_kernel,
        out_shape=jax.ShapeDtypeStruct((n_rows, ROW_W), F32),
        grid_spec=pltpu.PrefetchScalarGridSpec(
            num_scalar_prefetch=4,
            grid=(n_tok // tl,),
            in_specs=[pl.BlockSpec((tl, D_MODEL), lambda i, *_: (i, 0)),
                      pl.BlockSpec((tl, LANES), lambda i, *_: (i, 0)),
                      pl.BlockSpec((1, N_MOD, D_MODEL), lambda i, *_: (i // per_b, 0, 0)),
                      pl.BlockSpec((1, D_MODEL), lambda i, *_: (0, 0)),
                      pl.BlockSpec((MOE_BM, ROW_W), lambda i, *_: (0, 0))],
            out_specs=pl.BlockSpec(memory_space=pl.ANY),
            scratch_shapes=[pltpu.VMEM((2, tl, ROW_W), F32),
                            pltpu.SemaphoreType.DMA((2,)), pltpu.SemaphoreType.DMA]),
        compiler_params=pltpu.CompilerParams(dimension_semantics=("arbitrary",),
                                             vmem_limit_bytes=VMEM_LIMIT),
        name="dispatch",
    )(dest_flat, cnt, start, n_used, x1, meta_rows, mod3, g2, zsrc)


def _expert_kernel(be_ref, nu_ref, xs_ref, wgu_ref, bgu_ref, wd_ref, bd_ref, zero_ref, ytm_ref,
                   ybuf, idv, ids, sc_sem, id_sem, z_sem, *, n_tok):
    j = pl.program_id(0)
    nbk = pl.num_programs(0) - 1
    nu = nu_ref[0]
    slot = j % 2
    prev = 1 - slot

    @pl.when(j == 0)
    def _():
        cp = pltpu.make_async_copy(zero_ref, ytm_ref.at[pl.ds(TOP_K * n_tok, MOE_BM)], z_sem)
        cp.start()
        cp.wait()

    n_pc = 4
    pw1 = D_FF // n_pc
    pw2 = D_MODEL // n_pc
    groups = [48] * n_pc + [16] * n_pc
    assert sum(groups) == MOE_BM

    def scatter_group(g):
        lo = sum(groups[:g])
        for r in range(lo, lo + groups[g]):
            pltpu.make_async_copy(ybuf.at[prev, pl.ds(r, 1)], ytm_ref.at[pl.ds(ids[prev, 0, r], 1)],
                                  sc_sem.at[prev]).start(priority=r % 2)

    def scatter_wait(sl):
        pltpu.make_async_copy(ybuf.at[sl], ytm_ref.at[pl.ds(0, MOE_BM)], sc_sem.at[sl]).wait()

    def compute(with_scatter):
        e_f = be_ref[jnp.minimum(j, nbk - 1)].astype(F32)
        xb = xs_ref[:, 0:D_MODEL].astype(BF16)
        meta = xs_ref[:, D_MODEL:ROW_W]
        gate = jnp.zeros((MOE_BM, 1), F32)
        kk = jnp.zeros((MOE_BM, 1), F32)
        for k in range(TOP_K):
            mk = meta[:, META_IDX + k:META_IDX + k + 1] == e_f
            gate = gate + jnp.where(mk, meta[:, META_GATE + k:META_GATE + k + 1], 0.0)
            kk = kk + jnp.where(mk, float(k), 0.0)
        row = kk * float(n_tok) + meta[:, META_TOK:META_TOK + 1]
        row_t = jnp.broadcast_to(row, (MOE_BM, LANES)).T
        idv[slot] = row_t[0:8].astype(jnp.int32)
        pltpu.make_async_copy(idv.at[slot], ids.at[slot], id_sem.at[slot]).start()
        for c in range(n_pc):
            if with_scatter:
                scatter_group(c)
            cg = slice(c * pw1, (c + 1) * pw1)
            cl = slice(D_FF + c * pw1, D_FF + (c + 1) * pw1)
            glu = jnp.minimum(_dot(xb, wgu_ref[0, :, cg]) + bgu_ref[0, :, cg], SWIGLU_LIMIT)
            lin = jnp.clip(_dot(xb, wgu_ref[0, :, cl]) + bgu_ref[0, :, cl], -SWIGLU_LIMIT, SWIGLU_LIMIT)
            ybuf[2, :, cg] = glu * jax.nn.sigmoid(SWIGLU_ALPHA * glu) * (lin + 1.0)
        act = ybuf[2].astype(BF16)
        for c in range(n_pc):
            if with_scatter:
                scatter_group(n_pc + c)
            cs = slice(c * pw2, (c + 1) * pw2)
            ybuf[slot, :, cs] = (_dot(act, wd_ref[0, :, cs]) + bd_ref[0, :, cs]) * gate

    def ids_wait():
        pltpu.make_async_copy(idv.at[prev], ids.at[prev], id_sem.at[prev]).wait()

    @pl.when(jnp.logical_and(j >= 2, j < nu))
    def _():
        scatter_wait(slot)

    @pl.when(j == 0)
    def _():
        compute(False)

    @pl.when(jnp.logical_and(j >= 1, j < nu))
    def _():
        ids_wait()
        compute(True)

    @pl.when(j == nu)
    def _():
        ids_wait()
        for g in range(len(groups)):
            scatter_group(g)
        scatter_wait(prev)

    @pl.when(jnp.logical_and(j == nu, j >= 2))
    def _():
        scatter_wait(slot)


def _experts(blk_exp, n_used, xs, wgu, bgu, wd, bd, n_tok):
    n_rows = xs.shape[0]
    nb = n_rows // MOE_BM
    row_blk = lambda j, be, nu: (jnp.minimum(j, nu[0] - 1), 0)
    per_e = lambda j, be, nu: (be[jnp.minimum(j, nb - 1)], 0, 0)
    zero = jnp.zeros((MOE_BM, D_MODEL), F32)
    return pl.pallas_call(
        functools.partial(_expert_kernel, n_tok=n_tok),
        out_shape=jax.ShapeDtypeStruct((TOP_K * n_tok + MOE_BM, D_MODEL), F32),
        grid_spec=pltpu.PrefetchScalarGridSpec(
            num_scalar_prefetch=2,
            grid=(nb + 1,),
            in_specs=[pl.BlockSpec((MOE_BM, ROW_W), row_blk),
                      pl.BlockSpec((1, D_MODEL, 2 * D_FF), per_e),
                      pl.BlockSpec((1, 1, 2 * D_FF), per_e),
                      pl.BlockSpec((1, D_FF, D_MODEL), per_e),
                      pl.BlockSpec((1, 1, D_MODEL), per_e),
                      pl.BlockSpec((MOE_BM, D_MODEL), lambda j, be, nu: (0, 0))],
            out_specs=pl.BlockSpec(memory_space=pl.ANY),
            scratch_shapes=[pltpu.VMEM((3, MOE_BM, D_MODEL), F32), pltpu.VMEM((2, 8, MOE_BM), jnp.int32),
                            pltpu.SMEM((2, 8, MOE_BM), jnp.int32), pltpu.SemaphoreType.DMA((2,)),
                            pltpu.SemaphoreType.DMA((2,)), pltpu.SemaphoreType.DMA]),
        compiler_params=pltpu.CompilerParams(dimension_semantics=("arbitrary",),
                                             vmem_limit_bytes=VMEM_LIMIT),
        name="experts",
    )(blk_exp, n_used, xs, wgu, bgu, wd, bd, zero)


def _combine_kernel(y0_ref, y1_ref, y2_ref, y3_ref, x1_ref, mod_ref, fg_ref, o_ref):
    m = mod_ref[0]
    moe = (y0_ref[...] + y1_ref[...]) + (y2_ref[...] + y3_ref[...])
    x2 = x1_ref[...] + m[5:6] * moe
    o_ref[...] = _rms(x2) * fg_ref[...]


def _combine(ytm, x1, mod3, fg):
    n_tok = x1.shape[0]
    tc = COMB_TILE
    per_b = SEQ // tc
    nt = n_tok // tc
    slot_spec = lambda k: pl.BlockSpec((tc, D_MODEL), lambda i: (i + k * nt, 0))
    return pl.pallas_call(
        _combine_kernel,
        out_shape=jax.ShapeDtypeStruct((n_tok, D_MODEL), F32),
        grid=(nt,),
        in_specs=[slot_spec(0), slot_spec(1), slot_spec(2), slot_spec(3),
                  pl.BlockSpec((tc, D_MODEL), lambda i: (i, 0)),
                  pl.BlockSpec((1, N_MOD, D_MODEL), lambda i: (i // per_b, 0, 0)),
                  pl.BlockSpec((1, D_MODEL), lambda i: (0, 0))],
        out_specs=pl.BlockSpec((tc, D_MODEL), lambda i: (i, 0)),
        compiler_params=pltpu.CompilerParams(dimension_semantics=("arbitrary",),
                                             vmem_limit_bytes=VMEM_LIMIT),
        name="combine",
    )(ytm, ytm, ytm, ytm, x1, mod3, fg)


def _expansion_matrices():
    r = jnp.arange(LANES)[:, None]
    out64, out128 = [], []
    for d in range(2):
        l64 = jnp.arange(D_SSD)[None, :]
        l128 = jnp.arange(HEADS * LANES)[None, :]
        out64.append((l64 // HEAD_DIM == r - HEADS * d).astype(BF16))
        out128.append((l128 // LANES == r - HEADS * d).astype(BF16))
    return jnp.stack(out64), jnp.stack(out128)


def _pad_lanes(v):
    return jnp.pad(v, [(0, 0)] * (v.ndim - 1) + [(0, LANES - v.shape[-1])])


def kernel(x, c, ctx, c_ctx, w_mod, b_mod, norm1_g, w_in, ssd_conv_w, ssd_conv_b, ssd_dt_bias, ssd_a_log,
           ssd_d, ssd_norm_g, sc_conv_w, w_out, norm2_g, w_router, b_router, w_gate_up, b_gate_up, w_down,
           b_down, final_g):
    bsz = x.shape[0]
    n_tok = bsz * SEQ
    n_assign = n_tok * TOP_K
    n_blocks = n_assign // MOE_BM + N_EXPERTS
    n_rows = n_blocks * MOE_BM
    li = 0

    cvec = jnp.concatenate([c, c_ctx[None, :], jnp.zeros((7, D_MODEL), F32)], axis=0)
    mod3 = _mod(cvec, w_mod[li], b_mod[li][None, :]).reshape(bsz + 8, N_MOD, D_MODEL)

    w = w_in[li]
    wz = w[:, Z0:X0].astype(BF16)
    wxbc = w[:, X0:DT0].astype(BF16)
    wdt = _pad_lanes(w[:, DT0:SC0]).astype(BF16)
    wb = w[:, SC0:SC0 + D_SC].astype(BF16)
    wc = w[:, SC0 + D_SC:SC0 + 2 * D_SC].astype(BF16)
    wu = w[:, SC0 + 2 * D_SC:].astype(BF16)
    g1 = norm1_g[li][None, :]
    cw = ssd_conv_w[li]
    cb = ssd_conv_b[li][None, :]
    dtb = _pad_lanes(ssd_dt_bias[li].reshape(1, 2 * HEADS))
    alog = _pad_lanes(ssd_a_log[li].reshape(1, 2 * HEADS))
    e64, e128 = _expansion_matrices()

    h0 = _ctx_states(ctx, mod3, g1, wxbc[:, :XB_W], wdt, cw[:, :XB_W], cb[:, :XB_W], dtb, alog, e64)

    x2 = x.reshape(n_tok, D_MODEL)
    z, xbc, dtr, scb, v = _inproj(x2, mod3, g1, wz, wxbc, wdt, wb, wc, wu)

    dsk = jnp.repeat(ssd_d[li], HEAD_DIM)[None, :]
    yssd = _ssd(xbc.reshape(bsz, SEQ, XBC_W), z.reshape(bsz, SEQ, D_SSD), dtr.reshape(bsz, SEQ, LANES), h0,
                cw, cb, dtb, alog, dsk, ssd_norm_g[li][None, :], e64, e128)

    wo = w_out[li].astype(BF16)
    g2 = norm2_g[li][None, :]
    x1, lgt = _outproj(x2, yssd.reshape(n_tok, D_SSD), scb, v, mod3, sc_conv_w[li], wo[:D_SSD], wo[D_SSD:],
                       g2, w_router[li].T, b_router[li][:, None])

    dest_t, gate_t, idx_t, meta = _route(lgt, n_blocks)
    dest_flat = dest_t.T.reshape(n_assign)
    cnt = meta[0, :N_EXPERTS]
    start = meta[1, :N_EXPERTS]
    blk_exp = meta[2, :n_blocks]
    n_used = meta[3, :1]

    meta_rows = _pad_lanes(jnp.concatenate(
        [idx_t.T.astype(F32), gate_t.T, jnp.arange(n_tok, dtype=F32)[:, None]], axis=1))
    pad_meta = _pad_lanes(jnp.concatenate(
        [jnp.full((MOE_BM, TOP_K), -1.0, F32), jnp.zeros((MOE_BM, TOP_K), F32),
         (TOP_K * n_tok + jnp.arange(MOE_BM, dtype=F32))[:, None]], axis=1))
    zsrc = jnp.concatenate([jnp.zeros((MOE_BM, D_MODEL), F32), pad_meta], axis=1)

    xs = _dispatch(dest_flat, cnt, start, n_used, x1, meta_rows, mod3, g2, zsrc, n_rows)
    ytm = _experts(blk_exp, n_used, xs, w_gate_up[li].astype(BF16), b_gate_up[li][:, None, :],
                   w_down[li].astype(BF16), b_down[li][:, None, :], n_tok)
    out = _combine(ytm, x1, mod3, final_g[None, :])
    return out.reshape(bsz, SEQ, D_MODEL)
```

```python
import functools

import jax
import jax.numpy as jnp
from jax import lax
from jax.experimental import pallas as pl
from jax.experimental.pallas import tpu as pltpu

F32 = jnp.float32
BF16 = jnp.bfloat16
HIGHEST = lax.Precision.HIGHEST

D_MODEL = 1024
SEQ = 2048
CTX_LEN = 256
GRID_W = 64
D_SSD = 1024
D_SC = 1024
HEAD_DIM = 64
HEADS = 16
GROUPS = 2
STATE = 128
CHUNK = 128
N_EXPERTS = 32
TOP_K = 4
D_FF = 1024
SWIGLU_LIMIT = 7.0
SWIGLU_ALPHA = 1.702
NORM_EPS = 1e-6
N_MOD = 6
XBC_W = D_SSD + 2 * GROUPS * STATE
XB_W = D_SSD + GROUPS * STATE
LANES = 128

Z0 = 0
X0 = Z0 + D_SSD
B0 = X0 + D_SSD
C0 = B0 + GROUPS * STATE
DT0 = C0 + GROUPS * STATE
SC0 = DT0 + 2 * HEADS

TOK_TILE = 512
MOE_BM = 256
RT_TILE = 512
DISP_TILE = 256
COMB_TILE = 512
ROW_W = D_MODEL + LANES
META_IDX = 0
META_GATE = TOP_K
META_TOK = 2 * TOP_K
VMEM_LIMIT = 56 * 1024 * 1024


def _silu(v):
    return v * jax.nn.sigmoid(v)


def _softplus(v):
    return jnp.maximum(v, 0.0) + jnp.log1p(jnp.exp(-jnp.abs(v)))


def _rms(v):
    return v * lax.rsqrt(jnp.mean(v * v, axis=-1, keepdims=True) + NORM_EPS)


def _dot(a, b):
    return jnp.dot(a, b, preferred_element_type=F32)


def _expand(v, e, pieces):
    acc = None
    rem = v
    for _ in range(pieces):
        p = rem.astype(BF16)
        rem = rem - p.astype(F32)
        t = _dot(p, e)
        acc = t if acc is None else acc + t
    return acc


def _mod_kernel(c_ref, w_ref, b_ref, o_ref):
    o_ref[...] = jnp.dot(_silu(c_ref[...]), w_ref[...], precision=HIGHEST,
                         preferred_element_type=F32) + b_ref[...]


def _mod(cvec, w_mod, b_mod):
    rows = cvec.shape[0]
    n = w_mod.shape[1]
    tn = 1536
    return pl.pallas_call(
        _mod_kernel,
        out_shape=jax.ShapeDtypeStruct((rows, n), F32),
        grid=(n // tn,),
        in_specs=[pl.BlockSpec((rows, D_MODEL), lambda j: (0, 0)),
                  pl.BlockSpec((D_MODEL, tn), lambda j: (0, j)),
                  pl.BlockSpec((1, tn), lambda j: (0, j))],
        out_specs=pl.BlockSpec((rows, tn), lambda j: (0, j)),
        compiler_params=pltpu.CompilerParams(dimension_semantics=("arbitrary",),
                                             vmem_limit_bytes=VMEM_LIMIT),
        name="mod",
    )(cvec, w_mod, b_mod)


def _ctx_kernel(ctx_ref, mod_ref, g1_ref, wxb_ref, wdt_ref, cw_ref, cb_ref, dtb_ref, alog_ref, e64_ref,
                h0_ref):
    L = CTX_LEN
    m = mod_ref[0]
    hc = _rms(ctx_ref[0]) * g1_ref[...] * (1.0 + m[1:2]) + m[0:1]
    hb = hc.astype(BF16)
    pxb = _dot(hb, wxb_ref[...])
    dtr = _dot(hb, wdt_ref[...])
    rowi = lax.broadcasted_iota(jnp.int32, (L, XB_W), 0)
    dn = jnp.where(rowi == 0, 0.0, pltpu.roll(pxb, 1, 0))
    up = jnp.where(rowi == L - 1, 0.0, pltpu.roll(pxb, L - 1, 0))
    cw = cw_ref[...]
    xb = _silu(cw[0:1] * dn + cw[1:2] * pxb + cw[2:3] * up + cb_ref[...])
    xs = xb[:, :D_SSD]
    bm = xb[:, D_SSD:].astype(BF16)
    dt = _softplus(dtr + dtb_ref[...])
    da = dt * (-jnp.exp(alog_ref[...]))
    ri = lax.broadcasted_iota(jnp.int32, (L, L), 0)
    ci = lax.broadcasted_iota(jnp.int32, (L, L), 1)
    for d in range(2):
        tri = (ci <= ri) if d == 0 else (ci >= ri)
        cum = jnp.dot(tri.astype(F32), da, precision=HIGHEST, preferred_element_type=F32)
        last = cum[L - 1:L] if d == 0 else cum[0:1]
        w_e = _expand(jnp.exp(last - cum) * dt, e64_ref[d], 2)
        xw = (xs * w_e).astype(BF16)
        for g in range(GROUPS):
            gw = D_SSD // GROUPS
            st = lax.dot_general(bm[:, g * STATE:(g + 1) * STATE], xw[:, g * gw:(g + 1) * gw],
                                 (((0,), (0,)), ((), ())), preferred_element_type=F32)
            h0_ref[0, d, :, g * gw:(g + 1) * gw] = st


def _ctx_states(ctx, mod3, g1, wxb, wdt, cw, cb, dtb, alog, e64):
    bsz = ctx.shape[0]
    mod_row = bsz
    const = lambda *shape: pl.BlockSpec(shape, lambda b: (0,) * len(shape))
    return pl.pallas_call(
        _ctx_kernel,
        out_shape=jax.ShapeDtypeStruct((bsz, 2, STATE, D_SSD), F32),
        grid=(bsz,),
        in_specs=[pl.BlockSpec((1, CTX_LEN, D_MODEL), lambda b: (b, 0, 0)),
                  pl.BlockSpec((1, N_MOD, D_MODEL), lambda b: (mod_row, 0, 0)),
                  const(1, D_MODEL), const(D_MODEL, XB_W), const(D_MODEL, LANES),
                  const(3, XB_W), const(1, XB_W), const(1, LANES), const(1, LANES),
                  const(2, LANES, D_SSD)],
        out_specs=pl.BlockSpec((1, 2, STATE, D_SSD), lambda b: (b, 0, 0, 0)),
        compiler_params=pltpu.CompilerParams(dimension_semantics=("arbitrary",),
                                             vmem_limit_bytes=VMEM_LIMIT),
        name="ctx_states",
    )(ctx, mod3, g1, wxb, wdt, cw, cb, dtb, alog, e64)


def _inproj_kernel(x_ref, mod_ref, g1_ref, wz_ref, wxbc_ref, wdt_ref, wb_ref, wc_ref, wu_ref,
                   z_ref, xbc_ref, dt_ref, scb_ref, v_ref):
    m = mod_ref[0]
    hx = _rms(x_ref[...]) * g1_ref[...] * (1.0 + m[1:2]) + m[0:1]
    hb = hx.astype(BF16)
    z_ref[...] = _dot(hb, wz_ref[...]).astype(BF16)
    xbc_ref[...] = _dot(hb, wxbc_ref[...]).astype(BF16)
    dt_ref[...] = _dot(hb, wdt_ref[...])
    scb_ref[...] = _dot(hb, wb_ref[...]).astype(BF16)
    v_ref[...] = (_dot(hb, wc_ref[...]) * _dot(hb, wu_ref[...])).astype(BF16)


def _inproj(x2, mod3, g1, wz, wxbc, wdt, wb, wc, wu):
    t = x2.shape[0]
    tm = TOK_TILE
    per_b = SEQ // tm
    const = lambda *shape: pl.BlockSpec(shape, lambda i: (0,) * len(shape))
    tile = lambda w: pl.BlockSpec((tm, w), lambda i: (i, 0))
    return pl.pallas_call(
        _inproj_kernel,
        out_shape=(jax.ShapeDtypeStruct((t, D_SSD), BF16), jax.ShapeDtypeStruct((t, XBC_W), BF16),
                   jax.ShapeDtypeStruct((t, LANES), F32), jax.ShapeDtypeStruct((t, D_SC), BF16),
                   jax.ShapeDtypeStruct((t, D_SC), BF16)),
        grid=(t // tm,),
        in_specs=[tile(D_MODEL),
                  pl.BlockSpec((1, N_MOD, D_MODEL), lambda i: (i // per_b, 0, 0)),
                  const(1, D_MODEL), const(D_MODEL, D_SSD), const(D_MODEL, XBC_W), const(D_MODEL, LANES),
                  const(D_MODEL, D_SC), const(D_MODEL, D_SC), const(D_MODEL, D_SC)],
        out_specs=(tile(D_SSD), tile(XBC_W), tile(LANES), tile(D_SC), tile(D_SC)),
        compiler_params=pltpu.CompilerParams(dimension_semantics=("arbitrary",),
                                             vmem_limit_bytes=VMEM_LIMIT),
        name="inproj",
    )(x2, mod3, g1, wz, wxbc, wdt, wb, wc, wu)


def _ssd_kernel(xbc_ref, z_ref, dt_ref, h0_ref, cw_ref, cb_ref, dtb_ref, alog_ref, dsk_ref, g_ref,
                e64_ref, e128_ref, o_ref, xc_ref, y_ref, s_ref):
    Q = CHUNK
    nck = SEQ // Q
    gw = D_SSD // GROUPS

    rowi = lax.broadcasted_iota(jnp.int32, (Q, XBC_W), 0)

    def conv_body(c, carry):
        r0 = pl.multiple_of(c * Q, Q)
        main = xbc_ref[0, pl.ds(r0, Q), :].astype(F32)
        pstart = pl.multiple_of(jnp.maximum(r0 - 16, 0), 16)
        nstart = pl.multiple_of(jnp.minimum(r0 + Q, SEQ - 16), 16)
        prev = xbc_ref[0, pl.ds(pstart, 16), :].astype(F32)[15:16]
        nxt = xbc_ref[0, pl.ds(nstart, 16), :].astype(F32)[0:1]
        prev = jnp.where(c > 0, prev, 0.0)
        nxt = jnp.where(c < nck - 1, nxt, 0.0)
        dn = jnp.where(rowi == 0, prev, pltpu.roll(main, 1, 0))
        up = jnp.where(rowi == Q - 1, nxt, pltpu.roll(main, Q - 1, 0))
        cw = cw_ref[...]
        conv = cw[0:1] * dn + cw[1:2] * main + cw[2:3] * up + cb_ref[...]
        xc_ref[pl.ds(r0, Q), :] = _silu(conv).astype(BF16)
        return carry

    lax.fori_loop(0, nck, conv_body, 0)

    ri = lax.broadcasted_iota(jnp.int32, (Q, Q), 0)
    ci = lax.broadcasted_iota(jnp.int32, (Q, Q), 1)
    lane = lax.broadcasted_iota(jnp.int32, (Q, LANES), 1)
    a_neg = -jnp.exp(alog_ref[...])

    def chunk(c, d):
        r0 = pl.multiple_of(c * Q, Q)
        rows = pl.ds(r0, Q)
        xs = xc_ref[rows, 0:D_SSD].astype(F32)
        bm = xc_ref[rows, D_SSD:D_SSD + GROUPS * STATE]
        cm = xc_ref[rows, D_SSD + GROUPS * STATE:XBC_W]
        dt = _softplus(dt_ref[0, rows, :] + dtb_ref[...])
        da = dt * a_neg
        tri = (ci <= ri) if d == 0 else (ci >= ri)
        cum = jnp.dot(tri.astype(F32), da, precision=HIGHEST, preferred_element_type=F32)
        cum_t = cum.T
        last = cum[Q - 1:Q] if d == 0 else cum[0:1]
        ecum_e = _expand(jnp.exp(cum), e64_ref[d], 2)
        dt_e = _expand(dt, e64_ref[d], 2)
        w_e = _expand(jnp.exp(last - cum) * dt, e64_ref[d], 2)
        colb = _expand(cum, e128_ref[d], 3)
        decay_e = ecum_e[Q - 1:Q] if d == 0 else ecum_e[0:1]

        gmat = [lax.dot_general(cm[:, g * STATE:(g + 1) * STATE], bm[:, g * STATE:(g + 1) * STATE],
                                (((1,), (1,)), ((), ())), preferred_element_type=F32)
                for g in range(GROUPS)]
        xdt = xs * dt_e
        y_parts = []
        for p in range(HEADS // 2):
            g = (2 * p) // (HEADS // GROUPS)
            ms = []
            for hh in (2 * p, 2 * p + 1):
                seg = colb[:, hh * LANES:(hh + 1) * LANES] - cum_t[HEADS * d + hh:HEADS * d + hh + 1, :]
                mm = jnp.where(tri, jnp.exp(jnp.where(tri, seg, 0.0)), 0.0) * gmat[g]
                ms.append(mm.astype(BF16))
            mcat = jnp.concatenate(ms, axis=1)
            xp = xdt[:, p * LANES:(p + 1) * LANES]
            rhs = jnp.concatenate([jnp.where(lane < HEAD_DIM, xp, 0.0).astype(BF16),
                                   jnp.where(lane >= HEAD_DIM, xp, 0.0).astype(BF16)], axis=0)
            y_parts.append(_dot(mcat, rhs))
        y_diag = jnp.concatenate(y_parts, axis=1)

        s_old = s_ref[...]
        s_bf = s_old.astype(BF16)
        y_off = jnp.concatenate(
            [_dot(cm[:, g * STATE:(g + 1) * STATE], s_bf[:, g * gw:(g + 1) * gw]) for g in range(GROUPS)],
            axis=1)
        y = y_diag + y_off * ecum_e

        xw = (xs * w_e).astype(BF16)
        upd = jnp.concatenate(
            [lax.dot_general(bm[:, g * STATE:(g + 1) * STATE], xw[:, g * gw:(g + 1) * gw],
                             (((0,), (0,)), ((), ())), preferred_element_type=F32) for g in range(GROUPS)],
            axis=1)
        s_ref[...] = s_old * decay_e + upd

        if d == 0:
            y_ref[rows, :] = y + dsk_ref[...] * xs
        else:
            tot = y_ref[rows, :] + y
            zz = z_ref[0, rows, :].astype(F32)
            gz = tot * _silu(zz)
            outs = []
            for g in range(GROUPS):
                gg = gz[:, g * gw:(g + 1) * gw]
                outs.append(gg * lax.rsqrt(jnp.mean(gg * gg, axis=-1, keepdims=True) + NORM_EPS))
            o_ref[0, rows, :] = (jnp.concatenate(outs, axis=1) * g_ref[...]).astype(BF16)

    s_ref[...] = h0_ref[0, 0]

    def fwd_body(i, carry):
        chunk(i, 0)
        return carry

    lax.fori_loop(0, nck, fwd_body, 0)

    s_ref[...] = h0_ref[0, 1]

    def bwd_body(i, carry):
        chunk(nck - 1 - i, 1)
        return carry

    lax.fori_loop(0, nck, bwd_body, 0)


def _ssd(xbc3, z3, dt3, h0, cw, cb, dtb, alog, dsk, g, e64, e128):
    bsz = xbc3.shape[0]
    const = lambda *shape: pl.BlockSpec(shape, lambda b: (0,) * len(shape))
    seq = lambda w: pl.BlockSpec((1, SEQ, w), lambda b: (b, 0, 0))
    return pl.pallas_call(
        _ssd_kernel,
        out_shape=jax.ShapeDtypeStruct((bsz, SEQ, D_SSD), BF16),
        grid=(bsz,),
        in_specs=[seq(XBC_W), seq(D_SSD), seq(LANES),
                  pl.BlockSpec((1, 2, STATE, D_SSD), lambda b: (b, 0, 0, 0)),
                  const(3, XBC_W), const(1, XBC_W), const(1, LANES), const(1, LANES),
                  const(1, D_SSD), const(1, D_SSD), const(2, LANES, D_SSD), const(2, LANES, HEADS * LANES)],
        out_specs=seq(D_SSD),
        scratch_shapes=[pltpu.VMEM((SEQ, XBC_W), BF16), pltpu.VMEM((SEQ, D_SSD), F32),
                        pltpu.VMEM((STATE, D_SSD), F32)],
        compiler_params=pltpu.CompilerParams(dimension_semantics=("arbitrary",),
                                             vmem_limit_bytes=VMEM_LIMIT),
        name="ssd",
    )(xbc3, z3, dt3, h0, cw, cb, dtb, alog, dsk, g, e64, e128)


def _outproj_kernel(x_ref, yssd_ref, scb_ref, v_ref, vp_ref, vn_ref, mod_ref, scw_ref, wo1_ref, wo2_ref,
                    g2_ref, wrt_ref, br_ref, x1_ref, lg_ref):
    tm = TOK_TILE
    per_b = SEQ // tm
    i = pl.program_id(0)
    first = (i % per_b) == 0
    last = (i % per_b) == per_b - 1
    m = mod_ref[0]
    v = v_ref[...].astype(F32)
    vp = jnp.where(first, 0.0, vp_ref[...].astype(F32))
    vn = jnp.where(last, 0.0, vn_ref[...].astype(F32))
    dn = jnp.concatenate([vp, v[:tm - GRID_W]], axis=0)
    up = jnp.concatenate([v[GRID_W:], vn], axis=0)
    scw = scw_ref[...]
    ysc = scb_ref[...].astype(F32) * (scw[0:1] * dn + scw[1:2] * v + scw[2:3] * up)
    out = _dot(yssd_ref[...], wo1_ref[...]) + _dot(ysc.astype(BF16), wo2_ref[...])
    x1 = x_ref[...] + m[2:3] * out
    x1_ref[...] = x1
    h2 = _rms(x1) * g2_ref[...] * (1.0 + m[4:5]) + m[3:4]
    lg_ref[...] = lax.dot_general(wrt_ref[...], h2, (((1,), (1,)), ((), ())), precision=HIGHEST,
                                  preferred_element_type=F32) + br_ref[...]


def _outproj(x2, yssd, scb, v, mod3, scw, wo1, wo2, g2, wrt, br):
    t = x2.shape[0]
    tm = TOK_TILE
    per_b = SEQ // tm
    r = tm // GRID_W
    nrow = t // GRID_W
    const = lambda *shape: pl.BlockSpec(shape, lambda i: (0,) * len(shape))
    tile = lambda w: pl.BlockSpec((tm, w), lambda i: (i, 0))
    return pl.pallas_call(
        _outproj_kernel,
        out_shape=(jax.ShapeDtypeStruct((t, D_MODEL), F32), jax.ShapeDtypeStruct((N_EXPERTS, t), F32)),
        grid=(t // tm,),
        in_specs=[tile(D_MODEL), tile(D_SSD), tile(D_SC), tile(D_SC),
                  pl.BlockSpec((GRID_W, D_SC), lambda i: (jnp.maximum(i * r - 1, 0), 0)),
                  pl.BlockSpec((GRID_W, D_SC), lambda i: (jnp.minimum((i + 1) * r, nrow - 1), 0)),
                  pl.BlockSpec((1, N_MOD, D_MODEL), lambda i: (i // per_b, 0, 0)),
                  const(3, D_SC), const(D_SSD, D_MODEL), const(D_SC, D_MODEL), const(1, D_MODEL),
                  const(N_EXPERTS, D_MODEL), const(N_EXPERTS, 1)],
        out_specs=(tile(D_MODEL), pl.BlockSpec((N_EXPERTS, tm), lambda i: (0, i))),
        compiler_params=pltpu.CompilerParams(dimension_semantics=("arbitrary",),
                                             vmem_limit_bytes=VMEM_LIMIT),
        name="outproj",
    )(x2, yssd, scb, v, v, v, mod3, scw, wo1, wo2, g2, wrt, br)


def _route_kernel(lg_ref, dest_ref, gate_ref, idx_ref, meta_ref, rank_ref, carry_ref, *, n_tok, n_blocks):
    tt = RT_TILE
    ne = N_EXPERTS
    eio = lax.broadcasted_iota(jnp.int32, (ne, tt), 0)
    si = lax.broadcasted_iota(jnp.int32, (tt, tt), 0)
    ti = lax.broadcasted_iota(jnp.int32, (tt, tt), 1)
    before = (si < ti).astype(BF16)
    carry_ref[...] = jnp.zeros_like(carry_ref)

    def tile_body(j, c):
        t0 = pl.multiple_of(j * tt, tt)
        l = lg_ref[:, pl.ds(t0, tt)]
        onehot = jnp.zeros((ne, tt), F32)
        tops, sels = [], []
        for _ in range(TOP_K):
            mx = jnp.max(l, axis=0, keepdims=True)
            idx = jnp.min(jnp.where(l == mx, eio, ne), axis=0, keepdims=True)
            sel = eio == idx
            l = jnp.where(sel, -jnp.inf, l)
            onehot = onehot + sel.astype(F32)
            tops.append(mx)
            sels.append(sel)
            idx_ref[pl.ds(len(tops) - 1, 1), pl.ds(t0, tt)] = idx
        ex = [jnp.exp(tv - tops[0]) for tv in tops]
        den = ex[0] + ex[1] + ex[2] + ex[3]
        prefix = _dot(onehot.astype(BF16), before) + carry_ref[:, 0:1]
        for k in range(TOP_K):
            gate_ref[pl.ds(k, 1), pl.ds(t0, tt)] = ex[k] / den
            rk = jnp.sum(jnp.where(sels[k], prefix, 0.0), axis=0, keepdims=True)
            rank_ref[pl.ds(k, 1), pl.ds(t0, tt)] = rk.astype(jnp.int32)
        carry_ref[...] = carry_ref[...] + jnp.sum(onehot, axis=1, keepdims=True)
        return c

    lax.fori_loop(0, n_tok // tt, tile_body, 0)

    counts = carry_ref[...]
    padded = jnp.floor((counts + (MOE_BM - 1)) * (1.0 / MOE_BM)) * MOE_BM
    er = lax.broadcasted_iota(jnp.int32, (ne, ne), 0)
    ec = lax.broadcasted_iota(jnp.int32, (ne, ne), 1)
    pad_start = jnp.dot((ec < er).astype(F32), padded, precision=HIGHEST, preferred_element_type=F32)
    pad_end = pad_start + padded

    def dest_body(j, c):
        t0 = pl.multiple_of(j * tt, tt)
        for k in range(TOP_K):
            idx = idx_ref[pl.ds(k, 1), pl.ds(t0, tt)]
            base = jnp.sum(jnp.where(eio == idx, pad_start[:, 0:1], 0.0), axis=0, keepdims=True)
            dest_ref[pl.ds(k, 1), pl.ds(t0, tt)] = base.astype(jnp.int32) + rank_ref[pl.ds(k, 1), pl.ds(t0, tt)]
        return c

    lax.fori_loop(0, n_tok // tt, dest_body, 0)

    width = meta_ref.shape[1]
    sub = lax.broadcasted_iota(jnp.int32, (ne, width), 0)
    lan = lax.broadcasted_iota(jnp.int32, (ne, width), 1)
    diag = sub == lan
    cnt_row = jnp.sum(jnp.where(diag, counts[:, 0:1], 0.0), axis=0, keepdims=True)
    start_row = jnp.sum(jnp.where(diag, pad_start[:, 0:1], 0.0), axis=0, keepdims=True)
    blk_start = (lan * MOE_BM).astype(F32)
    blk_exp = jnp.sum((pad_end[:, 0:1] <= blk_start).astype(F32), axis=0, keepdims=True)
    blk_exp = jnp.minimum(blk_exp, float(ne - 1))
    used = jnp.sum(padded[:, 0:1], axis=0, keepdims=True) * (1.0 / MOE_BM)
    meta_ref[0:1, :] = cnt_row.astype(jnp.int32)
    meta_ref[1:2, :] = start_row.astype(jnp.int32)
    meta_ref[2:3, :] = blk_exp.astype(jnp.int32)
    meta_ref[3:4, :] = jnp.broadcast_to(used, (1, width)).astype(jnp.int32)
    meta_ref[4:8, :] = jnp.zeros((4, width), jnp.int32)


def _route(lgt, n_blocks):
    ne, n_tok = lgt.shape
    width = -(-n_blocks // LANES) * LANES
    full = lambda *shape: pl.BlockSpec(shape, lambda: (0,) * len(shape))
    return pl.pallas_call(
        functools.partial(_route_kernel, n_tok=n_tok, n_blocks=n_blocks),
        out_shape=(jax.ShapeDtypeStruct((TOP_K, n_tok), jnp.int32),
                   jax.ShapeDtypeStruct((TOP_K, n_tok), F32),
                   jax.ShapeDtypeStruct((TOP_K, n_tok), jnp.int32),
                   jax.ShapeDtypeStruct((8, width), jnp.int32)),
        in_specs=[full(ne, n_tok)],
        out_specs=(full(TOP_K, n_tok), full(TOP_K, n_tok), full(TOP_K, n_tok), full(8, width)),
        scratch_shapes=[pltpu.VMEM((TOP_K, n_tok), jnp.int32), pltpu.VMEM((ne, LANES), F32)],
        compiler_params=pltpu.CompilerParams(vmem_limit_bytes=VMEM_LIMIT),
        name="route",
    )(lgt)


def _dispatch_kernel(dest_ref, cnt_ref, start_ref, nu_ref, x1_ref, meta_ref, mod_ref, g2_ref, zsrc_ref, xs_ref,
                     hbuf, sem, zsem):
    i = pl.program_id(0)
    n = pl.num_programs(0)
    tl = DISP_TILE
    slot = i % 2
    nb = xs_ref.shape[0] // MOE_BM

    def zero_block(b):
        return pltpu.make_async_copy(zsrc_ref, xs_ref.at[pl.ds(b * MOE_BM, MOE_BM)], zsem)

    @pl.when(i == 0)
    def _():
        def start_e(e, c):
            @pl.when(cnt_ref[e] > 0)
            def _():
                zero_block((start_ref[e] + cnt_ref[e] - 1) // MOE_BM).start()
            return c

        def wait_e(e, c):
            @pl.when(cnt_ref[e] > 0)
            def _():
                zero_block(0).wait()
            return c

        def start_t(b, c):
            zero_block(b).start()
            return c

        def wait_t(b, c):
            zero_block(0).wait()
            return c

        lax.fori_loop(0, N_EXPERTS, start_e, 0)
        lax.fori_loop(nu_ref[0], nb, start_t, 0)
        lax.fori_loop(0, N_EXPERTS, wait_e, 0)
        lax.fori_loop(nu_ref[0], nb, wait_t, 0)

    m = mod_ref[0]
    hbuf[slot, :, 0:D_MODEL] = _rms(x1_ref[...]) * g2_ref[...] * (1.0 + m[4:5]) + m[3:4]
    hbuf[slot, :, D_MODEL:ROW_W] = meta_ref[...]

    def per_tok(t, c):
        tok = i * tl + t
        for k in range(TOP_K):
            d = dest_ref[tok * TOP_K + k]
            pltpu.make_async_copy(hbuf.at[slot, pl.ds(t, 1)], xs_ref.at[pl.ds(d, 1)], sem.at[slot]).start()
        return c

    lax.fori_loop(0, tl, per_tok, 0)

    def wait_slot(sl):
        for _ in range(TOP_K):
            pltpu.make_async_copy(hbuf.at[sl], xs_ref.at[pl.ds(0, tl)], sem.at[sl]).wait()

    @pl.when(i > 0)
    def _():
        wait_slot(1 - slot)

    @pl.when(i == n - 1)
    def _():
        wait_slot(slot)


def _dispatch(dest_flat, cnt, start, n_used, x1, meta_rows, mod3, g2, zsrc, n_rows):
    n_tok = x1.shape[0]
    tl = DISP_TILE
    per_b = SEQ // tl
    return pl.pallas_call(
        _dispatch_kernel,
        out_shape=jax.ShapeDtypeStruct((n_rows, ROW_W), F32),
        grid_spec=pltpu.PrefetchScalarGridSpec(
            num_scalar_prefetch=4,
            grid=(n_tok // tl,),
            in_specs=[pl.BlockSpec((tl, D_MODEL), lambda i, *_: (i, 0)),
                      pl.BlockSpec((tl, LANES), lambda i, *_: (i, 0)),
                      pl.BlockSpec((1, N_MOD, D_MODEL), lambda i, *_: (i // per_b, 0, 0)),
                      pl.BlockSpec((1, D_MODEL), lambda i, *_: (0, 0)),
                      pl.BlockSpec((MOE_BM, ROW_W), lambda i, *_: (0, 0))],
            out_specs=pl.BlockSpec(memory_space=pl.ANY),
            scratch_shapes=[pltpu.VMEM((2, tl, ROW_W), F32),
                            pltpu.SemaphoreType.DMA((2,)), pltpu.SemaphoreType.DMA]),
        compiler_params=pltpu.CompilerParams(dimension_semantics=("arbitrary",),
                                             vmem_limit_bytes=VMEM_LIMIT),
        name="dispatch",
    )(dest_flat, cnt, start, n_used, x1, meta_rows, mod3, g2, zsrc)


def _expert_kernel(be_ref, nu_ref, xs_ref, wgu_ref, bgu_ref, wd_ref, bd_ref, zero_ref, ytm_ref,
                   ybuf, stage, idv, ids, sc_sem, id_sem, z_sem, *, n_tok):
    j = pl.program_id(0)
    nbk = pl.num_programs(0) - 1
    nu = nu_ref[0]
    slot = j % 2
    prev = 1 - slot

    @pl.when(j == 0)
    def _():
        cp = pltpu.make_async_copy(zero_ref, ytm_ref.at[pl.ds(TOP_K * n_tok, MOE_BM)], z_sem)
        cp.start()
        cp.wait()

    n_pc = 4
    pw1 = D_FF // n_pc
    pw2 = D_MODEL // n_pc
    groups = [48] * n_pc + [16] * n_pc
    assert sum(groups) == MOE_BM

    def scatter_group(g):
        lo = sum(groups[:g])
        for r in range(lo, lo + groups[g]):
            pltpu.make_async_copy(ybuf.at[prev, r], ytm_ref.at[ids[prev, 0, r]],
                                  sc_sem.at[prev]).start(priority=r % 2)

    def scatter_wait(sl):
        pltpu.make_async_copy(ybuf.at[sl], ytm_ref.at[pl.ds(0, MOE_BM)], sc_sem.at[sl]).wait()

    def compute(with_scatter):
        e_f = be_ref[jnp.minimum(j, nbk - 1)].astype(F32)
        xb = xs_ref[:, 0:D_MODEL].astype(BF16)
        meta = xs_ref[:, D_MODEL:ROW_W]
        gate = jnp.zeros((MOE_BM, 1), F32)
        kk = jnp.zeros((MOE_BM, 1), F32)
        for k in range(TOP_K):
            mk = meta[:, META_IDX + k:META_IDX + k + 1] == e_f
            gate = gate + jnp.where(mk, meta[:, META_GATE + k:META_GATE + k + 1], 0.0)
            kk = kk + jnp.where(mk, float(k), 0.0)
        row = kk * float(n_tok) + meta[:, META_TOK:META_TOK + 1]
        row_t = jnp.broadcast_to(row, (MOE_BM, LANES)).T
        idv[slot] = row_t[0:8].astype(jnp.int32)
        pltpu.make_async_copy(idv.at[slot], ids.at[slot], id_sem.at[slot]).start()
        for c in range(n_pc):
            if with_scatter:
                scatter_group(c)
            cg = slice(c * pw1, (c + 1) * pw1)
            cl = slice(D_FF + c * pw1, D_FF + (c + 1) * pw1)
            glu = jnp.minimum(_dot(xb, wgu_ref[0, :, cg]) + bgu_ref[0, :, cg], SWIGLU_LIMIT)
            lin = jnp.clip(_dot(xb, wgu_ref[0, :, cl]) + bgu_ref[0, :, cl], -SWIGLU_LIMIT, SWIGLU_LIMIT)
            stage[:, cg] = glu * jax.nn.sigmoid(SWIGLU_ALPHA * glu) * (lin + 1.0)
        act = stage[...].astype(BF16)
        for c in range(n_pc):
            if with_scatter:
                scatter_group(n_pc + c)
            cs = slice(c * pw2, (c + 1) * pw2)
            stage[:, cs] = (_dot(act, wd_ref[0, :, cs]) + bd_ref[0, :, cs]) * gate
        ybuf[slot] = stage[...].reshape(MOE_BM, D_MODEL // LANES, LANES)

    def ids_wait():
        pltpu.make_async_copy(idv.at[prev], ids.at[prev], id_sem.at[prev]).wait()

    @pl.when(jnp.logical_and(j >= 2, j < nu))
    def _():
        scatter_wait(slot)

    @pl.when(j == 0)
    def _():
        compute(False)

    @pl.when(jnp.logical_and(j >= 1, j < nu))
    def _():
        ids_wait()
        compute(True)

    @pl.when(j == nu)
    def _():
        ids_wait()
        for g in range(len(groups)):
            scatter_group(g)
        scatter_wait(prev)

    @pl.when(jnp.logical_and(j == nu, j >= 2))
    def _():
        scatter_wait(slot)


def _experts(blk_exp, n_used, xs, wgu, bgu, wd, bd, n_tok):
    n_rows = xs.shape[0]
    nb = n_rows // MOE_BM
    row_blk = lambda j, be, nu: (jnp.minimum(j, nu[0] - 1), 0)
    per_e = lambda j, be, nu: (be[jnp.minimum(j, nb - 1)], 0, 0)
    sub = D_MODEL // LANES
    zero = jnp.zeros((MOE_BM, sub, LANES), F32)
    return pl.pallas_call(
        functools.partial(_expert_kernel, n_tok=n_tok),
        out_shape=jax.ShapeDtypeStruct((TOP_K * n_tok + MOE_BM, sub, LANES), F32),
        grid_spec=pltpu.PrefetchScalarGridSpec(
            num_scalar_prefetch=2,
            grid=(nb + 1,),
            in_specs=[pl.BlockSpec((MOE_BM, ROW_W), row_blk),
                      pl.BlockSpec((1, D_MODEL, 2 * D_FF), per_e),
                      pl.BlockSpec((1, 1, 2 * D_FF), per_e),
                      pl.BlockSpec((1, D_FF, D_MODEL), per_e),
                      pl.BlockSpec((1, 1, D_MODEL), per_e),
                      pl.BlockSpec((MOE_BM, sub, LANES), lambda j, be, nu: (0, 0, 0))],
            out_specs=pl.BlockSpec(memory_space=pl.ANY),
            scratch_shapes=[pltpu.VMEM((2, MOE_BM, sub, LANES), F32), pltpu.VMEM((MOE_BM, D_MODEL), F32),
                            pltpu.VMEM((2, 8, MOE_BM), jnp.int32),
                            pltpu.SMEM((2, 8, MOE_BM), jnp.int32), pltpu.SemaphoreType.DMA((2,)),
                            pltpu.SemaphoreType.DMA((2,)), pltpu.SemaphoreType.DMA]),
        compiler_params=pltpu.CompilerParams(dimension_semantics=("arbitrary",),
                                             vmem_limit_bytes=VMEM_LIMIT),
        name="experts",
    )(blk_exp, n_used, xs, wgu, bgu, wd, bd, zero)


def _combine_kernel(y0_ref, y1_ref, y2_ref, y3_ref, x1_ref, mod_ref, fg_ref, o_ref):
    m = mod_ref[0]
    moe = ((y0_ref[...] + y1_ref[...]) + (y2_ref[...] + y3_ref[...])).reshape(COMB_TILE, D_MODEL)
    x2 = x1_ref[...] + m[5:6] * moe
    o_ref[...] = _rms(x2) * fg_ref[...]


def _combine(ytm, x1, mod3, fg):
    n_tok = x1.shape[0]
    tc = COMB_TILE
    per_b = SEQ // tc
    nt = n_tok // tc
    slot_spec = lambda k: pl.BlockSpec((tc, D_MODEL // LANES, LANES), lambda i: (i + k * nt, 0, 0))
    return pl.pallas_call(
        _combine_kernel,
        out_shape=jax.ShapeDtypeStruct((n_tok, D_MODEL), F32),
        grid=(nt,),
        in_specs=[slot_spec(0), slot_spec(1), slot_spec(2), slot_spec(3),
                  pl.BlockSpec((tc, D_MODEL), lambda i: (i, 0)),
                  pl.BlockSpec((1, N_MOD, D_MODEL), lambda i: (i // per_b, 0, 0)),
                  pl.BlockSpec((1, D_MODEL), lambda i: (0, 0))],
        out_specs=pl.BlockSpec((tc, D_MODEL), lambda i: (i, 0)),
        compiler_params=pltpu.CompilerParams(dimension_semantics=("arbitrary",),
                                             vmem_limit_bytes=VMEM_LIMIT),
        name="combine",
    )(ytm, ytm, ytm, ytm, x1, mod3, fg)


def _expansion_matrices():
    r = jnp.arange(LANES)[:, None]
    out64, out128 = [], []
    for d in range(2):
        l64 = jnp.arange(D_SSD)[None, :]
        l128 = jnp.arange(HEADS * LANES)[None, :]
        out64.append((l64 // HEAD_DIM == r - HEADS * d).astype(BF16))
        out128.append((l128 // LANES == r - HEADS * d).astype(BF16))
    return jnp.stack(out64), jnp.stack(out128)


def _pad_lanes(v):
    return jnp.pad(v, [(0, 0)] * (v.ndim - 1) + [(0, LANES - v.shape[-1])])


def kernel(x, c, ctx, c_ctx, w_mod, b_mod, norm1_g, w_in, ssd_conv_w, ssd_conv_b, ssd_dt_bias, ssd_a_log,
           ssd_d, ssd_norm_g, sc_conv_w, w_out, norm2_g, w_router, b_router, w_gate_up, b_gate_up, w_down,
           b_down, final_g):
    bsz = x.shape[0]
    n_tok = bsz * SEQ
    n_assign = n_tok * TOP_K
    n_blocks = n_assign // MOE_BM + N_EXPERTS
    n_rows = n_blocks * MOE_BM
    li = 0

    cvec = jnp.concatenate([c, c_ctx[None, :], jnp.zeros((7, D_MODEL), F32)], axis=0)
    mod3 = _mod(cvec, w_mod[li], b_mod[li][None, :]).reshape(bsz + 8, N_MOD, D_MODEL)

    w = w_in[li]
    wz = w[:, Z0:X0].astype(BF16)
    wxbc = w[:, X0:DT0].astype(BF16)
    wdt = _pad_lanes(w[:, DT0:SC0]).astype(BF16)
    wb = w[:, SC0:SC0 + D_SC].astype(BF16)
    wc = w[:, SC0 + D_SC:SC0 + 2 * D_SC].astype(BF16)
    wu = w[:, SC0 + 2 * D_SC:].astype(BF16)
    g1 = norm1_g[li][None, :]
    cw = ssd_conv_w[li]
    cb = ssd_conv_b[li][None, :]
    dtb = _pad_lanes(ssd_dt_bias[li].reshape(1, 2 * HEADS))
    alog = _pad_lanes(ssd_a_log[li].reshape(1, 2 * HEADS))
    e64, e128 = _expansion_matrices()

    h0 = _ctx_states(ctx, mod3, g1, wxbc[:, :XB_W], wdt, cw[:, :XB_W], cb[:, :XB_W], dtb, alog, e64)

    x2 = x.reshape(n_tok, D_MODEL)
    z, xbc, dtr, scb, v = _inproj(x2, mod3, g1, wz, wxbc, wdt, wb, wc, wu)

    dsk = jnp.repeat(ssd_d[li], HEAD_DIM)[None, :]
    yssd = _ssd(xbc.reshape(bsz, SEQ, XBC_W), z.reshape(bsz, SEQ, D_SSD), dtr.reshape(bsz, SEQ, LANES), h0,
                cw, cb, dtb, alog, dsk, ssd_norm_g[li][None, :], e64, e128)

    wo = w_out[li].astype(BF16)
    g2 = norm2_g[li][None, :]
    x1, lgt = _outproj(x2, yssd.reshape(n_tok, D_SSD), scb, v, mod3, sc_conv_w[li], wo[:D_SSD], wo[D_SSD:],
                       g2, w_router[li].T, b_router[li][:, None])

    dest_t, gate_t, idx_t, meta = _route(lgt, n_blocks)
    dest_flat = dest_t.T.reshape(n_assign)
    cnt = meta[0, :N_EXPERTS]
    start = meta[1, :N_EXPERTS]
    blk_exp = meta[2, :n_blocks]
    n_used = meta[3, :1]

    meta_rows = _pad_lanes(jnp.concatenate(
        [idx_t.T.astype(F32), gate_t.T, jnp.arange(n_tok, dtype=F32)[:, None]], axis=1))
    pad_meta = _pad_lanes(jnp.concatenate(
        [jnp.full((MOE_BM, TOP_K), -1.0, F32), jnp.zeros((MOE_BM, TOP_K), F32),
         (TOP_K * n_tok + jnp.arange(MOE_BM, dtype=F32))[:, None]], axis=1))
    zsrc = jnp.concatenate([jnp.zeros((MOE_BM, D_MODEL), F32), pad_meta], axis=1)

    xs = _dispatch(dest_flat, cnt, start, n_used, x1, meta_rows, mod3, g2, zsrc, n_rows)
    ytm = _experts(blk_exp, n_used, xs, w_gate_up[li].astype(BF16), b_gate_up[li][:, None, :],
                   w_down[li].astype(BF16), b_down[li][:, None, :], n_tok)
    out = _combine(ytm, x1, mod3, final_g[None, :])
    return out.reshape(bsz, SEQ, D_MODEL)
```

```python
import functools

import jax
import jax.numpy as jnp
from jax import lax
from jax.experimental import pallas as pl
from jax.experimental.pallas import tpu as pltpu

F32 = jnp.float32
BF16 = jnp.bfloat16
HIGHEST = lax.Precision.HIGHEST

D_MODEL = 1024
SEQ = 2048
CTX_LEN = 256
GRID_W = 64
D_SSD = 1024
D_SC = 1024
HEAD_DIM = 64
HEADS = 16
GROUPS = 2
STATE = 128
CHUNK = 128
N_EXPERTS = 32
TOP_K = 4
D_FF = 1024
SWIGLU_LIMIT = 7.0
SWIGLU_ALPHA = 1.702
NORM_EPS = 1e-6
N_MOD = 6
XBC_W = D_SSD + 2 * GROUPS * STATE
XB_W = D_SSD + GROUPS * STATE
LANES = 128

Z0 = 0
X0 = Z0 + D_SSD
B0 = X0 + D_SSD
C0 = B0 + GROUPS * STATE
DT0 = C0 + GROUPS * STATE
SC0 = DT0 + 2 * HEADS

TOK_TILE = 512
MOE_BM = 256
RT_TILE = 512
DISP_TILE = 256
COMB_TILE = 512
SUBLANES = 8
PACK_W = D_MODEL // 2
META_IDX = 0
META_GATE = TOP_K
META_TOK = 2 * TOP_K
VMEM_LIMIT = 56 * 1024 * 1024


def _silu(v):
    return v * jax.nn.sigmoid(v)


def _softplus(v):
    return jnp.maximum(v, 0.0) + jnp.log1p(jnp.exp(-jnp.abs(v)))


def _rms(v):
    return v * lax.rsqrt(jnp.mean(v * v, axis=-1, keepdims=True) + NORM_EPS)


def _dot(a, b):
    return jnp.dot(a, b, preferred_element_type=F32)


def _expand(v, e, pieces):
    acc = None
    rem = v
    for _ in range(pieces):
        p = rem.astype(BF16)
        rem = rem - p.astype(F32)
        t = _dot(p, e)
        acc = t if acc is None else acc + t
    return acc


def _mod_kernel(c_ref, w_ref, b_ref, o_ref):
    o_ref[...] = jnp.dot(_silu(c_ref[...]), w_ref[...], precision=HIGHEST,
                         preferred_element_type=F32) + b_ref[...]


def _mod(cvec, w_mod, b_mod):
    rows = cvec.shape[0]
    n = w_mod.shape[1]
    tn = 1536
    return pl.pallas_call(
        _mod_kernel,
        out_shape=jax.ShapeDtypeStruct((rows, n), F32),
        grid=(n // tn,),
        in_specs=[pl.BlockSpec((rows, D_MODEL), lambda j: (0, 0)),
                  pl.BlockSpec((D_MODEL, tn), lambda j: (0, j)),
                  pl.BlockSpec((1, tn), lambda j: (0, j))],
        out_specs=pl.BlockSpec((rows, tn), lambda j: (0, j)),
        compiler_params=pltpu.CompilerParams(dimension_semantics=("arbitrary",),
                                             vmem_limit_bytes=VMEM_LIMIT),
        name="mod",
    )(cvec, w_mod, b_mod)


def _ctx_kernel(ctx_ref, mod_ref, g1_ref, wxb_ref, wdt_ref, cw_ref, cb_ref, dtb_ref, alog_ref, e64_ref,
                h0_ref):
    L = CTX_LEN
    m = mod_ref[0]
    hc = _rms(ctx_ref[0]) * g1_ref[...] * (1.0 + m[1:2]) + m[0:1]
    hb = hc.astype(BF16)
    pxb = _dot(hb, wxb_ref[...])
    dtr = _dot(hb, wdt_ref[...])
    rowi = lax.broadcasted_iota(jnp.int32, (L, XB_W), 0)
    dn = jnp.where(rowi == 0, 0.0, pltpu.roll(pxb, 1, 0))
    up = jnp.where(rowi == L - 1, 0.0, pltpu.roll(pxb, L - 1, 0))
    cw = cw_ref[...]
    xb = _silu(cw[0:1] * dn + cw[1:2] * pxb + cw[2:3] * up + cb_ref[...])
    xs = xb[:, :D_SSD]
    bm = xb[:, D_SSD:].astype(BF16)
    dt = _softplus(dtr + dtb_ref[...])
    da = dt * (-jnp.exp(alog_ref[...]))
    ri = lax.broadcasted_iota(jnp.int32, (L, L), 0)
    ci = lax.broadcasted_iota(jnp.int32, (L, L), 1)
    for d in range(2):
        tri = (ci <= ri) if d == 0 else (ci >= ri)
        cum = jnp.dot(tri.astype(F32), da, precision=HIGHEST, preferred_element_type=F32)
        last = cum[L - 1:L] if d == 0 else cum[0:1]
        w_e = _expand(jnp.exp(last - cum) * dt, e64_ref[d], 2)
        xw = (xs * w_e).astype(BF16)
        for g in range(GROUPS):
            gw = D_SSD // GROUPS
            st = lax.dot_general(bm[:, g * STATE:(g + 1) * STATE], xw[:, g * gw:(g + 1) * gw],
                                 (((0,), (0,)), ((), ())), preferred_element_type=F32)
            h0_ref[0, d, :, g * gw:(g + 1) * gw] = st


def _ctx_states(ctx, mod3, g1, wxb, wdt, cw, cb, dtb, alog, e64):
    bsz = ctx.shape[0]
    mod_row = bsz
    const = lambda *shape: pl.BlockSpec(shape, lambda b: (0,) * len(shape))
    return pl.pallas_call(
        _ctx_kernel,
        out_shape=jax.ShapeDtypeStruct((bsz, 2, STATE, D_SSD), F32),
        grid=(bsz,),
        in_specs=[pl.BlockSpec((1, CTX_LEN, D_MODEL), lambda b: (b, 0, 0)),
                  pl.BlockSpec((1, N_MOD, D_MODEL), lambda b: (mod_row, 0, 0)),
                  const(1, D_MODEL), const(D_MODEL, XB_W), const(D_MODEL, LANES),
                  const(3, XB_W), const(1, XB_W), const(1, LANES), const(1, LANES),
                  const(2, LANES, D_SSD)],
        out_specs=pl.BlockSpec((1, 2, STATE, D_SSD), lambda b: (b, 0, 0, 0)),
        compiler_params=pltpu.CompilerParams(dimension_semantics=("arbitrary",),
                                             vmem_limit_bytes=VMEM_LIMIT),
        name="ctx_states",
    )(ctx, mod3, g1, wxb, wdt, cw, cb, dtb, alog, e64)


def _inproj_kernel(x_ref, mod_ref, g1_ref, wz_ref, wxbc_ref, wdt_ref, wb_ref, wc_ref, wu_ref,
                   z_ref, xbc_ref, dt_ref, scb_ref, v_ref):
    m = mod_ref[0]
    hx = _rms(x_ref[...]) * g1_ref[...] * (1.0 + m[1:2]) + m[0:1]
    hb = hx.astype(BF16)
    z_ref[...] = _dot(hb, wz_ref[...]).astype(BF16)
    xbc_ref[...] = _dot(hb, wxbc_ref[...]).astype(BF16)
    dt_ref[...] = _dot(hb, wdt_ref[...])
    scb_ref[...] = _dot(hb, wb_ref[...]).astype(BF16)
    v_ref[...] = (_dot(hb, wc_ref[...]) * _dot(hb, wu_ref[...])).astype(BF16)


def _inproj(x2, mod3, g1, wz, wxbc, wdt, wb, wc, wu):
    t = x2.shape[0]
    tm = TOK_TILE
    per_b = SEQ // tm
    const = lambda *shape: pl.BlockSpec(shape, lambda i: (0,) * len(shape))
    tile = lambda w: pl.BlockSpec((tm, w), lambda i: (i, 0))
    return pl.pallas_call(
        _inproj_kernel,
        out_shape=(jax.ShapeDtypeStruct((t, D_SSD), BF16), jax.ShapeDtypeStruct((t, XBC_W), BF16),
                   jax.ShapeDtypeStruct((t, LANES), F32), jax.ShapeDtypeStruct((t, D_SC), BF16),
                   jax.ShapeDtypeStruct((t, D_SC), BF16)),
        grid=(t // tm,),
        in_specs=[tile(D_MODEL),
                  pl.BlockSpec((1, N_MOD, D_MODEL), lambda i: (i // per_b, 0, 0)),
                  const(1, D_MODEL), const(D_MODEL, D_SSD), const(D_MODEL, XBC_W), const(D_MODEL, LANES),
                  const(D_MODEL, D_SC), const(D_MODEL, D_SC), const(D_MODEL, D_SC)],
        out_specs=(tile(D_SSD), tile(XBC_W), tile(LANES), tile(D_SC), tile(D_SC)),
        compiler_params=pltpu.CompilerParams(dimension_semantics=("arbitrary",),
                                             vmem_limit_bytes=VMEM_LIMIT),
        name="inproj",
    )(x2, mod3, g1, wz, wxbc, wdt, wb, wc, wu)


def _ssd_kernel(xbc_ref, z_ref, dt_ref, h0_ref, cw_ref, cb_ref, dtb_ref, alog_ref, dsk_ref, g_ref,
                e64_ref, e128_ref, o_ref, xc_ref, y_ref, s_ref):
    Q = CHUNK
    nck = SEQ // Q
    gw = D_SSD // GROUPS

    rowi = lax.broadcasted_iota(jnp.int32, (Q, XBC_W), 0)

    def conv_body(c, carry):
        r0 = pl.multiple_of(c * Q, Q)
        main = xbc_ref[0, pl.ds(r0, Q), :].astype(F32)
        pstart = pl.multiple_of(jnp.maximum(r0 - 16, 0), 16)
        nstart = pl.multiple_of(jnp.minimum(r0 + Q, SEQ - 16), 16)
        prev = xbc_ref[0, pl.ds(pstart, 16), :].astype(F32)[15:16]
        nxt = xbc_ref[0, pl.ds(nstart, 16), :].astype(F32)[0:1]
        prev = jnp.where(c > 0, prev, 0.0)
        nxt = jnp.where(c < nck - 1, nxt, 0.0)
        dn = jnp.where(rowi == 0, prev, pltpu.roll(main, 1, 0))
        up = jnp.where(rowi == Q - 1, nxt, pltpu.roll(main, Q - 1, 0))
        cw = cw_ref[...]
        conv = cw[0:1] * dn + cw[1:2] * main + cw[2:3] * up + cb_ref[...]
        xc_ref[pl.ds(r0, Q), :] = _silu(conv).astype(BF16)
        return carry

    lax.fori_loop(0, nck, conv_body, 0)

    ri = lax.broadcasted_iota(jnp.int32, (Q, Q), 0)
    ci = lax.broadcasted_iota(jnp.int32, (Q, Q), 1)
    lane = lax.broadcasted_iota(jnp.int32, (Q, LANES), 1)
    a_neg = -jnp.exp(alog_ref[...])

    def chunk(c, d):
        r0 = pl.multiple_of(c * Q, Q)
        rows = pl.ds(r0, Q)
        xs = xc_ref[rows, 0:D_SSD].astype(F32)
        bm = xc_ref[rows, D_SSD:D_SSD + GROUPS * STATE]
        cm = xc_ref[rows, D_SSD + GROUPS * STATE:XBC_W]
        dt = _softplus(dt_ref[0, rows, :] + dtb_ref[...])
        da = dt * a_neg
        tri = (ci <= ri) if d == 0 else (ci >= ri)
        cum = jnp.dot(tri.astype(F32), da, precision=HIGHEST, preferred_element_type=F32)
        cum_t = cum.T
        last = cum[Q - 1:Q] if d == 0 else cum[0:1]
        ecum_e = _expand(jnp.exp(cum), e64_ref[d], 2)
        dt_e = _expand(dt, e64_ref[d], 2)
        w_e = _expand(jnp.exp(last - cum) * dt, e64_ref[d], 2)
        colb = _expand(cum, e128_ref[d], 3)
        decay_e = ecum_e[Q - 1:Q] if d == 0 else ecum_e[0:1]

        gmat = [lax.dot_general(cm[:, g * STATE:(g + 1) * STATE], bm[:, g * STATE:(g + 1) * STATE],
                                (((1,), (1,)), ((), ())), preferred_element_type=F32)
                for g in range(GROUPS)]
        xdt = xs * dt_e
        y_parts = []
        for p in range(HEADS // 2):
            g = (2 * p) // (HEADS // GROUPS)
            ms = []
            for hh in (2 * p, 2 * p + 1):
                seg = colb[:, hh * LANES:(hh + 1) * LANES] - cum_t[HEADS * d + hh:HEADS * d + hh + 1, :]
                mm = jnp.where(tri, jnp.exp(jnp.where(tri, seg, 0.0)), 0.0) * gmat[g]
                ms.append(mm.astype(BF16))
            mcat = jnp.concatenate(ms, axis=1)
            xp = xdt[:, p * LANES:(p + 1) * LANES]
            rhs = jnp.concatenate([jnp.where(lane < HEAD_DIM, xp, 0.0).astype(BF16),
                                   jnp.where(lane >= HEAD_DIM, xp, 0.0).astype(BF16)], axis=0)
            y_parts.append(_dot(mcat, rhs))
        y_diag = jnp.concatenate(y_parts, axis=1)

        s_old = s_ref[...]
        s_bf = s_old.astype(BF16)
        y_off = jnp.concatenate(
            [_dot(cm[:, g * STATE:(g + 1) * STATE], s_bf[:, g * gw:(g + 1) * gw]) for g in range(GROUPS)],
            axis=1)
        y = y_diag + y_off * ecum_e

        xw = (xs * w_e).astype(BF16)
        upd = jnp.concatenate(
            [lax.dot_general(bm[:, g * STATE:(g + 1) * STATE], xw[:, g * gw:(g + 1) * gw],
                             (((0,), (0,)), ((), ())), preferred_element_type=F32) for g in range(GROUPS)],
            axis=1)
        s_ref[...] = s_old * decay_e + upd

        if d == 0:
            y_ref[rows, :] = y + dsk_ref[...] * xs
        else:
            tot = y_ref[rows, :] + y
            zz = z_ref[0, rows, :].astype(F32)
            gz = tot * _silu(zz)
            outs = []
            for g in range(GROUPS):
                gg = gz[:, g * gw:(g + 1) * gw]
                outs.append(gg * lax.rsqrt(jnp.mean(gg * gg, axis=-1, keepdims=True) + NORM_EPS))
            o_ref[0, rows, :] = (jnp.concatenate(outs, axis=1) * g_ref[...]).astype(BF16)

    s_ref[...] = h0_ref[0, 0]

    def fwd_body(i, carry):
        chunk(i, 0)
        return carry

    lax.fori_loop(0, nck, fwd_body, 0)

    s_ref[...] = h0_ref[0, 1]

    def bwd_body(i, carry):
        chunk(nck - 1 - i, 1)
        return carry

    lax.fori_loop(0, nck, bwd_body, 0)


def _ssd(xbc3, z3, dt3, h0, cw, cb, dtb, alog, dsk, g, e64, e128):
    bsz = xbc3.shape[0]
    const = lambda *shape: pl.BlockSpec(shape, lambda b: (0,) * len(shape))
    seq = lambda w: pl.BlockSpec((1, SEQ, w), lambda b: (b, 0, 0))
    return pl.pallas_call(
        _ssd_kernel,
        out_shape=jax.ShapeDtypeStruct((bsz, SEQ, D_SSD), BF16),
        grid=(bsz,),
        in_specs=[seq(XBC_W), seq(D_SSD), seq(LANES),
                  pl.BlockSpec((1, 2, STATE, D_SSD), lambda b: (b, 0, 0, 0)),
                  const(3, XBC_W), const(1, XBC_W), const(1, LANES), const(1, LANES),
                  const(1, D_SSD), const(1, D_SSD), const(2, LANES, D_SSD), const(2, LANES, HEADS * LANES)],
        out_specs=seq(D_SSD),
        scratch_shapes=[pltpu.VMEM((SEQ, XBC_W), BF16), pltpu.VMEM((SEQ, D_SSD), F32),
                        pltpu.VMEM((STATE, D_SSD), F32)],
        compiler_params=pltpu.CompilerParams(dimension_semantics=("arbitrary",),
                                             vmem_limit_bytes=VMEM_LIMIT),
        name="ssd",
    )(xbc3, z3, dt3, h0, cw, cb, dtb, alog, dsk, g, e64, e128)


def _outproj_kernel(x_ref, yssd_ref, scb_ref, v_ref, vp_ref, vn_ref, mod_ref, scw_ref, wo1_ref, wo2_ref,
                    g2_ref, wrt_ref, br_ref, x1_ref, lg_ref):
    tm = TOK_TILE
    per_b = SEQ // tm
    i = pl.program_id(0)
    first = (i % per_b) == 0
    last = (i % per_b) == per_b - 1
    m = mod_ref[0]
    v = v_ref[...].astype(F32)
    vp = jnp.where(first, 0.0, vp_ref[...].astype(F32))
    vn = jnp.where(last, 0.0, vn_ref[...].astype(F32))
    dn = jnp.concatenate([vp, v[:tm - GRID_W]], axis=0)
    up = jnp.concatenate([v[GRID_W:], vn], axis=0)
    scw = scw_ref[...]
    ysc = scb_ref[...].astype(F32) * (scw[0:1] * dn + scw[1:2] * v + scw[2:3] * up)
    out = _dot(yssd_ref[...], wo1_ref[...]) + _dot(ysc.astype(BF16), wo2_ref[...])
    x1 = x_ref[...] + m[2:3] * out
    x1_ref[...] = x1
    h2 = _rms(x1) * g2_ref[...] * (1.0 + m[4:5]) + m[3:4]
    lg_ref[...] = lax.dot_general(wrt_ref[...], h2, (((1,), (1,)), ((), ())), precision=HIGHEST,
                                  preferred_element_type=F32) + br_ref[...]


def _outproj(x2, yssd, scb, v, mod3, scw, wo1, wo2, g2, wrt, br):
    t = x2.shape[0]
    tm = TOK_TILE
    per_b = SEQ // tm
    r = tm // GRID_W
    nrow = t // GRID_W
    const = lambda *shape: pl.BlockSpec(shape, lambda i: (0,) * len(shape))
    tile = lambda w: pl.BlockSpec((tm, w), lambda i: (i, 0))
    return pl.pallas_call(
        _outproj_kernel,
        out_shape=(jax.ShapeDtypeStruct((t, D_MODEL), F32), jax.ShapeDtypeStruct((N_EXPERTS, t), F32)),
        grid=(t // tm,),
        in_specs=[tile(D_MODEL), tile(D_SSD), tile(D_SC), tile(D_SC),
                  pl.BlockSpec((GRID_W, D_SC), lambda i: (jnp.maximum(i * r - 1, 0), 0)),
                  pl.BlockSpec((GRID_W, D_SC), lambda i: (jnp.minimum((i + 1) * r, nrow - 1), 0)),
                  pl.BlockSpec((1, N_MOD, D_MODEL), lambda i: (i // per_b, 0, 0)),
                  const(3, D_SC), const(D_SSD, D_MODEL), const(D_SC, D_MODEL), const(1, D_MODEL),
                  const(N_EXPERTS, D_MODEL), const(N_EXPERTS, 1)],
        out_specs=(tile(D_MODEL), pl.BlockSpec((N_EXPERTS, tm), lambda i: (0, i))),
        compiler_params=pltpu.CompilerParams(dimension_semantics=("arbitrary",),
                                             vmem_limit_bytes=VMEM_LIMIT),
        name="outproj",
    )(x2, yssd, scb, v, v, v, mod3, scw, wo1, wo2, g2, wrt, br)


def _route_kernel(lg_ref, dest_ref, gate_ref, idx_ref, meta_ref, rank_ref, carry_ref, *, n_tok, n_blocks):
    tt = RT_TILE
    ne = N_EXPERTS
    eio = lax.broadcasted_iota(jnp.int32, (ne, tt), 0)
    si = lax.broadcasted_iota(jnp.int32, (tt, tt), 0)
    ti = lax.broadcasted_iota(jnp.int32, (tt, tt), 1)
    before = (si < ti).astype(BF16)
    carry_ref[...] = jnp.zeros_like(carry_ref)

    def tile_body(j, c):
        t0 = pl.multiple_of(j * tt, tt)
        l = lg_ref[:, pl.ds(t0, tt)]
        onehot = jnp.zeros((ne, tt), F32)
        tops, sels = [], []
        for _ in range(TOP_K):
            mx = jnp.max(l, axis=0, keepdims=True)
            idx = jnp.min(jnp.where(l == mx, eio, ne), axis=0, keepdims=True)
            sel = eio == idx
            l = jnp.where(sel, -jnp.inf, l)
            onehot = onehot + sel.astype(F32)
            tops.append(mx)
            sels.append(sel)
            idx_ref[pl.ds(len(tops) - 1, 1), pl.ds(t0, tt)] = idx
        ex = [jnp.exp(tv - tops[0]) for tv in tops]
        den = ex[0] + ex[1] + ex[2] + ex[3]
        prefix = _dot(onehot.astype(BF16), before) + carry_ref[:, 0:1]
        for k in range(TOP_K):
            gate_ref[pl.ds(k, 1), pl.ds(t0, tt)] = ex[k] / den
            rk = jnp.sum(jnp.where(sels[k], prefix, 0.0), axis=0, keepdims=True)
            rank_ref[pl.ds(k, 1), pl.ds(t0, tt)] = rk.astype(jnp.int32)
        carry_ref[...] = carry_ref[...] + jnp.sum(onehot, axis=1, keepdims=True)
        return c

    lax.fori_loop(0, n_tok // tt, tile_body, 0)

    counts = carry_ref[...]
    padded = jnp.floor((counts + (MOE_BM - 1)) * (1.0 / MOE_BM)) * MOE_BM
    er = lax.broadcasted_iota(jnp.int32, (ne, ne), 0)
    ec = lax.broadcasted_iota(jnp.int32, (ne, ne), 1)
    pad_start = jnp.dot((ec < er).astype(F32), padded, precision=HIGHEST, preferred_element_type=F32)
    pad_end = pad_start + padded

    def dest_body(j, c):
        t0 = pl.multiple_of(j * tt, tt)
        for k in range(TOP_K):
            idx = idx_ref[pl.ds(k, 1), pl.ds(t0, tt)]
            base = jnp.sum(jnp.where(eio == idx, pad_start[:, 0:1], 0.0), axis=0, keepdims=True)
            dest_ref[pl.ds(k, 1), pl.ds(t0, tt)] = base.astype(jnp.int32) + rank_ref[pl.ds(k, 1), pl.ds(t0, tt)]
        return c

    lax.fori_loop(0, n_tok // tt, dest_body, 0)

    width = meta_ref.shape[1]
    sub = lax.broadcasted_iota(jnp.int32, (ne, width), 0)
    lan = lax.broadcasted_iota(jnp.int32, (ne, width), 1)
    diag = sub == lan
    cnt_row = jnp.sum(jnp.where(diag, counts[:, 0:1], 0.0), axis=0, keepdims=True)
    start_row = jnp.sum(jnp.where(diag, pad_start[:, 0:1], 0.0), axis=0, keepdims=True)
    blk_start = (lan * MOE_BM).astype(F32)
    blk_exp = jnp.sum((pad_end[:, 0:1] <= blk_start).astype(F32), axis=0, keepdims=True)
    blk_exp = jnp.minimum(blk_exp, float(ne - 1))
    used = jnp.sum(padded[:, 0:1], axis=0, keepdims=True) * (1.0 / MOE_BM)
    meta_ref[0:1, :] = cnt_row.astype(jnp.int32)
    meta_ref[1:2, :] = start_row.astype(jnp.int32)
    meta_ref[2:3, :] = blk_exp.astype(jnp.int32)
    meta_ref[3:4, :] = jnp.broadcast_to(used, (1, width)).astype(jnp.int32)
    meta_ref[4:8, :] = jnp.zeros((4, width), jnp.int32)


def _route(lgt, n_blocks):
    ne, n_tok = lgt.shape
    width = -(-n_blocks // LANES) * LANES
    full = lambda *shape: pl.BlockSpec(shape, lambda: (0,) * len(shape))
    return pl.pallas_call(
        functools.partial(_route_kernel, n_tok=n_tok, n_blocks=n_blocks),
        out_shape=(jax.ShapeDtypeStruct((TOP_K, n_tok), jnp.int32),
                   jax.ShapeDtypeStruct((TOP_K, n_tok), F32),
                   jax.ShapeDtypeStruct((TOP_K, n_tok), jnp.int32),
                   jax.ShapeDtypeStruct((8, width), jnp.int32)),
        in_specs=[full(ne, n_tok)],
        out_specs=(full(TOP_K, n_tok), full(TOP_K, n_tok), full(TOP_K, n_tok), full(8, width)),
        scratch_shapes=[pltpu.VMEM((TOP_K, n_tok), jnp.int32), pltpu.VMEM((ne, LANES), F32)],
        compiler_params=pltpu.CompilerParams(vmem_limit_bytes=VMEM_LIMIT),
        name="route",
    )(lgt)


def _dispatch_kernel(dest_ref, cnt_ref, start_ref, nu_ref, x1_ref, meta_ref, mod_ref, g2_ref, zsrc_ref, xs_ref,
                     hbuf, sem, zsem):
    i = pl.program_id(0)
    n = pl.num_programs(0)
    tl = DISP_TILE
    slot = i % 2
    nb = xs_ref.shape[0] // MOE_BM

    def zero_block(b):
        return pltpu.make_async_copy(zsrc_ref, xs_ref.at[pl.ds(b * MOE_BM, MOE_BM)], zsem)

    @pl.when(i == 0)
    def _():
        def start_e(e, c):
            @pl.when(cnt_ref[e] > 0)
            def _():
                zero_block((start_ref[e] + cnt_ref[e] - 1) // MOE_BM).start()
            return c

        def wait_e(e, c):
            @pl.when(cnt_ref[e] > 0)
            def _():
                zero_block(0).wait()
            return c

        def start_t(b, c):
            zero_block(b).start()
            return c

        def wait_t(b, c):
            zero_block(0).wait()
            return c

        lax.fori_loop(0, N_EXPERTS, start_e, 0)
        lax.fori_loop(nu_ref[0], nb, start_t, 0)
        lax.fori_loop(0, N_EXPERTS, wait_e, 0)
        lax.fori_loop(nu_ref[0], nb, wait_t, 0)

    m = mod_ref[0]
    h2 = _rms(x1_ref[...]) * g2_ref[...] * (1.0 + m[4:5]) + m[3:4]
    lo = pltpu.bitcast(h2[:, :PACK_W].astype(BF16).astype(F32), jnp.uint32) >> 16
    hi = pltpu.bitcast(h2[:, PACK_W:].astype(BF16).astype(F32), jnp.uint32) & jnp.uint32(0xFFFF0000)
    row = jnp.concatenate([lo | hi, meta_ref[...], jnp.zeros((tl, D_MODEL - PACK_W - LANES), jnp.uint32)], axis=1)
    hbuf[slot] = row.reshape(tl, SUBLANES, LANES)

    def per_tok(t, c):
        tok = i * tl + t
        for k in range(TOP_K):
            d = dest_ref[tok * TOP_K + k]
            pltpu.make_async_copy(hbuf.at[slot, t], xs_ref.at[d], sem.at[slot]).start()
        return c

    lax.fori_loop(0, tl, per_tok, 0)

    def wait_slot(sl):
        for _ in range(TOP_K):
            pltpu.make_async_copy(hbuf.at[sl], xs_ref.at[pl.ds(0, tl)], sem.at[sl]).wait()

    @pl.when(i > 0)
    def _():
        wait_slot(1 - slot)

    @pl.when(i == n - 1)
    def _():
        wait_slot(slot)


def _dispatch(dest_flat, cnt, start, n_used, x1, meta_rows, mod3, g2, zsrc, n_rows):
    n_tok = x1.shape[0]
    tl = DISP_TILE
    per_b = SEQ // tl
    return pl.pallas_call(
        _dispatch_kernel,
        out_shape=jax.ShapeDtypeStruct((n_rows, SUBLANES, LANES), jnp.uint32),
        grid_spec=pltpu.PrefetchScalarGridSpec(
            num_scalar_prefetch=4,
            grid=(n_tok // tl,),
            in_specs=[pl.BlockSpec((tl, D_MODEL), lambda i, *_: (i, 0)),
                      pl.BlockSpec((tl, LANES), lambda i, *_: (i, 0)),
                      pl.BlockSpec((1, N_MOD, D_MODEL), lambda i, *_: (i // per_b, 0, 0)),
                      pl.BlockSpec((1, D_MODEL), lambda i, *_: (0, 0)),
                      pl.BlockSpec((MOE_BM, SUBLANES, LANES), lambda i, *_: (0, 0, 0))],
            out_specs=pl.BlockSpec(memory_space=pl.ANY),
            scratch_shapes=[pltpu.VMEM((2, tl, SUBLANES, LANES), jnp.uint32),
                            pltpu.SemaphoreType.DMA((2,)), pltpu.SemaphoreType.DMA]),
        compiler_params=pltpu.CompilerParams(dimension_semantics=("arbitrary",),
                                             vmem_limit_bytes=VMEM_LIMIT),
        name="dispatch",
    )(dest_flat, cnt, start, n_used, x1, meta_rows, mod3, g2, zsrc)


def _expert_kernel(be_ref, nu_ref, xs_ref, wgu_ref, bgu_ref, wd_ref, bd_ref, zero_ref, ytm_ref,
                   ybuf, stage, idv, ids, sc_sem, id_sem, z_sem, *, n_tok):
    j = pl.program_id(0)
    nbk = pl.num_programs(0) - 1
    nu = nu_ref[0]
    slot = j % 2
    prev = 1 - slot

    @pl.when(j == 0)
    def _():
        cp = pltpu.make_async_copy(zero_ref, ytm_ref.at[pl.ds(TOP_K * n_tok, MOE_BM)], z_sem)
        cp.start()
        cp.wait()

    n_pc = 4
    pw1 = D_FF // n_pc
    pw2 = D_MODEL // n_pc
    groups = [48] * n_pc + [16] * n_pc
    assert sum(groups) == MOE_BM

    def scatter_group(g):
        lo = sum(groups[:g])
        for r in range(lo, lo + groups[g]):
            pltpu.make_async_copy(ybuf.at[prev, r], ytm_ref.at[ids[prev, 0, r]],
                                  sc_sem.at[prev]).start(priority=r % 2)

    def scatter_wait(sl):
        pltpu.make_async_copy(ybuf.at[sl], ytm_ref.at[pl.ds(0, MOE_BM)], sc_sem.at[sl]).wait()

    def compute(with_scatter):
        e_f = be_ref[jnp.minimum(j, nbk - 1)].astype(F32)
        words = xs_ref[...].reshape(MOE_BM, D_MODEL)
        packed = words[:, 0:PACK_W]
        meta = pltpu.bitcast(words[:, PACK_W:PACK_W + LANES], F32)
        xb = jnp.concatenate(
            [pltpu.bitcast(packed << 16, F32).astype(BF16),
             pltpu.bitcast(packed & jnp.uint32(0xFFFF0000), F32).astype(BF16)], axis=1)
        gate = jnp.zeros((MOE_BM, 1), F32)
        kk = jnp.zeros((MOE_BM, 1), F32)
        for k in range(TOP_K):
            mk = meta[:, META_IDX + k:META_IDX + k + 1] == e_f
            gate = gate + jnp.where(mk, meta[:, META_GATE + k:META_GATE + k + 1], 0.0)
            kk = kk + jnp.where(mk, float(k), 0.0)
        row = kk * float(n_tok) + meta[:, META_TOK:META_TOK + 1]
        row_t = jnp.broadcast_to(row, (MOE_BM, LANES)).T
        idv[slot] = row_t[0:8].astype(jnp.int32)
        pltpu.make_async_copy(idv.at[slot], ids.at[slot], id_sem.at[slot]).start()
        for c in range(n_pc):
            if with_scatter:
                scatter_group(c)
            cg = slice(c * pw1, (c + 1) * pw1)
            cl = slice(D_FF + c * pw1, D_FF + (c + 1) * pw1)
            glu = jnp.minimum(_dot(xb, wgu_ref[0, :, cg]) + bgu_ref[0, :, cg], SWIGLU_LIMIT)
            lin = jnp.clip(_dot(xb, wgu_ref[0, :, cl]) + bgu_ref[0, :, cl], -SWIGLU_LIMIT, SWIGLU_LIMIT)
            stage[:, cg] = glu * jax.nn.sigmoid(SWIGLU_ALPHA * glu) * (lin + 1.0)
        act = stage[...].astype(BF16)
        for c in range(n_pc):
            if with_scatter:
                scatter_group(n_pc + c)
            cs = slice(c * pw2, (c + 1) * pw2)
            stage[:, cs] = (_dot(act, wd_ref[0, :, cs]) + bd_ref[0, :, cs]) * gate
        ybuf[slot] = stage[...].reshape(MOE_BM, D_MODEL // LANES, LANES)

    def ids_wait():
        pltpu.make_async_copy(idv.at[prev], ids.at[prev], id_sem.at[prev]).wait()

    @pl.when(jnp.logical_and(j >= 2, j < nu))
    def _():
        scatter_wait(slot)

    @pl.when(j == 0)
    def _():
        compute(False)

    @pl.when(jnp.logical_and(j >= 1, j < nu))
    def _():
        ids_wait()
        compute(True)

    @pl.when(j == nu)
    def _():
        ids_wait()
        for g in range(len(groups)):
            scatter_group(g)
        scatter_wait(prev)

    @pl.when(jnp.logical_and(j == nu, j >= 2))
    def _():
        scatter_wait(slot)


def _experts(blk_exp, n_used, xs, wgu, bgu, wd, bd, n_tok):
    n_rows = xs.shape[0]
    nb = n_rows // MOE_BM
    row_blk = lambda j, be, nu: (jnp.minimum(j, nu[0] - 1), 0, 0)
    per_e = lambda j, be, nu: (be[jnp.minimum(j, nb - 1)], 0, 0)
    sub = D_MODEL // LANES
    zero = jnp.zeros((MOE_BM, sub, LANES), F32)
    return pl.pallas_call(
        functools.partial(_expert_kernel, n_tok=n_tok),
        out_shape=jax.ShapeDtypeStruct((TOP_K * n_tok + MOE_BM, sub, LANES), F32),
        grid_spec=pltpu.PrefetchScalarGridSpec(
            num_scalar_prefetch=2,
            grid=(nb + 1,),
            in_specs=[pl.BlockSpec((MOE_BM, SUBLANES, LANES), row_blk),
                      pl.BlockSpec((1, D_MODEL, 2 * D_FF), per_e),
                      pl.BlockSpec((1, 1, 2 * D_FF), per_e),
                      pl.BlockSpec((1, D_FF, D_MODEL), per_e),
                      pl.BlockSpec((1, 1, D_MODEL), per_e),
                      pl.BlockSpec((MOE_BM, sub, LANES), lambda j, be, nu: (0, 0, 0))],
            out_specs=pl.BlockSpec(memory_space=pl.ANY),
            scratch_shapes=[pltpu.VMEM((2, MOE_BM, sub, LANES), F32), pltpu.VMEM((MOE_BM, D_MODEL), F32),
                            pltpu.VMEM((2, 8, MOE_BM), jnp.int32),
                            pltpu.SMEM((2, 8, MOE_BM), jnp.int32), pltpu.SemaphoreType.DMA((2,)),
                            pltpu.SemaphoreType.DMA((2,)), pltpu.SemaphoreType.DMA]),
        compiler_params=pltpu.CompilerParams(dimension_semantics=("arbitrary",),
                                             vmem_limit_bytes=VMEM_LIMIT),
        name="experts",
    )(blk_exp, n_used, xs, wgu, bgu, wd, bd, zero)


def _combine_kernel(y0_ref, y1_ref, y2_ref, y3_ref, x1_ref, mod_ref, fg_ref, o_ref):
    m = mod_ref[0]
    moe = ((y0_ref[...] + y1_ref[...]) + (y2_ref[...] + y3_ref[...])).reshape(COMB_TILE, D_MODEL)
    x2 = x1_ref[...] + m[5:6] * moe
    o_ref[...] = _rms(x2) * fg_ref[...]


def _combine(ytm, x1, mod3, fg):
    n_tok = x1.shape[0]
    tc = COMB_TILE
    per_b = SEQ // tc
    nt = n_tok // tc
    slot_spec = lambda k: pl.BlockSpec((tc, D_MODEL // LANES, LANES), lambda i: (i + k * nt, 0, 0))
    return pl.pallas_call(
        _combine_kernel,
        out_shape=jax.ShapeDtypeStruct((n_tok, D_MODEL), F32),
        grid=(nt,),
        in_specs=[slot_spec(0), slot_spec(1), slot_spec(2), slot_spec(3),
                  pl.BlockSpec((tc, D_MODEL), lambda i: (i, 0)),
                  pl.BlockSpec((1, N_MOD, D_MODEL), lambda i: (i // per_b, 0, 0)),
                  pl.BlockSpec((1, D_MODEL), lambda i: (0, 0))],
        out_specs=pl.BlockSpec((tc, D_MODEL), lambda i: (i, 0)),
        compiler_params=pltpu.CompilerParams(dimension_semantics=("arbitrary",),
                                             vmem_limit_bytes=VMEM_LIMIT),
        name="combine",
    )(ytm, ytm, ytm, ytm, x1, mod3, fg)


def _expansion_matrices():
    r = jnp.arange(LANES)[:, None]
    out64, out128 = [], []
    for d in range(2):
        l64 = jnp.arange(D_SSD)[None, :]
        l128 = jnp.arange(HEADS * LANES)[None, :]
        out64.append((l64 // HEAD_DIM == r - HEADS * d).astype(BF16))
        out128.append((l128 // LANES == r - HEADS * d).astype(BF16))
    return jnp.stack(out64), jnp.stack(out128)


def _pad_lanes(v):
    return jnp.pad(v, [(0, 0)] * (v.ndim - 1) + [(0, LANES - v.shape[-1])])


def kernel(x, c, ctx, c_ctx, w_mod, b_mod, norm1_g, w_in, ssd_conv_w, ssd_conv_b, ssd_dt_bias, ssd_a_log,
           ssd_d, ssd_norm_g, sc_conv_w, w_out, norm2_g, w_router, b_router, w_gate_up, b_gate_up, w_down,
           b_down, final_g):
    bsz = x.shape[0]
    n_tok = bsz * SEQ
    n_assign = n_tok * TOP_K
    n_blocks = n_assign // MOE_BM + N_EXPERTS
    n_rows = n_blocks * MOE_BM
    li = 0

    cvec = jnp.concatenate([c, c_ctx[None, :], jnp.zeros((7, D_MODEL), F32)], axis=0)
    mod3 = _mod(cvec, w_mod[li], b_mod[li][None, :]).reshape(bsz + 8, N_MOD, D_MODEL)

    w = w_in[li]
    wz = w[:, Z0:X0].astype(BF16)
    wxbc = w[:, X0:DT0].astype(BF16)
    wdt = _pad_lanes(w[:, DT0:SC0]).astype(BF16)
    wb = w[:, SC0:SC0 + D_SC].astype(BF16)
    wc = w[:, SC0 + D_SC:SC0 + 2 * D_SC].astype(BF16)
    wu = w[:, SC0 + 2 * D_SC:].astype(BF16)
    g1 = norm1_g[li][None, :]
    cw = ssd_conv_w[li]
    cb = ssd_conv_b[li][None, :]
    dtb = _pad_lanes(ssd_dt_bias[li].reshape(1, 2 * HEADS))
    alog = _pad_lanes(ssd_a_log[li].reshape(1, 2 * HEADS))
    e64, e128 = _expansion_matrices()

    h0 = _ctx_states(ctx, mod3, g1, wxbc[:, :XB_W], wdt, cw[:, :XB_W], cb[:, :XB_W], dtb, alog, e64)

    x2 = x.reshape(n_tok, D_MODEL)
    z, xbc, dtr, scb, v = _inproj(x2, mod3, g1, wz, wxbc, wdt, wb, wc, wu)

    dsk = jnp.repeat(ssd_d[li], HEAD_DIM)[None, :]
    yssd = _ssd(xbc.reshape(bsz, SEQ, XBC_W), z.reshape(bsz, SEQ, D_SSD), dtr.reshape(bsz, SEQ, LANES), h0,
                cw, cb, dtb, alog, dsk, ssd_norm_g[li][None, :], e64, e128)

    wo = w_out[li].astype(BF16)
    g2 = norm2_g[li][None, :]
    x1, lgt = _outproj(x2, yssd.reshape(n_tok, D_SSD), scb, v, mod3, sc_conv_w[li], wo[:D_SSD], wo[D_SSD:],
                       g2, w_router[li].T, b_router[li][:, None])

    dest_t, gate_t, idx_t, meta = _route(lgt, n_blocks)
    dest_flat = dest_t.T.reshape(n_assign)
    cnt = meta[0, :N_EXPERTS]
    start = meta[1, :N_EXPERTS]
    blk_exp = meta[2, :n_blocks]
    n_used = meta[3, :1]

    meta_rows = lax.bitcast_convert_type(_pad_lanes(jnp.concatenate(
        [idx_t.T.astype(F32), gate_t.T, jnp.arange(n_tok, dtype=F32)[:, None]], axis=1)), jnp.uint32)
    pad_meta = lax.bitcast_convert_type(_pad_lanes(jnp.concatenate(
        [jnp.full((MOE_BM, TOP_K), -1.0, F32), jnp.zeros((MOE_BM, TOP_K), F32),
         (TOP_K * n_tok + jnp.arange(MOE_BM, dtype=F32))[:, None]], axis=1)), jnp.uint32)
    zsrc = jnp.concatenate([jnp.zeros((MOE_BM, PACK_W), jnp.uint32), pad_meta,
                            jnp.zeros((MOE_BM, D_MODEL - PACK_W - LANES), jnp.uint32)],
                           axis=1).reshape(MOE_BM, SUBLANES, LANES)

    xs = _dispatch(dest_flat, cnt, start, n_used, x1, meta_rows, mod3, g2, zsrc, n_rows)
    ytm = _experts(blk_exp, n_used, xs, w_gate_up[li].astype(BF16), b_gate_up[li][:, None, :],
                   w_down[li].astype(BF16), b_down[li][:, None, :], n_tok)
    out = _combine(ytm, x1, mod3, final_g[None, :])
    return out.reshape(bsz, SEQ, D_MODEL)
```

```python
import functools

import jax
import jax.numpy as jnp
from jax import lax
from jax.experimental import pallas as pl
from jax.experimental.pallas import tpu as pltpu

F32 = jnp.float32
BF16 = jnp.bfloat16
HIGHEST = lax.Precision.HIGHEST

D_MODEL = 1024
SEQ = 2048
CTX_LEN = 256
GRID_W = 64
D_SSD = 1024
D_SC = 1024
HEAD_DIM = 64
HEADS = 16
GROUPS = 2
STATE = 128
CHUNK = 128
N_EXPERTS = 32
TOP_K = 4
D_FF = 1024
SWIGLU_LIMIT = 7.0
SWIGLU_ALPHA = 1.702
NORM_EPS = 1e-6
N_MOD = 6
XBC_W = D_SSD + 2 * GROUPS * STATE
XB_W = D_SSD + GROUPS * STATE
LANES = 128

Z0 = 0
X0 = Z0 + D_SSD
B0 = X0 + D_SSD
C0 = B0 + GROUPS * STATE
DT0 = C0 + GROUPS * STATE
SC0 = DT0 + 2 * HEADS

TOK_TILE = 512
MOE_BM = 256
RT_TILE = 512
DISP_TILE = 256
COMB_TILE = 512
SUBLANES = 8
PACK_W = D_MODEL // 2
META_IDX = 0
META_GATE = TOP_K
META_TOK = 2 * TOP_K
VMEM_LIMIT = 56 * 1024 * 1024


def _silu(v):
    return v * jax.nn.sigmoid(v)


def _softplus(v):
    return jnp.maximum(v, 0.0) + jnp.log1p(jnp.exp(-jnp.abs(v)))


def _rms(v):
    return v * lax.rsqrt(jnp.mean(v * v, axis=-1, keepdims=True) + NORM_EPS)


def _dot(a, b):
    return jnp.dot(a, b, preferred_element_type=F32)


def _expand2(v, e2):
    hi = v.astype(BF16)
    lo = (v - hi.astype(F32)).astype(BF16)
    return _dot(jnp.concatenate([hi, lo], axis=1), e2)


def _mod_kernel(c_ref, w_ref, b_ref, o_ref):
    o_ref[...] = jnp.dot(_silu(c_ref[...]), w_ref[...], precision=HIGHEST,
                         preferred_element_type=F32) + b_ref[...]


def _mod(cvec, w_mod, b_mod):
    rows = cvec.shape[0]
    n = w_mod.shape[1]
    tn = 1536
    return pl.pallas_call(
        _mod_kernel,
        out_shape=jax.ShapeDtypeStruct((rows, n), F32),
        grid=(n // tn,),
        in_specs=[pl.BlockSpec((rows, D_MODEL), lambda j: (0, 0)),
                  pl.BlockSpec((D_MODEL, tn), lambda j: (0, j)),
                  pl.BlockSpec((1, tn), lambda j: (0, j))],
        out_specs=pl.BlockSpec((rows, tn), lambda j: (0, j)),
        compiler_params=pltpu.CompilerParams(dimension_semantics=("arbitrary",),
                                             vmem_limit_bytes=VMEM_LIMIT),
        name="mod",
    )(cvec, w_mod, b_mod)


def _ctx_kernel(ctx_ref, mod_ref, g1_ref, wxb_ref, wdt_ref, cw_ref, cb_ref, dtb_ref, alog_ref, e64_ref,
                h0_ref):
    L = CTX_LEN
    m = mod_ref[0]
    hc = _rms(ctx_ref[0]) * g1_ref[...] * (1.0 + m[1:2]) + m[0:1]
    hb = hc.astype(BF16)
    pxb = _dot(hb, wxb_ref[...])
    dtr = _dot(hb, wdt_ref[...])
    rowi = lax.broadcasted_iota(jnp.int32, (L, XB_W), 0)
    dn = jnp.where(rowi == 0, 0.0, pltpu.roll(pxb, 1, 0))
    up = jnp.where(rowi == L - 1, 0.0, pltpu.roll(pxb, L - 1, 0))
    cw = cw_ref[...]
    xb = _silu(cw[0:1] * dn + cw[1:2] * pxb + cw[2:3] * up + cb_ref[...])
    xs = xb[:, :D_SSD]
    bm = xb[:, D_SSD:].astype(BF16)
    dt = _softplus(dtr + dtb_ref[...])
    da = dt * (-jnp.exp(alog_ref[...]))
    ri = lax.broadcasted_iota(jnp.int32, (L, L), 0)
    ci = lax.broadcasted_iota(jnp.int32, (L, L), 1)
    for d in range(2):
        tri = (ci <= ri) if d == 0 else (ci >= ri)
        cum = jnp.dot(tri.astype(F32), da, precision=HIGHEST, preferred_element_type=F32)
        last = cum[L - 1:L] if d == 0 else cum[0:1]
        w_e = _expand2(jnp.exp(last - cum) * dt, e64_ref[d])
        xw = (xs * w_e).astype(BF16)
        for g in range(GROUPS):
            gw = D_SSD // GROUPS
            st = lax.dot_general(bm[:, g * STATE:(g + 1) * STATE], xw[:, g * gw:(g + 1) * gw],
                                 (((0,), (0,)), ((), ())), preferred_element_type=F32)
            h0_ref[0, d, :, g * gw:(g + 1) * gw] = st


def _ctx_states(ctx, mod3, g1, wxb, wdt, cw, cb, dtb, alog, e64):
    bsz = ctx.shape[0]
    mod_row = bsz
    const = lambda *shape: pl.BlockSpec(shape, lambda b: (0,) * len(shape))
    return pl.pallas_call(
        _ctx_kernel,
        out_shape=jax.ShapeDtypeStruct((bsz, 2, STATE, D_SSD), F32),
        grid=(bsz,),
        in_specs=[pl.BlockSpec((1, CTX_LEN, D_MODEL), lambda b: (b, 0, 0)),
                  pl.BlockSpec((1, N_MOD, D_MODEL), lambda b: (mod_row, 0, 0)),
                  const(1, D_MODEL), const(D_MODEL, XB_W), const(D_MODEL, LANES),
                  const(3, XB_W), const(1, XB_W), const(1, LANES), const(1, LANES),
                  const(2, 2 * LANES, D_SSD)],
        out_specs=pl.BlockSpec((1, 2, STATE, D_SSD), lambda b: (b, 0, 0, 0)),
        compiler_params=pltpu.CompilerParams(dimension_semantics=("arbitrary",),
                                             vmem_limit_bytes=VMEM_LIMIT),
        name="ctx_states",
    )(ctx, mod3, g1, wxb, wdt, cw, cb, dtb, alog, e64)


def _inproj_kernel(x_ref, mod_ref, g1_ref, wz_ref, wxbc_ref, wdt_ref, wb_ref, wc_ref, wu_ref,
                   z_ref, xbc_ref, dt_ref, scb_ref, v_ref):
    m = mod_ref[0]
    hx = _rms(x_ref[...]) * g1_ref[...] * (1.0 + m[1:2]) + m[0:1]
    hb = hx.astype(BF16)
    z_ref[...] = _dot(hb, wz_ref[...]).astype(BF16)
    xbc_ref[...] = _dot(hb, wxbc_ref[...]).astype(BF16)
    dt_ref[...] = _dot(hb, wdt_ref[...])
    scb_ref[...] = _dot(hb, wb_ref[...]).astype(BF16)
    v_ref[...] = (_dot(hb, wc_ref[...]) * _dot(hb, wu_ref[...])).astype(BF16)


def _inproj(x2, mod3, g1, wz, wxbc, wdt, wb, wc, wu):
    t = x2.shape[0]
    tm = TOK_TILE
    per_b = SEQ // tm
    const = lambda *shape: pl.BlockSpec(shape, lambda i: (0,) * len(shape))
    tile = lambda w: pl.BlockSpec((tm, w), lambda i: (i, 0))
    return pl.pallas_call(
        _inproj_kernel,
        out_shape=(jax.ShapeDtypeStruct((t, D_SSD), BF16), jax.ShapeDtypeStruct((t, XBC_W), BF16),
                   jax.ShapeDtypeStruct((t, LANES), F32), jax.ShapeDtypeStruct((t, D_SC), BF16),
                   jax.ShapeDtypeStruct((t, D_SC), BF16)),
        grid=(t // tm,),
        in_specs=[tile(D_MODEL),
                  pl.BlockSpec((1, N_MOD, D_MODEL), lambda i: (i // per_b, 0, 0)),
                  const(1, D_MODEL), const(D_MODEL, D_SSD), const(D_MODEL, XBC_W), const(D_MODEL, LANES),
                  const(D_MODEL, D_SC), const(D_MODEL, D_SC), const(D_MODEL, D_SC)],
        out_specs=(tile(D_SSD), tile(XBC_W), tile(LANES), tile(D_SC), tile(D_SC)),
        compiler_params=pltpu.CompilerParams(dimension_semantics=("arbitrary",),
                                             vmem_limit_bytes=VMEM_LIMIT),
        name="inproj",
    )(x2, mod3, g1, wz, wxbc, wdt, wb, wc, wu)


def _ssd_kernel(xbc_ref, z_ref, dt_ref, h0_ref, cw_ref, cb_ref, dtb_ref, alog_ref, dsk_ref, g_ref,
                e64_ref, e128_ref, o_ref, xc_ref, y_ref, s_ref):
    Q = CHUNK
    nck = SEQ // Q
    gw = D_SSD // GROUPS

    rowi = lax.broadcasted_iota(jnp.int32, (SUBLANES, XBC_W), 0)

    def conv_body(c, carry):
        r0 = pl.multiple_of(c * Q, Q)
        main = xbc_ref[0, pl.ds(r0, Q), :].astype(F32)
        pstart = pl.multiple_of(jnp.maximum(r0 - 16, 0), 16)
        nstart = pl.multiple_of(jnp.minimum(r0 + Q, SEQ - 16), 16)
        prev = xbc_ref[0, pl.ds(pstart, 16), :].astype(F32)[15:16]
        nxt = xbc_ref[0, pl.ds(nstart, 16), :].astype(F32)[0:1]
        prev = jnp.where(c > 0, prev, 0.0)
        nxt = jnp.where(c < nck - 1, nxt, 0.0)
        dn = pltpu.roll(main, 1, 0)
        up = pltpu.roll(main, Q - 1, 0)
        dn = jnp.concatenate([jnp.where(rowi == 0, prev, dn[0:SUBLANES]), dn[SUBLANES:]], axis=0)
        up = jnp.concatenate([up[:Q - SUBLANES], jnp.where(rowi == SUBLANES - 1, nxt, up[Q - SUBLANES:])], axis=0)
        cw = cw_ref[...]
        conv = cw[0:1] * dn + cw[1:2] * main + cw[2:3] * up + cb_ref[...]
        xc_ref[pl.ds(r0, Q), :] = _silu(conv).astype(BF16)
        return carry

    lax.fori_loop(0, nck, conv_body, 0)

    ri = lax.broadcasted_iota(jnp.int32, (Q, Q), 0)
    ci = lax.broadcasted_iota(jnp.int32, (Q, Q), 1)
    lane = lax.broadcasted_iota(jnp.int32, (Q, LANES), 1)
    a_neg = -jnp.exp(alog_ref[...])

    def chunk(c, d):
        r0 = pl.multiple_of(c * Q, Q)
        rows = pl.ds(r0, Q)
        xs_b = xc_ref[rows, 0:D_SSD]
        xs = xs_b.astype(F32)
        bm = xc_ref[rows, D_SSD:D_SSD + GROUPS * STATE]
        cm = xc_ref[rows, D_SSD + GROUPS * STATE:XBC_W]
        dt = _softplus(dt_ref[0, rows, :] + dtb_ref[...])
        da = dt * a_neg
        tri = (ci <= ri) if d == 0 else (ci >= ri)
        cum = jnp.dot(tri.astype(F32), da, precision=HIGHEST, preferred_element_type=F32)
        sub_t = (cum - jnp.log(dt)).T
        last = cum[Q - 1:Q] if d == 0 else cum[0:1]
        ecum_e = _expand2(jnp.exp(cum), e64_ref[d])
        w_e = _expand2(jnp.exp(last - cum) * dt, e64_ref[d])
        colb = _expand2(cum, e128_ref[d])
        decay_e = ecum_e[Q - 1:Q] if d == 0 else ecum_e[0:1]

        gmat = [lax.dot_general(cm[:, g * STATE:(g + 1) * STATE], bm[:, g * STATE:(g + 1) * STATE],
                                (((1,), (1,)), ((), ())), preferred_element_type=F32)
                for g in range(GROUPS)]
        zero_b = jnp.zeros((Q, LANES), BF16)
        y_parts = []
        for p in range(HEADS // 2):
            g = (2 * p) // (HEADS // GROUPS)
            ms = []
            for hh in (2 * p, 2 * p + 1):
                seg = colb[:, hh * LANES:(hh + 1) * LANES] - sub_t[HEADS * d + hh:HEADS * d + hh + 1, :]
                ms.append((jnp.where(tri, jnp.exp(seg), 0.0) * gmat[g]).astype(BF16))
            mcat = jnp.concatenate(ms, axis=1)
            xp = xs_b[:, p * LANES:(p + 1) * LANES]
            rhs = jnp.concatenate([jnp.where(lane < HEAD_DIM, xp, zero_b),
                                   jnp.where(lane >= HEAD_DIM, xp, zero_b)], axis=0)
            y_parts.append(_dot(mcat, rhs))
        y_diag = jnp.concatenate(y_parts, axis=1)

        s_old = s_ref[...]
        s_bf = s_old.astype(BF16)
        y_off = jnp.concatenate(
            [_dot(cm[:, g * STATE:(g + 1) * STATE], s_bf[:, g * gw:(g + 1) * gw]) for g in range(GROUPS)],
            axis=1)
        y = y_diag + y_off * ecum_e

        xw = (xs * w_e).astype(BF16)
        upd = jnp.concatenate(
            [lax.dot_general(bm[:, g * STATE:(g + 1) * STATE], xw[:, g * gw:(g + 1) * gw],
                             (((0,), (0,)), ((), ())), preferred_element_type=F32) for g in range(GROUPS)],
            axis=1)
        s_ref[...] = s_old * decay_e + upd

        if d == 0:
            y_ref[rows, :] = y + dsk_ref[...] * xs
        else:
            tot = y_ref[rows, :] + y
            zz = z_ref[0, rows, :].astype(F32)
            gz = tot * _silu(zz)
            outs = []
            for g in range(GROUPS):
                gg = gz[:, g * gw:(g + 1) * gw]
                outs.append(gg * lax.rsqrt(jnp.mean(gg * gg, axis=-1, keepdims=True) + NORM_EPS))
            o_ref[0, rows, :] = (jnp.concatenate(outs, axis=1) * g_ref[...]).astype(BF16)

    s_ref[...] = h0_ref[0, 0]

    def fwd_body(i, carry):
        chunk(i, 0)
        return carry

    lax.fori_loop(0, nck, fwd_body, 0)

    s_ref[...] = h0_ref[0, 1]

    def bwd_body(i, carry):
        chunk(nck - 1 - i, 1)
        return carry

    lax.fori_loop(0, nck, bwd_body, 0)


def _ssd(xbc3, z3, dt3, h0, cw, cb, dtb, alog, dsk, g, e64, e128):
    bsz = xbc3.shape[0]
    const = lambda *shape: pl.BlockSpec(shape, lambda b: (0,) * len(shape))
    seq = lambda w: pl.BlockSpec((1, SEQ, w), lambda b: (b, 0, 0))
    return pl.pallas_call(
        _ssd_kernel,
        out_shape=jax.ShapeDtypeStruct((bsz, SEQ, D_SSD), BF16),
        grid=(bsz,),
        in_specs=[seq(XBC_W), seq(D_SSD), seq(LANES),
                  pl.BlockSpec((1, 2, STATE, D_SSD), lambda b: (b, 0, 0, 0)),
                  const(3, XBC_W), const(1, XBC_W), const(1, LANES), const(1, LANES),
                  const(1, D_SSD), const(1, D_SSD), const(2, 2 * LANES, D_SSD),
                  const(2, 2 * LANES, HEADS * LANES)],
        out_specs=seq(D_SSD),
        scratch_shapes=[pltpu.VMEM((SEQ, XBC_W), BF16), pltpu.VMEM((SEQ, D_SSD), F32),
                        pltpu.VMEM((STATE, D_SSD), F32)],
        compiler_params=pltpu.CompilerParams(dimension_semantics=("arbitrary",),
                                             vmem_limit_bytes=VMEM_LIMIT),
        name="ssd",
    )(xbc3, z3, dt3, h0, cw, cb, dtb, alog, dsk, g, e64, e128)


def _outproj_kernel(x_ref, yssd_ref, scb_ref, v_ref, vp_ref, vn_ref, mod_ref, scw_ref, wo1_ref, wo2_ref,
                    g2_ref, wrt_ref, br_ref, x1_ref, lg_ref):
    tm = TOK_TILE
    per_b = SEQ // tm
    i = pl.program_id(0)
    first = (i % per_b) == 0
    last = (i % per_b) == per_b - 1
    m = mod_ref[0]
    v = v_ref[...].astype(F32)
    vp = jnp.where(first, 0.0, vp_ref[...].astype(F32))
    vn = jnp.where(last, 0.0, vn_ref[...].astype(F32))
    dn = jnp.concatenate([vp, v[:tm - GRID_W]], axis=0)
    up = jnp.concatenate([v[GRID_W:], vn], axis=0)
    scw = scw_ref[...]
    ysc = scb_ref[...].astype(F32) * (scw[0:1] * dn + scw[1:2] * v + scw[2:3] * up)
    out = _dot(yssd_ref[...], wo1_ref[...]) + _dot(ysc.astype(BF16), wo2_ref[...])
    x1 = x_ref[...] + m[2:3] * out
    x1_ref[...] = x1
    h2 = _rms(x1) * g2_ref[...] * (1.0 + m[4:5]) + m[3:4]
    lg_ref[...] = lax.dot_general(wrt_ref[...], h2, (((1,), (1,)), ((), ())), precision=HIGHEST,
                                  preferred_element_type=F32) + br_ref[...]


def _outproj(x2, yssd, scb, v, mod3, scw, wo1, wo2, g2, wrt, br):
    t = x2.shape[0]
    tm = TOK_TILE
    per_b = SEQ // tm
    r = tm // GRID_W
    nrow = t // GRID_W
    const = lambda *shape: pl.BlockSpec(shape, lambda i: (0,) * len(shape))
    tile = lambda w: pl.BlockSpec((tm, w), lambda i: (i, 0))
    return pl.pallas_call(
        _outproj_kernel,
        out_shape=(jax.ShapeDtypeStruct((t, D_MODEL), F32), jax.ShapeDtypeStruct((N_EXPERTS, t), F32)),
        grid=(t // tm,),
        in_specs=[tile(D_MODEL), tile(D_SSD), tile(D_SC), tile(D_SC),
                  pl.BlockSpec((GRID_W, D_SC), lambda i: (jnp.maximum(i * r - 1, 0), 0)),
                  pl.BlockSpec((GRID_W, D_SC), lambda i: (jnp.minimum((i + 1) * r, nrow - 1), 0)),
                  pl.BlockSpec((1, N_MOD, D_MODEL), lambda i: (i // per_b, 0, 0)),
                  const(3, D_SC), const(D_SSD, D_MODEL), const(D_SC, D_MODEL), const(1, D_MODEL),
                  const(N_EXPERTS, D_MODEL), const(N_EXPERTS, 1)],
        out_specs=(tile(D_MODEL), pl.BlockSpec((N_EXPERTS, tm), lambda i: (0, i))),
        compiler_params=pltpu.CompilerParams(dimension_semantics=("arbitrary",),
                                             vmem_limit_bytes=VMEM_LIMIT),
        name="outproj",
    )(x2, yssd, scb, v, v, v, mod3, scw, wo1, wo2, g2, wrt, br)


def _route_kernel(lg_ref, dest_ref, gate_ref, idx_ref, meta_ref, rank_ref, carry_ref, *, n_tok, n_blocks):
    tt = RT_TILE
    ne = N_EXPERTS
    eio = lax.broadcasted_iota(jnp.int32, (ne, tt), 0)
    si = lax.broadcasted_iota(jnp.int32, (tt, tt), 0)
    ti = lax.broadcasted_iota(jnp.int32, (tt, tt), 1)
    before = (si < ti).astype(BF16)
    carry_ref[...] = jnp.zeros_like(carry_ref)

    def tile_body(j, c):
        t0 = pl.multiple_of(j * tt, tt)
        l = lg_ref[:, pl.ds(t0, tt)]
        onehot = jnp.zeros((ne, tt), F32)
        tops, sels = [], []
        for _ in range(TOP_K):
            mx = jnp.max(l, axis=0, keepdims=True)
            idx = jnp.min(jnp.where(l == mx, eio, ne), axis=0, keepdims=True)
            sel = eio == idx
            l = jnp.where(sel, -jnp.inf, l)
            onehot = onehot + sel.astype(F32)
            tops.append(mx)
            sels.append(sel)
            idx_ref[pl.ds(len(tops) - 1, 1), pl.ds(t0, tt)] = idx
        ex = [jnp.exp(tv - tops[0]) for tv in tops]
        den = ex[0] + ex[1] + ex[2] + ex[3]
        prefix = _dot(onehot.astype(BF16), before) + carry_ref[:, 0:1]
        for k in range(TOP_K):
            gate_ref[pl.ds(k, 1), pl.ds(t0, tt)] = ex[k] / den
            rk = jnp.sum(jnp.where(sels[k], prefix, 0.0), axis=0, keepdims=True)
            rank_ref[pl.ds(k, 1), pl.ds(t0, tt)] = rk.astype(jnp.int32)
        carry_ref[...] = carry_ref[...] + jnp.sum(onehot, axis=1, keepdims=True)
        return c

    lax.fori_loop(0, n_tok // tt, tile_body, 0)

    counts = carry_ref[...]
    padded = jnp.floor((counts + (MOE_BM - 1)) * (1.0 / MOE_BM)) * MOE_BM
    er = lax.broadcasted_iota(jnp.int32, (ne, ne), 0)
    ec = lax.broadcasted_iota(jnp.int32, (ne, ne), 1)
    pad_start = jnp.dot((ec < er).astype(F32), padded, precision=HIGHEST, preferred_element_type=F32)
    pad_end = pad_start + padded

    def dest_body(j, c):
        t0 = pl.multiple_of(j * tt, tt)
        for k in range(TOP_K):
            idx = idx_ref[pl.ds(k, 1), pl.ds(t0, tt)]
            base = jnp.sum(jnp.where(eio == idx, pad_start[:, 0:1], 0.0), axis=0, keepdims=True)
            dest_ref[pl.ds(k, 1), pl.ds(t0, tt)] = base.astype(jnp.int32) + rank_ref[pl.ds(k, 1), pl.ds(t0, tt)]
        return c

    lax.fori_loop(0, n_tok // tt, dest_body, 0)

    width = meta_ref.shape[1]
    sub = lax.broadcasted_iota(jnp.int32, (ne, width), 0)
    lan = lax.broadcasted_iota(jnp.int32, (ne, width), 1)
    diag = sub == lan
    cnt_row = jnp.sum(jnp.where(diag, counts[:, 0:1], 0.0), axis=0, keepdims=True)
    start_row = jnp.sum(jnp.where(diag, pad_start[:, 0:1], 0.0), axis=0, keepdims=True)
    blk_start = (lan * MOE_BM).astype(F32)
    blk_exp = jnp.sum((pad_end[:, 0:1] <= blk_start).astype(F32), axis=0, keepdims=True)
    blk_exp = jnp.minimum(blk_exp, float(ne - 1))
    used = jnp.sum(padded[:, 0:1], axis=0, keepdims=True) * (1.0 / MOE_BM)
    meta_ref[0:1, :] = cnt_row.astype(jnp.int32)
    meta_ref[1:2, :] = start_row.astype(jnp.int32)
    meta_ref[2:3, :] = blk_exp.astype(jnp.int32)
    meta_ref[3:4, :] = jnp.broadcast_to(used, (1, width)).astype(jnp.int32)
    meta_ref[4:8, :] = jnp.zeros((4, width), jnp.int32)


def _route(lgt, n_blocks):
    ne, n_tok = lgt.shape
    width = -(-n_blocks // LANES) * LANES
    full = lambda *shape: pl.BlockSpec(shape, lambda: (0,) * len(shape))
    return pl.pallas_call(
        functools.partial(_route_kernel, n_tok=n_tok, n_blocks=n_blocks),
        out_shape=(jax.ShapeDtypeStruct((TOP_K, n_tok), jnp.int32),
                   jax.ShapeDtypeStruct((TOP_K, n_tok), F32),
                   jax.ShapeDtypeStruct((TOP_K, n_tok), jnp.int32),
                   jax.ShapeDtypeStruct((8, width), jnp.int32)),
        in_specs=[full(ne, n_tok)],
        out_specs=(full(TOP_K, n_tok), full(TOP_K, n_tok), full(TOP_K, n_tok), full(8, width)),
        scratch_shapes=[pltpu.VMEM((TOP_K, n_tok), jnp.int32), pltpu.VMEM((ne, LANES), F32)],
        compiler_params=pltpu.CompilerParams(vmem_limit_bytes=VMEM_LIMIT),
        name="route",
    )(lgt)


def _dispatch_kernel(dest_ref, cnt_ref, start_ref, nu_ref, x1_ref, meta_ref, mod_ref, g2_ref, zsrc_ref, xs_ref,
                     hbuf, sem, zsem):
    i = pl.program_id(0)
    n = pl.num_programs(0)
    tl = DISP_TILE
    slot = i % 2
    nb = xs_ref.shape[0] // MOE_BM

    def zero_block(b):
        return pltpu.make_async_copy(zsrc_ref, xs_ref.at[pl.ds(b * MOE_BM, MOE_BM)], zsem)

    @pl.when(i == 0)
    def _():
        def start_e(e, c):
            @pl.when(cnt_ref[e] > 0)
            def _():
                zero_block((start_ref[e] + cnt_ref[e] - 1) // MOE_BM).start()
            return c

        def wait_e(e, c):
            @pl.when(cnt_ref[e] > 0)
            def _():
                zero_block(0).wait()
            return c

        def start_t(b, c):
            zero_block(b).start()
            return c

        def wait_t(b, c):
            zero_block(0).wait()
            return c

        lax.fori_loop(0, N_EXPERTS, start_e, 0)
        lax.fori_loop(nu_ref[0], nb, start_t, 0)
        lax.fori_loop(0, N_EXPERTS, wait_e, 0)
        lax.fori_loop(nu_ref[0], nb, wait_t, 0)

    m = mod_ref[0]
    h2 = _rms(x1_ref[...]) * g2_ref[...] * (1.0 + m[4:5]) + m[3:4]
    lo = pltpu.bitcast(h2[:, :PACK_W].astype(BF16).astype(F32), jnp.uint32) >> 16
    hi = pltpu.bitcast(h2[:, PACK_W:].astype(BF16).astype(F32), jnp.uint32) & jnp.uint32(0xFFFF0000)
    row = jnp.concatenate([lo | hi, meta_ref[...], jnp.zeros((tl, D_MODEL - PACK_W - LANES), jnp.uint32)], axis=1)
    hbuf[slot] = row.reshape(tl, SUBLANES, LANES)

    def per_tok(t, c):
        tok = i * tl + t
        for k in range(TOP_K):
            d = dest_ref[tok * TOP_K + k]
            pltpu.make_async_copy(hbuf.at[slot, t], xs_ref.at[d], sem.at[slot]).start()
        return c

    lax.fori_loop(0, tl, per_tok, 0)

    def wait_slot(sl):
        for _ in range(TOP_K):
            pltpu.make_async_copy(hbuf.at[sl], xs_ref.at[pl.ds(0, tl)], sem.at[sl]).wait()

    @pl.when(i > 0)
    def _():
        wait_slot(1 - slot)

    @pl.when(i == n - 1)
    def _():
        wait_slot(slot)


def _dispatch(dest_flat, cnt, start, n_used, x1, meta_rows, mod3, g2, zsrc, n_rows):
    n_tok = x1.shape[0]
    tl = DISP_TILE
    per_b = SEQ // tl
    return pl.pallas_call(
        _dispatch_kernel,
        out_shape=jax.ShapeDtypeStruct((n_rows, SUBLANES, LANES), jnp.uint32),
        grid_spec=pltpu.PrefetchScalarGridSpec(
            num_scalar_prefetch=4,
            grid=(n_tok // tl,),
            in_specs=[pl.BlockSpec((tl, D_MODEL), lambda i, *_: (i, 0)),
                      pl.BlockSpec((tl, LANES), lambda i, *_: (i, 0)),
                      pl.BlockSpec((1, N_MOD, D_MODEL), lambda i, *_: (i // per_b, 0, 0)),
                      pl.BlockSpec((1, D_MODEL), lambda i, *_: (0, 0)),
                      pl.BlockSpec((MOE_BM, SUBLANES, LANES), lambda i, *_: (0, 0, 0))],
            out_specs=pl.BlockSpec(memory_space=pl.ANY),
            scratch_shapes=[pltpu.VMEM((2, tl, SUBLANES, LANES), jnp.uint32),
                            pltpu.SemaphoreType.DMA((2,)), pltpu.SemaphoreType.DMA]),
        compiler_params=pltpu.CompilerParams(dimension_semantics=("arbitrary",),
                                             vmem_limit_bytes=VMEM_LIMIT),
        name="dispatch",
    )(dest_flat, cnt, start, n_used, x1, meta_rows, mod3, g2, zsrc)


def _expert_kernel(be_ref, nu_ref, xs_ref, wgu_ref, bgu_ref, wd_ref, bd_ref, zero_ref, ytm_ref,
                   ybuf, stage, idv, ids, sc_sem, id_sem, z_sem, *, n_tok):
    j = pl.program_id(0)
    nbk = pl.num_programs(0) - 1
    nu = nu_ref[0]
    slot = j % 2
    prev = 1 - slot

    @pl.when(j == 0)
    def _():
        cp = pltpu.make_async_copy(zero_ref, ytm_ref.at[pl.ds(TOP_K * n_tok, MOE_BM)], z_sem)
        cp.start()
        cp.wait()

    n_pc = 4
    pw1 = D_FF // n_pc
    pw2 = D_MODEL // n_pc
    groups = [48] * n_pc + [16] * n_pc
    assert sum(groups) == MOE_BM

    def scatter_group(g):
        lo = sum(groups[:g])
        for r in range(lo, lo + groups[g]):
            pltpu.make_async_copy(ybuf.at[prev, r], ytm_ref.at[ids[prev, 0, r]],
                                  sc_sem.at[prev]).start(priority=r % 2)

    def scatter_wait(sl):
        pltpu.make_async_copy(ybuf.at[sl], ytm_ref.at[pl.ds(0, MOE_BM)], sc_sem.at[sl]).wait()

    def compute(with_scatter):
        e_f = be_ref[jnp.minimum(j, nbk - 1)].astype(F32)
        words = xs_ref[...].reshape(MOE_BM, D_MODEL)
        packed = words[:, 0:PACK_W]
        meta = pltpu.bitcast(words[:, PACK_W:PACK_W + LANES], F32)
        xb = jnp.concatenate(
            [pltpu.bitcast(packed << 16, F32).astype(BF16),
             pltpu.bitcast(packed & jnp.uint32(0xFFFF0000), F32).astype(BF16)], axis=1)
        gate = jnp.zeros((MOE_BM, 1), F32)
        kk = jnp.zeros((MOE_BM, 1), F32)
        for k in range(TOP_K):
            mk = meta[:, META_IDX + k:META_IDX + k + 1] == e_f
            gate = gate + jnp.where(mk, meta[:, META_GATE + k:META_GATE + k + 1], 0.0)
            kk = kk + jnp.where(mk, float(k), 0.0)
        row = kk * float(n_tok) + meta[:, META_TOK:META_TOK + 1]
        row_t = jnp.broadcast_to(row, (MOE_BM, LANES)).T
        idv[slot] = row_t[0:8].astype(jnp.int32)
        pltpu.make_async_copy(idv.at[slot], ids.at[slot], id_sem.at[slot]).start()
        for c in range(n_pc):
            if with_scatter:
                scatter_group(c)
            cg = slice(c * pw1, (c + 1) * pw1)
            cl = slice(D_FF + c * pw1, D_FF + (c + 1) * pw1)
            glu = jnp.minimum(_dot(xb, wgu_ref[0, :, cg]) + bgu_ref[0, :, cg], SWIGLU_LIMIT)
            lin = jnp.clip(_dot(xb, wgu_ref[0, :, cl]) + bgu_ref[0, :, cl], -SWIGLU_LIMIT, SWIGLU_LIMIT)
            stage[:, cg] = glu * jax.nn.sigmoid(SWIGLU_ALPHA * glu) * (lin + 1.0)
        act = stage[...].astype(BF16)
        for c in range(n_pc):
            if with_scatter:
                scatter_group(n_pc + c)
            cs = slice(c * pw2, (c + 1) * pw2)
            stage[:, cs] = (_dot(act, wd_ref[0, :, cs]) + bd_ref[0, :, cs]) * gate
        ybuf[slot] = stage[...].reshape(MOE_BM, D_MODEL // LANES, LANES)

    def ids_wait():
        pltpu.make_async_copy(idv.at[prev], ids.at[prev], id_sem.at[prev]).wait()

    @pl.when(jnp.logical_and(j >= 2, j < nu))
    def _():
        scatter_wait(slot)

    @pl.when(j == 0)
    def _():
        compute(False)

    @pl.when(jnp.logical_and(j >= 1, j < nu))
    def _():
        ids_wait()
        compute(True)

    @pl.when(j == nu)
    def _():
        ids_wait()
        for g in range(len(groups)):
            scatter_group(g)
        scatter_wait(prev)

    @pl.when(jnp.logical_and(j == nu, j >= 2))
    def _():
        scatter_wait(slot)


def _experts(blk_exp, n_used, xs, wgu, bgu, wd, bd, n_tok):
    n_rows = xs.shape[0]
    nb = n_rows // MOE_BM
    row_blk = lambda j, be, nu: (jnp.minimum(j, nu[0] - 1), 0, 0)
    per_e = lambda j, be, nu: (be[jnp.minimum(j, nb - 1)], 0, 0)
    sub = D_MODEL // LANES
    zero = jnp.zeros((MOE_BM, sub, LANES), F32)
    return pl.pallas_call(
        functools.partial(_expert_kernel, n_tok=n_tok),
        out_shape=jax.ShapeDtypeStruct((TOP_K * n_tok + MOE_BM, sub, LANES), F32),
        grid_spec=pltpu.PrefetchScalarGridSpec(
            num_scalar_prefetch=2,
            grid=(nb + 1,),
            in_specs=[pl.BlockSpec((MOE_BM, SUBLANES, LANES), row_blk),
                      pl.BlockSpec((1, D_MODEL, 2 * D_FF), per_e),
                      pl.BlockSpec((1, 1, 2 * D_FF), per_e),
                      pl.BlockSpec((1, D_FF, D_MODEL), per_e),
                      pl.BlockSpec((1, 1, D_MODEL), per_e),
                      pl.BlockSpec((MOE_BM, sub, LANES), lambda j, be, nu: (0, 0, 0))],
            out_specs=pl.BlockSpec(memory_space=pl.ANY),
            scratch_shapes=[pltpu.VMEM((2, MOE_BM, sub, LANES), F32), pltpu.VMEM((MOE_BM, D_MODEL), F32),
                            pltpu.VMEM((2, 8, MOE_BM), jnp.int32),
                            pltpu.SMEM((2, 8, MOE_BM), jnp.int32), pltpu.SemaphoreType.DMA((2,)),
                            pltpu.SemaphoreType.DMA((2,)), pltpu.SemaphoreType.DMA]),
        compiler_params=pltpu.CompilerParams(dimension_semantics=("arbitrary",),
                                             vmem_limit_bytes=VMEM_LIMIT),
        name="experts",
    )(blk_exp, n_used, xs, wgu, bgu, wd, bd, zero)


def _combine_kernel(y0_ref, y1_ref, y2_ref, y3_ref, x1_ref, mod_ref, fg_ref, o_ref):
    m = mod_ref[0]
    moe = ((y0_ref[...] + y1_ref[...]) + (y2_ref[...] + y3_ref[...])).reshape(COMB_TILE, D_MODEL)
    x2 = x1_ref[...] + m[5:6] * moe
    o_ref[...] = _rms(x2) * fg_ref[...]


def _combine(ytm, x1, mod3, fg):
    n_tok = x1.shape[0]
    tc = COMB_TILE
    per_b = SEQ // tc
    nt = n_tok // tc
    slot_spec = lambda k: pl.BlockSpec((tc, D_MODEL // LANES, LANES), lambda i: (i + k * nt, 0, 0))
    return pl.pallas_call(
        _combine_kernel,
        out_shape=jax.ShapeDtypeStruct((n_tok, D_MODEL), F32),
        grid=(nt,),
        in_specs=[slot_spec(0), slot_spec(1), slot_spec(2), slot_spec(3),
                  pl.BlockSpec((tc, D_MODEL), lambda i: (i, 0)),
                  pl.BlockSpec((1, N_MOD, D_MODEL), lambda i: (i // per_b, 0, 0)),
                  pl.BlockSpec((1, D_MODEL), lambda i: (0, 0))],
        out_specs=pl.BlockSpec((tc, D_MODEL), lambda i: (i, 0)),
        compiler_params=pltpu.CompilerParams(dimension_semantics=("arbitrary",),
                                             vmem_limit_bytes=VMEM_LIMIT),
        name="combine",
    )(ytm, ytm, ytm, ytm, x1, mod3, fg)


def _expansion_matrices():
    r = (jnp.arange(2 * LANES) % LANES)[:, None]
    out64, out128 = [], []
    for d in range(2):
        l64 = jnp.arange(D_SSD)[None, :]
        l128 = jnp.arange(HEADS * LANES)[None, :]
        out64.append((l64 // HEAD_DIM == r - HEADS * d).astype(BF16))
        out128.append((l128 // LANES == r - HEADS * d).astype(BF16))
    return jnp.stack(out64), jnp.stack(out128)


def _pad_lanes(v):
    return jnp.pad(v, [(0, 0)] * (v.ndim - 1) + [(0, LANES - v.shape[-1])])


def kernel(x, c, ctx, c_ctx, w_mod, b_mod, norm1_g, w_in, ssd_conv_w, ssd_conv_b, ssd_dt_bias, ssd_a_log,
           ssd_d, ssd_norm_g, sc_conv_w, w_out, norm2_g, w_router, b_router, w_gate_up, b_gate_up, w_down,
           b_down, final_g):
    bsz = x.shape[0]
    n_tok = bsz * SEQ
    n_assign = n_tok * TOP_K
    n_blocks = n_assign // MOE_BM + N_EXPERTS
    n_rows = n_blocks * MOE_BM
    li = 0

    cvec = jnp.concatenate([c, c_ctx[None, :], jnp.zeros((7, D_MODEL), F32)], axis=0)
    mod3 = _mod(cvec, w_mod[li], b_mod[li][None, :]).reshape(bsz + 8, N_MOD, D_MODEL)

    w = w_in[li]
    wz = w[:, Z0:X0].astype(BF16)
    wxbc = w[:, X0:DT0].astype(BF16)
    wdt = _pad_lanes(w[:, DT0:SC0]).astype(BF16)
    wb = w[:, SC0:SC0 + D_SC].astype(BF16)
    wc = w[:, SC0 + D_SC:SC0 + 2 * D_SC].astype(BF16)
    wu = w[:, SC0 + 2 * D_SC:].astype(BF16)
    g1 = norm1_g[li][None, :]
    cw = ssd_conv_w[li]
    cb = ssd_conv_b[li][None, :]
    dtb = _pad_lanes(ssd_dt_bias[li].reshape(1, 2 * HEADS))
    alog = _pad_lanes(ssd_a_log[li].reshape(1, 2 * HEADS))
    e64, e128 = _expansion_matrices()

    h0 = _ctx_states(ctx, mod3, g1, wxbc[:, :XB_W], wdt, cw[:, :XB_W], cb[:, :XB_W], dtb, alog, e64)

    x2 = x.reshape(n_tok, D_MODEL)
    z, xbc, dtr, scb, v = _inproj(x2, mod3, g1, wz, wxbc, wdt, wb, wc, wu)

    dsk = jnp.repeat(ssd_d[li], HEAD_DIM)[None, :]
    yssd = _ssd(xbc.reshape(bsz, SEQ, XBC_W), z.reshape(bsz, SEQ, D_SSD), dtr.reshape(bsz, SEQ, LANES), h0,
                cw, cb, dtb, alog, dsk, ssd_norm_g[li][None, :], e64, e128)

    wo = w_out[li].astype(BF16)
    g2 = norm2_g[li][None, :]
    x1, lgt = _outproj(x2, yssd.reshape(n_tok, D_SSD), scb, v, mod3, sc_conv_w[li], wo[:D_SSD], wo[D_SSD:],
                       g2, w_router[li].T, b_router[li][:, None])

    dest_t, gate_t, idx_t, meta = _route(lgt, n_blocks)
    dest_flat = dest_t.T.reshape(n_assign)
    cnt = meta[0, :N_EXPERTS]
    start = meta[1, :N_EXPERTS]
    blk_exp = meta[2, :n_blocks]
    n_used = meta[3, :1]

    meta_rows = lax.bitcast_convert_type(_pad_lanes(jnp.concatenate(
        [idx_t.T.astype(F32), gate_t.T, jnp.arange(n_tok, dtype=F32)[:, None]], axis=1)), jnp.uint32)
    pad_meta = lax.bitcast_convert_type(_pad_lanes(jnp.concatenate(
        [jnp.full((MOE_BM, TOP_K), -1.0, F32), jnp.zeros((MOE_BM, TOP_K), F32),
         (TOP_K * n_tok + jnp.arange(MOE_BM, dtype=F32))[:, None]], axis=1)), jnp.uint32)
    zsrc = jnp.concatenate([jnp.zeros((MOE_BM, PACK_W), jnp.uint32), pad_meta,
                            jnp.zeros((MOE_BM, D_MODEL - PACK_W - LANES), jnp.uint32)],
                           axis=1).reshape(MOE_BM, SUBLANES, LANES)

    xs = _dispatch(dest_flat, cnt, start, n_used, x1, meta_rows, mod3, g2, zsrc, n_rows)
    ytm = _experts(blk_exp, n_used, xs, w_gate_up[li].astype(BF16), b_gate_up[li][:, None, :],
                   w_down[li].astype(BF16), b_down[li][:, None, :], n_tok)
    out = _combine(ytm, x1, mod3, final_g[None, :])
    return out.reshape(bsz, SEQ, D_MODEL)
```

```python
import functools

import jax
import jax.numpy as jnp
from jax import lax
from jax.experimental import pallas as pl
from jax.experimental.pallas import tpu as pltpu

F32 = jnp.float32
BF16 = jnp.bfloat16
HIGHEST = lax.Precision.HIGHEST

D_MODEL = 1024
SEQ = 2048
CTX_LEN = 256
GRID_W = 64
D_SSD = 1024
D_SC = 1024
HEAD_DIM = 64
HEADS = 16
GROUPS = 2
STATE = 128
CHUNK = 128
N_EXPERTS = 32
TOP_K = 4
D_FF = 1024
SWIGLU_LIMIT = 7.0
SWIGLU_ALPHA = 1.702
NORM_EPS = 1e-6
N_MOD = 6
XBC_W = D_SSD + 2 * GROUPS * STATE
XB_W = D_SSD + GROUPS * STATE
LANES = 128

Z0 = 0
X0 = Z0 + D_SSD
B0 = X0 + D_SSD
C0 = B0 + GROUPS * STATE
DT0 = C0 + GROUPS * STATE
SC0 = DT0 + 2 * HEADS

TOK_TILE = 512
MOE_BM = 256
RT_TILE = 512
DISP_TILE = 256
COMB_TILE = 512
SUBLANES = 8
PACK_W = D_MODEL // 2
META_IDX = 0
META_GATE = TOP_K
META_TOK = 2 * TOP_K
VMEM_LIMIT = 56 * 1024 * 1024


def _silu(v):
    return v * jax.nn.sigmoid(v)


def _softplus(v):
    return jnp.maximum(v, 0.0) + jnp.log1p(jnp.exp(-jnp.abs(v)))


def _rms(v):
    return v * lax.rsqrt(jnp.mean(v * v, axis=-1, keepdims=True) + NORM_EPS)


def _dot(a, b):
    return jnp.dot(a, b, preferred_element_type=F32)


def _expand2(v, e2):
    hi = v.astype(BF16)
    lo = (v - hi.astype(F32)).astype(BF16)
    return _dot(jnp.concatenate([hi, lo], axis=1), e2)


def _mod_kernel(c_ref, w_ref, b_ref, o_ref):
    o_ref[...] = jnp.dot(_silu(c_ref[...]), w_ref[...], precision=HIGHEST,
                         preferred_element_type=F32) + b_ref[...]


def _mod(cvec, w_mod, b_mod):
    rows = cvec.shape[0]
    n = w_mod.shape[1]
    tn = 1536
    return pl.pallas_call(
        _mod_kernel,
        out_shape=jax.ShapeDtypeStruct((rows, n), F32),
        grid=(n // tn,),
        in_specs=[pl.BlockSpec((rows, D_MODEL), lambda j: (0, 0)),
                  pl.BlockSpec((D_MODEL, tn), lambda j: (0, j)),
                  pl.BlockSpec((1, tn), lambda j: (0, j))],
        out_specs=pl.BlockSpec((rows, tn), lambda j: (0, j)),
        compiler_params=pltpu.CompilerParams(dimension_semantics=("arbitrary",),
                                             vmem_limit_bytes=VMEM_LIMIT),
        name="mod",
    )(cvec, w_mod, b_mod)


def _ctx_kernel(ctx_ref, mod_ref, g1_ref, wxb_ref, wdt_ref, cw_ref, cb_ref, dtb_ref, alog_ref, e64_ref,
                h0_ref):
    L = CTX_LEN
    m = mod_ref[0]
    hc = _rms(ctx_ref[0]) * g1_ref[...] * (1.0 + m[1:2]) + m[0:1]
    hb = hc.astype(BF16)
    pxb = _dot(hb, wxb_ref[...])
    dtr = _dot(hb, wdt_ref[...])
    rowi = lax.broadcasted_iota(jnp.int32, (L, XB_W), 0)
    dn = jnp.where(rowi == 0, 0.0, pltpu.roll(pxb, 1, 0))
    up = jnp.where(rowi == L - 1, 0.0, pltpu.roll(pxb, L - 1, 0))
    cw = cw_ref[...]
    xb = _silu(cw[0:1] * dn + cw[1:2] * pxb + cw[2:3] * up + cb_ref[...])
    xs = xb[:, :D_SSD]
    bm = xb[:, D_SSD:].astype(BF16)
    dt = _softplus(dtr + dtb_ref[...])
    da = dt * (-jnp.exp(alog_ref[...]))
    ri = lax.broadcasted_iota(jnp.int32, (L, L), 0)
    ci = lax.broadcasted_iota(jnp.int32, (L, L), 1)
    for d in range(2):
        tri = (ci <= ri) if d == 0 else (ci >= ri)
        cum = jnp.dot(tri.astype(F32), da, precision=HIGHEST, preferred_element_type=F32)
        last = cum[L - 1:L] if d == 0 else cum[0:1]
        w_e = _expand2(jnp.exp(last - cum) * dt, e64_ref[d])
        xw = (xs * w_e).astype(BF16)
        for g in range(GROUPS):
            gw = D_SSD // GROUPS
            st = lax.dot_general(bm[:, g * STATE:(g + 1) * STATE], xw[:, g * gw:(g + 1) * gw],
                                 (((0,), (0,)), ((), ())), preferred_element_type=F32)
            h0_ref[0, d, :, g * gw:(g + 1) * gw] = st


def _ctx_states(ctx, mod3, g1, wxb, wdt, cw, cb, dtb, alog, e64):
    bsz = ctx.shape[0]
    mod_row = bsz
    const = lambda *shape: pl.BlockSpec(shape, lambda b: (0,) * len(shape))
    return pl.pallas_call(
        _ctx_kernel,
        out_shape=jax.ShapeDtypeStruct((bsz, 2, STATE, D_SSD), F32),
        grid=(bsz,),
        in_specs=[pl.BlockSpec((1, CTX_LEN, D_MODEL), lambda b: (b, 0, 0)),
                  pl.BlockSpec((1, N_MOD, D_MODEL), lambda b: (mod_row, 0, 0)),
                  const(1, D_MODEL), const(D_MODEL, XB_W), const(D_MODEL, LANES),
                  const(3, XB_W), const(1, XB_W), const(1, LANES), const(1, LANES),
                  const(2, 2 * LANES, D_SSD)],
        out_specs=pl.BlockSpec((1, 2, STATE, D_SSD), lambda b: (b, 0, 0, 0)),
        compiler_params=pltpu.CompilerParams(dimension_semantics=("arbitrary",),
                                             vmem_limit_bytes=VMEM_LIMIT),
        name="ctx_states",
    )(ctx, mod3, g1, wxb, wdt, cw, cb, dtb, alog, e64)


def _inproj_kernel(x_ref, mod_ref, g1_ref, wz_ref, wxbc_ref, wdt_ref, wb_ref, wc_ref, wu_ref,
                   z_ref, xbc_ref, dt_ref, scb_ref, v_ref):
    m = mod_ref[0]
    hx = _rms(x_ref[...]) * g1_ref[...] * (1.0 + m[1:2]) + m[0:1]
    hb = hx.astype(BF16)
    z_ref[...] = _dot(hb, wz_ref[...]).astype(BF16)
    xbc_ref[...] = _dot(hb, wxbc_ref[...]).astype(BF16)
    dt_ref[...] = _dot(hb, wdt_ref[...])
    scb_ref[...] = _dot(hb, wb_ref[...]).astype(BF16)
    v_ref[...] = (_dot(hb, wc_ref[...]) * _dot(hb, wu_ref[...])).astype(BF16)


def _inproj(x2, mod3, g1, wz, wxbc, wdt, wb, wc, wu):
    t = x2.shape[0]
    tm = TOK_TILE
    per_b = SEQ // tm
    const = lambda *shape: pl.BlockSpec(shape, lambda i: (0,) * len(shape))
    tile = lambda w: pl.BlockSpec((tm, w), lambda i: (i, 0))
    return pl.pallas_call(
        _inproj_kernel,
        out_shape=(jax.ShapeDtypeStruct((t, D_SSD), BF16), jax.ShapeDtypeStruct((t, XBC_W), BF16),
                   jax.ShapeDtypeStruct((t, LANES), F32), jax.ShapeDtypeStruct((t, D_SC), BF16),
                   jax.ShapeDtypeStruct((t, D_SC), BF16)),
        grid=(t // tm,),
        in_specs=[tile(D_MODEL),
                  pl.BlockSpec((1, N_MOD, D_MODEL), lambda i: (i // per_b, 0, 0)),
                  const(1, D_MODEL), const(D_MODEL, D_SSD), const(D_MODEL, XBC_W), const(D_MODEL, LANES),
                  const(D_MODEL, D_SC), const(D_MODEL, D_SC), const(D_MODEL, D_SC)],
        out_specs=(tile(D_SSD), tile(XBC_W), tile(LANES), tile(D_SC), tile(D_SC)),
        compiler_params=pltpu.CompilerParams(dimension_semantics=("arbitrary",),
                                             vmem_limit_bytes=VMEM_LIMIT),
        name="inproj",
    )(x2, mod3, g1, wz, wxbc, wdt, wb, wc, wu)


def _ssd_kernel(xbc_ref, z_ref, dt_ref, h0_ref, cw_ref, cb_ref, dtb_ref, alog_ref, dsk_ref, g_ref,
                e64_ref, e128_ref, o_ref, xc_ref, y_ref, s_ref):
    Q = CHUNK
    nck = SEQ // Q
    gw = D_SSD // GROUPS

    rowi = lax.broadcasted_iota(jnp.int32, (SUBLANES, XBC_W), 0)

    def conv_body(c, carry):
        r0 = pl.multiple_of(c * Q, Q)
        main = xbc_ref[0, pl.ds(r0, Q), :].astype(F32)
        pstart = pl.multiple_of(jnp.maximum(r0 - 16, 0), 16)
        nstart = pl.multiple_of(jnp.minimum(r0 + Q, SEQ - 16), 16)
        prev = xbc_ref[0, pl.ds(pstart, 16), :].astype(F32)[15:16]
        nxt = xbc_ref[0, pl.ds(nstart, 16), :].astype(F32)[0:1]
        prev = jnp.where(c > 0, prev, 0.0)
        nxt = jnp.where(c < nck - 1, nxt, 0.0)
        dn = pltpu.roll(main, 1, 0)
        up = pltpu.roll(main, Q - 1, 0)
        dn = jnp.concatenate([jnp.where(rowi == 0, prev, dn[0:SUBLANES]), dn[SUBLANES:]], axis=0)
        up = jnp.concatenate([up[:Q - SUBLANES], jnp.where(rowi == SUBLANES - 1, nxt, up[Q - SUBLANES:])], axis=0)
        cw = cw_ref[...]
        conv = cw[0:1] * dn + cw[1:2] * main + cw[2:3] * up + cb_ref[...]
        xc_ref[pl.ds(r0, Q), :] = _silu(conv).astype(BF16)
        return carry

    lax.fori_loop(0, nck, conv_body, 0)

    ri = lax.broadcasted_iota(jnp.int32, (Q, Q), 0)
    ci = lax.broadcasted_iota(jnp.int32, (Q, Q), 1)
    lane = lax.broadcasted_iota(jnp.int32, (Q, LANES), 1)
    a_neg = -jnp.exp(alog_ref[...])

    def chunk(c, d, first):
        r0 = pl.multiple_of(c * Q, Q)
        rows = pl.ds(r0, Q)
        xs_b = xc_ref[rows, 0:D_SSD]
        xs = xs_b.astype(F32)
        bm = xc_ref[rows, D_SSD:D_SSD + GROUPS * STATE]
        cm = xc_ref[rows, D_SSD + GROUPS * STATE:XBC_W]
        dt = _softplus(dt_ref[0, rows, :] + dtb_ref[...])
        da = dt * a_neg
        tri = (ci <= ri) if d == 0 else (ci >= ri)
        p0 = da.astype(BF16)
        r1 = da - p0.astype(F32)
        p1 = r1.astype(BF16)
        p2 = (r1 - p1.astype(F32)).astype(BF16)
        tri_b = jnp.where(tri, 1.0, 0.0).astype(BF16)
        cum = _dot(jnp.concatenate([tri_b, tri_b, tri_b], axis=1),
                   jnp.concatenate([p0, p1, p2], axis=0))
        sub_t = (cum - jnp.log(dt)).T
        last = cum[Q - 1:Q] if d == 0 else cum[0:1]
        ecum_e = _expand2(jnp.exp(cum), e64_ref[d])
        w_e = _expand2(jnp.exp(last - cum) * dt, e64_ref[d])
        colb = _expand2(cum, e128_ref[d])
        decay_e = ecum_e[Q - 1:Q] if d == 0 else ecum_e[0:1]

        gmat = [lax.dot_general(cm[:, g * STATE:(g + 1) * STATE], bm[:, g * STATE:(g + 1) * STATE],
                                (((1,), (1,)), ((), ())), preferred_element_type=F32)
                for g in range(GROUPS)]
        zero_b = jnp.zeros((Q, LANES), BF16)
        y_parts = []
        for p in range(HEADS // 2):
            g = (2 * p) // (HEADS // GROUPS)
            ms = []
            for hh in (2 * p, 2 * p + 1):
                seg = colb[:, hh * LANES:(hh + 1) * LANES] - sub_t[HEADS * d + hh:HEADS * d + hh + 1, :]
                ms.append((jnp.where(tri, jnp.exp(seg), 0.0) * gmat[g]).astype(BF16))
            mcat = jnp.concatenate(ms, axis=1)
            xp = xs_b[:, p * LANES:(p + 1) * LANES]
            rhs = jnp.concatenate([jnp.where(lane < HEAD_DIM, xp, zero_b),
                                   jnp.where(lane >= HEAD_DIM, xp, zero_b)], axis=0)
            y_parts.append(_dot(mcat, rhs))
        y_diag = jnp.concatenate(y_parts, axis=1)

        s_old = s_ref[d]
        s_bf = s_old.astype(BF16)
        y_off = jnp.concatenate(
            [_dot(cm[:, g * STATE:(g + 1) * STATE], s_bf[:, g * gw:(g + 1) * gw]) for g in range(GROUPS)],
            axis=1)
        y = y_diag + y_off * ecum_e

        xw = (xs * w_e).astype(BF16)
        upd = jnp.concatenate(
            [lax.dot_general(bm[:, g * STATE:(g + 1) * STATE], xw[:, g * gw:(g + 1) * gw],
                             (((0,), (0,)), ((), ())), preferred_element_type=F32) for g in range(GROUPS)],
            axis=1)
        s_ref[d] = s_old * decay_e + upd

        if first:
            y_ref[rows, :] = y + dsk_ref[...] * xs
        else:
            tot = y_ref[rows, :] + y
            zz = z_ref[0, rows, :].astype(F32)
            gz = tot * _silu(zz)
            outs = []
            for g in range(GROUPS):
                gg = gz[:, g * gw:(g + 1) * gw]
                outs.append(gg * lax.rsqrt(jnp.mean(gg * gg, axis=-1, keepdims=True) + NORM_EPS))
            o_ref[0, rows, :] = (jnp.concatenate(outs, axis=1) * g_ref[...]).astype(BF16)

    s_ref[...] = h0_ref[0]

    def first_half(i, carry):
        chunk(i, 0, True)
        chunk(nck - 1 - i, 1, True)
        return carry

    def second_half(i, carry):
        chunk(i, 0, False)
        chunk(nck - 1 - i, 1, False)
        return carry

    lax.fori_loop(0, nck // 2, first_half, 0)
    lax.fori_loop(nck // 2, nck, second_half, 0)


def _ssd(xbc3, z3, dt3, h0, cw, cb, dtb, alog, dsk, g, e64, e128):
    bsz = xbc3.shape[0]
    const = lambda *shape: pl.BlockSpec(shape, lambda b: (0,) * len(shape))
    seq = lambda w: pl.BlockSpec((1, SEQ, w), lambda b: (b, 0, 0))
    return pl.pallas_call(
        _ssd_kernel,
        out_shape=jax.ShapeDtypeStruct((bsz, SEQ, D_SSD), BF16),
        grid=(bsz,),
        in_specs=[seq(XBC_W), seq(D_SSD), seq(LANES),
                  pl.BlockSpec((1, 2, STATE, D_SSD), lambda b: (b, 0, 0, 0)),
                  const(3, XBC_W), const(1, XBC_W), const(1, LANES), const(1, LANES),
                  const(1, D_SSD), const(1, D_SSD), const(2, 2 * LANES, D_SSD),
                  const(2, 2 * LANES, HEADS * LANES)],
        out_specs=seq(D_SSD),
        scratch_shapes=[pltpu.VMEM((SEQ, XBC_W), BF16), pltpu.VMEM((SEQ, D_SSD), F32),
                        pltpu.VMEM((2, STATE, D_SSD), F32)],
        compiler_params=pltpu.CompilerParams(dimension_semantics=("arbitrary",),
                                             vmem_limit_bytes=VMEM_LIMIT),
        name="ssd",
    )(xbc3, z3, dt3, h0, cw, cb, dtb, alog, dsk, g, e64, e128)


def _outproj_kernel(x_ref, yssd_ref, scb_ref, v_ref, vp_ref, vn_ref, mod_ref, scw_ref, wo1_ref, wo2_ref,
                    g2_ref, wrt_ref, br_ref, x1_ref, lg_ref):
    tm = TOK_TILE
    per_b = SEQ // tm
    i = pl.program_id(0)
    first = (i % per_b) == 0
    last = (i % per_b) == per_b - 1
    m = mod_ref[0]
    v = v_ref[...].astype(F32)
    vp = jnp.where(first, 0.0, vp_ref[...].astype(F32))
    vn = jnp.where(last, 0.0, vn_ref[...].astype(F32))
    dn = jnp.concatenate([vp, v[:tm - GRID_W]], axis=0)
    up = jnp.concatenate([v[GRID_W:], vn], axis=0)
    scw = scw_ref[...]
    ysc = scb_ref[...].astype(F32) * (scw[0:1] * dn + scw[1:2] * v + scw[2:3] * up)
    out = _dot(yssd_ref[...], wo1_ref[...]) + _dot(ysc.astype(BF16), wo2_ref[...])
    x1 = x_ref[...] + m[2:3] * out
    x1_ref[...] = x1
    h2 = _rms(x1) * g2_ref[...] * (1.0 + m[4:5]) + m[3:4]
    lg_ref[...] = lax.dot_general(wrt_ref[...], h2, (((1,), (1,)), ((), ())), precision=HIGHEST,
                                  preferred_element_type=F32) + br_ref[...]


def _outproj(x2, yssd, scb, v, mod3, scw, wo1, wo2, g2, wrt, br):
    t = x2.shape[0]
    tm = TOK_TILE
    per_b = SEQ // tm
    r = tm // GRID_W
    nrow = t // GRID_W
    const = lambda *shape: pl.BlockSpec(shape, lambda i: (0,) * len(shape))
    tile = lambda w: pl.BlockSpec((tm, w), lambda i: (i, 0))
    return pl.pallas_call(
        _outproj_kernel,
        out_shape=(jax.ShapeDtypeStruct((t, D_MODEL), F32), jax.ShapeDtypeStruct((N_EXPERTS, t), F32)),
        grid=(t // tm,),
        in_specs=[tile(D_MODEL), tile(D_SSD), tile(D_SC), tile(D_SC),
                  pl.BlockSpec((GRID_W, D_SC), lambda i: (jnp.maximum(i * r - 1, 0), 0)),
                  pl.BlockSpec((GRID_W, D_SC), lambda i: (jnp.minimum((i + 1) * r, nrow - 1), 0)),
                  pl.BlockSpec((1, N_MOD, D_MODEL), lambda i: (i // per_b, 0, 0)),
                  const(3, D_SC), const(D_SSD, D_MODEL), const(D_SC, D_MODEL), const(1, D_MODEL),
                  const(N_EXPERTS, D_MODEL), const(N_EXPERTS, 1)],
        out_specs=(tile(D_MODEL), pl.BlockSpec((N_EXPERTS, tm), lambda i: (0, i))),
        compiler_params=pltpu.CompilerParams(dimension_semantics=("arbitrary",),
                                             vmem_limit_bytes=VMEM_LIMIT),
        name="outproj",
    )(x2, yssd, scb, v, v, v, mod3, scw, wo1, wo2, g2, wrt, br)


def _route_kernel(lg_ref, dest_ref, gate_ref, idx_ref, meta_ref, rank_ref, carry_ref, *, n_tok, n_blocks):
    tt = RT_TILE
    ne = N_EXPERTS
    eio = lax.broadcasted_iota(jnp.int32, (ne, tt), 0)
    si = lax.broadcasted_iota(jnp.int32, (tt, tt), 0)
    ti = lax.broadcasted_iota(jnp.int32, (tt, tt), 1)
    before = (si < ti).astype(BF16)
    carry_ref[...] = jnp.zeros_like(carry_ref)

    def tile_body(j, c):
        t0 = pl.multiple_of(j * tt, tt)
        l = lg_ref[:, pl.ds(t0, tt)]
        onehot = jnp.zeros((ne, tt), F32)
        tops, sels = [], []
        for _ in range(TOP_K):
            mx = jnp.max(l, axis=0, keepdims=True)
            idx = jnp.min(jnp.where(l == mx, eio, ne), axis=0, keepdims=True)
            sel = eio == idx
            l = jnp.where(sel, -jnp.inf, l)
            onehot = onehot + sel.astype(F32)
            tops.append(mx)
            sels.append(sel)
            idx_ref[pl.ds(len(tops) - 1, 1), pl.ds(t0, tt)] = idx
        ex = [jnp.exp(tv - tops[0]) for tv in tops]
        den = ex[0] + ex[1] + ex[2] + ex[3]
        prefix = _dot(onehot.astype(BF16), before) + carry_ref[:, 0:1]
        for k in range(TOP_K):
            gate_ref[pl.ds(k, 1), pl.ds(t0, tt)] = ex[k] / den
            rk = jnp.sum(jnp.where(sels[k], prefix, 0.0), axis=0, keepdims=True)
            rank_ref[pl.ds(k, 1), pl.ds(t0, tt)] = rk.astype(jnp.int32)
        carry_ref[...] = carry_ref[...] + jnp.sum(onehot, axis=1, keepdims=True)
        return c

    lax.fori_loop(0, n_tok // tt, tile_body, 0)

    counts = carry_ref[...]
    padded = jnp.floor((counts + (MOE_BM - 1)) * (1.0 / MOE_BM)) * MOE_BM
    er = lax.broadcasted_iota(jnp.int32, (ne, ne), 0)
    ec = lax.broadcasted_iota(jnp.int32, (ne, ne), 1)
    pad_start = jnp.dot((ec < er).astype(F32), padded, precision=HIGHEST, preferred_element_type=F32)
    pad_end = pad_start + padded

    def dest_body(j, c):
        t0 = pl.multiple_of(j * tt, tt)
        for k in range(TOP_K):
            idx = idx_ref[pl.ds(k, 1), pl.ds(t0, tt)]
            base = jnp.sum(jnp.where(eio == idx, pad_start[:, 0:1], 0.0), axis=0, keepdims=True)
            dest_ref[pl.ds(k, 1), pl.ds(t0, tt)] = base.astype(jnp.int32) + rank_ref[pl.ds(k, 1), pl.ds(t0, tt)]
        return c

    lax.fori_loop(0, n_tok // tt, dest_body, 0)

    width = meta_ref.shape[1]
    sub = lax.broadcasted_iota(jnp.int32, (ne, width), 0)
    lan = lax.broadcasted_iota(jnp.int32, (ne, width), 1)
    diag = sub == lan
    cnt_row = jnp.sum(jnp.where(diag, counts[:, 0:1], 0.0), axis=0, keepdims=True)
    start_row = jnp.sum(jnp.where(diag, pad_start[:, 0:1], 0.0), axis=0, keepdims=True)
    blk_start = (lan * MOE_BM).astype(F32)
    blk_exp = jnp.sum((pad_end[:, 0:1] <= blk_start).astype(F32), axis=0, keepdims=True)
    blk_exp = jnp.minimum(blk_exp, float(ne - 1))
    used = jnp.sum(padded[:, 0:1], axis=0, keepdims=True) * (1.0 / MOE_BM)
    meta_ref[0:1, :] = cnt_row.astype(jnp.int32)
    meta_ref[1:2, :] = start_row.astype(jnp.int32)
    meta_ref[2:3, :] = blk_exp.astype(jnp.int32)
    meta_ref[3:4, :] = jnp.broadcast_to(used, (1, width)).astype(jnp.int32)
    meta_ref[4:8, :] = jnp.zeros((4, width), jnp.int32)


def _route(lgt, n_blocks):
    ne, n_tok = lgt.shape
    width = -(-n_blocks // LANES) * LANES
    full = lambda *shape: pl.BlockSpec(shape, lambda: (0,) * len(shape))
    return pl.pallas_call(
        functools.partial(_route_kernel, n_tok=n_tok, n_blocks=n_blocks),
        out_shape=(jax.ShapeDtypeStruct((TOP_K, n_tok), jnp.int32),
                   jax.ShapeDtypeStruct((TOP_K, n_tok), F32),
                   jax.ShapeDtypeStruct((TOP_K, n_tok), jnp.int32),
                   jax.ShapeDtypeStruct((8, width), jnp.int32)),
        in_specs=[full(ne, n_tok)],
        out_specs=(full(TOP_K, n_tok), full(TOP_K, n_tok), full(TOP_K, n_tok), full(8, width)),
        scratch_shapes=[pltpu.VMEM((TOP_K, n_tok), jnp.int32), pltpu.VMEM((ne, LANES), F32)],
        compiler_params=pltpu.CompilerParams(vmem_limit_bytes=VMEM_LIMIT),
        name="route",
    )(lgt)


def _dispatch_kernel(dest_ref, cnt_ref, start_ref, nu_ref, x1_ref, meta_ref, mod_ref, g2_ref, zsrc_ref, xs_ref,
                     hbuf, sem, zsem):
    i = pl.program_id(0)
    n = pl.num_programs(0)
    tl = DISP_TILE
    slot = i % 2
    nb = xs_ref.shape[0] // MOE_BM

    def zero_block(b):
        return pltpu.make_async_copy(zsrc_ref, xs_ref.at[pl.ds(b * MOE_BM, MOE_BM)], zsem)

    @pl.when(i == 0)
    def _():
        def start_e(e, c):
            @pl.when(cnt_ref[e] > 0)
            def _():
                zero_block((start_ref[e] + cnt_ref[e] - 1) // MOE_BM).start()
            return c

        def wait_e(e, c):
            @pl.when(cnt_ref[e] > 0)
            def _():
                zero_block(0).wait()
            return c

        def start_t(b, c):
            zero_block(b).start()
            return c

        def wait_t(b, c):
            zero_block(0).wait()
            return c

        lax.fori_loop(0, N_EXPERTS, start_e, 0)
        lax.fori_loop(nu_ref[0], nb, start_t, 0)
        lax.fori_loop(0, N_EXPERTS, wait_e, 0)
        lax.fori_loop(nu_ref[0], nb, wait_t, 0)

    m = mod_ref[0]
    h2 = _rms(x1_ref[...]) * g2_ref[...] * (1.0 + m[4:5]) + m[3:4]
    lo = pltpu.bitcast(h2[:, :PACK_W].astype(BF16).astype(F32), jnp.uint32) >> 16
    hi = pltpu.bitcast(h2[:, PACK_W:].astype(BF16).astype(F32), jnp.uint32) & jnp.uint32(0xFFFF0000)
    row = jnp.concatenate([lo | hi, meta_ref[...], jnp.zeros((tl, D_MODEL - PACK_W - LANES), jnp.uint32)], axis=1)
    hbuf[slot] = row.reshape(tl, SUBLANES, LANES)

    def per_tok(t, c):
        tok = i * tl + t
        for k in range(TOP_K):
            d = dest_ref[tok * TOP_K + k]
            pltpu.make_async_copy(hbuf.at[slot, t], xs_ref.at[d], sem.at[slot]).start()
        return c

    lax.fori_loop(0, tl, per_tok, 0)

    def wait_slot(sl):
        for _ in range(TOP_K):
            pltpu.make_async_copy(hbuf.at[sl], xs_ref.at[pl.ds(0, tl)], sem.at[sl]).wait()

    @pl.when(i > 0)
    def _():
        wait_slot(1 - slot)

    @pl.when(i == n - 1)
    def _():
        wait_slot(slot)


def _dispatch(dest_flat, cnt, start, n_used, x1, meta_rows, mod3, g2, zsrc, n_rows):
    n_tok = x1.shape[0]
    tl = DISP_TILE
    per_b = SEQ // tl
    return pl.pallas_call(
        _dispatch_kernel,
        out_shape=jax.ShapeDtypeStruct((n_rows, SUBLANES, LANES), jnp.uint32),
        grid_spec=pltpu.PrefetchScalarGridSpec(
            num_scalar_prefetch=4,
            grid=(n_tok // tl,),
            in_specs=[pl.BlockSpec((tl, D_MODEL), lambda i, *_: (i, 0)),
                      pl.BlockSpec((tl, LANES), lambda i, *_: (i, 0)),
                      pl.BlockSpec((1, N_MOD, D_MODEL), lambda i, *_: (i // per_b, 0, 0)),
                      pl.BlockSpec((1, D_MODEL), lambda i, *_: (0, 0)),
                      pl.BlockSpec((MOE_BM, SUBLANES, LANES), lambda i, *_: (0, 0, 0))],
            out_specs=pl.BlockSpec(memory_space=pl.ANY),
            scratch_shapes=[pltpu.VMEM((2, tl, SUBLANES, LANES), jnp.uint32),
                            pltpu.SemaphoreType.DMA((2,)), pltpu.SemaphoreType.DMA]),
        compiler_params=pltpu.CompilerParams(dimension_semantics=("arbitrary",),
                                             vmem_limit_bytes=VMEM_LIMIT),
        name="dispatch",
    )(dest_flat, cnt, start, n_used, x1, meta_rows, mod3, g2, zsrc)


def _expert_kernel(be_ref, nu_ref, xs_ref, wgu_ref, bgu_ref, wd_ref, bd_ref, zero_ref, ytm_ref,
                   ybuf, stage, idv, ids, sc_sem, id_sem, z_sem, *, n_tok):
    j = pl.program_id(0)
    nbk = pl.num_programs(0) - 1
    nu = nu_ref[0]
    slot = j % 2
    prev = 1 - slot

    @pl.when(j == 0)
    def _():
        cp = pltpu.make_async_copy(zero_ref, ytm_ref.at[pl.ds(TOP_K * n_tok, MOE_BM)], z_sem)
        cp.start()
        cp.wait()

    n_pc = 4
    pw1 = D_FF // n_pc
    pw2 = D_MODEL // n_pc
    groups = [48] * n_pc + [16] * n_pc
    assert sum(groups) == MOE_BM

    def scatter_group(g):
        lo = sum(groups[:g])
        for r in range(lo, lo + groups[g]):
            pltpu.make_async_copy(ybuf.at[prev, r], ytm_ref.at[ids[prev, 0, r]],
                                  sc_sem.at[prev]).start(priority=r % 2)

    def scatter_wait(sl):
        pltpu.make_async_copy(ybuf.at[sl], ytm_ref.at[pl.ds(0, MOE_BM)], sc_sem.at[sl]).wait()

    def compute(with_scatter):
        e_f = be_ref[jnp.minimum(j, nbk - 1)].astype(F32)
        words = xs_ref[...].reshape(MOE_BM, D_MODEL)
        packed = words[:, 0:PACK_W]
        meta = pltpu.bitcast(words[:, PACK_W:PACK_W + LANES], F32)
        xb = jnp.concatenate(
            [pltpu.bitcast(packed << 16, F32).astype(BF16),
             pltpu.bitcast(packed & jnp.uint32(0xFFFF0000), F32).astype(BF16)], axis=1)
        gate = jnp.zeros((MOE_BM, 1), F32)
        kk = jnp.zeros((MOE_BM, 1), F32)
        for k in range(TOP_K):
            mk = meta[:, META_IDX + k:META_IDX + k + 1] == e_f
            gate = gate + jnp.where(mk, meta[:, META_GATE + k:META_GATE + k + 1], 0.0)
            kk = kk + jnp.where(mk, float(k), 0.0)
        row = kk * float(n_tok) + meta[:, META_TOK:META_TOK + 1]
        row_t = jnp.broadcast_to(row, (MOE_BM, LANES)).T
        idv[slot] = row_t[0:8].astype(jnp.int32)
        pltpu.make_async_copy(idv.at[slot], ids.at[slot], id_sem.at[slot]).start()
        for c in range(n_pc):
            if with_scatter:
                scatter_group(c)
            cg = slice(c * pw1, (c + 1) * pw1)
            cl = slice(D_FF + c * pw1, D_FF + (c + 1) * pw1)
            glu = jnp.minimum(_dot(xb, wgu_ref[0, :, cg]) + bgu_ref[0, :, cg], SWIGLU_LIMIT)
            lin = jnp.clip(_dot(xb, wgu_ref[0, :, cl]) + bgu_ref[0, :, cl], -SWIGLU_LIMIT, SWIGLU_LIMIT)
            stage[:, cg] = glu * jax.nn.sigmoid(SWIGLU_ALPHA * glu) * (lin + 1.0)
        act = stage[...].astype(BF16)
        for c in range(n_pc):
            if with_scatter:
                scatter_group(n_pc + c)
            cs = slice(c * pw2, (c + 1) * pw2)
            stage[:, cs] = (_dot(act, wd_ref[0, :, cs]) + bd_ref[0, :, cs]) * gate
        ybuf[slot] = stage[...].reshape(MOE_BM, D_MODEL // LANES, LANES)

    def ids_wait():
        pltpu.make_async_copy(idv.at[prev], ids.at[prev], id_sem.at[prev]).wait()

    @pl.when(jnp.logical_and(j >= 2, j < nu))
    def _():
        scatter_wait(slot)

    @pl.when(j == 0)
    def _():
        compute(False)

    @pl.when(jnp.logical_and(j >= 1, j < nu))
    def _():
        ids_wait()
        compute(True)

    @pl.when(j == nu)
    def _():
        ids_wait()
        for g in range(len(groups)):
            scatter_group(g)
        scatter_wait(prev)

    @pl.when(jnp.logical_and(j == nu, j >= 2))
    def _():
        scatter_wait(slot)


def _experts(blk_exp, n_used, xs, wgu, bgu, wd, bd, n_tok):
    n_rows = xs.shape[0]
    nb = n_rows // MOE_BM
    row_blk = lambda j, be, nu: (jnp.minimum(j, nu[0] - 1), 0, 0)
    per_e = lambda j, be, nu: (be[jnp.minimum(j, nb - 1)], 0, 0)
    sub = D_MODEL // LANES
    zero = jnp.zeros((MOE_BM, sub, LANES), F32)
    return pl.pallas_call(
        functools.partial(_expert_kernel, n_tok=n_tok),
        out_shape=jax.ShapeDtypeStruct((TOP_K * n_tok + MOE_BM, sub, LANES), F32),
        grid_spec=pltpu.PrefetchScalarGridSpec(
            num_scalar_prefetch=2,
            grid=(nb + 1,),
            in_specs=[pl.BlockSpec((MOE_BM, SUBLANES, LANES), row_blk),
                      pl.BlockSpec((1, D_MODEL, 2 * D_FF), per_e),
                      pl.BlockSpec((1, 1, 2 * D_FF), per_e),
                      pl.BlockSpec((1, D_FF, D_MODEL), per_e),
                      pl.BlockSpec((1, 1, D_MODEL), per_e),
                      pl.BlockSpec((MOE_BM, sub, LANES), lambda j, be, nu: (0, 0, 0))],
            out_specs=pl.BlockSpec(memory_space=pl.ANY),
            scratch_shapes=[pltpu.VMEM((2, MOE_BM, sub, LANES), F32), pltpu.VMEM((MOE_BM, D_MODEL), F32),
                            pltpu.VMEM((2, 8, MOE_BM), jnp.int32),
                            pltpu.SMEM((2, 8, MOE_BM), jnp.int32), pltpu.SemaphoreType.DMA((2,)),
                            pltpu.SemaphoreType.DMA((2,)), pltpu.SemaphoreType.DMA]),
        compiler_params=pltpu.CompilerParams(dimension_semantics=("arbitrary",),
                                             vmem_limit_bytes=VMEM_LIMIT),
        name="experts",
    )(blk_exp, n_used, xs, wgu, bgu, wd, bd, zero)


def _combine_kernel(y0_ref, y1_ref, y2_ref, y3_ref, x1_ref, mod_ref, fg_ref, o_ref):
    m = mod_ref[0]
    moe = ((y0_ref[...] + y1_ref[...]) + (y2_ref[...] + y3_ref[...])).reshape(COMB_TILE, D_MODEL)
    x2 = x1_ref[...] + m[5:6] * moe
    o_ref[...] = _rms(x2) * fg_ref[...]


def _combine(ytm, x1, mod3, fg):
    n_tok = x1.shape[0]
    tc = COMB_TILE
    per_b = SEQ // tc
    nt = n_tok // tc
    slot_spec = lambda k: pl.BlockSpec((tc, D_MODEL // LANES, LANES), lambda i: (i + k * nt, 0, 0))
    return pl.pallas_call(
        _combine_kernel,
        out_shape=jax.ShapeDtypeStruct((n_tok, D_MODEL), F32),
        grid=(nt,),
        in_specs=[slot_spec(0), slot_spec(1), slot_spec(2), slot_spec(3),
                  pl.BlockSpec((tc, D_MODEL), lambda i: (i, 0)),
                  pl.BlockSpec((1, N_MOD, D_MODEL), lambda i: (i // per_b, 0, 0)),
                  pl.BlockSpec((1, D_MODEL), lambda i: (0, 0))],
        out_specs=pl.BlockSpec((tc, D_MODEL), lambda i: (i, 0)),
        compiler_params=pltpu.CompilerParams(dimension_semantics=("arbitrary",),
                                             vmem_limit_bytes=VMEM_LIMIT),
        name="combine",
    )(ytm, ytm, ytm, ytm, x1, mod3, fg)


def _expansion_matrices():
    r = (jnp.arange(2 * LANES) % LANES)[:, None]
    out64, out128 = [], []
    for d in range(2):
        l64 = jnp.arange(D_SSD)[None, :]
        l128 = jnp.arange(HEADS * LANES)[None, :]
        out64.append((l64 // HEAD_DIM == r - HEADS * d).astype(BF16))
        out128.append((l128 // LANES == r - HEADS * d).astype(BF16))
    return jnp.stack(out64), jnp.stack(out128)


def _pad_lanes(v):
    return jnp.pad(v, [(0, 0)] * (v.ndim - 1) + [(0, LANES - v.shape[-1])])


def kernel(x, c, ctx, c_ctx, w_mod, b_mod, norm1_g, w_in, ssd_conv_w, ssd_conv_b, ssd_dt_bias, ssd_a_log,
           ssd_d, ssd_norm_g, sc_conv_w, w_out, norm2_g, w_router, b_router, w_gate_up, b_gate_up, w_down,
           b_down, final_g):
    bsz = x.shape[0]
    n_tok = bsz * SEQ
    n_assign = n_tok * TOP_K
    n_blocks = n_assign // MOE_BM + N_EXPERTS
    n_rows = n_blocks * MOE_BM
    li = 0

    cvec = jnp.concatenate([c, c_ctx[None, :], jnp.zeros((7, D_MODEL), F32)], axis=0)
    mod3 = _mod(cvec, w_mod[li], b_mod[li][None, :]).reshape(bsz + 8, N_MOD, D_MODEL)

    w = w_in[li]
    wz = w[:, Z0:X0].astype(BF16)
    wxbc = w[:, X0:DT0].astype(BF16)
    wdt = _pad_lanes(w[:, DT0:SC0]).astype(BF16)
    wb = w[:, SC0:SC0 + D_SC].astype(BF16)
    wc = w[:, SC0 + D_SC:SC0 + 2 * D_SC].astype(BF16)
    wu = w[:, SC0 + 2 * D_SC:].astype(BF16)
    g1 = norm1_g[li][None, :]
    cw = ssd_conv_w[li]
    cb = ssd_conv_b[li][None, :]
    dtb = _pad_lanes(ssd_dt_bias[li].reshape(1, 2 * HEADS))
    alog = _pad_lanes(ssd_a_log[li].reshape(1, 2 * HEADS))
    e64, e128 = _expansion_matrices()

    h0 = _ctx_states(ctx, mod3, g1, wxbc[:, :XB_W], wdt, cw[:, :XB_W], cb[:, :XB_W], dtb, alog, e64)

    x2 = x.reshape(n_tok, D_MODEL)
    z, xbc, dtr, scb, v = _inproj(x2, mod3, g1, wz, wxbc, wdt, wb, wc, wu)

    dsk = jnp.repeat(ssd_d[li], HEAD_DIM)[None, :]
    yssd = _ssd(xbc.reshape(bsz, SEQ, XBC_W), z.reshape(bsz, SEQ, D_SSD), dtr.reshape(bsz, SEQ, LANES), h0,
                cw, cb, dtb, alog, dsk, ssd_norm_g[li][None, :], e64, e128)

    wo = w_out[li].astype(BF16)
    g2 = norm2_g[li][None, :]
    x1, lgt = _outproj(x2, yssd.reshape(n_tok, D_SSD), scb, v, mod3, sc_conv_w[li], wo[:D_SSD], wo[D_SSD:],
                       g2, w_router[li].T, b_router[li][:, None])

    dest_t, gate_t, idx_t, meta = _route(lgt, n_blocks)
    dest_flat = dest_t.T.reshape(n_assign)
    cnt = meta[0, :N_EXPERTS]
    start = meta[1, :N_EXPERTS]
    blk_exp = meta[2, :n_blocks]
    n_used = meta[3, :1]

    meta_rows = lax.bitcast_convert_type(_pad_lanes(jnp.concatenate(
        [idx_t.T.astype(F32), gate_t.T, jnp.arange(n_tok, dtype=F32)[:, None]], axis=1)), jnp.uint32)
    pad_meta = lax.bitcast_convert_type(_pad_lanes(jnp.concatenate(
        [jnp.full((MOE_BM, TOP_K), -1.0, F32), jnp.zeros((MOE_BM, TOP_K), F32),
         (TOP_K * n_tok + jnp.arange(MOE_BM, dtype=F32))[:, None]], axis=1)), jnp.uint32)
    zsrc = jnp.concatenate([jnp.zeros((MOE_BM, PACK_W), jnp.uint32), pad_meta,
                            jnp.zeros((MOE_BM, D_MODEL - PACK_W - LANES), jnp.uint32)],
                           axis=1).reshape(MOE_BM, SUBLANES, LANES)

    xs = _dispatch(dest_flat, cnt, start, n_used, x1, meta_rows, mod3, g2, zsrc, n_rows)
    ytm = _experts(blk_exp, n_used, xs, w_gate_up[li].astype(BF16), b_gate_up[li][:, None, :],
                   w_down[li].astype(BF16), b_down[li][:, None, :], n_tok)
    out = _combine(ytm, x1, mod3, final_g[None, :])
    return out.reshape(bsz, SEQ, D_MODEL)
```

```python
import functools

import jax
import jax.numpy as jnp
from jax import lax
from jax.experimental import pallas as pl
from jax.experimental.pallas import tpu as pltpu

F32 = jnp.float32
BF16 = jnp.bfloat16
HIGHEST = lax.Precision.HIGHEST

D_MODEL = 1024
SEQ = 2048
CTX_LEN = 256
GRID_W = 64
D_SSD = 1024
D_SC = 1024
HEAD_DIM = 64
HEADS = 16
GROUPS = 2
STATE = 128
CHUNK = 128
N_EXPERTS = 32
TOP_K = 4
D_FF = 1024
SWIGLU_LIMIT = 7.0
SWIGLU_ALPHA = 1.702
NORM_EPS = 1e-6
N_MOD = 6
XBC_W = D_SSD + 2 * GROUPS * STATE
XB_W = D_SSD + GROUPS * STATE
LANES = 128

Z0 = 0
X0 = Z0 + D_SSD
B0 = X0 + D_SSD
C0 = B0 + GROUPS * STATE
DT0 = C0 + GROUPS * STATE
SC0 = DT0 + 2 * HEADS

TOK_TILE = 512
MOE_BM = 256
RT_TILE = 512
DISP_TILE = 256
COMB_TILE = 512
SUBLANES = 8
PACK_W = D_MODEL // 2
META_IDX = 0
META_GATE = TOP_K
META_TOK = 2 * TOP_K
VMEM_LIMIT = 56 * 1024 * 1024


def _silu(v):
    return v * jax.nn.sigmoid(v)


def _softplus(v):
    return jnp.maximum(v, 0.0) + jnp.log1p(jnp.exp(-jnp.abs(v)))


def _rms(v):
    return v * lax.rsqrt(jnp.mean(v * v, axis=-1, keepdims=True) + NORM_EPS)


def _dot(a, b):
    return jnp.dot(a, b, preferred_element_type=F32)


def _expand2(v, e2):
    hi = v.astype(BF16)
    lo = (v - hi.astype(F32)).astype(BF16)
    return _dot(jnp.concatenate([hi, lo], axis=1), e2)


def _mod_kernel(c_ref, w_ref, b_ref, o_ref):
    o_ref[...] = jnp.dot(_silu(c_ref[...]), w_ref[...], precision=HIGHEST,
                         preferred_element_type=F32) + b_ref[...]


def _mod(cvec, w_mod, b_mod):
    rows = cvec.shape[0]
    n = w_mod.shape[1]
    tn = 1536
    return pl.pallas_call(
        _mod_kernel,
        out_shape=jax.ShapeDtypeStruct((rows, n), F32),
        grid=(n // tn,),
        in_specs=[pl.BlockSpec((rows, D_MODEL), lambda j: (0, 0)),
                  pl.BlockSpec((D_MODEL, tn), lambda j: (0, j)),
                  pl.BlockSpec((1, tn), lambda j: (0, j))],
        out_specs=pl.BlockSpec((rows, tn), lambda j: (0, j)),
        compiler_params=pltpu.CompilerParams(dimension_semantics=("arbitrary",),
                                             vmem_limit_bytes=VMEM_LIMIT),
        name="mod",
    )(cvec, w_mod, b_mod)


def _ctx_kernel(ctx_ref, mod_ref, g1_ref, wxb_ref, wdt_ref, cw_ref, cb_ref, dtb_ref, alog_ref, e64_ref,
                h0_ref):
    L = CTX_LEN
    m = mod_ref[0]
    hc = _rms(ctx_ref[0]) * g1_ref[...] * (1.0 + m[1:2]) + m[0:1]
    hb = hc.astype(BF16)
    pxb = _dot(hb, wxb_ref[...])
    dtr = _dot(hb, wdt_ref[...])
    rowi = lax.broadcasted_iota(jnp.int32, (L, XB_W), 0)
    dn = jnp.where(rowi == 0, 0.0, pltpu.roll(pxb, 1, 0))
    up = jnp.where(rowi == L - 1, 0.0, pltpu.roll(pxb, L - 1, 0))
    cw = cw_ref[...]
    xb = _silu(cw[0:1] * dn + cw[1:2] * pxb + cw[2:3] * up + cb_ref[...])
    xs = xb[:, :D_SSD]
    bm = xb[:, D_SSD:].astype(BF16)
    dt = _softplus(dtr + dtb_ref[...])
    da = dt * (-jnp.exp(alog_ref[...]))
    ri = lax.broadcasted_iota(jnp.int32, (L, L), 0)
    ci = lax.broadcasted_iota(jnp.int32, (L, L), 1)
    for d in range(2):
        tri = (ci <= ri) if d == 0 else (ci >= ri)
        cum = jnp.dot(tri.astype(F32), da, precision=HIGHEST, preferred_element_type=F32)
        last = cum[L - 1:L] if d == 0 else cum[0:1]
        w_e = _expand2(jnp.exp(last - cum) * dt, e64_ref[d])
        xw = (xs * w_e).astype(BF16)
        for g in range(GROUPS):
            gw = D_SSD // GROUPS
            st = lax.dot_general(bm[:, g * STATE:(g + 1) * STATE], xw[:, g * gw:(g + 1) * gw],
                                 (((0,), (0,)), ((), ())), preferred_element_type=F32)
            h0_ref[0, d, :, g * gw:(g + 1) * gw] = st


def _ctx_states(ctx, mod3, g1, wxb, wdt, cw, cb, dtb, alog, e64):
    bsz = ctx.shape[0]
    mod_row = bsz
    const = lambda *shape: pl.BlockSpec(shape, lambda b: (0,) * len(shape))
    return pl.pallas_call(
        _ctx_kernel,
        out_shape=jax.ShapeDtypeStruct((bsz, 2, STATE, D_SSD), F32),
        grid=(bsz,),
        in_specs=[pl.BlockSpec((1, CTX_LEN, D_MODEL), lambda b: (b, 0, 0)),
                  pl.BlockSpec((1, N_MOD, D_MODEL), lambda b: (mod_row, 0, 0)),
                  const(1, D_MODEL), const(D_MODEL, XB_W), const(D_MODEL, LANES),
                  const(3, XB_W), const(1, XB_W), const(1, LANES), const(1, LANES),
                  const(2, 2 * LANES, D_SSD)],
        out_specs=pl.BlockSpec((1, 2, STATE, D_SSD), lambda b: (b, 0, 0, 0)),
        compiler_params=pltpu.CompilerParams(dimension_semantics=("arbitrary",),
                                             vmem_limit_bytes=VMEM_LIMIT),
        name="ctx_states",
    )(ctx, mod3, g1, wxb, wdt, cw, cb, dtb, alog, e64)


def _inproj_kernel(x_ref, mod_ref, g1_ref, wz_ref, wxbc_ref, wdt_ref, wb_ref, wc_ref, wu_ref,
                   z_ref, xbc_ref, dt_ref, scb_ref, v_ref):
    m = mod_ref[0]
    hx = _rms(x_ref[...]) * g1_ref[...] * (1.0 + m[1:2]) + m[0:1]
    hb = hx.astype(BF16)
    z_ref[...] = _dot(hb, wz_ref[...]).astype(BF16)
    xbc_ref[...] = _dot(hb, wxbc_ref[...]).astype(BF16)
    dt_ref[...] = _dot(hb, wdt_ref[...])
    scb_ref[...] = _dot(hb, wb_ref[...]).astype(BF16)
    v_ref[...] = (_dot(hb, wc_ref[...]) * _dot(hb, wu_ref[...])).astype(BF16)


def _inproj(x2, mod3, g1, wz, wxbc, wdt, wb, wc, wu):
    t = x2.shape[0]
    tm = TOK_TILE
    per_b = SEQ // tm
    const = lambda *shape: pl.BlockSpec(shape, lambda i: (0,) * len(shape))
    tile = lambda w: pl.BlockSpec((tm, w), lambda i: (i, 0))
    return pl.pallas_call(
        _inproj_kernel,
        out_shape=(jax.ShapeDtypeStruct((t, D_SSD), BF16), jax.ShapeDtypeStruct((t, XBC_W), BF16),
                   jax.ShapeDtypeStruct((t, LANES), F32), jax.ShapeDtypeStruct((t, D_SC), BF16),
                   jax.ShapeDtypeStruct((t, D_SC), BF16)),
        grid=(t // tm,),
        in_specs=[tile(D_MODEL),
                  pl.BlockSpec((1, N_MOD, D_MODEL), lambda i: (i // per_b, 0, 0)),
                  const(1, D_MODEL), const(D_MODEL, D_SSD), const(D_MODEL, XBC_W), const(D_MODEL, LANES),
                  const(D_MODEL, D_SC), const(D_MODEL, D_SC), const(D_MODEL, D_SC)],
        out_specs=(tile(D_SSD), tile(XBC_W), tile(LANES), tile(D_SC), tile(D_SC)),
        compiler_params=pltpu.CompilerParams(dimension_semantics=("arbitrary",),
                                             vmem_limit_bytes=VMEM_LIMIT),
        name="inproj",
    )(x2, mod3, g1, wz, wxbc, wdt, wb, wc, wu)


def _ssd_kernel(xbc_ref, z_ref, dt_ref, h0_ref, cw_ref, cb_ref, dtb_ref, alog_ref, dsk_ref, g_ref,
                e64_ref, e128_ref, o_ref, xc_ref, y_ref, s_ref):
    Q = CHUNK
    nck = SEQ // Q
    gw = D_SSD // GROUPS

    rowi = lax.broadcasted_iota(jnp.int32, (SUBLANES, XBC_W), 0)

    def conv_body(c, carry):
        r0 = pl.multiple_of(c * Q, Q)
        main = xbc_ref[0, pl.ds(r0, Q), :].astype(F32)
        pstart = pl.multiple_of(jnp.maximum(r0 - 16, 0), 16)
        nstart = pl.multiple_of(jnp.minimum(r0 + Q, SEQ - 16), 16)
        prev = xbc_ref[0, pl.ds(pstart, 16), :].astype(F32)[15:16]
        nxt = xbc_ref[0, pl.ds(nstart, 16), :].astype(F32)[0:1]
        prev = jnp.where(c > 0, prev, 0.0)
        nxt = jnp.where(c < nck - 1, nxt, 0.0)
        dn = pltpu.roll(main, 1, 0)
        up = pltpu.roll(main, Q - 1, 0)
        dn = jnp.concatenate([jnp.where(rowi == 0, prev, dn[0:SUBLANES]), dn[SUBLANES:]], axis=0)
        up = jnp.concatenate([up[:Q - SUBLANES], jnp.where(rowi == SUBLANES - 1, nxt, up[Q - SUBLANES:])], axis=0)
        cw = cw_ref[...]
        conv = cw[0:1] * dn + cw[1:2] * main + cw[2:3] * up + cb_ref[...]
        xc_ref[pl.ds(r0, Q), :] = _silu(conv).astype(BF16)
        return carry

    lax.fori_loop(0, nck, conv_body, 0)

    ri = lax.broadcasted_iota(jnp.int32, (Q, Q), 0)
    ci = lax.broadcasted_iota(jnp.int32, (Q, Q), 1)
    lane = lax.broadcasted_iota(jnp.int32, (Q, LANES), 1)
    a_neg = -jnp.exp(alog_ref[...])

    def chunk(c, d, first):
        r0 = pl.multiple_of(c * Q, Q)
        rows = pl.ds(r0, Q)
        xs_b = xc_ref[rows, 0:D_SSD]
        xs = xs_b.astype(F32)
        bm = xc_ref[rows, D_SSD:D_SSD + GROUPS * STATE]
        cm = xc_ref[rows, D_SSD + GROUPS * STATE:XBC_W]
        dt = _softplus(dt_ref[0, rows, :] + dtb_ref[...])
        da = dt * a_neg
        tri = (ci <= ri) if d == 0 else (ci >= ri)
        p0 = da.astype(BF16)
        r1 = da - p0.astype(F32)
        p1 = r1.astype(BF16)
        p2 = (r1 - p1.astype(F32)).astype(BF16)
        tri_b = jnp.where(tri, 1.0, 0.0).astype(BF16)
        cum = _dot(jnp.concatenate([tri_b, tri_b, tri_b], axis=1),
                   jnp.concatenate([p0, p1, p2], axis=0))
        sub_t = (cum - jnp.log(dt)).T
        last = cum[Q - 1:Q] if d == 0 else cum[0:1]
        ecum_e = _expand2(jnp.exp(cum), e64_ref[d])
        w_e = _expand2(jnp.exp(last - cum) * dt, e64_ref[d])
        colb = _expand2(cum, e128_ref[d])
        decay_e = ecum_e[Q - 1:Q] if d == 0 else ecum_e[0:1]

        gmat = [lax.dot_general(cm[:, g * STATE:(g + 1) * STATE], bm[:, g * STATE:(g + 1) * STATE],
                                (((1,), (1,)), ((), ())), preferred_element_type=F32)
                for g in range(GROUPS)]
        zero_b = jnp.zeros((Q, LANES), BF16)
        y_parts = []
        for p in range(HEADS // 2):
            g = (2 * p) // (HEADS // GROUPS)
            ms = []
            for hh in (2 * p, 2 * p + 1):
                seg = colb[:, hh * LANES:(hh + 1) * LANES] - sub_t[HEADS * d + hh:HEADS * d + hh + 1, :]
                ms.append((jnp.where(tri, jnp.exp(seg), 0.0) * gmat[g]).astype(BF16))
            mcat = jnp.concatenate(ms, axis=1)
            xp = xs_b[:, p * LANES:(p + 1) * LANES]
            rhs = jnp.concatenate([jnp.where(lane < HEAD_DIM, xp, zero_b),
                                   jnp.where(lane >= HEAD_DIM, xp, zero_b)], axis=0)
            y_parts.append(_dot(mcat, rhs))
        y_diag = jnp.concatenate(y_parts, axis=1)

        s_old = s_ref[d]
        s_bf = s_old.astype(BF16)
        y_off = jnp.concatenate(
            [_dot(cm[:, g * STATE:(g + 1) * STATE], s_bf[:, g * gw:(g + 1) * gw]) for g in range(GROUPS)],
            axis=1)
        y = y_diag + y_off * ecum_e

        xw = (xs * w_e).astype(BF16)
        upd = jnp.concatenate(
            [lax.dot_general(bm[:, g * STATE:(g + 1) * STATE], xw[:, g * gw:(g + 1) * gw],
                             (((0,), (0,)), ((), ())), preferred_element_type=F32) for g in range(GROUPS)],
            axis=1)
        s_ref[d] = s_old * decay_e + upd

        if first:
            y_ref[rows, :] = y + dsk_ref[...] * xs
        else:
            tot = y_ref[rows, :] + y
            zz = z_ref[0, rows, :].astype(F32)
            gz = tot * _silu(zz)
            outs = []
            for g in range(GROUPS):
                gg = gz[:, g * gw:(g + 1) * gw]
                outs.append(gg * lax.rsqrt(jnp.mean(gg * gg, axis=-1, keepdims=True) + NORM_EPS))
            o_ref[0, rows, :] = (jnp.concatenate(outs, axis=1) * g_ref[...]).astype(BF16)

    s_ref[...] = h0_ref[0]

    def first_half(i, carry):
        chunk(i, 0, True)
        chunk(nck - 1 - i, 1, True)
        return carry

    def second_half(i, carry):
        chunk(i, 0, False)
        chunk(nck - 1 - i, 1, False)
        return carry

    lax.fori_loop(0, nck // 2, first_half, 0)
    lax.fori_loop(nck // 2, nck, second_half, 0)


def _ssd(xbc3, z3, dt3, h0, cw, cb, dtb, alog, dsk, g, e64, e128):
    bsz = xbc3.shape[0]
    const = lambda *shape: pl.BlockSpec(shape, lambda b: (0,) * len(shape))
    seq = lambda w: pl.BlockSpec((1, SEQ, w), lambda b: (b, 0, 0))
    return pl.pallas_call(
        _ssd_kernel,
        out_shape=jax.ShapeDtypeStruct((bsz, SEQ, D_SSD), BF16),
        grid=(bsz,),
        in_specs=[seq(XBC_W), seq(D_SSD), seq(LANES),
                  pl.BlockSpec((1, 2, STATE, D_SSD), lambda b: (b, 0, 0, 0)),
                  const(3, XBC_W), const(1, XBC_W), const(1, LANES), const(1, LANES),
                  const(1, D_SSD), const(1, D_SSD), const(2, 2 * LANES, D_SSD),
                  const(2, 2 * LANES, HEADS * LANES)],
        out_specs=seq(D_SSD),
        scratch_shapes=[pltpu.VMEM((SEQ, XBC_W), BF16), pltpu.VMEM((SEQ, D_SSD), F32),
                        pltpu.VMEM((2, STATE, D_SSD), F32)],
        compiler_params=pltpu.CompilerParams(dimension_semantics=("arbitrary",),
                                             vmem_limit_bytes=VMEM_LIMIT),
        name="ssd",
    )(xbc3, z3, dt3, h0, cw, cb, dtb, alog, dsk, g, e64, e128)


def _outproj_kernel(x_ref, yssd_ref, scb_ref, v_ref, vp_ref, vn_ref, mod_ref, scw_ref, wo1_ref, wo2_ref,
                    g2_ref, wrt_ref, br_ref, x1_ref, lg_ref):
    tm = TOK_TILE
    per_b = SEQ // tm
    i = pl.program_id(0)
    first = (i % per_b) == 0
    last = (i % per_b) == per_b - 1
    m = mod_ref[0]
    v = v_ref[...].astype(F32)
    vp = jnp.where(first, 0.0, vp_ref[...].astype(F32))
    vn = jnp.where(last, 0.0, vn_ref[...].astype(F32))
    dn = jnp.concatenate([vp, v[:tm - GRID_W]], axis=0)
    up = jnp.concatenate([v[GRID_W:], vn], axis=0)
    scw = scw_ref[...]
    ysc = scb_ref[...].astype(F32) * (scw[0:1] * dn + scw[1:2] * v + scw[2:3] * up)
    out = _dot(yssd_ref[...], wo1_ref[...]) + _dot(ysc.astype(BF16), wo2_ref[...])
    x1 = x_ref[...] + m[2:3] * out
    x1_ref[...] = x1
    h2 = _rms(x1) * g2_ref[...] * (1.0 + m[4:5]) + m[3:4]
    lg_ref[...] = lax.dot_general(wrt_ref[...], h2, (((1,), (1,)), ((), ())), precision=HIGHEST,
                                  preferred_element_type=F32) + br_ref[...]


def _outproj(x2, yssd, scb, v, mod3, scw, wo1, wo2, g2, wrt, br):
    t = x2.shape[0]
    tm = TOK_TILE
    per_b = SEQ // tm
    r = tm // GRID_W
    nrow = t // GRID_W
    const = lambda *shape: pl.BlockSpec(shape, lambda i: (0,) * len(shape))
    tile = lambda w: pl.BlockSpec((tm, w), lambda i: (i, 0))
    return pl.pallas_call(
        _outproj_kernel,
        out_shape=(jax.ShapeDtypeStruct((t, D_MODEL), F32), jax.ShapeDtypeStruct((N_EXPERTS, t), F32)),
        grid=(t // tm,),
        in_specs=[tile(D_MODEL), tile(D_SSD), tile(D_SC), tile(D_SC),
                  pl.BlockSpec((GRID_W, D_SC), lambda i: (jnp.maximum(i * r - 1, 0), 0)),
                  pl.BlockSpec((GRID_W, D_SC), lambda i: (jnp.minimum((i + 1) * r, nrow - 1), 0)),
                  pl.BlockSpec((1, N_MOD, D_MODEL), lambda i: (i // per_b, 0, 0)),
                  const(3, D_SC), const(D_SSD, D_MODEL), const(D_SC, D_MODEL), const(1, D_MODEL),
                  const(N_EXPERTS, D_MODEL), const(N_EXPERTS, 1)],
        out_specs=(tile(D_MODEL), pl.BlockSpec((N_EXPERTS, tm), lambda i: (0, i))),
        compiler_params=pltpu.CompilerParams(dimension_semantics=("arbitrary",),
                                             vmem_limit_bytes=VMEM_LIMIT),
        name="outproj",
    )(x2, yssd, scb, v, v, v, mod3, scw, wo1, wo2, g2, wrt, br)


def _route_kernel(lg_ref, dest_ref, gate_ref, idx_ref, meta_ref, rank_ref, carry_ref, *, n_tok, n_blocks):
    tt = RT_TILE
    ne = N_EXPERTS
    eio = lax.broadcasted_iota(jnp.int32, (ne, tt), 0)
    si = lax.broadcasted_iota(jnp.int32, (tt, tt), 0)
    ti = lax.broadcasted_iota(jnp.int32, (tt, tt), 1)
    before = (si < ti).astype(BF16)
    carry_ref[...] = jnp.zeros_like(carry_ref)

    def tile_body(j, c):
        t0 = pl.multiple_of(j * tt, tt)
        l = lg_ref[:, pl.ds(t0, tt)]
        onehot = jnp.zeros((ne, tt), F32)
        tops, sels = [], []
        for _ in range(TOP_K):
            mx = jnp.max(l, axis=0, keepdims=True)
            idx = jnp.min(jnp.where(l == mx, eio, ne), axis=0, keepdims=True)
            sel = eio == idx
            l = jnp.where(sel, -jnp.inf, l)
            onehot = onehot + sel.astype(F32)
            tops.append(mx)
            sels.append(sel)
            idx_ref[pl.ds(len(tops) - 1, 1), pl.ds(t0, tt)] = idx
        ex = [jnp.exp(tv - tops[0]) for tv in tops]
        den = ex[0] + ex[1] + ex[2] + ex[3]
        prefix = _dot(onehot.astype(BF16), before) + carry_ref[:, 0:1]
        for k in range(TOP_K):
            gate_ref[pl.ds(k, 1), pl.ds(t0, tt)] = ex[k] / den
            rk = jnp.sum(jnp.where(sels[k], prefix, 0.0), axis=0, keepdims=True)
            rank_ref[pl.ds(k, 1), pl.ds(t0, tt)] = rk.astype(jnp.int32)
        carry_ref[...] = carry_ref[...] + jnp.sum(onehot, axis=1, keepdims=True)
        return c

    lax.fori_loop(0, n_tok // tt, tile_body, 0)

    counts = carry_ref[...]
    padded = jnp.floor((counts + (MOE_BM - 1)) * (1.0 / MOE_BM)) * MOE_BM
    er = lax.broadcasted_iota(jnp.int32, (ne, ne), 0)
    ec = lax.broadcasted_iota(jnp.int32, (ne, ne), 1)
    pad_start = jnp.dot((ec < er).astype(F32), padded, precision=HIGHEST, preferred_element_type=F32)
    pad_end = pad_start + padded

    def dest_body(j, c):
        t0 = pl.multiple_of(j * tt, tt)
        for k in range(TOP_K):
            idx = idx_ref[pl.ds(k, 1), pl.ds(t0, tt)]
            base = jnp.sum(jnp.where(eio == idx, pad_start[:, 0:1], 0.0), axis=0, keepdims=True)
            dest_ref[pl.ds(k, 1), pl.ds(t0, tt)] = base.astype(jnp.int32) + rank_ref[pl.ds(k, 1), pl.ds(t0, tt)]
        return c

    lax.fori_loop(0, n_tok // tt, dest_body, 0)

    width = meta_ref.shape[1]
    sub = lax.broadcasted_iota(jnp.int32, (ne, width), 0)
    lan = lax.broadcasted_iota(jnp.int32, (ne, width), 1)
    diag = sub == lan
    cnt_row = jnp.sum(jnp.where(diag, counts[:, 0:1], 0.0), axis=0, keepdims=True)
    start_row = jnp.sum(jnp.where(diag, pad_start[:, 0:1], 0.0), axis=0, keepdims=True)
    blk_start = (lan * MOE_BM).astype(F32)
    blk_exp = jnp.sum((pad_end[:, 0:1] <= blk_start).astype(F32), axis=0, keepdims=True)
    blk_exp = jnp.minimum(blk_exp, float(ne - 1))
    used = jnp.sum(padded[:, 0:1], axis=0, keepdims=True) * (1.0 / MOE_BM)
    meta_ref[0:1, :] = cnt_row.astype(jnp.int32)
    meta_ref[1:2, :] = start_row.astype(jnp.int32)
    meta_ref[2:3, :] = blk_exp.astype(jnp.int32)
    meta_ref[3:4, :] = jnp.broadcast_to(used, (1, width)).astype(jnp.int32)
    meta_ref[4:8, :] = jnp.zeros((4, width), jnp.int32)


def _route(lgt, n_blocks):
    ne, n_tok = lgt.shape
    width = -(-n_blocks // LANES) * LANES
    full = lambda *shape: pl.BlockSpec(shape, lambda: (0,) * len(shape))
    return pl.pallas_call(
        functools.partial(_route_kernel, n_tok=n_tok, n_blocks=n_blocks),
        out_shape=(jax.ShapeDtypeStruct((TOP_K, n_tok), jnp.int32),
                   jax.ShapeDtypeStruct((TOP_K, n_tok), F32),
                   jax.ShapeDtypeStruct((TOP_K, n_tok), jnp.int32),
                   jax.ShapeDtypeStruct((8, width), jnp.int32)),
        in_specs=[full(ne, n_tok)],
        out_specs=(full(TOP_K, n_tok), full(TOP_K, n_tok), full(TOP_K, n_tok), full(8, width)),
        scratch_shapes=[pltpu.VMEM((TOP_K, n_tok), jnp.int32), pltpu.VMEM((ne, LANES), F32)],
        compiler_params=pltpu.CompilerParams(vmem_limit_bytes=VMEM_LIMIT),
        name="route",
    )(lgt)


def _dispatch_kernel(dest_ref, cnt_ref, start_ref, nu_ref, x1_ref, meta_ref, mod_ref, g2_ref, zsrc_ref, xs_ref,
                     hbuf, sem, zsem):
    i = pl.program_id(0)
    n = pl.num_programs(0)
    tl = DISP_TILE
    slot = i % 2
    nb = xs_ref.shape[0] // MOE_BM

    def zero_block(b):
        return pltpu.make_async_copy(zsrc_ref, xs_ref.at[pl.ds(b * MOE_BM, MOE_BM)], zsem)

    @pl.when(i == 0)
    def _():
        def start_e(e, c):
            @pl.when(cnt_ref[e] > 0)
            def _():
                zero_block((start_ref[e] + cnt_ref[e] - 1) // MOE_BM).start()
            return c

        def wait_e(e, c):
            @pl.when(cnt_ref[e] > 0)
            def _():
                zero_block(0).wait()
            return c

        def start_t(b, c):
            zero_block(b).start()
            return c

        def wait_t(b, c):
            zero_block(0).wait()
            return c

        lax.fori_loop(0, N_EXPERTS, start_e, 0)
        lax.fori_loop(nu_ref[0], nb, start_t, 0)
        lax.fori_loop(0, N_EXPERTS, wait_e, 0)
        lax.fori_loop(nu_ref[0], nb, wait_t, 0)

    m = mod_ref[0]
    h2 = _rms(x1_ref[...]) * g2_ref[...] * (1.0 + m[4:5]) + m[3:4]
    lo = pltpu.bitcast(h2[:, :PACK_W].astype(BF16).astype(F32), jnp.uint32) >> 16
    hi = pltpu.bitcast(h2[:, PACK_W:].astype(BF16).astype(F32), jnp.uint32) & jnp.uint32(0xFFFF0000)
    row = jnp.concatenate([lo | hi, meta_ref[...], jnp.zeros((tl, D_MODEL - PACK_W - LANES), jnp.uint32)], axis=1)
    hbuf[slot] = row.reshape(tl, SUBLANES, LANES)

    def per_tok(t, c):
        tok = i * tl + t
        for k in range(TOP_K):
            d = dest_ref[tok * TOP_K + k]
            pltpu.make_async_copy(hbuf.at[slot, t], xs_ref.at[d], sem.at[slot]).start()
        return c

    lax.fori_loop(0, tl, per_tok, 0)

    def wait_slot(sl):
        for _ in range(TOP_K):
            pltpu.make_async_copy(hbuf.at[sl], xs_ref.at[pl.ds(0, tl)], sem.at[sl]).wait()

    @pl.when(i > 0)
    def _():
        wait_slot(1 - slot)

    @pl.when(i == n - 1)
    def _():
        wait_slot(slot)


def _dispatch(dest_flat, cnt, start, n_used, x1, meta_rows, mod3, g2, zsrc, n_rows):
    n_tok = x1.shape[0]
    tl = DISP_TILE
    per_b = SEQ // tl
    return pl.pallas_call(
        _dispatch_kernel,
        out_shape=jax.ShapeDtypeStruct((n_rows, SUBLANES, LANES), jnp.uint32),
        grid_spec=pltpu.PrefetchScalarGridSpec(
            num_scalar_prefetch=4,
            grid=(n_tok // tl,),
            in_specs=[pl.BlockSpec((tl, D_MODEL), lambda i, *_: (i, 0)),
                      pl.BlockSpec((tl, LANES), lambda i, *_: (i, 0)),
                      pl.BlockSpec((1, N_MOD, D_MODEL), lambda i, *_: (i // per_b, 0, 0)),
                      pl.BlockSpec((1, D_MODEL), lambda i, *_: (0, 0)),
                      pl.BlockSpec((MOE_BM, SUBLANES, LANES), lambda i, *_: (0, 0, 0))],
            out_specs=pl.BlockSpec(memory_space=pl.ANY),
            scratch_shapes=[pltpu.VMEM((2, tl, SUBLANES, LANES), jnp.uint32),
                            pltpu.SemaphoreType.DMA((2,)), pltpu.SemaphoreType.DMA]),
        compiler_params=pltpu.CompilerParams(dimension_semantics=("arbitrary",),
                                             vmem_limit_bytes=VMEM_LIMIT),
        name="dispatch",
    )(dest_flat, cnt, start, n_used, x1, meta_rows, mod3, g2, zsrc)


def _expert_kernel(be_ref, nu_ref, xs_ref, wgu_ref, bgu_ref, wd_ref, bd_ref, zero_ref, ytm_ref,
                   big, idv, ids, sc_sem, id_sem, z_sem, *, n_tok):
    j = pl.program_id(0)
    nbk = pl.num_programs(0) - 1
    nu = nu_ref[0]
    slot = j % 2
    prev = 1 - slot

    @pl.when(j == 0)
    def _():
        cp = pltpu.make_async_copy(zero_ref, ytm_ref.at[pl.ds(TOP_K * n_tok, MOE_BM)], z_sem)
        cp.start()
        cp.wait()

    n_pc = 4
    pw1 = D_FF // n_pc
    pw2 = D_MODEL // n_pc
    n_lt = D_MODEL // LANES
    spb = MOE_BM // SUBLANES
    groups = [48] * n_pc + [16] * n_pc
    assert sum(groups) == MOE_BM

    def slab_store(unit, col0, val):
        for s in range(val.shape[1] // LANES):
            t0 = unit * MOE_BM + (col0 // LANES + s) * spb
            big[pl.ds(t0, spb)] = val[:, s * LANES:(s + 1) * LANES].reshape(spb, SUBLANES, LANES)

    def slab_load(unit, row0, n):
        return jnp.concatenate(
            [big[pl.ds(unit * MOE_BM + s * spb + row0 // SUBLANES, n // SUBLANES)].reshape(n, LANES)
             for s in range(n_lt)], axis=1)

    def scatter_group(g):
        lo = sum(groups[:g])
        n = groups[g]
        big[pl.ds(prev * MOE_BM + lo, n)] = slab_load(3 + prev, lo, n).reshape(n, n_lt, LANES)
        for r in range(lo, lo + n):
            pltpu.make_async_copy(big.at[prev * MOE_BM + r], ytm_ref.at[ids[prev, 0, r]],
                                  sc_sem.at[prev]).start(priority=r % 2)

    def scatter_wait(sl):
        pltpu.make_async_copy(big.at[pl.ds(0, MOE_BM)], ytm_ref.at[pl.ds(0, MOE_BM)], sc_sem.at[sl]).wait()

    def compute(with_scatter):
        e_f = be_ref[jnp.minimum(j, nbk - 1)].astype(F32)
        words = xs_ref[...].reshape(MOE_BM, D_MODEL)
        packed = words[:, 0:PACK_W]
        meta = pltpu.bitcast(words[:, PACK_W:PACK_W + LANES], F32)
        xb = jnp.concatenate(
            [pltpu.bitcast(packed << 16, F32).astype(BF16),
             pltpu.bitcast(packed & jnp.uint32(0xFFFF0000), F32).astype(BF16)], axis=1)
        gate = jnp.zeros((MOE_BM, 1), F32)
        kk = jnp.zeros((MOE_BM, 1), F32)
        for k in range(TOP_K):
            mk = meta[:, META_IDX + k:META_IDX + k + 1] == e_f
            gate = gate + jnp.where(mk, meta[:, META_GATE + k:META_GATE + k + 1], 0.0)
            kk = kk + jnp.where(mk, float(k), 0.0)
        row = kk * float(n_tok) + meta[:, META_TOK:META_TOK + 1]
        row_t = jnp.broadcast_to(row, (MOE_BM, LANES)).T
        idv[slot] = row_t[0:8].astype(jnp.int32)
        pltpu.make_async_copy(idv.at[slot], ids.at[slot], id_sem.at[slot]).start()
        for c in range(n_pc):
            if with_scatter:
                scatter_group(c)
            cg = slice(c * pw1, (c + 1) * pw1)
            cl = slice(D_FF + c * pw1, D_FF + (c + 1) * pw1)
            glu = jnp.minimum(_dot(xb, wgu_ref[0, :, cg]) + bgu_ref[0, :, cg], SWIGLU_LIMIT)
            lin = jnp.clip(_dot(xb, wgu_ref[0, :, cl]) + bgu_ref[0, :, cl], -SWIGLU_LIMIT, SWIGLU_LIMIT)
            slab_store(2, c * pw1, glu * jax.nn.sigmoid(SWIGLU_ALPHA * glu) * (lin + 1.0))
        act = slab_load(2, 0, MOE_BM).astype(BF16)
        for c in range(n_pc):
            if with_scatter:
                scatter_group(n_pc + c)
            cs = slice(c * pw2, (c + 1) * pw2)
            slab_store(3 + slot, c * pw2, (_dot(act, wd_ref[0, :, cs]) + bd_ref[0, :, cs]) * gate)

    def ids_wait():
        pltpu.make_async_copy(idv.at[prev], ids.at[prev], id_sem.at[prev]).wait()

    @pl.when(jnp.logical_and(j >= 3, j <= nu))
    def _():
        scatter_wait(prev)

    @pl.when(j == 0)
    def _():
        compute(False)

    @pl.when(jnp.logical_and(j >= 1, j < nu))
    def _():
        ids_wait()
        compute(True)

    @pl.when(j == nu)
    def _():
        ids_wait()
        for g in range(len(groups)):
            scatter_group(g)
        scatter_wait(prev)

    @pl.when(jnp.logical_and(j == nu, j >= 2))
    def _():
        scatter_wait(slot)


def _experts(blk_exp, n_used, xs, wgu, bgu, wd, bd, n_tok):
    n_rows = xs.shape[0]
    nb = n_rows // MOE_BM
    row_blk = lambda j, be, nu: (jnp.minimum(j, nu[0] - 1), 0, 0)
    per_e = lambda j, be, nu: (be[jnp.minimum(j, nb - 1)], 0, 0)
    sub = D_MODEL // LANES
    zero = jnp.zeros((MOE_BM, sub, LANES), F32)
    return pl.pallas_call(
        functools.partial(_expert_kernel, n_tok=n_tok),
        out_shape=jax.ShapeDtypeStruct((TOP_K * n_tok + MOE_BM, sub, LANES), F32),
        grid_spec=pltpu.PrefetchScalarGridSpec(
            num_scalar_prefetch=2,
            grid=(nb + 1,),
            in_specs=[pl.BlockSpec((MOE_BM, SUBLANES, LANES), row_blk),
                      pl.BlockSpec((1, D_MODEL, 2 * D_FF), per_e),
                      pl.BlockSpec((1, 1, 2 * D_FF), per_e),
                      pl.BlockSpec((1, D_FF, D_MODEL), per_e),
                      pl.BlockSpec((1, 1, D_MODEL), per_e),
                      pl.BlockSpec((MOE_BM, sub, LANES), lambda j, be, nu: (0, 0, 0))],
            out_specs=pl.BlockSpec(memory_space=pl.ANY),
            scratch_shapes=[pltpu.VMEM((5 * MOE_BM, sub, LANES), F32), pltpu.VMEM((2, 8, MOE_BM), jnp.int32),
                            pltpu.SMEM((2, 8, MOE_BM), jnp.int32), pltpu.SemaphoreType.DMA((2,)),
                            pltpu.SemaphoreType.DMA((2,)), pltpu.SemaphoreType.DMA]),
        compiler_params=pltpu.CompilerParams(dimension_semantics=("arbitrary",),
                                             vmem_limit_bytes=VMEM_LIMIT),
        name="experts",
    )(blk_exp, n_used, xs, wgu, bgu, wd, bd, zero)


def _combine_kernel(y0_ref, y1_ref, y2_ref, y3_ref, x1_ref, mod_ref, fg_ref, o_ref):
    m = mod_ref[0]
    moe = ((y0_ref[...] + y1_ref[...]) + (y2_ref[...] + y3_ref[...])).reshape(COMB_TILE, D_MODEL)
    x2 = x1_ref[...] + m[5:6] * moe
    o_ref[...] = _rms(x2) * fg_ref[...]


def _combine(ytm, x1, mod3, fg):
    n_tok = x1.shape[0]
    tc = COMB_TILE
    per_b = SEQ // tc
    nt = n_tok // tc
    slot_spec = lambda k: pl.BlockSpec((tc, D_MODEL // LANES, LANES), lambda i: (i + k * nt, 0, 0))
    return pl.pallas_call(
        _combine_kernel,
        out_shape=jax.ShapeDtypeStruct((n_tok, D_MODEL), F32),
        grid=(nt,),
        in_specs=[slot_spec(0), slot_spec(1), slot_spec(2), slot_spec(3),
                  pl.BlockSpec((tc, D_MODEL), lambda i: (i, 0)),
                  pl.BlockSpec((1, N_MOD, D_MODEL), lambda i: (i // per_b, 0, 0)),
                  pl.BlockSpec((1, D_MODEL), lambda i: (0, 0))],
        out_specs=pl.BlockSpec((tc, D_MODEL), lambda i: (i, 0)),
        compiler_params=pltpu.CompilerParams(dimension_semantics=("arbitrary",),
                                             vmem_limit_bytes=VMEM_LIMIT),
        name="combine",
    )(ytm, ytm, ytm, ytm, x1, mod3, fg)


def _expansion_matrices():
    r = (jnp.arange(2 * LANES) % LANES)[:, None]
    out64, out128 = [], []
    for d in range(2):
        l64 = jnp.arange(D_SSD)[None, :]
        l128 = jnp.arange(HEADS * LANES)[None, :]
        out64.append((l64 // HEAD_DIM == r - HEADS * d).astype(BF16))
        out128.append((l128 // LANES == r - HEADS * d).astype(BF16))
    return jnp.stack(out64), jnp.stack(out128)


def _pad_lanes(v):
    return jnp.pad(v, [(0, 0)] * (v.ndim - 1) + [(0, LANES - v.shape[-1])])


def kernel(x, c, ctx, c_ctx, w_mod, b_mod, norm1_g, w_in, ssd_conv_w, ssd_conv_b, ssd_dt_bias, ssd_a_log,
           ssd_d, ssd_norm_g, sc_conv_w, w_out, norm2_g, w_router, b_router, w_gate_up, b_gate_up, w_down,
           b_down, final_g):
    bsz = x.shape[0]
    n_tok = bsz * SEQ
    n_assign = n_tok * TOP_K
    n_blocks = n_assign // MOE_BM + N_EXPERTS
    n_rows = n_blocks * MOE_BM
    li = 0

    cvec = jnp.concatenate([c, c_ctx[None, :], jnp.zeros((7, D_MODEL), F32)], axis=0)
    mod3 = _mod(cvec, w_mod[li], b_mod[li][None, :]).reshape(bsz + 8, N_MOD, D_MODEL)

    w = w_in[li]
    wz = w[:, Z0:X0].astype(BF16)
    wxbc = w[:, X0:DT0].astype(BF16)
    wdt = _pad_lanes(w[:, DT0:SC0]).astype(BF16)
    wb = w[:, SC0:SC0 + D_SC].astype(BF16)
    wc = w[:, SC0 + D_SC:SC0 + 2 * D_SC].astype(BF16)
    wu = w[:, SC0 + 2 * D_SC:].astype(BF16)
    g1 = norm1_g[li][None, :]
    cw = ssd_conv_w[li]
    cb = ssd_conv_b[li][None, :]
    dtb = _pad_lanes(ssd_dt_bias[li].reshape(1, 2 * HEADS))
    alog = _pad_lanes(ssd_a_log[li].reshape(1, 2 * HEADS))
    e64, e128 = _expansion_matrices()

    h0 = _ctx_states(ctx, mod3, g1, wxbc[:, :XB_W], wdt, cw[:, :XB_W], cb[:, :XB_W], dtb, alog, e64)

    x2 = x.reshape(n_tok, D_MODEL)
    z, xbc, dtr, scb, v = _inproj(x2, mod3, g1, wz, wxbc, wdt, wb, wc, wu)

    dsk = jnp.repeat(ssd_d[li], HEAD_DIM)[None, :]
    yssd = _ssd(xbc.reshape(bsz, SEQ, XBC_W), z.reshape(bsz, SEQ, D_SSD), dtr.reshape(bsz, SEQ, LANES), h0,
                cw, cb, dtb, alog, dsk, ssd_norm_g[li][None, :], e64, e128)

    wo = w_out[li].astype(BF16)
    g2 = norm2_g[li][None, :]
    x1, lgt = _outproj(x2, yssd.reshape(n_tok, D_SSD), scb, v, mod3, sc_conv_w[li], wo[:D_SSD], wo[D_SSD:],
                       g2, w_router[li].T, b_router[li][:, None])

    dest_t, gate_t, idx_t, meta = _route(lgt, n_blocks)
    dest_flat = dest_t.T.reshape(n_assign)
    cnt = meta[0, :N_EXPERTS]
    start = meta[1, :N_EXPERTS]
    blk_exp = meta[2, :n_blocks]
    n_used = meta[3, :1]

    meta_rows = lax.bitcast_convert_type(_pad_lanes(jnp.concatenate(
        [idx_t.T.astype(F32), gate_t.T, jnp.arange(n_tok, dtype=F32)[:, None]], axis=1)), jnp.uint32)
    pad_meta = lax.bitcast_convert_type(_pad_lanes(jnp.concatenate(
        [jnp.full((MOE_BM, TOP_K), -1.0, F32), jnp.zeros((MOE_BM, TOP_K), F32),
         (TOP_K * n_tok + jnp.arange(MOE_BM, dtype=F32))[:, None]], axis=1)), jnp.uint32)
    zsrc = jnp.concatenate([jnp.zeros((MOE_BM, PACK_W), jnp.uint32), pad_meta,
                            jnp.zeros((MOE_BM, D_MODEL - PACK_W - LANES), jnp.uint32)],
                           axis=1).reshape(MOE_BM, SUBLANES, LANES)

    xs = _dispatch(dest_flat, cnt, start, n_used, x1, meta_rows, mod3, g2, zsrc, n_rows)
    ytm = _experts(blk_exp, n_used, xs, w_gate_up[li].astype(BF16), b_gate_up[li][:, None, :],
                   w_down[li].astype(BF16), b_down[li][:, None, :], n_tok)
    out = _combine(ytm, x1, mod3, final_g[None, :])
    return out.reshape(bsz, SEQ, D_MODEL)
```

```python
import functools

import jax
import jax.numpy as jnp
from jax import lax
from jax.experimental import pallas as pl
from jax.experimental.pallas import tpu as pltpu

F32 = jnp.float32
BF16 = jnp.bfloat16
HIGHEST = lax.Precision.HIGHEST

D_MODEL = 1024
SEQ = 2048
CTX_LEN = 256
GRID_W = 64
D_SSD = 1024
D_SC = 1024
HEAD_DIM = 64
HEADS = 16
GROUPS = 2
STATE = 128
CHUNK = 128
N_EXPERTS = 32
TOP_K = 4
D_FF = 1024
SWIGLU_LIMIT = 7.0
SWIGLU_ALPHA = 1.702
NORM_EPS = 1e-6
N_MOD = 6
XBC_W = D_SSD + 2 * GROUPS * STATE
XB_W = D_SSD + GROUPS * STATE
LANES = 128

Z0 = 0
X0 = Z0 + D_SSD
B0 = X0 + D_SSD
C0 = B0 + GROUPS * STATE
DT0 = C0 + GROUPS * STATE
SC0 = DT0 + 2 * HEADS

TOK_TILE = 512
MOE_BM = 256
RT_TILE = 512
DISP_TILE = 256
COMB_TILE = 512
SUBLANES = 8
PACK_W = D_MODEL // 2
META_IDX = 0
META_GATE = TOP_K
META_TOK = 2 * TOP_K
VMEM_LIMIT = 56 * 1024 * 1024


def _silu(v):
    return v * jax.nn.sigmoid(v)


def _softplus(v):
    return jnp.maximum(v, 0.0) + jnp.log1p(jnp.exp(-jnp.abs(v)))


def _rms(v):
    return v * lax.rsqrt(jnp.mean(v * v, axis=-1, keepdims=True) + NORM_EPS)


def _dot(a, b):
    return jnp.dot(a, b, preferred_element_type=F32)


def _expand2(v, e2):
    hi = v.astype(BF16)
    lo = (v - hi.astype(F32)).astype(BF16)
    return _dot(jnp.concatenate([hi, lo], axis=1), e2)


def _mod_kernel(c_ref, w_ref, b_ref, o_ref):
    o_ref[...] = jnp.dot(_silu(c_ref[...]), w_ref[...], precision=HIGHEST,
                         preferred_element_type=F32) + b_ref[...]


def _mod(cvec, w_mod, b_mod):
    rows = cvec.shape[0]
    n = w_mod.shape[1]
    tn = 1536
    return pl.pallas_call(
        _mod_kernel,
        out_shape=jax.ShapeDtypeStruct((rows, n), F32),
        grid=(n // tn,),
        in_specs=[pl.BlockSpec((rows, D_MODEL), lambda j: (0, 0)),
                  pl.BlockSpec((D_MODEL, tn), lambda j: (0, j)),
                  pl.BlockSpec((1, tn), lambda j: (0, j))],
        out_specs=pl.BlockSpec((rows, tn), lambda j: (0, j)),
        compiler_params=pltpu.CompilerParams(dimension_semantics=("arbitrary",),
                                             vmem_limit_bytes=VMEM_LIMIT),
        name="mod",
    )(cvec, w_mod, b_mod)


def _ctx_kernel(ctx_ref, mod_ref, g1_ref, wxb_ref, wdt_ref, cw_ref, cb_ref, dtb_ref, alog_ref, e64_ref,
                h0_ref):
    L = CTX_LEN
    m = mod_ref[0]
    hc = _rms(ctx_ref[0]) * g1_ref[...] * (1.0 + m[1:2]) + m[0:1]
    hb = hc.astype(BF16)
    pxb = _dot(hb, wxb_ref[...])
    dtr = _dot(hb, wdt_ref[...])
    rowi = lax.broadcasted_iota(jnp.int32, (L, XB_W), 0)
    dn = jnp.where(rowi == 0, 0.0, pltpu.roll(pxb, 1, 0))
    up = jnp.where(rowi == L - 1, 0.0, pltpu.roll(pxb, L - 1, 0))
    cw = cw_ref[...]
    xb = _silu(cw[0:1] * dn + cw[1:2] * pxb + cw[2:3] * up + cb_ref[...])
    xs = xb[:, :D_SSD]
    bm = xb[:, D_SSD:].astype(BF16)
    dt = _softplus(dtr + dtb_ref[...])
    da = dt * (-jnp.exp(alog_ref[...]))
    ri = lax.broadcasted_iota(jnp.int32, (L, L), 0)
    ci = lax.broadcasted_iota(jnp.int32, (L, L), 1)
    for d in range(2):
        tri = (ci <= ri) if d == 0 else (ci >= ri)
        cum = jnp.dot(tri.astype(F32), da, precision=HIGHEST, preferred_element_type=F32)
        last = cum[L - 1:L] if d == 0 else cum[0:1]
        w_e = _expand2(jnp.exp(last - cum) * dt, e64_ref[d])
        xw = (xs * w_e).astype(BF16)
        for g in range(GROUPS):
            gw = D_SSD // GROUPS
            st = lax.dot_general(bm[:, g * STATE:(g + 1) * STATE], xw[:, g * gw:(g + 1) * gw],
                                 (((0,), (0,)), ((), ())), preferred_element_type=F32)
            h0_ref[0, d, :, g * gw:(g + 1) * gw] = st


def _ctx_states(ctx, mod3, g1, wxb, wdt, cw, cb, dtb, alog, e64):
    bsz = ctx.shape[0]
    mod_row = bsz
    const = lambda *shape: pl.BlockSpec(shape, lambda b: (0,) * len(shape))
    return pl.pallas_call(
        _ctx_kernel,
        out_shape=jax.ShapeDtypeStruct((bsz, 2, STATE, D_SSD), F32),
        grid=(bsz,),
        in_specs=[pl.BlockSpec((1, CTX_LEN, D_MODEL), lambda b: (b, 0, 0)),
                  pl.BlockSpec((1, N_MOD, D_MODEL), lambda b: (mod_row, 0, 0)),
                  const(1, D_MODEL), const(D_MODEL, XB_W), const(D_MODEL, LANES),
                  const(3, XB_W), const(1, XB_W), const(1, LANES), const(1, LANES),
                  const(2, 2 * LANES, D_SSD)],
        out_specs=pl.BlockSpec((1, 2, STATE, D_SSD), lambda b: (b, 0, 0, 0)),
        compiler_params=pltpu.CompilerParams(dimension_semantics=("arbitrary",),
                                             vmem_limit_bytes=VMEM_LIMIT),
        name="ctx_states",
    )(ctx, mod3, g1, wxb, wdt, cw, cb, dtb, alog, e64)


def _inproj_kernel(x_ref, mod_ref, g1_ref, wz_ref, wxbc_ref, wdt_ref, wb_ref, wc_ref, wu_ref,
                   z_ref, xbc_ref, dt_ref, scb_ref, v_ref):
    m = mod_ref[0]
    hx = _rms(x_ref[...]) * g1_ref[...] * (1.0 + m[1:2]) + m[0:1]
    hb = hx.astype(BF16)
    z_ref[...] = _dot(hb, wz_ref[...]).astype(BF16)
    xbc_ref[...] = _dot(hb, wxbc_ref[...]).astype(BF16)
    dt_ref[...] = _dot(hb, wdt_ref[...])
    scb_ref[...] = _dot(hb, wb_ref[...]).astype(BF16)
    v_ref[...] = (_dot(hb, wc_ref[...]) * _dot(hb, wu_ref[...])).astype(BF16)


def _inproj(x2, mod3, g1, wz, wxbc, wdt, wb, wc, wu):
    t = x2.shape[0]
    tm = TOK_TILE
    per_b = SEQ // tm
    const = lambda *shape: pl.BlockSpec(shape, lambda i: (0,) * len(shape))
    tile = lambda w: pl.BlockSpec((tm, w), lambda i: (i, 0))
    return pl.pallas_call(
        _inproj_kernel,
        out_shape=(jax.ShapeDtypeStruct((t, D_SSD), BF16), jax.ShapeDtypeStruct((t, XBC_W), BF16),
                   jax.ShapeDtypeStruct((t, LANES), F32), jax.ShapeDtypeStruct((t, D_SC), BF16),
                   jax.ShapeDtypeStruct((t, D_SC), BF16)),
        grid=(t // tm,),
        in_specs=[tile(D_MODEL),
                  pl.BlockSpec((1, N_MOD, D_MODEL), lambda i: (i // per_b, 0, 0)),
                  const(1, D_MODEL), const(D_MODEL, D_SSD), const(D_MODEL, XBC_W), const(D_MODEL, LANES),
                  const(D_MODEL, D_SC), const(D_MODEL, D_SC), const(D_MODEL, D_SC)],
        out_specs=(tile(D_SSD), tile(XBC_W), tile(LANES), tile(D_SC), tile(D_SC)),
        compiler_params=pltpu.CompilerParams(dimension_semantics=("arbitrary",),
                                             vmem_limit_bytes=VMEM_LIMIT),
        name="inproj",
    )(x2, mod3, g1, wz, wxbc, wdt, wb, wc, wu)


def _ssd_kernel(xbc_ref, z_ref, dt_ref, h0_ref, cw_ref, cb_ref, dtb_ref, alog_ref, dsk_ref, g_ref,
                e64_ref, e128_ref, o_ref, xc_ref, y_ref, s_ref):
    Q = CHUNK
    nck = SEQ // Q
    gw = D_SSD // GROUPS

    rowi = lax.broadcasted_iota(jnp.int32, (SUBLANES, XBC_W), 0)

    def conv_body(c, carry):
        r0 = pl.multiple_of(c * Q, Q)
        main = xbc_ref[0, pl.ds(r0, Q), :].astype(F32)
        pstart = pl.multiple_of(jnp.maximum(r0 - 16, 0), 16)
        nstart = pl.multiple_of(jnp.minimum(r0 + Q, SEQ - 16), 16)
        prev = xbc_ref[0, pl.ds(pstart, 16), :].astype(F32)[15:16]
        nxt = xbc_ref[0, pl.ds(nstart, 16), :].astype(F32)[0:1]
        prev = jnp.where(c > 0, prev, 0.0)
        nxt = jnp.where(c < nck - 1, nxt, 0.0)
        dn = pltpu.roll(main, 1, 0)
        up = pltpu.roll(main, Q - 1, 0)
        dn = jnp.concatenate([jnp.where(rowi == 0, prev, dn[0:SUBLANES]), dn[SUBLANES:]], axis=0)
        up = jnp.concatenate([up[:Q - SUBLANES], jnp.where(rowi == SUBLANES - 1, nxt, up[Q - SUBLANES:])], axis=0)
        cw = cw_ref[...]
        conv = cw[0:1] * dn + cw[1:2] * main + cw[2:3] * up + cb_ref[...]
        xc_ref[pl.ds(r0, Q), :] = _silu(conv).astype(BF16)
        return carry

    lax.fori_loop(0, nck, conv_body, 0)

    ri = lax.broadcasted_iota(jnp.int32, (Q, Q), 0)
    ci = lax.broadcasted_iota(jnp.int32, (Q, Q), 1)
    lane = lax.broadcasted_iota(jnp.int32, (Q, LANES), 1)
    a_neg = -jnp.exp(alog_ref[...])

    def chunk(c, d, first):
        r0 = pl.multiple_of(c * Q, Q)
        rows = pl.ds(r0, Q)
        xs_b = xc_ref[rows, 0:D_SSD]
        xs = xs_b.astype(F32)
        bm = xc_ref[rows, D_SSD:D_SSD + GROUPS * STATE]
        cm = xc_ref[rows, D_SSD + GROUPS * STATE:XBC_W]
        dt = _softplus(dt_ref[0, rows, :] + dtb_ref[...])
        da = dt * a_neg
        tri = (ci <= ri) if d == 0 else (ci >= ri)
        p0 = da.astype(BF16)
        r1 = da - p0.astype(F32)
        p1 = r1.astype(BF16)
        p2 = (r1 - p1.astype(F32)).astype(BF16)
        tri_b = jnp.where(tri, 1.0, 0.0).astype(BF16)
        cum = _dot(jnp.concatenate([tri_b, tri_b, tri_b], axis=1),
                   jnp.concatenate([p0, p1, p2], axis=0))
        sub_t = (cum - jnp.log(dt)).T
        last = cum[Q - 1:Q] if d == 0 else cum[0:1]
        ecum_e = _expand2(jnp.exp(cum), e64_ref[d])
        w_e = _expand2(jnp.exp(last - cum) * dt, e64_ref[d])
        colb = _expand2(cum, e128_ref[d])
        decay_e = ecum_e[Q - 1:Q] if d == 0 else ecum_e[0:1]

        gmat = [lax.dot_general(cm[:, g * STATE:(g + 1) * STATE], bm[:, g * STATE:(g + 1) * STATE],
                                (((1,), (1,)), ((), ())), preferred_element_type=F32)
                for g in range(GROUPS)]
        zero_b = jnp.zeros((Q, LANES), BF16)
        y_parts = []
        for p in range(HEADS // 2):
            g = (2 * p) // (HEADS // GROUPS)
            ms = []
            for hh in (2 * p, 2 * p + 1):
                seg = colb[:, hh * LANES:(hh + 1) * LANES] - sub_t[HEADS * d + hh:HEADS * d + hh + 1, :]
                ms.append((jnp.where(tri, jnp.exp(seg), 0.0) * gmat[g]).astype(BF16))
            mcat = jnp.concatenate(ms, axis=1)
            xp = xs_b[:, p * LANES:(p + 1) * LANES]
            rhs = jnp.concatenate([jnp.where(lane < HEAD_DIM, xp, zero_b),
                                   jnp.where(lane >= HEAD_DIM, xp, zero_b)], axis=0)
            y_parts.append(_dot(mcat, rhs))
        y_diag = jnp.concatenate(y_parts, axis=1)

        s_old = s_ref[d]
        s_bf = s_old.astype(BF16)
        y_off = jnp.concatenate(
            [_dot(cm[:, g * STATE:(g + 1) * STATE], s_bf[:, g * gw:(g + 1) * gw]) for g in range(GROUPS)],
            axis=1)
        y = y_diag + y_off * ecum_e

        xw = (xs * w_e).astype(BF16)
        upd = jnp.concatenate(
            [lax.dot_general(bm[:, g * STATE:(g + 1) * STATE], xw[:, g * gw:(g + 1) * gw],
                             (((0,), (0,)), ((), ())), preferred_element_type=F32) for g in range(GROUPS)],
            axis=1)
        s_ref[d] = s_old * decay_e + upd

        if first:
            y_ref[rows, :] = y + dsk_ref[...] * xs
        else:
            tot = y_ref[rows, :] + y
            zz = z_ref[0, rows, :].astype(F32)
            gz = tot * _silu(zz)
            outs = []
            for g in range(GROUPS):
                gg = gz[:, g * gw:(g + 1) * gw]
                outs.append(gg * lax.rsqrt(jnp.mean(gg * gg, axis=-1, keepdims=True) + NORM_EPS))
            o_ref[0, rows, :] = (jnp.concatenate(outs, axis=1) * g_ref[...]).astype(BF16)

    s_ref[...] = h0_ref[0]

    def first_half(i, carry):
        chunk(i, 0, True)
        chunk(nck - 1 - i, 1, True)
        return carry

    def second_half(i, carry):
        chunk(i, 0, False)
        chunk(nck - 1 - i, 1, False)
        return carry

    lax.fori_loop(0, nck // 2, first_half, 0)
    lax.fori_loop(nck // 2, nck, second_half, 0)


def _ssd(xbc3, z3, dt3, h0, cw, cb, dtb, alog, dsk, g, e64, e128):
    bsz = xbc3.shape[0]
    const = lambda *shape: pl.BlockSpec(shape, lambda b: (0,) * len(shape))
    seq = lambda w: pl.BlockSpec((1, SEQ, w), lambda b: (b, 0, 0))
    return pl.pallas_call(
        _ssd_kernel,
        out_shape=jax.ShapeDtypeStruct((bsz, SEQ, D_SSD), BF16),
        grid=(bsz,),
        in_specs=[seq(XBC_W), seq(D_SSD), seq(LANES),
                  pl.BlockSpec((1, 2, STATE, D_SSD), lambda b: (b, 0, 0, 0)),
                  const(3, XBC_W), const(1, XBC_W), const(1, LANES), const(1, LANES),
                  const(1, D_SSD), const(1, D_SSD), const(2, 2 * LANES, D_SSD),
                  const(2, 2 * LANES, HEADS * LANES)],
        out_specs=seq(D_SSD),
        scratch_shapes=[pltpu.VMEM((SEQ, XBC_W), BF16), pltpu.VMEM((SEQ, D_SSD), F32),
                        pltpu.VMEM((2, STATE, D_SSD), F32)],
        compiler_params=pltpu.CompilerParams(dimension_semantics=("arbitrary",),
                                             vmem_limit_bytes=VMEM_LIMIT),
        name="ssd",
    )(xbc3, z3, dt3, h0, cw, cb, dtb, alog, dsk, g, e64, e128)


def _outproj_kernel(x_ref, yssd_ref, scb_ref, v_ref, vp_ref, vn_ref, mod_ref, scw_ref, wo1_ref, wo2_ref,
                    g2_ref, wrt_ref, br_ref, x1_ref, lg_ref):
    tm = TOK_TILE
    per_b = SEQ // tm
    i = pl.program_id(0)
    first = (i % per_b) == 0
    last = (i % per_b) == per_b - 1
    m = mod_ref[0]
    v = v_ref[...].astype(F32)
    vp = jnp.where(first, 0.0, vp_ref[...].astype(F32))
    vn = jnp.where(last, 0.0, vn_ref[...].astype(F32))
    dn = jnp.concatenate([vp, v[:tm - GRID_W]], axis=0)
    up = jnp.concatenate([v[GRID_W:], vn], axis=0)
    scw = scw_ref[...]
    ysc = scb_ref[...].astype(F32) * (scw[0:1] * dn + scw[1:2] * v + scw[2:3] * up)
    out = _dot(yssd_ref[...], wo1_ref[...]) + _dot(ysc.astype(BF16), wo2_ref[...])
    x1 = x_ref[...] + m[2:3] * out
    x1_ref[...] = x1
    h2 = _rms(x1) * g2_ref[...] * (1.0 + m[4:5]) + m[3:4]
    lg_ref[...] = lax.dot_general(wrt_ref[...], h2, (((1,), (1,)), ((), ())), precision=HIGHEST,
                                  preferred_element_type=F32) + br_ref[...]


def _outproj(x2, yssd, scb, v, mod3, scw, wo1, wo2, g2, wrt, br):
    t = x2.shape[0]
    tm = TOK_TILE
    per_b = SEQ // tm
    r = tm // GRID_W
    nrow = t // GRID_W
    const = lambda *shape: pl.BlockSpec(shape, lambda i: (0,) * len(shape))
    tile = lambda w: pl.BlockSpec((tm, w), lambda i: (i, 0))
    return pl.pallas_call(
        _outproj_kernel,
        out_shape=(jax.ShapeDtypeStruct((t, D_MODEL), F32), jax.ShapeDtypeStruct((N_EXPERTS, t), F32)),
        grid=(t // tm,),
        in_specs=[tile(D_MODEL), tile(D_SSD), tile(D_SC), tile(D_SC),
                  pl.BlockSpec((GRID_W, D_SC), lambda i: (jnp.maximum(i * r - 1, 0), 0)),
                  pl.BlockSpec((GRID_W, D_SC), lambda i: (jnp.minimum((i + 1) * r, nrow - 1), 0)),
                  pl.BlockSpec((1, N_MOD, D_MODEL), lambda i: (i // per_b, 0, 0)),
                  const(3, D_SC), const(D_SSD, D_MODEL), const(D_SC, D_MODEL), const(1, D_MODEL),
                  const(N_EXPERTS, D_MODEL), const(N_EXPERTS, 1)],
        out_specs=(tile(D_MODEL), pl.BlockSpec((N_EXPERTS, tm), lambda i: (0, i))),
        compiler_params=pltpu.CompilerParams(dimension_semantics=("arbitrary",),
                                             vmem_limit_bytes=VMEM_LIMIT),
        name="outproj",
    )(x2, yssd, scb, v, v, v, mod3, scw, wo1, wo2, g2, wrt, br)


def _route_kernel(lg_ref, dest_ref, gate_ref, idx_ref, meta_ref, rank_ref, carry_ref, *, n_tok, n_blocks):
    tt = RT_TILE
    ne = N_EXPERTS
    eio = lax.broadcasted_iota(jnp.int32, (ne, tt), 0)
    si = lax.broadcasted_iota(jnp.int32, (tt, tt), 0)
    ti = lax.broadcasted_iota(jnp.int32, (tt, tt), 1)
    before = (si < ti).astype(BF16)
    carry_ref[...] = jnp.zeros_like(carry_ref)

    def tile_body(j, c):
        t0 = pl.multiple_of(j * tt, tt)
        l = lg_ref[:, pl.ds(t0, tt)]
        onehot = jnp.zeros((ne, tt), F32)
        tops, sels = [], []
        for _ in range(TOP_K):
            mx = jnp.max(l, axis=0, keepdims=True)
            idx = jnp.min(jnp.where(l == mx, eio, ne), axis=0, keepdims=True)
            sel = eio == idx
            l = jnp.where(sel, -jnp.inf, l)
            onehot = onehot + sel.astype(F32)
            tops.append(mx)
            sels.append(sel)
            idx_ref[pl.ds(len(tops) - 1, 1), pl.ds(t0, tt)] = idx
        ex = [jnp.exp(tv - tops[0]) for tv in tops]
        den = ex[0] + ex[1] + ex[2] + ex[3]
        prefix = _dot(onehot.astype(BF16), before) + carry_ref[:, 0:1]
        for k in range(TOP_K):
            gate_ref[pl.ds(k, 1), pl.ds(t0, tt)] = ex[k] / den
            rk = jnp.sum(jnp.where(sels[k], prefix, 0.0), axis=0, keepdims=True)
            rank_ref[pl.ds(k, 1), pl.ds(t0, tt)] = rk.astype(jnp.int32)
        carry_ref[...] = carry_ref[...] + jnp.sum(onehot, axis=1, keepdims=True)
        return c

    lax.fori_loop(0, n_tok // tt, tile_body, 0)

    counts = carry_ref[...]
    padded = jnp.floor((counts + (MOE_BM - 1)) * (1.0 / MOE_BM)) * MOE_BM
    er = lax.broadcasted_iota(jnp.int32, (ne, ne), 0)
    ec = lax.broadcasted_iota(jnp.int32, (ne, ne), 1)
    pad_start = jnp.dot((ec < er).astype(F32), padded, precision=HIGHEST, preferred_element_type=F32)
    pad_end = pad_start + padded

    def dest_body(j, c):
        t0 = pl.multiple_of(j * tt, tt)
        for k in range(TOP_K):
            idx = idx_ref[pl.ds(k, 1), pl.ds(t0, tt)]
            base = jnp.sum(jnp.where(eio == idx, pad_start[:, 0:1], 0.0), axis=0, keepdims=True)
            dest_ref[pl.ds(k, 1), pl.ds(t0, tt)] = base.astype(jnp.int32) + rank_ref[pl.ds(k, 1), pl.ds(t0, tt)]
        return c

    lax.fori_loop(0, n_tok // tt, dest_body, 0)

    width = meta_ref.shape[1]
    sub = lax.broadcasted_iota(jnp.int32, (ne, width), 0)
    lan = lax.broadcasted_iota(jnp.int32, (ne, width), 1)
    diag = sub == lan
    cnt_row = jnp.sum(jnp.where(diag, counts[:, 0:1], 0.0), axis=0, keepdims=True)
    start_row = jnp.sum(jnp.where(diag, pad_start[:, 0:1], 0.0), axis=0, keepdims=True)
    blk_start = (lan * MOE_BM).astype(F32)
    blk_exp = jnp.sum((pad_end[:, 0:1] <= blk_start).astype(F32), axis=0, keepdims=True)
    blk_exp = jnp.minimum(blk_exp, float(ne - 1))
    used = jnp.sum(padded[:, 0:1], axis=0, keepdims=True) * (1.0 / MOE_BM)
    meta_ref[0:1, :] = cnt_row.astype(jnp.int32)
    meta_ref[1:2, :] = start_row.astype(jnp.int32)
    meta_ref[2:3, :] = blk_exp.astype(jnp.int32)
    meta_ref[3:4, :] = jnp.broadcast_to(used, (1, width)).astype(jnp.int32)
    meta_ref[4:8, :] = jnp.zeros((4, width), jnp.int32)


def _route(lgt, n_blocks):
    ne, n_tok = lgt.shape
    width = -(-n_blocks // LANES) * LANES
    full = lambda *shape: pl.BlockSpec(shape, lambda: (0,) * len(shape))
    return pl.pallas_call(
        functools.partial(_route_kernel, n_tok=n_tok, n_blocks=n_blocks),
        out_shape=(jax.ShapeDtypeStruct((TOP_K, n_tok), jnp.int32),
                   jax.ShapeDtypeStruct((TOP_K, n_tok), F32),
                   jax.ShapeDtypeStruct((TOP_K, n_tok), jnp.int32),
                   jax.ShapeDtypeStruct((8, width), jnp.int32)),
        in_specs=[full(ne, n_tok)],
        out_specs=(full(TOP_K, n_tok), full(TOP_K, n_tok), full(TOP_K, n_tok), full(8, width)),
        scratch_shapes=[pltpu.VMEM((TOP_K, n_tok), jnp.int32), pltpu.VMEM((ne, LANES), F32)],
        compiler_params=pltpu.CompilerParams(vmem_limit_bytes=VMEM_LIMIT),
        name="route",
    )(lgt)


def _dispatch_kernel(dest_ref, cnt_ref, start_ref, nu_ref, x1_ref, meta_ref, mod_ref, g2_ref, zsrc_ref, xs_ref,
                     hbuf, sem, zsem):
    i = pl.program_id(0)
    n = pl.num_programs(0)
    tl = DISP_TILE
    slot = i % 2
    nb = xs_ref.shape[0] // MOE_BM

    def zero_block(b):
        return pltpu.make_async_copy(zsrc_ref, xs_ref.at[pl.ds(b * MOE_BM, MOE_BM)], zsem)

    @pl.when(i == 0)
    def _():
        def start_e(e, c):
            @pl.when(cnt_ref[e] > 0)
            def _():
                zero_block((start_ref[e] + cnt_ref[e] - 1) // MOE_BM).start()
            return c

        def wait_e(e, c):
            @pl.when(cnt_ref[e] > 0)
            def _():
                zero_block(0).wait()
            return c

        def start_t(b, c):
            zero_block(b).start()
            return c

        def wait_t(b, c):
            zero_block(0).wait()
            return c

        lax.fori_loop(0, N_EXPERTS, start_e, 0)
        lax.fori_loop(nu_ref[0], nb, start_t, 0)
        lax.fori_loop(0, N_EXPERTS, wait_e, 0)
        lax.fori_loop(nu_ref[0], nb, wait_t, 0)

    m = mod_ref[0]
    h2 = _rms(x1_ref[...]) * g2_ref[...] * (1.0 + m[4:5]) + m[3:4]
    lo = pltpu.bitcast(h2[:, :PACK_W].astype(BF16).astype(F32), jnp.uint32) >> 16
    hi = pltpu.bitcast(h2[:, PACK_W:].astype(BF16).astype(F32), jnp.uint32) & jnp.uint32(0xFFFF0000)
    row = jnp.concatenate([lo | hi, meta_ref[...], jnp.zeros((tl, D_MODEL - PACK_W - LANES), jnp.uint32)], axis=1)
    hbuf[slot] = row.reshape(tl, SUBLANES, LANES)

    def per_tok(t, c):
        tok = i * tl + t
        for k in range(TOP_K):
            d = dest_ref[tok * TOP_K + k]
            pltpu.make_async_copy(hbuf.at[slot, t], xs_ref.at[d], sem.at[slot]).start()
        return c

    lax.fori_loop(0, tl, per_tok, 0)

    def wait_slot(sl):
        for _ in range(TOP_K):
            pltpu.make_async_copy(hbuf.at[sl], xs_ref.at[pl.ds(0, tl)], sem.at[sl]).wait()

    @pl.when(i > 0)
    def _():
        wait_slot(1 - slot)

    @pl.when(i == n - 1)
    def _():
        wait_slot(slot)


def _dispatch(dest_flat, cnt, start, n_used, x1, meta_rows, mod3, g2, zsrc, n_rows):
    n_tok = x1.shape[0]
    tl = DISP_TILE
    per_b = SEQ // tl
    return pl.pallas_call(
        _dispatch_kernel,
        out_shape=jax.ShapeDtypeStruct((n_rows, SUBLANES, LANES), jnp.uint32),
        grid_spec=pltpu.PrefetchScalarGridSpec(
            num_scalar_prefetch=4,
            grid=(n_tok // tl,),
            in_specs=[pl.BlockSpec((tl, D_MODEL), lambda i, *_: (i, 0)),
                      pl.BlockSpec((tl, LANES), lambda i, *_: (i, 0)),
                      pl.BlockSpec((1, N_MOD, D_MODEL), lambda i, *_: (i // per_b, 0, 0)),
                      pl.BlockSpec((1, D_MODEL), lambda i, *_: (0, 0)),
                      pl.BlockSpec((MOE_BM, SUBLANES, LANES), lambda i, *_: (0, 0, 0))],
            out_specs=pl.BlockSpec(memory_space=pl.ANY),
            scratch_shapes=[pltpu.VMEM((2, tl, SUBLANES, LANES), jnp.uint32),
                            pltpu.SemaphoreType.DMA((2,)), pltpu.SemaphoreType.DMA]),
        compiler_params=pltpu.CompilerParams(dimension_semantics=("arbitrary",),
                                             vmem_limit_bytes=VMEM_LIMIT),
        name="dispatch",
    )(dest_flat, cnt, start, n_used, x1, meta_rows, mod3, g2, zsrc)


def _expert_kernel(be_ref, nu_ref, xs_ref, wgu_hbm, bgu_ref, wd_hbm, bd_ref, zero_ref, ytm_ref,
                   big, idv, ids, wgu_raw, wd_raw, wgu_bf, wd_bf, sc_sem, id_sem, z_sem, w_sem, *, n_tok):
    j = pl.program_id(0)
    nbk = pl.num_programs(0) - 1
    nu = nu_ref[0]
    slot = j % 2
    prev = 1 - slot

    def weight_copies(e):
        return (pltpu.make_async_copy(wgu_hbm.at[e], wgu_raw, w_sem.at[0]),
                pltpu.make_async_copy(wd_hbm.at[e], wd_raw, w_sem.at[1]))

    @pl.when(j == 0)
    def _():
        cp = pltpu.make_async_copy(zero_ref, ytm_ref.at[pl.ds(TOP_K * n_tok, MOE_BM)], z_sem)
        cp.start()
        for w in weight_copies(be_ref[0]):
            w.start()
        cp.wait()

    e_now = be_ref[jnp.minimum(j, nbk - 1)]
    new_expert = jnp.logical_or(j == 0, be_ref[jnp.maximum(jnp.minimum(j, nbk - 1) - 1, 0)] != e_now)

    @pl.when(jnp.logical_and(j < nu, new_expert))
    def _():
        for w in weight_copies(e_now):
            w.wait()
        n_cc = 8
        for c in range(n_cc):
            cc = slice(c * (2 * D_FF // n_cc), (c + 1) * (2 * D_FF // n_cc))
            wgu_bf[:, cc] = wgu_raw[:, cc].astype(BF16)
        for c in range(n_cc // 2):
            cc = slice(c * (2 * D_MODEL // n_cc), (c + 1) * (2 * D_MODEL // n_cc))
            wd_bf[:, cc] = wd_raw[:, cc].astype(BF16)
        j_next = lax.while_loop(lambda t: jnp.logical_and(t < nu, be_ref[jnp.minimum(t, nbk - 1)] == e_now),
                                lambda t: t + 1, j + 1)

        @pl.when(j_next < nu)
        def _():
            for w in weight_copies(be_ref[jnp.minimum(j_next, nbk - 1)]):
                w.start()

    n_pc = 4
    pw1 = D_FF // n_pc
    pw2 = D_MODEL // n_pc
    n_lt = D_MODEL // LANES
    spb = MOE_BM // SUBLANES
    groups = [48] * n_pc + [16] * n_pc
    assert sum(groups) == MOE_BM

    def slab_store(unit, col0, val):
        for s in range(val.shape[1] // LANES):
            t0 = unit * MOE_BM + (col0 // LANES + s) * spb
            big[pl.ds(t0, spb)] = val[:, s * LANES:(s + 1) * LANES].reshape(spb, SUBLANES, LANES)

    def slab_load(unit, row0, n):
        return jnp.concatenate(
            [big[pl.ds(unit * MOE_BM + s * spb + row0 // SUBLANES, n // SUBLANES)].reshape(n, LANES)
             for s in range(n_lt)], axis=1)

    def scatter_group(g):
        lo = sum(groups[:g])
        n = groups[g]
        big[pl.ds(prev * MOE_BM + lo, n)] = slab_load(3 + prev, lo, n).reshape(n, n_lt, LANES)
        for r in range(lo, lo + n):
            pltpu.make_async_copy(big.at[prev * MOE_BM + r], ytm_ref.at[ids[prev, 0, r]],
                                  sc_sem.at[prev]).start(priority=r % 2)

    def scatter_wait(sl):
        pltpu.make_async_copy(big.at[pl.ds(0, MOE_BM)], ytm_ref.at[pl.ds(0, MOE_BM)], sc_sem.at[sl]).wait()

    def compute(with_scatter):
        e_f = be_ref[jnp.minimum(j, nbk - 1)].astype(F32)
        words = xs_ref[...].reshape(MOE_BM, D_MODEL)
        packed = words[:, 0:PACK_W]
        meta = pltpu.bitcast(words[:, PACK_W:PACK_W + LANES], F32)
        xb = jnp.concatenate(
            [pltpu.bitcast(packed << 16, F32).astype(BF16),
             pltpu.bitcast(packed & jnp.uint32(0xFFFF0000), F32).astype(BF16)], axis=1)
        gate = jnp.zeros((MOE_BM, 1), F32)
        kk = jnp.zeros((MOE_BM, 1), F32)
        for k in range(TOP_K):
            mk = meta[:, META_IDX + k:META_IDX + k + 1] == e_f
            gate = gate + jnp.where(mk, meta[:, META_GATE + k:META_GATE + k + 1], 0.0)
            kk = kk + jnp.where(mk, float(k), 0.0)
        row = kk * float(n_tok) + meta[:, META_TOK:META_TOK + 1]
        row_t = jnp.broadcast_to(row, (MOE_BM, LANES)).T
        idv[slot] = row_t[0:8].astype(jnp.int32)
        pltpu.make_async_copy(idv.at[slot], ids.at[slot], id_sem.at[slot]).start()
        for c in range(n_pc):
            if with_scatter:
                scatter_group(c)
            cg = slice(c * pw1, (c + 1) * pw1)
            cl = slice(D_FF + c * pw1, D_FF + (c + 1) * pw1)
            glu = jnp.minimum(_dot(xb, wgu_bf[:, cg]) + bgu_ref[0, :, cg], SWIGLU_LIMIT)
            lin = jnp.clip(_dot(xb, wgu_bf[:, cl]) + bgu_ref[0, :, cl], -SWIGLU_LIMIT, SWIGLU_LIMIT)
            slab_store(2, c * pw1, glu * jax.nn.sigmoid(SWIGLU_ALPHA * glu) * (lin + 1.0))
        act = slab_load(2, 0, MOE_BM).astype(BF16)
        for c in range(n_pc):
            if with_scatter:
                scatter_group(n_pc + c)
            cs = slice(c * pw2, (c + 1) * pw2)
            slab_store(3 + slot, c * pw2, (_dot(act, wd_bf[:, cs]) + bd_ref[0, :, cs]) * gate)

    def ids_wait():
        pltpu.make_async_copy(idv.at[prev], ids.at[prev], id_sem.at[prev]).wait()

    @pl.when(jnp.logical_and(j >= 3, j <= nu))
    def _():
        scatter_wait(prev)

    @pl.when(j == 0)
    def _():
        compute(False)

    @pl.when(jnp.logical_and(j >= 1, j < nu))
    def _():
        ids_wait()
        compute(True)

    @pl.when(j == nu)
    def _():
        ids_wait()
        for g in range(len(groups)):
            scatter_group(g)
        scatter_wait(prev)

    @pl.when(jnp.logical_and(j == nu, j >= 2))
    def _():
        scatter_wait(slot)


def _experts(blk_exp, n_used, xs, wgu, bgu, wd, bd, n_tok):
    n_rows = xs.shape[0]
    nb = n_rows // MOE_BM
    row_blk = lambda j, be, nu: (jnp.minimum(j, nu[0] - 1), 0, 0)
    per_e = lambda j, be, nu: (be[jnp.minimum(j, nb - 1)], 0, 0)
    sub = D_MODEL // LANES
    zero = jnp.zeros((MOE_BM, sub, LANES), F32)
    return pl.pallas_call(
        functools.partial(_expert_kernel, n_tok=n_tok),
        out_shape=jax.ShapeDtypeStruct((TOP_K * n_tok + MOE_BM, sub, LANES), F32),
        grid_spec=pltpu.PrefetchScalarGridSpec(
            num_scalar_prefetch=2,
            grid=(nb + 1,),
            in_specs=[pl.BlockSpec((MOE_BM, SUBLANES, LANES), row_blk),
                      pl.BlockSpec(memory_space=pl.ANY),
                      pl.BlockSpec((1, 1, 2 * D_FF), per_e),
                      pl.BlockSpec(memory_space=pl.ANY),
                      pl.BlockSpec((1, 1, D_MODEL), per_e),
                      pl.BlockSpec((MOE_BM, sub, LANES), lambda j, be, nu: (0, 0, 0))],
            out_specs=pl.BlockSpec(memory_space=pl.ANY),
            scratch_shapes=[pltpu.VMEM((5 * MOE_BM, sub, LANES), F32), pltpu.VMEM((2, 8, MOE_BM), jnp.int32),
                            pltpu.SMEM((2, 8, MOE_BM), jnp.int32),
                            pltpu.VMEM((D_MODEL, 2 * D_FF), F32), pltpu.VMEM((D_FF, D_MODEL), F32),
                            pltpu.VMEM((D_MODEL, 2 * D_FF), BF16), pltpu.VMEM((D_FF, D_MODEL), BF16),
                            pltpu.SemaphoreType.DMA((2,)), pltpu.SemaphoreType.DMA((2,)),
                            pltpu.SemaphoreType.DMA, pltpu.SemaphoreType.DMA((2,))]),
        compiler_params=pltpu.CompilerParams(dimension_semantics=("arbitrary",),
                                             vmem_limit_bytes=VMEM_LIMIT),
        name="experts",
    )(blk_exp, n_used, xs, wgu, bgu, wd, bd, zero)


def _combine_kernel(y0_ref, y1_ref, y2_ref, y3_ref, x1_ref, mod_ref, fg_ref, o_ref):
    m = mod_ref[0]
    moe = ((y0_ref[...] + y1_ref[...]) + (y2_ref[...] + y3_ref[...])).reshape(COMB_TILE, D_MODEL)
    x2 = x1_ref[...] + m[5:6] * moe
    o_ref[...] = _rms(x2) * fg_ref[...]


def _combine(ytm, x1, mod3, fg):
    n_tok = x1.shape[0]
    tc = COMB_TILE
    per_b = SEQ // tc
    nt = n_tok // tc
    slot_spec = lambda k: pl.BlockSpec((tc, D_MODEL // LANES, LANES), lambda i: (i + k * nt, 0, 0))
    return pl.pallas_call(
        _combine_kernel,
        out_shape=jax.ShapeDtypeStruct((n_tok, D_MODEL), F32),
        grid=(nt,),
        in_specs=[slot_spec(0), slot_spec(1), slot_spec(2), slot_spec(3),
                  pl.BlockSpec((tc, D_MODEL), lambda i: (i, 0)),
                  pl.BlockSpec((1, N_MOD, D_MODEL), lambda i: (i // per_b, 0, 0)),
                  pl.BlockSpec((1, D_MODEL), lambda i: (0, 0))],
        out_specs=pl.BlockSpec((tc, D_MODEL), lambda i: (i, 0)),
        compiler_params=pltpu.CompilerParams(dimension_semantics=("arbitrary",),
                                             vmem_limit_bytes=VMEM_LIMIT),
        name="combine",
    )(ytm, ytm, ytm, ytm, x1, mod3, fg)


def _expansion_matrices():
    r = (jnp.arange(2 * LANES) % LANES)[:, None]
    out64, out128 = [], []
    for d in range(2):
        l64 = jnp.arange(D_SSD)[None, :]
        l128 = jnp.arange(HEADS * LANES)[None, :]
        out64.append((l64 // HEAD_DIM == r - HEADS * d).astype(BF16))
        out128.append((l128 // LANES == r - HEADS * d).astype(BF16))
    return jnp.stack(out64), jnp.stack(out128)


def _pad_lanes(v):
    return jnp.pad(v, [(0, 0)] * (v.ndim - 1) + [(0, LANES - v.shape[-1])])


def kernel(x, c, ctx, c_ctx, w_mod, b_mod, norm1_g, w_in, ssd_conv_w, ssd_conv_b, ssd_dt_bias, ssd_a_log,
           ssd_d, ssd_norm_g, sc_conv_w, w_out, norm2_g, w_router, b_router, w_gate_up, b_gate_up, w_down,
           b_down, final_g):
    bsz = x.shape[0]
    n_tok = bsz * SEQ
    n_assign = n_tok * TOP_K
    n_blocks = n_assign // MOE_BM + N_EXPERTS
    n_rows = n_blocks * MOE_BM
    li = 0

    cvec = jnp.concatenate([c, c_ctx[None, :], jnp.zeros((7, D_MODEL), F32)], axis=0)
    mod3 = _mod(cvec, w_mod[li], b_mod[li][None, :]).reshape(bsz + 8, N_MOD, D_MODEL)

    w = w_in[li]
    wz = w[:, Z0:X0].astype(BF16)
    wxbc = w[:, X0:DT0].astype(BF16)
    wdt = _pad_lanes(w[:, DT0:SC0]).astype(BF16)
    wb = w[:, SC0:SC0 + D_SC].astype(BF16)
    wc = w[:, SC0 + D_SC:SC0 + 2 * D_SC].astype(BF16)
    wu = w[:, SC0 + 2 * D_SC:].astype(BF16)
    g1 = norm1_g[li][None, :]
    cw = ssd_conv_w[li]
    cb = ssd_conv_b[li][None, :]
    dtb = _pad_lanes(ssd_dt_bias[li].reshape(1, 2 * HEADS))
    alog = _pad_lanes(ssd_a_log[li].reshape(1, 2 * HEADS))
    e64, e128 = _expansion_matrices()

    h0 = _ctx_states(ctx, mod3, g1, wxbc[:, :XB_W], wdt, cw[:, :XB_W], cb[:, :XB_W], dtb, alog, e64)

    x2 = x.reshape(n_tok, D_MODEL)
    z, xbc, dtr, scb, v = _inproj(x2, mod3, g1, wz, wxbc, wdt, wb, wc, wu)

    dsk = jnp.repeat(ssd_d[li], HEAD_DIM)[None, :]
    yssd = _ssd(xbc.reshape(bsz, SEQ, XBC_W), z.reshape(bsz, SEQ, D_SSD), dtr.reshape(bsz, SEQ, LANES), h0,
                cw, cb, dtb, alog, dsk, ssd_norm_g[li][None, :], e64, e128)

    wo = w_out[li].astype(BF16)
    g2 = norm2_g[li][None, :]
    x1, lgt = _outproj(x2, yssd.reshape(n_tok, D_SSD), scb, v, mod3, sc_conv_w[li], wo[:D_SSD], wo[D_SSD:],
                       g2, w_router[li].T, b_router[li][:, None])

    dest_t, gate_t, idx_t, meta = _route(lgt, n_blocks)
    dest_flat = dest_t.T.reshape(n_assign)
    cnt = meta[0, :N_EXPERTS]
    start = meta[1, :N_EXPERTS]
    blk_exp = meta[2, :n_blocks]
    n_used = meta[3, :1]

    meta_rows = lax.bitcast_convert_type(_pad_lanes(jnp.concatenate(
        [idx_t.T.astype(F32), gate_t.T, jnp.arange(n_tok, dtype=F32)[:, None]], axis=1)), jnp.uint32)
    pad_meta = lax.bitcast_convert_type(_pad_lanes(jnp.concatenate(
        [jnp.full((MOE_BM, TOP_K), -1.0, F32), jnp.zeros((MOE_BM, TOP_K), F32),
         (TOP_K * n_tok + jnp.arange(MOE_BM, dtype=F32))[:, None]], axis=1)), jnp.uint32)
    zsrc = jnp.concatenate([jnp.zeros((MOE_BM, PACK_W), jnp.uint32), pad_meta,
                            jnp.zeros((MOE_BM, D_MODEL - PACK_W - LANES), jnp.uint32)],
                           axis=1).reshape(MOE_BM, SUBLANES, LANES)

    xs = _dispatch(dest_flat, cnt, start, n_used, x1, meta_rows, mod3, g2, zsrc, n_rows)
    ytm = _experts(blk_exp, n_used, xs, w_gate_up[li], b_gate_up[li][:, None, :],
                   w_down[li], b_down[li][:, None, :], n_tok)
    out = _combine(ytm, x1, mod3, final_g[None, :])
    return out.reshape(bsz, SEQ, D_MODEL)
```

```python
import functools

import jax
import jax.numpy as jnp
from jax import lax
from jax.experimental import pallas as pl
from jax.experimental.pallas import tpu as pltpu

F32 = jnp.float32
BF16 = jnp.bfloat16
HIGHEST = lax.Precision.HIGHEST

D_MODEL = 1024
SEQ = 2048
CTX_LEN = 256
GRID_W = 64
D_SSD = 1024
D_SC = 1024
HEAD_DIM = 64
HEADS = 16
GROUPS = 2
STATE = 128
CHUNK = 128
N_EXPERTS = 32
TOP_K = 4
D_FF = 1024
SWIGLU_LIMIT = 7.0
SWIGLU_ALPHA = 1.702
NORM_EPS = 1e-6
N_MOD = 6
XBC_W = D_SSD + 2 * GROUPS * STATE
XB_W = D_SSD + GROUPS * STATE
LANES = 128

Z0 = 0
X0 = Z0 + D_SSD
B0 = X0 + D_SSD
C0 = B0 + GROUPS * STATE
DT0 = C0 + GROUPS * STATE
SC0 = DT0 + 2 * HEADS

TOK_TILE = 512
MOE_BM = 256
RT_TILE = 512
DISP_TILE = 256
COMB_TILE = 512
SUBLANES = 8
PACK_W = D_MODEL // 2
META_IDX = 0
META_GATE = TOP_K
META_TOK = 2 * TOP_K
VMEM_LIMIT = 56 * 1024 * 1024


def _silu(v):
    return v * jax.nn.sigmoid(v)


def _softplus(v):
    return jnp.maximum(v, 0.0) + jnp.log1p(jnp.exp(-jnp.abs(v)))


def _rms(v):
    return v * lax.rsqrt(jnp.mean(v * v, axis=-1, keepdims=True) + NORM_EPS)


def _dot(a, b):
    return jnp.dot(a, b, preferred_element_type=F32)


def _expand2(v, e2):
    hi = v.astype(BF16)
    lo = (v - hi.astype(F32)).astype(BF16)
    return _dot(jnp.concatenate([hi, lo], axis=1), e2)


def _mod_kernel(c_ref, w_ref, b_ref, o_ref):
    o_ref[...] = jnp.dot(_silu(c_ref[...]), w_ref[...], precision=HIGHEST,
                         preferred_element_type=F32) + b_ref[...]


def _mod(cvec, w_mod, b_mod):
    rows = cvec.shape[0]
    n = w_mod.shape[1]
    tn = 1536
    return pl.pallas_call(
        _mod_kernel,
        out_shape=jax.ShapeDtypeStruct((rows, n), F32),
        grid=(n // tn,),
        in_specs=[pl.BlockSpec((rows, D_MODEL), lambda j: (0, 0)),
                  pl.BlockSpec((D_MODEL, tn), lambda j: (0, j)),
                  pl.BlockSpec((1, tn), lambda j: (0, j))],
        out_specs=pl.BlockSpec((rows, tn), lambda j: (0, j)),
        compiler_params=pltpu.CompilerParams(dimension_semantics=("arbitrary",),
                                             vmem_limit_bytes=VMEM_LIMIT),
        name="mod",
    )(cvec, w_mod, b_mod)


def _ctx_kernel(ctx_ref, mod_ref, g1_ref, wxb_ref, wdt_ref, cw_ref, cb_ref, dtb_ref, alog_ref, e64_ref,
                h0_ref):
    L = CTX_LEN
    m = mod_ref[0]
    hc = _rms(ctx_ref[0]) * g1_ref[...] * (1.0 + m[1:2]) + m[0:1]
    hb = hc.astype(BF16)
    pxb = _dot(hb, wxb_ref[...])
    dtr = _dot(hb, wdt_ref[...])
    rowi = lax.broadcasted_iota(jnp.int32, (L, XB_W), 0)
    dn = jnp.where(rowi == 0, 0.0, pltpu.roll(pxb, 1, 0))
    up = jnp.where(rowi == L - 1, 0.0, pltpu.roll(pxb, L - 1, 0))
    cw = cw_ref[...]
    xb = _silu(cw[0:1] * dn + cw[1:2] * pxb + cw[2:3] * up + cb_ref[...])
    xs = xb[:, :D_SSD]
    bm = xb[:, D_SSD:].astype(BF16)
    dt = _softplus(dtr + dtb_ref[...])
    da = dt * (-jnp.exp(alog_ref[...]))
    ri = lax.broadcasted_iota(jnp.int32, (L, L), 0)
    ci = lax.broadcasted_iota(jnp.int32, (L, L), 1)
    for d in range(2):
        tri = (ci <= ri) if d == 0 else (ci >= ri)
        cum = jnp.dot(tri.astype(F32), da, precision=HIGHEST, preferred_element_type=F32)
        last = cum[L - 1:L] if d == 0 else cum[0:1]
        w_e = _expand2(jnp.exp(last - cum) * dt, e64_ref[d])
        xw = (xs * w_e).astype(BF16)
        for g in range(GROUPS):
            gw = D_SSD // GROUPS
            st = lax.dot_general(bm[:, g * STATE:(g + 1) * STATE], xw[:, g * gw:(g + 1) * gw],
                                 (((0,), (0,)), ((), ())), preferred_element_type=F32)
            h0_ref[0, d, :, g * gw:(g + 1) * gw] = st


def _ctx_states(ctx, mod3, g1, wxb, wdt, cw, cb, dtb, alog, e64):
    bsz = ctx.shape[0]
    mod_row = bsz
    const = lambda *shape: pl.BlockSpec(shape, lambda b: (0,) * len(shape))
    return pl.pallas_call(
        _ctx_kernel,
        out_shape=jax.ShapeDtypeStruct((bsz, 2, STATE, D_SSD), F32),
        grid=(bsz,),
        in_specs=[pl.BlockSpec((1, CTX_LEN, D_MODEL), lambda b: (b, 0, 0)),
                  pl.BlockSpec((1, N_MOD, D_MODEL), lambda b: (mod_row, 0, 0)),
                  const(1, D_MODEL), const(D_MODEL, XB_W), const(D_MODEL, LANES),
                  const(3, XB_W), const(1, XB_W), const(1, LANES), const(1, LANES),
                  const(2, 2 * LANES, D_SSD)],
        out_specs=pl.BlockSpec((1, 2, STATE, D_SSD), lambda b: (b, 0, 0, 0)),
        compiler_params=pltpu.CompilerParams(dimension_semantics=("arbitrary",),
                                             vmem_limit_bytes=VMEM_LIMIT),
        name="ctx_states",
    )(ctx, mod3, g1, wxb, wdt, cw, cb, dtb, alog, e64)


def _inproj_kernel(x_ref, mod_ref, g1_ref, wz_ref, wxbc_ref, wdt_ref, wb_ref, wc_ref, wu_ref,
                   z_ref, xbc_ref, dt_ref, scb_ref, v_ref):
    m = mod_ref[0]
    hx = _rms(x_ref[...]) * g1_ref[...] * (1.0 + m[1:2]) + m[0:1]
    hb = hx.astype(BF16)
    z_ref[...] = _dot(hb, wz_ref[...]).astype(BF16)
    xbc_ref[...] = _dot(hb, wxbc_ref[...]).astype(BF16)
    dt_ref[...] = _dot(hb, wdt_ref[...])
    scb_ref[...] = _dot(hb, wb_ref[...]).astype(BF16)
    v_ref[...] = (_dot(hb, wc_ref[...]) * _dot(hb, wu_ref[...])).astype(BF16)


def _inproj(x2, mod3, g1, wz, wxbc, wdt, wb, wc, wu):
    t = x2.shape[0]
    tm = TOK_TILE
    per_b = SEQ // tm
    const = lambda *shape: pl.BlockSpec(shape, lambda i: (0,) * len(shape))
    tile = lambda w: pl.BlockSpec((tm, w), lambda i: (i, 0))
    return pl.pallas_call(
        _inproj_kernel,
        out_shape=(jax.ShapeDtypeStruct((t, D_SSD), BF16), jax.ShapeDtypeStruct((t, XBC_W), BF16),
                   jax.ShapeDtypeStruct((t, LANES), F32), jax.ShapeDtypeStruct((t, D_SC), BF16),
                   jax.ShapeDtypeStruct((t, D_SC), BF16)),
        grid=(t // tm,),
        in_specs=[tile(D_MODEL),
                  pl.BlockSpec((1, N_MOD, D_MODEL), lambda i: (i // per_b, 0, 0)),
                  const(1, D_MODEL), const(D_MODEL, D_SSD), const(D_MODEL, XBC_W), const(D_MODEL, LANES),
                  const(D_MODEL, D_SC), const(D_MODEL, D_SC), const(D_MODEL, D_SC)],
        out_specs=(tile(D_SSD), tile(XBC_W), tile(LANES), tile(D_SC), tile(D_SC)),
        compiler_params=pltpu.CompilerParams(dimension_semantics=("arbitrary",),
                                             vmem_limit_bytes=VMEM_LIMIT),
        name="inproj",
    )(x2, mod3, g1, wz, wxbc, wdt, wb, wc, wu)


def _ssd_kernel(xbc_ref, z_ref, dt_ref, h0_ref, cw_ref, cb_ref, dtb_ref, alog_ref, dsk_ref, g_ref,
                e64_ref, e128_ref, o_ref, xc_ref, y_ref, s_ref):
    Q = CHUNK
    nck = SEQ // Q
    gw = D_SSD // GROUPS

    rowi = lax.broadcasted_iota(jnp.int32, (SUBLANES, XBC_W), 0)

    def conv_body(c, carry):
        r0 = pl.multiple_of(c * Q, Q)
        main = xbc_ref[0, pl.ds(r0, Q), :].astype(F32)
        pstart = pl.multiple_of(jnp.maximum(r0 - 16, 0), 16)
        nstart = pl.multiple_of(jnp.minimum(r0 + Q, SEQ - 16), 16)
        prev = xbc_ref[0, pl.ds(pstart, 16), :].astype(F32)[15:16]
        nxt = xbc_ref[0, pl.ds(nstart, 16), :].astype(F32)[0:1]
        prev = jnp.where(c > 0, prev, 0.0)
        nxt = jnp.where(c < nck - 1, nxt, 0.0)
        dn = pltpu.roll(main, 1, 0)
        up = pltpu.roll(main, Q - 1, 0)
        dn = jnp.concatenate([jnp.where(rowi == 0, prev, dn[0:SUBLANES]), dn[SUBLANES:]], axis=0)
        up = jnp.concatenate([up[:Q - SUBLANES], jnp.where(rowi == SUBLANES - 1, nxt, up[Q - SUBLANES:])], axis=0)
        cw = cw_ref[...]
        conv = cw[0:1] * dn + cw[1:2] * main + cw[2:3] * up + cb_ref[...]
        xc_ref[pl.ds(r0, Q), :] = _silu(conv).astype(BF16)
        return carry

    lax.fori_loop(0, nck, conv_body, 0)

    ri = lax.broadcasted_iota(jnp.int32, (Q, Q), 0)
    ci = lax.broadcasted_iota(jnp.int32, (Q, Q), 1)
    lane = lax.broadcasted_iota(jnp.int32, (Q, LANES), 1)
    a_neg = -jnp.exp(alog_ref[...])

    def chunk(c, d, first):
        r0 = pl.multiple_of(c * Q, Q)
        rows = pl.ds(r0, Q)
        xs_b = xc_ref[rows, 0:D_SSD]
        xs = xs_b.astype(F32)
        bm = xc_ref[rows, D_SSD:D_SSD + GROUPS * STATE]
        cm = xc_ref[rows, D_SSD + GROUPS * STATE:XBC_W]
        dt = _softplus(dt_ref[0, rows, :] + dtb_ref[...])
        da = dt * a_neg
        tri = (ci <= ri) if d == 0 else (ci >= ri)
        p0 = da.astype(BF16)
        r1 = da - p0.astype(F32)
        p1 = r1.astype(BF16)
        p2 = (r1 - p1.astype(F32)).astype(BF16)
        tri_b = jnp.where(tri, 1.0, 0.0).astype(BF16)
        cum = _dot(jnp.concatenate([tri_b, tri_b, tri_b], axis=1),
                   jnp.concatenate([p0, p1, p2], axis=0))
        sub_t = (cum - jnp.log(dt)).T
        last = cum[Q - 1:Q] if d == 0 else cum[0:1]
        ecum_e = _expand2(jnp.exp(cum), e64_ref[d])
        w_e = _expand2(jnp.exp(last - cum) * dt, e64_ref[d])
        colb = _expand2(cum, e128_ref[d])
        decay_e = ecum_e[Q - 1:Q] if d == 0 else ecum_e[0:1]

        gmat = [lax.dot_general(cm[:, g * STATE:(g + 1) * STATE], bm[:, g * STATE:(g + 1) * STATE],
                                (((1,), (1,)), ((), ())), preferred_element_type=F32)
                for g in range(GROUPS)]
        zero_b = jnp.zeros((Q, LANES), BF16)
        y_parts = []
        for p in range(HEADS // 2):
            g = (2 * p) // (HEADS // GROUPS)
            ms = []
            for hh in (2 * p, 2 * p + 1):
                seg = colb[:, hh * LANES:(hh + 1) * LANES] - sub_t[HEADS * d + hh:HEADS * d + hh + 1, :]
                ms.append((jnp.where(tri, jnp.exp(seg), 0.0) * gmat[g]).astype(BF16))
            mcat = jnp.concatenate(ms, axis=1)
            xp = xs_b[:, p * LANES:(p + 1) * LANES]
            rhs = jnp.concatenate([jnp.where(lane < HEAD_DIM, xp, zero_b),
                                   jnp.where(lane >= HEAD_DIM, xp, zero_b)], axis=0)
            y_parts.append(_dot(mcat, rhs))
        y_diag = jnp.concatenate(y_parts, axis=1)

        s_old = s_ref[d]
        s_bf = s_old.astype(BF16)
        y_off = jnp.concatenate(
            [_dot(cm[:, g * STATE:(g + 1) * STATE], s_bf[:, g * gw:(g + 1) * gw]) for g in range(GROUPS)],
            axis=1)
        y = y_diag + y_off * ecum_e

        xw = (xs * w_e).astype(BF16)
        upd = jnp.concatenate(
            [lax.dot_general(bm[:, g * STATE:(g + 1) * STATE], xw[:, g * gw:(g + 1) * gw],
                             (((0,), (0,)), ((), ())), preferred_element_type=F32) for g in range(GROUPS)],
            axis=1)
        s_ref[d] = s_old * decay_e + upd

        if first:
            y_ref[rows, :] = y + dsk_ref[...] * xs
        else:
            tot = y_ref[rows, :] + y
            zz = z_ref[0, rows, :].astype(F32)
            gz = tot * _silu(zz)
            outs = []
            for g in range(GROUPS):
                gg = gz[:, g * gw:(g + 1) * gw]
                outs.append(gg * lax.rsqrt(jnp.mean(gg * gg, axis=-1, keepdims=True) + NORM_EPS))
            o_ref[0, rows, :] = (jnp.concatenate(outs, axis=1) * g_ref[...]).astype(BF16)

    s_ref[...] = h0_ref[0]

    def first_half(i, carry):
        chunk(i, 0, True)
        chunk(nck - 1 - i, 1, True)
        return carry

    def second_half(i, carry):
        chunk(i, 0, False)
        chunk(nck - 1 - i, 1, False)
        return carry

    lax.fori_loop(0, nck // 2, first_half, 0)
    lax.fori_loop(nck // 2, nck, second_half, 0)


def _ssd(xbc3, z3, dt3, h0, cw, cb, dtb, alog, dsk, g, e64, e128):
    bsz = xbc3.shape[0]
    const = lambda *shape: pl.BlockSpec(shape, lambda b: (0,) * len(shape))
    seq = lambda w: pl.BlockSpec((1, SEQ, w), lambda b: (b, 0, 0))
    return pl.pallas_call(
        _ssd_kernel,
        out_shape=jax.ShapeDtypeStruct((bsz, SEQ, D_SSD), BF16),
        grid=(bsz,),
        in_specs=[seq(XBC_W), seq(D_SSD), seq(LANES),
                  pl.BlockSpec((1, 2, STATE, D_SSD), lambda b: (b, 0, 0, 0)),
                  const(3, XBC_W), const(1, XBC_W), const(1, LANES), const(1, LANES),
                  const(1, D_SSD), const(1, D_SSD), const(2, 2 * LANES, D_SSD),
                  const(2, 2 * LANES, HEADS * LANES)],
        out_specs=seq(D_SSD),
        scratch_shapes=[pltpu.VMEM((SEQ, XBC_W), BF16), pltpu.VMEM((SEQ, D_SSD), F32),
                        pltpu.VMEM((2, STATE, D_SSD), F32)],
        compiler_params=pltpu.CompilerParams(dimension_semantics=("arbitrary",),
                                             vmem_limit_bytes=VMEM_LIMIT),
        name="ssd",
    )(xbc3, z3, dt3, h0, cw, cb, dtb, alog, dsk, g, e64, e128)


def _outproj_kernel(x_ref, yssd_ref, scb_ref, v_ref, vp_ref, vn_ref, mod_ref, scw_ref, wo1_ref, wo2_ref,
                    g2_ref, wrt_ref, br_ref, x1_ref, lg_ref):
    tm = TOK_TILE
    per_b = SEQ // tm
    i = pl.program_id(0)
    first = (i % per_b) == 0
    last = (i % per_b) == per_b - 1
    m = mod_ref[0]
    v = v_ref[...].astype(F32)
    vp = jnp.where(first, 0.0, vp_ref[...].astype(F32))
    vn = jnp.where(last, 0.0, vn_ref[...].astype(F32))
    dn = jnp.concatenate([vp, v[:tm - GRID_W]], axis=0)
    up = jnp.concatenate([v[GRID_W:], vn], axis=0)
    scw = scw_ref[...]
    ysc = scb_ref[...].astype(F32) * (scw[0:1] * dn + scw[1:2] * v + scw[2:3] * up)
    out = _dot(yssd_ref[...], wo1_ref[...]) + _dot(ysc.astype(BF16), wo2_ref[...])
    x1 = x_ref[...] + m[2:3] * out
    x1_ref[...] = x1
    h2 = _rms(x1) * g2_ref[...] * (1.0 + m[4:5]) + m[3:4]
    lg_ref[...] = lax.dot_general(wrt_ref[...], h2, (((1,), (1,)), ((), ())), precision=HIGHEST,
                                  preferred_element_type=F32) + br_ref[...]


def _outproj(x2, yssd, scb, v, mod3, scw, wo1, wo2, g2, wrt, br):
    t = x2.shape[0]
    tm = TOK_TILE
    per_b = SEQ // tm
    r = tm // GRID_W
    nrow = t // GRID_W
    const = lambda *shape: pl.BlockSpec(shape, lambda i: (0,) * len(shape))
    tile = lambda w: pl.BlockSpec((tm, w), lambda i: (i, 0))
    return pl.pallas_call(
        _outproj_kernel,
        out_shape=(jax.ShapeDtypeStruct((t, D_MODEL), F32), jax.ShapeDtypeStruct((N_EXPERTS, t), F32)),
        grid=(t // tm,),
        in_specs=[tile(D_MODEL), tile(D_SSD), tile(D_SC), tile(D_SC),
                  pl.BlockSpec((GRID_W, D_SC), lambda i: (jnp.maximum(i * r - 1, 0), 0)),
                  pl.BlockSpec((GRID_W, D_SC), lambda i: (jnp.minimum((i + 1) * r, nrow - 1), 0)),
                  pl.BlockSpec((1, N_MOD, D_MODEL), lambda i: (i // per_b, 0, 0)),
                  const(3, D_SC), const(D_SSD, D_MODEL), const(D_SC, D_MODEL), const(1, D_MODEL),
                  const(N_EXPERTS, D_MODEL), const(N_EXPERTS, 1)],
        out_specs=(tile(D_MODEL), pl.BlockSpec((N_EXPERTS, tm), lambda i: (0, i))),
        compiler_params=pltpu.CompilerParams(dimension_semantics=("arbitrary",),
                                             vmem_limit_bytes=VMEM_LIMIT),
        name="outproj",
    )(x2, yssd, scb, v, v, v, mod3, scw, wo1, wo2, g2, wrt, br)


def _route_kernel(lg_ref, dest_ref, gate_ref, idx_ref, meta_ref, rank_ref, carry_ref, *, n_tok, n_blocks):
    tt = RT_TILE
    ne = N_EXPERTS
    eio = lax.broadcasted_iota(jnp.int32, (ne, tt), 0)
    si = lax.broadcasted_iota(jnp.int32, (tt, tt), 0)
    ti = lax.broadcasted_iota(jnp.int32, (tt, tt), 1)
    before = (si < ti).astype(BF16)
    carry_ref[...] = jnp.zeros_like(carry_ref)

    def tile_body(j, c):
        t0 = pl.multiple_of(j * tt, tt)
        l = lg_ref[:, pl.ds(t0, tt)]
        onehot = jnp.zeros((ne, tt), F32)
        tops, sels = [], []
        for _ in range(TOP_K):
            mx = jnp.max(l, axis=0, keepdims=True)
            idx = jnp.min(jnp.where(l == mx, eio, ne), axis=0, keepdims=True)
            sel = eio == idx
            l = jnp.where(sel, -jnp.inf, l)
            onehot = onehot + sel.astype(F32)
            tops.append(mx)
            sels.append(sel)
            idx_ref[pl.ds(len(tops) - 1, 1), pl.ds(t0, tt)] = idx
        ex = [jnp.exp(tv - tops[0]) for tv in tops]
        den = ex[0] + ex[1] + ex[2] + ex[3]
        prefix = _dot(onehot.astype(BF16), before) + carry_ref[:, 0:1]
        for k in range(TOP_K):
            gate_ref[pl.ds(k, 1), pl.ds(t0, tt)] = ex[k] / den
            rk = jnp.sum(jnp.where(sels[k], prefix, 0.0), axis=0, keepdims=True)
            rank_ref[pl.ds(k, 1), pl.ds(t0, tt)] = rk.astype(jnp.int32)
        carry_ref[...] = carry_ref[...] + jnp.sum(onehot, axis=1, keepdims=True)
        return c

    lax.fori_loop(0, n_tok // tt, tile_body, 0)

    counts = carry_ref[...]
    padded = jnp.floor((counts + (MOE_BM - 1)) * (1.0 / MOE_BM)) * MOE_BM
    er = lax.broadcasted_iota(jnp.int32, (ne, ne), 0)
    ec = lax.broadcasted_iota(jnp.int32, (ne, ne), 1)
    pad_start = jnp.dot((ec < er).astype(F32), padded, precision=HIGHEST, preferred_element_type=F32)
    pad_end = pad_start + padded

    def dest_body(j, c):
        t0 = pl.multiple_of(j * tt, tt)
        for k in range(TOP_K):
            idx = idx_ref[pl.ds(k, 1), pl.ds(t0, tt)]
            base = jnp.sum(jnp.where(eio == idx, pad_start[:, 0:1], 0.0), axis=0, keepdims=True)
            dest_ref[pl.ds(k, 1), pl.ds(t0, tt)] = base.astype(jnp.int32) + rank_ref[pl.ds(k, 1), pl.ds(t0, tt)]
        return c

    lax.fori_loop(0, n_tok // tt, dest_body, 0)

    width = meta_ref.shape[1]
    sub = lax.broadcasted_iota(jnp.int32, (ne, width), 0)
    lan = lax.broadcasted_iota(jnp.int32, (ne, width), 1)
    diag = sub == lan
    cnt_row = jnp.sum(jnp.where(diag, counts[:, 0:1], 0.0), axis=0, keepdims=True)
    start_row = jnp.sum(jnp.where(diag, pad_start[:, 0:1], 0.0), axis=0, keepdims=True)
    blk_start = (lan * MOE_BM).astype(F32)
    blk_exp = jnp.sum((pad_end[:, 0:1] <= blk_start).astype(F32), axis=0, keepdims=True)
    blk_exp = jnp.minimum(blk_exp, float(ne - 1))
    used = jnp.sum(padded[:, 0:1], axis=0, keepdims=True) * (1.0 / MOE_BM)
    meta_ref[0:1, :] = cnt_row.astype(jnp.int32)
    meta_ref[1:2, :] = start_row.astype(jnp.int32)
    meta_ref[2:3, :] = blk_exp.astype(jnp.int32)
    meta_ref[3:4, :] = jnp.broadcast_to(used, (1, width)).astype(jnp.int32)
    meta_ref[4:8, :] = jnp.zeros((4, width), jnp.int32)


def _route(lgt, n_blocks):
    ne, n_tok = lgt.shape
    width = -(-n_blocks // LANES) * LANES
    full = lambda *shape: pl.BlockSpec(shape, lambda: (0,) * len(shape))
    return pl.pallas_call(
        functools.partial(_route_kernel, n_tok=n_tok, n_blocks=n_blocks),
        out_shape=(jax.ShapeDtypeStruct((TOP_K, n_tok), jnp.int32),
                   jax.ShapeDtypeStruct((TOP_K, n_tok), F32),
                   jax.ShapeDtypeStruct((TOP_K, n_tok), jnp.int32),
                   jax.ShapeDtypeStruct((8, width), jnp.int32)),
        in_specs=[full(ne, n_tok)],
        out_specs=(full(TOP_K, n_tok), full(TOP_K, n_tok), full(TOP_K, n_tok), full(8, width)),
        scratch_shapes=[pltpu.VMEM((TOP_K, n_tok), jnp.int32), pltpu.VMEM((ne, LANES), F32)],
        compiler_params=pltpu.CompilerParams(vmem_limit_bytes=VMEM_LIMIT),
        name="route",
    )(lgt)


def _dispatch_kernel(dest_ref, cnt_ref, start_ref, nu_ref, x1_ref, meta_ref, mod_ref, g2_ref, zsrc_ref, xs_ref,
                     hbuf, sem, zsem):
    i = pl.program_id(0)
    n = pl.num_programs(0)
    tl = DISP_TILE
    slot = i % 2
    nb = xs_ref.shape[0] // MOE_BM

    def zero_block(b):
        return pltpu.make_async_copy(zsrc_ref, xs_ref.at[pl.ds(b * MOE_BM, MOE_BM)], zsem)

    @pl.when(i == 0)
    def _():
        def start_e(e, c):
            @pl.when(cnt_ref[e] > 0)
            def _():
                zero_block((start_ref[e] + cnt_ref[e] - 1) // MOE_BM).start()
            return c

        def wait_e(e, c):
            @pl.when(cnt_ref[e] > 0)
            def _():
                zero_block(0).wait()
            return c

        def start_t(b, c):
            zero_block(b).start()
            return c

        def wait_t(b, c):
            zero_block(0).wait()
            return c

        lax.fori_loop(0, N_EXPERTS, start_e, 0)
        lax.fori_loop(nu_ref[0], nb, start_t, 0)
        lax.fori_loop(0, N_EXPERTS, wait_e, 0)
        lax.fori_loop(nu_ref[0], nb, wait_t, 0)

    m = mod_ref[0]
    h2 = _rms(x1_ref[...]) * g2_ref[...] * (1.0 + m[4:5]) + m[3:4]
    lo = pltpu.bitcast(h2[:, :PACK_W].astype(BF16).astype(F32), jnp.uint32) >> 16
    hi = pltpu.bitcast(h2[:, PACK_W:].astype(BF16).astype(F32), jnp.uint32) & jnp.uint32(0xFFFF0000)
    row = jnp.concatenate([lo | hi, meta_ref[...], jnp.zeros((tl, D_MODEL - PACK_W - LANES), jnp.uint32)], axis=1)
    hbuf[slot] = row.reshape(tl, SUBLANES, LANES)

    def per_tok(t, c):
        tok = i * tl + t
        for k in range(TOP_K):
            d = dest_ref[tok * TOP_K + k]
            pltpu.make_async_copy(hbuf.at[slot, t], xs_ref.at[d], sem.at[slot]).start(priority=k % 2)
        return c

    lax.fori_loop(0, tl, per_tok, 0)

    def wait_slot(sl):
        for _ in range(TOP_K):
            pltpu.make_async_copy(hbuf.at[sl], xs_ref.at[pl.ds(0, tl)], sem.at[sl]).wait()

    @pl.when(i > 0)
    def _():
        wait_slot(1 - slot)

    @pl.when(i == n - 1)
    def _():
        wait_slot(slot)


def _dispatch(dest_flat, cnt, start, n_used, x1, meta_rows, mod3, g2, zsrc, n_rows):
    n_tok = x1.shape[0]
    tl = DISP_TILE
    per_b = SEQ // tl
    return pl.pallas_call(
        _dispatch_kernel,
        out_shape=jax.ShapeDtypeStruct((n_rows, SUBLANES, LANES), jnp.uint32),
        grid_spec=pltpu.PrefetchScalarGridSpec(
            num_scalar_prefetch=4,
            grid=(n_tok // tl,),
            in_specs=[pl.BlockSpec((tl, D_MODEL), lambda i, *_: (i, 0)),
                      pl.BlockSpec((tl, LANES), lambda i, *_: (i, 0)),
                      pl.BlockSpec((1, N_MOD, D_MODEL), lambda i, *_: (i // per_b, 0, 0)),
                      pl.BlockSpec((1, D_MODEL), lambda i, *_: (0, 0)),
                      pl.BlockSpec((MOE_BM, SUBLANES, LANES), lambda i, *_: (0, 0, 0))],
            out_specs=pl.BlockSpec(memory_space=pl.ANY),
            scratch_shapes=[pltpu.VMEM((2, tl, SUBLANES, LANES), jnp.uint32),
                            pltpu.SemaphoreType.DMA((2,)), pltpu.SemaphoreType.DMA]),
        compiler_params=pltpu.CompilerParams(dimension_semantics=("arbitrary",),
                                             vmem_limit_bytes=VMEM_LIMIT),
        name="dispatch",
    )(dest_flat, cnt, start, n_used, x1, meta_rows, mod3, g2, zsrc)


def _expert_kernel(be_ref, nu_ref, xs_ref, wgu_hbm, bgu_ref, wd_hbm, bd_ref, zero_ref, ytm_ref,
                   big, idv, ids, wgu_raw, wd_raw, wgu_bf, wd_bf, sc_sem, id_sem, z_sem, w_sem, *, n_tok):
    j = pl.program_id(0)
    nbk = pl.num_programs(0) - 1
    nu = nu_ref[0]
    slot = j % 2
    prev = 1 - slot

    def weight_copies(e):
        return (pltpu.make_async_copy(wgu_hbm.at[e], wgu_raw, w_sem.at[0]),
                pltpu.make_async_copy(wd_hbm.at[e], wd_raw, w_sem.at[1]))

    @pl.when(j == 0)
    def _():
        cp = pltpu.make_async_copy(zero_ref, ytm_ref.at[pl.ds(TOP_K * n_tok, MOE_BM)], z_sem)
        cp.start()
        for w in weight_copies(be_ref[0]):
            w.start()
        cp.wait()

    e_now = be_ref[jnp.minimum(j, nbk - 1)]
    new_expert = jnp.logical_or(j == 0, be_ref[jnp.maximum(jnp.minimum(j, nbk - 1) - 1, 0)] != e_now)

    @pl.when(jnp.logical_and(j < nu, new_expert))
    def _():
        for w in weight_copies(e_now):
            w.wait()
        n_cc = 8
        for c in range(n_cc):
            cc = slice(c * (2 * D_FF // n_cc), (c + 1) * (2 * D_FF // n_cc))
            wgu_bf[:, cc] = wgu_raw[:, cc].astype(BF16)
        for c in range(n_cc // 2):
            cc = slice(c * (2 * D_MODEL // n_cc), (c + 1) * (2 * D_MODEL // n_cc))
            wd_bf[:, cc] = wd_raw[:, cc].astype(BF16)
        j_next = lax.while_loop(lambda t: jnp.logical_and(t < nu, be_ref[jnp.minimum(t, nbk - 1)] == e_now),
                                lambda t: t + 1, j + 1)

        @pl.when(j_next < nu)
        def _():
            for w in weight_copies(be_ref[jnp.minimum(j_next, nbk - 1)]):
                w.start()

    n_pc = 4
    pw1 = D_FF // n_pc
    pw2 = D_MODEL // n_pc
    n_lt = D_MODEL // LANES
    spb = MOE_BM // SUBLANES
    groups = [3 * MOE_BM // 16] * n_pc + [MOE_BM // 16] * n_pc
    assert sum(groups) == MOE_BM

    def slab_store(unit, col0, val):
        for s in range(val.shape[1] // LANES):
            t0 = unit * MOE_BM + (col0 // LANES + s) * spb
            big[pl.ds(t0, spb)] = val[:, s * LANES:(s + 1) * LANES].reshape(spb, SUBLANES, LANES)

    def slab_load(unit, row0, n):
        return jnp.concatenate(
            [big[pl.ds(unit * MOE_BM + s * spb + row0 // SUBLANES, n // SUBLANES)].reshape(n, LANES)
             for s in range(n_lt)], axis=1)

    def scatter_group(g):
        lo = sum(groups[:g])
        n = groups[g]
        big[pl.ds(prev * MOE_BM + lo, n)] = slab_load(3 + prev, lo, n).reshape(n, n_lt, LANES)
        for r in range(lo, lo + n):
            pltpu.make_async_copy(big.at[prev * MOE_BM + r], ytm_ref.at[ids[prev, 0, r]],
                                  sc_sem.at[prev]).start(priority=r % 2)

    def scatter_wait(sl):
        pltpu.make_async_copy(big.at[pl.ds(0, MOE_BM)], ytm_ref.at[pl.ds(0, MOE_BM)], sc_sem.at[sl]).wait()

    def compute(with_scatter):
        e_f = be_ref[jnp.minimum(j, nbk - 1)].astype(F32)
        words = xs_ref[...].reshape(MOE_BM, D_MODEL)
        packed = words[:, 0:PACK_W]
        meta = pltpu.bitcast(words[:, PACK_W:PACK_W + LANES], F32)
        xb = jnp.concatenate(
            [pltpu.bitcast(packed << 16, F32).astype(BF16),
             pltpu.bitcast(packed & jnp.uint32(0xFFFF0000), F32).astype(BF16)], axis=1)
        gate = jnp.zeros((MOE_BM, 1), F32)
        kk = jnp.zeros((MOE_BM, 1), F32)
        for k in range(TOP_K):
            mk = meta[:, META_IDX + k:META_IDX + k + 1] == e_f
            gate = gate + jnp.where(mk, meta[:, META_GATE + k:META_GATE + k + 1], 0.0)
            kk = kk + jnp.where(mk, float(k), 0.0)
        row = kk * float(n_tok) + meta[:, META_TOK:META_TOK + 1]
        row_t = jnp.broadcast_to(row, (MOE_BM, LANES)).T
        idv[slot] = row_t[0:8].astype(jnp.int32)
        pltpu.make_async_copy(idv.at[slot], ids.at[slot], id_sem.at[slot]).start()
        for c in range(n_pc):
            if with_scatter:
                scatter_group(c)
            cg = slice(c * pw1, (c + 1) * pw1)
            cl = slice(D_FF + c * pw1, D_FF + (c + 1) * pw1)
            glu = jnp.minimum(_dot(xb, wgu_bf[:, cg]) + bgu_ref[0, :, cg], SWIGLU_LIMIT)
            lin = jnp.clip(_dot(xb, wgu_bf[:, cl]) + bgu_ref[0, :, cl], -SWIGLU_LIMIT, SWIGLU_LIMIT)
            slab_store(2, c * pw1, glu * jax.nn.sigmoid(SWIGLU_ALPHA * glu) * (lin + 1.0))
        act = slab_load(2, 0, MOE_BM).astype(BF16)
        for c in range(n_pc):
            if with_scatter:
                scatter_group(n_pc + c)
            cs = slice(c * pw2, (c + 1) * pw2)
            slab_store(3 + slot, c * pw2, (_dot(act, wd_bf[:, cs]) + bd_ref[0, :, cs]) * gate)

    def ids_wait():
        pltpu.make_async_copy(idv.at[prev], ids.at[prev], id_sem.at[prev]).wait()

    @pl.when(jnp.logical_and(j >= 3, j <= nu))
    def _():
        scatter_wait(prev)

    @pl.when(j == 0)
    def _():
        compute(False)

    @pl.when(jnp.logical_and(j >= 1, j < nu))
    def _():
        ids_wait()
        compute(True)

    @pl.when(j == nu)
    def _():
        ids_wait()
        for g in range(len(groups)):
            scatter_group(g)
        scatter_wait(prev)

    @pl.when(jnp.logical_and(j == nu, j >= 2))
    def _():
        scatter_wait(slot)


def _experts(blk_exp, n_used, xs, wgu, bgu, wd, bd, n_tok):
    n_rows = xs.shape[0]
    nb = n_rows // MOE_BM
    row_blk = lambda j, be, nu: (jnp.minimum(j, nu[0] - 1), 0, 0)
    per_e = lambda j, be, nu: (be[jnp.minimum(j, nb - 1)], 0, 0)
    sub = D_MODEL // LANES
    zero = jnp.zeros((MOE_BM, sub, LANES), F32)
    return pl.pallas_call(
        functools.partial(_expert_kernel, n_tok=n_tok),
        out_shape=jax.ShapeDtypeStruct((TOP_K * n_tok + MOE_BM, sub, LANES), F32),
        grid_spec=pltpu.PrefetchScalarGridSpec(
            num_scalar_prefetch=2,
            grid=(nb + 1,),
            in_specs=[pl.BlockSpec((MOE_BM, SUBLANES, LANES), row_blk),
                      pl.BlockSpec(memory_space=pl.ANY),
                      pl.BlockSpec((1, 1, 2 * D_FF), per_e),
                      pl.BlockSpec(memory_space=pl.ANY),
                      pl.BlockSpec((1, 1, D_MODEL), per_e),
                      pl.BlockSpec((MOE_BM, sub, LANES), lambda j, be, nu: (0, 0, 0))],
            out_specs=pl.BlockSpec(memory_space=pl.ANY),
            scratch_shapes=[pltpu.VMEM((5 * MOE_BM, sub, LANES), F32), pltpu.VMEM((2, 8, MOE_BM), jnp.int32),
                            pltpu.SMEM((2, 8, MOE_BM), jnp.int32),
                            pltpu.VMEM((D_MODEL, 2 * D_FF), F32), pltpu.VMEM((D_FF, D_MODEL), F32),
                            pltpu.VMEM((D_MODEL, 2 * D_FF), BF16), pltpu.VMEM((D_FF, D_MODEL), BF16),
                            pltpu.SemaphoreType.DMA((2,)), pltpu.SemaphoreType.DMA((2,)),
                            pltpu.SemaphoreType.DMA, pltpu.SemaphoreType.DMA((2,))]),
        compiler_params=pltpu.CompilerParams(dimension_semantics=("arbitrary",),
                                             vmem_limit_bytes=VMEM_LIMIT),
        name="experts",
    )(blk_exp, n_used, xs, wgu, bgu, wd, bd, zero)


def _combine_kernel(y0_ref, y1_ref, y2_ref, y3_ref, x1_ref, mod_ref, fg_ref, o_ref):
    m = mod_ref[0]
    moe = ((y0_ref[...] + y1_ref[...]) + (y2_ref[...] + y3_ref[...])).reshape(COMB_TILE, D_MODEL)
    x2 = x1_ref[...] + m[5:6] * moe
    o_ref[...] = _rms(x2) * fg_ref[...]


def _combine(ytm, x1, mod3, fg):
    n_tok = x1.shape[0]
    tc = COMB_TILE
    per_b = SEQ // tc
    nt = n_tok // tc
    slot_spec = lambda k: pl.BlockSpec((tc, D_MODEL // LANES, LANES), lambda i: (i + k * nt, 0, 0))
    return pl.pallas_call(
        _combine_kernel,
        out_shape=jax.ShapeDtypeStruct((n_tok, D_MODEL), F32),
        grid=(nt,),
        in_specs=[slot_spec(0), slot_spec(1), slot_spec(2), slot_spec(3),
                  pl.BlockSpec((tc, D_MODEL), lambda i: (i, 0)),
                  pl.BlockSpec((1, N_MOD, D_MODEL), lambda i: (i // per_b, 0, 0)),
                  pl.BlockSpec((1, D_MODEL), lambda i: (0, 0))],
        out_specs=pl.BlockSpec((tc, D_MODEL), lambda i: (i, 0)),
        compiler_params=pltpu.CompilerParams(dimension_semantics=("arbitrary",),
                                             vmem_limit_bytes=VMEM_LIMIT),
        name="combine",
    )(ytm, ytm, ytm, ytm, x1, mod3, fg)


def _expansion_matrices():
    r = (jnp.arange(2 * LANES) % LANES)[:, None]
    out64, out128 = [], []
    for d in range(2):
        l64 = jnp.arange(D_SSD)[None, :]
        l128 = jnp.arange(HEADS * LANES)[None, :]
        out64.append((l64 // HEAD_DIM == r - HEADS * d).astype(BF16))
        out128.append((l128 // LANES == r - HEADS * d).astype(BF16))
    return jnp.stack(out64), jnp.stack(out128)


def _pad_lanes(v):
    return jnp.pad(v, [(0, 0)] * (v.ndim - 1) + [(0, LANES - v.shape[-1])])


def kernel(x, c, ctx, c_ctx, w_mod, b_mod, norm1_g, w_in, ssd_conv_w, ssd_conv_b, ssd_dt_bias, ssd_a_log,
           ssd_d, ssd_norm_g, sc_conv_w, w_out, norm2_g, w_router, b_router, w_gate_up, b_gate_up, w_down,
           b_down, final_g):
    bsz = x.shape[0]
    n_tok = bsz * SEQ
    n_assign = n_tok * TOP_K
    n_blocks = n_assign // MOE_BM + N_EXPERTS
    n_rows = n_blocks * MOE_BM
    li = 0

    cvec = jnp.concatenate([c, c_ctx[None, :], jnp.zeros((7, D_MODEL), F32)], axis=0)
    mod3 = _mod(cvec, w_mod[li], b_mod[li][None, :]).reshape(bsz + 8, N_MOD, D_MODEL)

    w = w_in[li]
    wz = w[:, Z0:X0].astype(BF16)
    wxbc = w[:, X0:DT0].astype(BF16)
    wdt = _pad_lanes(w[:, DT0:SC0]).astype(BF16)
    wb = w[:, SC0:SC0 + D_SC].astype(BF16)
    wc = w[:, SC0 + D_SC:SC0 + 2 * D_SC].astype(BF16)
    wu = w[:, SC0 + 2 * D_SC:].astype(BF16)
    g1 = norm1_g[li][None, :]
    cw = ssd_conv_w[li]
    cb = ssd_conv_b[li][None, :]
    dtb = _pad_lanes(ssd_dt_bias[li].reshape(1, 2 * HEADS))
    alog = _pad_lanes(ssd_a_log[li].reshape(1, 2 * HEADS))
    e64, e128 = _expansion_matrices()

    h0 = _ctx_states(ctx, mod3, g1, wxbc[:, :XB_W], wdt, cw[:, :XB_W], cb[:, :XB_W], dtb, alog, e64)

    x2 = x.reshape(n_tok, D_MODEL)
    z, xbc, dtr, scb, v = _inproj(x2, mod3, g1, wz, wxbc, wdt, wb, wc, wu)

    dsk = jnp.repeat(ssd_d[li], HEAD_DIM)[None, :]
    yssd = _ssd(xbc.reshape(bsz, SEQ, XBC_W), z.reshape(bsz, SEQ, D_SSD), dtr.reshape(bsz, SEQ, LANES), h0,
                cw, cb, dtb, alog, dsk, ssd_norm_g[li][None, :], e64, e128)

    wo = w_out[li].astype(BF16)
    g2 = norm2_g[li][None, :]
    x1, lgt = _outproj(x2, yssd.reshape(n_tok, D_SSD), scb, v, mod3, sc_conv_w[li], wo[:D_SSD], wo[D_SSD:],
                       g2, w_router[li].T, b_router[li][:, None])

    dest_t, gate_t, idx_t, meta = _route(lgt, n_blocks)
    dest_flat = dest_t.T.reshape(n_assign)
    cnt = meta[0, :N_EXPERTS]
    start = meta[1, :N_EXPERTS]
    blk_exp = meta[2, :n_blocks]
    n_used = meta[3, :1]

    meta_rows = lax.bitcast_convert_type(_pad_lanes(jnp.concatenate(
        [idx_t.T.astype(F32), gate_t.T, jnp.arange(n_tok, dtype=F32)[:, None]], axis=1)), jnp.uint32)
    pad_meta = lax.bitcast_convert_type(_pad_lanes(jnp.concatenate(
        [jnp.full((MOE_BM, TOP_K), -1.0, F32), jnp.zeros((MOE_BM, TOP_K), F32),
         (TOP_K * n_tok + jnp.arange(MOE_BM, dtype=F32))[:, None]], axis=1)), jnp.uint32)
    zsrc = jnp.concatenate([jnp.zeros((MOE_BM, PACK_W), jnp.uint32), pad_meta,
                            jnp.zeros((MOE_BM, D_MODEL - PACK_W - LANES), jnp.uint32)],
                           axis=1).reshape(MOE_BM, SUBLANES, LANES)

    xs = _dispatch(dest_flat, cnt, start, n_used, x1, meta_rows, mod3, g2, zsrc, n_rows)
    ytm = _experts(blk_exp, n_used, xs, w_gate_up[li], b_gate_up[li][:, None, :],
                   w_down[li], b_down[li][:, None, :], n_tok)
    out = _combine(ytm, x1, mod3, final_g[None, :])
    return out.reshape(bsz, SEQ, D_MODEL)
```

```python
import functools

import jax
import jax.numpy as jnp
from jax import lax
from jax.experimental import pallas as pl
from jax.experimental.pallas import tpu as pltpu

F32 = jnp.float32
BF16 = jnp.bfloat16
HIGHEST = lax.Precision.HIGHEST

D_MODEL = 1024
SEQ = 2048
CTX_LEN = 256
GRID_W = 64
D_SSD = 1024
D_SC = 1024
HEAD_DIM = 64
HEADS = 16
GROUPS = 2
STATE = 128
CHUNK = 128
N_EXPERTS = 32
TOP_K = 4
D_FF = 1024
SWIGLU_LIMIT = 7.0
SWIGLU_ALPHA = 1.702
NORM_EPS = 1e-6
N_MOD = 6
XBC_W = D_SSD + 2 * GROUPS * STATE
XB_W = D_SSD + GROUPS * STATE
LANES = 128

Z0 = 0
X0 = Z0 + D_SSD
B0 = X0 + D_SSD
C0 = B0 + GROUPS * STATE
DT0 = C0 + GROUPS * STATE
SC0 = DT0 + 2 * HEADS

TOK_TILE = 512
MOE_BM = 256
RT_TILE = 512
DISP_TILE = 256
COMB_TILE = 512
SUBLANES = 8
PACK_W = D_MODEL // 2
META_IDX = 0
META_GATE = TOP_K
META_TOK = 2 * TOP_K
VMEM_LIMIT = 56 * 1024 * 1024


def _silu(v):
    return v * jax.nn.sigmoid(v)


def _softplus(v):
    return jnp.maximum(v, 0.0) + jnp.log1p(jnp.exp(-jnp.abs(v)))


def _rms(v):
    return v * lax.rsqrt(jnp.mean(v * v, axis=-1, keepdims=True) + NORM_EPS)


def _dot(a, b):
    return jnp.dot(a, b, preferred_element_type=F32)


def _expand2(v, e2):
    hi = v.astype(BF16)
    lo = (v - hi.astype(F32)).astype(BF16)
    return _dot(jnp.concatenate([hi, lo], axis=1), e2)


def _mod_kernel(c_ref, w_ref, b_ref, o_ref):
    o_ref[...] = jnp.dot(_silu(c_ref[...]), w_ref[...], precision=HIGHEST,
                         preferred_element_type=F32) + b_ref[...]


def _mod(cvec, w_mod, b_mod):
    rows = cvec.shape[0]
    n = w_mod.shape[1]
    tn = 1536
    return pl.pallas_call(
        _mod_kernel,
        out_shape=jax.ShapeDtypeStruct((rows, n), F32),
        grid=(n // tn,),
        in_specs=[pl.BlockSpec((rows, D_MODEL), lambda j: (0, 0)),
                  pl.BlockSpec((D_MODEL, tn), lambda j: (0, j)),
                  pl.BlockSpec((1, tn), lambda j: (0, j))],
        out_specs=pl.BlockSpec((rows, tn), lambda j: (0, j)),
        compiler_params=pltpu.CompilerParams(dimension_semantics=("arbitrary",),
                                             vmem_limit_bytes=VMEM_LIMIT),
        name="mod",
    )(cvec, w_mod, b_mod)


def _ctx_kernel(ctx_ref, mod_ref, g1_ref, wxb_ref, wdt_ref, cw_ref, cb_ref, dtb_ref, alog_ref, e64_ref,
                h0_ref):
    L = CTX_LEN
    m = mod_ref[0]
    hc = _rms(ctx_ref[0]) * g1_ref[...] * (1.0 + m[1:2]) + m[0:1]
    hb = hc.astype(BF16)
    pxb = _dot(hb, wxb_ref[...])
    dtr = _dot(hb, wdt_ref[...])
    rowi = lax.broadcasted_iota(jnp.int32, (L, XB_W), 0)
    dn = jnp.where(rowi == 0, 0.0, pltpu.roll(pxb, 1, 0))
    up = jnp.where(rowi == L - 1, 0.0, pltpu.roll(pxb, L - 1, 0))
    cw = cw_ref[...]
    xb = _silu(cw[0:1] * dn + cw[1:2] * pxb + cw[2:3] * up + cb_ref[...])
    xs = xb[:, :D_SSD]
    bm = xb[:, D_SSD:].astype(BF16)
    dt = _softplus(dtr + dtb_ref[...])
    da = dt * (-jnp.exp(alog_ref[...]))
    ri = lax.broadcasted_iota(jnp.int32, (L, L), 0)
    ci = lax.broadcasted_iota(jnp.int32, (L, L), 1)
    for d in range(2):
        tri = (ci <= ri) if d == 0 else (ci >= ri)
        cum = jnp.dot(tri.astype(F32), da, precision=HIGHEST, preferred_element_type=F32)
        last = cum[L - 1:L] if d == 0 else cum[0:1]
        w_e = _expand2(jnp.exp(last - cum) * dt, e64_ref[d])
        xw = (xs * w_e).astype(BF16)
        for g in range(GROUPS):
            gw = D_SSD // GROUPS
            st = lax.dot_general(bm[:, g * STATE:(g + 1) * STATE], xw[:, g * gw:(g + 1) * gw],
                                 (((0,), (0,)), ((), ())), preferred_element_type=F32)
            h0_ref[0, d, :, g * gw:(g + 1) * gw] = st


def _ctx_states(ctx, mod3, g1, wxb, wdt, cw, cb, dtb, alog, e64):
    bsz = ctx.shape[0]
    mod_row = bsz
    const = lambda *shape: pl.BlockSpec(shape, lambda b: (0,) * len(shape))
    return pl.pallas_call(
        _ctx_kernel,
        out_shape=jax.ShapeDtypeStruct((bsz, 2, STATE, D_SSD), F32),
        grid=(bsz,),
        in_specs=[pl.BlockSpec((1, CTX_LEN, D_MODEL), lambda b: (b, 0, 0)),
                  pl.BlockSpec((1, N_MOD, D_MODEL), lambda b: (mod_row, 0, 0)),
                  const(1, D_MODEL), const(D_MODEL, XB_W), const(D_MODEL, LANES),
                  const(3, XB_W), const(1, XB_W), const(1, LANES), const(1, LANES),
                  const(2, 2 * LANES, D_SSD)],
        out_specs=pl.BlockSpec((1, 2, STATE, D_SSD), lambda b: (b, 0, 0, 0)),
        compiler_params=pltpu.CompilerParams(dimension_semantics=("arbitrary",),
                                             vmem_limit_bytes=VMEM_LIMIT),
        name="ctx_states",
    )(ctx, mod3, g1, wxb, wdt, cw, cb, dtb, alog, e64)


def _inproj_kernel(x_ref, mod_ref, g1_ref, wz_ref, wxbc_ref, wdt_ref, wb_ref, wc_ref, wu_ref,
                   z_ref, xbc_ref, dt_ref, scb_ref, v_ref):
    m = mod_ref[0]
    hx = _rms(x_ref[...]) * g1_ref[...] * (1.0 + m[1:2]) + m[0:1]
    hb = hx.astype(BF16)
    z_ref[...] = _dot(hb, wz_ref[...]).astype(BF16)
    xbc_ref[...] = _dot(hb, wxbc_ref[...]).astype(BF16)
    dt_ref[...] = _dot(hb, wdt_ref[...])
    scb_ref[...] = _dot(hb, wb_ref[...]).astype(BF16)
    v_ref[...] = (_dot(hb, wc_ref[...]) * _dot(hb, wu_ref[...])).astype(BF16)


def _inproj(x2, mod3, g1, wz, wxbc, wdt, wb, wc, wu):
    t = x2.shape[0]
    tm = TOK_TILE
    per_b = SEQ // tm
    const = lambda *shape: pl.BlockSpec(shape, lambda i: (0,) * len(shape))
    tile = lambda w: pl.BlockSpec((tm, w), lambda i: (i, 0))
    return pl.pallas_call(
        _inproj_kernel,
        out_shape=(jax.ShapeDtypeStruct((t, D_SSD), BF16), jax.ShapeDtypeStruct((t, XBC_W), BF16),
                   jax.ShapeDtypeStruct((t, LANES), F32), jax.ShapeDtypeStruct((t, D_SC), BF16),
                   jax.ShapeDtypeStruct((t, D_SC), BF16)),
        grid=(t // tm,),
        in_specs=[tile(D_MODEL),
                  pl.BlockSpec((1, N_MOD, D_MODEL), lambda i: (i // per_b, 0, 0)),
                  const(1, D_MODEL), const(D_MODEL, D_SSD), const(D_MODEL, XBC_W), const(D_MODEL, LANES),
                  const(D_MODEL, D_SC), const(D_MODEL, D_SC), const(D_MODEL, D_SC)],
        out_specs=(tile(D_SSD), tile(XBC_W), tile(LANES), tile(D_SC), tile(D_SC)),
        compiler_params=pltpu.CompilerParams(dimension_semantics=("arbitrary",),
                                             vmem_limit_bytes=VMEM_LIMIT),
        name="inproj",
    )(x2, mod3, g1, wz, wxbc, wdt, wb, wc, wu)


def _ssd_kernel(xbc_ref, z_ref, dt_ref, h0_ref, cw_ref, cb_ref, dtb_ref, alog_ref, dsk_ref, g_ref,
                e64_ref, e128_ref, o_ref, xc_ref, y_ref, s_ref):
    Q = CHUNK
    nck = SEQ // Q
    gw = D_SSD // GROUPS

    rowi = lax.broadcasted_iota(jnp.int32, (SUBLANES, XBC_W), 0)

    def conv_body(c, carry):
        r0 = pl.multiple_of(c * Q, Q)
        main = xbc_ref[0, pl.ds(r0, Q), :].astype(F32)
        pstart = pl.multiple_of(jnp.maximum(r0 - 16, 0), 16)
        nstart = pl.multiple_of(jnp.minimum(r0 + Q, SEQ - 16), 16)
        prev = xbc_ref[0, pl.ds(pstart, 16), :].astype(F32)[15:16]
        nxt = xbc_ref[0, pl.ds(nstart, 16), :].astype(F32)[0:1]
        prev = jnp.where(c > 0, prev, 0.0)
        nxt = jnp.where(c < nck - 1, nxt, 0.0)
        dn = pltpu.roll(main, 1, 0)
        up = pltpu.roll(main, Q - 1, 0)
        dn = jnp.concatenate([jnp.where(rowi == 0, prev, dn[0:SUBLANES]), dn[SUBLANES:]], axis=0)
        up = jnp.concatenate([up[:Q - SUBLANES], jnp.where(rowi == SUBLANES - 1, nxt, up[Q - SUBLANES:])], axis=0)
        cw = cw_ref[...]
        conv = cw[0:1] * dn + cw[1:2] * main + cw[2:3] * up + cb_ref[...]
        xc_ref[pl.ds(r0, Q), :] = _silu(conv).astype(BF16)
        return carry

    lax.fori_loop(0, nck, conv_body, 0)

    ri = lax.broadcasted_iota(jnp.int32, (Q, Q), 0)
    ci = lax.broadcasted_iota(jnp.int32, (Q, Q), 1)
    lane = lax.broadcasted_iota(jnp.int32, (Q, LANES), 1)
    a_neg = -jnp.exp(alog_ref[...])

    def chunk(c, d, first):
        r0 = pl.multiple_of(c * Q, Q)
        rows = pl.ds(r0, Q)
        xs_b = xc_ref[rows, 0:D_SSD]
        xs = xs_b.astype(F32)
        bm = xc_ref[rows, D_SSD:D_SSD + GROUPS * STATE]
        cm = xc_ref[rows, D_SSD + GROUPS * STATE:XBC_W]
        dt = _softplus(dt_ref[0, rows, :] + dtb_ref[...])
        da = dt * a_neg
        tri = (ci <= ri) if d == 0 else (ci >= ri)
        p0 = da.astype(BF16)
        r1 = da - p0.astype(F32)
        p1 = r1.astype(BF16)
        p2 = (r1 - p1.astype(F32)).astype(BF16)
        tri_b = jnp.where(tri, 1.0, 0.0).astype(BF16)
        cum = _dot(jnp.concatenate([tri_b, tri_b, tri_b], axis=1),
                   jnp.concatenate([p0, p1, p2], axis=0))
        sub_t = (cum - jnp.log(dt)).T
        last = cum[Q - 1:Q] if d == 0 else cum[0:1]
        ecum_e = _expand2(jnp.exp(cum), e64_ref[d])
        w_e = _expand2(jnp.exp(last - cum) * dt, e64_ref[d])
        colb = _expand2(cum, e128_ref[d])
        decay_e = ecum_e[Q - 1:Q] if d == 0 else ecum_e[0:1]

        gmat = [lax.dot_general(cm[:, g * STATE:(g + 1) * STATE], bm[:, g * STATE:(g + 1) * STATE],
                                (((1,), (1,)), ((), ())), preferred_element_type=F32)
                for g in range(GROUPS)]
        zero_b = jnp.zeros((Q, LANES), BF16)
        y_parts = []
        for p in range(HEADS // 2):
            g = (2 * p) // (HEADS // GROUPS)
            ms = []
            for hh in (2 * p, 2 * p + 1):
                seg = colb[:, hh * LANES:(hh + 1) * LANES] - sub_t[HEADS * d + hh:HEADS * d + hh + 1, :]
                ms.append((jnp.where(tri, jnp.exp(seg), 0.0) * gmat[g]).astype(BF16))
            mcat = jnp.concatenate(ms, axis=1)
            xp = xs_b[:, p * LANES:(p + 1) * LANES]
            rhs = jnp.concatenate([jnp.where(lane < HEAD_DIM, xp, zero_b),
                                   jnp.where(lane >= HEAD_DIM, xp, zero_b)], axis=0)
            y_parts.append(_dot(mcat, rhs))
        y_diag = jnp.concatenate(y_parts, axis=1)

        s_old = s_ref[d]
        s_bf = s_old.astype(BF16)
        y_off = jnp.concatenate(
            [_dot(cm[:, g * STATE:(g + 1) * STATE], s_bf[:, g * gw:(g + 1) * gw]) for g in range(GROUPS)],
            axis=1)
        y = y_diag + y_off * ecum_e

        xw = (xs * w_e).astype(BF16)
        upd = jnp.concatenate(
            [lax.dot_general(bm[:, g * STATE:(g + 1) * STATE], xw[:, g * gw:(g + 1) * gw],
                             (((0,), (0,)), ((), ())), preferred_element_type=F32) for g in range(GROUPS)],
            axis=1)
        s_ref[d] = s_old * decay_e + upd

        if first:
            y_ref[rows, :] = y + dsk_ref[...] * xs
        else:
            tot = y_ref[rows, :] + y
            zz = z_ref[0, rows, :].astype(F32)
            gz = tot * _silu(zz)
            outs = []
            for g in range(GROUPS):
                gg = gz[:, g * gw:(g + 1) * gw]
                outs.append(gg * lax.rsqrt(jnp.mean(gg * gg, axis=-1, keepdims=True) + NORM_EPS))
            o_ref[0, rows, :] = (jnp.concatenate(outs, axis=1) * g_ref[...]).astype(BF16)

    s_ref[...] = h0_ref[0]

    def first_half(i, carry):
        chunk(i, 0, True)
        chunk(nck - 1 - i, 1, True)
        return carry

    def second_half(i, carry):
        chunk(i, 0, False)
        chunk(nck - 1 - i, 1, False)
        return carry

    lax.fori_loop(0, nck // 2, first_half, 0)
    lax.fori_loop(nck // 2, nck, second_half, 0)


def _ssd(xbc3, z3, dt3, h0, cw, cb, dtb, alog, dsk, g, e64, e128):
    bsz = xbc3.shape[0]
    const = lambda *shape: pl.BlockSpec(shape, lambda b: (0,) * len(shape))
    seq = lambda w: pl.BlockSpec((1, SEQ, w), lambda b: (b, 0, 0))
    return pl.pallas_call(
        _ssd_kernel,
        out_shape=jax.ShapeDtypeStruct((bsz, SEQ, D_SSD), BF16),
        grid=(bsz,),
        in_specs=[seq(XBC_W), seq(D_SSD), seq(LANES),
                  pl.BlockSpec((1, 2, STATE, D_SSD), lambda b: (b, 0, 0, 0)),
                  const(3, XBC_W), const(1, XBC_W), const(1, LANES), const(1, LANES),
                  const(1, D_SSD), const(1, D_SSD), const(2, 2 * LANES, D_SSD),
                  const(2, 2 * LANES, HEADS * LANES)],
        out_specs=seq(D_SSD),
        scratch_shapes=[pltpu.VMEM((SEQ, XBC_W), BF16), pltpu.VMEM((SEQ, D_SSD), F32),
                        pltpu.VMEM((2, STATE, D_SSD), F32)],
        compiler_params=pltpu.CompilerParams(dimension_semantics=("arbitrary",),
                                             vmem_limit_bytes=VMEM_LIMIT),
        name="ssd",
    )(xbc3, z3, dt3, h0, cw, cb, dtb, alog, dsk, g, e64, e128)


def _outproj_kernel(x_ref, yssd_ref, scb_ref, v_ref, vp_ref, vn_ref, mod_ref, scw_ref, wo1_ref, wo2_ref,
                    g2_ref, wrt_ref, br_ref, x1_ref, lg_ref):
    tm = TOK_TILE
    per_b = SEQ // tm
    i = pl.program_id(0)
    first = (i % per_b) == 0
    last = (i % per_b) == per_b - 1
    m = mod_ref[0]
    v = v_ref[...].astype(F32)
    vp = jnp.where(first, 0.0, vp_ref[...].astype(F32))
    vn = jnp.where(last, 0.0, vn_ref[...].astype(F32))
    dn = jnp.concatenate([vp, v[:tm - GRID_W]], axis=0)
    up = jnp.concatenate([v[GRID_W:], vn], axis=0)
    scw = scw_ref[...]
    ysc = scb_ref[...].astype(F32) * (scw[0:1] * dn + scw[1:2] * v + scw[2:3] * up)
    out = _dot(yssd_ref[...], wo1_ref[...]) + _dot(ysc.astype(BF16), wo2_ref[...])
    x1 = x_ref[...] + m[2:3] * out
    x1_ref[...] = x1
    h2 = _rms(x1) * g2_ref[...] * (1.0 + m[4:5]) + m[3:4]
    h_hi = h2.astype(BF16)
    h_lo = (h2 - h_hi.astype(F32)).astype(BF16)
    lg_ref[...] = lax.dot_general(wrt_ref[...], jnp.concatenate([h_hi, h_lo, h_hi], axis=1),
                                  (((1,), (1,)), ((), ())), preferred_element_type=F32) + br_ref[...]


def _outproj(x2, yssd, scb, v, mod3, scw, wo1, wo2, g2, wrt, br):
    t = x2.shape[0]
    tm = TOK_TILE
    per_b = SEQ // tm
    r = tm // GRID_W
    nrow = t // GRID_W
    const = lambda *shape: pl.BlockSpec(shape, lambda i: (0,) * len(shape))
    tile = lambda w: pl.BlockSpec((tm, w), lambda i: (i, 0))
    return pl.pallas_call(
        _outproj_kernel,
        out_shape=(jax.ShapeDtypeStruct((t, D_MODEL), F32), jax.ShapeDtypeStruct((N_EXPERTS, t), F32)),
        grid=(t // tm,),
        in_specs=[tile(D_MODEL), tile(D_SSD), tile(D_SC), tile(D_SC),
                  pl.BlockSpec((GRID_W, D_SC), lambda i: (jnp.maximum(i * r - 1, 0), 0)),
                  pl.BlockSpec((GRID_W, D_SC), lambda i: (jnp.minimum((i + 1) * r, nrow - 1), 0)),
                  pl.BlockSpec((1, N_MOD, D_MODEL), lambda i: (i // per_b, 0, 0)),
                  const(3, D_SC), const(D_SSD, D_MODEL), const(D_SC, D_MODEL), const(1, D_MODEL),
                  const(N_EXPERTS, 3 * D_MODEL), const(N_EXPERTS, 1)],
        out_specs=(tile(D_MODEL), pl.BlockSpec((N_EXPERTS, tm), lambda i: (0, i))),
        compiler_params=pltpu.CompilerParams(dimension_semantics=("arbitrary",),
                                             vmem_limit_bytes=VMEM_LIMIT),
        name="outproj",
    )(x2, yssd, scb, v, v, v, mod3, scw, wo1, wo2, g2, wrt, br)


def _route_kernel(lg_ref, dest_ref, gate_ref, idx_ref, meta_ref, rank_ref, carry_ref, *, n_tok, n_blocks):
    tt = RT_TILE
    ne = N_EXPERTS
    eio = lax.broadcasted_iota(jnp.int32, (ne, tt), 0)
    si = lax.broadcasted_iota(jnp.int32, (tt, tt), 0)
    ti = lax.broadcasted_iota(jnp.int32, (tt, tt), 1)
    before = (si < ti).astype(BF16)
    carry_ref[...] = jnp.zeros_like(carry_ref)

    def tile_body(j, c):
        t0 = pl.multiple_of(j * tt, tt)
        l = lg_ref[:, pl.ds(t0, tt)]
        onehot = jnp.zeros((ne, tt), F32)
        tops, sels = [], []
        for _ in range(TOP_K):
            mx = jnp.max(l, axis=0, keepdims=True)
            idx = jnp.min(jnp.where(l == mx, eio, ne), axis=0, keepdims=True)
            sel = eio == idx
            l = jnp.where(sel, -jnp.inf, l)
            onehot = onehot + sel.astype(F32)
            tops.append(mx)
            sels.append(sel)
            idx_ref[pl.ds(len(tops) - 1, 1), pl.ds(t0, tt)] = idx
        ex = [jnp.exp(tv - tops[0]) for tv in tops]
        den = ex[0] + ex[1] + ex[2] + ex[3]
        prefix = _dot(onehot.astype(BF16), before) + carry_ref[:, 0:1]
        for k in range(TOP_K):
            gate_ref[pl.ds(k, 1), pl.ds(t0, tt)] = ex[k] / den
            rk = jnp.sum(jnp.where(sels[k], prefix, 0.0), axis=0, keepdims=True)
            rank_ref[pl.ds(k, 1), pl.ds(t0, tt)] = rk.astype(jnp.int32)
        carry_ref[...] = carry_ref[...] + jnp.sum(onehot, axis=1, keepdims=True)
        return c

    lax.fori_loop(0, n_tok // tt, tile_body, 0)

    counts = carry_ref[...]
    padded = jnp.floor((counts + (MOE_BM - 1)) * (1.0 / MOE_BM)) * MOE_BM
    er = lax.broadcasted_iota(jnp.int32, (ne, ne), 0)
    ec = lax.broadcasted_iota(jnp.int32, (ne, ne), 1)
    pad_start = jnp.dot((ec < er).astype(F32), padded, precision=HIGHEST, preferred_element_type=F32)
    pad_end = pad_start + padded

    def dest_body(j, c):
        t0 = pl.multiple_of(j * tt, tt)
        for k in range(TOP_K):
            idx = idx_ref[pl.ds(k, 1), pl.ds(t0, tt)]
            base = jnp.sum(jnp.where(eio == idx, pad_start[:, 0:1], 0.0), axis=0, keepdims=True)
            dest_ref[pl.ds(k, 1), pl.ds(t0, tt)] = base.astype(jnp.int32) + rank_ref[pl.ds(k, 1), pl.ds(t0, tt)]
        return c

    lax.fori_loop(0, n_tok // tt, dest_body, 0)

    width = meta_ref.shape[1]
    sub = lax.broadcasted_iota(jnp.int32, (ne, width), 0)
    lan = lax.broadcasted_iota(jnp.int32, (ne, width), 1)
    diag = sub == lan
    cnt_row = jnp.sum(jnp.where(diag, counts[:, 0:1], 0.0), axis=0, keepdims=True)
    start_row = jnp.sum(jnp.where(diag, pad_start[:, 0:1], 0.0), axis=0, keepdims=True)
    blk_start = (lan * MOE_BM).astype(F32)
    blk_exp = jnp.sum((pad_end[:, 0:1] <= blk_start).astype(F32), axis=0, keepdims=True)
    blk_exp = jnp.minimum(blk_exp, float(ne - 1))
    used = jnp.sum(padded[:, 0:1], axis=0, keepdims=True) * (1.0 / MOE_BM)
    meta_ref[0:1, :] = cnt_row.astype(jnp.int32)
    meta_ref[1:2, :] = start_row.astype(jnp.int32)
    meta_ref[2:3, :] = blk_exp.astype(jnp.int32)
    meta_ref[3:4, :] = jnp.broadcast_to(used, (1, width)).astype(jnp.int32)
    meta_ref[4:8, :] = jnp.zeros((4, width), jnp.int32)


def _route(lgt, n_blocks):
    ne, n_tok = lgt.shape
    width = -(-n_blocks // LANES) * LANES
    full = lambda *shape: pl.BlockSpec(shape, lambda: (0,) * len(shape))
    return pl.pallas_call(
        functools.partial(_route_kernel, n_tok=n_tok, n_blocks=n_blocks),
        out_shape=(jax.ShapeDtypeStruct((TOP_K, n_tok), jnp.int32),
                   jax.ShapeDtypeStruct((TOP_K, n_tok), F32),
                   jax.ShapeDtypeStruct((TOP_K, n_tok), jnp.int32),
                   jax.ShapeDtypeStruct((8, width), jnp.int32)),
        in_specs=[full(ne, n_tok)],
        out_specs=(full(TOP_K, n_tok), full(TOP_K, n_tok), full(TOP_K, n_tok), full(8, width)),
        scratch_shapes=[pltpu.VMEM((TOP_K, n_tok), jnp.int32), pltpu.VMEM((ne, LANES), F32)],
        compiler_params=pltpu.CompilerParams(vmem_limit_bytes=VMEM_LIMIT),
        name="route",
    )(lgt)


def _dispatch_kernel(dest_ref, cnt_ref, start_ref, nu_ref, x1_ref, meta_ref, mod_ref, g2_ref, zsrc_ref, xs_ref,
                     hbuf, sem, zsem):
    i = pl.program_id(0)
    n = pl.num_programs(0)
    tl = DISP_TILE
    slot = i % 2
    nb = xs_ref.shape[0] // MOE_BM

    def zero_block(b):
        return pltpu.make_async_copy(zsrc_ref, xs_ref.at[pl.ds(b * MOE_BM, MOE_BM)], zsem)

    @pl.when(i == 0)
    def _():
        def start_e(e, c):
            @pl.when(cnt_ref[e] > 0)
            def _():
                zero_block((start_ref[e] + cnt_ref[e] - 1) // MOE_BM).start()
            return c

        def wait_e(e, c):
            @pl.when(cnt_ref[e] > 0)
            def _():
                zero_block(0).wait()
            return c

        def start_t(b, c):
            zero_block(b).start()
            return c

        def wait_t(b, c):
            zero_block(0).wait()
            return c

        lax.fori_loop(0, N_EXPERTS, start_e, 0)
        lax.fori_loop(nu_ref[0], nb, start_t, 0)
        lax.fori_loop(0, N_EXPERTS, wait_e, 0)
        lax.fori_loop(nu_ref[0], nb, wait_t, 0)

    m = mod_ref[0]
    h2 = _rms(x1_ref[...]) * g2_ref[...] * (1.0 + m[4:5]) + m[3:4]
    lo = pltpu.bitcast(h2[:, :PACK_W].astype(BF16).astype(F32), jnp.uint32) >> 16
    hi = pltpu.bitcast(h2[:, PACK_W:].astype(BF16).astype(F32), jnp.uint32) & jnp.uint32(0xFFFF0000)
    row = jnp.concatenate([lo | hi, meta_ref[...], jnp.zeros((tl, D_MODEL - PACK_W - LANES), jnp.uint32)], axis=1)
    hbuf[slot] = row.reshape(tl, SUBLANES, LANES)

    def per_tok(t, c):
        tok = i * tl + t
        for k in range(TOP_K):
            d = dest_ref[tok * TOP_K + k]
            pltpu.make_async_copy(hbuf.at[slot, t], xs_ref.at[d], sem.at[slot]).start(priority=k % 2)
        return c

    lax.fori_loop(0, tl, per_tok, 0, unroll=8)

    def wait_slot(sl):
        for _ in range(TOP_K):
            pltpu.make_async_copy(hbuf.at[sl], xs_ref.at[pl.ds(0, tl)], sem.at[sl]).wait()

    @pl.when(i > 0)
    def _():
        wait_slot(1 - slot)

    @pl.when(i == n - 1)
    def _():
        wait_slot(slot)


def _dispatch(dest_flat, cnt, start, n_used, x1, meta_rows, mod3, g2, zsrc, n_rows):
    n_tok = x1.shape[0]
    tl = DISP_TILE
    per_b = SEQ // tl
    return pl.pallas_call(
        _dispatch_kernel,
        out_shape=jax.ShapeDtypeStruct((n_rows, SUBLANES, LANES), jnp.uint32),
        grid_spec=pltpu.PrefetchScalarGridSpec(
            num_scalar_prefetch=4,
            grid=(n_tok // tl,),
            in_specs=[pl.BlockSpec((tl, D_MODEL), lambda i, *_: (i, 0)),
                      pl.BlockSpec((tl, LANES), lambda i, *_: (i, 0)),
                      pl.BlockSpec((1, N_MOD, D_MODEL), lambda i, *_: (i // per_b, 0, 0)),
                      pl.BlockSpec((1, D_MODEL), lambda i, *_: (0, 0)),
                      pl.BlockSpec((MOE_BM, SUBLANES, LANES), lambda i, *_: (0, 0, 0))],
            out_specs=pl.BlockSpec(memory_space=pl.ANY),
            scratch_shapes=[pltpu.VMEM((2, tl, SUBLANES, LANES), jnp.uint32),
                            pltpu.SemaphoreType.DMA((2,)), pltpu.SemaphoreType.DMA]),
        compiler_params=pltpu.CompilerParams(dimension_semantics=("arbitrary",),
                                             vmem_limit_bytes=VMEM_LIMIT),
        name="dispatch",
    )(dest_flat, cnt, start, n_used, x1, meta_rows, mod3, g2, zsrc)


def _expert_kernel(be_ref, nu_ref, xs_ref, wgu_hbm, bgu_ref, wd_hbm, bd_ref, zero_ref, ytm_ref,
                   big, idv, ids, wgu_raw, wd_raw, wgu_bf, wd_bf, sc_sem, id_sem, z_sem, w_sem, *, n_tok):
    j = pl.program_id(0)
    nbk = pl.num_programs(0) - 1
    nu = nu_ref[0]
    slot = j % 2
    prev = 1 - slot

    def weight_copies(e):
        return (pltpu.make_async_copy(wgu_hbm.at[e], wgu_raw, w_sem.at[0]),
                pltpu.make_async_copy(wd_hbm.at[e], wd_raw, w_sem.at[1]))

    @pl.when(j == 0)
    def _():
        cp = pltpu.make_async_copy(zero_ref, ytm_ref.at[pl.ds(TOP_K * n_tok, MOE_BM)], z_sem)
        cp.start()
        for w in weight_copies(be_ref[0]):
            w.start()
        cp.wait()

    e_now = be_ref[jnp.minimum(j, nbk - 1)]
    new_expert = jnp.logical_or(j == 0, be_ref[jnp.maximum(jnp.minimum(j, nbk - 1) - 1, 0)] != e_now)

    @pl.when(jnp.logical_and(j < nu, new_expert))
    def _():
        for w in weight_copies(e_now):
            w.wait()
        n_cc = 8
        for c in range(n_cc):
            cc = slice(c * (2 * D_FF // n_cc), (c + 1) * (2 * D_FF // n_cc))
            wgu_bf[:, cc] = wgu_raw[:, cc].astype(BF16)
        for c in range(n_cc // 2):
            cc = slice(c * (2 * D_MODEL // n_cc), (c + 1) * (2 * D_MODEL // n_cc))
            wd_bf[:, cc] = wd_raw[:, cc].astype(BF16)
        j_next = lax.while_loop(lambda t: jnp.logical_and(t < nu, be_ref[jnp.minimum(t, nbk - 1)] == e_now),
                                lambda t: t + 1, j + 1)

        @pl.when(j_next < nu)
        def _():
            for w in weight_copies(be_ref[jnp.minimum(j_next, nbk - 1)]):
                w.start()

    n_pc = 4
    pw1 = D_FF // n_pc
    pw2 = D_MODEL // n_pc
    n_lt = D_MODEL // LANES
    spb = MOE_BM // SUBLANES
    groups = [3 * MOE_BM // 16] * n_pc + [MOE_BM // 16] * n_pc
    assert sum(groups) == MOE_BM

    def slab_store(unit, col0, val):
        for s in range(val.shape[1] // LANES):
            t0 = unit * MOE_BM + (col0 // LANES + s) * spb
            big[pl.ds(t0, spb)] = val[:, s * LANES:(s + 1) * LANES].reshape(spb, SUBLANES, LANES)

    def slab_load(unit, row0, n):
        return jnp.concatenate(
            [big[pl.ds(unit * MOE_BM + s * spb + row0 // SUBLANES, n // SUBLANES)].reshape(n, LANES)
             for s in range(n_lt)], axis=1)

    def scatter_group(g):
        lo = sum(groups[:g])
        n = groups[g]
        big[pl.ds(prev * MOE_BM + lo, n)] = slab_load(3 + prev, lo, n).reshape(n, n_lt, LANES)
        for r in range(lo, lo + n):
            pltpu.make_async_copy(big.at[prev * MOE_BM + r], ytm_ref.at[ids[prev, 0, r]],
                                  sc_sem.at[prev]).start(priority=r % 2)

    def scatter_wait(sl):
        pltpu.make_async_copy(big.at[pl.ds(0, MOE_BM)], ytm_ref.at[pl.ds(0, MOE_BM)], sc_sem.at[sl]).wait()

    def compute(with_scatter):
        e_f = be_ref[jnp.minimum(j, nbk - 1)].astype(F32)
        words = xs_ref[...].reshape(MOE_BM, D_MODEL)
        packed = words[:, 0:PACK_W]
        meta = pltpu.bitcast(words[:, PACK_W:PACK_W + LANES], F32)
        xb = jnp.concatenate(
            [pltpu.bitcast(packed << 16, F32).astype(BF16),
             pltpu.bitcast(packed & jnp.uint32(0xFFFF0000), F32).astype(BF16)], axis=1)
        gate = jnp.zeros((MOE_BM, 1), F32)
        kk = jnp.zeros((MOE_BM, 1), F32)
        for k in range(TOP_K):
            mk = meta[:, META_IDX + k:META_IDX + k + 1] == e_f
            gate = gate + jnp.where(mk, meta[:, META_GATE + k:META_GATE + k + 1], 0.0)
            kk = kk + jnp.where(mk, float(k), 0.0)
        row = kk * float(n_tok) + meta[:, META_TOK:META_TOK + 1]
        row_t = jnp.broadcast_to(row, (MOE_BM, LANES)).T
        idv[slot] = row_t[0:8].astype(jnp.int32)
        pltpu.make_async_copy(idv.at[slot], ids.at[slot], id_sem.at[slot]).start()
        for c in range(n_pc):
            if with_scatter:
                scatter_group(c)
            cg = slice(c * pw1, (c + 1) * pw1)
            cl = slice(D_FF + c * pw1, D_FF + (c + 1) * pw1)
            glu = jnp.minimum(_dot(xb, wgu_bf[:, cg]) + bgu_ref[0, :, cg], SWIGLU_LIMIT)
            lin = jnp.clip(_dot(xb, wgu_bf[:, cl]) + bgu_ref[0, :, cl], -SWIGLU_LIMIT, SWIGLU_LIMIT)
            slab_store(2, c * pw1, glu * jax.nn.sigmoid(SWIGLU_ALPHA * glu) * (lin + 1.0))
        act = slab_load(2, 0, MOE_BM).astype(BF16)
        for c in range(n_pc):
            if with_scatter:
                scatter_group(n_pc + c)
            cs = slice(c * pw2, (c + 1) * pw2)
            slab_store(3 + slot, c * pw2, (_dot(act, wd_bf[:, cs]) + bd_ref[0, :, cs]) * gate)

    def ids_wait():
        pltpu.make_async_copy(idv.at[prev], ids.at[prev], id_sem.at[prev]).wait()

    @pl.when(jnp.logical_and(j >= 3, j <= nu))
    def _():
        scatter_wait(prev)

    @pl.when(j == 0)
    def _():
        compute(False)

    @pl.when(jnp.logical_and(j >= 1, j < nu))
    def _():
        ids_wait()
        compute(True)

    @pl.when(j == nu)
    def _():
        ids_wait()
        for g in range(len(groups)):
            scatter_group(g)
        scatter_wait(prev)

    @pl.when(jnp.logical_and(j == nu, j >= 2))
    def _():
        scatter_wait(slot)


def _experts(blk_exp, n_used, xs, wgu, bgu, wd, bd, n_tok):
    n_rows = xs.shape[0]
    nb = n_rows // MOE_BM
    row_blk = lambda j, be, nu: (jnp.minimum(j, nu[0] - 1), 0, 0)
    per_e = lambda j, be, nu: (be[jnp.minimum(j, nb - 1)], 0, 0)
    sub = D_MODEL // LANES
    zero = jnp.zeros((MOE_BM, sub, LANES), F32)
    return pl.pallas_call(
        functools.partial(_expert_kernel, n_tok=n_tok),
        out_shape=jax.ShapeDtypeStruct((TOP_K * n_tok + MOE_BM, sub, LANES), F32),
        grid_spec=pltpu.PrefetchScalarGridSpec(
            num_scalar_prefetch=2,
            grid=(nb + 1,),
            in_specs=[pl.BlockSpec((MOE_BM, SUBLANES, LANES), row_blk),
                      pl.BlockSpec(memory_space=pl.ANY),
                      pl.BlockSpec((1, 1, 2 * D_FF), per_e),
                      pl.BlockSpec(memory_space=pl.ANY),
                      pl.BlockSpec((1, 1, D_MODEL), per_e),
                      pl.BlockSpec((MOE_BM, sub, LANES), lambda j, be, nu: (0, 0, 0))],
            out_specs=pl.BlockSpec(memory_space=pl.ANY),
            scratch_shapes=[pltpu.VMEM((5 * MOE_BM, sub, LANES), F32), pltpu.VMEM((2, 8, MOE_BM), jnp.int32),
                            pltpu.SMEM((2, 8, MOE_BM), jnp.int32),
                            pltpu.VMEM((D_MODEL, 2 * D_FF), F32), pltpu.VMEM((D_FF, D_MODEL), F32),
                            pltpu.VMEM((D_MODEL, 2 * D_FF), BF16), pltpu.VMEM((D_FF, D_MODEL), BF16),
                            pltpu.SemaphoreType.DMA((2,)), pltpu.SemaphoreType.DMA((2,)),
                            pltpu.SemaphoreType.DMA, pltpu.SemaphoreType.DMA((2,))]),
        compiler_params=pltpu.CompilerParams(dimension_semantics=("arbitrary",),
                                             vmem_limit_bytes=VMEM_LIMIT),
        name="experts",
    )(blk_exp, n_used, xs, wgu, bgu, wd, bd, zero)


def _combine_kernel(y0_ref, y1_ref, y2_ref, y3_ref, x1_ref, mod_ref, fg_ref, o_ref):
    m = mod_ref[0]
    moe = ((y0_ref[...] + y1_ref[...]) + (y2_ref[...] + y3_ref[...])).reshape(COMB_TILE, D_MODEL)
    x2 = x1_ref[...] + m[5:6] * moe
    o_ref[...] = _rms(x2) * fg_ref[...]


def _combine(ytm, x1, mod3, fg):
    n_tok = x1.shape[0]
    tc = COMB_TILE
    per_b = SEQ // tc
    nt = n_tok // tc
    slot_spec = lambda k: pl.BlockSpec((tc, D_MODEL // LANES, LANES), lambda i: (i + k * nt, 0, 0))
    return pl.pallas_call(
        _combine_kernel,
        out_shape=jax.ShapeDtypeStruct((n_tok, D_MODEL), F32),
        grid=(nt,),
        in_specs=[slot_spec(0), slot_spec(1), slot_spec(2), slot_spec(3),
                  pl.BlockSpec((tc, D_MODEL), lambda i: (i, 0)),
                  pl.BlockSpec((1, N_MOD, D_MODEL), lambda i: (i // per_b, 0, 0)),
                  pl.BlockSpec((1, D_MODEL), lambda i: (0, 0))],
        out_specs=pl.BlockSpec((tc, D_MODEL), lambda i: (i, 0)),
        compiler_params=pltpu.CompilerParams(dimension_semantics=("arbitrary",),
                                             vmem_limit_bytes=VMEM_LIMIT),
        name="combine",
    )(ytm, ytm, ytm, ytm, x1, mod3, fg)


def _expansion_matrices():
    r = (jnp.arange(2 * LANES) % LANES)[:, None]
    out64, out128 = [], []
    for d in range(2):
        l64 = jnp.arange(D_SSD)[None, :]
        l128 = jnp.arange(HEADS * LANES)[None, :]
        out64.append((l64 // HEAD_DIM == r - HEADS * d).astype(BF16))
        out128.append((l128 // LANES == r - HEADS * d).astype(BF16))
    return jnp.stack(out64), jnp.stack(out128)


def _pad_lanes(v):
    return jnp.pad(v, [(0, 0)] * (v.ndim - 1) + [(0, LANES - v.shape[-1])])


def kernel(x, c, ctx, c_ctx, w_mod, b_mod, norm1_g, w_in, ssd_conv_w, ssd_conv_b, ssd_dt_bias, ssd_a_log,
           ssd_d, ssd_norm_g, sc_conv_w, w_out, norm2_g, w_router, b_router, w_gate_up, b_gate_up, w_down,
           b_down, final_g):
    bsz = x.shape[0]
    n_tok = bsz * SEQ
    n_assign = n_tok * TOP_K
    n_blocks = n_assign // MOE_BM + N_EXPERTS
    n_rows = n_blocks * MOE_BM
    li = 0

    cvec = jnp.concatenate([c, c_ctx[None, :], jnp.zeros((7, D_MODEL), F32)], axis=0)
    mod3 = _mod(cvec, w_mod[li], b_mod[li][None, :]).reshape(bsz + 8, N_MOD, D_MODEL)

    w = w_in[li]
    wz = w[:, Z0:X0].astype(BF16)
    wxbc = w[:, X0:DT0].astype(BF16)
    wdt = _pad_lanes(w[:, DT0:SC0]).astype(BF16)
    wb = w[:, SC0:SC0 + D_SC].astype(BF16)
    wc = w[:, SC0 + D_SC:SC0 + 2 * D_SC].astype(BF16)
    wu = w[:, SC0 + 2 * D_SC:].astype(BF16)
    g1 = norm1_g[li][None, :]
    cw = ssd_conv_w[li]
    cb = ssd_conv_b[li][None, :]
    dtb = _pad_lanes(ssd_dt_bias[li].reshape(1, 2 * HEADS))
    alog = _pad_lanes(ssd_a_log[li].reshape(1, 2 * HEADS))
    e64, e128 = _expansion_matrices()

    h0 = _ctx_states(ctx, mod3, g1, wxbc[:, :XB_W], wdt, cw[:, :XB_W], cb[:, :XB_W], dtb, alog, e64)

    x2 = x.reshape(n_tok, D_MODEL)
    z, xbc, dtr, scb, v = _inproj(x2, mod3, g1, wz, wxbc, wdt, wb, wc, wu)

    dsk = jnp.repeat(ssd_d[li], HEAD_DIM)[None, :]
    yssd = _ssd(xbc.reshape(bsz, SEQ, XBC_W), z.reshape(bsz, SEQ, D_SSD), dtr.reshape(bsz, SEQ, LANES), h0,
                cw, cb, dtb, alog, dsk, ssd_norm_g[li][None, :], e64, e128)

    wo = w_out[li].astype(BF16)
    g2 = norm2_g[li][None, :]
    wr = w_router[li].T
    wr_hi = wr.astype(BF16)
    wr_lo = (wr - wr_hi.astype(F32)).astype(BF16)
    x1, lgt = _outproj(x2, yssd.reshape(n_tok, D_SSD), scb, v, mod3, sc_conv_w[li], wo[:D_SSD], wo[D_SSD:],
                       g2, jnp.concatenate([wr_hi, wr_hi, wr_lo], axis=1), b_router[li][:, None])

    dest_t, gate_t, idx_t, meta = _route(lgt, n_blocks)
    dest_flat = dest_t.T.reshape(n_assign)
    cnt = meta[0, :N_EXPERTS]
    start = meta[1, :N_EXPERTS]
    blk_exp = meta[2, :n_blocks]
    n_used = meta[3, :1]

    meta_rows = lax.bitcast_convert_type(_pad_lanes(jnp.concatenate(
        [idx_t.T.astype(F32), gate_t.T, jnp.arange(n_tok, dtype=F32)[:, None]], axis=1)), jnp.uint32)
    pad_meta = lax.bitcast_convert_type(_pad_lanes(jnp.concatenate(
        [jnp.full((MOE_BM, TOP_K), -1.0, F32), jnp.zeros((MOE_BM, TOP_K), F32),
         (TOP_K * n_tok + jnp.arange(MOE_BM, dtype=F32))[:, None]], axis=1)), jnp.uint32)
    zsrc = jnp.concatenate([jnp.zeros((MOE_BM, PACK_W), jnp.uint32), pad_meta,
                            jnp.zeros((MOE_BM, D_MODEL - PACK_W - LANES), jnp.uint32)],
                           axis=1).reshape(MOE_BM, SUBLANES, LANES)

    xs = _dispatch(dest_flat, cnt, start, n_used, x1, meta_rows, mod3, g2, zsrc, n_rows)
    ytm = _experts(blk_exp, n_used, xs, w_gate_up[li], b_gate_up[li][:, None, :],
                   w_down[li], b_down[li][:, None, :], n_tok)
    out = _combine(ytm, x1, mod3, final_g[None, :])
    return out.reshape(bsz, SEQ, D_MODEL)
```

```python
import functools

import jax
import jax.numpy as jnp
from jax import lax
from jax.experimental import pallas as pl
from jax.experimental.pallas import tpu as pltpu

F32 = jnp.float32
BF16 = jnp.bfloat16
HIGHEST = lax.Precision.HIGHEST

D_MODEL = 1024
SEQ = 2048
CTX_LEN = 256
GRID_W = 64
D_SSD = 1024
D_SC = 1024
HEAD_DIM = 64
HEADS = 16
GROUPS = 2
STATE = 128
CHUNK = 128
N_EXPERTS = 32
TOP_K = 4
D_FF = 1024
SWIGLU_LIMIT = 7.0
SWIGLU_ALPHA = 1.702
NORM_EPS = 1e-6
N_MOD = 6
XBC_W = D_SSD + 2 * GROUPS * STATE
XB_W = D_SSD + GROUPS * STATE
LANES = 128

Z0 = 0
X0 = Z0 + D_SSD
B0 = X0 + D_SSD
C0 = B0 + GROUPS * STATE
DT0 = C0 + GROUPS * STATE
SC0 = DT0 + 2 * HEADS

TOK_TILE = 512
MOE_BM = 256
RT_TILE = 512
DISP_TILE = 256
COMB_TILE = 512
SUBLANES = 8
PACK_W = D_MODEL // 2
META_IDX = 0
META_GATE = TOP_K
META_TOK = 2 * TOP_K
VMEM_LIMIT = 56 * 1024 * 1024


def _silu(v):
    return v * jax.nn.sigmoid(v)


def _softplus(v):
    return jnp.maximum(v, 0.0) + jnp.log1p(jnp.exp(-jnp.abs(v)))


def _rms(v):
    return v * lax.rsqrt(jnp.mean(v * v, axis=-1, keepdims=True) + NORM_EPS)


def _dot(a, b):
    return jnp.dot(a, b, preferred_element_type=F32)


def _expand2(v, e2):
    hi = v.astype(BF16)
    lo = (v - hi.astype(F32)).astype(BF16)
    return _dot(jnp.concatenate([hi, lo], axis=1), e2)


def _mod_kernel(c_ref, w_ref, b_ref, o_ref):
    o_ref[...] = jnp.dot(_silu(c_ref[...]), w_ref[...], precision=HIGHEST,
                         preferred_element_type=F32) + b_ref[...]


def _mod(cvec, w_mod, b_mod):
    rows = cvec.shape[0]
    n = w_mod.shape[1]
    tn = 1536
    return pl.pallas_call(
        _mod_kernel,
        out_shape=jax.ShapeDtypeStruct((rows, n), F32),
        grid=(n // tn,),
        in_specs=[pl.BlockSpec((rows, D_MODEL), lambda j: (0, 0)),
                  pl.BlockSpec((D_MODEL, tn), lambda j: (0, j)),
                  pl.BlockSpec((1, tn), lambda j: (0, j))],
        out_specs=pl.BlockSpec((rows, tn), lambda j: (0, j)),
        compiler_params=pltpu.CompilerParams(dimension_semantics=("arbitrary",),
                                             vmem_limit_bytes=VMEM_LIMIT),
        name="mod",
    )(cvec, w_mod, b_mod)


def _ctx_kernel(ctx_ref, mod_ref, g1_ref, wxb_ref, wdt_ref, cw_ref, cb_ref, dtb_ref, alog_ref, e64_ref,
                h0_ref):
    L = CTX_LEN
    m = mod_ref[0]
    hc = _rms(ctx_ref[0]) * g1_ref[...] * (1.0 + m[1:2]) + m[0:1]
    hb = hc.astype(BF16)
    pxb = _dot(hb, wxb_ref[...])
    dtr = _dot(hb, wdt_ref[...])
    rowi = lax.broadcasted_iota(jnp.int32, (L, XB_W), 0)
    dn = jnp.where(rowi == 0, 0.0, pltpu.roll(pxb, 1, 0))
    up = jnp.where(rowi == L - 1, 0.0, pltpu.roll(pxb, L - 1, 0))
    cw = cw_ref[...]
    xb = _silu(cw[0:1] * dn + cw[1:2] * pxb + cw[2:3] * up + cb_ref[...])
    xs = xb[:, :D_SSD]
    bm = xb[:, D_SSD:].astype(BF16)
    dt = _softplus(dtr + dtb_ref[...])
    da = dt * (-jnp.exp(alog_ref[...]))
    ri = lax.broadcasted_iota(jnp.int32, (L, L), 0)
    ci = lax.broadcasted_iota(jnp.int32, (L, L), 1)
    for d in range(2):
        tri = (ci <= ri) if d == 0 else (ci >= ri)
        cum = jnp.dot(tri.astype(F32), da, precision=HIGHEST, preferred_element_type=F32)
        last = cum[L - 1:L] if d == 0 else cum[0:1]
        w_e = _expand2(jnp.exp(last - cum) * dt, e64_ref[d])
        xw = (xs * w_e).astype(BF16)
        for g in range(GROUPS):
            gw = D_SSD // GROUPS
            st = lax.dot_general(bm[:, g * STATE:(g + 1) * STATE], xw[:, g * gw:(g + 1) * gw],
                                 (((0,), (0,)), ((), ())), preferred_element_type=F32)
            h0_ref[0, d, :, g * gw:(g + 1) * gw] = st


def _ctx_states(ctx, mod3, g1, wxb, wdt, cw, cb, dtb, alog, e64):
    bsz = ctx.shape[0]
    mod_row = bsz
    const = lambda *shape: pl.BlockSpec(shape, lambda b: (0,) * len(shape))
    return pl.pallas_call(
        _ctx_kernel,
        out_shape=jax.ShapeDtypeStruct((bsz, 2, STATE, D_SSD), F32),
        grid=(bsz,),
        in_specs=[pl.BlockSpec((1, CTX_LEN, D_MODEL), lambda b: (b, 0, 0)),
                  pl.BlockSpec((1, N_MOD, D_MODEL), lambda b: (mod_row, 0, 0)),
                  const(1, D_MODEL), const(D_MODEL, XB_W), const(D_MODEL, LANES),
                  const(3, XB_W), const(1, XB_W), const(1, LANES), const(1, LANES),
                  const(2, 2 * LANES, D_SSD)],
        out_specs=pl.BlockSpec((1, 2, STATE, D_SSD), lambda b: (b, 0, 0, 0)),
        compiler_params=pltpu.CompilerParams(dimension_semantics=("arbitrary",),
                                             vmem_limit_bytes=VMEM_LIMIT),
        name="ctx_states",
    )(ctx, mod3, g1, wxb, wdt, cw, cb, dtb, alog, e64)


def _inproj_kernel(x_ref, mod_ref, g1_ref, wz_ref, wxbc_ref, wdt_ref, wb_ref, wc_ref, wu_ref,
                   z_ref, xbc_ref, dt_ref, scb_ref, v_ref):
    m = mod_ref[0]
    hx = _rms(x_ref[...]) * g1_ref[...] * (1.0 + m[1:2]) + m[0:1]
    hb = hx.astype(BF16)
    z_ref[...] = _dot(hb, wz_ref[...]).astype(BF16)
    xbc_ref[...] = _dot(hb, wxbc_ref[...]).astype(BF16)
    dt_ref[...] = _dot(hb, wdt_ref[...])
    scb_ref[...] = _dot(hb, wb_ref[...]).astype(BF16)
    v_ref[...] = (_dot(hb, wc_ref[...]) * _dot(hb, wu_ref[...])).astype(BF16)


def _inproj(x2, mod3, g1, wz, wxbc, wdt, wb, wc, wu):
    t = x2.shape[0]
    tm = TOK_TILE
    per_b = SEQ // tm
    const = lambda *shape: pl.BlockSpec(shape, lambda i: (0,) * len(shape))
    tile = lambda w: pl.BlockSpec((tm, w), lambda i: (i, 0))
    return pl.pallas_call(
        _inproj_kernel,
        out_shape=(jax.ShapeDtypeStruct((t, D_SSD), BF16), jax.ShapeDtypeStruct((t, XBC_W), BF16),
                   jax.ShapeDtypeStruct((t, LANES), F32), jax.ShapeDtypeStruct((t, D_SC), BF16),
                   jax.ShapeDtypeStruct((t, D_SC), BF16)),
        grid=(t // tm,),
        in_specs=[tile(D_MODEL),
                  pl.BlockSpec((1, N_MOD, D_MODEL), lambda i: (i // per_b, 0, 0)),
                  const(1, D_MODEL), const(D_MODEL, D_SSD), const(D_MODEL, XBC_W), const(D_MODEL, LANES),
                  const(D_MODEL, D_SC), const(D_MODEL, D_SC), const(D_MODEL, D_SC)],
        out_specs=(tile(D_SSD), tile(XBC_W), tile(LANES), tile(D_SC), tile(D_SC)),
        compiler_params=pltpu.CompilerParams(dimension_semantics=("arbitrary",),
                                             vmem_limit_bytes=VMEM_LIMIT),
        name="inproj",
    )(x2, mod3, g1, wz, wxbc, wdt, wb, wc, wu)


def _ssd_kernel(xbc_ref, z_ref, dt_ref, h0_ref, cw_ref, cb_ref, dtb_ref, alog_ref, dsk_ref, g_ref,
                e64_ref, e128_ref, o_ref, xc_ref, y_ref, s_ref):
    Q = CHUNK
    nck = SEQ // Q
    gw = D_SSD // GROUPS

    rowi = lax.broadcasted_iota(jnp.int32, (SUBLANES, XBC_W), 0)

    def conv_body(c, carry):
        r0 = pl.multiple_of(c * Q, Q)
        main = xbc_ref[0, pl.ds(r0, Q), :].astype(F32)
        pstart = pl.multiple_of(jnp.maximum(r0 - 16, 0), 16)
        nstart = pl.multiple_of(jnp.minimum(r0 + Q, SEQ - 16), 16)
        prev = xbc_ref[0, pl.ds(pstart, 16), :].astype(F32)[15:16]
        nxt = xbc_ref[0, pl.ds(nstart, 16), :].astype(F32)[0:1]
        prev = jnp.where(c > 0, prev, 0.0)
        nxt = jnp.where(c < nck - 1, nxt, 0.0)
        dn = pltpu.roll(main, 1, 0)
        up = pltpu.roll(main, Q - 1, 0)
        dn = jnp.concatenate([jnp.where(rowi == 0, prev, dn[0:SUBLANES]), dn[SUBLANES:]], axis=0)
        up = jnp.concatenate([up[:Q - SUBLANES], jnp.where(rowi == SUBLANES - 1, nxt, up[Q - SUBLANES:])], axis=0)
        cw = cw_ref[...]
        conv = cw[0:1] * dn + cw[1:2] * main + cw[2:3] * up + cb_ref[...]
        xc_ref[pl.ds(r0, Q), :] = _silu(conv).astype(BF16)
        return carry

    lax.fori_loop(0, nck, conv_body, 0)

    ri = lax.broadcasted_iota(jnp.int32, (Q, Q), 0)
    ci = lax.broadcasted_iota(jnp.int32, (Q, Q), 1)
    lane = lax.broadcasted_iota(jnp.int32, (Q, LANES), 1)
    a_neg = -jnp.exp(alog_ref[...])

    def chunk(c, d, first):
        r0 = pl.multiple_of(c * Q, Q)
        rows = pl.ds(r0, Q)
        xs_b = xc_ref[rows, 0:D_SSD]
        xs = xs_b.astype(F32)
        bm = xc_ref[rows, D_SSD:D_SSD + GROUPS * STATE]
        cm = xc_ref[rows, D_SSD + GROUPS * STATE:XBC_W]
        dt = _softplus(dt_ref[0, rows, :] + dtb_ref[...])
        da = dt * a_neg
        tri = (ci <= ri) if d == 0 else (ci >= ri)
        p0 = da.astype(BF16)
        r1 = da - p0.astype(F32)
        p1 = r1.astype(BF16)
        p2 = (r1 - p1.astype(F32)).astype(BF16)
        tri_b = jnp.where(tri, 1.0, 0.0).astype(BF16)
        cum = _dot(jnp.concatenate([tri_b, tri_b, tri_b], axis=1),
                   jnp.concatenate([p0, p1, p2], axis=0))
        sub_t = (cum - jnp.log(dt)).T
        last = cum[Q - 1:Q] if d == 0 else cum[0:1]
        ecum_e = _expand2(jnp.exp(cum), e64_ref[d])
        w_e = _expand2(jnp.exp(last - cum) * dt, e64_ref[d])
        colb = _expand2(cum, e128_ref[d])
        decay_e = ecum_e[Q - 1:Q] if d == 0 else ecum_e[0:1]

        gmat = [lax.dot_general(cm[:, g * STATE:(g + 1) * STATE], bm[:, g * STATE:(g + 1) * STATE],
                                (((1,), (1,)), ((), ())), preferred_element_type=F32)
                for g in range(GROUPS)]
        zero_b = jnp.zeros((Q, LANES), BF16)
        y_parts = []
        for p in range(HEADS // 2):
            g = (2 * p) // (HEADS // GROUPS)
            ms = []
            for hh in (2 * p, 2 * p + 1):
                seg = colb[:, hh * LANES:(hh + 1) * LANES] - sub_t[HEADS * d + hh:HEADS * d + hh + 1, :]
                ms.append((jnp.where(tri, jnp.exp(seg), 0.0) * gmat[g]).astype(BF16))
            mcat = jnp.concatenate(ms, axis=1)
            xp = xs_b[:, p * LANES:(p + 1) * LANES]
            rhs = jnp.concatenate([jnp.where(lane < HEAD_DIM, xp, zero_b),
                                   jnp.where(lane >= HEAD_DIM, xp, zero_b)], axis=0)
            y_parts.append(_dot(mcat, rhs))
        y_diag = jnp.concatenate(y_parts, axis=1)

        s_old = s_ref[d]
        s_bf = s_old.astype(BF16)
        y_off = jnp.concatenate(
            [_dot(cm[:, g * STATE:(g + 1) * STATE], s_bf[:, g * gw:(g + 1) * gw]) for g in range(GROUPS)],
            axis=1)
        y = y_diag + y_off * ecum_e

        xw = (xs * w_e).astype(BF16)
        upd = jnp.concatenate(
            [lax.dot_general(bm[:, g * STATE:(g + 1) * STATE], xw[:, g * gw:(g + 1) * gw],
                             (((0,), (0,)), ((), ())), preferred_element_type=F32) for g in range(GROUPS)],
            axis=1)
        s_ref[d] = s_old * decay_e + upd

        if first:
            y_ref[rows, :] = y + dsk_ref[...] * xs
        else:
            tot = y_ref[rows, :] + y
            zz = z_ref[0, rows, :].astype(F32)
            gz = tot * _silu(zz)
            outs = []
            for g in range(GROUPS):
                gg = gz[:, g * gw:(g + 1) * gw]
                outs.append(gg * lax.rsqrt(jnp.mean(gg * gg, axis=-1, keepdims=True) + NORM_EPS))
            o_ref[0, rows, :] = (jnp.concatenate(outs, axis=1) * g_ref[...]).astype(BF16)

    s_ref[...] = h0_ref[0]

    def first_half(i, carry):
        chunk(i, 0, True)
        chunk(nck - 1 - i, 1, True)
        return carry

    def second_half(i, carry):
        chunk(i, 0, False)
        chunk(nck - 1 - i, 1, False)
        return carry

    lax.fori_loop(0, nck // 2, first_half, 0)
    lax.fori_loop(nck // 2, nck, second_half, 0)


def _ssd(xbc3, z3, dt3, h0, cw, cb, dtb, alog, dsk, g, e64, e128):
    bsz = xbc3.shape[0]
    const = lambda *shape: pl.BlockSpec(shape, lambda b: (0,) * len(shape))
    seq = lambda w: pl.BlockSpec((1, SEQ, w), lambda b: (b, 0, 0))
    return pl.pallas_call(
        _ssd_kernel,
        out_shape=jax.ShapeDtypeStruct((bsz, SEQ, D_SSD), BF16),
        grid=(bsz,),
        in_specs=[seq(XBC_W), seq(D_SSD), seq(LANES),
                  pl.BlockSpec((1, 2, STATE, D_SSD), lambda b: (b, 0, 0, 0)),
                  const(3, XBC_W), const(1, XBC_W), const(1, LANES), const(1, LANES),
                  const(1, D_SSD), const(1, D_SSD), const(2, 2 * LANES, D_SSD),
                  const(2, 2 * LANES, HEADS * LANES)],
        out_specs=seq(D_SSD),
        scratch_shapes=[pltpu.VMEM((SEQ, XBC_W), BF16), pltpu.VMEM((SEQ, D_SSD), F32),
                        pltpu.VMEM((2, STATE, D_SSD), F32)],
        compiler_params=pltpu.CompilerParams(dimension_semantics=("arbitrary",),
                                             vmem_limit_bytes=VMEM_LIMIT),
        name="ssd",
    )(xbc3, z3, dt3, h0, cw, cb, dtb, alog, dsk, g, e64, e128)


def _outproj_kernel(x_ref, yssd_ref, scb_ref, v_ref, vp_ref, vn_ref, mod_ref, scw_ref, wo1_ref, wo2_ref,
                    g2_ref, wrt_ref, br_ref, x1_ref, lg_ref):
    tm = TOK_TILE
    per_b = SEQ // tm
    i = pl.program_id(0)
    first = (i % per_b) == 0
    last = (i % per_b) == per_b - 1
    m = mod_ref[0]
    v = v_ref[...].astype(F32)
    vp = jnp.where(first, 0.0, vp_ref[...].astype(F32))
    vn = jnp.where(last, 0.0, vn_ref[...].astype(F32))
    dn = jnp.concatenate([vp, v[:tm - GRID_W]], axis=0)
    up = jnp.concatenate([v[GRID_W:], vn], axis=0)
    scw = scw_ref[...]
    ysc = scb_ref[...].astype(F32) * (scw[0:1] * dn + scw[1:2] * v + scw[2:3] * up)
    out = _dot(yssd_ref[...], wo1_ref[...]) + _dot(ysc.astype(BF16), wo2_ref[...])
    x1 = x_ref[...] + m[2:3] * out
    x1_ref[...] = x1
    h2 = _rms(x1) * g2_ref[...] * (1.0 + m[4:5]) + m[3:4]
    h_hi = h2.astype(BF16)
    h_lo = (h2 - h_hi.astype(F32)).astype(BF16)
    lg_ref[...] = lax.dot_general(wrt_ref[...], jnp.concatenate([h_hi, h_lo, h_hi], axis=1),
                                  (((1,), (1,)), ((), ())), preferred_element_type=F32) + br_ref[...]


def _outproj(x2, yssd, scb, v, mod3, scw, wo1, wo2, g2, wrt, br):
    t = x2.shape[0]
    tm = TOK_TILE
    per_b = SEQ // tm
    r = tm // GRID_W
    nrow = t // GRID_W
    const = lambda *shape: pl.BlockSpec(shape, lambda i: (0,) * len(shape))
    tile = lambda w: pl.BlockSpec((tm, w), lambda i: (i, 0))
    return pl.pallas_call(
        _outproj_kernel,
        out_shape=(jax.ShapeDtypeStruct((t, D_MODEL), F32), jax.ShapeDtypeStruct((N_EXPERTS, t), F32)),
        grid=(t // tm,),
        in_specs=[tile(D_MODEL), tile(D_SSD), tile(D_SC), tile(D_SC),
                  pl.BlockSpec((GRID_W, D_SC), lambda i: (jnp.maximum(i * r - 1, 0), 0)),
                  pl.BlockSpec((GRID_W, D_SC), lambda i: (jnp.minimum((i + 1) * r, nrow - 1), 0)),
                  pl.BlockSpec((1, N_MOD, D_MODEL), lambda i: (i // per_b, 0, 0)),
                  const(3, D_SC), const(D_SSD, D_MODEL), const(D_SC, D_MODEL), const(1, D_MODEL),
                  const(N_EXPERTS, 3 * D_MODEL), const(N_EXPERTS, 1)],
        out_specs=(tile(D_MODEL), pl.BlockSpec((N_EXPERTS, tm), lambda i: (0, i))),
        compiler_params=pltpu.CompilerParams(dimension_semantics=("arbitrary",),
                                             vmem_limit_bytes=VMEM_LIMIT),
        name="outproj",
    )(x2, yssd, scb, v, v, v, mod3, scw, wo1, wo2, g2, wrt, br)


def _route_kernel(lg_ref, dest_ref, gate_ref, idx_ref, meta_ref, rank_ref, carry_ref, *, n_tok, n_blocks):
    tt = RT_TILE
    ne = N_EXPERTS
    eio = lax.broadcasted_iota(jnp.int32, (ne, tt), 0)
    si = lax.broadcasted_iota(jnp.int32, (tt, tt), 0)
    ti = lax.broadcasted_iota(jnp.int32, (tt, tt), 1)
    before = (si < ti).astype(BF16)
    carry_ref[...] = jnp.zeros_like(carry_ref)

    def tile_body(j, c):
        t0 = pl.multiple_of(j * tt, tt)
        l = lg_ref[:, pl.ds(t0, tt)]
        onehot = jnp.zeros((ne, tt), F32)
        tops, sels = [], []
        for _ in range(TOP_K):
            mx = jnp.max(l, axis=0, keepdims=True)
            idx = jnp.min(jnp.where(l == mx, eio, ne), axis=0, keepdims=True)
            sel = eio == idx
            l = jnp.where(sel, -jnp.inf, l)
            onehot = onehot + sel.astype(F32)
            tops.append(mx)
            sels.append(sel)
            idx_ref[pl.ds(len(tops) - 1, 1), pl.ds(t0, tt)] = idx
        ex = [jnp.exp(tv - tops[0]) for tv in tops]
        den = ex[0] + ex[1] + ex[2] + ex[3]
        prefix = _dot(onehot.astype(BF16), before) + carry_ref[:, 0:1]
        for k in range(TOP_K):
            gate_ref[pl.ds(k, 1), pl.ds(t0, tt)] = ex[k] / den
            rk = jnp.sum(jnp.where(sels[k], prefix, 0.0), axis=0, keepdims=True)
            rank_ref[pl.ds(k, 1), pl.ds(t0, tt)] = rk.astype(jnp.int32)
        carry_ref[...] = carry_ref[...] + jnp.sum(onehot, axis=1, keepdims=True)
        return c

    lax.fori_loop(0, n_tok // tt, tile_body, 0)

    counts = carry_ref[...]
    padded = jnp.floor((counts + (MOE_BM - 1)) * (1.0 / MOE_BM)) * MOE_BM
    er = lax.broadcasted_iota(jnp.int32, (ne, ne), 0)
    ec = lax.broadcasted_iota(jnp.int32, (ne, ne), 1)
    pad_start = jnp.dot((ec < er).astype(F32), padded, precision=HIGHEST, preferred_element_type=F32)
    pad_end = pad_start + padded

    def dest_body(j, c):
        t0 = pl.multiple_of(j * tt, tt)
        for k in range(TOP_K):
            idx = idx_ref[pl.ds(k, 1), pl.ds(t0, tt)]
            base = jnp.sum(jnp.where(eio == idx, pad_start[:, 0:1], 0.0), axis=0, keepdims=True)
            dest_ref[pl.ds(k, 1), pl.ds(t0, tt)] = base.astype(jnp.int32) + rank_ref[pl.ds(k, 1), pl.ds(t0, tt)]
        return c

    lax.fori_loop(0, n_tok // tt, dest_body, 0)

    width = meta_ref.shape[1]
    sub = lax.broadcasted_iota(jnp.int32, (ne, width), 0)
    lan = lax.broadcasted_iota(jnp.int32, (ne, width), 1)
    diag = sub == lan
    cnt_row = jnp.sum(jnp.where(diag, counts[:, 0:1], 0.0), axis=0, keepdims=True)
    start_row = jnp.sum(jnp.where(diag, pad_start[:, 0:1], 0.0), axis=0, keepdims=True)
    blk_start = (lan * MOE_BM).astype(F32)
    blk_exp = jnp.sum((pad_end[:, 0:1] <= blk_start).astype(F32), axis=0, keepdims=True)
    blk_exp = jnp.minimum(blk_exp, float(ne - 1))
    used = jnp.sum(padded[:, 0:1], axis=0, keepdims=True) * (1.0 / MOE_BM)
    meta_ref[0:1, :] = cnt_row.astype(jnp.int32)
    meta_ref[1:2, :] = start_row.astype(jnp.int32)
    meta_ref[2:3, :] = blk_exp.astype(jnp.int32)
    meta_ref[3:4, :] = jnp.broadcast_to(used, (1, width)).astype(jnp.int32)
    meta_ref[4:8, :] = jnp.zeros((4, width), jnp.int32)


def _route(lgt, n_blocks):
    ne, n_tok = lgt.shape
    width = -(-n_blocks // LANES) * LANES
    full = lambda *shape: pl.BlockSpec(shape, lambda: (0,) * len(shape))
    return pl.pallas_call(
        functools.partial(_route_kernel, n_tok=n_tok, n_blocks=n_blocks),
        out_shape=(jax.ShapeDtypeStruct((TOP_K, n_tok), jnp.int32),
                   jax.ShapeDtypeStruct((TOP_K, n_tok), F32),
                   jax.ShapeDtypeStruct((TOP_K, n_tok), jnp.int32),
                   jax.ShapeDtypeStruct((8, width), jnp.int32)),
        in_specs=[full(ne, n_tok)],
        out_specs=(full(TOP_K, n_tok), full(TOP_K, n_tok), full(TOP_K, n_tok), full(8, width)),
        scratch_shapes=[pltpu.VMEM((TOP_K, n_tok), jnp.int32), pltpu.VMEM((ne, LANES), F32)],
        compiler_params=pltpu.CompilerParams(vmem_limit_bytes=VMEM_LIMIT),
        name="route",
    )(lgt)


def _dispatch_kernel(dest_ref, cnt_ref, start_ref, nu_ref, x1_ref, meta_ref, mod_ref, g2_ref, zsrc_ref, xs_ref,
                     hbuf, sem, zsem):
    i = pl.program_id(0)
    n = pl.num_programs(0)
    tl = DISP_TILE
    slot = i % 2
    nb = xs_ref.shape[0] // MOE_BM

    def zero_block(b):
        return pltpu.make_async_copy(zsrc_ref, xs_ref.at[pl.ds(b * MOE_BM, MOE_BM)], zsem)

    @pl.when(i == 0)
    def _():
        def start_e(e, c):
            @pl.when(cnt_ref[e] > 0)
            def _():
                zero_block((start_ref[e] + cnt_ref[e] - 1) // MOE_BM).start()
            return c

        def wait_e(e, c):
            @pl.when(cnt_ref[e] > 0)
            def _():
                zero_block(0).wait()
            return c

        def start_t(b, c):
            zero_block(b).start()
            return c

        def wait_t(b, c):
            zero_block(0).wait()
            return c

        lax.fori_loop(0, N_EXPERTS, start_e, 0)
        lax.fori_loop(nu_ref[0], nb, start_t, 0)
        lax.fori_loop(0, N_EXPERTS, wait_e, 0)
        lax.fori_loop(nu_ref[0], nb, wait_t, 0)

    m = mod_ref[0]
    h2 = _rms(x1_ref[...]) * g2_ref[...] * (1.0 + m[4:5]) + m[3:4]
    lo = pltpu.bitcast(h2[:, :PACK_W].astype(BF16).astype(F32), jnp.uint32) >> 16
    hi = pltpu.bitcast(h2[:, PACK_W:].astype(BF16).astype(F32), jnp.uint32) & jnp.uint32(0xFFFF0000)
    row = jnp.concatenate([lo | hi, meta_ref[...], jnp.zeros((tl, D_MODEL - PACK_W - LANES), jnp.uint32)], axis=1)
    hbuf[slot] = row.reshape(tl, SUBLANES, LANES)

    def per_tok(t, c):
        tok = i * tl + t
        for k in range(TOP_K):
            d = dest_ref[tok * TOP_K + k]
            pltpu.make_async_copy(hbuf.at[slot, t], xs_ref.at[d], sem.at[slot]).start(priority=k % 2)
        return c

    lax.fori_loop(0, tl, per_tok, 0, unroll=8)

    def wait_slot(sl):
        for _ in range(TOP_K):
            pltpu.make_async_copy(hbuf.at[sl], xs_ref.at[pl.ds(0, tl)], sem.at[sl]).wait()

    @pl.when(i > 0)
    def _():
        wait_slot(1 - slot)

    @pl.when(i == n - 1)
    def _():
        wait_slot(slot)


def _dispatch(dest_flat, cnt, start, n_used, x1, meta_rows, mod3, g2, zsrc, n_rows):
    n_tok = x1.shape[0]
    tl = DISP_TILE
    per_b = SEQ // tl
    return pl.pallas_call(
        _dispatch_kernel,
        out_shape=jax.ShapeDtypeStruct((n_rows, SUBLANES, LANES), jnp.uint32),
        grid_spec=pltpu.PrefetchScalarGridSpec(
            num_scalar_prefetch=4,
            grid=(n_tok // tl,),
            in_specs=[pl.BlockSpec((tl, D_MODEL), lambda i, *_: (i, 0)),
                      pl.BlockSpec((tl, LANES), lambda i, *_: (i, 0)),
                      pl.BlockSpec((1, N_MOD, D_MODEL), lambda i, *_: (i // per_b, 0, 0)),
                      pl.BlockSpec((1, D_MODEL), lambda i, *_: (0, 0)),
                      pl.BlockSpec((MOE_BM, SUBLANES, LANES), lambda i, *_: (0, 0, 0))],
            out_specs=pl.BlockSpec(memory_space=pl.ANY),
            scratch_shapes=[pltpu.VMEM((2, tl, SUBLANES, LANES), jnp.uint32),
                            pltpu.SemaphoreType.DMA((2,)), pltpu.SemaphoreType.DMA]),
        compiler_params=pltpu.CompilerParams(dimension_semantics=("arbitrary",),
                                             vmem_limit_bytes=VMEM_LIMIT),
        name="dispatch",
    )(dest_flat, cnt, start, n_used, x1, meta_rows, mod3, g2, zsrc)


def _expert_kernel(be_ref, nu_ref, xs_ref, wgu_hbm, bgu_ref, wd_hbm, bd_ref, zero_ref, ytm_ref,
                   big, idv, ids, wgu_raw, wd_raw, wgu_bf, wd_bf, sc_sem, id_sem, z_sem, w_sem, *, n_tok):
    j = pl.program_id(0)
    nbk = pl.num_programs(0) - 1
    nu = nu_ref[0]
    slot = j % 2
    prev = 1 - slot

    def weight_copies(e):
        return (pltpu.make_async_copy(wgu_hbm.at[e], wgu_raw, w_sem.at[0]),
                pltpu.make_async_copy(wd_hbm.at[e], wd_raw, w_sem.at[1]))

    @pl.when(j == 0)
    def _():
        cp = pltpu.make_async_copy(zero_ref, ytm_ref.at[pl.ds(TOP_K * n_tok, MOE_BM)], z_sem)
        cp.start()
        for w in weight_copies(be_ref[0]):
            w.start()
        cp.wait()

    e_now = be_ref[jnp.minimum(j, nbk - 1)]
    new_expert = jnp.logical_or(j == 0, be_ref[jnp.maximum(jnp.minimum(j, nbk - 1) - 1, 0)] != e_now)

    @pl.when(jnp.logical_and(j < nu, new_expert))
    def _():
        for w in weight_copies(e_now):
            w.wait()
        n_cc = 8
        for c in range(n_cc):
            cc = slice(c * (2 * D_FF // n_cc), (c + 1) * (2 * D_FF // n_cc))
            wgu_bf[:, cc] = wgu_raw[:, cc].astype(BF16)
        for c in range(n_cc // 2):
            cc = slice(c * (2 * D_MODEL // n_cc), (c + 1) * (2 * D_MODEL // n_cc))
            wd_bf[:, cc] = wd_raw[:, cc].astype(BF16)
        j_next = lax.while_loop(lambda t: jnp.logical_and(t < nu, be_ref[jnp.minimum(t, nbk - 1)] == e_now),
                                lambda t: t + 1, j + 1)

        @pl.when(j_next < nu)
        def _():
            for w in weight_copies(be_ref[jnp.minimum(j_next, nbk - 1)]):
                w.start()

    n_pc = 4
    pw1 = D_FF // n_pc
    pw2 = D_MODEL // n_pc
    n_lt = D_MODEL // LANES
    spb = MOE_BM // SUBLANES
    groups = [3 * MOE_BM // 16] * n_pc + [MOE_BM // 16] * n_pc
    assert sum(groups) == MOE_BM

    def slab_store(unit, col0, val):
        for s in range(val.shape[1] // LANES):
            t0 = unit * MOE_BM + (col0 // LANES + s) * spb
            big[pl.ds(t0, spb)] = val[:, s * LANES:(s + 1) * LANES].reshape(spb, SUBLANES, LANES)

    def slab_load(unit, row0, n):
        return jnp.concatenate(
            [big[pl.ds(unit * MOE_BM + s * spb + row0 // SUBLANES, n // SUBLANES)].reshape(n, LANES)
             for s in range(n_lt)], axis=1)

    def scatter_group(g):
        lo = sum(groups[:g])
        n = groups[g]
        big[pl.ds(prev * MOE_BM + lo, n)] = slab_load(3 + prev, lo, n).reshape(n, n_lt, LANES)
        for r in range(lo, lo + n):
            pltpu.make_async_copy(big.at[prev * MOE_BM + r], ytm_ref.at[ids[prev, 0, r]],
                                  sc_sem.at[prev]).start(priority=r % 2)

    def scatter_wait(sl):
        pltpu.make_async_copy(big.at[pl.ds(0, MOE_BM)], ytm_ref.at[pl.ds(0, MOE_BM)], sc_sem.at[sl]).wait()

    def compute(with_scatter):
        e_f = be_ref[jnp.minimum(j, nbk - 1)].astype(F32)
        words = xs_ref[...].reshape(MOE_BM, D_MODEL)
        packed = words[:, 0:PACK_W]
        meta = pltpu.bitcast(words[:, PACK_W:PACK_W + LANES], F32)
        xb = jnp.concatenate(
            [pltpu.bitcast(packed << 16, F32).astype(BF16),
             pltpu.bitcast(packed & jnp.uint32(0xFFFF0000), F32).astype(BF16)], axis=1)
        gate = jnp.zeros((MOE_BM, 1), F32)
        kk = jnp.zeros((MOE_BM, 1), F32)
        for k in range(TOP_K):
            mk = meta[:, META_IDX + k:META_IDX + k + 1] == e_f
            gate = gate + jnp.where(mk, meta[:, META_GATE + k:META_GATE + k + 1], 0.0)
            kk = kk + jnp.where(mk, float(k), 0.0)
        row = kk * float(n_tok) + meta[:, META_TOK:META_TOK + 1]
        row_t = jnp.broadcast_to(row, (MOE_BM, LANES)).T
        idv[slot] = row_t[0:8].astype(jnp.int32)
        pltpu.make_async_copy(idv.at[slot], ids.at[slot], id_sem.at[slot]).start()
        for c in range(n_pc):
            if with_scatter:
                scatter_group(c)
            cg = slice(c * pw1, (c + 1) * pw1)
            cl = slice(D_FF + c * pw1, D_FF + (c + 1) * pw1)
            glu = jnp.minimum(_dot(xb, wgu_bf[:, cg]) + bgu_ref[0, :, cg], SWIGLU_LIMIT)
            lin = jnp.clip(_dot(xb, wgu_bf[:, cl]) + bgu_ref[0, :, cl], -SWIGLU_LIMIT, SWIGLU_LIMIT)
            slab_store(2, c * pw1, glu * jax.nn.sigmoid(SWIGLU_ALPHA * glu) * (lin + 1.0))
        act = slab_load(2, 0, MOE_BM).astype(BF16)
        for c in range(n_pc):
            if with_scatter:
                scatter_group(n_pc + c)
            cs = slice(c * pw2, (c + 1) * pw2)
            slab_store(3 + slot, c * pw2, (_dot(act, wd_bf[:, cs]) + bd_ref[0, :, cs]) * gate)

    def ids_wait():
        pltpu.make_async_copy(idv.at[prev], ids.at[prev], id_sem.at[prev]).wait()

    @pl.when(jnp.logical_and(j >= 3, j <= nu))
    def _():
        scatter_wait(prev)

    @pl.when(j == 0)
    def _():
        compute(False)

    @pl.when(jnp.logical_and(j >= 1, j < nu))
    def _():
        ids_wait()
        compute(True)

    @pl.when(j == nu)
    def _():
        ids_wait()
        for g in range(len(groups)):
            scatter_group(g)
        scatter_wait(prev)

    @pl.when(jnp.logical_and(j == nu, j >= 2))
    def _():
        scatter_wait(slot)


def _experts(blk_exp, n_used, xs, wgu, bgu, wd, bd, n_tok):
    n_rows = xs.shape[0]
    nb = n_rows // MOE_BM
    row_blk = lambda j, be, nu: (jnp.minimum(j, nu[0] - 1), 0, 0)
    per_e = lambda j, be, nu: (be[jnp.minimum(j, nb - 1)], 0, 0)
    sub = D_MODEL // LANES
    zero = jnp.zeros((MOE_BM, sub, LANES), F32)
    return pl.pallas_call(
        functools.partial(_expert_kernel, n_tok=n_tok),
        out_shape=jax.ShapeDtypeStruct((TOP_K * n_tok + MOE_BM, sub, LANES), F32),
        grid_spec=pltpu.PrefetchScalarGridSpec(
            num_scalar_prefetch=2,
            grid=(nb + 1,),
            in_specs=[pl.BlockSpec((MOE_BM, SUBLANES, LANES), row_blk),
                      pl.BlockSpec(memory_space=pl.ANY),
                      pl.BlockSpec((1, 1, 2 * D_FF), per_e),
                      pl.BlockSpec(memory_space=pl.ANY),
                      pl.BlockSpec((1, 1, D_MODEL), per_e),
                      pl.BlockSpec((MOE_BM, sub, LANES), lambda j, be, nu: (0, 0, 0))],
            out_specs=pl.BlockSpec(memory_space=pl.ANY),
            scratch_shapes=[pltpu.VMEM((5 * MOE_BM, sub, LANES), F32), pltpu.VMEM((2, 8, MOE_BM), jnp.int32),
                            pltpu.SMEM((2, 8, MOE_BM), jnp.int32),
                            pltpu.VMEM((D_MODEL, 2 * D_FF), F32), pltpu.VMEM((D_FF, D_MODEL), F32),
                            pltpu.VMEM((D_MODEL, 2 * D_FF), BF16), pltpu.VMEM((D_FF, D_MODEL), BF16),
                            pltpu.SemaphoreType.DMA((2,)), pltpu.SemaphoreType.DMA((2,)),
                            pltpu.SemaphoreType.DMA, pltpu.SemaphoreType.DMA((2,))]),
        compiler_params=pltpu.CompilerParams(dimension_semantics=("arbitrary",),
                                             vmem_limit_bytes=VMEM_LIMIT),
        name="experts",
    )(blk_exp, n_used, xs, wgu, bgu, wd, bd, zero)


def _combine_kernel(y0_ref, y1_ref, y2_ref, y3_ref, x1_ref, mod_ref, fg_ref, o_ref):
    m = mod_ref[0]
    moe = ((y0_ref[...] + y1_ref[...]) + (y2_ref[...] + y3_ref[...])).reshape(COMB_TILE, D_MODEL)
    x2 = x1_ref[...] + m[5:6] * moe
    o_ref[...] = _rms(x2) * fg_ref[...]


def _combine(ytm, x1, mod3, fg):
    n_tok = x1.shape[0]
    tc = COMB_TILE
    per_b = SEQ // tc
    nt = n_tok // tc
    slot_spec = lambda k: pl.BlockSpec((tc, D_MODEL // LANES, LANES), lambda i: (i + k * nt, 0, 0))
    return pl.pallas_call(
        _combine_kernel,
        out_shape=jax.ShapeDtypeStruct((n_tok, D_MODEL), F32),
        grid=(nt,),
        in_specs=[slot_spec(0), slot_spec(1), slot_spec(2), slot_spec(3),
                  pl.BlockSpec((tc, D_MODEL), lambda i: (i, 0)),
                  pl.BlockSpec((1, N_MOD, D_MODEL), lambda i: (i // per_b, 0, 0)),
                  pl.BlockSpec((1, D_MODEL), lambda i: (0, 0))],
        out_specs=pl.BlockSpec((tc, D_MODEL), lambda i: (i, 0)),
        compiler_params=pltpu.CompilerParams(dimension_semantics=("arbitrary",),
                                             vmem_limit_bytes=VMEM_LIMIT),
        name="combine",
    )(ytm, ytm, ytm, ytm, x1, mod3, fg)


def _expert_rows_kernel(be_ref, nu_ref, xs_ref, wgu_hbm, bgu_ref, wd_hbm, bd_ref, ys_ref,
                        wgu_raw, wd_raw, wgu_bf, wd_bf, w_sem):
    j = pl.program_id(0)
    nu = nu_ref[0]
    n_lt = D_MODEL // LANES

    def weight_copies(e):
        return (pltpu.make_async_copy(wgu_hbm.at[e], wgu_raw, w_sem.at[0]),
                pltpu.make_async_copy(wd_hbm.at[e], wd_raw, w_sem.at[1]))

    @pl.when(j == 0)
    def _():
        for w in weight_copies(be_ref[0]):
            w.start()

    e_now = be_ref[j]
    new_expert = jnp.logical_or(j == 0, be_ref[jnp.maximum(j - 1, 0)] != e_now)

    @pl.when(jnp.logical_and(j < nu, new_expert))
    def _():
        for w in weight_copies(e_now):
            w.wait()
        n_cc = 8
        for c in range(n_cc):
            cc = slice(c * (2 * D_FF // n_cc), (c + 1) * (2 * D_FF // n_cc))
            wgu_bf[:, cc] = wgu_raw[:, cc].astype(BF16)
        for c in range(n_cc // 2):
            cc = slice(c * (2 * D_MODEL // n_cc), (c + 1) * (2 * D_MODEL // n_cc))
            wd_bf[:, cc] = wd_raw[:, cc].astype(BF16)
        j_next = lax.while_loop(lambda t: jnp.logical_and(t < nu, be_ref[jnp.minimum(t, pl.num_programs(0) - 1)] == e_now),
                                lambda t: t + 1, j + 1)

        @pl.when(j_next < nu)
        def _():
            for w in weight_copies(be_ref[jnp.minimum(j_next, pl.num_programs(0) - 1)]):
                w.start()

    @pl.when(j < nu)
    def _():
        words = xs_ref[...].reshape(MOE_BM, D_MODEL)
        packed = words[:, 0:PACK_W]
        meta = pltpu.bitcast(words[:, PACK_W:PACK_W + LANES], F32)
        xb = jnp.concatenate(
            [pltpu.bitcast(packed << 16, F32).astype(BF16),
             pltpu.bitcast(packed & jnp.uint32(0xFFFF0000), F32).astype(BF16)], axis=1)
        e_f = e_now.astype(F32)
        gate = jnp.zeros((MOE_BM, 1), F32)
        for k in range(TOP_K):
            mk = meta[:, META_IDX + k:META_IDX + k + 1] == e_f
            gate = gate + jnp.where(mk, meta[:, META_GATE + k:META_GATE + k + 1], 0.0)
        gu = _dot(xb, wgu_bf[...]) + bgu_ref[0]
        glu = jnp.minimum(gu[:, :D_FF], SWIGLU_LIMIT)
        lin = jnp.clip(gu[:, D_FF:], -SWIGLU_LIMIT, SWIGLU_LIMIT)
        act = glu * jax.nn.sigmoid(SWIGLU_ALPHA * glu) * (lin + 1.0)
        y = (_dot(act.astype(BF16), wd_bf[...]) + bd_ref[0]) * gate
        ys_ref[...] = y.reshape(MOE_BM, n_lt, LANES)

    @pl.when(j >= nu)
    def _():
        ys_ref[...] = jnp.zeros_like(ys_ref)


def _expert_rows(blk_exp, n_used, xs, wgu, bgu, wd, bd):
    n_rows = xs.shape[0]
    nb = n_rows // MOE_BM
    n_lt = D_MODEL // LANES
    row_blk = lambda j, be, nu: (jnp.minimum(j, nu[0] - 1), 0, 0)
    per_e = lambda j, be, nu: (be[j], 0, 0)
    return pl.pallas_call(
        _expert_rows_kernel,
        out_shape=jax.ShapeDtypeStruct((n_rows, n_lt, LANES), F32),
        grid_spec=pltpu.PrefetchScalarGridSpec(
            num_scalar_prefetch=2,
            grid=(nb,),
            in_specs=[pl.BlockSpec((MOE_BM, SUBLANES, LANES), row_blk),
                      pl.BlockSpec(memory_space=pl.ANY),
                      pl.BlockSpec((1, 1, 2 * D_FF), per_e),
                      pl.BlockSpec(memory_space=pl.ANY),
                      pl.BlockSpec((1, 1, D_MODEL), per_e)],
            out_specs=pl.BlockSpec((MOE_BM, n_lt, LANES), lambda j, be, nu: (j, 0, 0)),
            scratch_shapes=[pltpu.VMEM((D_MODEL, 2 * D_FF), F32), pltpu.VMEM((D_FF, D_MODEL), F32),
                            pltpu.VMEM((D_MODEL, 2 * D_FF), BF16), pltpu.VMEM((D_FF, D_MODEL), BF16),
                            pltpu.SemaphoreType.DMA((2,))]),
        compiler_params=pltpu.CompilerParams(dimension_semantics=("arbitrary",),
                                             vmem_limit_bytes=VMEM_LIMIT),
        name="experts",
    )(blk_exp, n_used, xs, wgu, bgu, wd, bd)


GATHER_TILE = 256


def _gather_combine_kernel(dest_ref, ys_ref, x1_ref, mod_ref, fg_ref, o_ref, buf, sem):
    i = pl.program_id(0)
    n = pl.num_programs(0)
    tc = GATHER_TILE
    slot = i % 2

    def issue(tile, sl):
        def per_tok(t, c):
            tok = tile * tc + t
            for k in range(TOP_K):
                pltpu.make_async_copy(ys_ref.at[dest_ref[tok * TOP_K + k]], buf.at[sl, k * tc + t],
                                      sem.at[sl]).start(priority=k % 2)
            return c

        lax.fori_loop(0, tc, per_tok, 0, unroll=8)

    @pl.when(i == 0)
    def _():
        issue(0, 0)

    @pl.when(i + 1 < n)
    def _():
        issue(i + 1, 1 - slot)

    for _ in range(TOP_K):
        pltpu.make_async_copy(ys_ref.at[pl.ds(0, tc)], buf.at[slot, pl.ds(0, tc)], sem.at[slot]).wait()

    moe = ((buf[slot, pl.ds(0, tc)] + buf[slot, pl.ds(tc, tc)])
           + (buf[slot, pl.ds(2 * tc, tc)] + buf[slot, pl.ds(3 * tc, tc)])).reshape(tc, D_MODEL)
    m = mod_ref[0]
    x2 = x1_ref[...] + m[5:6] * moe
    o_ref[...] = _rms(x2) * fg_ref[...]


def _gather_combine(dest_flat, ys, x1, mod3, fg):
    n_tok = x1.shape[0]
    tc = GATHER_TILE
    per_b = SEQ // tc
    n_lt = D_MODEL // LANES
    return pl.pallas_call(
        _gather_combine_kernel,
        out_shape=jax.ShapeDtypeStruct((n_tok, D_MODEL), F32),
        grid_spec=pltpu.PrefetchScalarGridSpec(
            num_scalar_prefetch=1,
            grid=(n_tok // tc,),
            in_specs=[pl.BlockSpec(memory_space=pl.ANY),
                      pl.BlockSpec((tc, D_MODEL), lambda i, d: (i, 0)),
                      pl.BlockSpec((1, N_MOD, D_MODEL), lambda i, d: (i // per_b, 0, 0)),
                      pl.BlockSpec((1, D_MODEL), lambda i, d: (0, 0))],
            out_specs=pl.BlockSpec((tc, D_MODEL), lambda i, d: (i, 0)),
            scratch_shapes=[pltpu.VMEM((2, TOP_K * tc, n_lt, LANES), F32), pltpu.SemaphoreType.DMA((2,))]),
        compiler_params=pltpu.CompilerParams(dimension_semantics=("arbitrary",),
                                             vmem_limit_bytes=VMEM_LIMIT),
        name="combine",
    )(dest_flat, ys, x1, mod3, fg)


def _expansion_matrices():
    r = (jnp.arange(2 * LANES) % LANES)[:, None]
    out64, out128 = [], []
    for d in range(2):
        l64 = jnp.arange(D_SSD)[None, :]
        l128 = jnp.arange(HEADS * LANES)[None, :]
        out64.append((l64 // HEAD_DIM == r - HEADS * d).astype(BF16))
        out128.append((l128 // LANES == r - HEADS * d).astype(BF16))
    return jnp.stack(out64), jnp.stack(out128)


def _pad_lanes(v):
    return jnp.pad(v, [(0, 0)] * (v.ndim - 1) + [(0, LANES - v.shape[-1])])


def kernel(x, c, ctx, c_ctx, w_mod, b_mod, norm1_g, w_in, ssd_conv_w, ssd_conv_b, ssd_dt_bias, ssd_a_log,
           ssd_d, ssd_norm_g, sc_conv_w, w_out, norm2_g, w_router, b_router, w_gate_up, b_gate_up, w_down,
           b_down, final_g):
    bsz = x.shape[0]
    n_tok = bsz * SEQ
    n_assign = n_tok * TOP_K
    n_blocks = n_assign // MOE_BM + N_EXPERTS
    n_rows = n_blocks * MOE_BM
    li = 0

    cvec = jnp.concatenate([c, c_ctx[None, :], jnp.zeros((7, D_MODEL), F32)], axis=0)
    mod3 = _mod(cvec, w_mod[li], b_mod[li][None, :]).reshape(bsz + 8, N_MOD, D_MODEL)

    w = w_in[li]
    wz = w[:, Z0:X0].astype(BF16)
    wxbc = w[:, X0:DT0].astype(BF16)
    wdt = _pad_lanes(w[:, DT0:SC0]).astype(BF16)
    wb = w[:, SC0:SC0 + D_SC].astype(BF16)
    wc = w[:, SC0 + D_SC:SC0 + 2 * D_SC].astype(BF16)
    wu = w[:, SC0 + 2 * D_SC:].astype(BF16)
    g1 = norm1_g[li][None, :]
    cw = ssd_conv_w[li]
    cb = ssd_conv_b[li][None, :]
    dtb = _pad_lanes(ssd_dt_bias[li].reshape(1, 2 * HEADS))
    alog = _pad_lanes(ssd_a_log[li].reshape(1, 2 * HEADS))
    e64, e128 = _expansion_matrices()

    h0 = _ctx_states(ctx, mod3, g1, wxbc[:, :XB_W], wdt, cw[:, :XB_W], cb[:, :XB_W], dtb, alog, e64)

    x2 = x.reshape(n_tok, D_MODEL)
    z, xbc, dtr, scb, v = _inproj(x2, mod3, g1, wz, wxbc, wdt, wb, wc, wu)

    dsk = jnp.repeat(ssd_d[li], HEAD_DIM)[None, :]
    yssd = _ssd(xbc.reshape(bsz, SEQ, XBC_W), z.reshape(bsz, SEQ, D_SSD), dtr.reshape(bsz, SEQ, LANES), h0,
                cw, cb, dtb, alog, dsk, ssd_norm_g[li][None, :], e64, e128)

    wo = w_out[li].astype(BF16)
    g2 = norm2_g[li][None, :]
    wr = w_router[li].T
    wr_hi = wr.astype(BF16)
    wr_lo = (wr - wr_hi.astype(F32)).astype(BF16)
    x1, lgt = _outproj(x2, yssd.reshape(n_tok, D_SSD), scb, v, mod3, sc_conv_w[li], wo[:D_SSD], wo[D_SSD:],
                       g2, jnp.concatenate([wr_hi, wr_hi, wr_lo], axis=1), b_router[li][:, None])

    dest_t, gate_t, idx_t, meta = _route(lgt, n_blocks)
    dest_flat = dest_t.T.reshape(n_assign)
    cnt = meta[0, :N_EXPERTS]
    start = meta[1, :N_EXPERTS]
    blk_exp = meta[2, :n_blocks]
    n_used = meta[3, :1]

    meta_rows = lax.bitcast_convert_type(_pad_lanes(jnp.concatenate(
        [idx_t.T.astype(F32), gate_t.T, jnp.arange(n_tok, dtype=F32)[:, None]], axis=1)), jnp.uint32)
    pad_meta = lax.bitcast_convert_type(_pad_lanes(jnp.concatenate(
        [jnp.full((MOE_BM, TOP_K), -1.0, F32), jnp.zeros((MOE_BM, TOP_K), F32),
         (TOP_K * n_tok + jnp.arange(MOE_BM, dtype=F32))[:, None]], axis=1)), jnp.uint32)
    zsrc = jnp.concatenate([jnp.zeros((MOE_BM, PACK_W), jnp.uint32), pad_meta,
                            jnp.zeros((MOE_BM, D_MODEL - PACK_W - LANES), jnp.uint32)],
                           axis=1).reshape(MOE_BM, SUBLANES, LANES)

    xs = _dispatch(dest_flat, cnt, start, n_used, x1, meta_rows, mod3, g2, zsrc, n_rows)
    ys = _expert_rows(blk_exp, n_used, xs, w_gate_up[li], b_gate_up[li][:, None, :],
                      w_down[li], b_down[li][:, None, :])
    out = _gather_combine(dest_flat, ys, x1, mod3, final_g[None, :])
    return out.reshape(bsz, SEQ, D_MODEL)
```

```python
import functools

import jax
import jax.numpy as jnp
from jax import lax
from jax.experimental import pallas as pl
from jax.experimental.pallas import tpu as pltpu

F32 = jnp.float32
BF16 = jnp.bfloat16
HIGHEST = lax.Precision.HIGHEST

D_MODEL = 1024
SEQ = 2048
CTX_LEN = 256
GRID_W = 64
D_SSD = 1024
D_SC = 1024
HEAD_DIM = 64
HEADS = 16
GROUPS = 2
STATE = 128
CHUNK = 128
N_EXPERTS = 32
TOP_K = 4
D_FF = 1024
SWIGLU_LIMIT = 7.0
SWIGLU_ALPHA = 1.702
NORM_EPS = 1e-6
N_MOD = 6
XBC_W = D_SSD + 2 * GROUPS * STATE
XB_W = D_SSD + GROUPS * STATE
LANES = 128

Z0 = 0
X0 = Z0 + D_SSD
B0 = X0 + D_SSD
C0 = B0 + GROUPS * STATE
DT0 = C0 + GROUPS * STATE
SC0 = DT0 + 2 * HEADS

TOK_TILE = 512
MOE_BM = 256
RT_TILE = 512
DISP_TILE = 256
GATHER_TILE = 256
SUBLANES = 8
PACK_W = D_MODEL // 2
META_IDX = 0
META_GATE = TOP_K
VMEM_LIMIT = 56 * 1024 * 1024


def _silu(v):
    return v * jax.nn.sigmoid(v)


def _softplus(v):
    return jnp.maximum(v, 0.0) + jnp.log1p(jnp.exp(-jnp.abs(v)))


def _rms(v):
    return v * lax.rsqrt(jnp.mean(v * v, axis=-1, keepdims=True) + NORM_EPS)


def _dot(a, b):
    return jnp.dot(a, b, preferred_element_type=F32)


def _expand2(v, e2):
    hi = v.astype(BF16)
    lo = (v - hi.astype(F32)).astype(BF16)
    return _dot(jnp.concatenate([hi, lo], axis=1), e2)


def _mod_kernel(c_ref, w_ref, b_ref, o_ref):
    o_ref[...] = jnp.dot(_silu(c_ref[...]), w_ref[...], precision=HIGHEST,
                         preferred_element_type=F32) + b_ref[...]


def _mod(cvec, w_mod, b_mod):
    rows = cvec.shape[0]
    n = w_mod.shape[1]
    tn = 1536
    return pl.pallas_call(
        _mod_kernel,
        out_shape=jax.ShapeDtypeStruct((rows, n), F32),
        grid=(n // tn,),
        in_specs=[pl.BlockSpec((rows, D_MODEL), lambda j: (0, 0)),
                  pl.BlockSpec((D_MODEL, tn), lambda j: (0, j)),
                  pl.BlockSpec((1, tn), lambda j: (0, j))],
        out_specs=pl.BlockSpec((rows, tn), lambda j: (0, j)),
        compiler_params=pltpu.CompilerParams(dimension_semantics=("arbitrary",),
                                             vmem_limit_bytes=VMEM_LIMIT),
        name="mod",
    )(cvec, w_mod, b_mod)


def _ctx_kernel(ctx_ref, mod_ref, g1_ref, wxb_ref, wdt_ref, cw_ref, cb_ref, dtb_ref, alog_ref, e64_ref,
                h0_ref):
    L = CTX_LEN
    m = mod_ref[0]
    hc = _rms(ctx_ref[0]) * g1_ref[...] * (1.0 + m[1:2]) + m[0:1]
    hb = hc.astype(BF16)
    pxb = _dot(hb, wxb_ref[...])
    dtr = _dot(hb, wdt_ref[...])
    rowi = lax.broadcasted_iota(jnp.int32, (L, XB_W), 0)
    dn = jnp.where(rowi == 0, 0.0, pltpu.roll(pxb, 1, 0))
    up = jnp.where(rowi == L - 1, 0.0, pltpu.roll(pxb, L - 1, 0))
    cw = cw_ref[...]
    xb = _silu(cw[0:1] * dn + cw[1:2] * pxb + cw[2:3] * up + cb_ref[...])
    xs = xb[:, :D_SSD]
    bm = xb[:, D_SSD:].astype(BF16)
    dt = _softplus(dtr + dtb_ref[...])
    da = dt * (-jnp.exp(alog_ref[...]))
    ri = lax.broadcasted_iota(jnp.int32, (L, L), 0)
    ci = lax.broadcasted_iota(jnp.int32, (L, L), 1)
    for d in range(2):
        tri = (ci <= ri) if d == 0 else (ci >= ri)
        cum = jnp.dot(tri.astype(F32), da, precision=HIGHEST, preferred_element_type=F32)
        last = cum[L - 1:L] if d == 0 else cum[0:1]
        w_e = _expand2(jnp.exp(last - cum) * dt, e64_ref[d])
        xw = (xs * w_e).astype(BF16)
        for g in range(GROUPS):
            gw = D_SSD // GROUPS
            st = lax.dot_general(bm[:, g * STATE:(g + 1) * STATE], xw[:, g * gw:(g + 1) * gw],
                                 (((0,), (0,)), ((), ())), preferred_element_type=F32)
            h0_ref[0, d, :, g * gw:(g + 1) * gw] = st


def _ctx_states(ctx, mod3, g1, wxb, wdt, cw, cb, dtb, alog, e64):
    bsz = ctx.shape[0]
    mod_row = bsz
    const = lambda *shape: pl.BlockSpec(shape, lambda b: (0,) * len(shape))
    return pl.pallas_call(
        _ctx_kernel,
        out_shape=jax.ShapeDtypeStruct((bsz, 2, STATE, D_SSD), F32),
        grid=(bsz,),
        in_specs=[pl.BlockSpec((1, CTX_LEN, D_MODEL), lambda b: (b, 0, 0)),
                  pl.BlockSpec((1, N_MOD, D_MODEL), lambda b: (mod_row, 0, 0)),
                  const(1, D_MODEL), const(D_MODEL, XB_W), const(D_MODEL, LANES),
                  const(3, XB_W), const(1, XB_W), const(1, LANES), const(1, LANES),
                  const(2, 2 * LANES, D_SSD)],
        out_specs=pl.BlockSpec((1, 2, STATE, D_SSD), lambda b: (b, 0, 0, 0)),
        compiler_params=pltpu.CompilerParams(dimension_semantics=("arbitrary",),
                                             vmem_limit_bytes=VMEM_LIMIT),
        name="ctx_states",
    )(ctx, mod3, g1, wxb, wdt, cw, cb, dtb, alog, e64)


def _inproj_kernel(x_ref, mod_ref, g1_ref, wz_ref, wxbc_ref, wdt_ref, wb_ref, wc_ref, wu_ref,
                   z_ref, xbc_ref, dt_ref, scb_ref, v_ref):
    m = mod_ref[0]
    hx = _rms(x_ref[...]) * g1_ref[...] * (1.0 + m[1:2]) + m[0:1]
    hb = hx.astype(BF16)
    z_ref[...] = _dot(hb, wz_ref[...]).astype(BF16)
    xbc_ref[...] = _dot(hb, wxbc_ref[...]).astype(BF16)
    dt_ref[...] = _dot(hb, wdt_ref[...])
    scb_ref[...] = _dot(hb, wb_ref[...]).astype(BF16)
    v_ref[...] = (_dot(hb, wc_ref[...]) * _dot(hb, wu_ref[...])).astype(BF16)


def _inproj(x2, mod3, g1, wz, wxbc, wdt, wb, wc, wu):
    t = x2.shape[0]
    tm = TOK_TILE
    per_b = SEQ // tm
    const = lambda *shape: pl.BlockSpec(shape, lambda i: (0,) * len(shape))
    tile = lambda w: pl.BlockSpec((tm, w), lambda i: (i, 0))
    return pl.pallas_call(
        _inproj_kernel,
        out_shape=(jax.ShapeDtypeStruct((t, D_SSD), BF16), jax.ShapeDtypeStruct((t, XBC_W), BF16),
                   jax.ShapeDtypeStruct((t, LANES), F32), jax.ShapeDtypeStruct((t, D_SC), BF16),
                   jax.ShapeDtypeStruct((t, D_SC), BF16)),
        grid=(t // tm,),
        in_specs=[tile(D_MODEL),
                  pl.BlockSpec((1, N_MOD, D_MODEL), lambda i: (i // per_b, 0, 0)),
                  const(1, D_MODEL), const(D_MODEL, D_SSD), const(D_MODEL, XBC_W), const(D_MODEL, LANES),
                  const(D_MODEL, D_SC), const(D_MODEL, D_SC), const(D_MODEL, D_SC)],
        out_specs=(tile(D_SSD), tile(XBC_W), tile(LANES), tile(D_SC), tile(D_SC)),
        compiler_params=pltpu.CompilerParams(dimension_semantics=("arbitrary",),
                                             vmem_limit_bytes=VMEM_LIMIT),
        name="inproj",
    )(x2, mod3, g1, wz, wxbc, wdt, wb, wc, wu)


def _ssd_kernel(xbc_ref, z_ref, dt_ref, h0_ref, cw_ref, cb_ref, dtb_ref, alog_ref, dsk_ref, g_ref,
                e64_ref, e128_ref, o_ref, xc_ref, y_ref, s_ref):
    Q = CHUNK
    nck = SEQ // Q
    gw = D_SSD // GROUPS

    rowi = lax.broadcasted_iota(jnp.int32, (SUBLANES, XBC_W), 0)

    def conv_body(c, carry):
        r0 = pl.multiple_of(c * Q, Q)
        main = xbc_ref[0, pl.ds(r0, Q), :].astype(F32)
        pstart = pl.multiple_of(jnp.maximum(r0 - 16, 0), 16)
        nstart = pl.multiple_of(jnp.minimum(r0 + Q, SEQ - 16), 16)
        prev = xbc_ref[0, pl.ds(pstart, 16), :].astype(F32)[15:16]
        nxt = xbc_ref[0, pl.ds(nstart, 16), :].astype(F32)[0:1]
        prev = jnp.where(c > 0, prev, 0.0)
        nxt = jnp.where(c < nck - 1, nxt, 0.0)
        dn = pltpu.roll(main, 1, 0)
        up = pltpu.roll(main, Q - 1, 0)
        dn = jnp.concatenate([jnp.where(rowi == 0, prev, dn[0:SUBLANES]), dn[SUBLANES:]], axis=0)
        up = jnp.concatenate([up[:Q - SUBLANES], jnp.where(rowi == SUBLANES - 1, nxt, up[Q - SUBLANES:])], axis=0)
        cw = cw_ref[...]
        conv = cw[0:1] * dn + cw[1:2] * main + cw[2:3] * up + cb_ref[...]
        xc_ref[pl.ds(r0, Q), :] = _silu(conv).astype(BF16)
        return carry

    lax.fori_loop(0, nck, conv_body, 0)

    ri = lax.broadcasted_iota(jnp.int32, (Q, Q), 0)
    ci = lax.broadcasted_iota(jnp.int32, (Q, Q), 1)
    lane = lax.broadcasted_iota(jnp.int32, (Q, LANES), 1)
    a_neg = -jnp.exp(alog_ref[...])

    def chunk(c, d, first):
        r0 = pl.multiple_of(c * Q, Q)
        rows = pl.ds(r0, Q)
        xs_b = xc_ref[rows, 0:D_SSD]
        xs = xs_b.astype(F32)
        bm = xc_ref[rows, D_SSD:D_SSD + GROUPS * STATE]
        cm = xc_ref[rows, D_SSD + GROUPS * STATE:XBC_W]
        dt = _softplus(dt_ref[0, rows, :] + dtb_ref[...])
        da = dt * a_neg
        tri = (ci <= ri) if d == 0 else (ci >= ri)
        p0 = da.astype(BF16)
        r1 = da - p0.astype(F32)
        p1 = r1.astype(BF16)
        p2 = (r1 - p1.astype(F32)).astype(BF16)
        tri_b = jnp.where(tri, 1.0, 0.0).astype(BF16)
        cum = _dot(jnp.concatenate([tri_b, tri_b, tri_b], axis=1),
                   jnp.concatenate([p0, p1, p2], axis=0))
        sub_t = (cum - jnp.log(dt)).T
        last = cum[Q - 1:Q] if d == 0 else cum[0:1]
        ecum_e = _expand2(jnp.exp(cum), e64_ref[d])
        w_e = _expand2(jnp.exp(last - cum) * dt, e64_ref[d])
        colb = _expand2(cum, e128_ref[d])
        decay_e = ecum_e[Q - 1:Q] if d == 0 else ecum_e[0:1]

        gmat = [lax.dot_general(cm[:, g * STATE:(g + 1) * STATE], bm[:, g * STATE:(g + 1) * STATE],
                                (((1,), (1,)), ((), ())), preferred_element_type=F32)
                for g in range(GROUPS)]
        zero_b = jnp.zeros((Q, LANES), BF16)
        y_parts = []
        for p in range(HEADS // 2):
            g = (2 * p) // (HEADS // GROUPS)
            ms = []
            for hh in (2 * p, 2 * p + 1):
                seg = colb[:, hh * LANES:(hh + 1) * LANES] - sub_t[HEADS * d + hh:HEADS * d + hh + 1, :]
                ms.append((jnp.where(tri, jnp.exp(seg), 0.0) * gmat[g]).astype(BF16))
            mcat = jnp.concatenate(ms, axis=1)
            xp = xs_b[:, p * LANES:(p + 1) * LANES]
            rhs = jnp.concatenate([jnp.where(lane < HEAD_DIM, xp, zero_b),
                                   jnp.where(lane >= HEAD_DIM, xp, zero_b)], axis=0)
            y_parts.append(_dot(mcat, rhs))
        y_diag = jnp.concatenate(y_parts, axis=1)

        s_old = s_ref[d]
        s_bf = s_old.astype(BF16)
        y_off = jnp.concatenate(
            [_dot(cm[:, g * STATE:(g + 1) * STATE], s_bf[:, g * gw:(g + 1) * gw]) for g in range(GROUPS)],
            axis=1)
        y = y_diag + y_off * ecum_e

        xw = (xs * w_e).astype(BF16)
        upd = jnp.concatenate(
            [lax.dot_general(bm[:, g * STATE:(g + 1) * STATE], xw[:, g * gw:(g + 1) * gw],
                             (((0,), (0,)), ((), ())), preferred_element_type=F32) for g in range(GROUPS)],
            axis=1)
        s_ref[d] = s_old * decay_e + upd

        if first:
            y_ref[rows, :] = y + dsk_ref[...] * xs
        else:
            tot = y_ref[rows, :] + y
            zz = z_ref[0, rows, :].astype(F32)
            gz = tot * _silu(zz)
            outs = []
            for g in range(GROUPS):
                gg = gz[:, g * gw:(g + 1) * gw]
                outs.append(gg * lax.rsqrt(jnp.mean(gg * gg, axis=-1, keepdims=True) + NORM_EPS))
            o_ref[0, rows, :] = (jnp.concatenate(outs, axis=1) * g_ref[...]).astype(BF16)

    s_ref[...] = h0_ref[0]

    def first_half(i, carry):
        chunk(i, 0, True)
        chunk(nck - 1 - i, 1, True)
        return carry

    def second_half(i, carry):
        chunk(i, 0, False)
        chunk(nck - 1 - i, 1, False)
        return carry

    lax.fori_loop(0, nck // 2, first_half, 0)
    lax.fori_loop(nck // 2, nck, second_half, 0)


def _ssd(xbc3, z3, dt3, h0, cw, cb, dtb, alog, dsk, g, e64, e128):
    bsz = xbc3.shape[0]
    const = lambda *shape: pl.BlockSpec(shape, lambda b: (0,) * len(shape))
    seq = lambda w: pl.BlockSpec((1, SEQ, w), lambda b: (b, 0, 0))
    return pl.pallas_call(
        _ssd_kernel,
        out_shape=jax.ShapeDtypeStruct((bsz, SEQ, D_SSD), BF16),
        grid=(bsz,),
        in_specs=[seq(XBC_W), seq(D_SSD), seq(LANES),
                  pl.BlockSpec((1, 2, STATE, D_SSD), lambda b: (b, 0, 0, 0)),
                  const(3, XBC_W), const(1, XBC_W), const(1, LANES), const(1, LANES),
                  const(1, D_SSD), const(1, D_SSD), const(2, 2 * LANES, D_SSD),
                  const(2, 2 * LANES, HEADS * LANES)],
        out_specs=seq(D_SSD),
        scratch_shapes=[pltpu.VMEM((SEQ, XBC_W), BF16), pltpu.VMEM((SEQ, D_SSD), F32),
                        pltpu.VMEM((2, STATE, D_SSD), F32)],
        compiler_params=pltpu.CompilerParams(dimension_semantics=("arbitrary",),
                                             vmem_limit_bytes=VMEM_LIMIT),
        name="ssd",
    )(xbc3, z3, dt3, h0, cw, cb, dtb, alog, dsk, g, e64, e128)


def _outproj_kernel(x_ref, yssd_ref, scb_ref, v_ref, vp_ref, vn_ref, mod_ref, scw_ref, wo1_ref, wo2_ref,
                    g2_ref, wrt_ref, br_ref, x1_ref, lg_ref):
    tm = TOK_TILE
    per_b = SEQ // tm
    i = pl.program_id(0)
    first = (i % per_b) == 0
    last = (i % per_b) == per_b - 1
    m = mod_ref[0]
    v = v_ref[...].astype(F32)
    vp = jnp.where(first, 0.0, vp_ref[...].astype(F32))
    vn = jnp.where(last, 0.0, vn_ref[...].astype(F32))
    dn = jnp.concatenate([vp, v[:tm - GRID_W]], axis=0)
    up = jnp.concatenate([v[GRID_W:], vn], axis=0)
    scw = scw_ref[...]
    ysc = scb_ref[...].astype(F32) * (scw[0:1] * dn + scw[1:2] * v + scw[2:3] * up)
    out = _dot(yssd_ref[...], wo1_ref[...]) + _dot(ysc.astype(BF16), wo2_ref[...])
    x1 = x_ref[...] + m[2:3] * out
    x1_ref[...] = x1
    h2 = _rms(x1) * g2_ref[...] * (1.0 + m[4:5]) + m[3:4]
    h_hi = h2.astype(BF16)
    h_lo = (h2 - h_hi.astype(F32)).astype(BF16)
    lg_ref[...] = lax.dot_general(wrt_ref[...], jnp.concatenate([h_hi, h_lo, h_hi], axis=1),
                                  (((1,), (1,)), ((), ())), preferred_element_type=F32) + br_ref[...]


def _outproj(x2, yssd, scb, v, mod3, scw, wo1, wo2, g2, wrt, br):
    t = x2.shape[0]
    tm = TOK_TILE
    per_b = SEQ // tm
    r = tm // GRID_W
    nrow = t // GRID_W
    const = lambda *shape: pl.BlockSpec(shape, lambda i: (0,) * len(shape))
    tile = lambda w: pl.BlockSpec((tm, w), lambda i: (i, 0))
    return pl.pallas_call(
        _outproj_kernel,
        out_shape=(jax.ShapeDtypeStruct((t, D_MODEL), F32), jax.ShapeDtypeStruct((N_EXPERTS, t), F32)),
        grid=(t // tm,),
        in_specs=[tile(D_MODEL), tile(D_SSD), tile(D_SC), tile(D_SC),
                  pl.BlockSpec((GRID_W, D_SC), lambda i: (jnp.maximum(i * r - 1, 0), 0)),
                  pl.BlockSpec((GRID_W, D_SC), lambda i: (jnp.minimum((i + 1) * r, nrow - 1), 0)),
                  pl.BlockSpec((1, N_MOD, D_MODEL), lambda i: (i // per_b, 0, 0)),
                  const(3, D_SC), const(D_SSD, D_MODEL), const(D_SC, D_MODEL), const(1, D_MODEL),
                  const(N_EXPERTS, 3 * D_MODEL), const(N_EXPERTS, 1)],
        out_specs=(tile(D_MODEL), pl.BlockSpec((N_EXPERTS, tm), lambda i: (0, i))),
        compiler_params=pltpu.CompilerParams(dimension_semantics=("arbitrary",),
                                             vmem_limit_bytes=VMEM_LIMIT),
        name="outproj",
    )(x2, yssd, scb, v, v, v, mod3, scw, wo1, wo2, g2, wrt, br)


def _route_kernel(lg_ref, dest_ref, gate_ref, idx_ref, meta_ref, rank_ref, carry_ref, *, n_tok, n_blocks):
    tt = RT_TILE
    ne = N_EXPERTS
    eio = lax.broadcasted_iota(jnp.int32, (ne, tt), 0)
    si = lax.broadcasted_iota(jnp.int32, (tt, tt), 0)
    ti = lax.broadcasted_iota(jnp.int32, (tt, tt), 1)
    before = (si < ti).astype(BF16)
    carry_ref[...] = jnp.zeros_like(carry_ref)

    def tile_body(j, c):
        t0 = pl.multiple_of(j * tt, tt)
        l = lg_ref[:, pl.ds(t0, tt)]
        onehot = jnp.zeros((ne, tt), F32)
        tops, sels = [], []
        for _ in range(TOP_K):
            mx = jnp.max(l, axis=0, keepdims=True)
            idx = jnp.min(jnp.where(l == mx, eio, ne), axis=0, keepdims=True)
            sel = eio == idx
            l = jnp.where(sel, -jnp.inf, l)
            onehot = onehot + sel.astype(F32)
            tops.append(mx)
            sels.append(sel)
            idx_ref[pl.ds(len(tops) - 1, 1), pl.ds(t0, tt)] = idx
        ex = [jnp.exp(tv - tops[0]) for tv in tops]
        den = ex[0] + ex[1] + ex[2] + ex[3]
        prefix = _dot(onehot.astype(BF16), before) + carry_ref[:, 0:1]
        for k in range(TOP_K):
            gate_ref[pl.ds(k, 1), pl.ds(t0, tt)] = ex[k] / den
            rk = jnp.sum(jnp.where(sels[k], prefix, 0.0), axis=0, keepdims=True)
            rank_ref[pl.ds(k, 1), pl.ds(t0, tt)] = rk.astype(jnp.int32)
        carry_ref[...] = carry_ref[...] + jnp.sum(onehot, axis=1, keepdims=True)
        return c

    lax.fori_loop(0, n_tok // tt, tile_body, 0)

    counts = carry_ref[...]
    padded = jnp.floor((counts + (MOE_BM - 1)) * (1.0 / MOE_BM)) * MOE_BM
    er = lax.broadcasted_iota(jnp.int32, (ne, ne), 0)
    ec = lax.broadcasted_iota(jnp.int32, (ne, ne), 1)
    pad_start = jnp.dot((ec < er).astype(F32), padded, precision=HIGHEST, preferred_element_type=F32)
    pad_end = pad_start + padded

    def dest_body(j, c):
        t0 = pl.multiple_of(j * tt, tt)
        for k in range(TOP_K):
            idx = idx_ref[pl.ds(k, 1), pl.ds(t0, tt)]
            base = jnp.sum(jnp.where(eio == idx, pad_start[:, 0:1], 0.0), axis=0, keepdims=True)
            dest_ref[pl.ds(k, 1), pl.ds(t0, tt)] = base.astype(jnp.int32) + rank_ref[pl.ds(k, 1), pl.ds(t0, tt)]
        return c

    lax.fori_loop(0, n_tok // tt, dest_body, 0)

    width = meta_ref.shape[1]
    sub = lax.broadcasted_iota(jnp.int32, (ne, width), 0)
    lan = lax.broadcasted_iota(jnp.int32, (ne, width), 1)
    diag = sub == lan
    cnt_row = jnp.sum(jnp.where(diag, counts[:, 0:1], 0.0), axis=0, keepdims=True)
    start_row = jnp.sum(jnp.where(diag, pad_start[:, 0:1], 0.0), axis=0, keepdims=True)
    blk_start = (lan * MOE_BM).astype(F32)
    blk_exp = jnp.sum((pad_end[:, 0:1] <= blk_start).astype(F32), axis=0, keepdims=True)
    blk_exp = jnp.minimum(blk_exp, float(ne - 1))
    used = jnp.sum(padded[:, 0:1], axis=0, keepdims=True) * (1.0 / MOE_BM)
    meta_ref[0:1, :] = cnt_row.astype(jnp.int32)
    meta_ref[1:2, :] = start_row.astype(jnp.int32)
    meta_ref[2:3, :] = blk_exp.astype(jnp.int32)
    meta_ref[3:4, :] = jnp.broadcast_to(used, (1, width)).astype(jnp.int32)
    meta_ref[4:8, :] = jnp.zeros((4, width), jnp.int32)


def _route(lgt, n_blocks):
    ne, n_tok = lgt.shape
    width = -(-n_blocks // LANES) * LANES
    full = lambda *shape: pl.BlockSpec(shape, lambda: (0,) * len(shape))
    return pl.pallas_call(
        functools.partial(_route_kernel, n_tok=n_tok, n_blocks=n_blocks),
        out_shape=(jax.ShapeDtypeStruct((TOP_K, n_tok), jnp.int32),
                   jax.ShapeDtypeStruct((TOP_K, n_tok), F32),
                   jax.ShapeDtypeStruct((TOP_K, n_tok), jnp.int32),
                   jax.ShapeDtypeStruct((8, width), jnp.int32)),
        in_specs=[full(ne, n_tok)],
        out_specs=(full(TOP_K, n_tok), full(TOP_K, n_tok), full(TOP_K, n_tok), full(8, width)),
        scratch_shapes=[pltpu.VMEM((TOP_K, n_tok), jnp.int32), pltpu.VMEM((ne, LANES), F32)],
        compiler_params=pltpu.CompilerParams(vmem_limit_bytes=VMEM_LIMIT),
        name="route",
    )(lgt)


def _dispatch_kernel(dest_ref, cnt_ref, start_ref, nu_ref, x1_ref, meta_ref, mod_ref, g2_ref, zsrc_ref, xs_ref,
                     hbuf, sem, zsem):
    i = pl.program_id(0)
    n = pl.num_programs(0)
    tl = DISP_TILE
    slot = i % 2
    nb = xs_ref.shape[0] // MOE_BM

    def zero_block(b):
        return pltpu.make_async_copy(zsrc_ref, xs_ref.at[pl.ds(b * MOE_BM, MOE_BM)], zsem)

    @pl.when(i == 0)
    def _():
        def start_e(e, c):
            @pl.when(cnt_ref[e] > 0)
            def _():
                zero_block((start_ref[e] + cnt_ref[e] - 1) // MOE_BM).start()
            return c

        def wait_e(e, c):
            @pl.when(cnt_ref[e] > 0)
            def _():
                zero_block(0).wait()
            return c

        def start_t(b, c):
            zero_block(b).start()
            return c

        def wait_t(b, c):
            zero_block(0).wait()
            return c

        lax.fori_loop(0, N_EXPERTS, start_e, 0)
        lax.fori_loop(nu_ref[0], nb, start_t, 0)
        lax.fori_loop(0, N_EXPERTS, wait_e, 0)
        lax.fori_loop(nu_ref[0], nb, wait_t, 0)

    m = mod_ref[0]
    h2 = _rms(x1_ref[...]) * g2_ref[...] * (1.0 + m[4:5]) + m[3:4]
    lo = pltpu.bitcast(h2[:, :PACK_W].astype(BF16).astype(F32), jnp.uint32) >> 16
    hi = pltpu.bitcast(h2[:, PACK_W:].astype(BF16).astype(F32), jnp.uint32) & jnp.uint32(0xFFFF0000)
    row = jnp.concatenate([lo | hi, meta_ref[...], jnp.zeros((tl, D_MODEL - PACK_W - LANES), jnp.uint32)], axis=1)
    hbuf[slot] = row.reshape(tl, SUBLANES, LANES)

    def per_tok(t, c):
        tok = i * tl + t
        for k in range(TOP_K):
            d = dest_ref[tok * TOP_K + k]
            pltpu.make_async_copy(hbuf.at[slot, t], xs_ref.at[d], sem.at[slot]).start(priority=k % 2)
        return c

    lax.fori_loop(0, tl, per_tok, 0, unroll=8)

    def wait_slot(sl):
        for _ in range(TOP_K):
            pltpu.make_async_copy(hbuf.at[sl], xs_ref.at[pl.ds(0, tl)], sem.at[sl]).wait()

    @pl.when(i > 0)
    def _():
        wait_slot(1 - slot)

    @pl.when(i == n - 1)
    def _():
        wait_slot(slot)


def _dispatch(dest_flat, cnt, start, n_used, x1, meta_rows, mod3, g2, zsrc, n_rows):
    n_tok = x1.shape[0]
    tl = DISP_TILE
    per_b = SEQ // tl
    return pl.pallas_call(
        _dispatch_kernel,
        out_shape=jax.ShapeDtypeStruct((n_rows, SUBLANES, LANES), jnp.uint32),
        grid_spec=pltpu.PrefetchScalarGridSpec(
            num_scalar_prefetch=4,
            grid=(n_tok // tl,),
            in_specs=[pl.BlockSpec((tl, D_MODEL), lambda i, *_: (i, 0)),
                      pl.BlockSpec((tl, LANES), lambda i, *_: (i, 0)),
                      pl.BlockSpec((1, N_MOD, D_MODEL), lambda i, *_: (i // per_b, 0, 0)),
                      pl.BlockSpec((1, D_MODEL), lambda i, *_: (0, 0)),
                      pl.BlockSpec((MOE_BM, SUBLANES, LANES), lambda i, *_: (0, 0, 0))],
            out_specs=pl.BlockSpec(memory_space=pl.ANY),
            scratch_shapes=[pltpu.VMEM((2, tl, SUBLANES, LANES), jnp.uint32),
                            pltpu.SemaphoreType.DMA((2,)), pltpu.SemaphoreType.DMA]),
        compiler_params=pltpu.CompilerParams(dimension_semantics=("arbitrary",),
                                             vmem_limit_bytes=VMEM_LIMIT),
        name="dispatch",
    )(dest_flat, cnt, start, n_used, x1, meta_rows, mod3, g2, zsrc)


def _expert_rows_kernel(be_ref, nu_ref, xs_ref, wgu_hbm, bgu_ref, wd_hbm, bd_ref, ys_ref,
                        wgu_raw, wd_raw, wgu_bf, wd_bf, w_sem):
    j = pl.program_id(0)
    nu = nu_ref[0]
    n_lt = D_MODEL // LANES

    def weight_copies(e):
        return (pltpu.make_async_copy(wgu_hbm.at[e], wgu_raw, w_sem.at[0]),
                pltpu.make_async_copy(wd_hbm.at[e], wd_raw, w_sem.at[1]))

    @pl.when(j == 0)
    def _():
        for w in weight_copies(be_ref[0]):
            w.start()

    e_now = be_ref[j]
    new_expert = jnp.logical_or(j == 0, be_ref[jnp.maximum(j - 1, 0)] != e_now)

    @pl.when(jnp.logical_and(j < nu, new_expert))
    def _():
        for w in weight_copies(e_now):
            w.wait()
        n_cc = 8
        for c in range(n_cc):
            cc = slice(c * (2 * D_FF // n_cc), (c + 1) * (2 * D_FF // n_cc))
            wgu_bf[:, cc] = wgu_raw[:, cc].astype(BF16)
        for c in range(n_cc // 2):
            cc = slice(c * (2 * D_MODEL // n_cc), (c + 1) * (2 * D_MODEL // n_cc))
            wd_bf[:, cc] = wd_raw[:, cc].astype(BF16)
        j_next = lax.while_loop(lambda t: jnp.logical_and(t < nu, be_ref[jnp.minimum(t, pl.num_programs(0) - 1)] == e_now),
                                lambda t: t + 1, j + 1)

        @pl.when(j_next < nu)
        def _():
            for w in weight_copies(be_ref[jnp.minimum(j_next, pl.num_programs(0) - 1)]):
                w.start()

    @pl.when(j < nu)
    def _():
        words = xs_ref[...].reshape(MOE_BM, D_MODEL)
        packed = words[:, 0:PACK_W]
        meta = pltpu.bitcast(words[:, PACK_W:PACK_W + LANES], F32)
        xb = jnp.concatenate(
            [pltpu.bitcast(packed << 16, F32).astype(BF16),
             pltpu.bitcast(packed & jnp.uint32(0xFFFF0000), F32).astype(BF16)], axis=1)
        e_f = e_now.astype(F32)
        gate = jnp.zeros((MOE_BM, 1), F32)
        for k in range(TOP_K):
            mk = meta[:, META_IDX + k:META_IDX + k + 1] == e_f
            gate = gate + jnp.where(mk, meta[:, META_GATE + k:META_GATE + k + 1], 0.0)
        gu = _dot(xb, wgu_bf[...]) + bgu_ref[0]
        glu = jnp.minimum(gu[:, :D_FF], SWIGLU_LIMIT)
        lin = jnp.clip(gu[:, D_FF:], -SWIGLU_LIMIT, SWIGLU_LIMIT)
        act = glu * jax.nn.sigmoid(SWIGLU_ALPHA * glu) * (lin + 1.0)
        y = (_dot(act.astype(BF16), wd_bf[...]) + bd_ref[0]) * gate
        ys_ref[...] = y.reshape(MOE_BM, n_lt, LANES)

    @pl.when(j >= nu)
    def _():
        ys_ref[...] = jnp.zeros_like(ys_ref)


def _expert_rows(blk_exp, n_used, xs, wgu, bgu, wd, bd):
    n_rows = xs.shape[0]
    nb = n_rows // MOE_BM
    n_lt = D_MODEL // LANES
    row_blk = lambda j, be, nu: (jnp.minimum(j, nu[0] - 1), 0, 0)
    per_e = lambda j, be, nu: (be[j], 0, 0)
    return pl.pallas_call(
        _expert_rows_kernel,
        out_shape=jax.ShapeDtypeStruct((n_rows, n_lt, LANES), F32),
        grid_spec=pltpu.PrefetchScalarGridSpec(
            num_scalar_prefetch=2,
            grid=(nb,),
            in_specs=[pl.BlockSpec((MOE_BM, SUBLANES, LANES), row_blk),
                      pl.BlockSpec(memory_space=pl.ANY),
                      pl.BlockSpec((1, 1, 2 * D_FF), per_e),
                      pl.BlockSpec(memory_space=pl.ANY),
                      pl.BlockSpec((1, 1, D_MODEL), per_e)],
            out_specs=pl.BlockSpec((MOE_BM, n_lt, LANES), lambda j, be, nu: (j, 0, 0)),
            scratch_shapes=[pltpu.VMEM((D_MODEL, 2 * D_FF), F32), pltpu.VMEM((D_FF, D_MODEL), F32),
                            pltpu.VMEM((D_MODEL, 2 * D_FF), BF16), pltpu.VMEM((D_FF, D_MODEL), BF16),
                            pltpu.SemaphoreType.DMA((2,))]),
        compiler_params=pltpu.CompilerParams(dimension_semantics=("arbitrary",),
                                             vmem_limit_bytes=VMEM_LIMIT),
        name="experts",
    )(blk_exp, n_used, xs, wgu, bgu, wd, bd)


def _gather_combine_kernel(dest_ref, ys_ref, x1_ref, mod_ref, fg_ref, o_ref, buf, sem):
    i = pl.program_id(0)
    n = pl.num_programs(0)
    tc = GATHER_TILE
    slot = i % 2

    def issue(tile, sl, t0, t1):
        base = tile * (tc * TOP_K)
        for t in range(t0, t1):
            for k in range(TOP_K):
                pltpu.make_async_copy(ys_ref.at[dest_ref[base + t * TOP_K + k]], buf.at[sl, k * tc + t],
                                      sem.at[sl]).start(priority=k % 2)

    def reduce_rows(sl, r0, nr):
        rows = lambda k: buf[sl, pl.ds(k * tc + r0, nr)]
        moe = ((rows(0) + rows(1)) + (rows(2) + rows(3))).reshape(nr, D_MODEL)
        m = mod_ref[0]
        x2 = x1_ref[pl.ds(r0, nr), :] + m[5:6] * moe
        o_ref[pl.ds(r0, nr), :] = _rms(x2) * fg_ref[...]

    @pl.when(i == 0)
    def _():
        issue(0, 0, 0, tc)

    n_part = 8
    pr = tc // n_part

    def step(sl):
        for _ in range(TOP_K):
            pltpu.make_async_copy(ys_ref.at[pl.ds(0, tc)], buf.at[sl, pl.ds(0, tc)], sem.at[sl]).wait()

        @pl.when(i + 1 < n)
        def _():
            for p in range(n_part):
                issue(i + 1, 1 - sl, p * pr, (p + 1) * pr)
                reduce_rows(sl, p * pr, pr)

        @pl.when(i + 1 == n)
        def _():
            for p in range(n_part):
                reduce_rows(sl, p * pr, pr)

    for sl in range(2):
        pl.when(slot == sl)(functools.partial(step, sl))


def _gather_combine(dest_flat, ys, x1, mod3, fg):
    n_tok = x1.shape[0]
    tc = GATHER_TILE
    per_b = SEQ // tc
    n_lt = D_MODEL // LANES
    return pl.pallas_call(
        _gather_combine_kernel,
        out_shape=jax.ShapeDtypeStruct((n_tok, D_MODEL), F32),
        grid_spec=pltpu.PrefetchScalarGridSpec(
            num_scalar_prefetch=1,
            grid=(n_tok // tc,),
            in_specs=[pl.BlockSpec(memory_space=pl.ANY),
                      pl.BlockSpec((tc, D_MODEL), lambda i, d: (i, 0)),
                      pl.BlockSpec((1, N_MOD, D_MODEL), lambda i, d: (i // per_b, 0, 0)),
                      pl.BlockSpec((1, D_MODEL), lambda i, d: (0, 0))],
            out_specs=pl.BlockSpec((tc, D_MODEL), lambda i, d: (i, 0)),
            scratch_shapes=[pltpu.VMEM((2, TOP_K * tc, n_lt, LANES), F32), pltpu.SemaphoreType.DMA((2,))]),
        compiler_params=pltpu.CompilerParams(dimension_semantics=("arbitrary",),
                                             vmem_limit_bytes=VMEM_LIMIT),
        name="combine",
    )(dest_flat, ys, x1, mod3, fg)


def _expansion_matrices():
    r = (jnp.arange(2 * LANES) % LANES)[:, None]
    out64, out128 = [], []
    for d in range(2):
        l64 = jnp.arange(D_SSD)[None, :]
        l128 = jnp.arange(HEADS * LANES)[None, :]
        out64.append((l64 // HEAD_DIM == r - HEADS * d).astype(BF16))
        out128.append((l128 // LANES == r - HEADS * d).astype(BF16))
    return jnp.stack(out64), jnp.stack(out128)


def _pad_lanes(v):
    return jnp.pad(v, [(0, 0)] * (v.ndim - 1) + [(0, LANES - v.shape[-1])])


def kernel(x, c, ctx, c_ctx, w_mod, b_mod, norm1_g, w_in, ssd_conv_w, ssd_conv_b, ssd_dt_bias, ssd_a_log,
           ssd_d, ssd_norm_g, sc_conv_w, w_out, norm2_g, w_router, b_router, w_gate_up, b_gate_up, w_down,
           b_down, final_g):
    bsz = x.shape[0]
    n_tok = bsz * SEQ
    n_assign = n_tok * TOP_K
    n_blocks = n_assign // MOE_BM + N_EXPERTS
    n_rows = n_blocks * MOE_BM
    li = 0

    cvec = jnp.concatenate([c, c_ctx[None, :], jnp.zeros((7, D_MODEL), F32)], axis=0)
    mod3 = _mod(cvec, w_mod[li], b_mod[li][None, :]).reshape(bsz + 8, N_MOD, D_MODEL)

    w = w_in[li]
    wz = w[:, Z0:X0].astype(BF16)
    wxbc = w[:, X0:DT0].astype(BF16)
    wdt = _pad_lanes(w[:, DT0:SC0]).astype(BF16)
    wb = w[:, SC0:SC0 + D_SC].astype(BF16)
    wc = w[:, SC0 + D_SC:SC0 + 2 * D_SC].astype(BF16)
    wu = w[:, SC0 + 2 * D_SC:].astype(BF16)
    g1 = norm1_g[li][None, :]
    cw = ssd_conv_w[li]
    cb = ssd_conv_b[li][None, :]
    dtb = _pad_lanes(ssd_dt_bias[li].reshape(1, 2 * HEADS))
    alog = _pad_lanes(ssd_a_log[li].reshape(1, 2 * HEADS))
    e64, e128 = _expansion_matrices()

    h0 = _ctx_states(ctx, mod3, g1, wxbc[:, :XB_W], wdt, cw[:, :XB_W], cb[:, :XB_W], dtb, alog, e64)

    x2 = x.reshape(n_tok, D_MODEL)
    z, xbc, dtr, scb, v = _inproj(x2, mod3, g1, wz, wxbc, wdt, wb, wc, wu)

    dsk = jnp.repeat(ssd_d[li], HEAD_DIM)[None, :]
    yssd = _ssd(xbc.reshape(bsz, SEQ, XBC_W), z.reshape(bsz, SEQ, D_SSD), dtr.reshape(bsz, SEQ, LANES), h0,
                cw, cb, dtb, alog, dsk, ssd_norm_g[li][None, :], e64, e128)

    wo = w_out[li].astype(BF16)
    g2 = norm2_g[li][None, :]
    wr = w_router[li].T
    wr_hi = wr.astype(BF16)
    wr_lo = (wr - wr_hi.astype(F32)).astype(BF16)
    x1, lgt = _outproj(x2, yssd.reshape(n_tok, D_SSD), scb, v, mod3, sc_conv_w[li], wo[:D_SSD], wo[D_SSD:],
                       g2, jnp.concatenate([wr_hi, wr_hi, wr_lo], axis=1), b_router[li][:, None])

    dest_t, gate_t, idx_t, meta = _route(lgt, n_blocks)
    dest_flat = dest_t.T.reshape(n_assign)
    cnt = meta[0, :N_EXPERTS]
    start = meta[1, :N_EXPERTS]
    blk_exp = meta[2, :n_blocks]
    n_used = meta[3, :1]

    meta_rows = lax.bitcast_convert_type(_pad_lanes(jnp.concatenate(
        [idx_t.T.astype(F32), gate_t.T], axis=1)), jnp.uint32)
    pad_meta = lax.bitcast_convert_type(_pad_lanes(jnp.concatenate(
        [jnp.full((MOE_BM, TOP_K), -1.0, F32), jnp.zeros((MOE_BM, TOP_K), F32)], axis=1)), jnp.uint32)
    zsrc = jnp.concatenate([jnp.zeros((MOE_BM, PACK_W), jnp.uint32), pad_meta,
                            jnp.zeros((MOE_BM, D_MODEL - PACK_W - LANES), jnp.uint32)],
                           axis=1).reshape(MOE_BM, SUBLANES, LANES)

    xs = _dispatch(dest_flat, cnt, start, n_used, x1, meta_rows, mod3, g2, zsrc, n_rows)
    ys = _expert_rows(blk_exp, n_used, xs, w_gate_up[li], b_gate_up[li][:, None, :],
                      w_down[li], b_down[li][:, None, :])
    out = _gather_combine(dest_flat, ys, x1, mod3, final_g[None, :])
    return out.reshape(bsz, SEQ, D_MODEL)
```

```python
import functools

import jax
import jax.numpy as jnp
from jax import lax
from jax.experimental import pallas as pl
from jax.experimental.pallas import tpu as pltpu

F32 = jnp.float32
BF16 = jnp.bfloat16
HIGHEST = lax.Precision.HIGHEST

D_MODEL = 1024
SEQ = 2048
CTX_LEN = 256
GRID_W = 64
D_SSD = 1024
D_SC = 1024
HEAD_DIM = 64
HEADS = 16
GROUPS = 2
STATE = 128
CHUNK = 128
N_EXPERTS = 32
TOP_K = 4
D_FF = 1024
SWIGLU_LIMIT = 7.0
SWIGLU_ALPHA = 1.702
NORM_EPS = 1e-6
N_MOD = 6
XBC_W = D_SSD + 2 * GROUPS * STATE
XB_W = D_SSD + GROUPS * STATE
LANES = 128

Z0 = 0
X0 = Z0 + D_SSD
B0 = X0 + D_SSD
C0 = B0 + GROUPS * STATE
DT0 = C0 + GROUPS * STATE
SC0 = DT0 + 2 * HEADS

TOK_TILE = 512
MOE_BM = 256
RT_TILE = 512
DISP_TILE = 256
GATHER_TILE = 256
DT_COPIES = 3
DT_LOG0 = 2 * HEADS
DT_DA0 = 4 * HEADS
SUBLANES = 8
PACK_W = D_MODEL // 2
META_IDX = 0
META_GATE = TOP_K
VMEM_LIMIT = 56 * 1024 * 1024


def _silu(v):
    return v * jax.nn.sigmoid(v)


def _softplus(v):
    return jnp.maximum(v, 0.0) + jnp.log1p(jnp.exp(-jnp.abs(v)))


def _rms(v):
    return v * lax.rsqrt(jnp.mean(v * v, axis=-1, keepdims=True) + NORM_EPS)


def _dot(a, b):
    return jnp.dot(a, b, preferred_element_type=F32)


def _expand2(v, e2):
    hi = v.astype(BF16)
    lo = (v - hi.astype(F32)).astype(BF16)
    return _dot(jnp.concatenate([hi, lo], axis=1), e2)


def _mod_kernel(c_ref, w_ref, b_ref, o_ref):
    o_ref[...] = jnp.dot(_silu(c_ref[...]), w_ref[...], precision=HIGHEST,
                         preferred_element_type=F32) + b_ref[...]


def _mod(cvec, w_mod, b_mod):
    rows = cvec.shape[0]
    n = w_mod.shape[1]
    tn = 1536
    return pl.pallas_call(
        _mod_kernel,
        out_shape=jax.ShapeDtypeStruct((rows, n), F32),
        grid=(n // tn,),
        in_specs=[pl.BlockSpec((rows, D_MODEL), lambda j: (0, 0)),
                  pl.BlockSpec((D_MODEL, tn), lambda j: (0, j)),
                  pl.BlockSpec((1, tn), lambda j: (0, j))],
        out_specs=pl.BlockSpec((rows, tn), lambda j: (0, j)),
        compiler_params=pltpu.CompilerParams(dimension_semantics=("arbitrary",),
                                             vmem_limit_bytes=VMEM_LIMIT),
        name="mod",
    )(cvec, w_mod, b_mod)


def _ctx_kernel(ctx_ref, mod_ref, g1_ref, wxb_ref, wdt_ref, cw_ref, cb_ref, dtb_ref, alog_ref, e64_ref,
                h0_ref):
    L = CTX_LEN
    m = mod_ref[0]
    hc = _rms(ctx_ref[0]) * g1_ref[...] * (1.0 + m[1:2]) + m[0:1]
    hb = hc.astype(BF16)
    pxb = _dot(hb, wxb_ref[...])
    dtr = _dot(hb, wdt_ref[...])
    rowi = lax.broadcasted_iota(jnp.int32, (L, XB_W), 0)
    dn = jnp.where(rowi == 0, 0.0, pltpu.roll(pxb, 1, 0))
    up = jnp.where(rowi == L - 1, 0.0, pltpu.roll(pxb, L - 1, 0))
    cw = cw_ref[...]
    xb = _silu(cw[0:1] * dn + cw[1:2] * pxb + cw[2:3] * up + cb_ref[...])
    xs = xb[:, :D_SSD]
    bm = xb[:, D_SSD:].astype(BF16)
    dt = _softplus(dtr + dtb_ref[...])
    da = dt * (-jnp.exp(alog_ref[...]))
    ri = lax.broadcasted_iota(jnp.int32, (L, L), 0)
    ci = lax.broadcasted_iota(jnp.int32, (L, L), 1)
    for d in range(2):
        tri = (ci <= ri) if d == 0 else (ci >= ri)
        cum = jnp.dot(tri.astype(F32), da, precision=HIGHEST, preferred_element_type=F32)
        last = cum[L - 1:L] if d == 0 else cum[0:1]
        w_e = _expand2(jnp.exp(last - cum) * dt, e64_ref[d])
        xw = (xs * w_e).astype(BF16)
        for g in range(GROUPS):
            gw = D_SSD // GROUPS
            st = lax.dot_general(bm[:, g * STATE:(g + 1) * STATE], xw[:, g * gw:(g + 1) * gw],
                                 (((0,), (0,)), ((), ())), preferred_element_type=F32)
            h0_ref[0, d, :, g * gw:(g + 1) * gw] = st


def _ctx_states(ctx, mod3, g1, wxb, wdt, cw, cb, dtb, alog, e64):
    bsz = ctx.shape[0]
    mod_row = bsz
    const = lambda *shape: pl.BlockSpec(shape, lambda b: (0,) * len(shape))
    return pl.pallas_call(
        _ctx_kernel,
        out_shape=jax.ShapeDtypeStruct((bsz, 2, STATE, D_SSD), F32),
        grid=(bsz,),
        in_specs=[pl.BlockSpec((1, CTX_LEN, D_MODEL), lambda b: (b, 0, 0)),
                  pl.BlockSpec((1, N_MOD, D_MODEL), lambda b: (mod_row, 0, 0)),
                  const(1, D_MODEL), const(D_MODEL, XB_W), const(D_MODEL, LANES),
                  const(3, XB_W), const(1, XB_W), const(1, LANES), const(1, LANES),
                  const(2, 2 * LANES, D_SSD)],
        out_specs=pl.BlockSpec((1, 2, STATE, D_SSD), lambda b: (b, 0, 0, 0)),
        compiler_params=pltpu.CompilerParams(dimension_semantics=("arbitrary",),
                                             vmem_limit_bytes=VMEM_LIMIT),
        name="ctx_states",
    )(ctx, mod3, g1, wxb, wdt, cw, cb, dtb, alog, e64)


def _inproj_kernel(x_ref, mod_ref, g1_ref, wz_ref, wxbc_ref, wdt_ref, dtb_ref, alog_ref, wb_ref, wc_ref, wu_ref,
                   z_ref, xbc_ref, dt_ref, scb_ref, v_ref):
    m = mod_ref[0]
    hx = _rms(x_ref[...]) * g1_ref[...] * (1.0 + m[1:2]) + m[0:1]
    hb = hx.astype(BF16)
    z_ref[...] = _dot(hb, wz_ref[...]).astype(BF16)
    xbc_ref[...] = _dot(hb, wxbc_ref[...]).astype(BF16)
    dt = _softplus(_dot(hb, wdt_ref[...]) + dtb_ref[...])
    lane = lax.broadcasted_iota(jnp.int32, dt.shape, 1)
    dt_ref[...] = jnp.where(lane < DT_LOG0, dt,
                            jnp.where(lane < DT_DA0, jnp.log(dt), dt * (-jnp.exp(alog_ref[...]))))
    scb_ref[...] = _dot(hb, wb_ref[...]).astype(BF16)
    v_ref[...] = (_dot(hb, wc_ref[...]) * _dot(hb, wu_ref[...])).astype(BF16)


def _inproj(x2, mod3, g1, wz, wxbc, wdt, dtb, alog, wb, wc, wu):
    t = x2.shape[0]
    tm = TOK_TILE
    per_b = SEQ // tm
    const = lambda *shape: pl.BlockSpec(shape, lambda i: (0,) * len(shape))
    tile = lambda w: pl.BlockSpec((tm, w), lambda i: (i, 0))
    return pl.pallas_call(
        _inproj_kernel,
        out_shape=(jax.ShapeDtypeStruct((t, D_SSD), BF16), jax.ShapeDtypeStruct((t, XBC_W), BF16),
                   jax.ShapeDtypeStruct((t, LANES), F32), jax.ShapeDtypeStruct((t, D_SC), BF16),
                   jax.ShapeDtypeStruct((t, D_SC), BF16)),
        grid=(t // tm,),
        in_specs=[tile(D_MODEL),
                  pl.BlockSpec((1, N_MOD, D_MODEL), lambda i: (i // per_b, 0, 0)),
                  const(1, D_MODEL), const(D_MODEL, D_SSD), const(D_MODEL, XBC_W), const(D_MODEL, LANES),
                  const(1, LANES), const(1, LANES),
                  const(D_MODEL, D_SC), const(D_MODEL, D_SC), const(D_MODEL, D_SC)],
        out_specs=(tile(D_SSD), tile(XBC_W), tile(LANES), tile(D_SC), tile(D_SC)),
        compiler_params=pltpu.CompilerParams(dimension_semantics=("arbitrary",),
                                             vmem_limit_bytes=VMEM_LIMIT),
        name="inproj",
    )(x2, mod3, g1, wz, wxbc, wdt, dtb, alog, wb, wc, wu)


def _ssd_kernel(xbc_ref, z_ref, dt_ref, h0_ref, cw_ref, cb_ref, dsk_ref, g_ref,
                e64_ref, e128_ref, o_ref, xc_ref, y_ref, s_ref):
    Q = CHUNK
    nck = SEQ // Q
    gw = D_SSD // GROUPS

    rowi = lax.broadcasted_iota(jnp.int32, (SUBLANES, XBC_W), 0)

    def conv_body(c, carry):
        r0 = pl.multiple_of(c * Q, Q)
        main = xbc_ref[0, pl.ds(r0, Q), :].astype(F32)
        pstart = pl.multiple_of(jnp.maximum(r0 - 16, 0), 16)
        nstart = pl.multiple_of(jnp.minimum(r0 + Q, SEQ - 16), 16)
        prev = xbc_ref[0, pl.ds(pstart, 16), :].astype(F32)[15:16]
        nxt = xbc_ref[0, pl.ds(nstart, 16), :].astype(F32)[0:1]
        prev = jnp.where(c > 0, prev, 0.0)
        nxt = jnp.where(c < nck - 1, nxt, 0.0)
        dn = pltpu.roll(main, 1, 0)
        up = pltpu.roll(main, Q - 1, 0)
        dn = jnp.concatenate([jnp.where(rowi == 0, prev, dn[0:SUBLANES]), dn[SUBLANES:]], axis=0)
        up = jnp.concatenate([up[:Q - SUBLANES], jnp.where(rowi == SUBLANES - 1, nxt, up[Q - SUBLANES:])], axis=0)
        cw = cw_ref[...]
        conv = cw[0:1] * dn + cw[1:2] * main + cw[2:3] * up + cb_ref[...]
        xc_ref[pl.ds(r0, Q), :] = _silu(conv).astype(BF16)
        return carry

    lax.fori_loop(0, nck, conv_body, 0)

    ri = lax.broadcasted_iota(jnp.int32, (Q, Q), 0)
    ci = lax.broadcasted_iota(jnp.int32, (Q, Q), 1)
    lane = lax.broadcasted_iota(jnp.int32, (Q, LANES), 1)
    da_lanes = jnp.logical_and(lane >= DT_DA0, lane < DT_DA0 + 2 * HEADS)

    def chunk(c, d, first):
        r0 = pl.multiple_of(c * Q, Q)
        rows = pl.ds(r0, Q)
        xs_b = xc_ref[rows, 0:D_SSD]
        xs = xs_b.astype(F32)
        bm = xc_ref[rows, D_SSD:D_SSD + GROUPS * STATE]
        cm = xc_ref[rows, D_SSD + GROUPS * STATE:XBC_W]
        dtp = dt_ref[0, rows, :]
        da = jnp.where(da_lanes, dtp, 0.0)
        dt = pltpu.roll(dtp, DT_DA0, 1)
        log_dt = pltpu.roll(dtp, DT_DA0 - DT_LOG0, 1)
        tri = (ci <= ri) if d == 0 else (ci >= ri)
        p0 = da.astype(BF16)
        r1 = da - p0.astype(F32)
        p1 = r1.astype(BF16)
        p2 = (r1 - p1.astype(F32)).astype(BF16)
        tri_b = jnp.where(tri, 1.0, 0.0).astype(BF16)
        cum = _dot(jnp.concatenate([tri_b, tri_b, tri_b], axis=1),
                   jnp.concatenate([p0, p1, p2], axis=0))
        sub_t = (cum - log_dt).T
        last = cum[Q - 1:Q] if d == 0 else cum[0:1]
        ecum_e = _expand2(jnp.exp(cum), e64_ref[d])
        w_e = _expand2(jnp.where(da_lanes, jnp.exp(last - cum) * dt, 0.0), e64_ref[d])
        colb = _expand2(cum, e128_ref[d])
        decay_e = ecum_e[Q - 1:Q] if d == 0 else ecum_e[0:1]

        gmat = [lax.dot_general(cm[:, g * STATE:(g + 1) * STATE], bm[:, g * STATE:(g + 1) * STATE],
                                (((1,), (1,)), ((), ())), preferred_element_type=F32)
                for g in range(GROUPS)]
        zero_b = jnp.zeros((Q, LANES), BF16)
        y_parts = []
        for p in range(HEADS // 2):
            g = (2 * p) // (HEADS // GROUPS)
            ms = []
            for hh in (2 * p, 2 * p + 1):
                src = DT_DA0 + HEADS * d + hh
                seg = colb[:, hh * LANES:(hh + 1) * LANES] - sub_t[src:src + 1, :]
                ms.append((jnp.where(tri, jnp.exp(seg), 0.0) * gmat[g]).astype(BF16))
            mcat = jnp.concatenate(ms, axis=1)
            xp = xs_b[:, p * LANES:(p + 1) * LANES]
            rhs = jnp.concatenate([jnp.where(lane < HEAD_DIM, xp, zero_b),
                                   jnp.where(lane >= HEAD_DIM, xp, zero_b)], axis=0)
            y_parts.append(_dot(mcat, rhs))
        y_diag = jnp.concatenate(y_parts, axis=1)

        s_old = s_ref[d]
        s_bf = s_old.astype(BF16)
        y_off = jnp.concatenate(
            [_dot(cm[:, g * STATE:(g + 1) * STATE], s_bf[:, g * gw:(g + 1) * gw]) for g in range(GROUPS)],
            axis=1)
        y = y_diag + y_off * ecum_e

        xw = (xs * w_e).astype(BF16)
        upd = jnp.concatenate(
            [lax.dot_general(bm[:, g * STATE:(g + 1) * STATE], xw[:, g * gw:(g + 1) * gw],
                             (((0,), (0,)), ((), ())), preferred_element_type=F32) for g in range(GROUPS)],
            axis=1)
        s_ref[d] = s_old * decay_e + upd

        if first:
            y_ref[rows, :] = y + dsk_ref[...] * xs
        else:
            tot = y_ref[rows, :] + y
            zz = z_ref[0, rows, :].astype(F32)
            gz = tot * _silu(zz)
            outs = []
            for g in range(GROUPS):
                gg = gz[:, g * gw:(g + 1) * gw]
                outs.append(gg * lax.rsqrt(jnp.mean(gg * gg, axis=-1, keepdims=True) + NORM_EPS))
            o_ref[0, rows, :] = (jnp.concatenate(outs, axis=1) * g_ref[...]).astype(BF16)

    s_ref[...] = h0_ref[0]

    def first_half(i, carry):
        chunk(i, 0, True)
        chunk(nck - 1 - i, 1, True)
        return carry

    def second_half(i, carry):
        chunk(i, 0, False)
        chunk(nck - 1 - i, 1, False)
        return carry

    lax.fori_loop(0, nck // 2, first_half, 0)
    lax.fori_loop(nck // 2, nck, second_half, 0)


def _ssd(xbc3, z3, dt3, h0, cw, cb, dsk, g, e64, e128):
    bsz = xbc3.shape[0]
    const = lambda *shape: pl.BlockSpec(shape, lambda b: (0,) * len(shape))
    seq = lambda w: pl.BlockSpec((1, SEQ, w), lambda b: (b, 0, 0))
    return pl.pallas_call(
        _ssd_kernel,
        out_shape=jax.ShapeDtypeStruct((bsz, SEQ, D_SSD), BF16),
        grid=(bsz,),
        in_specs=[seq(XBC_W), seq(D_SSD), seq(LANES),
                  pl.BlockSpec((1, 2, STATE, D_SSD), lambda b: (b, 0, 0, 0)),
                  const(3, XBC_W), const(1, XBC_W),
                  const(1, D_SSD), const(1, D_SSD), const(2, 2 * LANES, D_SSD),
                  const(2, 2 * LANES, HEADS * LANES)],
        out_specs=seq(D_SSD),
        scratch_shapes=[pltpu.VMEM((SEQ, XBC_W), BF16), pltpu.VMEM((SEQ, D_SSD), F32),
                        pltpu.VMEM((2, STATE, D_SSD), F32)],
        compiler_params=pltpu.CompilerParams(dimension_semantics=("arbitrary",),
                                             vmem_limit_bytes=VMEM_LIMIT),
        name="ssd",
    )(xbc3, z3, dt3, h0, cw, cb, dsk, g, e64, e128)


def _outproj_kernel(x_ref, yssd_ref, scb_ref, v_ref, vp_ref, vn_ref, mod_ref, scw_ref, wo1_ref, wo2_ref,
                    g2_ref, wrt_ref, br_ref, x1_ref, lg_ref):
    tm = TOK_TILE
    per_b = SEQ // tm
    i = pl.program_id(0)
    first = (i % per_b) == 0
    last = (i % per_b) == per_b - 1
    m = mod_ref[0]
    v = v_ref[...].astype(F32)
    vp = jnp.where(first, 0.0, vp_ref[...].astype(F32))
    vn = jnp.where(last, 0.0, vn_ref[...].astype(F32))
    dn = jnp.concatenate([vp, v[:tm - GRID_W]], axis=0)
    up = jnp.concatenate([v[GRID_W:], vn], axis=0)
    scw = scw_ref[...]
    ysc = scb_ref[...].astype(F32) * (scw[0:1] * dn + scw[1:2] * v + scw[2:3] * up)
    out = _dot(yssd_ref[...], wo1_ref[...]) + _dot(ysc.astype(BF16), wo2_ref[...])
    x1 = x_ref[...] + m[2:3] * out
    x1_ref[...] = x1
    h2 = _rms(x1) * g2_ref[...] * (1.0 + m[4:5]) + m[3:4]
    h_hi = h2.astype(BF16)
    h_lo = (h2 - h_hi.astype(F32)).astype(BF16)
    lg_ref[...] = lax.dot_general(wrt_ref[...], jnp.concatenate([h_hi, h_lo, h_hi], axis=1),
                                  (((1,), (1,)), ((), ())), preferred_element_type=F32) + br_ref[...]


def _outproj(x2, yssd, scb, v, mod3, scw, wo1, wo2, g2, wrt, br):
    t = x2.shape[0]
    tm = TOK_TILE
    per_b = SEQ // tm
    r = tm // GRID_W
    nrow = t // GRID_W
    const = lambda *shape: pl.BlockSpec(shape, lambda i: (0,) * len(shape))
    tile = lambda w: pl.BlockSpec((tm, w), lambda i: (i, 0))
    return pl.pallas_call(
        _outproj_kernel,
        out_shape=(jax.ShapeDtypeStruct((t, D_MODEL), F32), jax.ShapeDtypeStruct((N_EXPERTS, t), F32)),
        grid=(t // tm,),
        in_specs=[tile(D_MODEL), tile(D_SSD), tile(D_SC), tile(D_SC),
                  pl.BlockSpec((GRID_W, D_SC), lambda i: (jnp.maximum(i * r - 1, 0), 0)),
                  pl.BlockSpec((GRID_W, D_SC), lambda i: (jnp.minimum((i + 1) * r, nrow - 1), 0)),
                  pl.BlockSpec((1, N_MOD, D_MODEL), lambda i: (i // per_b, 0, 0)),
                  const(3, D_SC), const(D_SSD, D_MODEL), const(D_SC, D_MODEL), const(1, D_MODEL),
                  const(N_EXPERTS, 3 * D_MODEL), const(N_EXPERTS, 1)],
        out_specs=(tile(D_MODEL), pl.BlockSpec((N_EXPERTS, tm), lambda i: (0, i))),
        compiler_params=pltpu.CompilerParams(dimension_semantics=("arbitrary",),
                                             vmem_limit_bytes=VMEM_LIMIT),
        name="outproj",
    )(x2, yssd, scb, v, v, v, mod3, scw, wo1, wo2, g2, wrt, br)


def _route_kernel(lg_ref, dest_ref, gate_ref, idx_ref, meta_ref, rank_ref, carry_ref, *, n_tok, n_blocks):
    tt = RT_TILE
    ne = N_EXPERTS
    eio = lax.broadcasted_iota(jnp.int32, (ne, tt), 0)
    si = lax.broadcasted_iota(jnp.int32, (tt, tt), 0)
    ti = lax.broadcasted_iota(jnp.int32, (tt, tt), 1)
    before = (si < ti).astype(BF16)
    carry_ref[...] = jnp.zeros_like(carry_ref)

    def tile_body(j, c):
        t0 = pl.multiple_of(j * tt, tt)
        l = lg_ref[:, pl.ds(t0, tt)]
        onehot = jnp.zeros((ne, tt), F32)
        tops, sels = [], []
        for _ in range(TOP_K):
            mx = jnp.max(l, axis=0, keepdims=True)
            idx = jnp.min(jnp.where(l == mx, eio, ne), axis=0, keepdims=True)
            sel = eio == idx
            l = jnp.where(sel, -jnp.inf, l)
            onehot = onehot + sel.astype(F32)
            tops.append(mx)
            sels.append(sel)
            idx_ref[pl.ds(len(tops) - 1, 1), pl.ds(t0, tt)] = idx
        ex = [jnp.exp(tv - tops[0]) for tv in tops]
        den = ex[0] + ex[1] + ex[2] + ex[3]
        prefix = _dot(onehot.astype(BF16), before) + carry_ref[:, 0:1]
        for k in range(TOP_K):
            gate_ref[pl.ds(k, 1), pl.ds(t0, tt)] = ex[k] / den
            rk = jnp.sum(jnp.where(sels[k], prefix, 0.0), axis=0, keepdims=True)
            rank_ref[pl.ds(k, 1), pl.ds(t0, tt)] = rk.astype(jnp.int32)
        carry_ref[...] = carry_ref[...] + jnp.sum(onehot, axis=1, keepdims=True)
        return c

    lax.fori_loop(0, n_tok // tt, tile_body, 0)

    counts = carry_ref[...]
    padded = jnp.floor((counts + (MOE_BM - 1)) * (1.0 / MOE_BM)) * MOE_BM
    er = lax.broadcasted_iota(jnp.int32, (ne, ne), 0)
    ec = lax.broadcasted_iota(jnp.int32, (ne, ne), 1)
    pad_start = jnp.dot((ec < er).astype(F32), padded, precision=HIGHEST, preferred_element_type=F32)
    pad_end = pad_start + padded

    def dest_body(j, c):
        t0 = pl.multiple_of(j * tt, tt)
        for k in range(TOP_K):
            idx = idx_ref[pl.ds(k, 1), pl.ds(t0, tt)]
            base = jnp.sum(jnp.where(eio == idx, pad_start[:, 0:1], 0.0), axis=0, keepdims=True)
            dest_ref[pl.ds(k, 1), pl.ds(t0, tt)] = base.astype(jnp.int32) + rank_ref[pl.ds(k, 1), pl.ds(t0, tt)]
        return c

    lax.fori_loop(0, n_tok // tt, dest_body, 0)

    width = meta_ref.shape[1]
    sub = lax.broadcasted_iota(jnp.int32, (ne, width), 0)
    lan = lax.broadcasted_iota(jnp.int32, (ne, width), 1)
    diag = sub == lan
    cnt_row = jnp.sum(jnp.where(diag, counts[:, 0:1], 0.0), axis=0, keepdims=True)
    start_row = jnp.sum(jnp.where(diag, pad_start[:, 0:1], 0.0), axis=0, keepdims=True)
    blk_start = (lan * MOE_BM).astype(F32)
    blk_exp = jnp.sum((pad_end[:, 0:1] <= blk_start).astype(F32), axis=0, keepdims=True)
    blk_exp = jnp.minimum(blk_exp, float(ne - 1))
    used = jnp.sum(padded[:, 0:1], axis=0, keepdims=True) * (1.0 / MOE_BM)
    meta_ref[0:1, :] = cnt_row.astype(jnp.int32)
    meta_ref[1:2, :] = start_row.astype(jnp.int32)
    meta_ref[2:3, :] = blk_exp.astype(jnp.int32)
    meta_ref[3:4, :] = jnp.broadcast_to(used, (1, width)).astype(jnp.int32)
    meta_ref[4:8, :] = jnp.zeros((4, width), jnp.int32)


def _route(lgt, n_blocks):
    ne, n_tok = lgt.shape
    width = -(-n_blocks // LANES) * LANES
    full = lambda *shape: pl.BlockSpec(shape, lambda: (0,) * len(shape))
    return pl.pallas_call(
        functools.partial(_route_kernel, n_tok=n_tok, n_blocks=n_blocks),
        out_shape=(jax.ShapeDtypeStruct((TOP_K, n_tok), jnp.int32),
                   jax.ShapeDtypeStruct((TOP_K, n_tok), F32),
                   jax.ShapeDtypeStruct((TOP_K, n_tok), jnp.int32),
                   jax.ShapeDtypeStruct((8, width), jnp.int32)),
        in_specs=[full(ne, n_tok)],
        out_specs=(full(TOP_K, n_tok), full(TOP_K, n_tok), full(TOP_K, n_tok), full(8, width)),
        scratch_shapes=[pltpu.VMEM((TOP_K, n_tok), jnp.int32), pltpu.VMEM((ne, LANES), F32)],
        compiler_params=pltpu.CompilerParams(vmem_limit_bytes=VMEM_LIMIT),
        name="route",
    )(lgt)


def _dispatch_kernel(dest_ref, cnt_ref, start_ref, nu_ref, x1_ref, meta_ref, mod_ref, g2_ref, zsrc_ref, xs_ref,
                     hbuf, sem, zsem):
    i = pl.program_id(0)
    n = pl.num_programs(0)
    tl = DISP_TILE
    slot = i % 2
    nb = xs_ref.shape[0] // MOE_BM

    def zero_block(b):
        return pltpu.make_async_copy(zsrc_ref, xs_ref.at[pl.ds(b * MOE_BM, MOE_BM)], zsem)

    @pl.when(i == 0)
    def _():
        def start_e(e, c):
            @pl.when(cnt_ref[e] > 0)
            def _():
                zero_block((start_ref[e] + cnt_ref[e] - 1) // MOE_BM).start()
            return c

        def wait_e(e, c):
            @pl.when(cnt_ref[e] > 0)
            def _():
                zero_block(0).wait()
            return c

        def start_t(b, c):
            zero_block(b).start()
            return c

        def wait_t(b, c):
            zero_block(0).wait()
            return c

        lax.fori_loop(0, N_EXPERTS, start_e, 0)
        lax.fori_loop(nu_ref[0], nb, start_t, 0)
        lax.fori_loop(0, N_EXPERTS, wait_e, 0)
        lax.fori_loop(nu_ref[0], nb, wait_t, 0)

    m = mod_ref[0]
    h2 = _rms(x1_ref[...]) * g2_ref[...] * (1.0 + m[4:5]) + m[3:4]
    lo = pltpu.bitcast(h2[:, :PACK_W].astype(BF16).astype(F32), jnp.uint32) >> 16
    hi = pltpu.bitcast(h2[:, PACK_W:].astype(BF16).astype(F32), jnp.uint32) & jnp.uint32(0xFFFF0000)
    row = jnp.concatenate([lo | hi, meta_ref[...], jnp.zeros((tl, D_MODEL - PACK_W - LANES), jnp.uint32)], axis=1)
    hbuf[slot] = row.reshape(tl, SUBLANES, LANES)

    def per_tok(t, c):
        tok = i * tl + t
        for k in range(TOP_K):
            d = dest_ref[tok * TOP_K + k]
            pltpu.make_async_copy(hbuf.at[slot, t], xs_ref.at[d], sem.at[slot]).start(priority=k % 2)
        return c

    lax.fori_loop(0, tl, per_tok, 0, unroll=8)

    def wait_slot(sl):
        for _ in range(TOP_K):
            pltpu.make_async_copy(hbuf.at[sl], xs_ref.at[pl.ds(0, tl)], sem.at[sl]).wait()

    @pl.when(i > 0)
    def _():
        wait_slot(1 - slot)

    @pl.when(i == n - 1)
    def _():
        wait_slot(slot)


def _dispatch(dest_flat, cnt, start, n_used, x1, meta_rows, mod3, g2, zsrc, n_rows):
    n_tok = x1.shape[0]
    tl = DISP_TILE
    per_b = SEQ // tl
    return pl.pallas_call(
        _dispatch_kernel,
        out_shape=jax.ShapeDtypeStruct((n_rows, SUBLANES, LANES), jnp.uint32),
        grid_spec=pltpu.PrefetchScalarGridSpec(
            num_scalar_prefetch=4,
            grid=(n_tok // tl,),
            in_specs=[pl.BlockSpec((tl, D_MODEL), lambda i, *_: (i, 0)),
                      pl.BlockSpec((tl, LANES), lambda i, *_: (i, 0)),
                      pl.BlockSpec((1, N_MOD, D_MODEL), lambda i, *_: (i // per_b, 0, 0)),
                      pl.BlockSpec((1, D_MODEL), lambda i, *_: (0, 0)),
                      pl.BlockSpec((MOE_BM, SUBLANES, LANES), lambda i, *_: (0, 0, 0))],
            out_specs=pl.BlockSpec(memory_space=pl.ANY),
            scratch_shapes=[pltpu.VMEM((2, tl, SUBLANES, LANES), jnp.uint32),
                            pltpu.SemaphoreType.DMA((2,)), pltpu.SemaphoreType.DMA]),
        compiler_params=pltpu.CompilerParams(dimension_semantics=("arbitrary",),
                                             vmem_limit_bytes=VMEM_LIMIT),
        name="dispatch",
    )(dest_flat, cnt, start, n_used, x1, meta_rows, mod3, g2, zsrc)


def _expert_rows_kernel(be_ref, nu_ref, xs_ref, wgu_hbm, bgu_ref, wd_hbm, bd_ref, ys_ref,
                        wgu_raw, wd_raw, wgu_bf, wd_bf, w_sem):
    j = pl.program_id(0)
    nu = nu_ref[0]
    n_lt = D_MODEL // LANES

    def weight_copies(e):
        return (pltpu.make_async_copy(wgu_hbm.at[e], wgu_raw, w_sem.at[0]),
                pltpu.make_async_copy(wd_hbm.at[e], wd_raw, w_sem.at[1]))

    @pl.when(j == 0)
    def _():
        for w in weight_copies(be_ref[0]):
            w.start()

    e_now = be_ref[j]
    new_expert = jnp.logical_or(j == 0, be_ref[jnp.maximum(j - 1, 0)] != e_now)

    @pl.when(jnp.logical_and(j < nu, new_expert))
    def _():
        for w in weight_copies(e_now):
            w.wait()
        n_cc = 8
        for c in range(n_cc):
            cc = slice(c * (2 * D_FF // n_cc), (c + 1) * (2 * D_FF // n_cc))
            wgu_bf[:, cc] = wgu_raw[:, cc].astype(BF16)
        for c in range(n_cc // 2):
            cc = slice(c * (2 * D_MODEL // n_cc), (c + 1) * (2 * D_MODEL // n_cc))
            wd_bf[:, cc] = wd_raw[:, cc].astype(BF16)
        j_next = lax.while_loop(lambda t: jnp.logical_and(t < nu, be_ref[jnp.minimum(t, pl.num_programs(0) - 1)] == e_now),
                                lambda t: t + 1, j + 1)

        @pl.when(j_next < nu)
        def _():
            for w in weight_copies(be_ref[jnp.minimum(j_next, pl.num_programs(0) - 1)]):
                w.start()

    @pl.when(j < nu)
    def _():
        words = xs_ref[...].reshape(MOE_BM, D_MODEL)
        packed = words[:, 0:PACK_W]
        meta = pltpu.bitcast(words[:, PACK_W:PACK_W + LANES], F32)
        xb = jnp.concatenate(
            [pltpu.bitcast(packed << 16, F32).astype(BF16),
             pltpu.bitcast(packed & jnp.uint32(0xFFFF0000), F32).astype(BF16)], axis=1)
        e_f = e_now.astype(F32)
        gate = jnp.zeros((MOE_BM, 1), F32)
        for k in range(TOP_K):
            mk = meta[:, META_IDX + k:META_IDX + k + 1] == e_f
            gate = gate + jnp.where(mk, meta[:, META_GATE + k:META_GATE + k + 1], 0.0)
        gu = _dot(xb, wgu_bf[...]) + bgu_ref[0]
        glu = jnp.minimum(gu[:, :D_FF], SWIGLU_LIMIT)
        lin = jnp.clip(gu[:, D_FF:], -SWIGLU_LIMIT, SWIGLU_LIMIT)
        act = glu * jax.nn.sigmoid(SWIGLU_ALPHA * glu) * (lin + 1.0)
        y = (_dot(act.astype(BF16), wd_bf[...]) + bd_ref[0]) * gate
        ys_ref[...] = y.reshape(MOE_BM, n_lt, LANES)

    @pl.when(j >= nu)
    def _():
        ys_ref[...] = jnp.zeros_like(ys_ref)


def _expert_rows(blk_exp, n_used, xs, wgu, bgu, wd, bd):
    n_rows = xs.shape[0]
    nb = n_rows // MOE_BM
    n_lt = D_MODEL // LANES
    row_blk = lambda j, be, nu: (jnp.minimum(j, nu[0] - 1), 0, 0)
    per_e = lambda j, be, nu: (be[j], 0, 0)
    return pl.pallas_call(
        _expert_rows_kernel,
        out_shape=jax.ShapeDtypeStruct((n_rows, n_lt, LANES), F32),
        grid_spec=pltpu.PrefetchScalarGridSpec(
            num_scalar_prefetch=2,
            grid=(nb,),
            in_specs=[pl.BlockSpec((MOE_BM, SUBLANES, LANES), row_blk),
                      pl.BlockSpec(memory_space=pl.ANY),
                      pl.BlockSpec((1, 1, 2 * D_FF), per_e),
                      pl.BlockSpec(memory_space=pl.ANY),
                      pl.BlockSpec((1, 1, D_MODEL), per_e)],
            out_specs=pl.BlockSpec((MOE_BM, n_lt, LANES), lambda j, be, nu: (j, 0, 0)),
            scratch_shapes=[pltpu.VMEM((D_MODEL, 2 * D_FF), F32), pltpu.VMEM((D_FF, D_MODEL), F32),
                            pltpu.VMEM((D_MODEL, 2 * D_FF), BF16), pltpu.VMEM((D_FF, D_MODEL), BF16),
                            pltpu.SemaphoreType.DMA((2,))]),
        compiler_params=pltpu.CompilerParams(dimension_semantics=("arbitrary",),
                                             vmem_limit_bytes=VMEM_LIMIT),
        name="experts",
    )(blk_exp, n_used, xs, wgu, bgu, wd, bd)


def _gather_combine_kernel(dest_ref, ys_ref, x1_ref, mod_ref, fg_ref, o_ref, buf_a, buf_b, sem):
    i = pl.program_id(0)
    n = pl.num_programs(0)
    tc = GATHER_TILE
    bufs = (buf_a, buf_b)
    n_part = 8
    pr = tc // n_part

    def issue(tile, sl, t0, t1):
        base = tile * (tc * TOP_K)
        for t in range(t0, t1):
            for k in range(TOP_K):
                pltpu.make_async_copy(ys_ref.at[dest_ref[base + t * TOP_K + k]], bufs[sl].at[k * tc + t],
                                      sem.at[sl]).start(priority=k % 2)

    def reduce_rows(sl, r0, nr):
        rows = lambda k: bufs[sl][pl.ds(k * tc + r0, nr)]
        moe = ((rows(0) + rows(1)) + (rows(2) + rows(3))).reshape(nr, D_MODEL)
        x2 = x1_ref[pl.ds(r0, nr), :] + mod_ref[0][5:6] * moe
        o_ref[pl.ds(r0, nr), :] = _rms(x2) * fg_ref[...]

    @pl.when(i == 0)
    def _():
        def per_tok(t, c):
            for k in range(TOP_K):
                pltpu.make_async_copy(ys_ref.at[dest_ref[t * TOP_K + k]], buf_a.at[k * tc + t],
                                      sem.at[0]).start(priority=k % 2)
            return c

        lax.fori_loop(0, tc, per_tok, 0, unroll=8)

    def step(sl):
        for _ in range(TOP_K):
            pltpu.make_async_copy(ys_ref.at[pl.ds(0, tc)], bufs[sl].at[pl.ds(0, tc)], sem.at[sl]).wait()

        @pl.when(i + 1 < n)
        def _():
            for p in range(n_part):
                issue(i + 1, 1 - sl, p * pr, (p + 1) * pr)
                reduce_rows(sl, p * pr, pr)

        @pl.when(i + 1 == n)
        def _():
            for p in range(n_part):
                reduce_rows(sl, p * pr, pr)

    for sl in range(2):
        pl.when(i % 2 == sl)(functools.partial(step, sl))


def _gather_combine(dest_flat, ys, x1, mod3, fg):
    n_tok = x1.shape[0]
    tc = GATHER_TILE
    per_b = SEQ // tc
    n_lt = D_MODEL // LANES
    return pl.pallas_call(
        _gather_combine_kernel,
        out_shape=jax.ShapeDtypeStruct((n_tok, D_MODEL), F32),
        grid_spec=pltpu.PrefetchScalarGridSpec(
            num_scalar_prefetch=1,
            grid=(n_tok // tc,),
            in_specs=[pl.BlockSpec(memory_space=pl.ANY),
                      pl.BlockSpec((tc, D_MODEL), lambda i, d: (i, 0)),
                      pl.BlockSpec((1, N_MOD, D_MODEL), lambda i, d: (i // per_b, 0, 0)),
                      pl.BlockSpec((1, D_MODEL), lambda i, d: (0, 0))],
            out_specs=pl.BlockSpec((tc, D_MODEL), lambda i, d: (i, 0)),
            scratch_shapes=[pltpu.VMEM((TOP_K * tc, n_lt, LANES), F32), pltpu.VMEM((TOP_K * tc, n_lt, LANES), F32),
                            pltpu.SemaphoreType.DMA((2,))]),
        compiler_params=pltpu.CompilerParams(dimension_semantics=("arbitrary",),
                                             vmem_limit_bytes=VMEM_LIMIT),
        name="combine",
    )(dest_flat, ys, x1, mod3, fg)


def _expansion_matrices(src0):
    r = (jnp.arange(2 * LANES) % LANES)[:, None]
    out64, out128 = [], []
    for d in range(2):
        l64 = jnp.arange(D_SSD)[None, :]
        l128 = jnp.arange(HEADS * LANES)[None, :]
        out64.append((l64 // HEAD_DIM == r - src0 - HEADS * d).astype(BF16))
        out128.append((l128 // LANES == r - src0 - HEADS * d).astype(BF16))
    return jnp.stack(out64), jnp.stack(out128)


def _pad_lanes(v):
    return jnp.pad(v, [(0, 0)] * (v.ndim - 1) + [(0, LANES - v.shape[-1])])


def kernel(x, c, ctx, c_ctx, w_mod, b_mod, norm1_g, w_in, ssd_conv_w, ssd_conv_b, ssd_dt_bias, ssd_a_log,
           ssd_d, ssd_norm_g, sc_conv_w, w_out, norm2_g, w_router, b_router, w_gate_up, b_gate_up, w_down,
           b_down, final_g):
    bsz = x.shape[0]
    n_tok = bsz * SEQ
    n_assign = n_tok * TOP_K
    n_blocks = n_assign // MOE_BM + N_EXPERTS
    n_rows = n_blocks * MOE_BM
    li = 0

    cvec = jnp.concatenate([c, c_ctx[None, :], jnp.zeros((7, D_MODEL), F32)], axis=0)
    mod3 = _mod(cvec, w_mod[li], b_mod[li][None, :]).reshape(bsz + 8, N_MOD, D_MODEL)

    w = w_in[li]
    wz = w[:, Z0:X0].astype(BF16)
    wxbc = w[:, X0:DT0].astype(BF16)
    wdt = _pad_lanes(w[:, DT0:SC0]).astype(BF16)
    wb = w[:, SC0:SC0 + D_SC].astype(BF16)
    wc = w[:, SC0 + D_SC:SC0 + 2 * D_SC].astype(BF16)
    wu = w[:, SC0 + 2 * D_SC:].astype(BF16)
    g1 = norm1_g[li][None, :]
    cw = ssd_conv_w[li]
    cb = ssd_conv_b[li][None, :]
    dtb = _pad_lanes(ssd_dt_bias[li].reshape(1, 2 * HEADS))
    alog = _pad_lanes(ssd_a_log[li].reshape(1, 2 * HEADS))
    e64_ctx, _ = _expansion_matrices(0)

    h0 = _ctx_states(ctx, mod3, g1, wxbc[:, :XB_W], wdt, cw[:, :XB_W], cb[:, :XB_W], dtb, alog, e64_ctx)

    rep = lambda a: _pad_lanes(jnp.tile(a[..., :2 * HEADS], (1, DT_COPIES)))
    e64, e128 = _expansion_matrices(DT_DA0)
    x2 = x.reshape(n_tok, D_MODEL)
    z, xbc, dtp, scb, v = _inproj(x2, mod3, g1, wz, wxbc, rep(wdt), rep(dtb), rep(alog), wb, wc, wu)

    dsk = jnp.repeat(ssd_d[li], HEAD_DIM)[None, :]
    yssd = _ssd(xbc.reshape(bsz, SEQ, XBC_W), z.reshape(bsz, SEQ, D_SSD), dtp.reshape(bsz, SEQ, LANES), h0,
                cw, cb, dsk, ssd_norm_g[li][None, :], e64, e128)

    wo = w_out[li].astype(BF16)
    g2 = norm2_g[li][None, :]
    wr = w_router[li].T
    wr_hi = wr.astype(BF16)
    wr_lo = (wr - wr_hi.astype(F32)).astype(BF16)
    x1, lgt = _outproj(x2, yssd.reshape(n_tok, D_SSD), scb, v, mod3, sc_conv_w[li], wo[:D_SSD], wo[D_SSD:],
                       g2, jnp.concatenate([wr_hi, wr_hi, wr_lo], axis=1), b_router[li][:, None])

    dest_t, gate_t, idx_t, meta = _route(lgt, n_blocks)
    dest_flat = dest_t.T.reshape(n_assign)
    cnt = meta[0, :N_EXPERTS]
    start = meta[1, :N_EXPERTS]
    blk_exp = meta[2, :n_blocks]
    n_used = meta[3, :1]

    meta_rows = lax.bitcast_convert_type(_pad_lanes(jnp.concatenate(
        [idx_t.T.astype(F32), gate_t.T], axis=1)), jnp.uint32)
    pad_meta = lax.bitcast_convert_type(_pad_lanes(jnp.concatenate(
        [jnp.full((MOE_BM, TOP_K), -1.0, F32), jnp.zeros((MOE_BM, TOP_K), F32)], axis=1)), jnp.uint32)
    zsrc = jnp.concatenate([jnp.zeros((MOE_BM, PACK_W), jnp.uint32), pad_meta,
                            jnp.zeros((MOE_BM, D_MODEL - PACK_W - LANES), jnp.uint32)],
                           axis=1).reshape(MOE_BM, SUBLANES, LANES)

    xs = _dispatch(dest_flat, cnt, start, n_used, x1, meta_rows, mod3, g2, zsrc, n_rows)
    ys = _expert_rows(blk_exp, n_used, xs, w_gate_up[li], b_gate_up[li][:, None, :],
                      w_down[li], b_down[li][:, None, :])
    out = _gather_combine(dest_flat, ys, x1, mod3, final_g[None, :])
    return out.reshape(bsz, SEQ, D_MODEL)
```

```python
import functools

import jax
import jax.numpy as jnp
from jax import lax
from jax.experimental import pallas as pl
from jax.experimental.pallas import tpu as pltpu

F32 = jnp.float32
BF16 = jnp.bfloat16
HIGHEST = lax.Precision.HIGHEST

D_MODEL = 1024
SEQ = 2048
CTX_LEN = 256
GRID_W = 64
D_SSD = 1024
D_SC = 1024
HEAD_DIM = 64
HEADS = 16
GROUPS = 2
STATE = 128
CHUNK = 128
N_EXPERTS = 32
TOP_K = 4
D_FF = 1024
SWIGLU_LIMIT = 7.0
SWIGLU_ALPHA = 1.702
NORM_EPS = 1e-6
N_MOD = 6
XBC_W = D_SSD + 2 * GROUPS * STATE
XB_W = D_SSD + GROUPS * STATE
LANES = 128

Z0 = 0
X0 = Z0 + D_SSD
B0 = X0 + D_SSD
C0 = B0 + GROUPS * STATE
DT0 = C0 + GROUPS * STATE
SC0 = DT0 + 2 * HEADS

TOK_TILE = 512
MOE_BM = 512
MOE_PARTS = 2
RT_TILE = 512
DISP_TILE = 256
GATHER_TILE = 256
DT_COPIES = 3
DT_LOG0 = 2 * HEADS
DT_DA0 = 4 * HEADS
SUBLANES = 8
PACK_W = D_MODEL // 2
META_IDX = 0
META_GATE = TOP_K
VMEM_LIMIT = 56 * 1024 * 1024


def _silu(v):
    return v * jax.nn.sigmoid(v)


def _softplus(v):
    return jnp.maximum(v, 0.0) + jnp.log1p(jnp.exp(-jnp.abs(v)))


def _rms(v):
    return v * lax.rsqrt(jnp.mean(v * v, axis=-1, keepdims=True) + NORM_EPS)


def _dot(a, b):
    return jnp.dot(a, b, preferred_element_type=F32)


def _expand2(v, e2):
    hi = v.astype(BF16)
    lo = (v - hi.astype(F32)).astype(BF16)
    return _dot(jnp.concatenate([hi, lo], axis=1), e2)


def _mod_kernel(c_ref, w_ref, b_ref, o_ref):
    o_ref[...] = jnp.dot(_silu(c_ref[...]), w_ref[...], precision=HIGHEST,
                         preferred_element_type=F32) + b_ref[...]


def _mod(cvec, w_mod, b_mod):
    rows = cvec.shape[0]
    n = w_mod.shape[1]
    tn = 1536
    return pl.pallas_call(
        _mod_kernel,
        out_shape=jax.ShapeDtypeStruct((rows, n), F32),
        grid=(n // tn,),
        in_specs=[pl.BlockSpec((rows, D_MODEL), lambda j: (0, 0)),
                  pl.BlockSpec((D_MODEL, tn), lambda j: (0, j)),
                  pl.BlockSpec((1, tn), lambda j: (0, j))],
        out_specs=pl.BlockSpec((rows, tn), lambda j: (0, j)),
        compiler_params=pltpu.CompilerParams(dimension_semantics=("arbitrary",),
                                             vmem_limit_bytes=VMEM_LIMIT),
        name="mod",
    )(cvec, w_mod, b_mod)


def _ctx_kernel(ctx_ref, mod_ref, g1_ref, wxb_ref, wdt_ref, cw_ref, cb_ref, dtb_ref, alog_ref, e64_ref,
                h0_ref):
    L = CTX_LEN
    m = mod_ref[0]
    hc = _rms(ctx_ref[0]) * g1_ref[...] * (1.0 + m[1:2]) + m[0:1]
    hb = hc.astype(BF16)
    pxb = _dot(hb, wxb_ref[...])
    dtr = _dot(hb, wdt_ref[...])
    rowi = lax.broadcasted_iota(jnp.int32, (L, XB_W), 0)
    dn = jnp.where(rowi == 0, 0.0, pltpu.roll(pxb, 1, 0))
    up = jnp.where(rowi == L - 1, 0.0, pltpu.roll(pxb, L - 1, 0))
    cw = cw_ref[...]
    xb = _silu(cw[0:1] * dn + cw[1:2] * pxb + cw[2:3] * up + cb_ref[...])
    xs = xb[:, :D_SSD]
    bm = xb[:, D_SSD:].astype(BF16)
    dt = _softplus(dtr + dtb_ref[...])
    da = dt * (-jnp.exp(alog_ref[...]))
    ri = lax.broadcasted_iota(jnp.int32, (L, L), 0)
    ci = lax.broadcasted_iota(jnp.int32, (L, L), 1)
    for d in range(2):
        tri = (ci <= ri) if d == 0 else (ci >= ri)
        cum = jnp.dot(tri.astype(F32), da, precision=HIGHEST, preferred_element_type=F32)
        last = cum[L - 1:L] if d == 0 else cum[0:1]
        w_e = _expand2(jnp.exp(last - cum) * dt, e64_ref[d])
        xw = (xs * w_e).astype(BF16)
        for g in range(GROUPS):
            gw = D_SSD // GROUPS
            st = lax.dot_general(bm[:, g * STATE:(g + 1) * STATE], xw[:, g * gw:(g + 1) * gw],
                                 (((0,), (0,)), ((), ())), preferred_element_type=F32)
            h0_ref[0, d, :, g * gw:(g + 1) * gw] = st


def _ctx_states(ctx, mod3, g1, wxb, wdt, cw, cb, dtb, alog, e64):
    bsz = ctx.shape[0]
    mod_row = bsz
    const = lambda *shape: pl.BlockSpec(shape, lambda b: (0,) * len(shape))
    return pl.pallas_call(
        _ctx_kernel,
        out_shape=jax.ShapeDtypeStruct((bsz, 2, STATE, D_SSD), F32),
        grid=(bsz,),
        in_specs=[pl.BlockSpec((1, CTX_LEN, D_MODEL), lambda b: (b, 0, 0)),
                  pl.BlockSpec((1, N_MOD, D_MODEL), lambda b: (mod_row, 0, 0)),
                  const(1, D_MODEL), const(D_MODEL, XB_W), const(D_MODEL, LANES),
                  const(3, XB_W), const(1, XB_W), const(1, LANES), const(1, LANES),
                  const(2, 2 * LANES, D_SSD)],
        out_specs=pl.BlockSpec((1, 2, STATE, D_SSD), lambda b: (b, 0, 0, 0)),
        compiler_params=pltpu.CompilerParams(dimension_semantics=("arbitrary",),
                                             vmem_limit_bytes=VMEM_LIMIT),
        name="ctx_states",
    )(ctx, mod3, g1, wxb, wdt, cw, cb, dtb, alog, e64)


def _inproj_kernel(x_ref, mod_ref, g1_ref, wz_ref, wxbc_ref, wdt_ref, dtb_ref, alog_ref, wb_ref, wc_ref, wu_ref,
                   z_ref, xbc_ref, dt_ref, scb_ref, v_ref):
    m = mod_ref[0]
    hx = _rms(x_ref[...]) * g1_ref[...] * (1.0 + m[1:2]) + m[0:1]
    hb = hx.astype(BF16)
    z_ref[...] = _dot(hb, wz_ref[...]).astype(BF16)
    xbc_ref[...] = _dot(hb, wxbc_ref[...]).astype(BF16)
    dt = _softplus(_dot(hb, wdt_ref[...]) + dtb_ref[...])
    lane = lax.broadcasted_iota(jnp.int32, dt.shape, 1)
    dt_ref[...] = jnp.where(lane < DT_LOG0, dt,
                            jnp.where(lane < DT_DA0, jnp.log(dt), dt * (-jnp.exp(alog_ref[...]))))
    scb_ref[...] = _dot(hb, wb_ref[...]).astype(BF16)
    v_ref[...] = (_dot(hb, wc_ref[...]) * _dot(hb, wu_ref[...])).astype(BF16)


def _inproj(x2, mod3, g1, wz, wxbc, wdt, dtb, alog, wb, wc, wu):
    t = x2.shape[0]
    tm = TOK_TILE
    per_b = SEQ // tm
    const = lambda *shape: pl.BlockSpec(shape, lambda i: (0,) * len(shape))
    tile = lambda w: pl.BlockSpec((tm, w), lambda i: (i, 0))
    return pl.pallas_call(
        _inproj_kernel,
        out_shape=(jax.ShapeDtypeStruct((t, D_SSD), BF16), jax.ShapeDtypeStruct((t, XBC_W), BF16),
                   jax.ShapeDtypeStruct((t, LANES), F32), jax.ShapeDtypeStruct((t, D_SC), BF16),
                   jax.ShapeDtypeStruct((t, D_SC), BF16)),
        grid=(t // tm,),
        in_specs=[tile(D_MODEL),
                  pl.BlockSpec((1, N_MOD, D_MODEL), lambda i: (i // per_b, 0, 0)),
                  const(1, D_MODEL), const(D_MODEL, D_SSD), const(D_MODEL, XBC_W), const(D_MODEL, LANES),
                  const(1, LANES), const(1, LANES),
                  const(D_MODEL, D_SC), const(D_MODEL, D_SC), const(D_MODEL, D_SC)],
        out_specs=(tile(D_SSD), tile(XBC_W), tile(LANES), tile(D_SC), tile(D_SC)),
        compiler_params=pltpu.CompilerParams(dimension_semantics=("arbitrary",),
                                             vmem_limit_bytes=VMEM_LIMIT),
        name="inproj",
    )(x2, mod3, g1, wz, wxbc, wdt, dtb, alog, wb, wc, wu)


def _ssd_kernel(xbc_ref, z_ref, dt_ref, h0_ref, cw_ref, cb_ref, dsk_ref, g_ref,
                e64_ref, e128_ref, o_ref, xc_ref, y_ref, s_ref):
    Q = CHUNK
    nck = SEQ // Q
    gw = D_SSD // GROUPS

    rowi = lax.broadcasted_iota(jnp.int32, (SUBLANES, XBC_W), 0)

    def conv_body(c, carry):
        r0 = pl.multiple_of(c * Q, Q)
        main = xbc_ref[0, pl.ds(r0, Q), :].astype(F32)
        pstart = pl.multiple_of(jnp.maximum(r0 - 16, 0), 16)
        nstart = pl.multiple_of(jnp.minimum(r0 + Q, SEQ - 16), 16)
        prev = xbc_ref[0, pl.ds(pstart, 16), :].astype(F32)[15:16]
        nxt = xbc_ref[0, pl.ds(nstart, 16), :].astype(F32)[0:1]
        prev = jnp.where(c > 0, prev, 0.0)
        nxt = jnp.where(c < nck - 1, nxt, 0.0)
        dn = pltpu.roll(main, 1, 0)
        up = pltpu.roll(main, Q - 1, 0)
        dn = jnp.concatenate([jnp.where(rowi == 0, prev, dn[0:SUBLANES]), dn[SUBLANES:]], axis=0)
        up = jnp.concatenate([up[:Q - SUBLANES], jnp.where(rowi == SUBLANES - 1, nxt, up[Q - SUBLANES:])], axis=0)
        cw = cw_ref[...]
        conv = cw[0:1] * dn + cw[1:2] * main + cw[2:3] * up + cb_ref[...]
        xc_ref[pl.ds(r0, Q), :] = _silu(conv).astype(BF16)
        return carry

    lax.fori_loop(0, nck, conv_body, 0)

    ri = lax.broadcasted_iota(jnp.int32, (Q, Q), 0)
    ci = lax.broadcasted_iota(jnp.int32, (Q, Q), 1)
    lane = lax.broadcasted_iota(jnp.int32, (Q, LANES), 1)
    da_lanes = jnp.logical_and(lane >= DT_DA0, lane < DT_DA0 + 2 * HEADS)

    def chunk(c, d, first):
        r0 = pl.multiple_of(c * Q, Q)
        rows = pl.ds(r0, Q)
        xs_b = xc_ref[rows, 0:D_SSD]
        xs = xs_b.astype(F32)
        bm = xc_ref[rows, D_SSD:D_SSD + GROUPS * STATE]
        cm = xc_ref[rows, D_SSD + GROUPS * STATE:XBC_W]
        dtp = dt_ref[0, rows, :]
        da = jnp.where(da_lanes, dtp, 0.0)
        dt = pltpu.roll(dtp, DT_DA0, 1)
        log_dt = pltpu.roll(dtp, DT_DA0 - DT_LOG0, 1)
        tri = (ci <= ri) if d == 0 else (ci >= ri)
        p0 = da.astype(BF16)
        r1 = da - p0.astype(F32)
        p1 = r1.astype(BF16)
        p2 = (r1 - p1.astype(F32)).astype(BF16)
        tri_b = jnp.where(tri, 1.0, 0.0).astype(BF16)
        cum = _dot(jnp.concatenate([tri_b, tri_b, tri_b], axis=1),
                   jnp.concatenate([p0, p1, p2], axis=0))
        sub_t = (cum - log_dt).T
        last = cum[Q - 1:Q] if d == 0 else cum[0:1]
        ecum_e = _expand2(jnp.exp(cum), e64_ref[d])
        w_e = _expand2(jnp.where(da_lanes, jnp.exp(last - cum) * dt, 0.0), e64_ref[d])
        colb = _expand2(cum, e128_ref[d])
        decay_e = ecum_e[Q - 1:Q] if d == 0 else ecum_e[0:1]

        gmat = [lax.dot_general(cm[:, g * STATE:(g + 1) * STATE], bm[:, g * STATE:(g + 1) * STATE],
                                (((1,), (1,)), ((), ())), preferred_element_type=F32)
                for g in range(GROUPS)]
        zero_b = jnp.zeros((Q, LANES), BF16)
        y_parts = []
        for p in range(HEADS // 2):
            g = (2 * p) // (HEADS // GROUPS)
            ms = []
            for hh in (2 * p, 2 * p + 1):
                src = DT_DA0 + HEADS * d + hh
                seg = colb[:, hh * LANES:(hh + 1) * LANES] - sub_t[src:src + 1, :]
                ms.append((jnp.where(tri, jnp.exp(seg), 0.0) * gmat[g]).astype(BF16))
            mcat = jnp.concatenate(ms, axis=1)
            xp = xs_b[:, p * LANES:(p + 1) * LANES]
            rhs = jnp.concatenate([jnp.where(lane < HEAD_DIM, xp, zero_b),
                                   jnp.where(lane >= HEAD_DIM, xp, zero_b)], axis=0)
            y_parts.append(_dot(mcat, rhs))
        y_diag = jnp.concatenate(y_parts, axis=1)

        s_old = s_ref[d]
        s_bf = s_old.astype(BF16)
        y_off = jnp.concatenate(
            [_dot(cm[:, g * STATE:(g + 1) * STATE], s_bf[:, g * gw:(g + 1) * gw]) for g in range(GROUPS)],
            axis=1)
        y = y_diag + y_off * ecum_e

        xw = (xs * w_e).astype(BF16)
        upd = jnp.concatenate(
            [lax.dot_general(bm[:, g * STATE:(g + 1) * STATE], xw[:, g * gw:(g + 1) * gw],
                             (((0,), (0,)), ((), ())), preferred_element_type=F32) for g in range(GROUPS)],
            axis=1)
        s_ref[d] = s_old * decay_e + upd

        if first:
            y_ref[rows, :] = y + dsk_ref[...] * xs
        else:
            tot = y_ref[rows, :] + y
            zz = z_ref[0, rows, :].astype(F32)
            gz = tot * _silu(zz)
            outs = []
            for g in range(GROUPS):
                gg = gz[:, g * gw:(g + 1) * gw]
                outs.append(gg * lax.rsqrt(jnp.mean(gg * gg, axis=-1, keepdims=True) + NORM_EPS))
            o_ref[0, rows, :] = (jnp.concatenate(outs, axis=1) * g_ref[...]).astype(BF16)

    s_ref[...] = h0_ref[0]

    def first_half(i, carry):
        chunk(i, 0, True)
        chunk(nck - 1 - i, 1, True)
        return carry

    def second_half(i, carry):
        chunk(i, 0, False)
        chunk(nck - 1 - i, 1, False)
        return carry

    lax.fori_loop(0, nck // 2, first_half, 0)
    lax.fori_loop(nck // 2, nck, second_half, 0)


def _ssd(xbc3, z3, dt3, h0, cw, cb, dsk, g, e64, e128):
    bsz = xbc3.shape[0]
    const = lambda *shape: pl.BlockSpec(shape, lambda b: (0,) * len(shape))
    seq = lambda w: pl.BlockSpec((1, SEQ, w), lambda b: (b, 0, 0))
    return pl.pallas_call(
        _ssd_kernel,
        out_shape=jax.ShapeDtypeStruct((bsz, SEQ, D_SSD), BF16),
        grid=(bsz,),
        in_specs=[seq(XBC_W), seq(D_SSD), seq(LANES),
                  pl.BlockSpec((1, 2, STATE, D_SSD), lambda b: (b, 0, 0, 0)),
                  const(3, XBC_W), const(1, XBC_W),
                  const(1, D_SSD), const(1, D_SSD), const(2, 2 * LANES, D_SSD),
                  const(2, 2 * LANES, HEADS * LANES)],
        out_specs=seq(D_SSD),
        scratch_shapes=[pltpu.VMEM((SEQ, XBC_W), BF16), pltpu.VMEM((SEQ, D_SSD), F32),
                        pltpu.VMEM((2, STATE, D_SSD), F32)],
        compiler_params=pltpu.CompilerParams(dimension_semantics=("arbitrary",),
                                             vmem_limit_bytes=VMEM_LIMIT),
        name="ssd",
    )(xbc3, z3, dt3, h0, cw, cb, dsk, g, e64, e128)


def _outproj_kernel(x_ref, yssd_ref, scb_ref, v_ref, vp_ref, vn_ref, mod_ref, scw_ref, wo1_ref, wo2_ref,
                    g2_ref, wrt_ref, br_ref, x1_ref, lg_ref):
    tm = TOK_TILE
    per_b = SEQ // tm
    i = pl.program_id(0)
    first = (i % per_b) == 0
    last = (i % per_b) == per_b - 1
    m = mod_ref[0]
    v = v_ref[...].astype(F32)
    vp = jnp.where(first, 0.0, vp_ref[...].astype(F32))
    vn = jnp.where(last, 0.0, vn_ref[...].astype(F32))
    dn = jnp.concatenate([vp, v[:tm - GRID_W]], axis=0)
    up = jnp.concatenate([v[GRID_W:], vn], axis=0)
    scw = scw_ref[...]
    ysc = scb_ref[...].astype(F32) * (scw[0:1] * dn + scw[1:2] * v + scw[2:3] * up)
    out = _dot(yssd_ref[...], wo1_ref[...]) + _dot(ysc.astype(BF16), wo2_ref[...])
    x1 = x_ref[...] + m[2:3] * out
    x1_ref[...] = x1
    h2 = _rms(x1) * g2_ref[...] * (1.0 + m[4:5]) + m[3:4]
    h_hi = h2.astype(BF16)
    h_lo = (h2 - h_hi.astype(F32)).astype(BF16)
    lg_ref[...] = lax.dot_general(wrt_ref[...], jnp.concatenate([h_hi, h_lo, h_hi], axis=1),
                                  (((1,), (1,)), ((), ())), preferred_element_type=F32) + br_ref[...]


def _outproj(x2, yssd, scb, v, mod3, scw, wo1, wo2, g2, wrt, br):
    t = x2.shape[0]
    tm = TOK_TILE
    per_b = SEQ // tm
    r = tm // GRID_W
    nrow = t // GRID_W
    const = lambda *shape: pl.BlockSpec(shape, lambda i: (0,) * len(shape))
    tile = lambda w: pl.BlockSpec((tm, w), lambda i: (i, 0))
    return pl.pallas_call(
        _outproj_kernel,
        out_shape=(jax.ShapeDtypeStruct((t, D_MODEL), F32), jax.ShapeDtypeStruct((N_EXPERTS, t), F32)),
        grid=(t // tm,),
        in_specs=[tile(D_MODEL), tile(D_SSD), tile(D_SC), tile(D_SC),
                  pl.BlockSpec((GRID_W, D_SC), lambda i: (jnp.maximum(i * r - 1, 0), 0)),
                  pl.BlockSpec((GRID_W, D_SC), lambda i: (jnp.minimum((i + 1) * r, nrow - 1), 0)),
                  pl.BlockSpec((1, N_MOD, D_MODEL), lambda i: (i // per_b, 0, 0)),
                  const(3, D_SC), const(D_SSD, D_MODEL), const(D_SC, D_MODEL), const(1, D_MODEL),
                  const(N_EXPERTS, 3 * D_MODEL), const(N_EXPERTS, 1)],
        out_specs=(tile(D_MODEL), pl.BlockSpec((N_EXPERTS, tm), lambda i: (0, i))),
        compiler_params=pltpu.CompilerParams(dimension_semantics=("arbitrary",),
                                             vmem_limit_bytes=VMEM_LIMIT),
        name="outproj",
    )(x2, yssd, scb, v, v, v, mod3, scw, wo1, wo2, g2, wrt, br)


def _route_kernel(lg_ref, dest_ref, gate_ref, idx_ref, meta_ref, rank_ref, carry_ref, *, n_tok, n_blocks):
    tt = RT_TILE
    ne = N_EXPERTS
    eio = lax.broadcasted_iota(jnp.int32, (ne, tt), 0)
    si = lax.broadcasted_iota(jnp.int32, (tt, tt), 0)
    ti = lax.broadcasted_iota(jnp.int32, (tt, tt), 1)
    before = (si < ti).astype(BF16)
    carry_ref[...] = jnp.zeros_like(carry_ref)

    def tile_body(j, c):
        t0 = pl.multiple_of(j * tt, tt)
        l = lg_ref[:, pl.ds(t0, tt)]
        onehot = jnp.zeros((ne, tt), F32)
        tops, sels = [], []
        for _ in range(TOP_K):
            mx = jnp.max(l, axis=0, keepdims=True)
            idx = jnp.min(jnp.where(l == mx, eio, ne), axis=0, keepdims=True)
            sel = eio == idx
            l = jnp.where(sel, -jnp.inf, l)
            onehot = onehot + sel.astype(F32)
            tops.append(mx)
            sels.append(sel)
            idx_ref[pl.ds(len(tops) - 1, 1), pl.ds(t0, tt)] = idx
        ex = [jnp.exp(tv - tops[0]) for tv in tops]
        den = ex[0] + ex[1] + ex[2] + ex[3]
        prefix = _dot(onehot.astype(BF16), before) + carry_ref[:, 0:1]
        for k in range(TOP_K):
            gate_ref[pl.ds(k, 1), pl.ds(t0, tt)] = ex[k] / den
            rk = jnp.sum(jnp.where(sels[k], prefix, 0.0), axis=0, keepdims=True)
            rank_ref[pl.ds(k, 1), pl.ds(t0, tt)] = rk.astype(jnp.int32)
        carry_ref[...] = carry_ref[...] + jnp.sum(onehot, axis=1, keepdims=True)
        return c

    lax.fori_loop(0, n_tok // tt, tile_body, 0)

    counts = carry_ref[...]
    padded = jnp.floor((counts + (MOE_BM - 1)) * (1.0 / MOE_BM)) * MOE_BM
    er = lax.broadcasted_iota(jnp.int32, (ne, ne), 0)
    ec = lax.broadcasted_iota(jnp.int32, (ne, ne), 1)
    pad_start = jnp.dot((ec < er).astype(F32), padded, precision=HIGHEST, preferred_element_type=F32)
    pad_end = pad_start + padded

    def dest_body(j, c):
        t0 = pl.multiple_of(j * tt, tt)
        for k in range(TOP_K):
            idx = idx_ref[pl.ds(k, 1), pl.ds(t0, tt)]
            base = jnp.sum(jnp.where(eio == idx, pad_start[:, 0:1], 0.0), axis=0, keepdims=True)
            dest_ref[pl.ds(k, 1), pl.ds(t0, tt)] = base.astype(jnp.int32) + rank_ref[pl.ds(k, 1), pl.ds(t0, tt)]
        return c

    lax.fori_loop(0, n_tok // tt, dest_body, 0)

    width = meta_ref.shape[1]
    sub = lax.broadcasted_iota(jnp.int32, (ne, width), 0)
    lan = lax.broadcasted_iota(jnp.int32, (ne, width), 1)
    diag = sub == lan
    cnt_row = jnp.sum(jnp.where(diag, counts[:, 0:1], 0.0), axis=0, keepdims=True)
    start_row = jnp.sum(jnp.where(diag, pad_start[:, 0:1], 0.0), axis=0, keepdims=True)
    blk_start = (lan * MOE_BM).astype(F32)
    blk_exp = jnp.sum((pad_end[:, 0:1] <= blk_start).astype(F32), axis=0, keepdims=True)
    blk_exp = jnp.minimum(blk_exp, float(ne - 1))
    used = jnp.sum(padded[:, 0:1], axis=0, keepdims=True) * (1.0 / MOE_BM)
    meta_ref[0:1, :] = cnt_row.astype(jnp.int32)
    meta_ref[1:2, :] = start_row.astype(jnp.int32)
    meta_ref[2:3, :] = blk_exp.astype(jnp.int32)
    meta_ref[3:4, :] = jnp.broadcast_to(used, (1, width)).astype(jnp.int32)
    meta_ref[4:8, :] = jnp.zeros((4, width), jnp.int32)


def _route(lgt, n_blocks):
    ne, n_tok = lgt.shape
    width = -(-n_blocks // LANES) * LANES
    full = lambda *shape: pl.BlockSpec(shape, lambda: (0,) * len(shape))
    return pl.pallas_call(
        functools.partial(_route_kernel, n_tok=n_tok, n_blocks=n_blocks),
        out_shape=(jax.ShapeDtypeStruct((TOP_K, n_tok), jnp.int32),
                   jax.ShapeDtypeStruct((TOP_K, n_tok), F32),
                   jax.ShapeDtypeStruct((TOP_K, n_tok), jnp.int32),
                   jax.ShapeDtypeStruct((8, width), jnp.int32)),
        in_specs=[full(ne, n_tok)],
        out_specs=(full(TOP_K, n_tok), full(TOP_K, n_tok), full(TOP_K, n_tok), full(8, width)),
        scratch_shapes=[pltpu.VMEM((TOP_K, n_tok), jnp.int32), pltpu.VMEM((ne, LANES), F32)],
        compiler_params=pltpu.CompilerParams(vmem_limit_bytes=VMEM_LIMIT),
        name="route",
    )(lgt)


def _dispatch_kernel(dest_ref, cnt_ref, start_ref, nu_ref, x1_ref, meta_ref, mod_ref, g2_ref, zsrc_ref, xs_ref,
                     hbuf, sem, zsem):
    i = pl.program_id(0)
    n = pl.num_programs(0)
    tl = DISP_TILE
    slot = i % 2
    nb = xs_ref.shape[0] // MOE_BM

    def zero_block(b):
        return pltpu.make_async_copy(zsrc_ref, xs_ref.at[pl.ds(b * MOE_BM, MOE_BM)], zsem)

    @pl.when(i == 0)
    def _():
        def start_e(e, c):
            @pl.when(cnt_ref[e] > 0)
            def _():
                zero_block((start_ref[e] + cnt_ref[e] - 1) // MOE_BM).start()
            return c

        def wait_e(e, c):
            @pl.when(cnt_ref[e] > 0)
            def _():
                zero_block(0).wait()
            return c

        def start_t(b, c):
            zero_block(b).start()
            return c

        def wait_t(b, c):
            zero_block(0).wait()
            return c

        lax.fori_loop(0, N_EXPERTS, start_e, 0)
        lax.fori_loop(nu_ref[0], nb, start_t, 0)
        lax.fori_loop(0, N_EXPERTS, wait_e, 0)
        lax.fori_loop(nu_ref[0], nb, wait_t, 0)

    m = mod_ref[0]
    h2 = _rms(x1_ref[...]) * g2_ref[...] * (1.0 + m[4:5]) + m[3:4]
    lo = pltpu.bitcast(h2[:, :PACK_W].astype(BF16).astype(F32), jnp.uint32) >> 16
    hi = pltpu.bitcast(h2[:, PACK_W:].astype(BF16).astype(F32), jnp.uint32) & jnp.uint32(0xFFFF0000)
    row = jnp.concatenate([lo | hi, meta_ref[...], jnp.zeros((tl, D_MODEL - PACK_W - LANES), jnp.uint32)], axis=1)
    hbuf[slot] = row.reshape(tl, SUBLANES, LANES)

    def per_tok(t, c):
        tok = i * tl + t
        for k in range(TOP_K):
            d = dest_ref[tok * TOP_K + k]
            pltpu.make_async_copy(hbuf.at[slot, t], xs_ref.at[d], sem.at[slot]).start(priority=k % 2)
        return c

    lax.fori_loop(0, tl, per_tok, 0, unroll=8)

    def wait_slot(sl):
        for _ in range(TOP_K):
            pltpu.make_async_copy(hbuf.at[sl], xs_ref.at[pl.ds(0, tl)], sem.at[sl]).wait()

    @pl.when(i > 0)
    def _():
        wait_slot(1 - slot)

    @pl.when(i == n - 1)
    def _():
        wait_slot(slot)


def _dispatch(dest_flat, cnt, start, n_used, x1, meta_rows, mod3, g2, zsrc, n_rows):
    n_tok = x1.shape[0]
    tl = DISP_TILE
    per_b = SEQ // tl
    return pl.pallas_call(
        _dispatch_kernel,
        out_shape=jax.ShapeDtypeStruct((n_rows, SUBLANES, LANES), jnp.uint32),
        grid_spec=pltpu.PrefetchScalarGridSpec(
            num_scalar_prefetch=4,
            grid=(n_tok // tl,),
            in_specs=[pl.BlockSpec((tl, D_MODEL), lambda i, *_: (i, 0)),
                      pl.BlockSpec((tl, LANES), lambda i, *_: (i, 0)),
                      pl.BlockSpec((1, N_MOD, D_MODEL), lambda i, *_: (i // per_b, 0, 0)),
                      pl.BlockSpec((1, D_MODEL), lambda i, *_: (0, 0)),
                      pl.BlockSpec((MOE_BM, SUBLANES, LANES), lambda i, *_: (0, 0, 0))],
            out_specs=pl.BlockSpec(memory_space=pl.ANY),
            scratch_shapes=[pltpu.VMEM((2, tl, SUBLANES, LANES), jnp.uint32),
                            pltpu.SemaphoreType.DMA((2,)), pltpu.SemaphoreType.DMA]),
        compiler_params=pltpu.CompilerParams(dimension_semantics=("arbitrary",),
                                             vmem_limit_bytes=VMEM_LIMIT),
        name="dispatch",
    )(dest_flat, cnt, start, n_used, x1, meta_rows, mod3, g2, zsrc)


def _expert_rows_kernel(be_ref, nu_ref, xs_ref, wgu_hbm, bgu_ref, wd_hbm, bd_ref, ys_ref,
                        wgu_raw, wd_raw, wgu_bf, wd_bf, w_sem):
    j = pl.program_id(0)
    nu = nu_ref[0]
    n_lt = D_MODEL // LANES

    def weight_copies(e):
        return (pltpu.make_async_copy(wgu_hbm.at[e], wgu_raw, w_sem.at[0]),
                pltpu.make_async_copy(wd_hbm.at[e], wd_raw, w_sem.at[1]))

    @pl.when(j == 0)
    def _():
        for w in weight_copies(be_ref[0]):
            w.start()

    e_now = be_ref[j]
    new_expert = jnp.logical_or(j == 0, be_ref[jnp.maximum(j - 1, 0)] != e_now)

    @pl.when(jnp.logical_and(j < nu, new_expert))
    def _():
        for w in weight_copies(e_now):
            w.wait()
        n_cc = 8
        for c in range(n_cc):
            cc = slice(c * (2 * D_FF // n_cc), (c + 1) * (2 * D_FF // n_cc))
            wgu_bf[:, cc] = wgu_raw[:, cc].astype(BF16)
        for c in range(n_cc // 2):
            cc = slice(c * (2 * D_MODEL // n_cc), (c + 1) * (2 * D_MODEL // n_cc))
            wd_bf[:, cc] = wd_raw[:, cc].astype(BF16)
        j_next = lax.while_loop(lambda t: jnp.logical_and(t < nu, be_ref[jnp.minimum(t, pl.num_programs(0) - 1)] == e_now),
                                lambda t: t + 1, j + 1)

        @pl.when(j_next < nu)
        def _():
            for w in weight_copies(be_ref[jnp.minimum(j_next, pl.num_programs(0) - 1)]):
                w.start()

    @pl.when(j < nu)
    def _():
        e_f = e_now.astype(F32)
        pr = MOE_BM // MOE_PARTS
        for p in range(MOE_PARTS):
            words = xs_ref[pl.ds(p * pr, pr)].reshape(pr, D_MODEL)
            packed = words[:, 0:PACK_W]
            meta = pltpu.bitcast(words[:, PACK_W:PACK_W + LANES], F32)
            xb = jnp.concatenate(
                [pltpu.bitcast(packed << 16, F32).astype(BF16),
                 pltpu.bitcast(packed & jnp.uint32(0xFFFF0000), F32).astype(BF16)], axis=1)
            gate = jnp.zeros((pr, 1), F32)
            for k in range(TOP_K):
                mk = meta[:, META_IDX + k:META_IDX + k + 1] == e_f
                gate = gate + jnp.where(mk, meta[:, META_GATE + k:META_GATE + k + 1], 0.0)
            gu = _dot(xb, wgu_bf[...]) + bgu_ref[0]
            glu = jnp.minimum(gu[:, :D_FF], SWIGLU_LIMIT)
            lin = jnp.clip(gu[:, D_FF:], -SWIGLU_LIMIT, SWIGLU_LIMIT)
            act = glu * jax.nn.sigmoid(SWIGLU_ALPHA * glu) * (lin + 1.0)
            y = (_dot(act.astype(BF16), wd_bf[...]) + bd_ref[0]) * gate
            ys_ref[pl.ds(p * pr, pr)] = y.reshape(pr, n_lt, LANES)

    @pl.when(j >= nu)
    def _():
        ys_ref[...] = jnp.zeros_like(ys_ref)


def _expert_rows(blk_exp, n_used, xs, wgu, bgu, wd, bd):
    n_rows = xs.shape[0]
    nb = n_rows // MOE_BM
    n_lt = D_MODEL // LANES
    row_blk = lambda j, be, nu: (jnp.minimum(j, nu[0] - 1), 0, 0)
    per_e = lambda j, be, nu: (be[j], 0, 0)
    return pl.pallas_call(
        _expert_rows_kernel,
        out_shape=jax.ShapeDtypeStruct((n_rows, n_lt, LANES), F32),
        grid_spec=pltpu.PrefetchScalarGridSpec(
            num_scalar_prefetch=2,
            grid=(nb,),
            in_specs=[pl.BlockSpec((MOE_BM, SUBLANES, LANES), row_blk),
                      pl.BlockSpec(memory_space=pl.ANY),
                      pl.BlockSpec((1, 1, 2 * D_FF), per_e),
                      pl.BlockSpec(memory_space=pl.ANY),
                      pl.BlockSpec((1, 1, D_MODEL), per_e)],
            out_specs=pl.BlockSpec((MOE_BM, n_lt, LANES), lambda j, be, nu: (j, 0, 0)),
            scratch_shapes=[pltpu.VMEM((D_MODEL, 2 * D_FF), F32), pltpu.VMEM((D_FF, D_MODEL), F32),
                            pltpu.VMEM((D_MODEL, 2 * D_FF), BF16), pltpu.VMEM((D_FF, D_MODEL), BF16),
                            pltpu.SemaphoreType.DMA((2,))]),
        compiler_params=pltpu.CompilerParams(dimension_semantics=("arbitrary",),
                                             vmem_limit_bytes=VMEM_LIMIT),
        name="experts",
    )(blk_exp, n_used, xs, wgu, bgu, wd, bd)


def _gather_combine_kernel(dest_ref, ys_ref, x1_ref, mod_ref, fg_ref, o_ref, buf, sem):
    i = pl.program_id(0)
    n = pl.num_programs(0)
    tc = GATHER_TILE
    slot = i % 2

    def issue(tile, sl):
        def per_tok(t, c):
            tok = tile * tc + t
            for k in range(TOP_K):
                pltpu.make_async_copy(ys_ref.at[dest_ref[tok * TOP_K + k]], buf.at[sl, k * tc + t],
                                      sem.at[sl]).start(priority=k % 2)
            return c

        lax.fori_loop(0, tc, per_tok, 0, unroll=8)

    @pl.when(i == 0)
    def _():
        issue(0, 0)

    @pl.when(i + 1 < n)
    def _():
        issue(i + 1, 1 - slot)

    for _ in range(TOP_K):
        pltpu.make_async_copy(ys_ref.at[pl.ds(0, tc)], buf.at[slot, pl.ds(0, tc)], sem.at[slot]).wait()

    moe = ((buf[slot, pl.ds(0, tc)] + buf[slot, pl.ds(tc, tc)])
           + (buf[slot, pl.ds(2 * tc, tc)] + buf[slot, pl.ds(3 * tc, tc)])).reshape(tc, D_MODEL)
    m = mod_ref[0]
    x2 = x1_ref[...] + m[5:6] * moe
    o_ref[...] = _rms(x2) * fg_ref[...]


def _gather_combine(dest_flat, ys, x1, mod3, fg):
    n_tok = x1.shape[0]
    tc = GATHER_TILE
    per_b = SEQ // tc
    n_lt = D_MODEL // LANES
    return pl.pallas_call(
        _gather_combine_kernel,
        out_shape=jax.ShapeDtypeStruct((n_tok, D_MODEL), F32),
        grid_spec=pltpu.PrefetchScalarGridSpec(
            num_scalar_prefetch=1,
            grid=(n_tok // tc,),
            in_specs=[pl.BlockSpec(memory_space=pl.ANY),
                      pl.BlockSpec((tc, D_MODEL), lambda i, d: (i, 0)),
                      pl.BlockSpec((1, N_MOD, D_MODEL), lambda i, d: (i // per_b, 0, 0)),
                      pl.BlockSpec((1, D_MODEL), lambda i, d: (0, 0))],
            out_specs=pl.BlockSpec((tc, D_MODEL), lambda i, d: (i, 0)),
            scratch_shapes=[pltpu.VMEM((2, TOP_K * tc, n_lt, LANES), F32), pltpu.SemaphoreType.DMA((2,))]),
        compiler_params=pltpu.CompilerParams(dimension_semantics=("arbitrary",),
                                             vmem_limit_bytes=VMEM_LIMIT),
        name="combine",
    )(dest_flat, ys, x1, mod3, fg)


def _expansion_matrices(src0):
    r = (jnp.arange(2 * LANES) % LANES)[:, None]
    out64, out128 = [], []
    for d in range(2):
        l64 = jnp.arange(D_SSD)[None, :]
        l128 = jnp.arange(HEADS * LANES)[None, :]
        out64.append((l64 // HEAD_DIM == r - src0 - HEADS * d).astype(BF16))
        out128.append((l128 // LANES == r - src0 - HEADS * d).astype(BF16))
    return jnp.stack(out64), jnp.stack(out128)


def _pad_lanes(v):
    return jnp.pad(v, [(0, 0)] * (v.ndim - 1) + [(0, LANES - v.shape[-1])])


def kernel(x, c, ctx, c_ctx, w_mod, b_mod, norm1_g, w_in, ssd_conv_w, ssd_conv_b, ssd_dt_bias, ssd_a_log,
           ssd_d, ssd_norm_g, sc_conv_w, w_out, norm2_g, w_router, b_router, w_gate_up, b_gate_up, w_down,
           b_down, final_g):
    bsz = x.shape[0]
    n_tok = bsz * SEQ
    n_assign = n_tok * TOP_K
    n_blocks = n_assign // MOE_BM + N_EXPERTS
    n_rows = n_blocks * MOE_BM
    li = 0

    cvec = jnp.concatenate([c, c_ctx[None, :], jnp.zeros((7, D_MODEL), F32)], axis=0)
    mod3 = _mod(cvec, w_mod[li], b_mod[li][None, :]).reshape(bsz + 8, N_MOD, D_MODEL)

    w = w_in[li]
    wz = w[:, Z0:X0].astype(BF16)
    wxbc = w[:, X0:DT0].astype(BF16)
    wdt = _pad_lanes(w[:, DT0:SC0]).astype(BF16)
    wb = w[:, SC0:SC0 + D_SC].astype(BF16)
    wc = w[:, SC0 + D_SC:SC0 + 2 * D_SC].astype(BF16)
    wu = w[:, SC0 + 2 * D_SC:].astype(BF16)
    g1 = norm1_g[li][None, :]
    cw = ssd_conv_w[li]
    cb = ssd_conv_b[li][None, :]
    dtb = _pad_lanes(ssd_dt_bias[li].reshape(1, 2 * HEADS))
    alog = _pad_lanes(ssd_a_log[li].reshape(1, 2 * HEADS))
    e64_ctx, _ = _expansion_matrices(0)

    h0 = _ctx_states(ctx, mod3, g1, wxbc[:, :XB_W], wdt, cw[:, :XB_W], cb[:, :XB_W], dtb, alog, e64_ctx)

    rep = lambda a: _pad_lanes(jnp.tile(a[..., :2 * HEADS], (1, DT_COPIES)))
    e64, e128 = _expansion_matrices(DT_DA0)
    x2 = x.reshape(n_tok, D_MODEL)
    z, xbc, dtp, scb, v = _inproj(x2, mod3, g1, wz, wxbc, rep(wdt), rep(dtb), rep(alog), wb, wc, wu)

    dsk = jnp.repeat(ssd_d[li], HEAD_DIM)[None, :]
    yssd = _ssd(xbc.reshape(bsz, SEQ, XBC_W), z.reshape(bsz, SEQ, D_SSD), dtp.reshape(bsz, SEQ, LANES), h0,
                cw, cb, dsk, ssd_norm_g[li][None, :], e64, e128)

    wo = w_out[li].astype(BF16)
    g2 = norm2_g[li][None, :]
    wr = w_router[li].T
    wr_hi = wr.astype(BF16)
    wr_lo = (wr - wr_hi.astype(F32)).astype(BF16)
    x1, lgt = _outproj(x2, yssd.reshape(n_tok, D_SSD), scb, v, mod3, sc_conv_w[li], wo[:D_SSD], wo[D_SSD:],
                       g2, jnp.concatenate([wr_hi, wr_hi, wr_lo], axis=1), b_router[li][:, None])

    dest_t, gate_t, idx_t, meta = _route(lgt, n_blocks)
    dest_flat = dest_t.T.reshape(n_assign)
    cnt = meta[0, :N_EXPERTS]
    start = meta[1, :N_EXPERTS]
    blk_exp = meta[2, :n_blocks]
    n_used = meta[3, :1]

    meta_rows = lax.bitcast_convert_type(_pad_lanes(jnp.concatenate(
        [idx_t.T.astype(F32), gate_t.T], axis=1)), jnp.uint32)
    pad_meta = lax.bitcast_convert_type(_pad_lanes(jnp.concatenate(
        [jnp.full((MOE_BM, TOP_K), -1.0, F32), jnp.zeros((MOE_BM, TOP_K), F32)], axis=1)), jnp.uint32)
    zsrc = jnp.concatenate([jnp.zeros((MOE_BM, PACK_W), jnp.uint32), pad_meta,
                            jnp.zeros((MOE_BM, D_MODEL - PACK_W - LANES), jnp.uint32)],
                           axis=1).reshape(MOE_BM, SUBLANES, LANES)

    xs = _dispatch(dest_flat, cnt, start, n_used, x1, meta_rows, mod3, g2, zsrc, n_rows)
    ys = _expert_rows(blk_exp, n_used, xs, w_gate_up[li], b_gate_up[li][:, None, :],
                      w_down[li], b_down[li][:, None, :])
    out = _gather_combine(dest_flat, ys, x1, mod3, final_g[None, :])
    return out.reshape(bsz, SEQ, D_MODEL)
```

```python
import functools

import jax
import jax.numpy as jnp
from jax import lax
from jax.experimental import pallas as pl
from jax.experimental.pallas import tpu as pltpu

F32 = jnp.float32
BF16 = jnp.bfloat16
HIGHEST = lax.Precision.HIGHEST

D_MODEL = 1024
SEQ = 2048
CTX_LEN = 256
GRID_W = 64
D_SSD = 1024
D_SC = 1024
HEAD_DIM = 64
HEADS = 16
GROUPS = 2
STATE = 128
CHUNK = 128
N_EXPERTS = 32
TOP_K = 4
D_FF = 1024
SWIGLU_LIMIT = 7.0
SWIGLU_ALPHA = 1.702
NORM_EPS = 1e-6
N_MOD = 6
XBC_W = D_SSD + 2 * GROUPS * STATE
XB_W = D_SSD + GROUPS * STATE
LANES = 128

Z0 = 0
X0 = Z0 + D_SSD
B0 = X0 + D_SSD
C0 = B0 + GROUPS * STATE
DT0 = C0 + GROUPS * STATE
SC0 = DT0 + 2 * HEADS

TOK_TILE = 512
MOE_BM = 512
MOE_PARTS = 2
RT_TILE = 512
CTX_BATCH = 4
DISP_TILE = 512
GATHER_TILE = 512
DT_COPIES = 3
DT_LOG0 = 2 * HEADS
DT_DA0 = 4 * HEADS
SUBLANES = 8
PACK_W = D_MODEL // 2
META_IDX = 0
META_GATE = TOP_K
VMEM_LIMIT = 56 * 1024 * 1024


def _silu(v):
    return v * jax.nn.sigmoid(v)


def _softplus(v):
    return jnp.maximum(v, 0.0) + jnp.log1p(jnp.exp(-jnp.abs(v)))


def _rms(v):
    return v * lax.rsqrt(jnp.mean(v * v, axis=-1, keepdims=True) + NORM_EPS)


def _dot(a, b):
    return jnp.dot(a, b, preferred_element_type=F32)


def _expand2(v, e2):
    hi = v.astype(BF16)
    lo = (v - hi.astype(F32)).astype(BF16)
    return _dot(jnp.concatenate([hi, lo], axis=1), e2)


def _mod_kernel(c_ref, w_ref, b_ref, o_ref):
    o_ref[...] = jnp.dot(_silu(c_ref[...]), w_ref[...], precision=HIGHEST,
                         preferred_element_type=F32) + b_ref[...]


def _mod(cvec, w_mod, b_mod):
    rows = cvec.shape[0]
    n = w_mod.shape[1]
    tn = 1536
    return pl.pallas_call(
        _mod_kernel,
        out_shape=jax.ShapeDtypeStruct((rows, n), F32),
        grid=(n // tn,),
        in_specs=[pl.BlockSpec((rows, D_MODEL), lambda j: (0, 0)),
                  pl.BlockSpec((D_MODEL, tn), lambda j: (0, j)),
                  pl.BlockSpec((1, tn), lambda j: (0, j))],
        out_specs=pl.BlockSpec((rows, tn), lambda j: (0, j)),
        compiler_params=pltpu.CompilerParams(dimension_semantics=("arbitrary",),
                                             vmem_limit_bytes=VMEM_LIMIT),
        name="mod",
    )(cvec, w_mod, b_mod)


def _ctx_kernel(ctx_ref, mod_ref, g1_ref, wxb_ref, wdt_ref, cw_ref, cb_ref, dtb_ref, alog_ref, e64_ref,
                h0_ref):
    L = CTX_LEN
    nb = ctx_ref.shape[0]
    m = mod_ref[0]
    hc = _rms(ctx_ref[...].reshape(nb * L, D_MODEL)) * g1_ref[...] * (1.0 + m[1:2]) + m[0:1]
    hb = hc.astype(BF16)
    pxb = _dot(hb, wxb_ref[...])
    dtr = _dot(hb, wdt_ref[...])
    rowl = lax.broadcasted_iota(jnp.int32, (nb * L, XB_W), 0) & (L - 1)
    dn = jnp.where(rowl == 0, 0.0, pltpu.roll(pxb, 1, 0))
    up = jnp.where(rowl == L - 1, 0.0, pltpu.roll(pxb, nb * L - 1, 0))
    cw = cw_ref[...]
    xb = _silu(cw[0:1] * dn + cw[1:2] * pxb + cw[2:3] * up + cb_ref[...])
    dt_all = _softplus(dtr + dtb_ref[...])
    da_all = dt_all * (-jnp.exp(alog_ref[...]))
    ri = lax.broadcasted_iota(jnp.int32, (L, L), 0)
    ci = lax.broadcasted_iota(jnp.int32, (L, L), 1)
    for bi in range(nb):
        rs = slice(bi * L, (bi + 1) * L)
        xs = xb[rs, :D_SSD]
        bm = xb[rs, D_SSD:].astype(BF16)
        dt = dt_all[rs]
        da = da_all[rs]
        for d in range(2):
            tri = (ci <= ri) if d == 0 else (ci >= ri)
            cum = jnp.dot(tri.astype(F32), da, precision=HIGHEST, preferred_element_type=F32)
            last = cum[L - 1:L] if d == 0 else cum[0:1]
            w_e = _expand2(jnp.exp(last - cum) * dt, e64_ref[d])
            xw = (xs * w_e).astype(BF16)
            for g in range(GROUPS):
                gw = D_SSD // GROUPS
                st = lax.dot_general(bm[:, g * STATE:(g + 1) * STATE], xw[:, g * gw:(g + 1) * gw],
                                     (((0,), (0,)), ((), ())), preferred_element_type=F32)
                h0_ref[bi, d, :, g * gw:(g + 1) * gw] = st


def _ctx_states(ctx, mod3, g1, wxb, wdt, cw, cb, dtb, alog, e64):
    bsz = ctx.shape[0]
    mod_row = bsz
    nb = CTX_BATCH if bsz % CTX_BATCH == 0 else 1
    const = lambda *shape: pl.BlockSpec(shape, lambda b: (0,) * len(shape))
    return pl.pallas_call(
        _ctx_kernel,
        out_shape=jax.ShapeDtypeStruct((bsz, 2, STATE, D_SSD), F32),
        grid=(bsz // nb,),
        in_specs=[pl.BlockSpec((nb, CTX_LEN, D_MODEL), lambda b: (b, 0, 0)),
                  pl.BlockSpec((1, N_MOD, D_MODEL), lambda b: (mod_row, 0, 0)),
                  const(1, D_MODEL), const(D_MODEL, XB_W), const(D_MODEL, LANES),
                  const(3, XB_W), const(1, XB_W), const(1, LANES), const(1, LANES),
                  const(2, 2 * LANES, D_SSD)],
        out_specs=pl.BlockSpec((nb, 2, STATE, D_SSD), lambda b: (b, 0, 0, 0)),
        compiler_params=pltpu.CompilerParams(dimension_semantics=("arbitrary",),
                                             vmem_limit_bytes=VMEM_LIMIT),
        name="ctx_states",
    )(ctx, mod3, g1, wxb, wdt, cw, cb, dtb, alog, e64)


def _inproj_kernel(x_ref, mod_ref, g1_ref, wz_ref, wxbc_ref, wdt_ref, dtb_ref, alog_ref, wb_ref, wc_ref, wu_ref,
                   z_ref, xbc_ref, dt_ref, scb_ref, v_ref):
    m = mod_ref[0]
    hx = _rms(x_ref[...]) * g1_ref[...] * (1.0 + m[1:2]) + m[0:1]
    hb = hx.astype(BF16)
    z_ref[...] = _dot(hb, wz_ref[...]).astype(BF16)
    xbc_ref[...] = _dot(hb, wxbc_ref[...]).astype(BF16)
    dt = _softplus(_dot(hb, wdt_ref[...]) + dtb_ref[...])
    lane = lax.broadcasted_iota(jnp.int32, dt.shape, 1)
    dt_ref[...] = jnp.where(lane < DT_LOG0, dt,
                            jnp.where(lane < DT_DA0, jnp.log(dt), dt * (-jnp.exp(alog_ref[...]))))
    scb_ref[...] = _dot(hb, wb_ref[...]).astype(BF16)
    v_ref[...] = (_dot(hb, wc_ref[...]) * _dot(hb, wu_ref[...])).astype(BF16)


def _inproj(x2, mod3, g1, wz, wxbc, wdt, dtb, alog, wb, wc, wu):
    t = x2.shape[0]
    tm = TOK_TILE
    per_b = SEQ // tm
    const = lambda *shape: pl.BlockSpec(shape, lambda i: (0,) * len(shape))
    tile = lambda w: pl.BlockSpec((tm, w), lambda i: (i, 0))
    return pl.pallas_call(
        _inproj_kernel,
        out_shape=(jax.ShapeDtypeStruct((t, D_SSD), BF16), jax.ShapeDtypeStruct((t, XBC_W), BF16),
                   jax.ShapeDtypeStruct((t, LANES), F32), jax.ShapeDtypeStruct((t, D_SC), BF16),
                   jax.ShapeDtypeStruct((t, D_SC), BF16)),
        grid=(t // tm,),
        in_specs=[tile(D_MODEL),
                  pl.BlockSpec((1, N_MOD, D_MODEL), lambda i: (i // per_b, 0, 0)),
                  const(1, D_MODEL), const(D_MODEL, D_SSD), const(D_MODEL, XBC_W), const(D_MODEL, LANES),
                  const(1, LANES), const(1, LANES),
                  const(D_MODEL, D_SC), const(D_MODEL, D_SC), const(D_MODEL, D_SC)],
        out_specs=(tile(D_SSD), tile(XBC_W), tile(LANES), tile(D_SC), tile(D_SC)),
        compiler_params=pltpu.CompilerParams(dimension_semantics=("arbitrary",),
                                             vmem_limit_bytes=VMEM_LIMIT),
        name="inproj",
    )(x2, mod3, g1, wz, wxbc, wdt, dtb, alog, wb, wc, wu)


def _ssd_kernel(xbc_ref, z_ref, dt_ref, h0_ref, cw_ref, cb_ref, dsk_ref, g_ref,
                e64_ref, e128_ref, o_ref, xc_ref, y_ref, s_ref):
    Q = CHUNK
    nck = SEQ // Q
    gw = D_SSD // GROUPS

    rowi = lax.broadcasted_iota(jnp.int32, (SUBLANES, XBC_W), 0)

    def conv_body(c, carry):
        r0 = pl.multiple_of(c * Q, Q)
        main = xbc_ref[0, pl.ds(r0, Q), :].astype(F32)
        pstart = pl.multiple_of(jnp.maximum(r0 - 16, 0), 16)
        nstart = pl.multiple_of(jnp.minimum(r0 + Q, SEQ - 16), 16)
        prev = xbc_ref[0, pl.ds(pstart, 16), :].astype(F32)[15:16]
        nxt = xbc_ref[0, pl.ds(nstart, 16), :].astype(F32)[0:1]
        prev = jnp.where(c > 0, prev, 0.0)
        nxt = jnp.where(c < nck - 1, nxt, 0.0)
        dn = pltpu.roll(main, 1, 0)
        up = pltpu.roll(main, Q - 1, 0)
        dn = jnp.concatenate([jnp.where(rowi == 0, prev, dn[0:SUBLANES]), dn[SUBLANES:]], axis=0)
        up = jnp.concatenate([up[:Q - SUBLANES], jnp.where(rowi == SUBLANES - 1, nxt, up[Q - SUBLANES:])], axis=0)
        cw = cw_ref[...]
        conv = cw[0:1] * dn + cw[1:2] * main + cw[2:3] * up + cb_ref[...]
        xc_ref[pl.ds(r0, Q), :] = _silu(conv).astype(BF16)
        return carry

    lax.fori_loop(0, nck, conv_body, 0)

    ri = lax.broadcasted_iota(jnp.int32, (Q, Q), 0)
    ci = lax.broadcasted_iota(jnp.int32, (Q, Q), 1)
    lane = lax.broadcasted_iota(jnp.int32, (Q, LANES), 1)
    da_lanes = jnp.logical_and(lane >= DT_DA0, lane < DT_DA0 + 2 * HEADS)

    def chunk(c, d, first):
        r0 = pl.multiple_of(c * Q, Q)
        rows = pl.ds(r0, Q)
        xs_b = xc_ref[rows, 0:D_SSD]
        xs = xs_b.astype(F32)
        bm = xc_ref[rows, D_SSD:D_SSD + GROUPS * STATE]
        cm = xc_ref[rows, D_SSD + GROUPS * STATE:XBC_W]
        dtp = dt_ref[0, rows, :]
        da = jnp.where(da_lanes, dtp, 0.0)
        dt = pltpu.roll(dtp, DT_DA0, 1)
        log_dt = pltpu.roll(dtp, DT_DA0 - DT_LOG0, 1)
        tri = (ci <= ri) if d == 0 else (ci >= ri)
        p0 = da.astype(BF16)
        r1 = da - p0.astype(F32)
        p1 = r1.astype(BF16)
        p2 = (r1 - p1.astype(F32)).astype(BF16)
        tri_b = jnp.where(tri, 1.0, 0.0).astype(BF16)
        cum = _dot(jnp.concatenate([tri_b, tri_b, tri_b], axis=1),
                   jnp.concatenate([p0, p1, p2], axis=0))
        sub_t = (cum - log_dt).T
        last = cum[Q - 1:Q] if d == 0 else cum[0:1]
        ecum_e = _expand2(jnp.exp(cum), e64_ref[d])
        w_e = _expand2(jnp.where(da_lanes, jnp.exp(last - cum) * dt, 0.0), e64_ref[d])
        colb = _expand2(cum, e128_ref[d])
        decay_e = ecum_e[Q - 1:Q] if d == 0 else ecum_e[0:1]

        gmat = [lax.dot_general(cm[:, g * STATE:(g + 1) * STATE], bm[:, g * STATE:(g + 1) * STATE],
                                (((1,), (1,)), ((), ())), preferred_element_type=F32)
                for g in range(GROUPS)]
        zero_b = jnp.zeros((Q, LANES), BF16)
        y_parts = []
        for p in range(HEADS // 2):
            g = (2 * p) // (HEADS // GROUPS)
            ms = []
            for hh in (2 * p, 2 * p + 1):
                src = DT_DA0 + HEADS * d + hh
                seg = colb[:, hh * LANES:(hh + 1) * LANES] - sub_t[src:src + 1, :]
                ms.append((jnp.where(tri, jnp.exp(seg), 0.0) * gmat[g]).astype(BF16))
            mcat = jnp.concatenate(ms, axis=1)
            xp = xs_b[:, p * LANES:(p + 1) * LANES]
            rhs = jnp.concatenate([jnp.where(lane < HEAD_DIM, xp, zero_b),
                                   jnp.where(lane >= HEAD_DIM, xp, zero_b)], axis=0)
            y_parts.append(_dot(mcat, rhs))
        y_diag = jnp.concatenate(y_parts, axis=1)

        s_old = s_ref[d]
        s_bf = s_old.astype(BF16)
        y_off = jnp.concatenate(
            [_dot(cm[:, g * STATE:(g + 1) * STATE], s_bf[:, g * gw:(g + 1) * gw]) for g in range(GROUPS)],
            axis=1)
        y = y_diag + y_off * ecum_e

        xw = (xs * w_e).astype(BF16)
        upd = jnp.concatenate(
            [lax.dot_general(bm[:, g * STATE:(g + 1) * STATE], xw[:, g * gw:(g + 1) * gw],
                             (((0,), (0,)), ((), ())), preferred_element_type=F32) for g in range(GROUPS)],
            axis=1)
        s_ref[d] = s_old * decay_e + upd

        if first:
            y_ref[rows, :] = y + dsk_ref[...] * xs
        else:
            tot = y_ref[rows, :] + y
            zz = z_ref[0, rows, :].astype(F32)
            gz = tot * _silu(zz)
            outs = []
            for g in range(GROUPS):
                gg = gz[:, g * gw:(g + 1) * gw]
                outs.append(gg * lax.rsqrt(jnp.mean(gg * gg, axis=-1, keepdims=True) + NORM_EPS))
            o_ref[0, rows, :] = (jnp.concatenate(outs, axis=1) * g_ref[...]).astype(BF16)

    s_ref[...] = h0_ref[0]

    def first_half(i, carry):
        chunk(i, 0, True)
        chunk(nck - 1 - i, 1, True)
        return carry

    def second_half(i, carry):
        chunk(i, 0, False)
        chunk(nck - 1 - i, 1, False)
        return carry

    lax.fori_loop(0, nck // 2, first_half, 0)
    lax.fori_loop(nck // 2, nck, second_half, 0)


def _ssd(xbc3, z3, dt3, h0, cw, cb, dsk, g, e64, e128):
    bsz = xbc3.shape[0]
    const = lambda *shape: pl.BlockSpec(shape, lambda b: (0,) * len(shape))
    seq = lambda w: pl.BlockSpec((1, SEQ, w), lambda b: (b, 0, 0))
    return pl.pallas_call(
        _ssd_kernel,
        out_shape=jax.ShapeDtypeStruct((bsz, SEQ, D_SSD), BF16),
        grid=(bsz,),
        in_specs=[seq(XBC_W), seq(D_SSD), seq(LANES),
                  pl.BlockSpec((1, 2, STATE, D_SSD), lambda b: (b, 0, 0, 0)),
                  const(3, XBC_W), const(1, XBC_W),
                  const(1, D_SSD), const(1, D_SSD), const(2, 2 * LANES, D_SSD),
                  const(2, 2 * LANES, HEADS * LANES)],
        out_specs=seq(D_SSD),
        scratch_shapes=[pltpu.VMEM((SEQ, XBC_W), BF16), pltpu.VMEM((SEQ, D_SSD), F32),
                        pltpu.VMEM((2, STATE, D_SSD), F32)],
        compiler_params=pltpu.CompilerParams(dimension_semantics=("arbitrary",),
                                             vmem_limit_bytes=VMEM_LIMIT),
        name="ssd",
    )(xbc3, z3, dt3, h0, cw, cb, dsk, g, e64, e128)


def _outproj_kernel(x_ref, yssd_ref, scb_ref, v_ref, vp_ref, vn_ref, mod_ref, scw_ref, wo1_ref, wo2_ref,
                    g2_ref, wrt_ref, br_ref, x1_ref, lg_ref):
    tm = TOK_TILE
    per_b = SEQ // tm
    i = pl.program_id(0)
    first = (i % per_b) == 0
    last = (i % per_b) == per_b - 1
    m = mod_ref[0]
    v = v_ref[...].astype(F32)
    vp = jnp.where(first, 0.0, vp_ref[...].astype(F32))
    vn = jnp.where(last, 0.0, vn_ref[...].astype(F32))
    dn = jnp.concatenate([vp, v[:tm - GRID_W]], axis=0)
    up = jnp.concatenate([v[GRID_W:], vn], axis=0)
    scw = scw_ref[...]
    ysc = scb_ref[...].astype(F32) * (scw[0:1] * dn + scw[1:2] * v + scw[2:3] * up)
    out = _dot(yssd_ref[...], wo1_ref[...]) + _dot(ysc.astype(BF16), wo2_ref[...])
    x1 = x_ref[...] + m[2:3] * out
    x1_ref[...] = x1
    h2 = _rms(x1) * g2_ref[...] * (1.0 + m[4:5]) + m[3:4]
    h_hi = h2.astype(BF16)
    h_lo = (h2 - h_hi.astype(F32)).astype(BF16)
    lg_ref[...] = lax.dot_general(wrt_ref[...], jnp.concatenate([h_hi, h_lo, h_hi], axis=1),
                                  (((1,), (1,)), ((), ())), preferred_element_type=F32) + br_ref[...]


def _outproj(x2, yssd, scb, v, mod3, scw, wo1, wo2, g2, wrt, br):
    t = x2.shape[0]
    tm = TOK_TILE
    per_b = SEQ // tm
    r = tm // GRID_W
    nrow = t // GRID_W
    const = lambda *shape: pl.BlockSpec(shape, lambda i: (0,) * len(shape))
    tile = lambda w: pl.BlockSpec((tm, w), lambda i: (i, 0))
    return pl.pallas_call(
        _outproj_kernel,
        out_shape=(jax.ShapeDtypeStruct((t, D_MODEL), F32), jax.ShapeDtypeStruct((N_EXPERTS, t), F32)),
        grid=(t // tm,),
        in_specs=[tile(D_MODEL), tile(D_SSD), tile(D_SC), tile(D_SC),
                  pl.BlockSpec((GRID_W, D_SC), lambda i: (jnp.maximum(i * r - 1, 0), 0)),
                  pl.BlockSpec((GRID_W, D_SC), lambda i: (jnp.minimum((i + 1) * r, nrow - 1), 0)),
                  pl.BlockSpec((1, N_MOD, D_MODEL), lambda i: (i // per_b, 0, 0)),
                  const(3, D_SC), const(D_SSD, D_MODEL), const(D_SC, D_MODEL), const(1, D_MODEL),
                  const(N_EXPERTS, 3 * D_MODEL), const(N_EXPERTS, 1)],
        out_specs=(tile(D_MODEL), pl.BlockSpec((N_EXPERTS, tm), lambda i: (0, i))),
        compiler_params=pltpu.CompilerParams(dimension_semantics=("arbitrary",),
                                             vmem_limit_bytes=VMEM_LIMIT),
        name="outproj",
    )(x2, yssd, scb, v, v, v, mod3, scw, wo1, wo2, g2, wrt, br)


def _route_kernel(lg_ref, dest_ref, gate_ref, idx_ref, meta_ref, rank_ref, carry_ref, *, n_tok, n_blocks):
    tt = RT_TILE
    ne = N_EXPERTS
    eio = lax.broadcasted_iota(jnp.int32, (ne, tt), 0)
    si = lax.broadcasted_iota(jnp.int32, (tt, tt), 0)
    ti = lax.broadcasted_iota(jnp.int32, (tt, tt), 1)
    before = (si < ti).astype(BF16)
    carry_ref[...] = jnp.zeros_like(carry_ref)

    def tile_body(j, c):
        t0 = pl.multiple_of(j * tt, tt)
        l = lg_ref[:, pl.ds(t0, tt)]
        onehot = jnp.zeros((ne, tt), F32)
        tops, sels = [], []
        for _ in range(TOP_K):
            mx = jnp.max(l, axis=0, keepdims=True)
            idx = jnp.min(jnp.where(l == mx, eio, ne), axis=0, keepdims=True)
            sel = eio == idx
            l = jnp.where(sel, -jnp.inf, l)
            onehot = onehot + sel.astype(F32)
            tops.append(mx)
            sels.append(sel)
            idx_ref[pl.ds(len(tops) - 1, 1), pl.ds(t0, tt)] = idx
        ex = [jnp.exp(tv - tops[0]) for tv in tops]
        den = ex[0] + ex[1] + ex[2] + ex[3]
        prefix = _dot(onehot.astype(BF16), before) + carry_ref[:, 0:1]
        for k in range(TOP_K):
            gate_ref[pl.ds(k, 1), pl.ds(t0, tt)] = ex[k] / den
            rk = jnp.sum(jnp.where(sels[k], prefix, 0.0), axis=0, keepdims=True)
            rank_ref[pl.ds(k, 1), pl.ds(t0, tt)] = rk.astype(jnp.int32)
        carry_ref[...] = carry_ref[...] + jnp.sum(onehot, axis=1, keepdims=True)
        return c

    lax.fori_loop(0, n_tok // tt, tile_body, 0)

    counts = carry_ref[...]
    padded = jnp.floor((counts + (MOE_BM - 1)) * (1.0 / MOE_BM)) * MOE_BM
    er = lax.broadcasted_iota(jnp.int32, (ne, ne), 0)
    ec = lax.broadcasted_iota(jnp.int32, (ne, ne), 1)
    pad_start = jnp.dot((ec < er).astype(F32), padded, precision=HIGHEST, preferred_element_type=F32)
    pad_end = pad_start + padded

    def dest_body(j, c):
        t0 = pl.multiple_of(j * tt, tt)
        for k in range(TOP_K):
            idx = idx_ref[pl.ds(k, 1), pl.ds(t0, tt)]
            base = jnp.sum(jnp.where(eio == idx, pad_start[:, 0:1], 0.0), axis=0, keepdims=True)
            dest_ref[pl.ds(k, 1), pl.ds(t0, tt)] = base.astype(jnp.int32) + rank_ref[pl.ds(k, 1), pl.ds(t0, tt)]
        return c

    lax.fori_loop(0, n_tok // tt, dest_body, 0)

    width = meta_ref.shape[1]
    sub = lax.broadcasted_iota(jnp.int32, (ne, width), 0)
    lan = lax.broadcasted_iota(jnp.int32, (ne, width), 1)
    diag = sub == lan
    cnt_row = jnp.sum(jnp.where(diag, counts[:, 0:1], 0.0), axis=0, keepdims=True)
    start_row = jnp.sum(jnp.where(diag, pad_start[:, 0:1], 0.0), axis=0, keepdims=True)
    blk_start = (lan * MOE_BM).astype(F32)
    blk_exp = jnp.sum((pad_end[:, 0:1] <= blk_start).astype(F32), axis=0, keepdims=True)
    blk_exp = jnp.minimum(blk_exp, float(ne - 1))
    used = jnp.sum(padded[:, 0:1], axis=0, keepdims=True) * (1.0 / MOE_BM)
    meta_ref[0:1, :] = cnt_row.astype(jnp.int32)
    meta_ref[1:2, :] = start_row.astype(jnp.int32)
    meta_ref[2:3, :] = blk_exp.astype(jnp.int32)
    meta_ref[3:4, :] = jnp.broadcast_to(used, (1, width)).astype(jnp.int32)
    meta_ref[4:8, :] = jnp.zeros((4, width), jnp.int32)


def _route(lgt, n_blocks):
    ne, n_tok = lgt.shape
    width = -(-n_blocks // LANES) * LANES
    full = lambda *shape: pl.BlockSpec(shape, lambda: (0,) * len(shape))
    return pl.pallas_call(
        functools.partial(_route_kernel, n_tok=n_tok, n_blocks=n_blocks),
        out_shape=(jax.ShapeDtypeStruct((TOP_K, n_tok), jnp.int32),
                   jax.ShapeDtypeStruct((TOP_K, n_tok), F32),
                   jax.ShapeDtypeStruct((TOP_K, n_tok), jnp.int32),
                   jax.ShapeDtypeStruct((8, width), jnp.int32)),
        in_specs=[full(ne, n_tok)],
        out_specs=(full(TOP_K, n_tok), full(TOP_K, n_tok), full(TOP_K, n_tok), full(8, width)),
        scratch_shapes=[pltpu.VMEM((TOP_K, n_tok), jnp.int32), pltpu.VMEM((ne, LANES), F32)],
        compiler_params=pltpu.CompilerParams(vmem_limit_bytes=VMEM_LIMIT),
        name="route",
    )(lgt)


def _dispatch_kernel(dest_ref, cnt_ref, start_ref, nu_ref, x1_ref, meta_ref, mod_ref, g2_ref, zsrc_ref, xs_ref,
                     hbuf, sem, zsem):
    i = pl.program_id(0)
    n = pl.num_programs(0)
    tl = DISP_TILE
    slot = i % 2
    nb = xs_ref.shape[0] // MOE_BM

    def zero_block(b):
        return pltpu.make_async_copy(zsrc_ref, xs_ref.at[pl.ds(b * MOE_BM, MOE_BM)], zsem)

    @pl.when(i == 0)
    def _():
        def start_e(e, c):
            @pl.when(cnt_ref[e] > 0)
            def _():
                zero_block((start_ref[e] + cnt_ref[e] - 1) // MOE_BM).start()
            return c

        def wait_e(e, c):
            @pl.when(cnt_ref[e] > 0)
            def _():
                zero_block(0).wait()
            return c

        def start_t(b, c):
            zero_block(b).start()
            return c

        def wait_t(b, c):
            zero_block(0).wait()
            return c

        lax.fori_loop(0, N_EXPERTS, start_e, 0)
        lax.fori_loop(nu_ref[0], nb, start_t, 0)
        lax.fori_loop(0, N_EXPERTS, wait_e, 0)
        lax.fori_loop(nu_ref[0], nb, wait_t, 0)

    m = mod_ref[0]
    h2 = _rms(x1_ref[...]) * g2_ref[...] * (1.0 + m[4:5]) + m[3:4]
    lo = pltpu.bitcast(h2[:, :PACK_W].astype(BF16).astype(F32), jnp.uint32) >> 16
    hi = pltpu.bitcast(h2[:, PACK_W:].astype(BF16).astype(F32), jnp.uint32) & jnp.uint32(0xFFFF0000)
    row = jnp.concatenate([lo | hi, meta_ref[...], jnp.zeros((tl, D_MODEL - PACK_W - LANES), jnp.uint32)], axis=1)
    hbuf[slot] = row.reshape(tl, SUBLANES, LANES)

    def per_tok(t, c):
        tok = i * tl + t
        for k in range(TOP_K):
            d = dest_ref[k * (dest_ref.shape[0] // TOP_K) + tok]
            pltpu.make_async_copy(hbuf.at[slot, t], xs_ref.at[d], sem.at[slot]).start(priority=k % 2)
        return c

    lax.fori_loop(0, tl, per_tok, 0, unroll=8)

    def wait_slot(sl):
        for _ in range(TOP_K):
            pltpu.make_async_copy(hbuf.at[sl], xs_ref.at[pl.ds(0, tl)], sem.at[sl]).wait()

    @pl.when(i > 0)
    def _():
        wait_slot(1 - slot)

    @pl.when(i == n - 1)
    def _():
        wait_slot(slot)


def _dispatch(dest_flat, cnt, start, n_used, x1, meta_rows, mod3, g2, zsrc, n_rows):
    n_tok = x1.shape[0]
    tl = DISP_TILE
    per_b = SEQ // tl
    return pl.pallas_call(
        _dispatch_kernel,
        out_shape=jax.ShapeDtypeStruct((n_rows, SUBLANES, LANES), jnp.uint32),
        grid_spec=pltpu.PrefetchScalarGridSpec(
            num_scalar_prefetch=4,
            grid=(n_tok // tl,),
            in_specs=[pl.BlockSpec((tl, D_MODEL), lambda i, *_: (i, 0)),
                      pl.BlockSpec((tl, LANES), lambda i, *_: (i, 0)),
                      pl.BlockSpec((1, N_MOD, D_MODEL), lambda i, *_: (i // per_b, 0, 0)),
                      pl.BlockSpec((1, D_MODEL), lambda i, *_: (0, 0)),
                      pl.BlockSpec((MOE_BM, SUBLANES, LANES), lambda i, *_: (0, 0, 0))],
            out_specs=pl.BlockSpec(memory_space=pl.ANY),
            scratch_shapes=[pltpu.VMEM((2, tl, SUBLANES, LANES), jnp.uint32),
                            pltpu.SemaphoreType.DMA((2,)), pltpu.SemaphoreType.DMA]),
        compiler_params=pltpu.CompilerParams(dimension_semantics=("arbitrary",),
                                             vmem_limit_bytes=VMEM_LIMIT),
        name="dispatch",
    )(dest_flat, cnt, start, n_used, x1, meta_rows, mod3, g2, zsrc)


def _expert_rows_kernel(be_ref, nu_ref, xs_ref, wgu_hbm, bgu_ref, wd_hbm, bd_ref, ys_ref,
                        wgu_raw, wd_raw, wgu_bf, wd_bf, w_sem):
    j = pl.program_id(0)
    nu = nu_ref[0]
    n_lt = D_MODEL // LANES

    def weight_copies(e):
        return (pltpu.make_async_copy(wgu_hbm.at[e], wgu_raw, w_sem.at[0]),
                pltpu.make_async_copy(wd_hbm.at[e], wd_raw, w_sem.at[1]))

    @pl.when(j == 0)
    def _():
        for w in weight_copies(be_ref[0]):
            w.start()

    e_now = be_ref[j]
    new_expert = jnp.logical_or(j == 0, be_ref[jnp.maximum(j - 1, 0)] != e_now)

    @pl.when(jnp.logical_and(j < nu, new_expert))
    def _():
        for w in weight_copies(e_now):
            w.wait()
        n_cc = 8
        for c in range(n_cc):
            cc = slice(c * (2 * D_FF // n_cc), (c + 1) * (2 * D_FF // n_cc))
            wgu_bf[:, cc] = wgu_raw[:, cc].astype(BF16)
        for c in range(n_cc // 2):
            cc = slice(c * (2 * D_MODEL // n_cc), (c + 1) * (2 * D_MODEL // n_cc))
            wd_bf[:, cc] = wd_raw[:, cc].astype(BF16)
        j_next = lax.while_loop(lambda t: jnp.logical_and(t < nu, be_ref[jnp.minimum(t, pl.num_programs(0) - 1)] == e_now),
                                lambda t: t + 1, j + 1)

        @pl.when(j_next < nu)
        def _():
            for w in weight_copies(be_ref[jnp.minimum(j_next, pl.num_programs(0) - 1)]):
                w.start()

    @pl.when(j < nu)
    def _():
        e_f = e_now.astype(F32)
        pr = MOE_BM // MOE_PARTS
        for p in range(MOE_PARTS):
            words = xs_ref[pl.ds(p * pr, pr)].reshape(pr, D_MODEL)
            packed = words[:, 0:PACK_W]
            meta = pltpu.bitcast(words[:, PACK_W:PACK_W + LANES], F32)
            xb = jnp.concatenate(
                [pltpu.bitcast(packed << 16, F32).astype(BF16),
                 pltpu.bitcast(packed & jnp.uint32(0xFFFF0000), F32).astype(BF16)], axis=1)
            gate = jnp.zeros((pr, 1), F32)
            for k in range(TOP_K):
                mk = meta[:, META_IDX + k:META_IDX + k + 1] == e_f
                gate = gate + jnp.where(mk, meta[:, META_GATE + k:META_GATE + k + 1], 0.0)
            gu = _dot(xb, wgu_bf[...]) + bgu_ref[0]
            glu = jnp.minimum(gu[:, :D_FF], SWIGLU_LIMIT)
            lin = jnp.clip(gu[:, D_FF:], -SWIGLU_LIMIT, SWIGLU_LIMIT)
            act = glu * jax.nn.sigmoid(SWIGLU_ALPHA * glu) * (lin + 1.0)
            y = (_dot(act.astype(BF16), wd_bf[...]) + bd_ref[0]) * gate
            ys_ref[pl.ds(p * pr, pr)] = y.reshape(pr, n_lt, LANES)

    @pl.when(j >= nu)
    def _():
        ys_ref[...] = jnp.zeros_like(ys_ref)


def _expert_rows(blk_exp, n_used, xs, wgu, bgu, wd, bd):
    n_rows = xs.shape[0]
    nb = n_rows // MOE_BM
    n_lt = D_MODEL // LANES
    row_blk = lambda j, be, nu: (jnp.minimum(j, nu[0] - 1), 0, 0)
    per_e = lambda j, be, nu: (be[j], 0, 0)
    return pl.pallas_call(
        _expert_rows_kernel,
        out_shape=jax.ShapeDtypeStruct((n_rows, n_lt, LANES), F32),
        grid_spec=pltpu.PrefetchScalarGridSpec(
            num_scalar_prefetch=2,
            grid=(nb,),
            in_specs=[pl.BlockSpec((MOE_BM, SUBLANES, LANES), row_blk),
                      pl.BlockSpec(memory_space=pl.ANY),
                      pl.BlockSpec((1, 1, 2 * D_FF), per_e),
                      pl.BlockSpec(memory_space=pl.ANY),
                      pl.BlockSpec((1, 1, D_MODEL), per_e)],
            out_specs=pl.BlockSpec((MOE_BM, n_lt, LANES), lambda j, be, nu: (j, 0, 0)),
            scratch_shapes=[pltpu.VMEM((D_MODEL, 2 * D_FF), F32), pltpu.VMEM((D_FF, D_MODEL), F32),
                            pltpu.VMEM((D_MODEL, 2 * D_FF), BF16), pltpu.VMEM((D_FF, D_MODEL), BF16),
                            pltpu.SemaphoreType.DMA((2,))]),
        compiler_params=pltpu.CompilerParams(dimension_semantics=("arbitrary",),
                                             vmem_limit_bytes=VMEM_LIMIT),
        name="experts",
    )(blk_exp, n_used, xs, wgu, bgu, wd, bd)


def _gather_combine_kernel(dest_ref, ys_ref, x1_ref, mod_ref, fg_ref, o_ref, buf, sem):
    i = pl.program_id(0)
    n = pl.num_programs(0)
    tc = GATHER_TILE
    slot = i % 2

    def issue(tile, sl):
        def per_tok(t, c):
            tok = tile * tc + t
            for k in range(TOP_K):
                d = dest_ref[k * (dest_ref.shape[0] // TOP_K) + tok]
                pltpu.make_async_copy(ys_ref.at[d], buf.at[sl, k * tc + t], sem.at[sl]).start(priority=k % 2)
            return c

        lax.fori_loop(0, tc, per_tok, 0, unroll=8)

    @pl.when(i == 0)
    def _():
        issue(0, 0)

    @pl.when(i + 1 < n)
    def _():
        issue(i + 1, 1 - slot)

    for _ in range(TOP_K):
        pltpu.make_async_copy(ys_ref.at[pl.ds(0, tc)], buf.at[slot, pl.ds(0, tc)], sem.at[slot]).wait()

    moe = ((buf[slot, pl.ds(0, tc)] + buf[slot, pl.ds(tc, tc)])
           + (buf[slot, pl.ds(2 * tc, tc)] + buf[slot, pl.ds(3 * tc, tc)])).reshape(tc, D_MODEL)
    m = mod_ref[0]
    x2 = x1_ref[...] + m[5:6] * moe
    o_ref[...] = _rms(x2) * fg_ref[...]


def _gather_combine(dest_flat, ys, x1, mod3, fg):
    n_tok = x1.shape[0]
    tc = GATHER_TILE
    per_b = SEQ // tc
    n_lt = D_MODEL // LANES
    return pl.pallas_call(
        _gather_combine_kernel,
        out_shape=jax.ShapeDtypeStruct((n_tok, D_MODEL), F32),
        grid_spec=pltpu.PrefetchScalarGridSpec(
            num_scalar_prefetch=1,
            grid=(n_tok // tc,),
            in_specs=[pl.BlockSpec(memory_space=pl.ANY),
                      pl.BlockSpec((tc, D_MODEL), lambda i, d: (i, 0)),
                      pl.BlockSpec((1, N_MOD, D_MODEL), lambda i, d: (i // per_b, 0, 0)),
                      pl.BlockSpec((1, D_MODEL), lambda i, d: (0, 0))],
            out_specs=pl.BlockSpec((tc, D_MODEL), lambda i, d: (i, 0)),
            scratch_shapes=[pltpu.VMEM((2, TOP_K * tc, n_lt, LANES), F32), pltpu.SemaphoreType.DMA((2,))]),
        compiler_params=pltpu.CompilerParams(dimension_semantics=("arbitrary",),
                                             vmem_limit_bytes=VMEM_LIMIT),
        name="combine",
    )(dest_flat, ys, x1, mod3, fg)


def _expansion_matrices(src0):
    r = (jnp.arange(2 * LANES) % LANES)[:, None]
    out64, out128 = [], []
    for d in range(2):
        l64 = jnp.arange(D_SSD)[None, :]
        l128 = jnp.arange(HEADS * LANES)[None, :]
        out64.append((l64 // HEAD_DIM == r - src0 - HEADS * d).astype(BF16))
        out128.append((l128 // LANES == r - src0 - HEADS * d).astype(BF16))
    return jnp.stack(out64), jnp.stack(out128)


def _pad_lanes(v):
    return jnp.pad(v, [(0, 0)] * (v.ndim - 1) + [(0, LANES - v.shape[-1])])


def kernel(x, c, ctx, c_ctx, w_mod, b_mod, norm1_g, w_in, ssd_conv_w, ssd_conv_b, ssd_dt_bias, ssd_a_log,
           ssd_d, ssd_norm_g, sc_conv_w, w_out, norm2_g, w_router, b_router, w_gate_up, b_gate_up, w_down,
           b_down, final_g):
    bsz = x.shape[0]
    n_tok = bsz * SEQ
    n_assign = n_tok * TOP_K
    n_blocks = n_assign // MOE_BM + N_EXPERTS
    n_rows = n_blocks * MOE_BM
    li = 0

    cvec = jnp.concatenate([c, c_ctx[None, :], jnp.zeros((7, D_MODEL), F32)], axis=0)
    mod3 = _mod(cvec, w_mod[li], b_mod[li][None, :]).reshape(bsz + 8, N_MOD, D_MODEL)

    w = w_in[li]
    wz = w[:, Z0:X0].astype(BF16)
    wxbc = w[:, X0:DT0].astype(BF16)
    wdt = _pad_lanes(w[:, DT0:SC0]).astype(BF16)
    wb = w[:, SC0:SC0 + D_SC].astype(BF16)
    wc = w[:, SC0 + D_SC:SC0 + 2 * D_SC].astype(BF16)
    wu = w[:, SC0 + 2 * D_SC:].astype(BF16)
    g1 = norm1_g[li][None, :]
    cw = ssd_conv_w[li]
    cb = ssd_conv_b[li][None, :]
    dtb = _pad_lanes(ssd_dt_bias[li].reshape(1, 2 * HEADS))
    alog = _pad_lanes(ssd_a_log[li].reshape(1, 2 * HEADS))
    e64_ctx, _ = _expansion_matrices(0)

    h0 = _ctx_states(ctx, mod3, g1, wxbc[:, :XB_W], wdt, cw[:, :XB_W], cb[:, :XB_W], dtb, alog, e64_ctx)

    rep = lambda a: _pad_lanes(jnp.tile(a[..., :2 * HEADS], (1, DT_COPIES)))
    e64, e128 = _expansion_matrices(DT_DA0)
    x2 = x.reshape(n_tok, D_MODEL)
    z, xbc, dtp, scb, v = _inproj(x2, mod3, g1, wz, wxbc, rep(wdt), rep(dtb), rep(alog), wb, wc, wu)

    dsk = jnp.repeat(ssd_d[li], HEAD_DIM)[None, :]
    yssd = _ssd(xbc.reshape(bsz, SEQ, XBC_W), z.reshape(bsz, SEQ, D_SSD), dtp.reshape(bsz, SEQ, LANES), h0,
                cw, cb, dsk, ssd_norm_g[li][None, :], e64, e128)

    wo = w_out[li].astype(BF16)
    g2 = norm2_g[li][None, :]
    wr = w_router[li].T
    wr_hi = wr.astype(BF16)
    wr_lo = (wr - wr_hi.astype(F32)).astype(BF16)
    x1, lgt = _outproj(x2, yssd.reshape(n_tok, D_SSD), scb, v, mod3, sc_conv_w[li], wo[:D_SSD], wo[D_SSD:],
                       g2, jnp.concatenate([wr_hi, wr_hi, wr_lo], axis=1), b_router[li][:, None])

    dest_t, gate_t, idx_t, meta = _route(lgt, n_blocks)
    dest_flat = dest_t.reshape(n_assign)
    cnt = meta[0, :N_EXPERTS]
    start = meta[1, :N_EXPERTS]
    blk_exp = meta[2, :n_blocks]
    n_used = meta[3, :1]

    meta_rows = lax.bitcast_convert_type(_pad_lanes(jnp.concatenate(
        [idx_t.T.astype(F32), gate_t.T], axis=1)), jnp.uint32)
    pad_meta = lax.bitcast_convert_type(_pad_lanes(jnp.concatenate(
        [jnp.full((MOE_BM, TOP_K), -1.0, F32), jnp.zeros((MOE_BM, TOP_K), F32)], axis=1)), jnp.uint32)
    zsrc = jnp.concatenate([jnp.zeros((MOE_BM, PACK_W), jnp.uint32), pad_meta,
                            jnp.zeros((MOE_BM, D_MODEL - PACK_W - LANES), jnp.uint32)],
                           axis=1).reshape(MOE_BM, SUBLANES, LANES)

    xs = _dispatch(dest_flat, cnt, start, n_used, x1, meta_rows, mod3, g2, zsrc, n_rows)
    ys = _expert_rows(blk_exp, n_used, xs, w_gate_up[li], b_gate_up[li][:, None, :],
                      w_down[li], b_down[li][:, None, :])
    out = _gather_combine(dest_flat, ys, x1, mod3, final_g[None, :])
    return out.reshape(bsz, SEQ, D_MODEL)
```

```python
import functools

import jax
import jax.numpy as jnp
from jax import lax
from jax.experimental import pallas as pl
from jax.experimental.pallas import tpu as pltpu

F32 = jnp.float32
BF16 = jnp.bfloat16
HIGHEST = lax.Precision.HIGHEST

D_MODEL = 1024
SEQ = 2048
CTX_LEN = 256
GRID_W = 64
D_SSD = 1024
D_SC = 1024
HEAD_DIM = 64
HEADS = 16
GROUPS = 2
STATE = 128
CHUNK = 128
N_EXPERTS = 32
TOP_K = 4
D_FF = 1024
SWIGLU_LIMIT = 7.0
SWIGLU_ALPHA = 1.702
NORM_EPS = 1e-6
N_MOD = 6
XBC_W = D_SSD + 2 * GROUPS * STATE
XB_W = D_SSD + GROUPS * STATE
LANES = 128

Z0 = 0
X0 = Z0 + D_SSD
B0 = X0 + D_SSD
C0 = B0 + GROUPS * STATE
DT0 = C0 + GROUPS * STATE
SC0 = DT0 + 2 * HEADS

TOK_TILE = 512
MOE_BM = 512
MOE_PARTS = 2
RT_TILE = 512
CTX_BATCH = 4
DISP_TILE = 512
GATHER_TILE = 256
DT_COPIES = 3
DT_LOG0 = 2 * HEADS
DT_DA0 = 4 * HEADS
SUBLANES = 8
PACK_W = D_MODEL // 2
META_IDX = 0
META_GATE = TOP_K
VMEM_LIMIT = 56 * 1024 * 1024


def _silu(v):
    return v * jax.nn.sigmoid(v)


def _softplus(v):
    return jnp.maximum(v, 0.0) + jnp.log1p(jnp.exp(-jnp.abs(v)))


def _rms(v):
    return v * lax.rsqrt(jnp.mean(v * v, axis=-1, keepdims=True) + NORM_EPS)


def _dot(a, b):
    return jnp.dot(a, b, preferred_element_type=F32)


def _expand2(v, e2):
    hi = v.astype(BF16)
    lo = (v - hi.astype(F32)).astype(BF16)
    return _dot(jnp.concatenate([hi, lo], axis=1), e2)


def _mod_kernel(c_ref, w_ref, b_ref, o_ref):
    o_ref[...] = jnp.dot(_silu(c_ref[...]), w_ref[0], precision=HIGHEST,
                         preferred_element_type=F32) + b_ref[...]


def _mod(cvec, w_mod, b_mod, layer):
    rows = cvec.shape[0]
    n = w_mod.shape[2]
    tn = 1536
    return pl.pallas_call(
        _mod_kernel,
        out_shape=jax.ShapeDtypeStruct((rows, n), F32),
        grid=(n // tn,),
        in_specs=[pl.BlockSpec((rows, D_MODEL), lambda j: (0, 0)),
                  pl.BlockSpec((1, D_MODEL, tn), lambda j: (layer, 0, j)),
                  pl.BlockSpec((1, tn), lambda j: (0, j))],
        out_specs=pl.BlockSpec((rows, tn), lambda j: (0, j)),
        compiler_params=pltpu.CompilerParams(dimension_semantics=("arbitrary",),
                                             vmem_limit_bytes=VMEM_LIMIT),
        name="mod",
    )(cvec, w_mod, b_mod)


def _ctx_kernel(ctx_ref, mod_ref, g1_ref, wxb_ref, wdt_ref, cw_ref, cb_ref, dtb_ref, alog_ref, e64_ref,
                h0_ref):
    L = CTX_LEN
    nb = ctx_ref.shape[0]
    m = mod_ref[0]
    hc = _rms(ctx_ref[...].reshape(nb * L, D_MODEL)) * g1_ref[...] * (1.0 + m[1:2]) + m[0:1]
    hb = hc.astype(BF16)
    pxb = _dot(hb, wxb_ref[...])
    dtr = _dot(hb, wdt_ref[...])
    rowl = lax.broadcasted_iota(jnp.int32, (nb * L, XB_W), 0) & (L - 1)
    dn = jnp.where(rowl == 0, 0.0, pltpu.roll(pxb, 1, 0))
    up = jnp.where(rowl == L - 1, 0.0, pltpu.roll(pxb, nb * L - 1, 0))
    cw = cw_ref[...]
    xb = _silu(cw[0:1] * dn + cw[1:2] * pxb + cw[2:3] * up + cb_ref[...])
    dt_all = _softplus(dtr + dtb_ref[...])
    da_all = dt_all * (-jnp.exp(alog_ref[...]))
    ri = lax.broadcasted_iota(jnp.int32, (L, L), 0)
    ci = lax.broadcasted_iota(jnp.int32, (L, L), 1)
    for bi in range(nb):
        rs = slice(bi * L, (bi + 1) * L)
        xs = xb[rs, :D_SSD]
        bm = xb[rs, D_SSD:].astype(BF16)
        dt = dt_all[rs]
        da = da_all[rs]
        for d in range(2):
            tri = (ci <= ri) if d == 0 else (ci >= ri)
            cum = jnp.dot(tri.astype(F32), da, precision=HIGHEST, preferred_element_type=F32)
            last = cum[L - 1:L] if d == 0 else cum[0:1]
            w_e = _expand2(jnp.exp(last - cum) * dt, e64_ref[d])
            xw = (xs * w_e).astype(BF16)
            for g in range(GROUPS):
                gw = D_SSD // GROUPS
                st = lax.dot_general(bm[:, g * STATE:(g + 1) * STATE], xw[:, g * gw:(g + 1) * gw],
                                     (((0,), (0,)), ((), ())), preferred_element_type=F32)
                h0_ref[bi, d, :, g * gw:(g + 1) * gw] = st


def _ctx_states(ctx, mod3, g1, wxb, wdt, cw, cb, dtb, alog, e64):
    bsz = ctx.shape[0]
    mod_row = bsz
    nb = CTX_BATCH if bsz % CTX_BATCH == 0 else 1
    const = lambda *shape: pl.BlockSpec(shape, lambda b: (0,) * len(shape))
    return pl.pallas_call(
        _ctx_kernel,
        out_shape=jax.ShapeDtypeStruct((bsz, 2, STATE, D_SSD), F32),
        grid=(bsz // nb,),
        in_specs=[pl.BlockSpec((nb, CTX_LEN, D_MODEL), lambda b: (b, 0, 0)),
                  pl.BlockSpec((1, N_MOD, D_MODEL), lambda b: (mod_row, 0, 0)),
                  const(1, D_MODEL), const(D_MODEL, XB_W), const(D_MODEL, LANES),
                  const(3, XB_W), const(1, XB_W), const(1, LANES), const(1, LANES),
                  const(2, 2 * LANES, D_SSD)],
        out_specs=pl.BlockSpec((nb, 2, STATE, D_SSD), lambda b: (b, 0, 0, 0)),
        compiler_params=pltpu.CompilerParams(dimension_semantics=("arbitrary",),
                                             vmem_limit_bytes=VMEM_LIMIT),
        name="ctx_states",
    )(ctx, mod3, g1, wxb, wdt, cw, cb, dtb, alog, e64)


def _inproj_kernel(x_ref, mod_ref, g1_ref, wz_ref, wxbc_ref, wdt_ref, dtb_ref, alog_ref, wb_ref, wc_ref, wu_ref,
                   z_ref, xbc_ref, dt_ref, scb_ref, v_ref):
    m = mod_ref[0]
    hx = _rms(x_ref[...]) * g1_ref[...] * (1.0 + m[1:2]) + m[0:1]
    hb = hx.astype(BF16)
    z_ref[...] = _dot(hb, wz_ref[...]).astype(BF16)
    xbc_ref[...] = _dot(hb, wxbc_ref[...]).astype(BF16)
    dt = _softplus(_dot(hb, wdt_ref[...]) + dtb_ref[...])
    lane = lax.broadcasted_iota(jnp.int32, dt.shape, 1)
    dt_ref[...] = jnp.where(lane < DT_LOG0, dt,
                            jnp.where(lane < DT_DA0, jnp.log(dt), dt * (-jnp.exp(alog_ref[...]))))
    scb_ref[...] = _dot(hb, wb_ref[...]).astype(BF16)
    v_ref[...] = (_dot(hb, wc_ref[...]) * _dot(hb, wu_ref[...])).astype(BF16)


def _inproj(x2, mod3, g1, wz, wxbc, wdt, dtb, alog, wb, wc, wu):
    t = x2.shape[0]
    tm = TOK_TILE
    per_b = SEQ // tm
    const = lambda *shape: pl.BlockSpec(shape, lambda i: (0,) * len(shape))
    tile = lambda w: pl.BlockSpec((tm, w), lambda i: (i, 0))
    return pl.pallas_call(
        _inproj_kernel,
        out_shape=(jax.ShapeDtypeStruct((t, D_SSD), BF16), jax.ShapeDtypeStruct((t, XBC_W), BF16),
                   jax.ShapeDtypeStruct((t, LANES), F32), jax.ShapeDtypeStruct((t, D_SC), BF16),
                   jax.ShapeDtypeStruct((t, D_SC), BF16)),
        grid=(t // tm,),
        in_specs=[tile(D_MODEL),
                  pl.BlockSpec((1, N_MOD, D_MODEL), lambda i: (i // per_b, 0, 0)),
                  const(1, D_MODEL), const(D_MODEL, D_SSD), const(D_MODEL, XBC_W), const(D_MODEL, LANES),
                  const(1, LANES), const(1, LANES),
                  const(D_MODEL, D_SC), const(D_MODEL, D_SC), const(D_MODEL, D_SC)],
        out_specs=(tile(D_SSD), tile(XBC_W), tile(LANES), tile(D_SC), tile(D_SC)),
        compiler_params=pltpu.CompilerParams(dimension_semantics=("arbitrary",),
                                             vmem_limit_bytes=VMEM_LIMIT),
        name="inproj",
    )(x2, mod3, g1, wz, wxbc, wdt, dtb, alog, wb, wc, wu)


def _ssd_kernel(xbc_ref, z_ref, dt_ref, h0_ref, cw_ref, cb_ref, dsk_ref, g_ref,
                e64_ref, e128_ref, o_ref, xc_ref, y_ref, s_ref):
    Q = CHUNK
    nck = SEQ // Q
    gw = D_SSD // GROUPS

    rowi = lax.broadcasted_iota(jnp.int32, (SUBLANES, XBC_W), 0)

    def conv_body(c, carry):
        r0 = pl.multiple_of(c * Q, Q)
        main = xbc_ref[0, pl.ds(r0, Q), :].astype(F32)
        pstart = pl.multiple_of(jnp.maximum(r0 - 16, 0), 16)
        nstart = pl.multiple_of(jnp.minimum(r0 + Q, SEQ - 16), 16)
        prev = xbc_ref[0, pl.ds(pstart, 16), :].astype(F32)[15:16]
        nxt = xbc_ref[0, pl.ds(nstart, 16), :].astype(F32)[0:1]
        prev = jnp.where(c > 0, prev, 0.0)
        nxt = jnp.where(c < nck - 1, nxt, 0.0)
        dn = pltpu.roll(main, 1, 0)
        up = pltpu.roll(main, Q - 1, 0)
        dn = jnp.concatenate([jnp.where(rowi == 0, prev, dn[0:SUBLANES]), dn[SUBLANES:]], axis=0)
        up = jnp.concatenate([up[:Q - SUBLANES], jnp.where(rowi == SUBLANES - 1, nxt, up[Q - SUBLANES:])], axis=0)
        cw = cw_ref[...]
        conv = cw[0:1] * dn + cw[1:2] * main + cw[2:3] * up + cb_ref[...]
        xc_ref[pl.ds(r0, Q), :] = _silu(conv).astype(BF16)
        return carry

    lax.fori_loop(0, nck, conv_body, 0)

    ri = lax.broadcasted_iota(jnp.int32, (Q, Q), 0)
    ci = lax.broadcasted_iota(jnp.int32, (Q, Q), 1)
    lane = lax.broadcasted_iota(jnp.int32, (Q, LANES), 1)
    da_lanes = jnp.logical_and(lane >= DT_DA0, lane < DT_DA0 + 2 * HEADS)

    def chunk(c, d, first):
        r0 = pl.multiple_of(c * Q, Q)
        rows = pl.ds(r0, Q)
        xs_b = xc_ref[rows, 0:D_SSD]
        xs = xs_b.astype(F32)
        bm = xc_ref[rows, D_SSD:D_SSD + GROUPS * STATE]
        cm = xc_ref[rows, D_SSD + GROUPS * STATE:XBC_W]
        dtp = dt_ref[0, rows, :]
        da = jnp.where(da_lanes, dtp, 0.0)
        dt = pltpu.roll(dtp, DT_DA0, 1)
        log_dt = pltpu.roll(dtp, DT_DA0 - DT_LOG0, 1)
        tri = (ci <= ri) if d == 0 else (ci >= ri)
        p0 = da.astype(BF16)
        r1 = da - p0.astype(F32)
        p1 = r1.astype(BF16)
        p2 = (r1 - p1.astype(F32)).astype(BF16)
        tri_b = jnp.where(tri, 1.0, 0.0).astype(BF16)
        cum = _dot(jnp.concatenate([tri_b, tri_b, tri_b], axis=1),
                   jnp.concatenate([p0, p1, p2], axis=0))
        sub_t = (cum - log_dt).T
        last = cum[Q - 1:Q] if d == 0 else cum[0:1]
        ecum_e = _expand2(jnp.exp(cum), e64_ref[d])
        w_e = _expand2(jnp.where(da_lanes, jnp.exp(last - cum) * dt, 0.0), e64_ref[d])
        colb = _expand2(cum, e128_ref[d])
        decay_e = ecum_e[Q - 1:Q] if d == 0 else ecum_e[0:1]

        gmat = [lax.dot_general(cm[:, g * STATE:(g + 1) * STATE], bm[:, g * STATE:(g + 1) * STATE],
                                (((1,), (1,)), ((), ())), preferred_element_type=F32)
                for g in range(GROUPS)]
        zero_b = jnp.zeros((Q, LANES), BF16)
        y_parts = []
        for p in range(HEADS // 2):
            g = (2 * p) // (HEADS // GROUPS)
            ms = []
            for hh in (2 * p, 2 * p + 1):
                src = DT_DA0 + HEADS * d + hh
                seg = colb[:, hh * LANES:(hh + 1) * LANES] - sub_t[src:src + 1, :]
                ms.append((jnp.where(tri, jnp.exp(seg), 0.0) * gmat[g]).astype(BF16))
            mcat = jnp.concatenate(ms, axis=1)
            xp = xs_b[:, p * LANES:(p + 1) * LANES]
            rhs = jnp.concatenate([jnp.where(lane < HEAD_DIM, xp, zero_b),
                                   jnp.where(lane >= HEAD_DIM, xp, zero_b)], axis=0)
            y_parts.append(_dot(mcat, rhs))
        y_diag = jnp.concatenate(y_parts, axis=1)

        s_old = s_ref[d]
        s_bf = s_old.astype(BF16)
        y_off = jnp.concatenate(
            [_dot(cm[:, g * STATE:(g + 1) * STATE], s_bf[:, g * gw:(g + 1) * gw]) for g in range(GROUPS)],
            axis=1)
        y = y_diag + y_off * ecum_e

        xw = (xs * w_e).astype(BF16)
        upd = jnp.concatenate(
            [lax.dot_general(bm[:, g * STATE:(g + 1) * STATE], xw[:, g * gw:(g + 1) * gw],
                             (((0,), (0,)), ((), ())), preferred_element_type=F32) for g in range(GROUPS)],
            axis=1)
        s_ref[d] = s_old * decay_e + upd

        if first:
            y_ref[rows, :] = y + dsk_ref[...] * xs
        else:
            tot = y_ref[rows, :] + y
            zz = z_ref[0, rows, :].astype(F32)
            gz = tot * _silu(zz)
            outs = []
            for g in range(GROUPS):
                gg = gz[:, g * gw:(g + 1) * gw]
                outs.append(gg * lax.rsqrt(jnp.mean(gg * gg, axis=-1, keepdims=True) + NORM_EPS))
            o_ref[0, rows, :] = (jnp.concatenate(outs, axis=1) * g_ref[...]).astype(BF16)

    s_ref[...] = h0_ref[0]

    def first_half(i, carry):
        chunk(i, 0, True)
        chunk(nck - 1 - i, 1, True)
        return carry

    def second_half(i, carry):
        chunk(i, 0, False)
        chunk(nck - 1 - i, 1, False)
        return carry

    lax.fori_loop(0, nck // 2, first_half, 0)
    lax.fori_loop(nck // 2, nck, second_half, 0)


def _ssd(xbc3, z3, dt3, h0, cw, cb, dsk, g, e64, e128):
    bsz = xbc3.shape[0]
    const = lambda *shape: pl.BlockSpec(shape, lambda b: (0,) * len(shape))
    seq = lambda w: pl.BlockSpec((1, SEQ, w), lambda b: (b, 0, 0))
    return pl.pallas_call(
        _ssd_kernel,
        out_shape=jax.ShapeDtypeStruct((bsz, SEQ, D_SSD), BF16),
        grid=(bsz,),
        in_specs=[seq(XBC_W), seq(D_SSD), seq(LANES),
                  pl.BlockSpec((1, 2, STATE, D_SSD), lambda b: (b, 0, 0, 0)),
                  const(3, XBC_W), const(1, XBC_W),
                  const(1, D_SSD), const(1, D_SSD), const(2, 2 * LANES, D_SSD),
                  const(2, 2 * LANES, HEADS * LANES)],
        out_specs=seq(D_SSD),
        scratch_shapes=[pltpu.VMEM((SEQ, XBC_W), BF16), pltpu.VMEM((SEQ, D_SSD), F32),
                        pltpu.VMEM((2, STATE, D_SSD), F32)],
        compiler_params=pltpu.CompilerParams(dimension_semantics=("arbitrary",),
                                             vmem_limit_bytes=VMEM_LIMIT),
        name="ssd",
    )(xbc3, z3, dt3, h0, cw, cb, dsk, g, e64, e128)


def _outproj_kernel(x_ref, yssd_ref, scb_ref, v_ref, vp_ref, vn_ref, mod_ref, scw_ref, wo1_ref, wo2_ref,
                    g2_ref, wrt_ref, br_ref, x1_ref, lg_ref):
    tm = TOK_TILE
    per_b = SEQ // tm
    i = pl.program_id(0)
    first = (i % per_b) == 0
    last = (i % per_b) == per_b - 1
    m = mod_ref[0]
    v = v_ref[...].astype(F32)
    vp = jnp.where(first, 0.0, vp_ref[...].astype(F32))
    vn = jnp.where(last, 0.0, vn_ref[...].astype(F32))
    dn = jnp.concatenate([vp, v[:tm - GRID_W]], axis=0)
    up = jnp.concatenate([v[GRID_W:], vn], axis=0)
    scw = scw_ref[...]
    ysc = scb_ref[...].astype(F32) * (scw[0:1] * dn + scw[1:2] * v + scw[2:3] * up)
    out = _dot(yssd_ref[...], wo1_ref[...]) + _dot(ysc.astype(BF16), wo2_ref[...])
    x1 = x_ref[...] + m[2:3] * out
    x1_ref[...] = x1
    h2 = _rms(x1) * g2_ref[...] * (1.0 + m[4:5]) + m[3:4]
    h_hi = h2.astype(BF16)
    h_lo = (h2 - h_hi.astype(F32)).astype(BF16)
    lg_ref[...] = lax.dot_general(wrt_ref[...], jnp.concatenate([h_hi, h_lo, h_hi], axis=1),
                                  (((1,), (1,)), ((), ())), preferred_element_type=F32) + br_ref[...]


def _outproj(x2, yssd, scb, v, mod3, scw, wo1, wo2, g2, wrt, br):
    t = x2.shape[0]
    tm = TOK_TILE
    per_b = SEQ // tm
    r = tm // GRID_W
    nrow = t // GRID_W
    const = lambda *shape: pl.BlockSpec(shape, lambda i: (0,) * len(shape))
    tile = lambda w: pl.BlockSpec((tm, w), lambda i: (i, 0))
    return pl.pallas_call(
        _outproj_kernel,
        out_shape=(jax.ShapeDtypeStruct((t, D_MODEL), F32), jax.ShapeDtypeStruct((N_EXPERTS, t), F32)),
        grid=(t // tm,),
        in_specs=[tile(D_MODEL), tile(D_SSD), tile(D_SC), tile(D_SC),
                  pl.BlockSpec((GRID_W, D_SC), lambda i: (jnp.maximum(i * r - 1, 0), 0)),
                  pl.BlockSpec((GRID_W, D_SC), lambda i: (jnp.minimum((i + 1) * r, nrow - 1), 0)),
                  pl.BlockSpec((1, N_MOD, D_MODEL), lambda i: (i // per_b, 0, 0)),
                  const(3, D_SC), const(D_SSD, D_MODEL), const(D_SC, D_MODEL), const(1, D_MODEL),
                  const(N_EXPERTS, 3 * D_MODEL), const(N_EXPERTS, 1)],
        out_specs=(tile(D_MODEL), pl.BlockSpec((N_EXPERTS, tm), lambda i: (0, i))),
        compiler_params=pltpu.CompilerParams(dimension_semantics=("arbitrary",),
                                             vmem_limit_bytes=VMEM_LIMIT),
        name="outproj",
    )(x2, yssd, scb, v, v, v, mod3, scw, wo1, wo2, g2, wrt, br)


def _route_kernel(lg_ref, dest_ref, gate_ref, idx_ref, meta_ref, rank_ref, carry_ref, *, n_tok, n_blocks):
    tt = RT_TILE
    ne = N_EXPERTS
    eio = lax.broadcasted_iota(jnp.int32, (ne, tt), 0)
    si = lax.broadcasted_iota(jnp.int32, (tt, tt), 0)
    ti = lax.broadcasted_iota(jnp.int32, (tt, tt), 1)
    before = (si < ti).astype(BF16)
    carry_ref[...] = jnp.zeros_like(carry_ref)

    def tile_body(j, c):
        t0 = pl.multiple_of(j * tt, tt)
        l = lg_ref[:, pl.ds(t0, tt)]
        onehot = jnp.zeros((ne, tt), F32)
        tops, sels = [], []
        for _ in range(TOP_K):
            mx = jnp.max(l, axis=0, keepdims=True)
            idx = jnp.min(jnp.where(l == mx, eio, ne), axis=0, keepdims=True)
            sel = eio == idx
            l = jnp.where(sel, -jnp.inf, l)
            onehot = onehot + sel.astype(F32)
            tops.append(mx)
            sels.append(sel)
            idx_ref[pl.ds(len(tops) - 1, 1), pl.ds(t0, tt)] = idx
        ex = [jnp.exp(tv - tops[0]) for tv in tops]
        den = ex[0] + ex[1] + ex[2] + ex[3]
        prefix = _dot(onehot.astype(BF16), before) + carry_ref[:, 0:1]
        for k in range(TOP_K):
            gate_ref[pl.ds(k, 1), pl.ds(t0, tt)] = ex[k] / den
            rk = jnp.sum(jnp.where(sels[k], prefix, 0.0), axis=0, keepdims=True)
            rank_ref[pl.ds(k, 1), pl.ds(t0, tt)] = rk.astype(jnp.int32)
        carry_ref[...] = carry_ref[...] + jnp.sum(onehot, axis=1, keepdims=True)
        return c

    lax.fori_loop(0, n_tok // tt, tile_body, 0)

    counts = carry_ref[...]
    padded = jnp.floor((counts + (MOE_BM - 1)) * (1.0 / MOE_BM)) * MOE_BM
    er = lax.broadcasted_iota(jnp.int32, (ne, ne), 0)
    ec = lax.broadcasted_iota(jnp.int32, (ne, ne), 1)
    pad_start = jnp.dot((ec < er).astype(F32), padded, precision=HIGHEST, preferred_element_type=F32)
    pad_end = pad_start + padded

    def dest_body(j, c):
        t0 = pl.multiple_of(j * tt, tt)
        for k in range(TOP_K):
            idx = idx_ref[pl.ds(k, 1), pl.ds(t0, tt)]
            base = jnp.sum(jnp.where(eio == idx, pad_start[:, 0:1], 0.0), axis=0, keepdims=True)
            dest_ref[pl.ds(k, 1), pl.ds(t0, tt)] = base.astype(jnp.int32) + rank_ref[pl.ds(k, 1), pl.ds(t0, tt)]
        return c

    lax.fori_loop(0, n_tok // tt, dest_body, 0)

    width = meta_ref.shape[1]
    sub = lax.broadcasted_iota(jnp.int32, (ne, width), 0)
    lan = lax.broadcasted_iota(jnp.int32, (ne, width), 1)
    diag = sub == lan
    cnt_row = jnp.sum(jnp.where(diag, counts[:, 0:1], 0.0), axis=0, keepdims=True)
    start_row = jnp.sum(jnp.where(diag, pad_start[:, 0:1], 0.0), axis=0, keepdims=True)
    blk_start = (lan * MOE_BM).astype(F32)
    blk_exp = jnp.sum((pad_end[:, 0:1] <= blk_start).astype(F32), axis=0, keepdims=True)
    blk_exp = jnp.minimum(blk_exp, float(ne - 1))
    used = jnp.sum(padded[:, 0:1], axis=0, keepdims=True) * (1.0 / MOE_BM)
    meta_ref[0:1, :] = cnt_row.astype(jnp.int32)
    meta_ref[1:2, :] = start_row.astype(jnp.int32)
    meta_ref[2:3, :] = blk_exp.astype(jnp.int32)
    meta_ref[3:4, :] = jnp.broadcast_to(used, (1, width)).astype(jnp.int32)
    meta_ref[4:8, :] = jnp.zeros((4, width), jnp.int32)


def _route(lgt, n_blocks):
    ne, n_tok = lgt.shape
    width = -(-n_blocks // LANES) * LANES
    full = lambda *shape: pl.BlockSpec(shape, lambda: (0,) * len(shape))
    return pl.pallas_call(
        functools.partial(_route_kernel, n_tok=n_tok, n_blocks=n_blocks),
        out_shape=(jax.ShapeDtypeStruct((TOP_K, n_tok), jnp.int32),
                   jax.ShapeDtypeStruct((TOP_K, n_tok), F32),
                   jax.ShapeDtypeStruct((TOP_K, n_tok), jnp.int32),
                   jax.ShapeDtypeStruct((8, width), jnp.int32)),
        in_specs=[full(ne, n_tok)],
        out_specs=(full(TOP_K, n_tok), full(TOP_K, n_tok), full(TOP_K, n_tok), full(8, width)),
        scratch_shapes=[pltpu.VMEM((TOP_K, n_tok), jnp.int32), pltpu.VMEM((ne, LANES), F32)],
        compiler_params=pltpu.CompilerParams(vmem_limit_bytes=VMEM_LIMIT),
        name="route",
    )(lgt)


def _dispatch_kernel(dest_ref, cnt_ref, start_ref, nu_ref, x1_ref, meta_ref, mod_ref, g2_ref, zsrc_ref, xs_ref,
                     hbuf, sem, zsem):
    i = pl.program_id(0)
    n = pl.num_programs(0)
    tl = DISP_TILE
    slot = i % 2
    nb = xs_ref.shape[0] // MOE_BM

    def zero_block(b):
        return pltpu.make_async_copy(zsrc_ref, xs_ref.at[pl.ds(b * MOE_BM, MOE_BM)], zsem)

    @pl.when(i == 0)
    def _():
        def start_e(e, c):
            @pl.when(cnt_ref[e] > 0)
            def _():
                zero_block((start_ref[e] + cnt_ref[e] - 1) // MOE_BM).start()
            return c

        def wait_e(e, c):
            @pl.when(cnt_ref[e] > 0)
            def _():
                zero_block(0).wait()
            return c

        def start_t(b, c):
            zero_block(b).start()
            return c

        def wait_t(b, c):
            zero_block(0).wait()
            return c

        lax.fori_loop(0, N_EXPERTS, start_e, 0)
        lax.fori_loop(nu_ref[0], nb, start_t, 0)
        lax.fori_loop(0, N_EXPERTS, wait_e, 0)
        lax.fori_loop(nu_ref[0], nb, wait_t, 0)

    m = mod_ref[0]
    h2 = _rms(x1_ref[...]) * g2_ref[...] * (1.0 + m[4:5]) + m[3:4]
    lo = pltpu.bitcast(h2[:, :PACK_W].astype(BF16).astype(F32), jnp.uint32) >> 16
    hi = pltpu.bitcast(h2[:, PACK_W:].astype(BF16).astype(F32), jnp.uint32) & jnp.uint32(0xFFFF0000)
    row = jnp.concatenate([lo | hi, meta_ref[...], jnp.zeros((tl, D_MODEL - PACK_W - LANES), jnp.uint32)], axis=1)
    hbuf[slot] = row.reshape(tl, SUBLANES, LANES)

    def per_tok(t, c):
        tok = i * tl + t
        for k in range(TOP_K):
            d = dest_ref[k * (dest_ref.shape[0] // TOP_K) + tok]
            pltpu.make_async_copy(hbuf.at[slot, t], xs_ref.at[d], sem.at[slot]).start(priority=k % 2)
        return c

    lax.fori_loop(0, tl, per_tok, 0, unroll=8)

    def wait_slot(sl):
        for _ in range(TOP_K):
            pltpu.make_async_copy(hbuf.at[sl], xs_ref.at[pl.ds(0, tl)], sem.at[sl]).wait()

    @pl.when(i > 0)
    def _():
        wait_slot(1 - slot)

    @pl.when(i == n - 1)
    def _():
        wait_slot(slot)


def _dispatch(dest_flat, cnt, start, n_used, x1, meta_rows, mod3, g2, zsrc, n_rows):
    n_tok = x1.shape[0]
    tl = DISP_TILE
    per_b = SEQ // tl
    return pl.pallas_call(
        _dispatch_kernel,
        out_shape=jax.ShapeDtypeStruct((n_rows, SUBLANES, LANES), jnp.uint32),
        grid_spec=pltpu.PrefetchScalarGridSpec(
            num_scalar_prefetch=4,
            grid=(n_tok // tl,),
            in_specs=[pl.BlockSpec((tl, D_MODEL), lambda i, *_: (i, 0)),
                      pl.BlockSpec((tl, LANES), lambda i, *_: (i, 0)),
                      pl.BlockSpec((1, N_MOD, D_MODEL), lambda i, *_: (i // per_b, 0, 0)),
                      pl.BlockSpec((1, D_MODEL), lambda i, *_: (0, 0)),
                      pl.BlockSpec((MOE_BM, SUBLANES, LANES), lambda i, *_: (0, 0, 0))],
            out_specs=pl.BlockSpec(memory_space=pl.ANY),
            scratch_shapes=[pltpu.VMEM((2, tl, SUBLANES, LANES), jnp.uint32),
                            pltpu.SemaphoreType.DMA((2,)), pltpu.SemaphoreType.DMA]),
        compiler_params=pltpu.CompilerParams(dimension_semantics=("arbitrary",),
                                             vmem_limit_bytes=VMEM_LIMIT),
        name="dispatch",
    )(dest_flat, cnt, start, n_used, x1, meta_rows, mod3, g2, zsrc)


def _expert_rows_kernel(be_ref, nu_ref, xs_ref, wgu_hbm, bgu_ref, wd_hbm, bd_ref, ys_ref,
                        wgu_raw, wd_raw, wgu_bf, wd_bf, w_sem):
    j = pl.program_id(0)
    nu = nu_ref[0]
    n_lt = D_MODEL // LANES

    def weight_copies(e):
        return (pltpu.make_async_copy(wgu_hbm.at[e], wgu_raw, w_sem.at[0]),
                pltpu.make_async_copy(wd_hbm.at[e], wd_raw, w_sem.at[1]))

    @pl.when(j == 0)
    def _():
        for w in weight_copies(be_ref[0]):
            w.start()

    e_now = be_ref[j]
    new_expert = jnp.logical_or(j == 0, be_ref[jnp.maximum(j - 1, 0)] != e_now)

    @pl.when(jnp.logical_and(j < nu, new_expert))
    def _():
        for w in weight_copies(e_now):
            w.wait()
        n_cc = 8
        for c in range(n_cc):
            cc = slice(c * (2 * D_FF // n_cc), (c + 1) * (2 * D_FF // n_cc))
            wgu_bf[:, cc] = wgu_raw[:, cc].astype(BF16)
        for c in range(n_cc // 2):
            cc = slice(c * (2 * D_MODEL // n_cc), (c + 1) * (2 * D_MODEL // n_cc))
            wd_bf[:, cc] = wd_raw[:, cc].astype(BF16)
        j_next = lax.while_loop(lambda t: jnp.logical_and(t < nu, be_ref[jnp.minimum(t, pl.num_programs(0) - 1)] == e_now),
                                lambda t: t + 1, j + 1)

        @pl.when(j_next < nu)
        def _():
            for w in weight_copies(be_ref[jnp.minimum(j_next, pl.num_programs(0) - 1)]):
                w.start()

    @pl.when(j < nu)
    def _():
        e_f = e_now.astype(F32)
        pr = MOE_BM // MOE_PARTS
        for p in range(MOE_PARTS):
            words = xs_ref[pl.ds(p * pr, pr)].reshape(pr, D_MODEL)
            packed = words[:, 0:PACK_W]
            meta = pltpu.bitcast(words[:, PACK_W:PACK_W + LANES], F32)
            xb = jnp.concatenate(
                [pltpu.bitcast(packed << 16, F32).astype(BF16),
                 pltpu.bitcast(packed & jnp.uint32(0xFFFF0000), F32).astype(BF16)], axis=1)
            gate = jnp.zeros((pr, 1), F32)
            for k in range(TOP_K):
                mk = meta[:, META_IDX + k:META_IDX + k + 1] == e_f
                gate = gate + jnp.where(mk, meta[:, META_GATE + k:META_GATE + k + 1], 0.0)
            gu = _dot(xb, wgu_bf[...]) + bgu_ref[0]
            glu = jnp.minimum(gu[:, :D_FF], SWIGLU_LIMIT)
            lin = jnp.clip(gu[:, D_FF:], -SWIGLU_LIMIT, SWIGLU_LIMIT)
            act = glu * jax.nn.sigmoid(SWIGLU_ALPHA * glu) * (lin + 1.0)
            y = (_dot(act.astype(BF16), wd_bf[...]) + bd_ref[0]) * gate
            ys_ref[pl.ds(p * pr, pr)] = y.reshape(pr, n_lt, LANES)

    @pl.when(j >= nu)
    def _():
        ys_ref[...] = jnp.zeros_like(ys_ref)


def _expert_rows(blk_exp, n_used, xs, wgu, bgu, wd, bd):
    n_rows = xs.shape[0]
    nb = n_rows // MOE_BM
    n_lt = D_MODEL // LANES
    row_blk = lambda j, be, nu: (jnp.minimum(j, nu[0] - 1), 0, 0)
    per_e = lambda j, be, nu: (be[j], 0, 0)
    return pl.pallas_call(
        _expert_rows_kernel,
        out_shape=jax.ShapeDtypeStruct((n_rows, n_lt, LANES), F32),
        grid_spec=pltpu.PrefetchScalarGridSpec(
            num_scalar_prefetch=2,
            grid=(nb,),
            in_specs=[pl.BlockSpec((MOE_BM, SUBLANES, LANES), row_blk),
                      pl.BlockSpec(memory_space=pl.ANY),
                      pl.BlockSpec((1, 1, 2 * D_FF), per_e),
                      pl.BlockSpec(memory_space=pl.ANY),
                      pl.BlockSpec((1, 1, D_MODEL), per_e)],
            out_specs=pl.BlockSpec((MOE_BM, n_lt, LANES), lambda j, be, nu: (j, 0, 0)),
            scratch_shapes=[pltpu.VMEM((D_MODEL, 2 * D_FF), F32), pltpu.VMEM((D_FF, D_MODEL), F32),
                            pltpu.VMEM((D_MODEL, 2 * D_FF), BF16), pltpu.VMEM((D_FF, D_MODEL), BF16),
                            pltpu.SemaphoreType.DMA((2,))]),
        compiler_params=pltpu.CompilerParams(dimension_semantics=("arbitrary",),
                                             vmem_limit_bytes=VMEM_LIMIT),
        name="experts",
    )(blk_exp, n_used, xs, wgu, bgu, wd, bd)


def _gather_combine_kernel(dest_ref, ys_ref, x1_ref, mod_ref, fg_ref, o_ref, buf, sem):
    i = pl.program_id(0)
    n = pl.num_programs(0)
    tc = GATHER_TILE
    slot = i % 2

    def issue(tile, sl):
        def per_tok(t, c):
            tok = tile * tc + t
            for k in range(TOP_K):
                d = dest_ref[k * (dest_ref.shape[0] // TOP_K) + tok]
                pltpu.make_async_copy(ys_ref.at[d], buf.at[sl, k * tc + t], sem.at[sl]).start(priority=k % 2)
            return c

        lax.fori_loop(0, tc, per_tok, 0, unroll=8)

    @pl.when(i == 0)
    def _():
        issue(0, 0)

    @pl.when(i + 1 < n)
    def _():
        issue(i + 1, 1 - slot)

    for _ in range(TOP_K):
        pltpu.make_async_copy(ys_ref.at[pl.ds(0, tc)], buf.at[slot, pl.ds(0, tc)], sem.at[slot]).wait()

    moe = ((buf[slot, pl.ds(0, tc)] + buf[slot, pl.ds(tc, tc)])
           + (buf[slot, pl.ds(2 * tc, tc)] + buf[slot, pl.ds(3 * tc, tc)])).reshape(tc, D_MODEL)
    m = mod_ref[0]
    x2 = x1_ref[...] + m[5:6] * moe
    o_ref[...] = _rms(x2) * fg_ref[...]


def _gather_combine(dest_flat, ys, x1, mod3, fg):
    n_tok = x1.shape[0]
    tc = GATHER_TILE
    per_b = SEQ // tc
    n_lt = D_MODEL // LANES
    return pl.pallas_call(
        _gather_combine_kernel,
        out_shape=jax.ShapeDtypeStruct((n_tok, D_MODEL), F32),
        grid_spec=pltpu.PrefetchScalarGridSpec(
            num_scalar_prefetch=1,
            grid=(n_tok // tc,),
            in_specs=[pl.BlockSpec(memory_space=pl.ANY),
                      pl.BlockSpec((tc, D_MODEL), lambda i, d: (i, 0)),
                      pl.BlockSpec((1, N_MOD, D_MODEL), lambda i, d: (i // per_b, 0, 0)),
                      pl.BlockSpec((1, D_MODEL), lambda i, d: (0, 0))],
            out_specs=pl.BlockSpec((tc, D_MODEL), lambda i, d: (i, 0)),
            scratch_shapes=[pltpu.VMEM((2, TOP_K * tc, n_lt, LANES), F32), pltpu.SemaphoreType.DMA((2,))]),
        compiler_params=pltpu.CompilerParams(dimension_semantics=("arbitrary",),
                                             vmem_limit_bytes=VMEM_LIMIT),
        name="combine",
    )(dest_flat, ys, x1, mod3, fg)


def _expansion_matrices(src0):
    r = (jnp.arange(2 * LANES) % LANES)[:, None]
    out64, out128 = [], []
    for d in range(2):
        l64 = jnp.arange(D_SSD)[None, :]
        l128 = jnp.arange(HEADS * LANES)[None, :]
        out64.append((l64 // HEAD_DIM == r - src0 - HEADS * d).astype(BF16))
        out128.append((l128 // LANES == r - src0 - HEADS * d).astype(BF16))
    return jnp.stack(out64), jnp.stack(out128)


def _pad_lanes(v):
    return jnp.pad(v, [(0, 0)] * (v.ndim - 1) + [(0, LANES - v.shape[-1])])


def kernel(x, c, ctx, c_ctx, w_mod, b_mod, norm1_g, w_in, ssd_conv_w, ssd_conv_b, ssd_dt_bias, ssd_a_log,
           ssd_d, ssd_norm_g, sc_conv_w, w_out, norm2_g, w_router, b_router, w_gate_up, b_gate_up, w_down,
           b_down, final_g):
    bsz = x.shape[0]
    n_tok = bsz * SEQ
    n_assign = n_tok * TOP_K
    n_blocks = n_assign // MOE_BM + N_EXPERTS
    n_rows = n_blocks * MOE_BM
    li = 0

    cvec = jnp.concatenate([c, c_ctx[None, :], jnp.zeros((7, D_MODEL), F32)], axis=0)
    mod3 = _mod(cvec, w_mod, b_mod[li][None, :], li).reshape(bsz + 8, N_MOD, D_MODEL)

    w = w_in[li]
    wz = w[:, Z0:X0].astype(BF16)
    wxbc = w[:, X0:DT0].astype(BF16)
    wdt = _pad_lanes(w[:, DT0:SC0]).astype(BF16)
    wb = w[:, SC0:SC0 + D_SC].astype(BF16)
    wc = w[:, SC0 + D_SC:SC0 + 2 * D_SC].astype(BF16)
    wu = w[:, SC0 + 2 * D_SC:].astype(BF16)
    g1 = norm1_g[li][None, :]
    cw = ssd_conv_w[li]
    cb = ssd_conv_b[li][None, :]
    dtb = _pad_lanes(ssd_dt_bias[li].reshape(1, 2 * HEADS))
    alog = _pad_lanes(ssd_a_log[li].reshape(1, 2 * HEADS))
    e64_ctx, _ = _expansion_matrices(0)

    h0 = _ctx_states(ctx, mod3, g1, wxbc[:, :XB_W], wdt, cw[:, :XB_W], cb[:, :XB_W], dtb, alog, e64_ctx)

    rep = lambda a: _pad_lanes(jnp.tile(a[..., :2 * HEADS], (1, DT_COPIES)))
    e64, e128 = _expansion_matrices(DT_DA0)
    x2 = x.reshape(n_tok, D_MODEL)
    z, xbc, dtp, scb, v = _inproj(x2, mod3, g1, wz, wxbc, rep(wdt), rep(dtb), rep(alog), wb, wc, wu)

    dsk = jnp.repeat(ssd_d[li], HEAD_DIM)[None, :]
    yssd = _ssd(xbc.reshape(bsz, SEQ, XBC_W), z.reshape(bsz, SEQ, D_SSD), dtp.reshape(bsz, SEQ, LANES), h0,
                cw, cb, dsk, ssd_norm_g[li][None, :], e64, e128)

    wo = w_out[li].astype(BF16)
    g2 = norm2_g[li][None, :]
    wr = w_router[li].T
    wr_hi = wr.astype(BF16)
    wr_lo = (wr - wr_hi.astype(F32)).astype(BF16)
    x1, lgt = _outproj(x2, yssd.reshape(n_tok, D_SSD), scb, v, mod3, sc_conv_w[li], wo[:D_SSD], wo[D_SSD:],
                       g2, jnp.concatenate([wr_hi, wr_hi, wr_lo], axis=1), b_router[li][:, None])

    dest_t, gate_t, idx_t, meta = _route(lgt, n_blocks)
    dest_flat = dest_t.reshape(n_assign)
    cnt = meta[0, :N_EXPERTS]
    start = meta[1, :N_EXPERTS]
    blk_exp = meta[2, :n_blocks]
    n_used = meta[3, :1]

    meta_rows = lax.bitcast_convert_type(_pad_lanes(jnp.concatenate(
        [idx_t.T.astype(F32), gate_t.T], axis=1)), jnp.uint32)
    pad_meta = lax.bitcast_convert_type(_pad_lanes(jnp.concatenate(
        [jnp.full((MOE_BM, TOP_K), -1.0, F32), jnp.zeros((MOE_BM, TOP_K), F32)], axis=1)), jnp.uint32)
    zsrc = jnp.concatenate([jnp.zeros((MOE_BM, PACK_W), jnp.uint32), pad_meta,
                            jnp.zeros((MOE_BM, D_MODEL - PACK_W - LANES), jnp.uint32)],
                           axis=1).reshape(MOE_BM, SUBLANES, LANES)

    xs = _dispatch(dest_flat, cnt, start, n_used, x1, meta_rows, mod3, g2, zsrc, n_rows)
    ys = _expert_rows(blk_exp, n_used, xs, w_gate_up[li], b_gate_up[li][:, None, :],
                      w_down[li], b_down[li][:, None, :])
    out = _gather_combine(dest_flat, ys, x1, mod3, final_g[None, :])
    return out.reshape(bsz, SEQ, D_MODEL)
```

```python
import functools

import jax
import jax.numpy as jnp
from jax import lax
from jax.experimental import pallas as pl
from jax.experimental.pallas import tpu as pltpu

F32 = jnp.float32
BF16 = jnp.bfloat16
HIGHEST = lax.Precision.HIGHEST

D_MODEL = 1024
SEQ = 2048
CTX_LEN = 256
GRID_W = 64
D_SSD = 1024
D_SC = 1024
HEAD_DIM = 64
HEADS = 16
GROUPS = 2
STATE = 128
CHUNK = 128
N_EXPERTS = 32
TOP_K = 4
D_FF = 1024
SWIGLU_LIMIT = 7.0
SWIGLU_ALPHA = 1.702
NORM_EPS = 1e-6
N_MOD = 6
XBC_W = D_SSD + 2 * GROUPS * STATE
XB_W = D_SSD + GROUPS * STATE
LANES = 128

Z0 = 0
X0 = Z0 + D_SSD
B0 = X0 + D_SSD
C0 = B0 + GROUPS * STATE
DT0 = C0 + GROUPS * STATE
SC0 = DT0 + 2 * HEADS

TOK_TILE = 512
OUT_TILE = 1024
MOE_BM = 512
MOE_PARTS = 2
RT_TILE = 512
CTX_BATCH = 4
DISP_TILE = 1024
GATHER_TILE = 256
DT_COPIES = 3
DT_LOG0 = 2 * HEADS
DT_DA0 = 4 * HEADS
SUBLANES = 8
PACK_W = D_MODEL // 2
META_IDX = 0
META_GATE = TOP_K
VMEM_LIMIT = 56 * 1024 * 1024


def _silu(v):
    return v * jax.nn.sigmoid(v)


def _softplus(v):
    return jnp.maximum(v, 0.0) + jnp.log1p(jnp.exp(-jnp.abs(v)))


def _rms(v):
    return v * lax.rsqrt(jnp.mean(v * v, axis=-1, keepdims=True) + NORM_EPS)


def _dot(a, b):
    return jnp.dot(a, b, preferred_element_type=F32)


def _expand2(v, e2):
    hi = v.astype(BF16)
    lo = (v - hi.astype(F32)).astype(BF16)
    return _dot(jnp.concatenate([hi, lo], axis=1), e2)


def _mod_kernel(c_ref, w_ref, b_ref, o_ref):
    o_ref[...] = jnp.dot(_silu(c_ref[...]), w_ref[0], precision=HIGHEST,
                         preferred_element_type=F32) + b_ref[...]


def _mod(cvec, w_mod, b_mod, layer):
    rows = cvec.shape[0]
    n = w_mod.shape[2]
    tn = 1536
    return pl.pallas_call(
        _mod_kernel,
        out_shape=jax.ShapeDtypeStruct((rows, n), F32),
        grid=(n // tn,),
        in_specs=[pl.BlockSpec((rows, D_MODEL), lambda j: (0, 0)),
                  pl.BlockSpec((1, D_MODEL, tn), lambda j: (layer, 0, j)),
                  pl.BlockSpec((1, tn), lambda j: (0, j))],
        out_specs=pl.BlockSpec((rows, tn), lambda j: (0, j)),
        compiler_params=pltpu.CompilerParams(dimension_semantics=("arbitrary",),
                                             vmem_limit_bytes=VMEM_LIMIT),
        name="mod",
    )(cvec, w_mod, b_mod)


def _ctx_kernel(ctx_ref, mod_ref, g1_ref, wxb_ref, wdt_ref, cw_ref, cb_ref, dtb_ref, alog_ref, e64_ref,
                h0_ref):
    L = CTX_LEN
    nb = ctx_ref.shape[0]
    m = mod_ref[0]
    hc = _rms(ctx_ref[...].reshape(nb * L, D_MODEL)) * g1_ref[...] * (1.0 + m[1:2]) + m[0:1]
    hb = hc.astype(BF16)
    pxb = _dot(hb, wxb_ref[...])
    dtr = _dot(hb, wdt_ref[...])
    rowl = lax.broadcasted_iota(jnp.int32, (nb * L, XB_W), 0) & (L - 1)
    dn = jnp.where(rowl == 0, 0.0, pltpu.roll(pxb, 1, 0))
    up = jnp.where(rowl == L - 1, 0.0, pltpu.roll(pxb, nb * L - 1, 0))
    cw = cw_ref[...]
    xb = _silu(cw[0:1] * dn + cw[1:2] * pxb + cw[2:3] * up + cb_ref[...])
    dt_all = _softplus(dtr + dtb_ref[...])
    da_all = dt_all * (-jnp.exp(alog_ref[...]))
    ri = lax.broadcasted_iota(jnp.int32, (L, L), 0)
    ci = lax.broadcasted_iota(jnp.int32, (L, L), 1)
    for bi in range(nb):
        rs = slice(bi * L, (bi + 1) * L)
        xs = xb[rs, :D_SSD]
        bm = xb[rs, D_SSD:].astype(BF16)
        dt = dt_all[rs]
        da = da_all[rs]
        for d in range(2):
            tri = (ci <= ri) if d == 0 else (ci >= ri)
            cum = jnp.dot(tri.astype(F32), da, precision=HIGHEST, preferred_element_type=F32)
            last = cum[L - 1:L] if d == 0 else cum[0:1]
            w_e = _expand2(jnp.exp(last - cum) * dt, e64_ref[d])
            xw = (xs * w_e).astype(BF16)
            for g in range(GROUPS):
                gw = D_SSD // GROUPS
                st = lax.dot_general(bm[:, g * STATE:(g + 1) * STATE], xw[:, g * gw:(g + 1) * gw],
                                     (((0,), (0,)), ((), ())), preferred_element_type=F32)
                h0_ref[bi, d, :, g * gw:(g + 1) * gw] = st


def _ctx_states(ctx, mod3, g1, wxb, wdt, cw, cb, dtb, alog, e64):
    bsz = ctx.shape[0]
    mod_row = bsz
    nb = CTX_BATCH if bsz % CTX_BATCH == 0 else 1
    const = lambda *shape: pl.BlockSpec(shape, lambda b: (0,) * len(shape))
    return pl.pallas_call(
        _ctx_kernel,
        out_shape=jax.ShapeDtypeStruct((bsz, 2, STATE, D_SSD), F32),
        grid=(bsz // nb,),
        in_specs=[pl.BlockSpec((nb, CTX_LEN, D_MODEL), lambda b: (b, 0, 0)),
                  pl.BlockSpec((1, N_MOD, D_MODEL), lambda b: (mod_row, 0, 0)),
                  const(1, D_MODEL), const(D_MODEL, XB_W), const(D_MODEL, LANES),
                  const(3, XB_W), const(1, XB_W), const(1, LANES), const(1, LANES),
                  const(2, 2 * LANES, D_SSD)],
        out_specs=pl.BlockSpec((nb, 2, STATE, D_SSD), lambda b: (b, 0, 0, 0)),
        compiler_params=pltpu.CompilerParams(dimension_semantics=("arbitrary",),
                                             vmem_limit_bytes=VMEM_LIMIT),
        name="ctx_states",
    )(ctx, mod3, g1, wxb, wdt, cw, cb, dtb, alog, e64)


def _inproj_kernel(x_ref, mod_ref, g1_ref, wz_ref, wxbc_ref, wdt_ref, dtb_ref, alog_ref, wb_ref, wc_ref, wu_ref,
                   z_ref, xbc_ref, dt_ref, scb_ref, v_ref):
    m = mod_ref[0]
    hx = _rms(x_ref[...]) * g1_ref[...] * (1.0 + m[1:2]) + m[0:1]
    hb = hx.astype(BF16)
    z_ref[...] = _dot(hb, wz_ref[...]).astype(BF16)
    xbc_ref[...] = _dot(hb, wxbc_ref[...]).astype(BF16)
    dt = _softplus(_dot(hb, wdt_ref[...]) + dtb_ref[...])
    lane = lax.broadcasted_iota(jnp.int32, dt.shape, 1)
    dt_ref[...] = jnp.where(lane < DT_LOG0, dt,
                            jnp.where(lane < DT_DA0, jnp.log(dt), dt * (-jnp.exp(alog_ref[...]))))
    scb_ref[...] = _dot(hb, wb_ref[...]).astype(BF16)
    v_ref[...] = (_dot(hb, wc_ref[...]) * _dot(hb, wu_ref[...])).astype(BF16)


def _inproj(x2, mod3, g1, wz, wxbc, wdt, dtb, alog, wb, wc, wu):
    t = x2.shape[0]
    tm = TOK_TILE
    per_b = SEQ // tm
    const = lambda *shape: pl.BlockSpec(shape, lambda i: (0,) * len(shape))
    tile = lambda w: pl.BlockSpec((tm, w), lambda i: (i, 0))
    return pl.pallas_call(
        _inproj_kernel,
        out_shape=(jax.ShapeDtypeStruct((t, D_SSD), BF16), jax.ShapeDtypeStruct((t, XBC_W), BF16),
                   jax.ShapeDtypeStruct((t, LANES), F32), jax.ShapeDtypeStruct((t, D_SC), BF16),
                   jax.ShapeDtypeStruct((t, D_SC), BF16)),
        grid=(t // tm,),
        in_specs=[tile(D_MODEL),
                  pl.BlockSpec((1, N_MOD, D_MODEL), lambda i: (i // per_b, 0, 0)),
                  const(1, D_MODEL), const(D_MODEL, D_SSD), const(D_MODEL, XBC_W), const(D_MODEL, LANES),
                  const(1, LANES), const(1, LANES),
                  const(D_MODEL, D_SC), const(D_MODEL, D_SC), const(D_MODEL, D_SC)],
        out_specs=(tile(D_SSD), tile(XBC_W), tile(LANES), tile(D_SC), tile(D_SC)),
        compiler_params=pltpu.CompilerParams(dimension_semantics=("arbitrary",),
                                             vmem_limit_bytes=VMEM_LIMIT),
        name="inproj",
    )(x2, mod3, g1, wz, wxbc, wdt, dtb, alog, wb, wc, wu)


def _ssd_kernel(xbc_ref, z_ref, dt_ref, h0_ref, cw_ref, cb_ref, dsk_ref, g_ref,
                e64_ref, e128_ref, o_ref, xc_ref, y_ref, s_ref):
    Q = CHUNK
    nck = SEQ // Q
    gw = D_SSD // GROUPS

    rowi = lax.broadcasted_iota(jnp.int32, (SUBLANES, XBC_W), 0)

    def conv_body(c, carry):
        r0 = pl.multiple_of(c * Q, Q)
        main = xbc_ref[0, pl.ds(r0, Q), :].astype(F32)
        pstart = pl.multiple_of(jnp.maximum(r0 - 16, 0), 16)
        nstart = pl.multiple_of(jnp.minimum(r0 + Q, SEQ - 16), 16)
        prev = xbc_ref[0, pl.ds(pstart, 16), :].astype(F32)[15:16]
        nxt = xbc_ref[0, pl.ds(nstart, 16), :].astype(F32)[0:1]
        prev = jnp.where(c > 0, prev, 0.0)
        nxt = jnp.where(c < nck - 1, nxt, 0.0)
        dn = pltpu.roll(main, 1, 0)
        up = pltpu.roll(main, Q - 1, 0)
        dn = jnp.concatenate([jnp.where(rowi == 0, prev, dn[0:SUBLANES]), dn[SUBLANES:]], axis=0)
        up = jnp.concatenate([up[:Q - SUBLANES], jnp.where(rowi == SUBLANES - 1, nxt, up[Q - SUBLANES:])], axis=0)
        cw = cw_ref[...]
        conv = cw[0:1] * dn + cw[1:2] * main + cw[2:3] * up + cb_ref[...]
        xc_ref[pl.ds(r0, Q), :] = _silu(conv).astype(BF16)
        return carry

    lax.fori_loop(0, nck, conv_body, 0)

    ri = lax.broadcasted_iota(jnp.int32, (Q, Q), 0)
    ci = lax.broadcasted_iota(jnp.int32, (Q, Q), 1)
    lane = lax.broadcasted_iota(jnp.int32, (Q, LANES), 1)
    da_lanes = jnp.logical_and(lane >= DT_DA0, lane < DT_DA0 + 2 * HEADS)

    def chunk(c, d, first):
        r0 = pl.multiple_of(c * Q, Q)
        rows = pl.ds(r0, Q)
        xs_b = xc_ref[rows, 0:D_SSD]
        xs = xs_b.astype(F32)
        bm = xc_ref[rows, D_SSD:D_SSD + GROUPS * STATE]
        cm = xc_ref[rows, D_SSD + GROUPS * STATE:XBC_W]
        dtp = dt_ref[0, rows, :]
        da = jnp.where(da_lanes, dtp, 0.0)
        dt = pltpu.roll(dtp, DT_DA0, 1)
        log_dt = pltpu.roll(dtp, DT_DA0 - DT_LOG0, 1)
        tri = (ci <= ri) if d == 0 else (ci >= ri)
        p0 = da.astype(BF16)
        r1 = da - p0.astype(F32)
        p1 = r1.astype(BF16)
        p2 = (r1 - p1.astype(F32)).astype(BF16)
        tri_b = jnp.where(tri, 1.0, 0.0).astype(BF16)
        cum = _dot(jnp.concatenate([tri_b, tri_b, tri_b], axis=1),
                   jnp.concatenate([p0, p1, p2], axis=0))
        sub_t = (cum - log_dt).T
        last = cum[Q - 1:Q] if d == 0 else cum[0:1]
        ecum_e = _expand2(jnp.exp(cum), e64_ref[d])
        w_e = _expand2(jnp.where(da_lanes, jnp.exp(last - cum) * dt, 0.0), e64_ref[d])
        colb = _expand2(cum, e128_ref[d])
        decay_e = ecum_e[Q - 1:Q] if d == 0 else ecum_e[0:1]

        gmat = [lax.dot_general(cm[:, g * STATE:(g + 1) * STATE], bm[:, g * STATE:(g + 1) * STATE],
                                (((1,), (1,)), ((), ())), preferred_element_type=F32)
                for g in range(GROUPS)]
        zero_b = jnp.zeros((Q, LANES), BF16)
        y_parts = []
        for p in range(HEADS // 2):
            g = (2 * p) // (HEADS // GROUPS)
            ms = []
            for hh in (2 * p, 2 * p + 1):
                src = DT_DA0 + HEADS * d + hh
                seg = colb[:, hh * LANES:(hh + 1) * LANES] - sub_t[src:src + 1, :]
                ms.append((jnp.where(tri, jnp.exp(seg), 0.0) * gmat[g]).astype(BF16))
            mcat = jnp.concatenate(ms, axis=1)
            xp = xs_b[:, p * LANES:(p + 1) * LANES]
            rhs = jnp.concatenate([jnp.where(lane < HEAD_DIM, xp, zero_b),
                                   jnp.where(lane >= HEAD_DIM, xp, zero_b)], axis=0)
            y_parts.append(_dot(mcat, rhs))
        y_diag = jnp.concatenate(y_parts, axis=1)

        s_old = s_ref[d]
        s_bf = s_old.astype(BF16)
        y_off = jnp.concatenate(
            [_dot(cm[:, g * STATE:(g + 1) * STATE], s_bf[:, g * gw:(g + 1) * gw]) for g in range(GROUPS)],
            axis=1)
        y = y_diag + y_off * ecum_e

        xw = (xs * w_e).astype(BF16)
        upd = jnp.concatenate(
            [lax.dot_general(bm[:, g * STATE:(g + 1) * STATE], xw[:, g * gw:(g + 1) * gw],
                             (((0,), (0,)), ((), ())), preferred_element_type=F32) for g in range(GROUPS)],
            axis=1)
        s_ref[d] = s_old * decay_e + upd

        if first:
            y_ref[rows, :] = y + dsk_ref[...] * xs
        else:
            tot = y_ref[rows, :] + y
            zz = z_ref[0, rows, :].astype(F32)
            gz = tot * _silu(zz)
            outs = []
            for g in range(GROUPS):
                gg = gz[:, g * gw:(g + 1) * gw]
                outs.append(gg * lax.rsqrt(jnp.mean(gg * gg, axis=-1, keepdims=True) + NORM_EPS))
            o_ref[0, rows, :] = (jnp.concatenate(outs, axis=1) * g_ref[...]).astype(BF16)

    s_ref[...] = h0_ref[0]

    def first_half(i, carry):
        chunk(i, 0, True)
        chunk(nck - 1 - i, 1, True)
        return carry

    def second_half(i, carry):
        chunk(i, 0, False)
        chunk(nck - 1 - i, 1, False)
        return carry

    lax.fori_loop(0, nck // 2, first_half, 0)
    lax.fori_loop(nck // 2, nck, second_half, 0)


def _ssd(xbc3, z3, dt3, h0, cw, cb, dsk, g, e64, e128):
    bsz = xbc3.shape[0]
    const = lambda *shape: pl.BlockSpec(shape, lambda b: (0,) * len(shape))
    seq = lambda w: pl.BlockSpec((1, SEQ, w), lambda b: (b, 0, 0))
    return pl.pallas_call(
        _ssd_kernel,
        out_shape=jax.ShapeDtypeStruct((bsz, SEQ, D_SSD), BF16),
        grid=(bsz,),
        in_specs=[seq(XBC_W), seq(D_SSD), seq(LANES),
                  pl.BlockSpec((1, 2, STATE, D_SSD), lambda b: (b, 0, 0, 0)),
                  const(3, XBC_W), const(1, XBC_W),
                  const(1, D_SSD), const(1, D_SSD), const(2, 2 * LANES, D_SSD),
                  const(2, 2 * LANES, HEADS * LANES)],
        out_specs=seq(D_SSD),
        scratch_shapes=[pltpu.VMEM((SEQ, XBC_W), BF16), pltpu.VMEM((SEQ, D_SSD), F32),
                        pltpu.VMEM((2, STATE, D_SSD), F32)],
        compiler_params=pltpu.CompilerParams(dimension_semantics=("arbitrary",),
                                             vmem_limit_bytes=VMEM_LIMIT),
        name="ssd",
    )(xbc3, z3, dt3, h0, cw, cb, dsk, g, e64, e128)


def _outproj_kernel(x_ref, yssd_ref, scb_ref, v_ref, vp_ref, vn_ref, mod_ref, scw_ref, wo1_ref, wo2_ref,
                    g2_ref, wrt_ref, br_ref, x1_ref, lg_ref):
    tm = OUT_TILE
    per_b = SEQ // tm
    i = pl.program_id(0)
    first = (i % per_b) == 0
    last = (i % per_b) == per_b - 1
    m = mod_ref[0]
    v = v_ref[...].astype(F32)
    vp = jnp.where(first, 0.0, vp_ref[...].astype(F32))
    vn = jnp.where(last, 0.0, vn_ref[...].astype(F32))
    dn = jnp.concatenate([vp, v[:tm - GRID_W]], axis=0)
    up = jnp.concatenate([v[GRID_W:], vn], axis=0)
    scw = scw_ref[...]
    ysc = scb_ref[...].astype(F32) * (scw[0:1] * dn + scw[1:2] * v + scw[2:3] * up)
    out = _dot(yssd_ref[...], wo1_ref[...]) + _dot(ysc.astype(BF16), wo2_ref[...])
    x1 = x_ref[...] + m[2:3] * out
    x1_ref[...] = x1
    h2 = _rms(x1) * g2_ref[...] * (1.0 + m[4:5]) + m[3:4]
    h_hi = h2.astype(BF16)
    h_lo = (h2 - h_hi.astype(F32)).astype(BF16)
    lg_ref[...] = lax.dot_general(wrt_ref[...], jnp.concatenate([h_hi, h_lo, h_hi], axis=1),
                                  (((1,), (1,)), ((), ())), preferred_element_type=F32) + br_ref[...]


def _outproj(x2, yssd, scb, v, mod3, scw, wo1, wo2, g2, wrt, br):
    t = x2.shape[0]
    tm = OUT_TILE
    per_b = SEQ // tm
    r = tm // GRID_W
    nrow = t // GRID_W
    const = lambda *shape: pl.BlockSpec(shape, lambda i: (0,) * len(shape))
    tile = lambda w: pl.BlockSpec((tm, w), lambda i: (i, 0))
    return pl.pallas_call(
        _outproj_kernel,
        out_shape=(jax.ShapeDtypeStruct((t, D_MODEL), F32), jax.ShapeDtypeStruct((N_EXPERTS, t), F32)),
        grid=(t // tm,),
        in_specs=[tile(D_MODEL), tile(D_SSD), tile(D_SC), tile(D_SC),
                  pl.BlockSpec((GRID_W, D_SC), lambda i: (jnp.maximum(i * r - 1, 0), 0)),
                  pl.BlockSpec((GRID_W, D_SC), lambda i: (jnp.minimum((i + 1) * r, nrow - 1), 0)),
                  pl.BlockSpec((1, N_MOD, D_MODEL), lambda i: (i // per_b, 0, 0)),
                  const(3, D_SC), const(D_SSD, D_MODEL), const(D_SC, D_MODEL), const(1, D_MODEL),
                  const(N_EXPERTS, 3 * D_MODEL), const(N_EXPERTS, 1)],
        out_specs=(tile(D_MODEL), pl.BlockSpec((N_EXPERTS, tm), lambda i: (0, i))),
        compiler_params=pltpu.CompilerParams(dimension_semantics=("arbitrary",),
                                             vmem_limit_bytes=VMEM_LIMIT),
        name="outproj",
    )(x2, yssd, scb, v, v, v, mod3, scw, wo1, wo2, g2, wrt, br)


def _route_kernel(lg_ref, dest_ref, gate_ref, idx_ref, meta_ref, rank_ref, carry_ref, *, n_tok, n_blocks):
    tt = RT_TILE
    ne = N_EXPERTS
    eio = lax.broadcasted_iota(jnp.int32, (ne, tt), 0)
    si = lax.broadcasted_iota(jnp.int32, (tt, tt), 0)
    ti = lax.broadcasted_iota(jnp.int32, (tt, tt), 1)
    before = (si < ti).astype(BF16)
    carry_ref[...] = jnp.zeros_like(carry_ref)

    def tile_body(j, c):
        t0 = pl.multiple_of(j * tt, tt)
        l = lg_ref[:, pl.ds(t0, tt)]
        onehot = jnp.zeros((ne, tt), F32)
        tops, sels = [], []
        for _ in range(TOP_K):
            mx = jnp.max(l, axis=0, keepdims=True)
            idx = jnp.min(jnp.where(l == mx, eio, ne), axis=0, keepdims=True)
            sel = eio == idx
            l = jnp.where(sel, -jnp.inf, l)
            onehot = onehot + sel.astype(F32)
            tops.append(mx)
            sels.append(sel)
            idx_ref[pl.ds(len(tops) - 1, 1), pl.ds(t0, tt)] = idx
        ex = [jnp.exp(tv - tops[0]) for tv in tops]
        den = ex[0] + ex[1] + ex[2] + ex[3]
        prefix = _dot(onehot.astype(BF16), before) + carry_ref[:, 0:1]
        for k in range(TOP_K):
            gate_ref[pl.ds(k, 1), pl.ds(t0, tt)] = ex[k] / den
            rk = jnp.sum(jnp.where(sels[k], prefix, 0.0), axis=0, keepdims=True)
            rank_ref[pl.ds(k, 1), pl.ds(t0, tt)] = rk.astype(jnp.int32)
        carry_ref[...] = carry_ref[...] + jnp.sum(onehot, axis=1, keepdims=True)
        return c

    lax.fori_loop(0, n_tok // tt, tile_body, 0)

    counts = carry_ref[...]
    padded = jnp.floor((counts + (MOE_BM - 1)) * (1.0 / MOE_BM)) * MOE_BM
    er = lax.broadcasted_iota(jnp.int32, (ne, ne), 0)
    ec = lax.broadcasted_iota(jnp.int32, (ne, ne), 1)
    pad_start = jnp.dot((ec < er).astype(F32), padded, precision=HIGHEST, preferred_element_type=F32)
    pad_end = pad_start + padded

    def dest_body(j, c):
        t0 = pl.multiple_of(j * tt, tt)
        for k in range(TOP_K):
            idx = idx_ref[pl.ds(k, 1), pl.ds(t0, tt)]
            base = jnp.sum(jnp.where(eio == idx, pad_start[:, 0:1], 0.0), axis=0, keepdims=True)
            dest_ref[pl.ds(k, 1), pl.ds(t0, tt)] = base.astype(jnp.int32) + rank_ref[pl.ds(k, 1), pl.ds(t0, tt)]
        return c

    lax.fori_loop(0, n_tok // tt, dest_body, 0)

    width = meta_ref.shape[1]
    sub = lax.broadcasted_iota(jnp.int32, (ne, width), 0)
    lan = lax.broadcasted_iota(jnp.int32, (ne, width), 1)
    diag = sub == lan
    cnt_row = jnp.sum(jnp.where(diag, counts[:, 0:1], 0.0), axis=0, keepdims=True)
    start_row = jnp.sum(jnp.where(diag, pad_start[:, 0:1], 0.0), axis=0, keepdims=True)
    blk_start = (lan * MOE_BM).astype(F32)
    blk_exp = jnp.sum((pad_end[:, 0:1] <= blk_start).astype(F32), axis=0, keepdims=True)
    blk_exp = jnp.minimum(blk_exp, float(ne - 1))
    used = jnp.sum(padded[:, 0:1], axis=0, keepdims=True) * (1.0 / MOE_BM)
    meta_ref[0:1, :] = cnt_row.astype(jnp.int32)
    meta_ref[1:2, :] = start_row.astype(jnp.int32)
    meta_ref[2:3, :] = blk_exp.astype(jnp.int32)
    meta_ref[3:4, :] = jnp.broadcast_to(used, (1, width)).astype(jnp.int32)
    meta_ref[4:8, :] = jnp.zeros((4, width), jnp.int32)


def _route(lgt, n_blocks):
    ne, n_tok = lgt.shape
    width = -(-n_blocks // LANES) * LANES
    full = lambda *shape: pl.BlockSpec(shape, lambda: (0,) * len(shape))
    return pl.pallas_call(
        functools.partial(_route_kernel, n_tok=n_tok, n_blocks=n_blocks),
        out_shape=(jax.ShapeDtypeStruct((TOP_K, n_tok), jnp.int32),
                   jax.ShapeDtypeStruct((TOP_K, n_tok), F32),
                   jax.ShapeDtypeStruct((TOP_K, n_tok), jnp.int32),
                   jax.ShapeDtypeStruct((8, width), jnp.int32)),
        in_specs=[full(ne, n_tok)],
        out_specs=(full(TOP_K, n_tok), full(TOP_K, n_tok), full(TOP_K, n_tok), full(8, width)),
        scratch_shapes=[pltpu.VMEM((TOP_K, n_tok), jnp.int32), pltpu.VMEM((ne, LANES), F32)],
        compiler_params=pltpu.CompilerParams(vmem_limit_bytes=VMEM_LIMIT),
        name="route",
    )(lgt)


def _dispatch_kernel(dest_ref, cnt_ref, start_ref, nu_ref, x1_ref, meta_ref, mod_ref, g2_ref, zsrc_ref, xs_ref,
                     hbuf, sem, zsem):
    i = pl.program_id(0)
    n = pl.num_programs(0)
    tl = DISP_TILE
    slot = i % 2
    nb = xs_ref.shape[0] // MOE_BM

    def zero_block(b):
        return pltpu.make_async_copy(zsrc_ref, xs_ref.at[pl.ds(b * MOE_BM, MOE_BM)], zsem)

    @pl.when(i == 0)
    def _():
        def start_e(e, c):
            @pl.when(cnt_ref[e] > 0)
            def _():
                zero_block((start_ref[e] + cnt_ref[e] - 1) // MOE_BM).start()
            return c

        def wait_e(e, c):
            @pl.when(cnt_ref[e] > 0)
            def _():
                zero_block(0).wait()
            return c

        def start_t(b, c):
            zero_block(b).start()
            return c

        def wait_t(b, c):
            zero_block(0).wait()
            return c

        lax.fori_loop(0, N_EXPERTS, start_e, 0)
        lax.fori_loop(nu_ref[0], nb, start_t, 0)
        lax.fori_loop(0, N_EXPERTS, wait_e, 0)
        lax.fori_loop(nu_ref[0], nb, wait_t, 0)

    m = mod_ref[0]
    h2 = _rms(x1_ref[...]) * g2_ref[...] * (1.0 + m[4:5]) + m[3:4]
    lo = pltpu.bitcast(h2[:, :PACK_W].astype(BF16).astype(F32), jnp.uint32) >> 16
    hi = pltpu.bitcast(h2[:, PACK_W:].astype(BF16).astype(F32), jnp.uint32) & jnp.uint32(0xFFFF0000)
    row = jnp.concatenate([lo | hi, meta_ref[...], jnp.zeros((tl, D_MODEL - PACK_W - LANES), jnp.uint32)], axis=1)
    hbuf[slot] = row.reshape(tl, SUBLANES, LANES)

    def per_tok(t, c):
        tok = i * tl + t
        for k in range(TOP_K):
            d = dest_ref[k * (dest_ref.shape[0] // TOP_K) + tok]
            pltpu.make_async_copy(hbuf.at[slot, t], xs_ref.at[d], sem.at[slot]).start(priority=k % 2)
        return c

    lax.fori_loop(0, tl, per_tok, 0, unroll=8)

    def wait_slot(sl):
        for _ in range(TOP_K):
            pltpu.make_async_copy(hbuf.at[sl], xs_ref.at[pl.ds(0, tl)], sem.at[sl]).wait()

    @pl.when(i > 0)
    def _():
        wait_slot(1 - slot)

    @pl.when(i == n - 1)
    def _():
        wait_slot(slot)


def _dispatch(dest_flat, cnt, start, n_used, x1, meta_rows, mod3, g2, zsrc, n_rows):
    n_tok = x1.shape[0]
    tl = DISP_TILE
    per_b = SEQ // tl
    return pl.pallas_call(
        _dispatch_kernel,
        out_shape=jax.ShapeDtypeStruct((n_rows, SUBLANES, LANES), jnp.uint32),
        grid_spec=pltpu.PrefetchScalarGridSpec(
            num_scalar_prefetch=4,
            grid=(n_tok // tl,),
            in_specs=[pl.BlockSpec((tl, D_MODEL), lambda i, *_: (i, 0)),
                      pl.BlockSpec((tl, LANES), lambda i, *_: (i, 0)),
                      pl.BlockSpec((1, N_MOD, D_MODEL), lambda i, *_: (i // per_b, 0, 0)),
                      pl.BlockSpec((1, D_MODEL), lambda i, *_: (0, 0)),
                      pl.BlockSpec((MOE_BM, SUBLANES, LANES), lambda i, *_: (0, 0, 0))],
            out_specs=pl.BlockSpec(memory_space=pl.ANY),
            scratch_shapes=[pltpu.VMEM((2, tl, SUBLANES, LANES), jnp.uint32),
                            pltpu.SemaphoreType.DMA((2,)), pltpu.SemaphoreType.DMA]),
        compiler_params=pltpu.CompilerParams(dimension_semantics=("arbitrary",),
                                             vmem_limit_bytes=VMEM_LIMIT),
        name="dispatch",
    )(dest_flat, cnt, start, n_used, x1, meta_rows, mod3, g2, zsrc)


def _expert_rows_kernel(be_ref, nu_ref, xs_ref, wgu_hbm, bgu_ref, wd_hbm, bd_ref, ys_ref,
                        wgu_raw, wd_raw, wgu_bf, wd_bf, w_sem):
    j = pl.program_id(0)
    nu = nu_ref[0]
    n_lt = D_MODEL // LANES

    def weight_copies(e):
        return (pltpu.make_async_copy(wgu_hbm.at[e], wgu_raw, w_sem.at[0]),
                pltpu.make_async_copy(wd_hbm.at[e], wd_raw, w_sem.at[1]))

    @pl.when(j == 0)
    def _():
        for w in weight_copies(be_ref[0]):
            w.start()

    e_now = be_ref[j]
    new_expert = jnp.logical_or(j == 0, be_ref[jnp.maximum(j - 1, 0)] != e_now)

    @pl.when(jnp.logical_and(j < nu, new_expert))
    def _():
        for w in weight_copies(e_now):
            w.wait()
        n_cc = 8
        for c in range(n_cc):
            cc = slice(c * (2 * D_FF // n_cc), (c + 1) * (2 * D_FF // n_cc))
            wgu_bf[:, cc] = wgu_raw[:, cc].astype(BF16)
        for c in range(n_cc // 2):
            cc = slice(c * (2 * D_MODEL // n_cc), (c + 1) * (2 * D_MODEL // n_cc))
            wd_bf[:, cc] = wd_raw[:, cc].astype(BF16)
        j_next = lax.while_loop(lambda t: jnp.logical_and(t < nu, be_ref[jnp.minimum(t, pl.num_programs(0) - 1)] == e_now),
                                lambda t: t + 1, j + 1)

        @pl.when(j_next < nu)
        def _():
            for w in weight_copies(be_ref[jnp.minimum(j_next, pl.num_programs(0) - 1)]):
                w.start()

    @pl.when(j < nu)
    def _():
        e_f = e_now.astype(F32)
        pr = MOE_BM // MOE_PARTS
        for p in range(MOE_PARTS):
            words = xs_ref[pl.ds(p * pr, pr)].reshape(pr, D_MODEL)
            packed = words[:, 0:PACK_W]
            meta = pltpu.bitcast(words[:, PACK_W:PACK_W + LANES], F32)
            xb = jnp.concatenate(
                [pltpu.bitcast(packed << 16, F32).astype(BF16),
                 pltpu.bitcast(packed & jnp.uint32(0xFFFF0000), F32).astype(BF16)], axis=1)
            gate = jnp.zeros((pr, 1), F32)
            for k in range(TOP_K):
                mk = meta[:, META_IDX + k:META_IDX + k + 1] == e_f
                gate = gate + jnp.where(mk, meta[:, META_GATE + k:META_GATE + k + 1], 0.0)
            gu = _dot(xb, wgu_bf[...]) + bgu_ref[0]
            glu = jnp.minimum(gu[:, :D_FF], SWIGLU_LIMIT)
            lin = jnp.clip(gu[:, D_FF:], -SWIGLU_LIMIT, SWIGLU_LIMIT)
            act = glu * jax.nn.sigmoid(SWIGLU_ALPHA * glu) * (lin + 1.0)
            y = (_dot(act.astype(BF16), wd_bf[...]) + bd_ref[0]) * gate
            ys_ref[pl.ds(p * pr, pr)] = y.reshape(pr, n_lt, LANES)

    @pl.when(j >= nu)
    def _():
        ys_ref[...] = jnp.zeros_like(ys_ref)


def _expert_rows(blk_exp, n_used, xs, wgu, bgu, wd, bd):
    n_rows = xs.shape[0]
    nb = n_rows // MOE_BM
    n_lt = D_MODEL // LANES
    row_blk = lambda j, be, nu: (jnp.minimum(j, nu[0] - 1), 0, 0)
    per_e = lambda j, be, nu: (be[j], 0, 0)
    return pl.pallas_call(
        _expert_rows_kernel,
        out_shape=jax.ShapeDtypeStruct((n_rows, n_lt, LANES), F32),
        grid_spec=pltpu.PrefetchScalarGridSpec(
            num_scalar_prefetch=2,
            grid=(nb,),
            in_specs=[pl.BlockSpec((MOE_BM, SUBLANES, LANES), row_blk),
                      pl.BlockSpec(memory_space=pl.ANY),
                      pl.BlockSpec((1, 1, 2 * D_FF), per_e),
                      pl.BlockSpec(memory_space=pl.ANY),
                      pl.BlockSpec((1, 1, D_MODEL), per_e)],
            out_specs=pl.BlockSpec((MOE_BM, n_lt, LANES), lambda j, be, nu: (j, 0, 0)),
            scratch_shapes=[pltpu.VMEM((D_MODEL, 2 * D_FF), F32), pltpu.VMEM((D_FF, D_MODEL), F32),
                            pltpu.VMEM((D_MODEL, 2 * D_FF), BF16), pltpu.VMEM((D_FF, D_MODEL), BF16),
                            pltpu.SemaphoreType.DMA((2,))]),
        compiler_params=pltpu.CompilerParams(dimension_semantics=("arbitrary",),
                                             vmem_limit_bytes=VMEM_LIMIT),
        name="experts",
    )(blk_exp, n_used, xs, wgu, bgu, wd, bd)


def _gather_combine_kernel(dest_ref, ys_ref, x1_ref, mod_ref, fg_ref, o_ref, buf, sem):
    i = pl.program_id(0)
    n = pl.num_programs(0)
    tc = GATHER_TILE
    slot = i % 2

    def issue(tile, sl):
        def per_tok(t, c):
            tok = tile * tc + t
            for k in range(TOP_K):
                d = dest_ref[k * (dest_ref.shape[0] // TOP_K) + tok]
                pltpu.make_async_copy(ys_ref.at[d], buf.at[sl, k * tc + t], sem.at[sl]).start(priority=k % 2)
            return c

        lax.fori_loop(0, tc, per_tok, 0, unroll=8)

    @pl.when(i == 0)
    def _():
        issue(0, 0)

    @pl.when(i + 1 < n)
    def _():
        issue(i + 1, 1 - slot)

    for _ in range(TOP_K):
        pltpu.make_async_copy(ys_ref.at[pl.ds(0, tc)], buf.at[slot, pl.ds(0, tc)], sem.at[slot]).wait()

    moe = ((buf[slot, pl.ds(0, tc)] + buf[slot, pl.ds(tc, tc)])
           + (buf[slot, pl.ds(2 * tc, tc)] + buf[slot, pl.ds(3 * tc, tc)])).reshape(tc, D_MODEL)
    m = mod_ref[0]
    x2 = x1_ref[...] + m[5:6] * moe
    o_ref[...] = _rms(x2) * fg_ref[...]


def _gather_combine(dest_flat, ys, x1, mod3, fg):
    n_tok = x1.shape[0]
    tc = GATHER_TILE
    per_b = SEQ // tc
    n_lt = D_MODEL // LANES
    return pl.pallas_call(
        _gather_combine_kernel,
        out_shape=jax.ShapeDtypeStruct((n_tok, D_MODEL), F32),
        grid_spec=pltpu.PrefetchScalarGridSpec(
            num_scalar_prefetch=1,
            grid=(n_tok // tc,),
            in_specs=[pl.BlockSpec(memory_space=pl.ANY),
                      pl.BlockSpec((tc, D_MODEL), lambda i, d: (i, 0)),
                      pl.BlockSpec((1, N_MOD, D_MODEL), lambda i, d: (i // per_b, 0, 0)),
                      pl.BlockSpec((1, D_MODEL), lambda i, d: (0, 0))],
            out_specs=pl.BlockSpec((tc, D_MODEL), lambda i, d: (i, 0)),
            scratch_shapes=[pltpu.VMEM((2, TOP_K * tc, n_lt, LANES), F32), pltpu.SemaphoreType.DMA((2,))]),
        compiler_params=pltpu.CompilerParams(dimension_semantics=("arbitrary",),
                                             vmem_limit_bytes=VMEM_LIMIT),
        name="combine",
    )(dest_flat, ys, x1, mod3, fg)


def _expansion_matrices(src0):
    r = (jnp.arange(2 * LANES) % LANES)[:, None]
    out64, out128 = [], []
    for d in range(2):
        l64 = jnp.arange(D_SSD)[None, :]
        l128 = jnp.arange(HEADS * LANES)[None, :]
        out64.append((l64 // HEAD_DIM == r - src0 - HEADS * d).astype(BF16))
        out128.append((l128 // LANES == r - src0 - HEADS * d).astype(BF16))
    return jnp.stack(out64), jnp.stack(out128)


def _pad_lanes(v):
    return jnp.pad(v, [(0, 0)] * (v.ndim - 1) + [(0, LANES - v.shape[-1])])


def kernel(x, c, ctx, c_ctx, w_mod, b_mod, norm1_g, w_in, ssd_conv_w, ssd_conv_b, ssd_dt_bias, ssd_a_log,
           ssd_d, ssd_norm_g, sc_conv_w, w_out, norm2_g, w_router, b_router, w_gate_up, b_gate_up, w_down,
           b_down, final_g):
    bsz = x.shape[0]
    n_tok = bsz * SEQ
    n_assign = n_tok * TOP_K
    n_blocks = n_assign // MOE_BM + N_EXPERTS
    n_rows = n_blocks * MOE_BM
    li = 0

    cvec = jnp.concatenate([c, c_ctx[None, :], jnp.zeros((7, D_MODEL), F32)], axis=0)
    mod3 = _mod(cvec, w_mod, b_mod[li][None, :], li).reshape(bsz + 8, N_MOD, D_MODEL)

    w = w_in[li]
    wz = w[:, Z0:X0].astype(BF16)
    wxbc = w[:, X0:DT0].astype(BF16)
    wdt = _pad_lanes(w[:, DT0:SC0]).astype(BF16)
    wb = w[:, SC0:SC0 + D_SC].astype(BF16)
    wc = w[:, SC0 + D_SC:SC0 + 2 * D_SC].astype(BF16)
    wu = w[:, SC0 + 2 * D_SC:].astype(BF16)
    g1 = norm1_g[li][None, :]
    cw = ssd_conv_w[li]
    cb = ssd_conv_b[li][None, :]
    dtb = _pad_lanes(ssd_dt_bias[li].reshape(1, 2 * HEADS))
    alog = _pad_lanes(ssd_a_log[li].reshape(1, 2 * HEADS))
    e64_ctx, _ = _expansion_matrices(0)

    h0 = _ctx_states(ctx, mod3, g1, wxbc[:, :XB_W], wdt, cw[:, :XB_W], cb[:, :XB_W], dtb, alog, e64_ctx)

    rep = lambda a: _pad_lanes(jnp.tile(a[..., :2 * HEADS], (1, DT_COPIES)))
    e64, e128 = _expansion_matrices(DT_DA0)
    x2 = x.reshape(n_tok, D_MODEL)
    z, xbc, dtp, scb, v = _inproj(x2, mod3, g1, wz, wxbc, rep(wdt), rep(dtb), rep(alog), wb, wc, wu)

    dsk = jnp.repeat(ssd_d[li], HEAD_DIM)[None, :]
    yssd = _ssd(xbc.reshape(bsz, SEQ, XBC_W), z.reshape(bsz, SEQ, D_SSD), dtp.reshape(bsz, SEQ, LANES), h0,
                cw, cb, dsk, ssd_norm_g[li][None, :], e64, e128)

    wo = w_out[li].astype(BF16)
    g2 = norm2_g[li][None, :]
    wr = w_router[li].T
    wr_hi = wr.astype(BF16)
    wr_lo = (wr - wr_hi.astype(F32)).astype(BF16)
    x1, lgt = _outproj(x2, yssd.reshape(n_tok, D_SSD), scb, v, mod3, sc_conv_w[li], wo[:D_SSD], wo[D_SSD:],
                       g2, jnp.concatenate([wr_hi, wr_hi, wr_lo], axis=1), b_router[li][:, None])

    dest_t, gate_t, idx_t, meta = _route(lgt, n_blocks)
    dest_flat = dest_t.reshape(n_assign)
    cnt = meta[0, :N_EXPERTS]
    start = meta[1, :N_EXPERTS]
    blk_exp = meta[2, :n_blocks]
    n_used = meta[3, :1]

    meta_rows = lax.bitcast_convert_type(_pad_lanes(jnp.concatenate(
        [idx_t.T.astype(F32), gate_t.T], axis=1)), jnp.uint32)
    pad_meta = lax.bitcast_convert_type(_pad_lanes(jnp.concatenate(
        [jnp.full((MOE_BM, TOP_K), -1.0, F32), jnp.zeros((MOE_BM, TOP_K), F32)], axis=1)), jnp.uint32)
    zsrc = jnp.concatenate([jnp.zeros((MOE_BM, PACK_W), jnp.uint32), pad_meta,
                            jnp.zeros((MOE_BM, D_MODEL - PACK_W - LANES), jnp.uint32)],
                           axis=1).reshape(MOE_BM, SUBLANES, LANES)

    xs = _dispatch(dest_flat, cnt, start, n_used, x1, meta_rows, mod3, g2, zsrc, n_rows)
    ys = _expert_rows(blk_exp, n_used, xs, w_gate_up[li], b_gate_up[li][:, None, :],
                      w_down[li], b_down[li][:, None, :])
    out = _gather_combine(dest_flat, ys, x1, mod3, final_g[None, :])
    return out.reshape(bsz, SEQ, D_MODEL)
```

```python
import functools

import jax
import jax.numpy as jnp
from jax import lax
from jax.experimental import pallas as pl
from jax.experimental.pallas import tpu as pltpu

F32 = jnp.float32
BF16 = jnp.bfloat16
HIGHEST = lax.Precision.HIGHEST

D_MODEL = 1024
SEQ = 2048
CTX_LEN = 256
GRID_W = 64
D_SSD = 1024
D_SC = 1024
HEAD_DIM = 64
HEADS = 16
GROUPS = 2
STATE = 128
CHUNK = 128
N_EXPERTS = 32
TOP_K = 4
D_FF = 1024
SWIGLU_LIMIT = 7.0
SWIGLU_ALPHA = 1.702
NORM_EPS = 1e-6
N_MOD = 6
XBC_W = D_SSD + 2 * GROUPS * STATE
XB_W = D_SSD + GROUPS * STATE
LANES = 128

Z0 = 0
X0 = Z0 + D_SSD
B0 = X0 + D_SSD
C0 = B0 + GROUPS * STATE
DT0 = C0 + GROUPS * STATE
SC0 = DT0 + 2 * HEADS

TOK_TILE = 512
OUT_TILE = 1024
MOE_BM = 512
MOE_PARTS = 2
RT_TILE = 512
CTX_BATCH = 4
DISP_TILE = 512
GATHER_TILE = 256
DT_COPIES = 3
DT_LOG0 = 2 * HEADS
DT_DA0 = 4 * HEADS
SUBLANES = 8
PACK_W = D_MODEL // 2
META_IDX = 0
META_GATE = TOP_K
VMEM_LIMIT = 56 * 1024 * 1024


def _silu(v):
    return v * jax.nn.sigmoid(v)


def _softplus(v):
    return jnp.maximum(v, 0.0) + jnp.log1p(jnp.exp(-jnp.abs(v)))


def _rms(v):
    return v * lax.rsqrt(jnp.mean(v * v, axis=-1, keepdims=True) + NORM_EPS)


def _dot(a, b):
    return jnp.dot(a, b, preferred_element_type=F32)


def _expand2(v, e2):
    hi = v.astype(BF16)
    lo = (v - hi.astype(F32)).astype(BF16)
    return _dot(jnp.concatenate([hi, lo], axis=1), e2)


def _mod_kernel(c_ref, w_ref, b_ref, o_ref):
    o_ref[...] = jnp.dot(_silu(c_ref[...]), w_ref[0], precision=HIGHEST,
                         preferred_element_type=F32) + b_ref[...]


def _mod(cvec, w_mod, b_mod, layer):
    rows = cvec.shape[0]
    n = w_mod.shape[2]
    tn = 1536
    return pl.pallas_call(
        _mod_kernel,
        out_shape=jax.ShapeDtypeStruct((rows, n), F32),
        grid=(n // tn,),
        in_specs=[pl.BlockSpec((rows, D_MODEL), lambda j: (0, 0)),
                  pl.BlockSpec((1, D_MODEL, tn), lambda j: (layer, 0, j)),
                  pl.BlockSpec((1, tn), lambda j: (0, j))],
        out_specs=pl.BlockSpec((rows, tn), lambda j: (0, j)),
        compiler_params=pltpu.CompilerParams(dimension_semantics=("arbitrary",),
                                             vmem_limit_bytes=VMEM_LIMIT),
        name="mod",
    )(cvec, w_mod, b_mod)


def _ctx_kernel(ctx_ref, mod_ref, g1_ref, wxb_ref, wdt_ref, cw_ref, cb_ref, dtb_ref, alog_ref, e64_ref,
                h0_ref):
    L = CTX_LEN
    nb = ctx_ref.shape[0]
    m = mod_ref[0]
    hc = _rms(ctx_ref[...].reshape(nb * L, D_MODEL)) * g1_ref[...] * (1.0 + m[1:2]) + m[0:1]
    hb = hc.astype(BF16)
    pxb = _dot(hb, wxb_ref[...])
    dtr = _dot(hb, wdt_ref[...])
    rowl = lax.broadcasted_iota(jnp.int32, (nb * L, XB_W), 0) & (L - 1)
    dn = jnp.where(rowl == 0, 0.0, pltpu.roll(pxb, 1, 0))
    up = jnp.where(rowl == L - 1, 0.0, pltpu.roll(pxb, nb * L - 1, 0))
    cw = cw_ref[...]
    xb = _silu(cw[0:1] * dn + cw[1:2] * pxb + cw[2:3] * up + cb_ref[...])
    dt_all = _softplus(dtr + dtb_ref[...])
    da_all = dt_all * (-jnp.exp(alog_ref[...]))
    ri = lax.broadcasted_iota(jnp.int32, (L, L), 0)
    ci = lax.broadcasted_iota(jnp.int32, (L, L), 1)
    for bi in range(nb):
        rs = slice(bi * L, (bi + 1) * L)
        xs = xb[rs, :D_SSD]
        bm = xb[rs, D_SSD:].astype(BF16)
        dt = dt_all[rs]
        da = da_all[rs]
        for d in range(2):
            tri = (ci <= ri) if d == 0 else (ci >= ri)
            cum = jnp.dot(tri.astype(F32), da, precision=HIGHEST, preferred_element_type=F32)
            last = cum[L - 1:L] if d == 0 else cum[0:1]
            w_e = _expand2(jnp.exp(last - cum) * dt, e64_ref[d])
            xw = (xs * w_e).astype(BF16)
            for g in range(GROUPS):
                gw = D_SSD // GROUPS
                st = lax.dot_general(bm[:, g * STATE:(g + 1) * STATE], xw[:, g * gw:(g + 1) * gw],
                                     (((0,), (0,)), ((), ())), preferred_element_type=F32)
                h0_ref[bi, d, :, g * gw:(g + 1) * gw] = st


def _ctx_states(ctx, mod3, g1, wxb, wdt, cw, cb, dtb, alog, e64):
    bsz = ctx.shape[0]
    mod_row = bsz
    nb = CTX_BATCH if bsz % CTX_BATCH == 0 else 1
    const = lambda *shape: pl.BlockSpec(shape, lambda b: (0,) * len(shape))
    return pl.pallas_call(
        _ctx_kernel,
        out_shape=jax.ShapeDtypeStruct((bsz, 2, STATE, D_SSD), F32),
        grid=(bsz // nb,),
        in_specs=[pl.BlockSpec((nb, CTX_LEN, D_MODEL), lambda b: (b, 0, 0)),
                  pl.BlockSpec((1, N_MOD, D_MODEL), lambda b: (mod_row, 0, 0)),
                  const(1, D_MODEL), const(D_MODEL, XB_W), const(D_MODEL, LANES),
                  const(3, XB_W), const(1, XB_W), const(1, LANES), const(1, LANES),
                  const(2, 2 * LANES, D_SSD)],
        out_specs=pl.BlockSpec((nb, 2, STATE, D_SSD), lambda b: (b, 0, 0, 0)),
        compiler_params=pltpu.CompilerParams(dimension_semantics=("arbitrary",),
                                             vmem_limit_bytes=VMEM_LIMIT),
        name="ctx_states",
    )(ctx, mod3, g1, wxb, wdt, cw, cb, dtb, alog, e64)


def _inproj_kernel(x_ref, mod_ref, g1_ref, wz_ref, wxbc_ref, wdt_ref, dtb_ref, alog_ref, wb_ref, wc_ref, wu_ref,
                   z_ref, xbc_ref, dt_ref, scb_ref, v_ref):
    m = mod_ref[0]
    hx = _rms(x_ref[...]) * g1_ref[...] * (1.0 + m[1:2]) + m[0:1]
    hb = hx.astype(BF16)
    z_ref[...] = _dot(hb, wz_ref[...]).astype(BF16)
    xbc_ref[...] = _dot(hb, wxbc_ref[...]).astype(BF16)
    dt = _softplus(_dot(hb, wdt_ref[...]) + dtb_ref[...])
    lane = lax.broadcasted_iota(jnp.int32, dt.shape, 1)
    dt_ref[...] = jnp.where(lane < DT_LOG0, dt,
                            jnp.where(lane < DT_DA0, jnp.log(dt), dt * (-jnp.exp(alog_ref[...]))))
    scb_ref[...] = _dot(hb, wb_ref[...]).astype(BF16)
    v_ref[...] = (_dot(hb, wc_ref[...]) * _dot(hb, wu_ref[...])).astype(BF16)


def _inproj(x2, mod3, g1, wz, wxbc, wdt, dtb, alog, wb, wc, wu):
    t = x2.shape[0]
    tm = TOK_TILE
    per_b = SEQ // tm
    const = lambda *shape: pl.BlockSpec(shape, lambda i: (0,) * len(shape))
    tile = lambda w: pl.BlockSpec((tm, w), lambda i: (i, 0))
    return pl.pallas_call(
        _inproj_kernel,
        out_shape=(jax.ShapeDtypeStruct((t, D_SSD), BF16), jax.ShapeDtypeStruct((t, XBC_W), BF16),
                   jax.ShapeDtypeStruct((t, LANES), F32), jax.ShapeDtypeStruct((t, D_SC), BF16),
                   jax.ShapeDtypeStruct((t, D_SC), BF16)),
        grid=(t // tm,),
        in_specs=[tile(D_MODEL),
                  pl.BlockSpec((1, N_MOD, D_MODEL), lambda i: (i // per_b, 0, 0)),
                  const(1, D_MODEL), const(D_MODEL, D_SSD), const(D_MODEL, XBC_W), const(D_MODEL, LANES),
                  const(1, LANES), const(1, LANES),
                  const(D_MODEL, D_SC), const(D_MODEL, D_SC), const(D_MODEL, D_SC)],
        out_specs=(tile(D_SSD), tile(XBC_W), tile(LANES), tile(D_SC), tile(D_SC)),
        compiler_params=pltpu.CompilerParams(dimension_semantics=("arbitrary",),
                                             vmem_limit_bytes=VMEM_LIMIT),
        name="inproj",
    )(x2, mod3, g1, wz, wxbc, wdt, dtb, alog, wb, wc, wu)


def _ssd_kernel(xbc_ref, z_ref, dt_ref, h0_ref, cw_ref, cb_ref, dsk_ref, g_ref,
                e64_ref, e128_ref, o_ref, xc_ref, y_ref, s_ref):
    Q = CHUNK
    nck = SEQ // Q
    gw = D_SSD // GROUPS

    rowi = lax.broadcasted_iota(jnp.int32, (SUBLANES, XBC_W), 0)

    def conv_body(c, carry):
        r0 = pl.multiple_of(c * Q, Q)
        main = xbc_ref[0, pl.ds(r0, Q), :].astype(F32)
        pstart = pl.multiple_of(jnp.maximum(r0 - 16, 0), 16)
        nstart = pl.multiple_of(jnp.minimum(r0 + Q, SEQ - 16), 16)
        prev = xbc_ref[0, pl.ds(pstart, 16), :].astype(F32)[15:16]
        nxt = xbc_ref[0, pl.ds(nstart, 16), :].astype(F32)[0:1]
        prev = jnp.where(c > 0, prev, 0.0)
        nxt = jnp.where(c < nck - 1, nxt, 0.0)
        dn = pltpu.roll(main, 1, 0)
        up = pltpu.roll(main, Q - 1, 0)
        dn = jnp.concatenate([jnp.where(rowi == 0, prev, dn[0:SUBLANES]), dn[SUBLANES:]], axis=0)
        up = jnp.concatenate([up[:Q - SUBLANES], jnp.where(rowi == SUBLANES - 1, nxt, up[Q - SUBLANES:])], axis=0)
        cw = cw_ref[...]
        conv = cw[0:1] * dn + cw[1:2] * main + cw[2:3] * up + cb_ref[...]
        xc_ref[pl.ds(r0, Q), :] = _silu(conv).astype(BF16)
        return carry

    lax.fori_loop(0, nck, conv_body, 0)

    ri = lax.broadcasted_iota(jnp.int32, (Q, Q), 0)
    ci = lax.broadcasted_iota(jnp.int32, (Q, Q), 1)
    lane = lax.broadcasted_iota(jnp.int32, (Q, LANES), 1)
    da_lanes = jnp.logical_and(lane >= DT_DA0, lane < DT_DA0 + 2 * HEADS)

    def chunk(c, d, first):
        r0 = pl.multiple_of(c * Q, Q)
        rows = pl.ds(r0, Q)
        xs_b = xc_ref[rows, 0:D_SSD]
        xs = xs_b.astype(F32)
        bm = xc_ref[rows, D_SSD:D_SSD + GROUPS * STATE]
        cm = xc_ref[rows, D_SSD + GROUPS * STATE:XBC_W]
        dtp = dt_ref[0, rows, :]
        da = jnp.where(da_lanes, dtp, 0.0)
        dt = pltpu.roll(dtp, DT_DA0, 1)
        log_dt = pltpu.roll(dtp, DT_DA0 - DT_LOG0, 1)
        tri = (ci <= ri) if d == 0 else (ci >= ri)
        p0 = da.astype(BF16)
        r1 = da - p0.astype(F32)
        p1 = r1.astype(BF16)
        p2 = (r1 - p1.astype(F32)).astype(BF16)
        tri_b = jnp.where(tri, 1.0, 0.0).astype(BF16)
        cum = _dot(jnp.concatenate([tri_b, tri_b, tri_b], axis=1),
                   jnp.concatenate([p0, p1, p2], axis=0))
        sub_t = (cum - log_dt).T
        last = cum[Q - 1:Q] if d == 0 else cum[0:1]
        ecum_e = _expand2(jnp.exp(cum), e64_ref[d])
        w_e = _expand2(jnp.where(da_lanes, jnp.exp(last - cum) * dt, 0.0), e64_ref[d])
        colb = _expand2(cum, e128_ref[d])
        decay_e = ecum_e[Q - 1:Q] if d == 0 else ecum_e[0:1]

        gmat = [lax.dot_general(cm[:, g * STATE:(g + 1) * STATE], bm[:, g * STATE:(g + 1) * STATE],
                                (((1,), (1,)), ((), ())), preferred_element_type=F32)
                for g in range(GROUPS)]
        zero_b = jnp.zeros((Q, LANES), BF16)
        y_parts = []
        for p in range(HEADS // 2):
            g = (2 * p) // (HEADS // GROUPS)
            ms = []
            for hh in (2 * p, 2 * p + 1):
                src = DT_DA0 + HEADS * d + hh
                seg = colb[:, hh * LANES:(hh + 1) * LANES] - sub_t[src:src + 1, :]
                ms.append((jnp.where(tri, jnp.exp(seg), 0.0) * gmat[g]).astype(BF16))
            mcat = jnp.concatenate(ms, axis=1)
            xp = xs_b[:, p * LANES:(p + 1) * LANES]
            rhs = jnp.concatenate([jnp.where(lane < HEAD_DIM, xp, zero_b),
                                   jnp.where(lane >= HEAD_DIM, xp, zero_b)], axis=0)
            y_parts.append(_dot(mcat, rhs))
        y_diag = jnp.concatenate(y_parts, axis=1)

        s_old = s_ref[d]
        s_bf = s_old.astype(BF16)
        y_off = jnp.concatenate(
            [_dot(cm[:, g * STATE:(g + 1) * STATE], s_bf[:, g * gw:(g + 1) * gw]) for g in range(GROUPS)],
            axis=1)
        y = y_diag + y_off * ecum_e

        xw = (xs * w_e).astype(BF16)
        upd = jnp.concatenate(
            [lax.dot_general(bm[:, g * STATE:(g + 1) * STATE], xw[:, g * gw:(g + 1) * gw],
                             (((0,), (0,)), ((), ())), preferred_element_type=F32) for g in range(GROUPS)],
            axis=1)
        s_ref[d] = s_old * decay_e + upd

        if first:
            y_ref[rows, :] = y + dsk_ref[...] * xs
        else:
            tot = y_ref[rows, :] + y
            zz = z_ref[0, rows, :].astype(F32)
            gz = tot * _silu(zz)
            outs = []
            for g in range(GROUPS):
                gg = gz[:, g * gw:(g + 1) * gw]
                outs.append(gg * lax.rsqrt(jnp.mean(gg * gg, axis=-1, keepdims=True) + NORM_EPS))
            o_ref[0, rows, :] = (jnp.concatenate(outs, axis=1) * g_ref[...]).astype(BF16)

    s_ref[...] = h0_ref[0]

    def first_half(i, carry):
        chunk(i, 0, True)
        chunk(nck - 1 - i, 1, True)
        return carry

    def second_half(i, carry):
        chunk(i, 0, False)
        chunk(nck - 1 - i, 1, False)
        return carry

    lax.fori_loop(0, nck // 2, first_half, 0)
    lax.fori_loop(nck // 2, nck, second_half, 0)


def _ssd(xbc3, z3, dt3, h0, cw, cb, dsk, g, e64, e128):
    bsz = xbc3.shape[0]
    const = lambda *shape: pl.BlockSpec(shape, lambda b: (0,) * len(shape))
    seq = lambda w: pl.BlockSpec((1, SEQ, w), lambda b: (b, 0, 0))
    return pl.pallas_call(
        _ssd_kernel,
        out_shape=jax.ShapeDtypeStruct((bsz, SEQ, D_SSD), BF16),
        grid=(bsz,),
        in_specs=[seq(XBC_W), seq(D_SSD), seq(LANES),
                  pl.BlockSpec((1, 2, STATE, D_SSD), lambda b: (b, 0, 0, 0)),
                  const(3, XBC_W), const(1, XBC_W),
                  const(1, D_SSD), const(1, D_SSD), const(2, 2 * LANES, D_SSD),
                  const(2, 2 * LANES, HEADS * LANES)],
        out_specs=seq(D_SSD),
        scratch_shapes=[pltpu.VMEM((SEQ, XBC_W), BF16), pltpu.VMEM((SEQ, D_SSD), F32),
                        pltpu.VMEM((2, STATE, D_SSD), F32)],
        compiler_params=pltpu.CompilerParams(dimension_semantics=("arbitrary",),
                                             vmem_limit_bytes=VMEM_LIMIT),
        name="ssd",
    )(xbc3, z3, dt3, h0, cw, cb, dsk, g, e64, e128)


def _outproj_kernel(x_ref, yssd_ref, scb_ref, v_ref, vp_ref, vn_ref, mod_ref, scw_ref, wo1_ref, wo2_ref,
                    g2_ref, wrt_ref, br_ref, x1_ref, lg_ref):
    tm = OUT_TILE
    per_b = SEQ // tm
    i = pl.program_id(0)
    first = (i % per_b) == 0
    last = (i % per_b) == per_b - 1
    m = mod_ref[0]
    v = v_ref[...].astype(F32)
    vp = jnp.where(first, 0.0, vp_ref[...].astype(F32))
    vn = jnp.where(last, 0.0, vn_ref[...].astype(F32))
    dn = jnp.concatenate([vp, v[:tm - GRID_W]], axis=0)
    up = jnp.concatenate([v[GRID_W:], vn], axis=0)
    scw = scw_ref[...]
    ysc = scb_ref[...].astype(F32) * (scw[0:1] * dn + scw[1:2] * v + scw[2:3] * up)
    out = _dot(yssd_ref[...], wo1_ref[...]) + _dot(ysc.astype(BF16), wo2_ref[...])
    x1 = x_ref[...] + m[2:3] * out
    x1_ref[...] = x1
    h2 = _rms(x1) * g2_ref[...] * (1.0 + m[4:5]) + m[3:4]
    h_hi = h2.astype(BF16)
    h_lo = (h2 - h_hi.astype(F32)).astype(BF16)
    lg_ref[...] = lax.dot_general(wrt_ref[...], jnp.concatenate([h_hi, h_lo, h_hi], axis=1),
                                  (((1,), (1,)), ((), ())), preferred_element_type=F32) + br_ref[...]


def _outproj(x2, yssd, scb, v, mod3, scw, wo1, wo2, g2, wrt, br):
    t = x2.shape[0]
    tm = OUT_TILE
    per_b = SEQ // tm
    r = tm // GRID_W
    nrow = t // GRID_W
    const = lambda *shape: pl.BlockSpec(shape, lambda i: (0,) * len(shape))
    tile = lambda w: pl.BlockSpec((tm, w), lambda i: (i, 0))
    return pl.pallas_call(
        _outproj_kernel,
        out_shape=(jax.ShapeDtypeStruct((t, D_MODEL), F32), jax.ShapeDtypeStruct((N_EXPERTS, t), F32)),
        grid=(t // tm,),
        in_specs=[tile(D_MODEL), tile(D_SSD), tile(D_SC), tile(D_SC),
                  pl.BlockSpec((GRID_W, D_SC), lambda i: (jnp.maximum(i * r - 1, 0), 0)),
                  pl.BlockSpec((GRID_W, D_SC), lambda i: (jnp.minimum((i + 1) * r, nrow - 1), 0)),
                  pl.BlockSpec((1, N_MOD, D_MODEL), lambda i: (i // per_b, 0, 0)),
                  const(3, D_SC), const(D_SSD, D_MODEL), const(D_SC, D_MODEL), const(1, D_MODEL),
                  const(N_EXPERTS, 3 * D_MODEL), const(N_EXPERTS, 1)],
        out_specs=(tile(D_MODEL), pl.BlockSpec((N_EXPERTS, tm), lambda i: (0, i))),
        compiler_params=pltpu.CompilerParams(dimension_semantics=("arbitrary",),
                                             vmem_limit_bytes=VMEM_LIMIT),
        name="outproj",
    )(x2, yssd, scb, v, v, v, mod3, scw, wo1, wo2, g2, wrt, br)


def _route_kernel(lg_ref, dest_ref, gate_ref, idx_ref, meta_ref, rank_ref, carry_ref, *, n_tok, n_blocks):
    tt = RT_TILE
    ne = N_EXPERTS
    eio = lax.broadcasted_iota(jnp.int32, (ne, tt), 0)
    si = lax.broadcasted_iota(jnp.int32, (tt, tt), 0)
    ti = lax.broadcasted_iota(jnp.int32, (tt, tt), 1)
    before = (si < ti).astype(BF16)
    carry_ref[...] = jnp.zeros_like(carry_ref)

    def tile_body(j, c):
        t0 = pl.multiple_of(j * tt, tt)
        l = lg_ref[:, pl.ds(t0, tt)]
        onehot = jnp.zeros((ne, tt), F32)
        tops, sels = [], []
        for _ in range(TOP_K):
            mx = jnp.max(l, axis=0, keepdims=True)
            idx = jnp.min(jnp.where(l == mx, eio, ne), axis=0, keepdims=True)
            sel = eio == idx
            l = jnp.where(sel, -jnp.inf, l)
            onehot = onehot + sel.astype(F32)
            tops.append(mx)
            sels.append(sel)
            idx_ref[pl.ds(len(tops) - 1, 1), pl.ds(t0, tt)] = idx
        ex = [jnp.exp(tv - tops[0]) for tv in tops]
        den = ex[0] + ex[1] + ex[2] + ex[3]
        prefix = _dot(onehot.astype(BF16), before) + carry_ref[:, 0:1]
        for k in range(TOP_K):
            gate_ref[pl.ds(k, 1), pl.ds(t0, tt)] = ex[k] / den
            rk = jnp.sum(jnp.where(sels[k], prefix, 0.0), axis=0, keepdims=True)
            rank_ref[pl.ds(k, 1), pl.ds(t0, tt)] = rk.astype(jnp.int32)
        carry_ref[...] = carry_ref[...] + jnp.sum(onehot, axis=1, keepdims=True)
        return c

    lax.fori_loop(0, n_tok // tt, tile_body, 0)

    counts = carry_ref[...]
    padded = jnp.floor((counts + (MOE_BM - 1)) * (1.0 / MOE_BM)) * MOE_BM
    er = lax.broadcasted_iota(jnp.int32, (ne, ne), 0)
    ec = lax.broadcasted_iota(jnp.int32, (ne, ne), 1)
    pad_start = jnp.dot((ec < er).astype(F32), padded, precision=HIGHEST, preferred_element_type=F32)
    pad_end = pad_start + padded

    def dest_body(j, c):
        t0 = pl.multiple_of(j * tt, tt)
        for k in range(TOP_K):
            idx = idx_ref[pl.ds(k, 1), pl.ds(t0, tt)]
            base = jnp.sum(jnp.where(eio == idx, pad_start[:, 0:1], 0.0), axis=0, keepdims=True)
            dest_ref[pl.ds(k, 1), pl.ds(t0, tt)] = base.astype(jnp.int32) + rank_ref[pl.ds(k, 1), pl.ds(t0, tt)]
        return c

    lax.fori_loop(0, n_tok // tt, dest_body, 0)

    width = meta_ref.shape[1]
    sub = lax.broadcasted_iota(jnp.int32, (ne, width), 0)
    lan = lax.broadcasted_iota(jnp.int32, (ne, width), 1)
    diag = sub == lan
    cnt_row = jnp.sum(jnp.where(diag, counts[:, 0:1], 0.0), axis=0, keepdims=True)
    start_row = jnp.sum(jnp.where(diag, pad_start[:, 0:1], 0.0), axis=0, keepdims=True)
    blk_start = (lan * MOE_BM).astype(F32)
    blk_exp = jnp.sum((pad_end[:, 0:1] <= blk_start).astype(F32), axis=0, keepdims=True)
    blk_exp = jnp.minimum(blk_exp, float(ne - 1))
    used = jnp.sum(padded[:, 0:1], axis=0, keepdims=True) * (1.0 / MOE_BM)
    meta_ref[0:1, :] = cnt_row.astype(jnp.int32)
    meta_ref[1:2, :] = start_row.astype(jnp.int32)
    meta_ref[2:3, :] = blk_exp.astype(jnp.int32)
    meta_ref[3:4, :] = jnp.broadcast_to(used, (1, width)).astype(jnp.int32)
    meta_ref[4:8, :] = jnp.zeros((4, width), jnp.int32)


def _route(lgt, n_blocks):
    ne, n_tok = lgt.shape
    width = -(-n_blocks // LANES) * LANES
    full = lambda *shape: pl.BlockSpec(shape, lambda: (0,) * len(shape))
    return pl.pallas_call(
        functools.partial(_route_kernel, n_tok=n_tok, n_blocks=n_blocks),
        out_shape=(jax.ShapeDtypeStruct((TOP_K, n_tok), jnp.int32),
                   jax.ShapeDtypeStruct((TOP_K, n_tok), F32),
                   jax.ShapeDtypeStruct((TOP_K, n_tok), jnp.int32),
                   jax.ShapeDtypeStruct((8, width), jnp.int32)),
        in_specs=[full(ne, n_tok)],
        out_specs=(full(TOP_K, n_tok), full(TOP_K, n_tok), full(TOP_K, n_tok), full(8, width)),
        scratch_shapes=[pltpu.VMEM((TOP_K, n_tok), jnp.int32), pltpu.VMEM((ne, LANES), F32)],
        compiler_params=pltpu.CompilerParams(vmem_limit_bytes=VMEM_LIMIT),
        name="route",
    )(lgt)


def _dispatch_kernel(dest_ref, cnt_ref, start_ref, nu_ref, x1_ref, meta_ref, mod_ref, g2_ref, zsrc_ref, xs_ref,
                     hbuf, sem, zsem):
    i = pl.program_id(0)
    n = pl.num_programs(0)
    tl = DISP_TILE
    slot = i % 2
    nb = xs_ref.shape[0] // MOE_BM

    def zero_block(b):
        return pltpu.make_async_copy(zsrc_ref, xs_ref.at[pl.ds(b * MOE_BM, MOE_BM)], zsem)

    @pl.when(i == 0)
    def _():
        def start_e(e, c):
            @pl.when(cnt_ref[e] > 0)
            def _():
                zero_block((start_ref[e] + cnt_ref[e] - 1) // MOE_BM).start()
            return c

        def wait_e(e, c):
            @pl.when(cnt_ref[e] > 0)
            def _():
                zero_block(0).wait()
            return c

        def start_t(b, c):
            zero_block(b).start()
            return c

        def wait_t(b, c):
            zero_block(0).wait()
            return c

        lax.fori_loop(0, N_EXPERTS, start_e, 0)
        lax.fori_loop(nu_ref[0], nb, start_t, 0)
        lax.fori_loop(0, N_EXPERTS, wait_e, 0)
        lax.fori_loop(nu_ref[0], nb, wait_t, 0)

    m = mod_ref[0]
    h2 = _rms(x1_ref[...]) * g2_ref[...] * (1.0 + m[4:5]) + m[3:4]
    lo = pltpu.bitcast(h2[:, :PACK_W].astype(BF16).astype(F32), jnp.uint32) >> 16
    hi = pltpu.bitcast(h2[:, PACK_W:].astype(BF16).astype(F32), jnp.uint32) & jnp.uint32(0xFFFF0000)
    row = jnp.concatenate([lo | hi, meta_ref[...], jnp.zeros((tl, D_MODEL - PACK_W - LANES), jnp.uint32)], axis=1)
    hbuf[slot] = row.reshape(tl, SUBLANES, LANES)

    def per_tok(t, c):
        tok = i * tl + t
        for k in range(TOP_K):
            d = dest_ref[k * (dest_ref.shape[0] // TOP_K) + tok]
            pltpu.make_async_copy(hbuf.at[slot, t], xs_ref.at[d], sem.at[slot]).start(priority=k % 2)
        return c

    lax.fori_loop(0, tl, per_tok, 0, unroll=8)

    def wait_slot(sl):
        for _ in range(TOP_K):
            pltpu.make_async_copy(hbuf.at[sl], xs_ref.at[pl.ds(0, tl)], sem.at[sl]).wait()

    @pl.when(i > 0)
    def _():
        wait_slot(1 - slot)

    @pl.when(i == n - 1)
    def _():
        wait_slot(slot)


def _dispatch(dest_flat, cnt, start, n_used, x1, meta_rows, mod3, g2, zsrc, n_rows):
    n_tok = x1.shape[0]
    tl = DISP_TILE
    per_b = SEQ // tl
    return pl.pallas_call(
        _dispatch_kernel,
        out_shape=jax.ShapeDtypeStruct((n_rows, SUBLANES, LANES), jnp.uint32),
        grid_spec=pltpu.PrefetchScalarGridSpec(
            num_scalar_prefetch=4,
            grid=(n_tok // tl,),
            in_specs=[pl.BlockSpec((tl, D_MODEL), lambda i, *_: (i, 0)),
                      pl.BlockSpec((tl, LANES), lambda i, *_: (i, 0)),
                      pl.BlockSpec((1, N_MOD, D_MODEL), lambda i, *_: (i // per_b, 0, 0)),
                      pl.BlockSpec((1, D_MODEL), lambda i, *_: (0, 0)),
                      pl.BlockSpec((MOE_BM, SUBLANES, LANES), lambda i, *_: (0, 0, 0))],
            out_specs=pl.BlockSpec(memory_space=pl.ANY),
            scratch_shapes=[pltpu.VMEM((2, tl, SUBLANES, LANES), jnp.uint32),
                            pltpu.SemaphoreType.DMA((2,)), pltpu.SemaphoreType.DMA]),
        compiler_params=pltpu.CompilerParams(dimension_semantics=("arbitrary",),
                                             vmem_limit_bytes=VMEM_LIMIT),
        name="dispatch",
    )(dest_flat, cnt, start, n_used, x1, meta_rows, mod3, g2, zsrc)


def _expert_rows_kernel(be_ref, nu_ref, xs_ref, wgu_hbm, bgu_ref, wd_hbm, bd_ref, ys_ref,
                        wgu_raw, wd_raw, wgu_bf, wd_bf, w_sem):
    j = pl.program_id(0)
    nu = nu_ref[0]
    n_lt = D_MODEL // LANES

    def weight_copies(e):
        return (pltpu.make_async_copy(wgu_hbm.at[e], wgu_raw, w_sem.at[0]),
                pltpu.make_async_copy(wd_hbm.at[e], wd_raw, w_sem.at[1]))

    @pl.when(j == 0)
    def _():
        for w in weight_copies(be_ref[0]):
            w.start()

    e_now = be_ref[j]
    new_expert = jnp.logical_or(j == 0, be_ref[jnp.maximum(j - 1, 0)] != e_now)

    @pl.when(jnp.logical_and(j < nu, new_expert))
    def _():
        for w in weight_copies(e_now):
            w.wait()
        n_cc = 8
        for c in range(n_cc):
            cc = slice(c * (2 * D_FF // n_cc), (c + 1) * (2 * D_FF // n_cc))
            wgu_bf[:, cc] = wgu_raw[:, cc].astype(BF16)
        for c in range(n_cc // 2):
            cc = slice(c * (2 * D_MODEL // n_cc), (c + 1) * (2 * D_MODEL // n_cc))
            wd_bf[:, cc] = wd_raw[:, cc].astype(BF16)
        j_next = lax.while_loop(lambda t: jnp.logical_and(t < nu, be_ref[jnp.minimum(t, pl.num_programs(0) - 1)] == e_now),
                                lambda t: t + 1, j + 1)

        @pl.when(j_next < nu)
        def _():
            for w in weight_copies(be_ref[jnp.minimum(j_next, pl.num_programs(0) - 1)]):
                w.start()

    @pl.when(j < nu)
    def _():
        e_f = e_now.astype(F32)
        pr = MOE_BM // MOE_PARTS
        for p in range(MOE_PARTS):
            words = xs_ref[pl.ds(p * pr, pr)].reshape(pr, D_MODEL)
            packed = words[:, 0:PACK_W]
            meta = pltpu.bitcast(words[:, PACK_W:PACK_W + LANES], F32)
            xb = jnp.concatenate(
                [pltpu.bitcast(packed << 16, F32).astype(BF16),
                 pltpu.bitcast(packed & jnp.uint32(0xFFFF0000), F32).astype(BF16)], axis=1)
            gate = jnp.zeros((pr, 1), F32)
            for k in range(TOP_K):
                mk = meta[:, META_IDX + k:META_IDX + k + 1] == e_f
                gate = gate + jnp.where(mk, meta[:, META_GATE + k:META_GATE + k + 1], 0.0)
            gu = _dot(xb, wgu_bf[...]) + bgu_ref[0]
            glu = jnp.minimum(gu[:, :D_FF], SWIGLU_LIMIT)
            lin = jnp.clip(gu[:, D_FF:], -SWIGLU_LIMIT, SWIGLU_LIMIT)
            act = glu * jax.nn.sigmoid(SWIGLU_ALPHA * glu) * (lin + 1.0)
            y = (_dot(act.astype(BF16), wd_bf[...]) + bd_ref[0]) * gate
            ys_ref[pl.ds(p * pr, pr)] = y.reshape(pr, n_lt, LANES)

    @pl.when(j >= nu)
    def _():
        ys_ref[...] = jnp.zeros_like(ys_ref)


def _expert_rows(blk_exp, n_used, xs, wgu, bgu, wd, bd):
    n_rows = xs.shape[0]
    nb = n_rows // MOE_BM
    n_lt = D_MODEL // LANES
    row_blk = lambda j, be, nu: (jnp.minimum(j, nu[0] - 1), 0, 0)
    per_e = lambda j, be, nu: (be[j], 0, 0)
    return pl.pallas_call(
        _expert_rows_kernel,
        out_shape=jax.ShapeDtypeStruct((n_rows, n_lt, LANES), F32),
        grid_spec=pltpu.PrefetchScalarGridSpec(
            num_scalar_prefetch=2,
            grid=(nb,),
            in_specs=[pl.BlockSpec((MOE_BM, SUBLANES, LANES), row_blk),
                      pl.BlockSpec(memory_space=pl.ANY),
                      pl.BlockSpec((1, 1, 2 * D_FF), per_e),
                      pl.BlockSpec(memory_space=pl.ANY),
                      pl.BlockSpec((1, 1, D_MODEL), per_e)],
            out_specs=pl.BlockSpec((MOE_BM, n_lt, LANES), lambda j, be, nu: (j, 0, 0)),
            scratch_shapes=[pltpu.VMEM((D_MODEL, 2 * D_FF), F32), pltpu.VMEM((D_FF, D_MODEL), F32),
                            pltpu.VMEM((D_MODEL, 2 * D_FF), BF16), pltpu.VMEM((D_FF, D_MODEL), BF16),
                            pltpu.SemaphoreType.DMA((2,))]),
        compiler_params=pltpu.CompilerParams(dimension_semantics=("arbitrary",),
                                             vmem_limit_bytes=VMEM_LIMIT),
        name="experts",
    )(blk_exp, n_used, xs, wgu, bgu, wd, bd)


def _gather_combine_kernel(dest_ref, ys_ref, x1_ref, mod_ref, fg_ref, o_ref, buf, sem):
    i = pl.program_id(0)
    n = pl.num_programs(0)
    tc = GATHER_TILE
    slot = i % 2

    def issue(tile, sl):
        def per_tok(t, c):
            tok = tile * tc + t
            for k in range(TOP_K):
                d = dest_ref[k * (dest_ref.shape[0] // TOP_K) + tok]
                pltpu.make_async_copy(ys_ref.at[d], buf.at[sl, k * tc + t], sem.at[sl]).start(priority=k % 2)
            return c

        lax.fori_loop(0, tc, per_tok, 0, unroll=8)

    @pl.when(i == 0)
    def _():
        issue(0, 0)

    @pl.when(i + 1 < n)
    def _():
        issue(i + 1, 1 - slot)

    for _ in range(TOP_K):
        pltpu.make_async_copy(ys_ref.at[pl.ds(0, tc)], buf.at[slot, pl.ds(0, tc)], sem.at[slot]).wait()

    moe = ((buf[slot, pl.ds(0, tc)] + buf[slot, pl.ds(tc, tc)])
           + (buf[slot, pl.ds(2 * tc, tc)] + buf[slot, pl.ds(3 * tc, tc)])).reshape(tc, D_MODEL)
    m = mod_ref[0]
    x2 = x1_ref[...] + m[5:6] * moe
    o_ref[...] = _rms(x2) * fg_ref[...]


def _gather_combine(dest_flat, ys, x1, mod3, fg):
    n_tok = x1.shape[0]
    tc = GATHER_TILE
    per_b = SEQ // tc
    n_lt = D_MODEL // LANES
    return pl.pallas_call(
        _gather_combine_kernel,
        out_shape=jax.ShapeDtypeStruct((n_tok, D_MODEL), F32),
        grid_spec=pltpu.PrefetchScalarGridSpec(
            num_scalar_prefetch=1,
            grid=(n_tok // tc,),
            in_specs=[pl.BlockSpec(memory_space=pl.ANY),
                      pl.BlockSpec((tc, D_MODEL), lambda i, d: (i, 0)),
                      pl.BlockSpec((1, N_MOD, D_MODEL), lambda i, d: (i // per_b, 0, 0)),
                      pl.BlockSpec((1, D_MODEL), lambda i, d: (0, 0))],
            out_specs=pl.BlockSpec((tc, D_MODEL), lambda i, d: (i, 0)),
            scratch_shapes=[pltpu.VMEM((2, TOP_K * tc, n_lt, LANES), F32), pltpu.SemaphoreType.DMA((2,))]),
        compiler_params=pltpu.CompilerParams(dimension_semantics=("arbitrary",),
                                             vmem_limit_bytes=VMEM_LIMIT),
        name="combine",
    )(dest_flat, ys, x1, mod3, fg)


def _expansion_matrices(src0):
    r = (jnp.arange(2 * LANES) % LANES)[:, None]
    out64, out128 = [], []
    for d in range(2):
        l64 = jnp.arange(D_SSD)[None, :]
        l128 = jnp.arange(HEADS * LANES)[None, :]
        out64.append((l64 // HEAD_DIM == r - src0 - HEADS * d).astype(BF16))
        out128.append((l128 // LANES == r - src0 - HEADS * d).astype(BF16))
    return jnp.stack(out64), jnp.stack(out128)


def _pad_lanes(v):
    return jnp.pad(v, [(0, 0)] * (v.ndim - 1) + [(0, LANES - v.shape[-1])])


def kernel(x, c, ctx, c_ctx, w_mod, b_mod, norm1_g, w_in, ssd_conv_w, ssd_conv_b, ssd_dt_bias, ssd_a_log,
           ssd_d, ssd_norm_g, sc_conv_w, w_out, norm2_g, w_router, b_router, w_gate_up, b_gate_up, w_down,
           b_down, final_g):
    bsz = x.shape[0]
    n_tok = bsz * SEQ
    n_assign = n_tok * TOP_K
    n_blocks = n_assign // MOE_BM + N_EXPERTS
    n_rows = n_blocks * MOE_BM
    li = 0

    cvec = jnp.concatenate([c, c_ctx[None, :], jnp.zeros((7, D_MODEL), F32)], axis=0)
    mod3 = _mod(cvec, w_mod, b_mod[li][None, :], li).reshape(bsz + 8, N_MOD, D_MODEL)

    w = w_in[li]
    wz = w[:, Z0:X0].astype(BF16)
    wxbc = w[:, X0:DT0].astype(BF16)
    wdt = _pad_lanes(w[:, DT0:SC0]).astype(BF16)
    wb = w[:, SC0:SC0 + D_SC].astype(BF16)
    wc = w[:, SC0 + D_SC:SC0 + 2 * D_SC].astype(BF16)
    wu = w[:, SC0 + 2 * D_SC:].astype(BF16)
    g1 = norm1_g[li][None, :]
    cw = ssd_conv_w[li]
    cb = ssd_conv_b[li][None, :]
    dtb = _pad_lanes(ssd_dt_bias[li].reshape(1, 2 * HEADS))
    alog = _pad_lanes(ssd_a_log[li].reshape(1, 2 * HEADS))
    e64_ctx, _ = _expansion_matrices(0)

    h0 = _ctx_states(ctx, mod3, g1, wxbc[:, :XB_W], wdt, cw[:, :XB_W], cb[:, :XB_W], dtb, alog, e64_ctx)

    rep = lambda a: _pad_lanes(jnp.tile(a[..., :2 * HEADS], (1, DT_COPIES)))
    e64, e128 = _expansion_matrices(DT_DA0)
    x2 = x.reshape(n_tok, D_MODEL)
    z, xbc, dtp, scb, v = _inproj(x2, mod3, g1, wz, wxbc, rep(wdt), rep(dtb), rep(alog), wb, wc, wu)

    dsk = jnp.repeat(ssd_d[li], HEAD_DIM)[None, :]
    yssd = _ssd(xbc.reshape(bsz, SEQ, XBC_W), z.reshape(bsz, SEQ, D_SSD), dtp.reshape(bsz, SEQ, LANES), h0,
                cw, cb, dsk, ssd_norm_g[li][None, :], e64, e128)

    wo = w_out[li].astype(BF16)
    g2 = norm2_g[li][None, :]
    wr = w_router[li].T
    wr_hi = wr.astype(BF16)
    wr_lo = (wr - wr_hi.astype(F32)).astype(BF16)
    x1, lgt = _outproj(x2, yssd.reshape(n_tok, D_SSD), scb, v, mod3, sc_conv_w[li], wo[:D_SSD], wo[D_SSD:],
                       g2, jnp.concatenate([wr_hi, wr_hi, wr_lo], axis=1), b_router[li][:, None])

    dest_t, gate_t, idx_t, meta = _route(lgt, n_blocks)
    dest_flat = dest_t.reshape(n_assign)
    cnt = meta[0, :N_EXPERTS]
    start = meta[1, :N_EXPERTS]
    blk_exp = meta[2, :n_blocks]
    n_used = meta[3, :1]

    meta_rows = lax.bitcast_convert_type(_pad_lanes(jnp.concatenate(
        [idx_t.T.astype(F32), gate_t.T], axis=1)), jnp.uint32)
    pad_meta = lax.bitcast_convert_type(_pad_lanes(jnp.concatenate(
        [jnp.full((MOE_BM, TOP_K), -1.0, F32), jnp.zeros((MOE_BM, TOP_K), F32)], axis=1)), jnp.uint32)
    zsrc = jnp.concatenate([jnp.zeros((MOE_BM, PACK_W), jnp.uint32), pad_meta,
                            jnp.zeros((MOE_BM, D_MODEL - PACK_W - LANES), jnp.uint32)],
                           axis=1).reshape(MOE_BM, SUBLANES, LANES)

    xs = _dispatch(dest_flat, cnt, start, n_used, x1, meta_rows, mod3, g2, zsrc, n_rows)
    ys = _expert_rows(blk_exp, n_used, xs, w_gate_up[li], b_gate_up[li][:, None, :],
                      w_down[li], b_down[li][:, None, :])
    out = _gather_combine(dest_flat, ys, x1, mod3, final_g[None, :])
    return out.reshape(bsz, SEQ, D_MODEL)
```

```python
import functools

import jax
import jax.numpy as jnp
from jax import lax
from jax.experimental import pallas as pl
from jax.experimental.pallas import tpu as pltpu

F32 = jnp.float32
BF16 = jnp.bfloat16
HIGHEST = lax.Precision.HIGHEST

D_MODEL = 1024
SEQ = 2048
CTX_LEN = 256
GRID_W = 64
D_SSD = 1024
D_SC = 1024
HEAD_DIM = 64
HEADS = 16
GROUPS = 2
STATE = 128
CHUNK = 128
N_EXPERTS = 32
TOP_K = 4
D_FF = 1024
SWIGLU_LIMIT = 7.0
SWIGLU_ALPHA = 1.702
NORM_EPS = 1e-6
N_MOD = 6
XBC_W = D_SSD + 2 * GROUPS * STATE
XB_W = D_SSD + GROUPS * STATE
LANES = 128

Z0 = 0
X0 = Z0 + D_SSD
B0 = X0 + D_SSD
C0 = B0 + GROUPS * STATE
DT0 = C0 + GROUPS * STATE
SC0 = DT0 + 2 * HEADS

TOK_TILE = 1024
OUT_TILE = 1024
MOE_BM = 512
MOE_PARTS = 2
RT_TILE = 512
CTX_BATCH = 4
DISP_TILE = 512
GATHER_TILE = 256
DT_COPIES = 3
DT_LOG0 = 2 * HEADS
DT_DA0 = 4 * HEADS
SUBLANES = 8
PACK_W = D_MODEL // 2
META_IDX = 0
META_GATE = TOP_K
VMEM_LIMIT = 56 * 1024 * 1024


def _silu(v):
    return v * jax.nn.sigmoid(v)


def _softplus(v):
    return jnp.maximum(v, 0.0) + jnp.log1p(jnp.exp(-jnp.abs(v)))


def _rms(v):
    return v * lax.rsqrt(jnp.mean(v * v, axis=-1, keepdims=True) + NORM_EPS)


def _dot(a, b):
    return jnp.dot(a, b, preferred_element_type=F32)


def _expand2(v, e2):
    hi = v.astype(BF16)
    lo = (v - hi.astype(F32)).astype(BF16)
    return _dot(jnp.concatenate([hi, lo], axis=1), e2)


def _mod_kernel(c_ref, w_ref, b_ref, o_ref):
    o_ref[...] = jnp.dot(_silu(c_ref[...]), w_ref[0], precision=HIGHEST,
                         preferred_element_type=F32) + b_ref[...]


def _mod(cvec, w_mod, b_mod, layer):
    rows = cvec.shape[0]
    n = w_mod.shape[2]
    tn = 1536
    return pl.pallas_call(
        _mod_kernel,
        out_shape=jax.ShapeDtypeStruct((rows, n), F32),
        grid=(n // tn,),
        in_specs=[pl.BlockSpec((rows, D_MODEL), lambda j: (0, 0)),
                  pl.BlockSpec((1, D_MODEL, tn), lambda j: (layer, 0, j)),
                  pl.BlockSpec((1, tn), lambda j: (0, j))],
        out_specs=pl.BlockSpec((rows, tn), lambda j: (0, j)),
        compiler_params=pltpu.CompilerParams(dimension_semantics=("arbitrary",),
                                             vmem_limit_bytes=VMEM_LIMIT),
        name="mod",
    )(cvec, w_mod, b_mod)


def _ctx_kernel(ctx_ref, mod_ref, g1_ref, wxb_ref, wdt_ref, cw_ref, cb_ref, dtb_ref, alog_ref, e64_ref,
                h0_ref):
    L = CTX_LEN
    nb = ctx_ref.shape[0]
    m = mod_ref[0]
    hc = _rms(ctx_ref[...].reshape(nb * L, D_MODEL)) * g1_ref[...] * (1.0 + m[1:2]) + m[0:1]
    hb = hc.astype(BF16)
    pxb = _dot(hb, wxb_ref[...])
    dtr = _dot(hb, wdt_ref[...])
    rowl = lax.broadcasted_iota(jnp.int32, (nb * L, XB_W), 0) & (L - 1)
    dn = jnp.where(rowl == 0, 0.0, pltpu.roll(pxb, 1, 0))
    up = jnp.where(rowl == L - 1, 0.0, pltpu.roll(pxb, nb * L - 1, 0))
    cw = cw_ref[...]
    xb = _silu(cw[0:1] * dn + cw[1:2] * pxb + cw[2:3] * up + cb_ref[...])
    dt_all = _softplus(dtr + dtb_ref[...])
    da_all = dt_all * (-jnp.exp(alog_ref[...]))
    ri = lax.broadcasted_iota(jnp.int32, (L, L), 0)
    ci = lax.broadcasted_iota(jnp.int32, (L, L), 1)
    for bi in range(nb):
        rs = slice(bi * L, (bi + 1) * L)
        xs = xb[rs, :D_SSD]
        bm = xb[rs, D_SSD:].astype(BF16)
        dt = dt_all[rs]
        da = da_all[rs]
        for d in range(2):
            tri = (ci <= ri) if d == 0 else (ci >= ri)
            cum = jnp.dot(tri.astype(F32), da, precision=HIGHEST, preferred_element_type=F32)
            last = cum[L - 1:L] if d == 0 else cum[0:1]
            w_e = _expand2(jnp.exp(last - cum) * dt, e64_ref[d])
            xw = (xs * w_e).astype(BF16)
            for g in range(GROUPS):
                gw = D_SSD // GROUPS
                st = lax.dot_general(bm[:, g * STATE:(g + 1) * STATE], xw[:, g * gw:(g + 1) * gw],
                                     (((0,), (0,)), ((), ())), preferred_element_type=F32)
                h0_ref[bi, d, :, g * gw:(g + 1) * gw] = st


def _ctx_states(ctx, mod3, g1, wxb, wdt, cw, cb, dtb, alog, e64):
    bsz = ctx.shape[0]
    mod_row = bsz
    nb = CTX_BATCH if bsz % CTX_BATCH == 0 else 1
    const = lambda *shape: pl.BlockSpec(shape, lambda b: (0,) * len(shape))
    return pl.pallas_call(
        _ctx_kernel,
        out_shape=jax.ShapeDtypeStruct((bsz, 2, STATE, D_SSD), F32),
        grid=(bsz // nb,),
        in_specs=[pl.BlockSpec((nb, CTX_LEN, D_MODEL), lambda b: (b, 0, 0)),
                  pl.BlockSpec((1, N_MOD, D_MODEL), lambda b: (mod_row, 0, 0)),
                  const(1, D_MODEL), const(D_MODEL, XB_W), const(D_MODEL, LANES),
                  const(3, XB_W), const(1, XB_W), const(1, LANES), const(1, LANES),
                  const(2, 2 * LANES, D_SSD)],
        out_specs=pl.BlockSpec((nb, 2, STATE, D_SSD), lambda b: (b, 0, 0, 0)),
        compiler_params=pltpu.CompilerParams(dimension_semantics=("arbitrary",),
                                             vmem_limit_bytes=VMEM_LIMIT),
        name="ctx_states",
    )(ctx, mod3, g1, wxb, wdt, cw, cb, dtb, alog, e64)


def _inproj_kernel(x_ref, mod_ref, g1_ref, wz_ref, wxbc_ref, wdt_ref, dtb_ref, alog_ref, wb_ref, wc_ref, wu_ref,
                   z_ref, xbc_ref, dt_ref, scb_ref, v_ref):
    m = mod_ref[0]
    hx = _rms(x_ref[...]) * g1_ref[...] * (1.0 + m[1:2]) + m[0:1]
    hb = hx.astype(BF16)
    z_ref[...] = _dot(hb, wz_ref[...]).astype(BF16)
    xbc_ref[...] = _dot(hb, wxbc_ref[...]).astype(BF16)
    dt = _softplus(_dot(hb, wdt_ref[...]) + dtb_ref[...])
    lane = lax.broadcasted_iota(jnp.int32, dt.shape, 1)
    dt_ref[...] = jnp.where(lane < DT_LOG0, dt,
                            jnp.where(lane < DT_DA0, jnp.log(dt), dt * (-jnp.exp(alog_ref[...]))))
    scb_ref[...] = _dot(hb, wb_ref[...]).astype(BF16)
    v_ref[...] = (_dot(hb, wc_ref[...]) * _dot(hb, wu_ref[...])).astype(BF16)


def _inproj(x2, mod3, g1, wz, wxbc, wdt, dtb, alog, wb, wc, wu):
    t = x2.shape[0]
    tm = TOK_TILE
    per_b = SEQ // tm
    const = lambda *shape: pl.BlockSpec(shape, lambda i: (0,) * len(shape), pipeline_mode=pl.Buffered(1))
    tile = lambda w: pl.BlockSpec((tm, w), lambda i: (i, 0))
    return pl.pallas_call(
        _inproj_kernel,
        out_shape=(jax.ShapeDtypeStruct((t, D_SSD), BF16), jax.ShapeDtypeStruct((t, XBC_W), BF16),
                   jax.ShapeDtypeStruct((t, LANES), F32), jax.ShapeDtypeStruct((t, D_SC), BF16),
                   jax.ShapeDtypeStruct((t, D_SC), BF16)),
        grid=(t // tm,),
        in_specs=[tile(D_MODEL),
                  pl.BlockSpec((1, N_MOD, D_MODEL), lambda i: (i // per_b, 0, 0)),
                  const(1, D_MODEL), const(D_MODEL, D_SSD), const(D_MODEL, XBC_W), const(D_MODEL, LANES),
                  const(1, LANES), const(1, LANES),
                  const(D_MODEL, D_SC), const(D_MODEL, D_SC), const(D_MODEL, D_SC)],
        out_specs=(tile(D_SSD), tile(XBC_W), tile(LANES), tile(D_SC), tile(D_SC)),
        compiler_params=pltpu.CompilerParams(dimension_semantics=("arbitrary",),
                                             vmem_limit_bytes=VMEM_LIMIT),
        name="inproj",
    )(x2, mod3, g1, wz, wxbc, wdt, dtb, alog, wb, wc, wu)


def _ssd_kernel(xbc_ref, z_ref, dt_ref, h0_ref, cw_ref, cb_ref, dsk_ref, g_ref,
                e64_ref, e128_ref, o_ref, xc_ref, y_ref, s_ref):
    Q = CHUNK
    nck = SEQ // Q
    gw = D_SSD // GROUPS

    rowi = lax.broadcasted_iota(jnp.int32, (SUBLANES, XBC_W), 0)

    def conv_body(c, carry):
        r0 = pl.multiple_of(c * Q, Q)
        main = xbc_ref[0, pl.ds(r0, Q), :].astype(F32)
        pstart = pl.multiple_of(jnp.maximum(r0 - 16, 0), 16)
        nstart = pl.multiple_of(jnp.minimum(r0 + Q, SEQ - 16), 16)
        prev = xbc_ref[0, pl.ds(pstart, 16), :].astype(F32)[15:16]
        nxt = xbc_ref[0, pl.ds(nstart, 16), :].astype(F32)[0:1]
        prev = jnp.where(c > 0, prev, 0.0)
        nxt = jnp.where(c < nck - 1, nxt, 0.0)
        dn = pltpu.roll(main, 1, 0)
        up = pltpu.roll(main, Q - 1, 0)
        dn = jnp.concatenate([jnp.where(rowi == 0, prev, dn[0:SUBLANES]), dn[SUBLANES:]], axis=0)
        up = jnp.concatenate([up[:Q - SUBLANES], jnp.where(rowi == SUBLANES - 1, nxt, up[Q - SUBLANES:])], axis=0)
        cw = cw_ref[...]
        conv = cw[0:1] * dn + cw[1:2] * main + cw[2:3] * up + cb_ref[...]
        xc_ref[pl.ds(r0, Q), :] = _silu(conv).astype(BF16)
        return carry

    lax.fori_loop(0, nck, conv_body, 0)

    ri = lax.broadcasted_iota(jnp.int32, (Q, Q), 0)
    ci = lax.broadcasted_iota(jnp.int32, (Q, Q), 1)
    lane = lax.broadcasted_iota(jnp.int32, (Q, LANES), 1)
    da_lanes = jnp.logical_and(lane >= DT_DA0, lane < DT_DA0 + 2 * HEADS)

    def chunk(c, d, first):
        r0 = pl.multiple_of(c * Q, Q)
        rows = pl.ds(r0, Q)
        xs_b = xc_ref[rows, 0:D_SSD]
        xs = xs_b.astype(F32)
        bm = xc_ref[rows, D_SSD:D_SSD + GROUPS * STATE]
        cm = xc_ref[rows, D_SSD + GROUPS * STATE:XBC_W]
        dtp = dt_ref[0, rows, :]
        da = jnp.where(da_lanes, dtp, 0.0)
        dt = pltpu.roll(dtp, DT_DA0, 1)
        log_dt = pltpu.roll(dtp, DT_DA0 - DT_LOG0, 1)
        tri = (ci <= ri) if d == 0 else (ci >= ri)
        p0 = da.astype(BF16)
        r1 = da - p0.astype(F32)
        p1 = r1.astype(BF16)
        p2 = (r1 - p1.astype(F32)).astype(BF16)
        tri_b = jnp.where(tri, 1.0, 0.0).astype(BF16)
        cum = _dot(jnp.concatenate([tri_b, tri_b, tri_b], axis=1),
                   jnp.concatenate([p0, p1, p2], axis=0))
        sub_t = (cum - log_dt).T
        last = cum[Q - 1:Q] if d == 0 else cum[0:1]
        ecum_e = _expand2(jnp.exp(cum), e64_ref[d])
        w_e = _expand2(jnp.where(da_lanes, jnp.exp(last - cum) * dt, 0.0), e64_ref[d])
        colb = _expand2(cum, e128_ref[d])
        decay_e = ecum_e[Q - 1:Q] if d == 0 else ecum_e[0:1]

        gmat = [lax.dot_general(cm[:, g * STATE:(g + 1) * STATE], bm[:, g * STATE:(g + 1) * STATE],
                                (((1,), (1,)), ((), ())), preferred_element_type=F32)
                for g in range(GROUPS)]
        zero_b = jnp.zeros((Q, LANES), BF16)
        y_parts = []
        for p in range(HEADS // 2):
            g = (2 * p) // (HEADS // GROUPS)
            ms = []
            for hh in (2 * p, 2 * p + 1):
                src = DT_DA0 + HEADS * d + hh
                seg = colb[:, hh * LANES:(hh + 1) * LANES] - sub_t[src:src + 1, :]
                ms.append((jnp.where(tri, jnp.exp(seg), 0.0) * gmat[g]).astype(BF16))
            mcat = jnp.concatenate(ms, axis=1)
            xp = xs_b[:, p * LANES:(p + 1) * LANES]
            rhs = jnp.concatenate([jnp.where(lane < HEAD_DIM, xp, zero_b),
                                   jnp.where(lane >= HEAD_DIM, xp, zero_b)], axis=0)
            y_parts.append(_dot(mcat, rhs))
        y_diag = jnp.concatenate(y_parts, axis=1)

        s_old = s_ref[d]
        s_bf = s_old.astype(BF16)
        y_off = jnp.concatenate(
            [_dot(cm[:, g * STATE:(g + 1) * STATE], s_bf[:, g * gw:(g + 1) * gw]) for g in range(GROUPS)],
            axis=1)
        y = y_diag + y_off * ecum_e

        xw = (xs * w_e).astype(BF16)
        upd = jnp.concatenate(
            [lax.dot_general(bm[:, g * STATE:(g + 1) * STATE], xw[:, g * gw:(g + 1) * gw],
                             (((0,), (0,)), ((), ())), preferred_element_type=F32) for g in range(GROUPS)],
            axis=1)
        s_ref[d] = s_old * decay_e + upd

        if first:
            y_ref[rows, :] = y + dsk_ref[...] * xs
        else:
            tot = y_ref[rows, :] + y
            zz = z_ref[0, rows, :].astype(F32)
            gz = tot * _silu(zz)
            outs = []
            for g in range(GROUPS):
                gg = gz[:, g * gw:(g + 1) * gw]
                outs.append(gg * lax.rsqrt(jnp.mean(gg * gg, axis=-1, keepdims=True) + NORM_EPS))
            o_ref[0, rows, :] = (jnp.concatenate(outs, axis=1) * g_ref[...]).astype(BF16)

    s_ref[...] = h0_ref[0]

    def first_half(i, carry):
        chunk(i, 0, True)
        chunk(nck - 1 - i, 1, True)
        return carry

    def second_half(i, carry):
        chunk(i, 0, False)
        chunk(nck - 1 - i, 1, False)
        return carry

    lax.fori_loop(0, nck // 2, first_half, 0)
    lax.fori_loop(nck // 2, nck, second_half, 0)


def _ssd(xbc3, z3, dt3, h0, cw, cb, dsk, g, e64, e128):
    bsz = xbc3.shape[0]
    const = lambda *shape: pl.BlockSpec(shape, lambda b: (0,) * len(shape))
    seq = lambda w: pl.BlockSpec((1, SEQ, w), lambda b: (b, 0, 0))
    return pl.pallas_call(
        _ssd_kernel,
        out_shape=jax.ShapeDtypeStruct((bsz, SEQ, D_SSD), BF16),
        grid=(bsz,),
        in_specs=[seq(XBC_W), seq(D_SSD), seq(LANES),
                  pl.BlockSpec((1, 2, STATE, D_SSD), lambda b: (b, 0, 0, 0)),
                  const(3, XBC_W), const(1, XBC_W),
                  const(1, D_SSD), const(1, D_SSD), const(2, 2 * LANES, D_SSD),
                  const(2, 2 * LANES, HEADS * LANES)],
        out_specs=seq(D_SSD),
        scratch_shapes=[pltpu.VMEM((SEQ, XBC_W), BF16), pltpu.VMEM((SEQ, D_SSD), F32),
                        pltpu.VMEM((2, STATE, D_SSD), F32)],
        compiler_params=pltpu.CompilerParams(dimension_semantics=("arbitrary",),
                                             vmem_limit_bytes=VMEM_LIMIT),
        name="ssd",
    )(xbc3, z3, dt3, h0, cw, cb, dsk, g, e64, e128)


def _outproj_kernel(x_ref, yssd_ref, scb_ref, v_ref, vp_ref, vn_ref, mod_ref, scw_ref, wo1_ref, wo2_ref,
                    g2_ref, wrt_ref, br_ref, x1_ref, lg_ref):
    tm = OUT_TILE
    per_b = SEQ // tm
    i = pl.program_id(0)
    first = (i % per_b) == 0
    last = (i % per_b) == per_b - 1
    m = mod_ref[0]
    v = v_ref[...].astype(F32)
    vp = jnp.where(first, 0.0, vp_ref[...].astype(F32))
    vn = jnp.where(last, 0.0, vn_ref[...].astype(F32))
    dn = jnp.concatenate([vp, v[:tm - GRID_W]], axis=0)
    up = jnp.concatenate([v[GRID_W:], vn], axis=0)
    scw = scw_ref[...]
    ysc = scb_ref[...].astype(F32) * (scw[0:1] * dn + scw[1:2] * v + scw[2:3] * up)
    out = _dot(yssd_ref[...], wo1_ref[...]) + _dot(ysc.astype(BF16), wo2_ref[...])
    x1 = x_ref[...] + m[2:3] * out
    x1_ref[...] = x1
    h2 = _rms(x1) * g2_ref[...] * (1.0 + m[4:5]) + m[3:4]
    h_hi = h2.astype(BF16)
    h_lo = (h2 - h_hi.astype(F32)).astype(BF16)
    lg_ref[...] = lax.dot_general(wrt_ref[...], jnp.concatenate([h_hi, h_lo, h_hi], axis=1),
                                  (((1,), (1,)), ((), ())), preferred_element_type=F32) + br_ref[...]


def _outproj(x2, yssd, scb, v, mod3, scw, wo1, wo2, g2, wrt, br):
    t = x2.shape[0]
    tm = OUT_TILE
    per_b = SEQ // tm
    r = tm // GRID_W
    nrow = t // GRID_W
    const = lambda *shape: pl.BlockSpec(shape, lambda i: (0,) * len(shape))
    tile = lambda w: pl.BlockSpec((tm, w), lambda i: (i, 0))
    return pl.pallas_call(
        _outproj_kernel,
        out_shape=(jax.ShapeDtypeStruct((t, D_MODEL), F32), jax.ShapeDtypeStruct((N_EXPERTS, t), F32)),
        grid=(t // tm,),
        in_specs=[tile(D_MODEL), tile(D_SSD), tile(D_SC), tile(D_SC),
                  pl.BlockSpec((GRID_W, D_SC), lambda i: (jnp.maximum(i * r - 1, 0), 0)),
                  pl.BlockSpec((GRID_W, D_SC), lambda i: (jnp.minimum((i + 1) * r, nrow - 1), 0)),
                  pl.BlockSpec((1, N_MOD, D_MODEL), lambda i: (i // per_b, 0, 0)),
                  const(3, D_SC), const(D_SSD, D_MODEL), const(D_SC, D_MODEL), const(1, D_MODEL),
                  const(N_EXPERTS, 3 * D_MODEL), const(N_EXPERTS, 1)],
        out_specs=(tile(D_MODEL), pl.BlockSpec((N_EXPERTS, tm), lambda i: (0, i))),
        compiler_params=pltpu.CompilerParams(dimension_semantics=("arbitrary",),
                                             vmem_limit_bytes=VMEM_LIMIT),
        name="outproj",
    )(x2, yssd, scb, v, v, v, mod3, scw, wo1, wo2, g2, wrt, br)


def _route_kernel(lg_ref, dest_ref, gate_ref, idx_ref, meta_ref, rank_ref, carry_ref, *, n_tok, n_blocks):
    tt = RT_TILE
    ne = N_EXPERTS
    eio = lax.broadcasted_iota(jnp.int32, (ne, tt), 0)
    si = lax.broadcasted_iota(jnp.int32, (tt, tt), 0)
    ti = lax.broadcasted_iota(jnp.int32, (tt, tt), 1)
    before = (si < ti).astype(BF16)
    carry_ref[...] = jnp.zeros_like(carry_ref)

    def tile_body(j, c):
        t0 = pl.multiple_of(j * tt, tt)
        l = lg_ref[:, pl.ds(t0, tt)]
        onehot = jnp.zeros((ne, tt), F32)
        tops, sels = [], []
        for _ in range(TOP_K):
            mx = jnp.max(l, axis=0, keepdims=True)
            idx = jnp.min(jnp.where(l == mx, eio, ne), axis=0, keepdims=True)
            sel = eio == idx
            l = jnp.where(sel, -jnp.inf, l)
            onehot = onehot + sel.astype(F32)
            tops.append(mx)
            sels.append(sel)
            idx_ref[pl.ds(len(tops) - 1, 1), pl.ds(t0, tt)] = idx
        ex = [jnp.exp(tv - tops[0]) for tv in tops]
        den = ex[0] + ex[1] + ex[2] + ex[3]
        prefix = _dot(onehot.astype(BF16), before) + carry_ref[:, 0:1]
        for k in range(TOP_K):
            gate_ref[pl.ds(k, 1), pl.ds(t0, tt)] = ex[k] / den
            rk = jnp.sum(jnp.where(sels[k], prefix, 0.0), axis=0, keepdims=True)
            rank_ref[pl.ds(k, 1), pl.ds(t0, tt)] = rk.astype(jnp.int32)
        carry_ref[...] = carry_ref[...] + jnp.sum(onehot, axis=1, keepdims=True)
        return c

    lax.fori_loop(0, n_tok // tt, tile_body, 0)

    counts = carry_ref[...]
    padded = jnp.floor((counts + (MOE_BM - 1)) * (1.0 / MOE_BM)) * MOE_BM
    er = lax.broadcasted_iota(jnp.int32, (ne, ne), 0)
    ec = lax.broadcasted_iota(jnp.int32, (ne, ne), 1)
    pad_start = jnp.dot((ec < er).astype(F32), padded, precision=HIGHEST, preferred_element_type=F32)
    pad_end = pad_start + padded

    def dest_body(j, c):
        t0 = pl.multiple_of(j * tt, tt)
        for k in range(TOP_K):
            idx = idx_ref[pl.ds(k, 1), pl.ds(t0, tt)]
            base = jnp.sum(jnp.where(eio == idx, pad_start[:, 0:1], 0.0), axis=0, keepdims=True)
            dest_ref[pl.ds(k, 1), pl.ds(t0, tt)] = base.astype(jnp.int32) + rank_ref[pl.ds(k, 1), pl.ds(t0, tt)]
        return c

    lax.fori_loop(0, n_tok // tt, dest_body, 0)

    width = meta_ref.shape[1]
    sub = lax.broadcasted_iota(jnp.int32, (ne, width), 0)
    lan = lax.broadcasted_iota(jnp.int32, (ne, width), 1)
    diag = sub == lan
    cnt_row = jnp.sum(jnp.where(diag, counts[:, 0:1], 0.0), axis=0, keepdims=True)
    start_row = jnp.sum(jnp.where(diag, pad_start[:, 0:1], 0.0), axis=0, keepdims=True)
    blk_start = (lan * MOE_BM).astype(F32)
    blk_exp = jnp.sum((pad_end[:, 0:1] <= blk_start).astype(F32), axis=0, keepdims=True)
    blk_exp = jnp.minimum(blk_exp, float(ne - 1))
    used = jnp.sum(padded[:, 0:1], axis=0, keepdims=True) * (1.0 / MOE_BM)
    meta_ref[0:1, :] = cnt_row.astype(jnp.int32)
    meta_ref[1:2, :] = start_row.astype(jnp.int32)
    meta_ref[2:3, :] = blk_exp.astype(jnp.int32)
    meta_ref[3:4, :] = jnp.broadcast_to(used, (1, width)).astype(jnp.int32)
    meta_ref[4:8, :] = jnp.zeros((4, width), jnp.int32)


def _route(lgt, n_blocks):
    ne, n_tok = lgt.shape
    width = -(-n_blocks // LANES) * LANES
    full = lambda *shape: pl.BlockSpec(shape, lambda: (0,) * len(shape))
    return pl.pallas_call(
        functools.partial(_route_kernel, n_tok=n_tok, n_blocks=n_blocks),
        out_shape=(jax.ShapeDtypeStruct((TOP_K, n_tok), jnp.int32),
                   jax.ShapeDtypeStruct((TOP_K, n_tok), F32),
                   jax.ShapeDtypeStruct((TOP_K, n_tok), jnp.int32),
                   jax.ShapeDtypeStruct((8, width), jnp.int32)),
        in_specs=[full(ne, n_tok)],
        out_specs=(full(TOP_K, n_tok), full(TOP_K, n_tok), full(TOP_K, n_tok), full(8, width)),
        scratch_shapes=[pltpu.VMEM((TOP_K, n_tok), jnp.int32), pltpu.VMEM((ne, LANES), F32)],
        compiler_params=pltpu.CompilerParams(vmem_limit_bytes=VMEM_LIMIT),
        name="route",
    )(lgt)


def _dispatch_kernel(dest_ref, cnt_ref, start_ref, nu_ref, x1_ref, meta_ref, mod_ref, g2_ref, zsrc_ref, xs_ref,
                     hbuf, sem, zsem):
    i = pl.program_id(0)
    n = pl.num_programs(0)
    tl = DISP_TILE
    slot = i % 2
    nb = xs_ref.shape[0] // MOE_BM

    def zero_block(b):
        return pltpu.make_async_copy(zsrc_ref, xs_ref.at[pl.ds(b * MOE_BM, MOE_BM)], zsem)

    @pl.when(i == 0)
    def _():
        def start_e(e, c):
            @pl.when(cnt_ref[e] > 0)
            def _():
                zero_block((start_ref[e] + cnt_ref[e] - 1) // MOE_BM).start()
            return c

        def wait_e(e, c):
            @pl.when(cnt_ref[e] > 0)
            def _():
                zero_block(0).wait()
            return c

        def start_t(b, c):
            zero_block(b).start()
            return c

        def wait_t(b, c):
            zero_block(0).wait()
            return c

        lax.fori_loop(0, N_EXPERTS, start_e, 0)
        lax.fori_loop(nu_ref[0], nb, start_t, 0)
        lax.fori_loop(0, N_EXPERTS, wait_e, 0)
        lax.fori_loop(nu_ref[0], nb, wait_t, 0)

    m = mod_ref[0]
    h2 = _rms(x1_ref[...]) * g2_ref[...] * (1.0 + m[4:5]) + m[3:4]
    lo = pltpu.bitcast(h2[:, :PACK_W].astype(BF16).astype(F32), jnp.uint32) >> 16
    hi = pltpu.bitcast(h2[:, PACK_W:].astype(BF16).astype(F32), jnp.uint32) & jnp.uint32(0xFFFF0000)
    row = jnp.concatenate([lo | hi, meta_ref[...], jnp.zeros((tl, D_MODEL - PACK_W - LANES), jnp.uint32)], axis=1)
    hbuf[slot] = row.reshape(tl, SUBLANES, LANES)

    def per_tok(t, c):
        tok = i * tl + t
        for k in range(TOP_K):
            d = dest_ref[k * (dest_ref.shape[0] // TOP_K) + tok]
            pltpu.make_async_copy(hbuf.at[slot, t], xs_ref.at[d], sem.at[slot]).start(priority=k % 2)
        return c

    lax.fori_loop(0, tl, per_tok, 0, unroll=8)

    def wait_slot(sl):
        for _ in range(TOP_K):
            pltpu.make_async_copy(hbuf.at[sl], xs_ref.at[pl.ds(0, tl)], sem.at[sl]).wait()

    @pl.when(i > 0)
    def _():
        wait_slot(1 - slot)

    @pl.when(i == n - 1)
    def _():
        wait_slot(slot)


def _dispatch(dest_flat, cnt, start, n_used, x1, meta_rows, mod3, g2, zsrc, n_rows):
    n_tok = x1.shape[0]
    tl = DISP_TILE
    per_b = SEQ // tl
    return pl.pallas_call(
        _dispatch_kernel,
        out_shape=jax.ShapeDtypeStruct((n_rows, SUBLANES, LANES), jnp.uint32),
        grid_spec=pltpu.PrefetchScalarGridSpec(
            num_scalar_prefetch=4,
            grid=(n_tok // tl,),
            in_specs=[pl.BlockSpec((tl, D_MODEL), lambda i, *_: (i, 0)),
                      pl.BlockSpec((tl, LANES), lambda i, *_: (i, 0)),
                      pl.BlockSpec((1, N_MOD, D_MODEL), lambda i, *_: (i // per_b, 0, 0)),
                      pl.BlockSpec((1, D_MODEL), lambda i, *_: (0, 0)),
                      pl.BlockSpec((MOE_BM, SUBLANES, LANES), lambda i, *_: (0, 0, 0))],
            out_specs=pl.BlockSpec(memory_space=pl.ANY),
            scratch_shapes=[pltpu.VMEM((2, tl, SUBLANES, LANES), jnp.uint32),
                            pltpu.SemaphoreType.DMA((2,)), pltpu.SemaphoreType.DMA]),
        compiler_params=pltpu.CompilerParams(dimension_semantics=("arbitrary",),
                                             vmem_limit_bytes=VMEM_LIMIT),
        name="dispatch",
    )(dest_flat, cnt, start, n_used, x1, meta_rows, mod3, g2, zsrc)


def _expert_rows_kernel(be_ref, nu_ref, xs_ref, wgu_hbm, bgu_ref, wd_hbm, bd_ref, ys_ref,
                        wgu_raw, wd_raw, wgu_bf, wd_bf, w_sem):
    j = pl.program_id(0)
    nu = nu_ref[0]
    n_lt = D_MODEL // LANES

    def weight_copies(e):
        return (pltpu.make_async_copy(wgu_hbm.at[e], wgu_raw, w_sem.at[0]),
                pltpu.make_async_copy(wd_hbm.at[e], wd_raw, w_sem.at[1]))

    @pl.when(j == 0)
    def _():
        for w in weight_copies(be_ref[0]):
            w.start()

    e_now = be_ref[j]
    new_expert = jnp.logical_or(j == 0, be_ref[jnp.maximum(j - 1, 0)] != e_now)

    @pl.when(jnp.logical_and(j < nu, new_expert))
    def _():
        for w in weight_copies(e_now):
            w.wait()
        n_cc = 8
        for c in range(n_cc):
            cc = slice(c * (2 * D_FF // n_cc), (c + 1) * (2 * D_FF // n_cc))
            wgu_bf[:, cc] = wgu_raw[:, cc].astype(BF16)
        for c in range(n_cc // 2):
            cc = slice(c * (2 * D_MODEL // n_cc), (c + 1) * (2 * D_MODEL // n_cc))
            wd_bf[:, cc] = wd_raw[:, cc].astype(BF16)
        j_next = lax.while_loop(lambda t: jnp.logical_and(t < nu, be_ref[jnp.minimum(t, pl.num_programs(0) - 1)] == e_now),
                                lambda t: t + 1, j + 1)

        @pl.when(j_next < nu)
        def _():
            for w in weight_copies(be_ref[jnp.minimum(j_next, pl.num_programs(0) - 1)]):
                w.start()

    @pl.when(j < nu)
    def _():
        e_f = e_now.astype(F32)
        pr = MOE_BM // MOE_PARTS
        for p in range(MOE_PARTS):
            words = xs_ref[pl.ds(p * pr, pr)].reshape(pr, D_MODEL)
            packed = words[:, 0:PACK_W]
            meta = pltpu.bitcast(words[:, PACK_W:PACK_W + LANES], F32)
            xb = jnp.concatenate(
                [pltpu.bitcast(packed << 16, F32).astype(BF16),
                 pltpu.bitcast(packed & jnp.uint32(0xFFFF0000), F32).astype(BF16)], axis=1)
            gate = jnp.zeros((pr, 1), F32)
            for k in range(TOP_K):
                mk = meta[:, META_IDX + k:META_IDX + k + 1] == e_f
                gate = gate + jnp.where(mk, meta[:, META_GATE + k:META_GATE + k + 1], 0.0)
            gu = _dot(xb, wgu_bf[...]) + bgu_ref[0]
            glu = jnp.minimum(gu[:, :D_FF], SWIGLU_LIMIT)
            lin = jnp.clip(gu[:, D_FF:], -SWIGLU_LIMIT, SWIGLU_LIMIT)
            act = glu * jax.nn.sigmoid(SWIGLU_ALPHA * glu) * (lin + 1.0)
            y = (_dot(act.astype(BF16), wd_bf[...]) + bd_ref[0]) * gate
            ys_ref[pl.ds(p * pr, pr)] = y.reshape(pr, n_lt, LANES)

    @pl.when(j >= nu)
    def _():
        ys_ref[...] = jnp.zeros_like(ys_ref)


def _expert_rows(blk_exp, n_used, xs, wgu, bgu, wd, bd):
    n_rows = xs.shape[0]
    nb = n_rows // MOE_BM
    n_lt = D_MODEL // LANES
    row_blk = lambda j, be, nu: (jnp.minimum(j, nu[0] - 1), 0, 0)
    per_e = lambda j, be, nu: (be[j], 0, 0)
    return pl.pallas_call(
        _expert_rows_kernel,
        out_shape=jax.ShapeDtypeStruct((n_rows, n_lt, LANES), F32),
        grid_spec=pltpu.PrefetchScalarGridSpec(
            num_scalar_prefetch=2,
            grid=(nb,),
            in_specs=[pl.BlockSpec((MOE_BM, SUBLANES, LANES), row_blk),
                      pl.BlockSpec(memory_space=pl.ANY),
                      pl.BlockSpec((1, 1, 2 * D_FF), per_e),
                      pl.BlockSpec(memory_space=pl.ANY),
                      pl.BlockSpec((1, 1, D_MODEL), per_e)],
            out_specs=pl.BlockSpec((MOE_BM, n_lt, LANES), lambda j, be, nu: (j, 0, 0)),
            scratch_shapes=[pltpu.VMEM((D_MODEL, 2 * D_FF), F32), pltpu.VMEM((D_FF, D_MODEL), F32),
                            pltpu.VMEM((D_MODEL, 2 * D_FF), BF16), pltpu.VMEM((D_FF, D_MODEL), BF16),
                            pltpu.SemaphoreType.DMA((2,))]),
        compiler_params=pltpu.CompilerParams(dimension_semantics=("arbitrary",),
                                             vmem_limit_bytes=VMEM_LIMIT),
        name="experts",
    )(blk_exp, n_used, xs, wgu, bgu, wd, bd)


def _gather_combine_kernel(dest_ref, ys_ref, x1_ref, mod_ref, fg_ref, o_ref, buf, sem):
    i = pl.program_id(0)
    n = pl.num_programs(0)
    tc = GATHER_TILE
    slot = i % 2

    def issue(tile, sl):
        def per_tok(t, c):
            tok = tile * tc + t
            for k in range(TOP_K):
                d = dest_ref[k * (dest_ref.shape[0] // TOP_K) + tok]
                pltpu.make_async_copy(ys_ref.at[d], buf.at[sl, k * tc + t], sem.at[sl]).start(priority=k % 2)
            return c

        lax.fori_loop(0, tc, per_tok, 0, unroll=8)

    @pl.when(i == 0)
    def _():
        issue(0, 0)

    @pl.when(i + 1 < n)
    def _():
        issue(i + 1, 1 - slot)

    for _ in range(TOP_K):
        pltpu.make_async_copy(ys_ref.at[pl.ds(0, tc)], buf.at[slot, pl.ds(0, tc)], sem.at[slot]).wait()

    moe = ((buf[slot, pl.ds(0, tc)] + buf[slot, pl.ds(tc, tc)])
           + (buf[slot, pl.ds(2 * tc, tc)] + buf[slot, pl.ds(3 * tc, tc)])).reshape(tc, D_MODEL)
    m = mod_ref[0]
    x2 = x1_ref[...] + m[5:6] * moe
    o_ref[...] = _rms(x2) * fg_ref[...]


def _gather_combine(dest_flat, ys, x1, mod3, fg):
    n_tok = x1.shape[0]
    tc = GATHER_TILE
    per_b = SEQ // tc
    n_lt = D_MODEL // LANES
    return pl.pallas_call(
        _gather_combine_kernel,
        out_shape=jax.ShapeDtypeStruct((n_tok, D_MODEL), F32),
        grid_spec=pltpu.PrefetchScalarGridSpec(
            num_scalar_prefetch=1,
            grid=(n_tok // tc,),
            in_specs=[pl.BlockSpec(memory_space=pl.ANY),
                      pl.BlockSpec((tc, D_MODEL), lambda i, d: (i, 0)),
                      pl.BlockSpec((1, N_MOD, D_MODEL), lambda i, d: (i // per_b, 0, 0)),
                      pl.BlockSpec((1, D_MODEL), lambda i, d: (0, 0))],
            out_specs=pl.BlockSpec((tc, D_MODEL), lambda i, d: (i, 0)),
            scratch_shapes=[pltpu.VMEM((2, TOP_K * tc, n_lt, LANES), F32), pltpu.SemaphoreType.DMA((2,))]),
        compiler_params=pltpu.CompilerParams(dimension_semantics=("arbitrary",),
                                             vmem_limit_bytes=VMEM_LIMIT),
        name="combine",
    )(dest_flat, ys, x1, mod3, fg)


def _expansion_matrices(src0):
    r = (jnp.arange(2 * LANES) % LANES)[:, None]
    out64, out128 = [], []
    for d in range(2):
        l64 = jnp.arange(D_SSD)[None, :]
        l128 = jnp.arange(HEADS * LANES)[None, :]
        out64.append((l64 // HEAD_DIM == r - src0 - HEADS * d).astype(BF16))
        out128.append((l128 // LANES == r - src0 - HEADS * d).astype(BF16))
    return jnp.stack(out64), jnp.stack(out128)


def _pad_lanes(v):
    return jnp.pad(v, [(0, 0)] * (v.ndim - 1) + [(0, LANES - v.shape[-1])])


def kernel(x, c, ctx, c_ctx, w_mod, b_mod, norm1_g, w_in, ssd_conv_w, ssd_conv_b, ssd_dt_bias, ssd_a_log,
           ssd_d, ssd_norm_g, sc_conv_w, w_out, norm2_g, w_router, b_router, w_gate_up, b_gate_up, w_down,
           b_down, final_g):
    bsz = x.shape[0]
    n_tok = bsz * SEQ
    n_assign = n_tok * TOP_K
    n_blocks = n_assign // MOE_BM + N_EXPERTS
    n_rows = n_blocks * MOE_BM
    li = 0

    cvec = jnp.concatenate([c, c_ctx[None, :], jnp.zeros((7, D_MODEL), F32)], axis=0)
    mod3 = _mod(cvec, w_mod, b_mod[li][None, :], li).reshape(bsz + 8, N_MOD, D_MODEL)

    w = w_in[li]
    wz = w[:, Z0:X0].astype(BF16)
    wxbc = w[:, X0:DT0].astype(BF16)
    wdt = _pad_lanes(w[:, DT0:SC0]).astype(BF16)
    wb = w[:, SC0:SC0 + D_SC].astype(BF16)
    wc = w[:, SC0 + D_SC:SC0 + 2 * D_SC].astype(BF16)
    wu = w[:, SC0 + 2 * D_SC:].astype(BF16)
    g1 = norm1_g[li][None, :]
    cw = ssd_conv_w[li]
    cb = ssd_conv_b[li][None, :]
    dtb = _pad_lanes(ssd_dt_bias[li].reshape(1, 2 * HEADS))
    alog = _pad_lanes(ssd_a_log[li].reshape(1, 2 * HEADS))
    e64_ctx, _ = _expansion_matrices(0)

    h0 = _ctx_states(ctx, mod3, g1, wxbc[:, :XB_W], wdt, cw[:, :XB_W], cb[:, :XB_W], dtb, alog, e64_ctx)

    rep = lambda a: _pad_lanes(jnp.tile(a[..., :2 * HEADS], (1, DT_COPIES)))
    e64, e128 = _expansion_matrices(DT_DA0)
    x2 = x.reshape(n_tok, D_MODEL)
    z, xbc, dtp, scb, v = _inproj(x2, mod3, g1, wz, wxbc, rep(wdt), rep(dtb), rep(alog), wb, wc, wu)

    dsk = jnp.repeat(ssd_d[li], HEAD_DIM)[None, :]
    yssd = _ssd(xbc.reshape(bsz, SEQ, XBC_W), z.reshape(bsz, SEQ, D_SSD), dtp.reshape(bsz, SEQ, LANES), h0,
                cw, cb, dsk, ssd_norm_g[li][None, :], e64, e128)

    wo = w_out[li].astype(BF16)
    g2 = norm2_g[li][None, :]
    wr = w_router[li].T
    wr_hi = wr.astype(BF16)
    wr_lo = (wr - wr_hi.astype(F32)).astype(BF16)
    x1, lgt = _outproj(x2, yssd.reshape(n_tok, D_SSD), scb, v, mod3, sc_conv_w[li], wo[:D_SSD], wo[D_SSD:],
                       g2, jnp.concatenate([wr_hi, wr_hi, wr_lo], axis=1), b_router[li][:, None])

    dest_t, gate_t, idx_t, meta = _route(lgt, n_blocks)
    dest_flat = dest_t.reshape(n_assign)
    cnt = meta[0, :N_EXPERTS]
    start = meta[1, :N_EXPERTS]
    blk_exp = meta[2, :n_blocks]
    n_used = meta[3, :1]

    meta_rows = lax.bitcast_convert_type(_pad_lanes(jnp.concatenate(
        [idx_t.T.astype(F32), gate_t.T], axis=1)), jnp.uint32)
    pad_meta = lax.bitcast_convert_type(_pad_lanes(jnp.concatenate(
        [jnp.full((MOE_BM, TOP_K), -1.0, F32), jnp.zeros((MOE_BM, TOP_K), F32)], axis=1)), jnp.uint32)
    zsrc = jnp.concatenate([jnp.zeros((MOE_BM, PACK_W), jnp.uint32), pad_meta,
                            jnp.zeros((MOE_BM, D_MODEL - PACK_W - LANES), jnp.uint32)],
                           axis=1).reshape(MOE_BM, SUBLANES, LANES)

    xs = _dispatch(dest_flat, cnt, start, n_used, x1, meta_rows, mod3, g2, zsrc, n_rows)
    ys = _expert_rows(blk_exp, n_used, xs, w_gate_up[li], b_gate_up[li][:, None, :],
                      w_down[li], b_down[li][:, None, :])
    out = _gather_combine(dest_flat, ys, x1, mod3, final_g[None, :])
    return out.reshape(bsz, SEQ, D_MODEL)
```

```python
import functools

import jax
import jax.numpy as jnp
from jax import lax
from jax.experimental import pallas as pl
from jax.experimental.pallas import tpu as pltpu

F32 = jnp.float32
BF16 = jnp.bfloat16
HIGHEST = lax.Precision.HIGHEST

D_MODEL = 1024
SEQ = 2048
CTX_LEN = 256
GRID_W = 64
D_SSD = 1024
D_SC = 1024
HEAD_DIM = 64
HEADS = 16
GROUPS = 2
STATE = 128
CHUNK = 128
N_EXPERTS = 32
TOP_K = 4
D_FF = 1024
SWIGLU_LIMIT = 7.0
SWIGLU_ALPHA = 1.702
NORM_EPS = 1e-6
N_MOD = 6
XBC_W = D_SSD + 2 * GROUPS * STATE
XB_W = D_SSD + GROUPS * STATE
LANES = 128

Z0 = 0
X0 = Z0 + D_SSD
B0 = X0 + D_SSD
C0 = B0 + GROUPS * STATE
DT0 = C0 + GROUPS * STATE
SC0 = DT0 + 2 * HEADS

TOK_TILE = 1024
OUT_TILE = 1024
MOE_BM = 512
MOE_PARTS = 2
RT_TILE = 512
CTX_BATCH = 4
DISP_TILE = 512
GATHER_TILE = 256
DT_COPIES = 3
DT_LOG0 = 2 * HEADS
DT_DA0 = 4 * HEADS
SUBLANES = 8
PACK_W = D_MODEL // 2
META_IDX = 0
META_GATE = TOP_K
VMEM_LIMIT = 56 * 1024 * 1024


def _silu(v):
    return v * jax.nn.sigmoid(v)


def _softplus(v):
    return jnp.maximum(v, 0.0) + jnp.log1p(jnp.exp(-jnp.abs(v)))


def _rms(v):
    return v * lax.rsqrt(jnp.mean(v * v, axis=-1, keepdims=True) + NORM_EPS)


def _dot(a, b):
    return jnp.dot(a, b, preferred_element_type=F32)


def _expand2(v, e2):
    hi = v.astype(BF16)
    lo = (v - hi.astype(F32)).astype(BF16)
    return _dot(jnp.concatenate([hi, lo], axis=1), e2)


def _mod_kernel(c_ref, w_ref, b_ref, o_ref):
    o_ref[...] = jnp.dot(_silu(c_ref[...]), w_ref[0], precision=HIGHEST,
                         preferred_element_type=F32) + b_ref[...]


def _mod(cvec, w_mod, b_mod, layer):
    rows = cvec.shape[0]
    n = w_mod.shape[2]
    tn = 1536
    return pl.pallas_call(
        _mod_kernel,
        out_shape=jax.ShapeDtypeStruct((rows, n), F32),
        grid=(n // tn,),
        in_specs=[pl.BlockSpec((rows, D_MODEL), lambda j: (0, 0)),
                  pl.BlockSpec((1, D_MODEL, tn), lambda j: (layer, 0, j)),
                  pl.BlockSpec((1, tn), lambda j: (0, j))],
        out_specs=pl.BlockSpec((rows, tn), lambda j: (0, j)),
        compiler_params=pltpu.CompilerParams(dimension_semantics=("arbitrary",),
                                             vmem_limit_bytes=VMEM_LIMIT),
        name="mod",
    )(cvec, w_mod, b_mod)


def _ctx_kernel(ctx_ref, mod_ref, g1_ref, wxb_ref, wdt_ref, cw_ref, cb_ref, dtb_ref, alog_ref, e64_ref,
                h0_ref):
    L = CTX_LEN
    nb = ctx_ref.shape[0]
    m = mod_ref[0]
    hc = _rms(ctx_ref[...].reshape(nb * L, D_MODEL)) * g1_ref[...] * (1.0 + m[1:2]) + m[0:1]
    hb = hc.astype(BF16)
    pxb = _dot(hb, wxb_ref[...])
    dtr = _dot(hb, wdt_ref[...])
    rowl = lax.broadcasted_iota(jnp.int32, (nb * L, XB_W), 0) & (L - 1)
    dn = jnp.where(rowl == 0, 0.0, pltpu.roll(pxb, 1, 0))
    up = jnp.where(rowl == L - 1, 0.0, pltpu.roll(pxb, nb * L - 1, 0))
    cw = cw_ref[...]
    xb = _silu(cw[0:1] * dn + cw[1:2] * pxb + cw[2:3] * up + cb_ref[...])
    dt_all = _softplus(dtr + dtb_ref[...])
    da_all = dt_all * (-jnp.exp(alog_ref[...]))
    ri = lax.broadcasted_iota(jnp.int32, (L, L), 0)
    ci = lax.broadcasted_iota(jnp.int32, (L, L), 1)
    for bi in range(nb):
        rs = slice(bi * L, (bi + 1) * L)
        xs = xb[rs, :D_SSD]
        bm = xb[rs, D_SSD:].astype(BF16)
        dt = dt_all[rs]
        da = da_all[rs]
        for d in range(2):
            tri = (ci <= ri) if d == 0 else (ci >= ri)
            cum = jnp.dot(tri.astype(F32), da, precision=HIGHEST, preferred_element_type=F32)
            last = cum[L - 1:L] if d == 0 else cum[0:1]
            w_e = _expand2(jnp.exp(last - cum) * dt, e64_ref[d])
            xw = (xs * w_e).astype(BF16)
            for g in range(GROUPS):
                gw = D_SSD // GROUPS
                st = lax.dot_general(bm[:, g * STATE:(g + 1) * STATE], xw[:, g * gw:(g + 1) * gw],
                                     (((0,), (0,)), ((), ())), preferred_element_type=F32)
                h0_ref[bi, d, :, g * gw:(g + 1) * gw] = st


def _ctx_states(ctx, mod3, g1, wxb, wdt, cw, cb, dtb, alog, e64):
    bsz = ctx.shape[0]
    mod_row = bsz
    nb = CTX_BATCH if bsz % CTX_BATCH == 0 else 1
    const = lambda *shape: pl.BlockSpec(shape, lambda b: (0,) * len(shape))
    return pl.pallas_call(
        _ctx_kernel,
        out_shape=jax.ShapeDtypeStruct((bsz, 2, STATE, D_SSD), F32),
        grid=(bsz // nb,),
        in_specs=[pl.BlockSpec((nb, CTX_LEN, D_MODEL), lambda b: (b, 0, 0)),
                  pl.BlockSpec((1, N_MOD, D_MODEL), lambda b: (mod_row, 0, 0)),
                  const(1, D_MODEL), const(D_MODEL, XB_W), const(D_MODEL, LANES),
                  const(3, XB_W), const(1, XB_W), const(1, LANES), const(1, LANES),
                  const(2, 2 * LANES, D_SSD)],
        out_specs=pl.BlockSpec((nb, 2, STATE, D_SSD), lambda b: (b, 0, 0, 0)),
        compiler_params=pltpu.CompilerParams(dimension_semantics=("arbitrary",),
                                             vmem_limit_bytes=VMEM_LIMIT),
        name="ctx_states",
    )(ctx, mod3, g1, wxb, wdt, cw, cb, dtb, alog, e64)


def _inproj_kernel(x_ref, mod_ref, g1_ref, wz_ref, wxbc_ref, wdt_ref, dtb_ref, alog_ref, wb_ref, wc_ref, wu_ref,
                   z_ref, xbc_ref, dt_ref, scb_ref, v_ref):
    m = mod_ref[0]
    hx = _rms(x_ref[...]) * g1_ref[...] * (1.0 + m[1:2]) + m[0:1]
    hb = hx.astype(BF16)
    z_ref[...] = _dot(hb, wz_ref[...]).astype(BF16)
    xbc_ref[...] = _dot(hb, wxbc_ref[...]).astype(BF16)
    dt = _softplus(_dot(hb, wdt_ref[...]) + dtb_ref[...])
    lane = lax.broadcasted_iota(jnp.int32, dt.shape, 1)
    dt_ref[...] = jnp.where(lane < DT_LOG0, dt,
                            jnp.where(lane < DT_DA0, jnp.log(dt), dt * (-jnp.exp(alog_ref[...]))))
    scb_ref[...] = _dot(hb, wb_ref[...]).astype(BF16)
    v_ref[...] = (_dot(hb, wc_ref[...]) * _dot(hb, wu_ref[...])).astype(BF16)


def _inproj(x2, mod3, g1, wz, wxbc, wdt, dtb, alog, wb, wc, wu):
    t = x2.shape[0]
    tm = TOK_TILE
    per_b = SEQ // tm
    const = lambda *shape: pl.BlockSpec(shape, lambda i: (0,) * len(shape), pipeline_mode=pl.Buffered(1))
    tile = lambda w: pl.BlockSpec((tm, w), lambda i: (i, 0))
    return pl.pallas_call(
        _inproj_kernel,
        out_shape=(jax.ShapeDtypeStruct((t, D_SSD), BF16), jax.ShapeDtypeStruct((t, XBC_W), BF16),
                   jax.ShapeDtypeStruct((t, LANES), F32), jax.ShapeDtypeStruct((t, D_SC), BF16),
                   jax.ShapeDtypeStruct((t, D_SC), BF16)),
        grid=(t // tm,),
        in_specs=[tile(D_MODEL),
                  pl.BlockSpec((1, N_MOD, D_MODEL), lambda i: (i // per_b, 0, 0)),
                  const(1, D_MODEL), const(D_MODEL, D_SSD), const(D_MODEL, XBC_W), const(D_MODEL, LANES),
                  const(1, LANES), const(1, LANES),
                  const(D_MODEL, D_SC), const(D_MODEL, D_SC), const(D_MODEL, D_SC)],
        out_specs=(tile(D_SSD), tile(XBC_W), tile(LANES), tile(D_SC), tile(D_SC)),
        compiler_params=pltpu.CompilerParams(dimension_semantics=("arbitrary",),
                                             vmem_limit_bytes=VMEM_LIMIT),
        name="inproj",
    )(x2, mod3, g1, wz, wxbc, wdt, dtb, alog, wb, wc, wu)


def _ssd_kernel(xbc_ref, z_ref, dt_ref, h0_ref, cw_ref, cb_ref, dsk_ref, g_ref,
                e64_ref, e128_ref, o_ref, xc_ref, y_ref, s_ref):
    Q = CHUNK
    nck = SEQ // Q
    gw = D_SSD // GROUPS

    rowi = lax.broadcasted_iota(jnp.int32, (SUBLANES, XBC_W), 0)

    def conv_body(c, carry):
        r0 = pl.multiple_of(c * Q, Q)
        main = xbc_ref[0, pl.ds(r0, Q), :].astype(F32)
        pstart = pl.multiple_of(jnp.maximum(r0 - 16, 0), 16)
        nstart = pl.multiple_of(jnp.minimum(r0 + Q, SEQ - 16), 16)
        prev = xbc_ref[0, pl.ds(pstart, 16), :].astype(F32)[15:16]
        nxt = xbc_ref[0, pl.ds(nstart, 16), :].astype(F32)[0:1]
        prev = jnp.where(c > 0, prev, 0.0)
        nxt = jnp.where(c < nck - 1, nxt, 0.0)
        dn = pltpu.roll(main, 1, 0)
        up = pltpu.roll(main, Q - 1, 0)
        dn = jnp.concatenate([jnp.where(rowi == 0, prev, dn[0:SUBLANES]), dn[SUBLANES:]], axis=0)
        up = jnp.concatenate([up[:Q - SUBLANES], jnp.where(rowi == SUBLANES - 1, nxt, up[Q - SUBLANES:])], axis=0)
        cw = cw_ref[...]
        conv = cw[0:1] * dn + cw[1:2] * main + cw[2:3] * up + cb_ref[...]
        xc_ref[pl.ds(r0, Q), :] = _silu(conv).astype(BF16)
        return carry

    lax.fori_loop(0, nck, conv_body, 0)

    ri = lax.broadcasted_iota(jnp.int32, (Q, Q), 0)
    ci = lax.broadcasted_iota(jnp.int32, (Q, Q), 1)
    lane = lax.broadcasted_iota(jnp.int32, (Q, LANES), 1)
    da_lanes = jnp.logical_and(lane >= DT_DA0, lane < DT_DA0 + 2 * HEADS)

    def chunk(c, d, first):
        r0 = pl.multiple_of(c * Q, Q)
        rows = pl.ds(r0, Q)
        xs_b = xc_ref[rows, 0:D_SSD]
        xs = xs_b.astype(F32)
        bm = xc_ref[rows, D_SSD:D_SSD + GROUPS * STATE]
        cm = xc_ref[rows, D_SSD + GROUPS * STATE:XBC_W]
        dtp = dt_ref[0, rows, :]
        da = jnp.where(da_lanes, dtp, 0.0)
        dt = pltpu.roll(dtp, DT_DA0, 1)
        log_dt = pltpu.roll(dtp, DT_DA0 - DT_LOG0, 1)
        tri = (ci <= ri) if d == 0 else (ci >= ri)
        p0 = da.astype(BF16)
        r1 = da - p0.astype(F32)
        p1 = r1.astype(BF16)
        p2 = (r1 - p1.astype(F32)).astype(BF16)
        tri_b = jnp.where(tri, 1.0, 0.0).astype(BF16)
        cum = _dot(jnp.concatenate([tri_b, tri_b, tri_b], axis=1),
                   jnp.concatenate([p0, p1, p2], axis=0))
        sub_t = (cum - log_dt).T
        last = cum[Q - 1:Q] if d == 0 else cum[0:1]
        ecum_e = _expand2(jnp.exp(cum), e64_ref[d])
        w_e = _expand2(jnp.where(da_lanes, jnp.exp(last - cum) * dt, 0.0), e64_ref[d])
        colb = _expand2(cum, e128_ref[d])
        decay_e = ecum_e[Q - 1:Q] if d == 0 else ecum_e[0:1]

        gmat = [lax.dot_general(cm[:, g * STATE:(g + 1) * STATE], bm[:, g * STATE:(g + 1) * STATE],
                                (((1,), (1,)), ((), ())), preferred_element_type=F32)
                for g in range(GROUPS)]
        zero_b = jnp.zeros((Q, LANES), BF16)
        y_parts = []
        for p in range(HEADS // 2):
            g = (2 * p) // (HEADS // GROUPS)
            ms = []
            for hh in (2 * p, 2 * p + 1):
                src = DT_DA0 + HEADS * d + hh
                seg = colb[:, hh * LANES:(hh + 1) * LANES] - sub_t[src:src + 1, :]
                ms.append((jnp.where(tri, jnp.exp(seg), 0.0) * gmat[g]).astype(BF16))
            mcat = jnp.concatenate(ms, axis=1)
            xp = xs_b[:, p * LANES:(p + 1) * LANES]
            rhs = jnp.concatenate([jnp.where(lane < HEAD_DIM, xp, zero_b),
                                   jnp.where(lane >= HEAD_DIM, xp, zero_b)], axis=0)
            y_parts.append(_dot(mcat, rhs))
        y_diag = jnp.concatenate(y_parts, axis=1)

        s_old = s_ref[d]
        s_bf = s_old.astype(BF16)
        y_off = jnp.concatenate(
            [_dot(cm[:, g * STATE:(g + 1) * STATE], s_bf[:, g * gw:(g + 1) * gw]) for g in range(GROUPS)],
            axis=1)
        y = y_diag + y_off * ecum_e

        xw = (xs * w_e).astype(BF16)
        upd = jnp.concatenate(
            [lax.dot_general(bm[:, g * STATE:(g + 1) * STATE], xw[:, g * gw:(g + 1) * gw],
                             (((0,), (0,)), ((), ())), preferred_element_type=F32) for g in range(GROUPS)],
            axis=1)
        s_ref[d] = s_old * decay_e + upd

        if first:
            y_ref[rows, :] = y + dsk_ref[...] * xs
        else:
            tot = y_ref[rows, :] + y
            zz = z_ref[0, rows, :].astype(F32)
            gz = tot * _silu(zz)
            outs = []
            for g in range(GROUPS):
                gg = gz[:, g * gw:(g + 1) * gw]
                outs.append(gg * lax.rsqrt(jnp.mean(gg * gg, axis=-1, keepdims=True) + NORM_EPS))
            o_ref[0, rows, :] = (jnp.concatenate(outs, axis=1) * g_ref[...]).astype(BF16)

    s_ref[...] = h0_ref[0]

    def first_half(i, carry):
        chunk(i, 0, True)
        chunk(nck - 1 - i, 1, True)
        return carry

    def second_half(i, carry):
        chunk(i, 0, False)
        chunk(nck - 1 - i, 1, False)
        return carry

    lax.fori_loop(0, nck // 2, first_half, 0, unroll=2)
    lax.fori_loop(nck // 2, nck, second_half, 0, unroll=2)


def _ssd(xbc3, z3, dt3, h0, cw, cb, dsk, g, e64, e128):
    bsz = xbc3.shape[0]
    const = lambda *shape: pl.BlockSpec(shape, lambda b: (0,) * len(shape))
    seq = lambda w: pl.BlockSpec((1, SEQ, w), lambda b: (b, 0, 0))
    return pl.pallas_call(
        _ssd_kernel,
        out_shape=jax.ShapeDtypeStruct((bsz, SEQ, D_SSD), BF16),
        grid=(bsz,),
        in_specs=[seq(XBC_W), seq(D_SSD), seq(LANES),
                  pl.BlockSpec((1, 2, STATE, D_SSD), lambda b: (b, 0, 0, 0)),
                  const(3, XBC_W), const(1, XBC_W),
                  const(1, D_SSD), const(1, D_SSD), const(2, 2 * LANES, D_SSD),
                  const(2, 2 * LANES, HEADS * LANES)],
        out_specs=seq(D_SSD),
        scratch_shapes=[pltpu.VMEM((SEQ, XBC_W), BF16), pltpu.VMEM((SEQ, D_SSD), F32),
                        pltpu.VMEM((2, STATE, D_SSD), F32)],
        compiler_params=pltpu.CompilerParams(dimension_semantics=("arbitrary",),
                                             vmem_limit_bytes=VMEM_LIMIT),
        name="ssd",
    )(xbc3, z3, dt3, h0, cw, cb, dsk, g, e64, e128)


def _outproj_kernel(x_ref, yssd_ref, scb_ref, v_ref, vp_ref, vn_ref, mod_ref, scw_ref, wo1_ref, wo2_ref,
                    g2_ref, wrt_ref, br_ref, x1_ref, lg_ref):
    tm = OUT_TILE
    per_b = SEQ // tm
    i = pl.program_id(0)
    first = (i % per_b) == 0
    last = (i % per_b) == per_b - 1
    m = mod_ref[0]
    v = v_ref[...].astype(F32)
    vp = jnp.where(first, 0.0, vp_ref[...].astype(F32))
    vn = jnp.where(last, 0.0, vn_ref[...].astype(F32))
    dn = jnp.concatenate([vp, v[:tm - GRID_W]], axis=0)
    up = jnp.concatenate([v[GRID_W:], vn], axis=0)
    scw = scw_ref[...]
    ysc = scb_ref[...].astype(F32) * (scw[0:1] * dn + scw[1:2] * v + scw[2:3] * up)
    out = _dot(yssd_ref[...], wo1_ref[...]) + _dot(ysc.astype(BF16), wo2_ref[...])
    x1 = x_ref[...] + m[2:3] * out
    x1_ref[...] = x1
    h2 = _rms(x1) * g2_ref[...] * (1.0 + m[4:5]) + m[3:4]
    h_hi = h2.astype(BF16)
    h_lo = (h2 - h_hi.astype(F32)).astype(BF16)
    lg_ref[...] = lax.dot_general(wrt_ref[...], jnp.concatenate([h_hi, h_lo, h_hi], axis=1),
                                  (((1,), (1,)), ((), ())), preferred_element_type=F32) + br_ref[...]


def _outproj(x2, yssd, scb, v, mod3, scw, wo1, wo2, g2, wrt, br):
    t = x2.shape[0]
    tm = OUT_TILE
    per_b = SEQ // tm
    r = tm // GRID_W
    nrow = t // GRID_W
    const = lambda *shape: pl.BlockSpec(shape, lambda i: (0,) * len(shape))
    tile = lambda w: pl.BlockSpec((tm, w), lambda i: (i, 0))
    return pl.pallas_call(
        _outproj_kernel,
        out_shape=(jax.ShapeDtypeStruct((t, D_MODEL), F32), jax.ShapeDtypeStruct((N_EXPERTS, t), F32)),
        grid=(t // tm,),
        in_specs=[tile(D_MODEL), tile(D_SSD), tile(D_SC), tile(D_SC),
                  pl.BlockSpec((GRID_W, D_SC), lambda i: (jnp.maximum(i * r - 1, 0), 0)),
                  pl.BlockSpec((GRID_W, D_SC), lambda i: (jnp.minimum((i + 1) * r, nrow - 1), 0)),
                  pl.BlockSpec((1, N_MOD, D_MODEL), lambda i: (i // per_b, 0, 0)),
                  const(3, D_SC), const(D_SSD, D_MODEL), const(D_SC, D_MODEL), const(1, D_MODEL),
                  const(N_EXPERTS, 3 * D_MODEL), const(N_EXPERTS, 1)],
        out_specs=(tile(D_MODEL), pl.BlockSpec((N_EXPERTS, tm), lambda i: (0, i))),
        compiler_params=pltpu.CompilerParams(dimension_semantics=("arbitrary",),
                                             vmem_limit_bytes=VMEM_LIMIT),
        name="outproj",
    )(x2, yssd, scb, v, v, v, mod3, scw, wo1, wo2, g2, wrt, br)


def _route_kernel(lg_ref, dest_ref, gate_ref, idx_ref, meta_ref, rank_ref, carry_ref, *, n_tok, n_blocks):
    tt = RT_TILE
    ne = N_EXPERTS
    eio = lax.broadcasted_iota(jnp.int32, (ne, tt), 0)
    si = lax.broadcasted_iota(jnp.int32, (tt, tt), 0)
    ti = lax.broadcasted_iota(jnp.int32, (tt, tt), 1)
    before = (si < ti).astype(BF16)
    carry_ref[...] = jnp.zeros_like(carry_ref)

    def tile_body(j, c):
        t0 = pl.multiple_of(j * tt, tt)
        l = lg_ref[:, pl.ds(t0, tt)]
        onehot = jnp.zeros((ne, tt), F32)
        tops, sels = [], []
        for _ in range(TOP_K):
            mx = jnp.max(l, axis=0, keepdims=True)
            idx = jnp.min(jnp.where(l == mx, eio, ne), axis=0, keepdims=True)
            sel = eio == idx
            l = jnp.where(sel, -jnp.inf, l)
            onehot = onehot + sel.astype(F32)
            tops.append(mx)
            sels.append(sel)
            idx_ref[pl.ds(len(tops) - 1, 1), pl.ds(t0, tt)] = idx
        ex = [jnp.exp(tv - tops[0]) for tv in tops]
        den = ex[0] + ex[1] + ex[2] + ex[3]
        prefix = _dot(onehot.astype(BF16), before) + carry_ref[:, 0:1]
        for k in range(TOP_K):
            gate_ref[pl.ds(k, 1), pl.ds(t0, tt)] = ex[k] / den
            rk = jnp.sum(jnp.where(sels[k], prefix, 0.0), axis=0, keepdims=True)
            rank_ref[pl.ds(k, 1), pl.ds(t0, tt)] = rk.astype(jnp.int32)
        carry_ref[...] = carry_ref[...] + jnp.sum(onehot, axis=1, keepdims=True)
        return c

    lax.fori_loop(0, n_tok // tt, tile_body, 0)

    counts = carry_ref[...]
    padded = jnp.floor((counts + (MOE_BM - 1)) * (1.0 / MOE_BM)) * MOE_BM
    er = lax.broadcasted_iota(jnp.int32, (ne, ne), 0)
    ec = lax.broadcasted_iota(jnp.int32, (ne, ne), 1)
    pad_start = jnp.dot((ec < er).astype(F32), padded, precision=HIGHEST, preferred_element_type=F32)
    pad_end = pad_start + padded

    def dest_body(j, c):
        t0 = pl.multiple_of(j * tt, tt)
        for k in range(TOP_K):
            idx = idx_ref[pl.ds(k, 1), pl.ds(t0, tt)]
            base = jnp.sum(jnp.where(eio == idx, pad_start[:, 0:1], 0.0), axis=0, keepdims=True)
            dest_ref[pl.ds(k, 1), pl.ds(t0, tt)] = base.astype(jnp.int32) + rank_ref[pl.ds(k, 1), pl.ds(t0, tt)]
        return c

    lax.fori_loop(0, n_tok // tt, dest_body, 0)

    width = meta_ref.shape[1]
    sub = lax.broadcasted_iota(jnp.int32, (ne, width), 0)
    lan = lax.broadcasted_iota(jnp.int32, (ne, width), 1)
    diag = sub == lan
    cnt_row = jnp.sum(jnp.where(diag, counts[:, 0:1], 0.0), axis=0, keepdims=True)
    start_row = jnp.sum(jnp.where(diag, pad_start[:, 0:1], 0.0), axis=0, keepdims=True)
    blk_start = (lan * MOE_BM).astype(F32)
    blk_exp = jnp.sum((pad_end[:, 0:1] <= blk_start).astype(F32), axis=0, keepdims=True)
    blk_exp = jnp.minimum(blk_exp, float(ne - 1))
    used = jnp.sum(padded[:, 0:1], axis=0, keepdims=True) * (1.0 / MOE_BM)
    meta_ref[0:1, :] = cnt_row.astype(jnp.int32)
    meta_ref[1:2, :] = start_row.astype(jnp.int32)
    meta_ref[2:3, :] = blk_exp.astype(jnp.int32)
    meta_ref[3:4, :] = jnp.broadcast_to(used, (1, width)).astype(jnp.int32)
    meta_ref[4:8, :] = jnp.zeros((4, width), jnp.int32)


def _route(lgt, n_blocks):
    ne, n_tok = lgt.shape
    width = -(-n_blocks // LANES) * LANES
    full = lambda *shape: pl.BlockSpec(shape, lambda: (0,) * len(shape))
    return pl.pallas_call(
        functools.partial(_route_kernel, n_tok=n_tok, n_blocks=n_blocks),
        out_shape=(jax.ShapeDtypeStruct((TOP_K, n_tok), jnp.int32),
                   jax.ShapeDtypeStruct((TOP_K, n_tok), F32),
                   jax.ShapeDtypeStruct((TOP_K, n_tok), jnp.int32),
                   jax.ShapeDtypeStruct((8, width), jnp.int32)),
        in_specs=[full(ne, n_tok)],
        out_specs=(full(TOP_K, n_tok), full(TOP_K, n_tok), full(TOP_K, n_tok), full(8, width)),
        scratch_shapes=[pltpu.VMEM((TOP_K, n_tok), jnp.int32), pltpu.VMEM((ne, LANES), F32)],
        compiler_params=pltpu.CompilerParams(vmem_limit_bytes=VMEM_LIMIT),
        name="route",
    )(lgt)


def _dispatch_kernel(dest_ref, cnt_ref, start_ref, nu_ref, x1_ref, meta_ref, mod_ref, g2_ref, zsrc_ref, xs_ref,
                     hbuf, sem, zsem):
    i = pl.program_id(0)
    n = pl.num_programs(0)
    tl = DISP_TILE
    slot = i % 2
    nb = xs_ref.shape[0] // MOE_BM

    def zero_block(b):
        return pltpu.make_async_copy(zsrc_ref, xs_ref.at[pl.ds(b * MOE_BM, MOE_BM)], zsem)

    @pl.when(i == 0)
    def _():
        def start_e(e, c):
            @pl.when(cnt_ref[e] > 0)
            def _():
                zero_block((start_ref[e] + cnt_ref[e] - 1) // MOE_BM).start()
            return c

        def wait_e(e, c):
            @pl.when(cnt_ref[e] > 0)
            def _():
                zero_block(0).wait()
            return c

        def start_t(b, c):
            zero_block(b).start()
            return c

        def wait_t(b, c):
            zero_block(0).wait()
            return c

        lax.fori_loop(0, N_EXPERTS, start_e, 0)
        lax.fori_loop(nu_ref[0], nb, start_t, 0)
        lax.fori_loop(0, N_EXPERTS, wait_e, 0)
        lax.fori_loop(nu_ref[0], nb, wait_t, 0)

    m = mod_ref[0]
    h2 = _rms(x1_ref[...]) * g2_ref[...] * (1.0 + m[4:5]) + m[3:4]
    lo = pltpu.bitcast(h2[:, :PACK_W].astype(BF16).astype(F32), jnp.uint32) >> 16
    hi = pltpu.bitcast(h2[:, PACK_W:].astype(BF16).astype(F32), jnp.uint32) & jnp.uint32(0xFFFF0000)
    row = jnp.concatenate([lo | hi, meta_ref[...], jnp.zeros((tl, D_MODEL - PACK_W - LANES), jnp.uint32)], axis=1)
    hbuf[slot] = row.reshape(tl, SUBLANES, LANES)

    def per_tok(t, c):
        tok = i * tl + t
        for k in range(TOP_K):
            d = dest_ref[k * (dest_ref.shape[0] // TOP_K) + tok]
            pltpu.make_async_copy(hbuf.at[slot, t], xs_ref.at[d], sem.at[slot]).start(priority=k % 2)
        return c

    lax.fori_loop(0, tl, per_tok, 0, unroll=8)

    def wait_slot(sl):
        for _ in range(TOP_K):
            pltpu.make_async_copy(hbuf.at[sl], xs_ref.at[pl.ds(0, tl)], sem.at[sl]).wait()

    @pl.when(i > 0)
    def _():
        wait_slot(1 - slot)

    @pl.when(i == n - 1)
    def _():
        wait_slot(slot)


def _dispatch(dest_flat, cnt, start, n_used, x1, meta_rows, mod3, g2, zsrc, n_rows):
    n_tok = x1.shape[0]
    tl = DISP_TILE
    per_b = SEQ // tl
    return pl.pallas_call(
        _dispatch_kernel,
        out_shape=jax.ShapeDtypeStruct((n_rows, SUBLANES, LANES), jnp.uint32),
        grid_spec=pltpu.PrefetchScalarGridSpec(
            num_scalar_prefetch=4,
            grid=(n_tok // tl,),
            in_specs=[pl.BlockSpec((tl, D_MODEL), lambda i, *_: (i, 0)),
                      pl.BlockSpec((tl, LANES), lambda i, *_: (i, 0)),
                      pl.BlockSpec((1, N_MOD, D_MODEL), lambda i, *_: (i // per_b, 0, 0)),
                      pl.BlockSpec((1, D_MODEL), lambda i, *_: (0, 0)),
                      pl.BlockSpec((MOE_BM, SUBLANES, LANES), lambda i, *_: (0, 0, 0))],
            out_specs=pl.BlockSpec(memory_space=pl.ANY),
            scratch_shapes=[pltpu.VMEM((2, tl, SUBLANES, LANES), jnp.uint32),
                            pltpu.SemaphoreType.DMA((2,)), pltpu.SemaphoreType.DMA]),
        compiler_params=pltpu.CompilerParams(dimension_semantics=("arbitrary",),
                                             vmem_limit_bytes=VMEM_LIMIT),
        name="dispatch",
    )(dest_flat, cnt, start, n_used, x1, meta_rows, mod3, g2, zsrc)


def _expert_rows_kernel(be_ref, nu_ref, xs_ref, wgu_hbm, bgu_ref, wd_hbm, bd_ref, ys_ref,
                        wgu_raw, wd_raw, wgu_bf, wd_bf, w_sem):
    j = pl.program_id(0)
    nu = nu_ref[0]
    n_lt = D_MODEL // LANES

    def weight_copies(e):
        return (pltpu.make_async_copy(wgu_hbm.at[e], wgu_raw, w_sem.at[0]),
                pltpu.make_async_copy(wd_hbm.at[e], wd_raw, w_sem.at[1]))

    @pl.when(j == 0)
    def _():
        for w in weight_copies(be_ref[0]):
            w.start()

    e_now = be_ref[j]
    new_expert = jnp.logical_or(j == 0, be_ref[jnp.maximum(j - 1, 0)] != e_now)

    @pl.when(jnp.logical_and(j < nu, new_expert))
    def _():
        for w in weight_copies(e_now):
            w.wait()
        n_cc = 8
        for c in range(n_cc):
            cc = slice(c * (2 * D_FF // n_cc), (c + 1) * (2 * D_FF // n_cc))
            wgu_bf[:, cc] = wgu_raw[:, cc].astype(BF16)
        for c in range(n_cc // 2):
            cc = slice(c * (2 * D_MODEL // n_cc), (c + 1) * (2 * D_MODEL // n_cc))
            wd_bf[:, cc] = wd_raw[:, cc].astype(BF16)
        j_next = lax.while_loop(lambda t: jnp.logical_and(t < nu, be_ref[jnp.minimum(t, pl.num_programs(0) - 1)] == e_now),
                                lambda t: t + 1, j + 1)

        @pl.when(j_next < nu)
        def _():
            for w in weight_copies(be_ref[jnp.minimum(j_next, pl.num_programs(0) - 1)]):
                w.start()

    @pl.when(j < nu)
    def _():
        e_f = e_now.astype(F32)
        pr = MOE_BM // MOE_PARTS
        for p in range(MOE_PARTS):
            words = xs_ref[pl.ds(p * pr, pr)].reshape(pr, D_MODEL)
            packed = words[:, 0:PACK_W]
            meta = pltpu.bitcast(words[:, PACK_W:PACK_W + LANES], F32)
            xb = jnp.concatenate(
                [pltpu.bitcast(packed << 16, F32).astype(BF16),
                 pltpu.bitcast(packed & jnp.uint32(0xFFFF0000), F32).astype(BF16)], axis=1)
            gate = jnp.zeros((pr, 1), F32)
            for k in range(TOP_K):
                mk = meta[:, META_IDX + k:META_IDX + k + 1] == e_f
                gate = gate + jnp.where(mk, meta[:, META_GATE + k:META_GATE + k + 1], 0.0)
            gu = _dot(xb, wgu_bf[...]) + bgu_ref[0]
            glu = jnp.minimum(gu[:, :D_FF], SWIGLU_LIMIT)
            lin = jnp.clip(gu[:, D_FF:], -SWIGLU_LIMIT, SWIGLU_LIMIT)
            act = glu * jax.nn.sigmoid(SWIGLU_ALPHA * glu) * (lin + 1.0)
            y = (_dot(act.astype(BF16), wd_bf[...]) + bd_ref[0]) * gate
            ys_ref[pl.ds(p * pr, pr)] = y.reshape(pr, n_lt, LANES)

    @pl.when(j >= nu)
    def _():
        ys_ref[...] = jnp.zeros_like(ys_ref)


def _expert_rows(blk_exp, n_used, xs, wgu, bgu, wd, bd):
    n_rows = xs.shape[0]
    nb = n_rows // MOE_BM
    n_lt = D_MODEL // LANES
    row_blk = lambda j, be, nu: (jnp.minimum(j, nu[0] - 1), 0, 0)
    per_e = lambda j, be, nu: (be[j], 0, 0)
    return pl.pallas_call(
        _expert_rows_kernel,
        out_shape=jax.ShapeDtypeStruct((n_rows, n_lt, LANES), F32),
        grid_spec=pltpu.PrefetchScalarGridSpec(
            num_scalar_prefetch=2,
            grid=(nb,),
            in_specs=[pl.BlockSpec((MOE_BM, SUBLANES, LANES), row_blk),
                      pl.BlockSpec(memory_space=pl.ANY),
                      pl.BlockSpec((1, 1, 2 * D_FF), per_e),
                      pl.BlockSpec(memory_space=pl.ANY),
                      pl.BlockSpec((1, 1, D_MODEL), per_e)],
            out_specs=pl.BlockSpec((MOE_BM, n_lt, LANES), lambda j, be, nu: (j, 0, 0)),
            scratch_shapes=[pltpu.VMEM((D_MODEL, 2 * D_FF), F32), pltpu.VMEM((D_FF, D_MODEL), F32),
                            pltpu.VMEM((D_MODEL, 2 * D_FF), BF16), pltpu.VMEM((D_FF, D_MODEL), BF16),
                            pltpu.SemaphoreType.DMA((2,))]),
        compiler_params=pltpu.CompilerParams(dimension_semantics=("arbitrary",),
                                             vmem_limit_bytes=VMEM_LIMIT),
        name="experts",
    )(blk_exp, n_used, xs, wgu, bgu, wd, bd)


def _gather_combine_kernel(dest_ref, ys_ref, x1_ref, mod_ref, fg_ref, o_ref, buf, sem):
    i = pl.program_id(0)
    n = pl.num_programs(0)
    tc = GATHER_TILE
    slot = i % 2

    def issue(tile, sl):
        def per_tok(t, c):
            tok = tile * tc + t
            for k in range(TOP_K):
                d = dest_ref[k * (dest_ref.shape[0] // TOP_K) + tok]
                pltpu.make_async_copy(ys_ref.at[d], buf.at[sl, k * tc + t], sem.at[sl]).start(priority=k % 2)
            return c

        lax.fori_loop(0, tc, per_tok, 0, unroll=8)

    @pl.when(i == 0)
    def _():
        issue(0, 0)

    @pl.when(i + 1 < n)
    def _():
        issue(i + 1, 1 - slot)

    for _ in range(TOP_K):
        pltpu.make_async_copy(ys_ref.at[pl.ds(0, tc)], buf.at[slot, pl.ds(0, tc)], sem.at[slot]).wait()

    moe = ((buf[slot, pl.ds(0, tc)] + buf[slot, pl.ds(tc, tc)])
           + (buf[slot, pl.ds(2 * tc, tc)] + buf[slot, pl.ds(3 * tc, tc)])).reshape(tc, D_MODEL)
    m = mod_ref[0]
    x2 = x1_ref[...] + m[5:6] * moe
    o_ref[...] = _rms(x2) * fg_ref[...]


def _gather_combine(dest_flat, ys, x1, mod3, fg):
    n_tok = x1.shape[0]
    tc = GATHER_TILE
    per_b = SEQ // tc
    n_lt = D_MODEL // LANES
    return pl.pallas_call(
        _gather_combine_kernel,
        out_shape=jax.ShapeDtypeStruct((n_tok, D_MODEL), F32),
        grid_spec=pltpu.PrefetchScalarGridSpec(
            num_scalar_prefetch=1,
            grid=(n_tok // tc,),
            in_specs=[pl.BlockSpec(memory_space=pl.ANY),
                      pl.BlockSpec((tc, D_MODEL), lambda i, d: (i, 0)),
                      pl.BlockSpec((1, N_MOD, D_MODEL), lambda i, d: (i // per_b, 0, 0)),
                      pl.BlockSpec((1, D_MODEL), lambda i, d: (0, 0))],
            out_specs=pl.BlockSpec((tc, D_MODEL), lambda i, d: (i, 0)),
            scratch_shapes=[pltpu.VMEM((2, TOP_K * tc, n_lt, LANES), F32), pltpu.SemaphoreType.DMA((2,))]),
        compiler_params=pltpu.CompilerParams(dimension_semantics=("arbitrary",),
                                             vmem_limit_bytes=VMEM_LIMIT),
        name="combine",
    )(dest_flat, ys, x1, mod3, fg)


def _expansion_matrices(src0):
    r = (jnp.arange(2 * LANES) % LANES)[:, None]
    out64, out128 = [], []
    for d in range(2):
        l64 = jnp.arange(D_SSD)[None, :]
        l128 = jnp.arange(HEADS * LANES)[None, :]
        out64.append((l64 // HEAD_DIM == r - src0 - HEADS * d).astype(BF16))
        out128.append((l128 // LANES == r - src0 - HEADS * d).astype(BF16))
    return jnp.stack(out64), jnp.stack(out128)


def _pad_lanes(v):
    return jnp.pad(v, [(0, 0)] * (v.ndim - 1) + [(0, LANES - v.shape[-1])])


def kernel(x, c, ctx, c_ctx, w_mod, b_mod, norm1_g, w_in, ssd_conv_w, ssd_conv_b, ssd_dt_bias, ssd_a_log,
           ssd_d, ssd_norm_g, sc_conv_w, w_out, norm2_g, w_router, b_router, w_gate_up, b_gate_up, w_down,
           b_down, final_g):
    bsz = x.shape[0]
    n_tok = bsz * SEQ
    n_assign = n_tok * TOP_K
    n_blocks = n_assign // MOE_BM + N_EXPERTS
    n_rows = n_blocks * MOE_BM
    li = 0

    cvec = jnp.concatenate([c, c_ctx[None, :], jnp.zeros((7, D_MODEL), F32)], axis=0)
    mod3 = _mod(cvec, w_mod, b_mod[li][None, :], li).reshape(bsz + 8, N_MOD, D_MODEL)

    w = w_in[li]
    wz = w[:, Z0:X0].astype(BF16)
    wxbc = w[:, X0:DT0].astype(BF16)
    wdt = _pad_lanes(w[:, DT0:SC0]).astype(BF16)
    wb = w[:, SC0:SC0 + D_SC].astype(BF16)
    wc = w[:, SC0 + D_SC:SC0 + 2 * D_SC].astype(BF16)
    wu = w[:, SC0 + 2 * D_SC:].astype(BF16)
    g1 = norm1_g[li][None, :]
    cw = ssd_conv_w[li]
    cb = ssd_conv_b[li][None, :]
    dtb = _pad_lanes(ssd_dt_bias[li].reshape(1, 2 * HEADS))
    alog = _pad_lanes(ssd_a_log[li].reshape(1, 2 * HEADS))
    e64_ctx, _ = _expansion_matrices(0)

    h0 = _ctx_states(ctx, mod3, g1, wxbc[:, :XB_W], wdt, cw[:, :XB_W], cb[:, :XB_W], dtb, alog, e64_ctx)

    rep = lambda a: _pad_lanes(jnp.tile(a[..., :2 * HEADS], (1, DT_COPIES)))
    e64, e128 = _expansion_matrices(DT_DA0)
    x2 = x.reshape(n_tok, D_MODEL)
    z, xbc, dtp, scb, v = _inproj(x2, mod3, g1, wz, wxbc, rep(wdt), rep(dtb), rep(alog), wb, wc, wu)

    dsk = jnp.repeat(ssd_d[li], HEAD_DIM)[None, :]
    yssd = _ssd(xbc.reshape(bsz, SEQ, XBC_W), z.reshape(bsz, SEQ, D_SSD), dtp.reshape(bsz, SEQ, LANES), h0,
                cw, cb, dsk, ssd_norm_g[li][None, :], e64, e128)

    wo = w_out[li].astype(BF16)
    g2 = norm2_g[li][None, :]
    wr = w_router[li].T
    wr_hi = wr.astype(BF16)
    wr_lo = (wr - wr_hi.astype(F32)).astype(BF16)
    x1, lgt = _outproj(x2, yssd.reshape(n_tok, D_SSD), scb, v, mod3, sc_conv_w[li], wo[:D_SSD], wo[D_SSD:],
                       g2, jnp.concatenate([wr_hi, wr_hi, wr_lo], axis=1), b_router[li][:, None])

    dest_t, gate_t, idx_t, meta = _route(lgt, n_blocks)
    dest_flat = dest_t.reshape(n_assign)
    cnt = meta[0, :N_EXPERTS]
    start = meta[1, :N_EXPERTS]
    blk_exp = meta[2, :n_blocks]
    n_used = meta[3, :1]

    meta_rows = lax.bitcast_convert_type(_pad_lanes(jnp.concatenate(
        [idx_t.T.astype(F32), gate_t.T], axis=1)), jnp.uint32)
    pad_meta = lax.bitcast_convert_type(_pad_lanes(jnp.concatenate(
        [jnp.full((MOE_BM, TOP_K), -1.0, F32), jnp.zeros((MOE_BM, TOP_K), F32)], axis=1)), jnp.uint32)
    zsrc = jnp.concatenate([jnp.zeros((MOE_BM, PACK_W), jnp.uint32), pad_meta,
                            jnp.zeros((MOE_BM, D_MODEL - PACK_W - LANES), jnp.uint32)],
                           axis=1).reshape(MOE_BM, SUBLANES, LANES)

    xs = _dispatch(dest_flat, cnt, start, n_used, x1, meta_rows, mod3, g2, zsrc, n_rows)
    ys = _expert_rows(blk_exp, n_used, xs, w_gate_up[li], b_gate_up[li][:, None, :],
                      w_down[li], b_down[li][:, None, :])
    out = _gather_combine(dest_flat, ys, x1, mod3, final_g[None, :])
    return out.reshape(bsz, SEQ, D_MODEL)
```

```python
import functools

import jax
import jax.numpy as jnp
from jax import lax
from jax.experimental import pallas as pl
from jax.experimental.pallas import tpu as pltpu

F32 = jnp.float32
BF16 = jnp.bfloat16
HIGHEST = lax.Precision.HIGHEST

D_MODEL = 1024
SEQ = 2048
CTX_LEN = 256
GRID_W = 64
D_SSD = 1024
D_SC = 1024
HEAD_DIM = 64
HEADS = 16
GROUPS = 2
STATE = 128
CHUNK = 128
N_EXPERTS = 32
TOP_K = 4
D_FF = 1024
SWIGLU_LIMIT = 7.0
SWIGLU_ALPHA = 1.702
NORM_EPS = 1e-6
N_MOD = 6
XBC_W = D_SSD + 2 * GROUPS * STATE
XB_W = D_SSD + GROUPS * STATE
LANES = 128

Z0 = 0
X0 = Z0 + D_SSD
B0 = X0 + D_SSD
C0 = B0 + GROUPS * STATE
DT0 = C0 + GROUPS * STATE
SC0 = DT0 + 2 * HEADS

TOK_TILE = 1024
OUT_TILE = 1024
MOE_BM = 512
MOE_PARTS = 2
RT_TILE = 512
CTX_BATCH = 4
DISP_TILE = 512
GATHER_TILE = 128
DT_COPIES = 3
DT_LOG0 = 2 * HEADS
DT_DA0 = 4 * HEADS
SUBLANES = 8
PACK_W = D_MODEL // 2
META_IDX = 0
META_GATE = TOP_K
VMEM_LIMIT = 56 * 1024 * 1024


def _silu(v):
    return v * jax.nn.sigmoid(v)


def _softplus(v):
    return jnp.maximum(v, 0.0) + jnp.log1p(jnp.exp(-jnp.abs(v)))


def _rms(v):
    return v * lax.rsqrt(jnp.mean(v * v, axis=-1, keepdims=True) + NORM_EPS)


def _dot(a, b):
    return jnp.dot(a, b, preferred_element_type=F32)


def _expand2(v, e2):
    hi = v.astype(BF16)
    lo = (v - hi.astype(F32)).astype(BF16)
    return _dot(jnp.concatenate([hi, lo], axis=1), e2)


def _mod_kernel(c_ref, w_ref, b_ref, o_ref):
    o_ref[...] = jnp.dot(_silu(c_ref[...]), w_ref[0], precision=HIGHEST,
                         preferred_element_type=F32) + b_ref[...]


def _mod(cvec, w_mod, b_mod, layer):
    rows = cvec.shape[0]
    n = w_mod.shape[2]
    tn = 1536
    return pl.pallas_call(
        _mod_kernel,
        out_shape=jax.ShapeDtypeStruct((rows, n), F32),
        grid=(n // tn,),
        in_specs=[pl.BlockSpec((rows, D_MODEL), lambda j: (0, 0)),
                  pl.BlockSpec((1, D_MODEL, tn), lambda j: (layer, 0, j)),
                  pl.BlockSpec((1, tn), lambda j: (0, j))],
        out_specs=pl.BlockSpec((rows, tn), lambda j: (0, j)),
        compiler_params=pltpu.CompilerParams(dimension_semantics=("arbitrary",),
                                             vmem_limit_bytes=VMEM_LIMIT),
        name="mod",
    )(cvec, w_mod, b_mod)


def _ctx_kernel(ctx_ref, mod_ref, g1_ref, wxb_ref, wdt_ref, cw_ref, cb_ref, dtb_ref, alog_ref, e64_ref,
                h0_ref):
    L = CTX_LEN
    nb = ctx_ref.shape[0]
    m = mod_ref[0]
    hc = _rms(ctx_ref[...].reshape(nb * L, D_MODEL)) * g1_ref[...] * (1.0 + m[1:2]) + m[0:1]
    hb = hc.astype(BF16)
    pxb = _dot(hb, wxb_ref[...])
    dtr = _dot(hb, wdt_ref[...])
    rowl = lax.broadcasted_iota(jnp.int32, (nb * L, XB_W), 0) & (L - 1)
    dn = jnp.where(rowl == 0, 0.0, pltpu.roll(pxb, 1, 0))
    up = jnp.where(rowl == L - 1, 0.0, pltpu.roll(pxb, nb * L - 1, 0))
    cw = cw_ref[...]
    xb = _silu(cw[0:1] * dn + cw[1:2] * pxb + cw[2:3] * up + cb_ref[...])
    dt_all = _softplus(dtr + dtb_ref[...])
    da_all = dt_all * (-jnp.exp(alog_ref[...]))
    ri = lax.broadcasted_iota(jnp.int32, (L, L), 0)
    ci = lax.broadcasted_iota(jnp.int32, (L, L), 1)
    for bi in range(nb):
        rs = slice(bi * L, (bi + 1) * L)
        xs = xb[rs, :D_SSD]
        bm = xb[rs, D_SSD:].astype(BF16)
        dt = dt_all[rs]
        da = da_all[rs]
        for d in range(2):
            tri = (ci <= ri) if d == 0 else (ci >= ri)
            cum = jnp.dot(tri.astype(F32), da, precision=HIGHEST, preferred_element_type=F32)
            last = cum[L - 1:L] if d == 0 else cum[0:1]
            w_e = _expand2(jnp.exp(last - cum) * dt, e64_ref[d])
            xw = (xs * w_e).astype(BF16)
            for g in range(GROUPS):
                gw = D_SSD // GROUPS
                st = lax.dot_general(bm[:, g * STATE:(g + 1) * STATE], xw[:, g * gw:(g + 1) * gw],
                                     (((0,), (0,)), ((), ())), preferred_element_type=F32)
                h0_ref[bi, d, :, g * gw:(g + 1) * gw] = st


def _ctx_states(ctx, mod3, g1, wxb, wdt, cw, cb, dtb, alog, e64):
    bsz = ctx.shape[0]
    mod_row = bsz
    nb = CTX_BATCH if bsz % CTX_BATCH == 0 else 1
    const = lambda *shape: pl.BlockSpec(shape, lambda b: (0,) * len(shape))
    return pl.pallas_call(
        _ctx_kernel,
        out_shape=jax.ShapeDtypeStruct((bsz, 2, STATE, D_SSD), F32),
        grid=(bsz // nb,),
        in_specs=[pl.BlockSpec((nb, CTX_LEN, D_MODEL), lambda b: (b, 0, 0)),
                  pl.BlockSpec((1, N_MOD, D_MODEL), lambda b: (mod_row, 0, 0)),
                  const(1, D_MODEL), const(D_MODEL, XB_W), const(D_MODEL, LANES),
                  const(3, XB_W), const(1, XB_W), const(1, LANES), const(1, LANES),
                  const(2, 2 * LANES, D_SSD)],
        out_specs=pl.BlockSpec((nb, 2, STATE, D_SSD), lambda b: (b, 0, 0, 0)),
        compiler_params=pltpu.CompilerParams(dimension_semantics=("arbitrary",),
                                             vmem_limit_bytes=VMEM_LIMIT),
        name="ctx_states",
    )(ctx, mod3, g1, wxb, wdt, cw, cb, dtb, alog, e64)


def _inproj_kernel(x_ref, mod_ref, g1_ref, wz_ref, wxbc_ref, wdt_ref, dtb_ref, alog_ref, wb_ref, wc_ref, wu_ref,
                   z_ref, xbc_ref, dt_ref, scb_ref, v_ref):
    m = mod_ref[0]
    hx = _rms(x_ref[...]) * g1_ref[...] * (1.0 + m[1:2]) + m[0:1]
    hb = hx.astype(BF16)
    z_ref[...] = _dot(hb, wz_ref[...]).astype(BF16)
    xbc_ref[...] = _dot(hb, wxbc_ref[...]).astype(BF16)
    dt = _softplus(_dot(hb, wdt_ref[...]) + dtb_ref[...])
    lane = lax.broadcasted_iota(jnp.int32, dt.shape, 1)
    dt_ref[...] = jnp.where(lane < DT_LOG0, dt,
                            jnp.where(lane < DT_DA0, jnp.log(dt), dt * (-jnp.exp(alog_ref[...]))))
    scb_ref[...] = _dot(hb, wb_ref[...]).astype(BF16)
    v_ref[...] = (_dot(hb, wc_ref[...]) * _dot(hb, wu_ref[...])).astype(BF16)


def _inproj(x2, mod3, g1, wz, wxbc, wdt, dtb, alog, wb, wc, wu):
    t = x2.shape[0]
    tm = TOK_TILE
    per_b = SEQ // tm
    const = lambda *shape: pl.BlockSpec(shape, lambda i: (0,) * len(shape), pipeline_mode=pl.Buffered(1))
    tile = lambda w: pl.BlockSpec((tm, w), lambda i: (i, 0))
    return pl.pallas_call(
        _inproj_kernel,
        out_shape=(jax.ShapeDtypeStruct((t, D_SSD), BF16), jax.ShapeDtypeStruct((t, XBC_W), BF16),
                   jax.ShapeDtypeStruct((t, LANES), F32), jax.ShapeDtypeStruct((t, D_SC), BF16),
                   jax.ShapeDtypeStruct((t, D_SC), BF16)),
        grid=(t // tm,),
        in_specs=[tile(D_MODEL),
                  pl.BlockSpec((1, N_MOD, D_MODEL), lambda i: (i // per_b, 0, 0)),
                  const(1, D_MODEL), const(D_MODEL, D_SSD), const(D_MODEL, XBC_W), const(D_MODEL, LANES),
                  const(1, LANES), const(1, LANES),
                  const(D_MODEL, D_SC), const(D_MODEL, D_SC), const(D_MODEL, D_SC)],
        out_specs=(tile(D_SSD), tile(XBC_W), tile(LANES), tile(D_SC), tile(D_SC)),
        compiler_params=pltpu.CompilerParams(dimension_semantics=("arbitrary",),
                                             vmem_limit_bytes=VMEM_LIMIT),
        name="inproj",
    )(x2, mod3, g1, wz, wxbc, wdt, dtb, alog, wb, wc, wu)


def _ssd_kernel(xbc_ref, z_ref, dt_ref, h0_ref, cw_ref, cb_ref, dsk_ref, g_ref,
                e64_ref, e128_ref, o_ref, xc_ref, y_ref, s_ref):
    Q = CHUNK
    nck = SEQ // Q
    gw = D_SSD // GROUPS

    rowi = lax.broadcasted_iota(jnp.int32, (SUBLANES, XBC_W), 0)

    def conv_body(c, carry):
        r0 = pl.multiple_of(c * Q, Q)
        main = xbc_ref[0, pl.ds(r0, Q), :].astype(F32)
        pstart = pl.multiple_of(jnp.maximum(r0 - 16, 0), 16)
        nstart = pl.multiple_of(jnp.minimum(r0 + Q, SEQ - 16), 16)
        prev = xbc_ref[0, pl.ds(pstart, 16), :].astype(F32)[15:16]
        nxt = xbc_ref[0, pl.ds(nstart, 16), :].astype(F32)[0:1]
        prev = jnp.where(c > 0, prev, 0.0)
        nxt = jnp.where(c < nck - 1, nxt, 0.0)
        dn = pltpu.roll(main, 1, 0)
        up = pltpu.roll(main, Q - 1, 0)
        dn = jnp.concatenate([jnp.where(rowi == 0, prev, dn[0:SUBLANES]), dn[SUBLANES:]], axis=0)
        up = jnp.concatenate([up[:Q - SUBLANES], jnp.where(rowi == SUBLANES - 1, nxt, up[Q - SUBLANES:])], axis=0)
        cw = cw_ref[...]
        conv = cw[0:1] * dn + cw[1:2] * main + cw[2:3] * up + cb_ref[...]
        xc_ref[pl.ds(r0, Q), :] = _silu(conv).astype(BF16)
        return carry

    lax.fori_loop(0, nck, conv_body, 0)

    ri = lax.broadcasted_iota(jnp.int32, (Q, Q), 0)
    ci = lax.broadcasted_iota(jnp.int32, (Q, Q), 1)
    lane = lax.broadcasted_iota(jnp.int32, (Q, LANES), 1)
    da_lanes = jnp.logical_and(lane >= DT_DA0, lane < DT_DA0 + 2 * HEADS)

    def chunk(c, d, first):
        r0 = pl.multiple_of(c * Q, Q)
        rows = pl.ds(r0, Q)
        xs_b = xc_ref[rows, 0:D_SSD]
        xs = xs_b.astype(F32)
        bm = xc_ref[rows, D_SSD:D_SSD + GROUPS * STATE]
        cm = xc_ref[rows, D_SSD + GROUPS * STATE:XBC_W]
        dtp = dt_ref[0, rows, :]
        da = jnp.where(da_lanes, dtp, 0.0)
        dt = pltpu.roll(dtp, DT_DA0, 1)
        log_dt = pltpu.roll(dtp, DT_DA0 - DT_LOG0, 1)
        tri = (ci <= ri) if d == 0 else (ci >= ri)
        p0 = da.astype(BF16)
        r1 = da - p0.astype(F32)
        p1 = r1.astype(BF16)
        p2 = (r1 - p1.astype(F32)).astype(BF16)
        tri_b = jnp.where(tri, 1.0, 0.0).astype(BF16)
        cum = _dot(jnp.concatenate([tri_b, tri_b, tri_b], axis=1),
                   jnp.concatenate([p0, p1, p2], axis=0))
        sub_t = (cum - log_dt).T
        last = cum[Q - 1:Q] if d == 0 else cum[0:1]
        ecum_e = _expand2(jnp.exp(cum), e64_ref[d])
        w_e = _expand2(jnp.where(da_lanes, jnp.exp(last - cum) * dt, 0.0), e64_ref[d])
        colb = _expand2(cum, e128_ref[d])
        decay_e = ecum_e[Q - 1:Q] if d == 0 else ecum_e[0:1]

        gmat = [lax.dot_general(cm[:, g * STATE:(g + 1) * STATE], bm[:, g * STATE:(g + 1) * STATE],
                                (((1,), (1,)), ((), ())), preferred_element_type=F32)
                for g in range(GROUPS)]
        zero_b = jnp.zeros((Q, LANES), BF16)
        y_parts = []
        for p in range(HEADS // 2):
            g = (2 * p) // (HEADS // GROUPS)
            ms = []
            for hh in (2 * p, 2 * p + 1):
                src = DT_DA0 + HEADS * d + hh
                seg = colb[:, hh * LANES:(hh + 1) * LANES] - sub_t[src:src + 1, :]
                ms.append((jnp.where(tri, jnp.exp(seg), 0.0) * gmat[g]).astype(BF16))
            mcat = jnp.concatenate(ms, axis=1)
            xp = xs_b[:, p * LANES:(p + 1) * LANES]
            rhs = jnp.concatenate([jnp.where(lane < HEAD_DIM, xp, zero_b),
                                   jnp.where(lane >= HEAD_DIM, xp, zero_b)], axis=0)
            y_parts.append(_dot(mcat, rhs))
        y_diag = jnp.concatenate(y_parts, axis=1)

        s_old = s_ref[d]
        s_bf = s_old.astype(BF16)
        y_off = jnp.concatenate(
            [_dot(cm[:, g * STATE:(g + 1) * STATE], s_bf[:, g * gw:(g + 1) * gw]) for g in range(GROUPS)],
            axis=1)
        y = y_diag + y_off * ecum_e

        xw = (xs * w_e).astype(BF16)
        upd = jnp.concatenate(
            [lax.dot_general(bm[:, g * STATE:(g + 1) * STATE], xw[:, g * gw:(g + 1) * gw],
                             (((0,), (0,)), ((), ())), preferred_element_type=F32) for g in range(GROUPS)],
            axis=1)
        s_ref[d] = s_old * decay_e + upd

        if first:
            y_ref[rows, :] = y + dsk_ref[...] * xs
        else:
            tot = y_ref[rows, :] + y
            zz = z_ref[0, rows, :].astype(F32)
            gz = tot * _silu(zz)
            outs = []
            for g in range(GROUPS):
                gg = gz[:, g * gw:(g + 1) * gw]
                outs.append(gg * lax.rsqrt(jnp.mean(gg * gg, axis=-1, keepdims=True) + NORM_EPS))
            o_ref[0, rows, :] = (jnp.concatenate(outs, axis=1) * g_ref[...]).astype(BF16)

    s_ref[...] = h0_ref[0]

    def first_half(i, carry):
        chunk(i, 0, True)
        chunk(nck - 1 - i, 1, True)
        return carry

    def second_half(i, carry):
        chunk(i, 0, False)
        chunk(nck - 1 - i, 1, False)
        return carry

    lax.fori_loop(0, nck // 2, first_half, 0, unroll=2)
    lax.fori_loop(nck // 2, nck, second_half, 0, unroll=2)


def _ssd(xbc3, z3, dt3, h0, cw, cb, dsk, g, e64, e128):
    bsz = xbc3.shape[0]
    const = lambda *shape: pl.BlockSpec(shape, lambda b: (0,) * len(shape))
    seq = lambda w: pl.BlockSpec((1, SEQ, w), lambda b: (b, 0, 0))
    return pl.pallas_call(
        _ssd_kernel,
        out_shape=jax.ShapeDtypeStruct((bsz, SEQ, D_SSD), BF16),
        grid=(bsz,),
        in_specs=[seq(XBC_W), seq(D_SSD), seq(LANES),
                  pl.BlockSpec((1, 2, STATE, D_SSD), lambda b: (b, 0, 0, 0)),
                  const(3, XBC_W), const(1, XBC_W),
                  const(1, D_SSD), const(1, D_SSD), const(2, 2 * LANES, D_SSD),
                  const(2, 2 * LANES, HEADS * LANES)],
        out_specs=seq(D_SSD),
        scratch_shapes=[pltpu.VMEM((SEQ, XBC_W), BF16), pltpu.VMEM((SEQ, D_SSD), F32),
                        pltpu.VMEM((2, STATE, D_SSD), F32)],
        compiler_params=pltpu.CompilerParams(dimension_semantics=("arbitrary",),
                                             vmem_limit_bytes=VMEM_LIMIT),
        name="ssd",
    )(xbc3, z3, dt3, h0, cw, cb, dsk, g, e64, e128)


def _outproj_kernel(x_ref, yssd_ref, scb_ref, v_ref, vp_ref, vn_ref, mod_ref, scw_ref, wo1_ref, wo2_ref,
                    g2_ref, wrt_ref, br_ref, x1_ref, lg_ref):
    tm = OUT_TILE
    per_b = SEQ // tm
    i = pl.program_id(0)
    first = (i % per_b) == 0
    last = (i % per_b) == per_b - 1
    m = mod_ref[0]
    v = v_ref[...].astype(F32)
    vp = jnp.where(first, 0.0, vp_ref[...].astype(F32))
    vn = jnp.where(last, 0.0, vn_ref[...].astype(F32))
    dn = jnp.concatenate([vp, v[:tm - GRID_W]], axis=0)
    up = jnp.concatenate([v[GRID_W:], vn], axis=0)
    scw = scw_ref[...]
    ysc = scb_ref[...].astype(F32) * (scw[0:1] * dn + scw[1:2] * v + scw[2:3] * up)
    out = _dot(yssd_ref[...], wo1_ref[...]) + _dot(ysc.astype(BF16), wo2_ref[...])
    x1 = x_ref[...] + m[2:3] * out
    x1_ref[...] = x1
    h2 = _rms(x1) * g2_ref[...] * (1.0 + m[4:5]) + m[3:4]
    h_hi = h2.astype(BF16)
    h_lo = (h2 - h_hi.astype(F32)).astype(BF16)
    lg_ref[...] = lax.dot_general(wrt_ref[...], jnp.concatenate([h_hi, h_lo, h_hi], axis=1),
                                  (((1,), (1,)), ((), ())), preferred_element_type=F32) + br_ref[...]


def _outproj(x2, yssd, scb, v, mod3, scw, wo1, wo2, g2, wrt, br):
    t = x2.shape[0]
    tm = OUT_TILE
    per_b = SEQ // tm
    r = tm // GRID_W
    nrow = t // GRID_W
    const = lambda *shape: pl.BlockSpec(shape, lambda i: (0,) * len(shape))
    tile = lambda w: pl.BlockSpec((tm, w), lambda i: (i, 0))
    return pl.pallas_call(
        _outproj_kernel,
        out_shape=(jax.ShapeDtypeStruct((t, D_MODEL), F32), jax.ShapeDtypeStruct((N_EXPERTS, t), F32)),
        grid=(t // tm,),
        in_specs=[tile(D_MODEL), tile(D_SSD), tile(D_SC), tile(D_SC),
                  pl.BlockSpec((GRID_W, D_SC), lambda i: (jnp.maximum(i * r - 1, 0), 0)),
                  pl.BlockSpec((GRID_W, D_SC), lambda i: (jnp.minimum((i + 1) * r, nrow - 1), 0)),
                  pl.BlockSpec((1, N_MOD, D_MODEL), lambda i: (i // per_b, 0, 0)),
                  const(3, D_SC), const(D_SSD, D_MODEL), const(D_SC, D_MODEL), const(1, D_MODEL),
                  const(N_EXPERTS, 3 * D_MODEL), const(N_EXPERTS, 1)],
        out_specs=(tile(D_MODEL), pl.BlockSpec((N_EXPERTS, tm), lambda i: (0, i))),
        compiler_params=pltpu.CompilerParams(dimension_semantics=("arbitrary",),
                                             vmem_limit_bytes=VMEM_LIMIT),
        name="outproj",
    )(x2, yssd, scb, v, v, v, mod3, scw, wo1, wo2, g2, wrt, br)


def _route_kernel(lg_ref, dest_ref, gate_ref, idx_ref, meta_ref, rank_ref, carry_ref, *, n_tok, n_blocks):
    tt = RT_TILE
    ne = N_EXPERTS
    eio = lax.broadcasted_iota(jnp.int32, (ne, tt), 0)
    si = lax.broadcasted_iota(jnp.int32, (tt, tt), 0)
    ti = lax.broadcasted_iota(jnp.int32, (tt, tt), 1)
    before = (si < ti).astype(BF16)
    carry_ref[...] = jnp.zeros_like(carry_ref)

    def tile_body(j, c):
        t0 = pl.multiple_of(j * tt, tt)
        l = lg_ref[:, pl.ds(t0, tt)]
        onehot = jnp.zeros((ne, tt), F32)
        tops, sels = [], []
        for _ in range(TOP_K):
            mx = jnp.max(l, axis=0, keepdims=True)
            idx = jnp.min(jnp.where(l == mx, eio, ne), axis=0, keepdims=True)
            sel = eio == idx
            l = jnp.where(sel, -jnp.inf, l)
            onehot = onehot + sel.astype(F32)
            tops.append(mx)
            sels.append(sel)
            idx_ref[pl.ds(len(tops) - 1, 1), pl.ds(t0, tt)] = idx
        ex = [jnp.exp(tv - tops[0]) for tv in tops]
        den = ex[0] + ex[1] + ex[2] + ex[3]
        prefix = _dot(onehot.astype(BF16), before) + carry_ref[:, 0:1]
        for k in range(TOP_K):
            gate_ref[pl.ds(k, 1), pl.ds(t0, tt)] = ex[k] / den
            rk = jnp.sum(jnp.where(sels[k], prefix, 0.0), axis=0, keepdims=True)
            rank_ref[pl.ds(k, 1), pl.ds(t0, tt)] = rk.astype(jnp.int32)
        carry_ref[...] = carry_ref[...] + jnp.sum(onehot, axis=1, keepdims=True)
        return c

    lax.fori_loop(0, n_tok // tt, tile_body, 0)

    counts = carry_ref[...]
    padded = jnp.floor((counts + (MOE_BM - 1)) * (1.0 / MOE_BM)) * MOE_BM
    er = lax.broadcasted_iota(jnp.int32, (ne, ne), 0)
    ec = lax.broadcasted_iota(jnp.int32, (ne, ne), 1)
    pad_start = jnp.dot((ec < er).astype(F32), padded, precision=HIGHEST, preferred_element_type=F32)
    pad_end = pad_start + padded

    def dest_body(j, c):
        t0 = pl.multiple_of(j * tt, tt)
        for k in range(TOP_K):
            idx = idx_ref[pl.ds(k, 1), pl.ds(t0, tt)]
            base = jnp.sum(jnp.where(eio == idx, pad_start[:, 0:1], 0.0), axis=0, keepdims=True)
            dest_ref[pl.ds(k, 1), pl.ds(t0, tt)] = base.astype(jnp.int32) + rank_ref[pl.ds(k, 1), pl.ds(t0, tt)]
        return c

    lax.fori_loop(0, n_tok // tt, dest_body, 0)

    width = meta_ref.shape[1]
    sub = lax.broadcasted_iota(jnp.int32, (ne, width), 0)
    lan = lax.broadcasted_iota(jnp.int32, (ne, width), 1)
    diag = sub == lan
    cnt_row = jnp.sum(jnp.where(diag, counts[:, 0:1], 0.0), axis=0, keepdims=True)
    start_row = jnp.sum(jnp.where(diag, pad_start[:, 0:1], 0.0), axis=0, keepdims=True)
    blk_start = (lan * MOE_BM).astype(F32)
    blk_exp = jnp.sum((pad_end[:, 0:1] <= blk_start).astype(F32), axis=0, keepdims=True)
    blk_exp = jnp.minimum(blk_exp, float(ne - 1))
    used = jnp.sum(padded[:, 0:1], axis=0, keepdims=True) * (1.0 / MOE_BM)
    meta_ref[0:1, :] = cnt_row.astype(jnp.int32)
    meta_ref[1:2, :] = start_row.astype(jnp.int32)
    meta_ref[2:3, :] = blk_exp.astype(jnp.int32)
    meta_ref[3:4, :] = jnp.broadcast_to(used, (1, width)).astype(jnp.int32)
    meta_ref[4:8, :] = jnp.zeros((4, width), jnp.int32)


def _route(lgt, n_blocks):
    ne, n_tok = lgt.shape
    width = -(-n_blocks // LANES) * LANES
    full = lambda *shape: pl.BlockSpec(shape, lambda: (0,) * len(shape))
    return pl.pallas_call(
        functools.partial(_route_kernel, n_tok=n_tok, n_blocks=n_blocks),
        out_shape=(jax.ShapeDtypeStruct((TOP_K, n_tok), jnp.int32),
                   jax.ShapeDtypeStruct((TOP_K, n_tok), F32),
                   jax.ShapeDtypeStruct((TOP_K, n_tok), jnp.int32),
                   jax.ShapeDtypeStruct((8, width), jnp.int32)),
        in_specs=[full(ne, n_tok)],
        out_specs=(full(TOP_K, n_tok), full(TOP_K, n_tok), full(TOP_K, n_tok), full(8, width)),
        scratch_shapes=[pltpu.VMEM((TOP_K, n_tok), jnp.int32), pltpu.VMEM((ne, LANES), F32)],
        compiler_params=pltpu.CompilerParams(vmem_limit_bytes=VMEM_LIMIT),
        name="route",
    )(lgt)


def _dispatch_kernel(dest_ref, cnt_ref, start_ref, nu_ref, x1_ref, meta_ref, mod_ref, g2_ref, zsrc_ref, xs_ref,
                     hbuf, sem, zsem):
    i = pl.program_id(0)
    n = pl.num_programs(0)
    tl = DISP_TILE
    slot = i % 2
    nb = xs_ref.shape[0] // MOE_BM

    def zero_block(b):
        return pltpu.make_async_copy(zsrc_ref, xs_ref.at[pl.ds(b * MOE_BM, MOE_BM)], zsem)

    @pl.when(i == 0)
    def _():
        def start_e(e, c):
            @pl.when(cnt_ref[e] > 0)
            def _():
                zero_block((start_ref[e] + cnt_ref[e] - 1) // MOE_BM).start()
            return c

        def wait_e(e, c):
            @pl.when(cnt_ref[e] > 0)
            def _():
                zero_block(0).wait()
            return c

        def start_t(b, c):
            zero_block(b).start()
            return c

        def wait_t(b, c):
            zero_block(0).wait()
            return c

        lax.fori_loop(0, N_EXPERTS, start_e, 0)
        lax.fori_loop(nu_ref[0], nb, start_t, 0)
        lax.fori_loop(0, N_EXPERTS, wait_e, 0)
        lax.fori_loop(nu_ref[0], nb, wait_t, 0)

    m = mod_ref[0]
    h2 = _rms(x1_ref[...]) * g2_ref[...] * (1.0 + m[4:5]) + m[3:4]
    lo = pltpu.bitcast(h2[:, :PACK_W].astype(BF16).astype(F32), jnp.uint32) >> 16
    hi = pltpu.bitcast(h2[:, PACK_W:].astype(BF16).astype(F32), jnp.uint32) & jnp.uint32(0xFFFF0000)
    row = jnp.concatenate([lo | hi, meta_ref[...], jnp.zeros((tl, D_MODEL - PACK_W - LANES), jnp.uint32)], axis=1)
    hbuf[slot] = row.reshape(tl, SUBLANES, LANES)

    def per_tok(t, c):
        tok = i * tl + t
        for k in range(TOP_K):
            d = dest_ref[k * (dest_ref.shape[0] // TOP_K) + tok]
            pltpu.make_async_copy(hbuf.at[slot, t], xs_ref.at[d], sem.at[slot]).start(priority=k % 2)
        return c

    lax.fori_loop(0, tl, per_tok, 0, unroll=8)

    def wait_slot(sl):
        for _ in range(TOP_K):
            pltpu.make_async_copy(hbuf.at[sl], xs_ref.at[pl.ds(0, tl)], sem.at[sl]).wait()

    @pl.when(i > 0)
    def _():
        wait_slot(1 - slot)

    @pl.when(i == n - 1)
    def _():
        wait_slot(slot)


def _dispatch(dest_flat, cnt, start, n_used, x1, meta_rows, mod3, g2, zsrc, n_rows):
    n_tok = x1.shape[0]
    tl = DISP_TILE
    per_b = SEQ // tl
    return pl.pallas_call(
        _dispatch_kernel,
        out_shape=jax.ShapeDtypeStruct((n_rows, SUBLANES, LANES), jnp.uint32),
        grid_spec=pltpu.PrefetchScalarGridSpec(
            num_scalar_prefetch=4,
            grid=(n_tok // tl,),
            in_specs=[pl.BlockSpec((tl, D_MODEL), lambda i, *_: (i, 0)),
                      pl.BlockSpec((tl, LANES), lambda i, *_: (i, 0)),
                      pl.BlockSpec((1, N_MOD, D_MODEL), lambda i, *_: (i // per_b, 0, 0)),
                      pl.BlockSpec((1, D_MODEL), lambda i, *_: (0, 0)),
                      pl.BlockSpec((MOE_BM, SUBLANES, LANES), lambda i, *_: (0, 0, 0))],
            out_specs=pl.BlockSpec(memory_space=pl.ANY),
            scratch_shapes=[pltpu.VMEM((2, tl, SUBLANES, LANES), jnp.uint32),
                            pltpu.SemaphoreType.DMA((2,)), pltpu.SemaphoreType.DMA]),
        compiler_params=pltpu.CompilerParams(dimension_semantics=("arbitrary",),
                                             vmem_limit_bytes=VMEM_LIMIT),
        name="dispatch",
    )(dest_flat, cnt, start, n_used, x1, meta_rows, mod3, g2, zsrc)


def _expert_rows_kernel(be_ref, nu_ref, xs_ref, wgu_hbm, bgu_ref, wd_hbm, bd_ref, ys_ref,
                        wgu_raw, wd_raw, wgu_bf, wd_bf, w_sem):
    j = pl.program_id(0)
    nu = nu_ref[0]
    n_lt = D_MODEL // LANES

    def weight_copies(e):
        return (pltpu.make_async_copy(wgu_hbm.at[e], wgu_raw, w_sem.at[0]),
                pltpu.make_async_copy(wd_hbm.at[e], wd_raw, w_sem.at[1]))

    @pl.when(j == 0)
    def _():
        for w in weight_copies(be_ref[0]):
            w.start()

    e_now = be_ref[j]
    new_expert = jnp.logical_or(j == 0, be_ref[jnp.maximum(j - 1, 0)] != e_now)

    @pl.when(jnp.logical_and(j < nu, new_expert))
    def _():
        for w in weight_copies(e_now):
            w.wait()
        n_cc = 8
        for c in range(n_cc):
            cc = slice(c * (2 * D_FF // n_cc), (c + 1) * (2 * D_FF // n_cc))
            wgu_bf[:, cc] = wgu_raw[:, cc].astype(BF16)
        for c in range(n_cc // 2):
            cc = slice(c * (2 * D_MODEL // n_cc), (c + 1) * (2 * D_MODEL // n_cc))
            wd_bf[:, cc] = wd_raw[:, cc].astype(BF16)
        j_next = lax.while_loop(lambda t: jnp.logical_and(t < nu, be_ref[jnp.minimum(t, pl.num_programs(0) - 1)] == e_now),
                                lambda t: t + 1, j + 1)

        @pl.when(j_next < nu)
        def _():
            for w in weight_copies(be_ref[jnp.minimum(j_next, pl.num_programs(0) - 1)]):
                w.start()

    @pl.when(j < nu)
    def _():
        e_f = e_now.astype(F32)
        pr = MOE_BM // MOE_PARTS
        for p in range(MOE_PARTS):
            words = xs_ref[pl.ds(p * pr, pr)].reshape(pr, D_MODEL)
            packed = words[:, 0:PACK_W]
            meta = pltpu.bitcast(words[:, PACK_W:PACK_W + LANES], F32)
            xb = jnp.concatenate(
                [pltpu.bitcast(packed << 16, F32).astype(BF16),
                 pltpu.bitcast(packed & jnp.uint32(0xFFFF0000), F32).astype(BF16)], axis=1)
            gate = jnp.zeros((pr, 1), F32)
            for k in range(TOP_K):
                mk = meta[:, META_IDX + k:META_IDX + k + 1] == e_f
                gate = gate + jnp.where(mk, meta[:, META_GATE + k:META_GATE + k + 1], 0.0)
            gu = _dot(xb, wgu_bf[...]) + bgu_ref[0]
            glu = jnp.minimum(gu[:, :D_FF], SWIGLU_LIMIT)
            lin = jnp.clip(gu[:, D_FF:], -SWIGLU_LIMIT, SWIGLU_LIMIT)
            act = glu * jax.nn.sigmoid(SWIGLU_ALPHA * glu) * (lin + 1.0)
            y = (_dot(act.astype(BF16), wd_bf[...]) + bd_ref[0]) * gate
            ys_ref[pl.ds(p * pr, pr)] = y.reshape(pr, n_lt, LANES)

    @pl.when(j >= nu)
    def _():
        ys_ref[...] = jnp.zeros_like(ys_ref)


def _expert_rows(blk_exp, n_used, xs, wgu, bgu, wd, bd):
    n_rows = xs.shape[0]
    nb = n_rows // MOE_BM
    n_lt = D_MODEL // LANES
    row_blk = lambda j, be, nu: (jnp.minimum(j, nu[0] - 1), 0, 0)
    per_e = lambda j, be, nu: (be[j], 0, 0)
    return pl.pallas_call(
        _expert_rows_kernel,
        out_shape=jax.ShapeDtypeStruct((n_rows, n_lt, LANES), F32),
        grid_spec=pltpu.PrefetchScalarGridSpec(
            num_scalar_prefetch=2,
            grid=(nb,),
            in_specs=[pl.BlockSpec((MOE_BM, SUBLANES, LANES), row_blk),
                      pl.BlockSpec(memory_space=pl.ANY),
                      pl.BlockSpec((1, 1, 2 * D_FF), per_e),
                      pl.BlockSpec(memory_space=pl.ANY),
                      pl.BlockSpec((1, 1, D_MODEL), per_e)],
            out_specs=pl.BlockSpec((MOE_BM, n_lt, LANES), lambda j, be, nu: (j, 0, 0)),
            scratch_shapes=[pltpu.VMEM((D_MODEL, 2 * D_FF), F32), pltpu.VMEM((D_FF, D_MODEL), F32),
                            pltpu.VMEM((D_MODEL, 2 * D_FF), BF16), pltpu.VMEM((D_FF, D_MODEL), BF16),
                            pltpu.SemaphoreType.DMA((2,))]),
        compiler_params=pltpu.CompilerParams(dimension_semantics=("arbitrary",),
                                             vmem_limit_bytes=VMEM_LIMIT),
        name="experts",
    )(blk_exp, n_used, xs, wgu, bgu, wd, bd)


def _gather_combine_kernel(dest_ref, ys_ref, x1_ref, mod_ref, fg_ref, o_ref, buf, sem):
    i = pl.program_id(0)
    n = pl.num_programs(0)
    tc = GATHER_TILE
    slot = i % 2

    def issue(tile, sl):
        def per_tok(t, c):
            tok = tile * tc + t
            for k in range(TOP_K):
                d = dest_ref[k * (dest_ref.shape[0] // TOP_K) + tok]
                pltpu.make_async_copy(ys_ref.at[d], buf.at[sl, k * tc + t], sem.at[sl]).start(priority=k % 2)
            return c

        lax.fori_loop(0, tc, per_tok, 0, unroll=8)

    @pl.when(i == 0)
    def _():
        issue(0, 0)

    @pl.when(i + 1 < n)
    def _():
        issue(i + 1, 1 - slot)

    for _ in range(TOP_K):
        pltpu.make_async_copy(ys_ref.at[pl.ds(0, tc)], buf.at[slot, pl.ds(0, tc)], sem.at[slot]).wait()

    moe = ((buf[slot, pl.ds(0, tc)] + buf[slot, pl.ds(tc, tc)])
           + (buf[slot, pl.ds(2 * tc, tc)] + buf[slot, pl.ds(3 * tc, tc)])).reshape(tc, D_MODEL)
    m = mod_ref[0]
    x2 = x1_ref[...] + m[5:6] * moe
    o_ref[...] = _rms(x2) * fg_ref[...]


def _gather_combine(dest_flat, ys, x1, mod3, fg):
    n_tok = x1.shape[0]
    tc = GATHER_TILE
    per_b = SEQ // tc
    n_lt = D_MODEL // LANES
    return pl.pallas_call(
        _gather_combine_kernel,
        out_shape=jax.ShapeDtypeStruct((n_tok, D_MODEL), F32),
        grid_spec=pltpu.PrefetchScalarGridSpec(
            num_scalar_prefetch=1,
            grid=(n_tok // tc,),
            in_specs=[pl.BlockSpec(memory_space=pl.ANY),
                      pl.BlockSpec((tc, D_MODEL), lambda i, d: (i, 0)),
                      pl.BlockSpec((1, N_MOD, D_MODEL), lambda i, d: (i // per_b, 0, 0)),
                      pl.BlockSpec((1, D_MODEL), lambda i, d: (0, 0))],
            out_specs=pl.BlockSpec((tc, D_MODEL), lambda i, d: (i, 0)),
            scratch_shapes=[pltpu.VMEM((2, TOP_K * tc, n_lt, LANES), F32), pltpu.SemaphoreType.DMA((2,))]),
        compiler_params=pltpu.CompilerParams(dimension_semantics=("arbitrary",),
                                             vmem_limit_bytes=VMEM_LIMIT),
        name="combine",
    )(dest_flat, ys, x1, mod3, fg)


def _expansion_matrices(src0):
    r = (jnp.arange(2 * LANES) % LANES)[:, None]
    out64, out128 = [], []
    for d in range(2):
        l64 = jnp.arange(D_SSD)[None, :]
        l128 = jnp.arange(HEADS * LANES)[None, :]
        out64.append((l64 // HEAD_DIM == r - src0 - HEADS * d).astype(BF16))
        out128.append((l128 // LANES == r - src0 - HEADS * d).astype(BF16))
    return jnp.stack(out64), jnp.stack(out128)


def _pad_lanes(v):
    return jnp.pad(v, [(0, 0)] * (v.ndim - 1) + [(0, LANES - v.shape[-1])])


def kernel(x, c, ctx, c_ctx, w_mod, b_mod, norm1_g, w_in, ssd_conv_w, ssd_conv_b, ssd_dt_bias, ssd_a_log,
           ssd_d, ssd_norm_g, sc_conv_w, w_out, norm2_g, w_router, b_router, w_gate_up, b_gate_up, w_down,
           b_down, final_g):
    bsz = x.shape[0]
    n_tok = bsz * SEQ
    n_assign = n_tok * TOP_K
    n_blocks = n_assign // MOE_BM + N_EXPERTS
    n_rows = n_blocks * MOE_BM
    li = 0

    cvec = jnp.concatenate([c, c_ctx[None, :], jnp.zeros((7, D_MODEL), F32)], axis=0)
    mod3 = _mod(cvec, w_mod, b_mod[li][None, :], li).reshape(bsz + 8, N_MOD, D_MODEL)

    w = w_in[li]
    wz = w[:, Z0:X0].astype(BF16)
    wxbc = w[:, X0:DT0].astype(BF16)
    wdt = _pad_lanes(w[:, DT0:SC0]).astype(BF16)
    wb = w[:, SC0:SC0 + D_SC].astype(BF16)
    wc = w[:, SC0 + D_SC:SC0 + 2 * D_SC].astype(BF16)
    wu = w[:, SC0 + 2 * D_SC:].astype(BF16)
    g1 = norm1_g[li][None, :]
    cw = ssd_conv_w[li]
    cb = ssd_conv_b[li][None, :]
    dtb = _pad_lanes(ssd_dt_bias[li].reshape(1, 2 * HEADS))
    alog = _pad_lanes(ssd_a_log[li].reshape(1, 2 * HEADS))
    e64_ctx, _ = _expansion_matrices(0)

    h0 = _ctx_states(ctx, mod3, g1, wxbc[:, :XB_W], wdt, cw[:, :XB_W], cb[:, :XB_W], dtb, alog, e64_ctx)

    rep = lambda a: _pad_lanes(jnp.tile(a[..., :2 * HEADS], (1, DT_COPIES)))
    e64, e128 = _expansion_matrices(DT_DA0)
    x2 = x.reshape(n_tok, D_MODEL)
    z, xbc, dtp, scb, v = _inproj(x2, mod3, g1, wz, wxbc, rep(wdt), rep(dtb), rep(alog), wb, wc, wu)

    dsk = jnp.repeat(ssd_d[li], HEAD_DIM)[None, :]
    yssd = _ssd(xbc.reshape(bsz, SEQ, XBC_W), z.reshape(bsz, SEQ, D_SSD), dtp.reshape(bsz, SEQ, LANES), h0,
                cw, cb, dsk, ssd_norm_g[li][None, :], e64, e128)

    wo = w_out[li].astype(BF16)
    g2 = norm2_g[li][None, :]
    wr = w_router[li].T
    wr_hi = wr.astype(BF16)
    wr_lo = (wr - wr_hi.astype(F32)).astype(BF16)
    x1, lgt = _outproj(x2, yssd.reshape(n_tok, D_SSD), scb, v, mod3, sc_conv_w[li], wo[:D_SSD], wo[D_SSD:],
                       g2, jnp.concatenate([wr_hi, wr_hi, wr_lo], axis=1), b_router[li][:, None])

    dest_t, gate_t, idx_t, meta = _route(lgt, n_blocks)
    dest_flat = dest_t.reshape(n_assign)
    cnt = meta[0, :N_EXPERTS]
    start = meta[1, :N_EXPERTS]
    blk_exp = meta[2, :n_blocks]
    n_used = meta[3, :1]

    meta_rows = lax.bitcast_convert_type(_pad_lanes(jnp.concatenate(
        [idx_t.T.astype(F32), gate_t.T], axis=1)), jnp.uint32)
    pad_meta = lax.bitcast_convert_type(_pad_lanes(jnp.concatenate(
        [jnp.full((MOE_BM, TOP_K), -1.0, F32), jnp.zeros((MOE_BM, TOP_K), F32)], axis=1)), jnp.uint32)
    zsrc = jnp.concatenate([jnp.zeros((MOE_BM, PACK_W), jnp.uint32), pad_meta,
                            jnp.zeros((MOE_BM, D_MODEL - PACK_W - LANES), jnp.uint32)],
                           axis=1).reshape(MOE_BM, SUBLANES, LANES)

    xs = _dispatch(dest_flat, cnt, start, n_used, x1, meta_rows, mod3, g2, zsrc, n_rows)
    ys = _expert_rows(blk_exp, n_used, xs, w_gate_up[li], b_gate_up[li][:, None, :],
                      w_down[li], b_down[li][:, None, :])
    out = _gather_combine(dest_flat, ys, x1, mod3, final_g[None, :])
    return out.reshape(bsz, SEQ, D_MODEL)
```

```python
import functools

import jax
import jax.numpy as jnp
from jax import lax
from jax.experimental import pallas as pl
from jax.experimental.pallas import tpu as pltpu

F32 = jnp.float32
BF16 = jnp.bfloat16
HIGHEST = lax.Precision.HIGHEST

D_MODEL = 1024
SEQ = 2048
CTX_LEN = 256
GRID_W = 64
D_SSD = 1024
D_SC = 1024
HEAD_DIM = 64
HEADS = 16
GROUPS = 2
STATE = 128
CHUNK = 128
N_EXPERTS = 32
TOP_K = 4
D_FF = 1024
SWIGLU_LIMIT = 7.0
SWIGLU_ALPHA = 1.702
NORM_EPS = 1e-6
N_MOD = 6
XBC_W = D_SSD + 2 * GROUPS * STATE
XB_W = D_SSD + GROUPS * STATE
LANES = 128

Z0 = 0
X0 = Z0 + D_SSD
B0 = X0 + D_SSD
C0 = B0 + GROUPS * STATE
DT0 = C0 + GROUPS * STATE
SC0 = DT0 + 2 * HEADS

TOK_TILE = 1024
OUT_TILE = 1024
MOE_BM = 512
MOE_PARTS = 2
RT_TILE = 512
CTX_BATCH = 4
DISP_TILE = 512
GATHER_TILE = 256
DT_COPIES = 3
DT_LOG0 = 2 * HEADS
DT_DA0 = 4 * HEADS
SUBLANES = 8
PACK_W = D_MODEL // 2
META_IDX = 0
META_GATE = TOP_K
VMEM_LIMIT = 56 * 1024 * 1024


def _silu(v):
    return v * jax.nn.sigmoid(v)


def _softplus(v):
    return jnp.maximum(v, 0.0) + jnp.log1p(jnp.exp(-jnp.abs(v)))


def _rms(v):
    return v * lax.rsqrt(jnp.mean(v * v, axis=-1, keepdims=True) + NORM_EPS)


def _dot(a, b):
    return jnp.dot(a, b, preferred_element_type=F32)


def _expand2(v, e2):
    hi = v.astype(BF16)
    lo = (v - hi.astype(F32)).astype(BF16)
    return _dot(jnp.concatenate([hi, lo], axis=1), e2)


def _mod_kernel(c_ref, w_ref, b_ref, o_ref):
    o_ref[...] = jnp.dot(_silu(c_ref[...]), w_ref[0], precision=HIGHEST,
                         preferred_element_type=F32) + b_ref[...]


def _mod(cvec, w_mod, b_mod, layer):
    rows = cvec.shape[0]
    n = w_mod.shape[2]
    tn = 1536
    return pl.pallas_call(
        _mod_kernel,
        out_shape=jax.ShapeDtypeStruct((rows, n), F32),
        grid=(n // tn,),
        in_specs=[pl.BlockSpec((rows, D_MODEL), lambda j: (0, 0)),
                  pl.BlockSpec((1, D_MODEL, tn), lambda j: (layer, 0, j)),
                  pl.BlockSpec((1, tn), lambda j: (0, j))],
        out_specs=pl.BlockSpec((rows, tn), lambda j: (0, j)),
        compiler_params=pltpu.CompilerParams(dimension_semantics=("arbitrary",),
                                             vmem_limit_bytes=VMEM_LIMIT),
        name="mod",
    )(cvec, w_mod, b_mod)


def _ctx_kernel(ctx_ref, mod_ref, g1_ref, wxb_ref, wdt_ref, cw_ref, cb_ref, dtb_ref, alog_ref, e64_ref,
                h0_ref):
    L = CTX_LEN
    nb = ctx_ref.shape[0]
    m = mod_ref[0]
    hc = _rms(ctx_ref[...].reshape(nb * L, D_MODEL)) * g1_ref[...] * (1.0 + m[1:2]) + m[0:1]
    hb = hc.astype(BF16)
    pxb = _dot(hb, wxb_ref[...])
    dtr = _dot(hb, wdt_ref[...])
    rowl = lax.broadcasted_iota(jnp.int32, (nb * L, XB_W), 0) & (L - 1)
    dn = jnp.where(rowl == 0, 0.0, pltpu.roll(pxb, 1, 0))
    up = jnp.where(rowl == L - 1, 0.0, pltpu.roll(pxb, nb * L - 1, 0))
    cw = cw_ref[...]
    xb = _silu(cw[0:1] * dn + cw[1:2] * pxb + cw[2:3] * up + cb_ref[...])
    dt_all = _softplus(dtr + dtb_ref[...])
    da_all = dt_all * (-jnp.exp(alog_ref[...]))
    ri = lax.broadcasted_iota(jnp.int32, (L, L), 0)
    ci = lax.broadcasted_iota(jnp.int32, (L, L), 1)
    for bi in range(nb):
        rs = slice(bi * L, (bi + 1) * L)
        xs = xb[rs, :D_SSD]
        bm = xb[rs, D_SSD:].astype(BF16)
        dt = dt_all[rs]
        da = da_all[rs]
        for d in range(2):
            tri = (ci <= ri) if d == 0 else (ci >= ri)
            cum = jnp.dot(tri.astype(F32), da, precision=HIGHEST, preferred_element_type=F32)
            last = cum[L - 1:L] if d == 0 else cum[0:1]
            w_e = _expand2(jnp.exp(last - cum) * dt, e64_ref[d])
            xw = (xs * w_e).astype(BF16)
            for g in range(GROUPS):
                gw = D_SSD // GROUPS
                st = lax.dot_general(bm[:, g * STATE:(g + 1) * STATE], xw[:, g * gw:(g + 1) * gw],
                                     (((0,), (0,)), ((), ())), preferred_element_type=F32)
                h0_ref[bi, d, :, g * gw:(g + 1) * gw] = st


def _ctx_states(ctx, mod3, g1, wxb, wdt, cw, cb, dtb, alog, e64):
    bsz = ctx.shape[0]
    mod_row = bsz
    nb = CTX_BATCH if bsz % CTX_BATCH == 0 else 1
    const = lambda *shape: pl.BlockSpec(shape, lambda b: (0,) * len(shape))
    return pl.pallas_call(
        _ctx_kernel,
        out_shape=jax.ShapeDtypeStruct((bsz, 2, STATE, D_SSD), F32),
        grid=(bsz // nb,),
        in_specs=[pl.BlockSpec((nb, CTX_LEN, D_MODEL), lambda b: (b, 0, 0)),
                  pl.BlockSpec((1, N_MOD, D_MODEL), lambda b: (mod_row, 0, 0)),
                  const(1, D_MODEL), const(D_MODEL, XB_W), const(D_MODEL, LANES),
                  const(3, XB_W), const(1, XB_W), const(1, LANES), const(1, LANES),
                  const(2, 2 * LANES, D_SSD)],
        out_specs=pl.BlockSpec((nb, 2, STATE, D_SSD), lambda b: (b, 0, 0, 0)),
        compiler_params=pltpu.CompilerParams(dimension_semantics=("arbitrary",),
                                             vmem_limit_bytes=VMEM_LIMIT),
        name="ctx_states",
    )(ctx, mod3, g1, wxb, wdt, cw, cb, dtb, alog, e64)


def _inproj_kernel(x_ref, mod_ref, g1_ref, wz_ref, wxbc_ref, wdt_ref, dtb_ref, alog_ref, wb_ref, wc_ref, wu_ref,
                   z_ref, xbc_ref, dt_ref, scb_ref, v_ref):
    m = mod_ref[0]
    hx = _rms(x_ref[...]) * g1_ref[...] * (1.0 + m[1:2]) + m[0:1]
    hb = hx.astype(BF16)
    z_ref[...] = _dot(hb, wz_ref[...]).astype(BF16)
    xbc_ref[...] = _dot(hb, wxbc_ref[...]).astype(BF16)
    dt = _softplus(_dot(hb, wdt_ref[...]) + dtb_ref[...])
    lane = lax.broadcasted_iota(jnp.int32, dt.shape, 1)
    dt_ref[...] = jnp.where(lane < DT_LOG0, dt,
                            jnp.where(lane < DT_DA0, jnp.log(dt), dt * (-jnp.exp(alog_ref[...]))))
    scb_ref[...] = _dot(hb, wb_ref[...]).astype(BF16)
    v_ref[...] = (_dot(hb, wc_ref[...]) * _dot(hb, wu_ref[...])).astype(BF16)


def _inproj(x2, mod3, g1, wz, wxbc, wdt, dtb, alog, wb, wc, wu):
    t = x2.shape[0]
    tm = TOK_TILE
    per_b = SEQ // tm
    const = lambda *shape: pl.BlockSpec(shape, lambda i: (0,) * len(shape), pipeline_mode=pl.Buffered(1))
    tile = lambda w: pl.BlockSpec((tm, w), lambda i: (i, 0))
    return pl.pallas_call(
        _inproj_kernel,
        out_shape=(jax.ShapeDtypeStruct((t, D_SSD), BF16), jax.ShapeDtypeStruct((t, XBC_W), BF16),
                   jax.ShapeDtypeStruct((t, LANES), F32), jax.ShapeDtypeStruct((t, D_SC), BF16),
                   jax.ShapeDtypeStruct((t, D_SC), BF16)),
        grid=(t // tm,),
        in_specs=[tile(D_MODEL),
                  pl.BlockSpec((1, N_MOD, D_MODEL), lambda i: (i // per_b, 0, 0)),
                  const(1, D_MODEL), const(D_MODEL, D_SSD), const(D_MODEL, XBC_W), const(D_MODEL, LANES),
                  const(1, LANES), const(1, LANES),
                  const(D_MODEL, D_SC), const(D_MODEL, D_SC), const(D_MODEL, D_SC)],
        out_specs=(tile(D_SSD), tile(XBC_W), tile(LANES), tile(D_SC), tile(D_SC)),
        compiler_params=pltpu.CompilerParams(dimension_semantics=("arbitrary",),
                                             vmem_limit_bytes=VMEM_LIMIT),
        name="inproj",
    )(x2, mod3, g1, wz, wxbc, wdt, dtb, alog, wb, wc, wu)


def _ssd_kernel(xbc_ref, z_ref, dt_ref, h0_ref, cw_ref, cb_ref, dsk_ref, g_ref,
                e64_ref, e128_ref, o_ref, xc_ref, y_ref, s_ref):
    Q = CHUNK
    nck = SEQ // Q
    gw = D_SSD // GROUPS

    rowi = lax.broadcasted_iota(jnp.int32, (SUBLANES, XBC_W), 0)

    def conv_body(c, carry):
        r0 = pl.multiple_of(c * Q, Q)
        main = xbc_ref[0, pl.ds(r0, Q), :].astype(F32)
        pstart = pl.multiple_of(jnp.maximum(r0 - 16, 0), 16)
        nstart = pl.multiple_of(jnp.minimum(r0 + Q, SEQ - 16), 16)
        prev = xbc_ref[0, pl.ds(pstart, 16), :].astype(F32)[15:16]
        nxt = xbc_ref[0, pl.ds(nstart, 16), :].astype(F32)[0:1]
        prev = jnp.where(c > 0, prev, 0.0)
        nxt = jnp.where(c < nck - 1, nxt, 0.0)
        dn = pltpu.roll(main, 1, 0)
        up = pltpu.roll(main, Q - 1, 0)
        dn = jnp.concatenate([jnp.where(rowi == 0, prev, dn[0:SUBLANES]), dn[SUBLANES:]], axis=0)
        up = jnp.concatenate([up[:Q - SUBLANES], jnp.where(rowi == SUBLANES - 1, nxt, up[Q - SUBLANES:])], axis=0)
        cw = cw_ref[...]
        conv = cw[0:1] * dn + cw[1:2] * main + cw[2:3] * up + cb_ref[...]
        xc_ref[pl.ds(r0, Q), :] = _silu(conv).astype(BF16)
        return carry

    lax.fori_loop(0, nck, conv_body, 0)

    ri = lax.broadcasted_iota(jnp.int32, (Q, Q), 0)
    ci = lax.broadcasted_iota(jnp.int32, (Q, Q), 1)
    lane = lax.broadcasted_iota(jnp.int32, (Q, LANES), 1)
    da_lanes = jnp.logical_and(lane >= DT_DA0, lane < DT_DA0 + 2 * HEADS)

    def chunk(c, d, first):
        r0 = pl.multiple_of(c * Q, Q)
        rows = pl.ds(r0, Q)
        xs_b = xc_ref[rows, 0:D_SSD]
        xs = xs_b.astype(F32)
        bm = xc_ref[rows, D_SSD:D_SSD + GROUPS * STATE]
        cm = xc_ref[rows, D_SSD + GROUPS * STATE:XBC_W]
        dtp = dt_ref[0, rows, :]
        da = jnp.where(da_lanes, dtp, 0.0)
        dt = pltpu.roll(dtp, DT_DA0, 1)
        log_dt = pltpu.roll(dtp, DT_DA0 - DT_LOG0, 1)
        tri = (ci <= ri) if d == 0 else (ci >= ri)
        p0 = da.astype(BF16)
        r1 = da - p0.astype(F32)
        p1 = r1.astype(BF16)
        p2 = (r1 - p1.astype(F32)).astype(BF16)
        tri_b = jnp.where(tri, 1.0, 0.0).astype(BF16)
        cum = _dot(jnp.concatenate([tri_b, tri_b, tri_b], axis=1),
                   jnp.concatenate([p0, p1, p2], axis=0))
        sub_t = (cum - log_dt).T
        last = cum[Q - 1:Q] if d == 0 else cum[0:1]
        ecum_e = _expand2(jnp.exp(cum), e64_ref[d])
        w_e = _expand2(jnp.where(da_lanes, jnp.exp(last - cum) * dt, 0.0), e64_ref[d])
        colb = _expand2(cum, e128_ref[d])
        decay_e = ecum_e[Q - 1:Q] if d == 0 else ecum_e[0:1]

        gmat = [lax.dot_general(cm[:, g * STATE:(g + 1) * STATE], bm[:, g * STATE:(g + 1) * STATE],
                                (((1,), (1,)), ((), ())), preferred_element_type=F32)
                for g in range(GROUPS)]
        zero_b = jnp.zeros((Q, LANES), BF16)
        y_parts = []
        for p in range(HEADS // 2):
            g = (2 * p) // (HEADS // GROUPS)
            ms = []
            for hh in (2 * p, 2 * p + 1):
                src = DT_DA0 + HEADS * d + hh
                seg = colb[:, hh * LANES:(hh + 1) * LANES] - sub_t[src:src + 1, :]
                ms.append((jnp.where(tri, jnp.exp(seg), 0.0) * gmat[g]).astype(BF16))
            mcat = jnp.concatenate(ms, axis=1)
            xp = xs_b[:, p * LANES:(p + 1) * LANES]
            rhs = jnp.concatenate([jnp.where(lane < HEAD_DIM, xp, zero_b),
                                   jnp.where(lane >= HEAD_DIM, xp, zero_b)], axis=0)
            y_parts.append(_dot(mcat, rhs))
        y_diag = jnp.concatenate(y_parts, axis=1)

        s_old = s_ref[d]
        s_bf = s_old.astype(BF16)
        y_off = jnp.concatenate(
            [_dot(cm[:, g * STATE:(g + 1) * STATE], s_bf[:, g * gw:(g + 1) * gw]) for g in range(GROUPS)],
            axis=1)
        y = y_diag + y_off * ecum_e

        xw = (xs * w_e).astype(BF16)
        upd = jnp.concatenate(
            [lax.dot_general(bm[:, g * STATE:(g + 1) * STATE], xw[:, g * gw:(g + 1) * gw],
                             (((0,), (0,)), ((), ())), preferred_element_type=F32) for g in range(GROUPS)],
            axis=1)
        s_ref[d] = s_old * decay_e + upd

        if first:
            y_ref[rows, :] = y + dsk_ref[...] * xs
        else:
            tot = y_ref[rows, :] + y
            zz = z_ref[0, rows, :].astype(F32)
            gz = tot * _silu(zz)
            outs = []
            for g in range(GROUPS):
                gg = gz[:, g * gw:(g + 1) * gw]
                outs.append(gg * lax.rsqrt(jnp.mean(gg * gg, axis=-1, keepdims=True) + NORM_EPS))
            o_ref[0, rows, :] = (jnp.concatenate(outs, axis=1) * g_ref[...]).astype(BF16)

    s_ref[...] = h0_ref[0]

    def first_half(i, carry):
        chunk(i, 0, True)
        chunk(nck - 1 - i, 1, True)
        return carry

    def second_half(i, carry):
        chunk(i, 0, False)
        chunk(nck - 1 - i, 1, False)
        return carry

    lax.fori_loop(0, nck // 2, first_half, 0, unroll=4)
    lax.fori_loop(nck // 2, nck, second_half, 0, unroll=4)


def _ssd(xbc3, z3, dt3, h0, cw, cb, dsk, g, e64, e128):
    bsz = xbc3.shape[0]
    const = lambda *shape: pl.BlockSpec(shape, lambda b: (0,) * len(shape))
    seq = lambda w: pl.BlockSpec((1, SEQ, w), lambda b: (b, 0, 0))
    return pl.pallas_call(
        _ssd_kernel,
        out_shape=jax.ShapeDtypeStruct((bsz, SEQ, D_SSD), BF16),
        grid=(bsz,),
        in_specs=[seq(XBC_W), seq(D_SSD), seq(LANES),
                  pl.BlockSpec((1, 2, STATE, D_SSD), lambda b: (b, 0, 0, 0)),
                  const(3, XBC_W), const(1, XBC_W),
                  const(1, D_SSD), const(1, D_SSD), const(2, 2 * LANES, D_SSD),
                  const(2, 2 * LANES, HEADS * LANES)],
        out_specs=seq(D_SSD),
        scratch_shapes=[pltpu.VMEM((SEQ, XBC_W), BF16), pltpu.VMEM((SEQ, D_SSD), F32),
                        pltpu.VMEM((2, STATE, D_SSD), F32)],
        compiler_params=pltpu.CompilerParams(dimension_semantics=("arbitrary",),
                                             vmem_limit_bytes=VMEM_LIMIT),
        name="ssd",
    )(xbc3, z3, dt3, h0, cw, cb, dsk, g, e64, e128)


def _outproj_kernel(x_ref, yssd_ref, scb_ref, v_ref, vp_ref, vn_ref, mod_ref, scw_ref, wo1_ref, wo2_ref,
                    g2_ref, wrt_ref, br_ref, x1_ref, lg_ref):
    tm = OUT_TILE
    per_b = SEQ // tm
    i = pl.program_id(0)
    first = (i % per_b) == 0
    last = (i % per_b) == per_b - 1
    m = mod_ref[0]
    v = v_ref[...].astype(F32)
    vp = jnp.where(first, 0.0, vp_ref[...].astype(F32))
    vn = jnp.where(last, 0.0, vn_ref[...].astype(F32))
    dn = jnp.concatenate([vp, v[:tm - GRID_W]], axis=0)
    up = jnp.concatenate([v[GRID_W:], vn], axis=0)
    scw = scw_ref[...]
    ysc = scb_ref[...].astype(F32) * (scw[0:1] * dn + scw[1:2] * v + scw[2:3] * up)
    out = _dot(yssd_ref[...], wo1_ref[...]) + _dot(ysc.astype(BF16), wo2_ref[...])
    x1 = x_ref[...] + m[2:3] * out
    x1_ref[...] = x1
    h2 = _rms(x1) * g2_ref[...] * (1.0 + m[4:5]) + m[3:4]
    h_hi = h2.astype(BF16)
    h_lo = (h2 - h_hi.astype(F32)).astype(BF16)
    lg_ref[...] = lax.dot_general(wrt_ref[...], jnp.concatenate([h_hi, h_lo, h_hi], axis=1),
                                  (((1,), (1,)), ((), ())), preferred_element_type=F32) + br_ref[...]


def _outproj(x2, yssd, scb, v, mod3, scw, wo1, wo2, g2, wrt, br):
    t = x2.shape[0]
    tm = OUT_TILE
    per_b = SEQ // tm
    r = tm // GRID_W
    nrow = t // GRID_W
    const = lambda *shape: pl.BlockSpec(shape, lambda i: (0,) * len(shape))
    tile = lambda w: pl.BlockSpec((tm, w), lambda i: (i, 0))
    return pl.pallas_call(
        _outproj_kernel,
        out_shape=(jax.ShapeDtypeStruct((t, D_MODEL), F32), jax.ShapeDtypeStruct((N_EXPERTS, t), F32)),
        grid=(t // tm,),
        in_specs=[tile(D_MODEL), tile(D_SSD), tile(D_SC), tile(D_SC),
                  pl.BlockSpec((GRID_W, D_SC), lambda i: (jnp.maximum(i * r - 1, 0), 0)),
                  pl.BlockSpec((GRID_W, D_SC), lambda i: (jnp.minimum((i + 1) * r, nrow - 1), 0)),
                  pl.BlockSpec((1, N_MOD, D_MODEL), lambda i: (i // per_b, 0, 0)),
                  const(3, D_SC), const(D_SSD, D_MODEL), const(D_SC, D_MODEL), const(1, D_MODEL),
                  const(N_EXPERTS, 3 * D_MODEL), const(N_EXPERTS, 1)],
        out_specs=(tile(D_MODEL), pl.BlockSpec((N_EXPERTS, tm), lambda i: (0, i))),
        compiler_params=pltpu.CompilerParams(dimension_semantics=("arbitrary",),
                                             vmem_limit_bytes=VMEM_LIMIT),
        name="outproj",
    )(x2, yssd, scb, v, v, v, mod3, scw, wo1, wo2, g2, wrt, br)


def _route_kernel(lg_ref, dest_ref, gate_ref, idx_ref, meta_ref, rank_ref, carry_ref, *, n_tok, n_blocks):
    tt = RT_TILE
    ne = N_EXPERTS
    eio = lax.broadcasted_iota(jnp.int32, (ne, tt), 0)
    si = lax.broadcasted_iota(jnp.int32, (tt, tt), 0)
    ti = lax.broadcasted_iota(jnp.int32, (tt, tt), 1)
    before = (si < ti).astype(BF16)
    carry_ref[...] = jnp.zeros_like(carry_ref)

    def tile_body(j, c):
        t0 = pl.multiple_of(j * tt, tt)
        l = lg_ref[:, pl.ds(t0, tt)]
        onehot = jnp.zeros((ne, tt), F32)
        tops, sels = [], []
        for _ in range(TOP_K):
            mx = jnp.max(l, axis=0, keepdims=True)
            idx = jnp.min(jnp.where(l == mx, eio, ne), axis=0, keepdims=True)
            sel = eio == idx
            l = jnp.where(sel, -jnp.inf, l)
            onehot = onehot + sel.astype(F32)
            tops.append(mx)
            sels.append(sel)
            idx_ref[pl.ds(len(tops) - 1, 1), pl.ds(t0, tt)] = idx
        ex = [jnp.exp(tv - tops[0]) for tv in tops]
        den = ex[0] + ex[1] + ex[2] + ex[3]
        prefix = _dot(onehot.astype(BF16), before) + carry_ref[:, 0:1]
        for k in range(TOP_K):
            gate_ref[pl.ds(k, 1), pl.ds(t0, tt)] = ex[k] / den
            rk = jnp.sum(jnp.where(sels[k], prefix, 0.0), axis=0, keepdims=True)
            rank_ref[pl.ds(k, 1), pl.ds(t0, tt)] = rk.astype(jnp.int32)
        carry_ref[...] = carry_ref[...] + jnp.sum(onehot, axis=1, keepdims=True)
        return c

    lax.fori_loop(0, n_tok // tt, tile_body, 0)

    counts = carry_ref[...]
    padded = jnp.floor((counts + (MOE_BM - 1)) * (1.0 / MOE_BM)) * MOE_BM
    er = lax.broadcasted_iota(jnp.int32, (ne, ne), 0)
    ec = lax.broadcasted_iota(jnp.int32, (ne, ne), 1)
    pad_start = jnp.dot((ec < er).astype(F32), padded, precision=HIGHEST, preferred_element_type=F32)
    pad_end = pad_start + padded

    def dest_body(j, c):
        t0 = pl.multiple_of(j * tt, tt)
        for k in range(TOP_K):
            idx = idx_ref[pl.ds(k, 1), pl.ds(t0, tt)]
            base = jnp.sum(jnp.where(eio == idx, pad_start[:, 0:1], 0.0), axis=0, keepdims=True)
            dest_ref[pl.ds(k, 1), pl.ds(t0, tt)] = base.astype(jnp.int32) + rank_ref[pl.ds(k, 1), pl.ds(t0, tt)]
        return c

    lax.fori_loop(0, n_tok // tt, dest_body, 0)

    width = meta_ref.shape[1]
    sub = lax.broadcasted_iota(jnp.int32, (ne, width), 0)
    lan = lax.broadcasted_iota(jnp.int32, (ne, width), 1)
    diag = sub == lan
    cnt_row = jnp.sum(jnp.where(diag, counts[:, 0:1], 0.0), axis=0, keepdims=True)
    start_row = jnp.sum(jnp.where(diag, pad_start[:, 0:1], 0.0), axis=0, keepdims=True)
    blk_start = (lan * MOE_BM).astype(F32)
    blk_exp = jnp.sum((pad_end[:, 0:1] <= blk_start).astype(F32), axis=0, keepdims=True)
    blk_exp = jnp.minimum(blk_exp, float(ne - 1))
    used = jnp.sum(padded[:, 0:1], axis=0, keepdims=True) * (1.0 / MOE_BM)
    meta_ref[0:1, :] = cnt_row.astype(jnp.int32)
    meta_ref[1:2, :] = start_row.astype(jnp.int32)
    meta_ref[2:3, :] = blk_exp.astype(jnp.int32)
    meta_ref[3:4, :] = jnp.broadcast_to(used, (1, width)).astype(jnp.int32)
    meta_ref[4:8, :] = jnp.zeros((4, width), jnp.int32)


def _route(lgt, n_blocks):
    ne, n_tok = lgt.shape
    width = -(-n_blocks // LANES) * LANES
    full = lambda *shape: pl.BlockSpec(shape, lambda: (0,) * len(shape))
    return pl.pallas_call(
        functools.partial(_route_kernel, n_tok=n_tok, n_blocks=n_blocks),
        out_shape=(jax.ShapeDtypeStruct((TOP_K, n_tok), jnp.int32),
                   jax.ShapeDtypeStruct((TOP_K, n_tok), F32),
                   jax.ShapeDtypeStruct((TOP_K, n_tok), jnp.int32),
                   jax.ShapeDtypeStruct((8, width), jnp.int32)),
        in_specs=[full(ne, n_tok)],
        out_specs=(full(TOP_K, n_tok), full(TOP_K, n_tok), full(TOP_K, n_tok), full(8, width)),
        scratch_shapes=[pltpu.VMEM((TOP_K, n_tok), jnp.int32), pltpu.VMEM((ne, LANES), F32)],
        compiler_params=pltpu.CompilerParams(vmem_limit_bytes=VMEM_LIMIT),
        name="route",
    )(lgt)


def _dispatch_kernel(dest_ref, cnt_ref, start_ref, nu_ref, x1_ref, meta_ref, mod_ref, g2_ref, zsrc_ref, xs_ref,
                     hbuf, sem, zsem):
    i = pl.program_id(0)
    n = pl.num_programs(0)
    tl = DISP_TILE
    slot = i % 2
    nb = xs_ref.shape[0] // MOE_BM

    def zero_block(b):
        return pltpu.make_async_copy(zsrc_ref, xs_ref.at[pl.ds(b * MOE_BM, MOE_BM)], zsem)

    @pl.when(i == 0)
    def _():
        def start_e(e, c):
            @pl.when(cnt_ref[e] > 0)
            def _():
                zero_block((start_ref[e] + cnt_ref[e] - 1) // MOE_BM).start()
            return c

        def wait_e(e, c):
            @pl.when(cnt_ref[e] > 0)
            def _():
                zero_block(0).wait()
            return c

        def start_t(b, c):
            zero_block(b).start()
            return c

        def wait_t(b, c):
            zero_block(0).wait()
            return c

        lax.fori_loop(0, N_EXPERTS, start_e, 0)
        lax.fori_loop(nu_ref[0], nb, start_t, 0)
        lax.fori_loop(0, N_EXPERTS, wait_e, 0)
        lax.fori_loop(nu_ref[0], nb, wait_t, 0)

    m = mod_ref[0]
    h2 = _rms(x1_ref[...]) * g2_ref[...] * (1.0 + m[4:5]) + m[3:4]
    lo = pltpu.bitcast(h2[:, :PACK_W].astype(BF16).astype(F32), jnp.uint32) >> 16
    hi = pltpu.bitcast(h2[:, PACK_W:].astype(BF16).astype(F32), jnp.uint32) & jnp.uint32(0xFFFF0000)
    row = jnp.concatenate([lo | hi, meta_ref[...], jnp.zeros((tl, D_MODEL - PACK_W - LANES), jnp.uint32)], axis=1)
    hbuf[slot] = row.reshape(tl, SUBLANES, LANES)

    def per_tok(t, c):
        tok = i * tl + t
        for k in range(TOP_K):
            d = dest_ref[k * (dest_ref.shape[0] // TOP_K) + tok]
            pltpu.make_async_copy(hbuf.at[slot, t], xs_ref.at[d], sem.at[slot]).start(priority=k % 2)
        return c

    lax.fori_loop(0, tl, per_tok, 0, unroll=8)

    def wait_slot(sl):
        for _ in range(TOP_K):
            pltpu.make_async_copy(hbuf.at[sl], xs_ref.at[pl.ds(0, tl)], sem.at[sl]).wait()

    @pl.when(i > 0)
    def _():
        wait_slot(1 - slot)

    @pl.when(i == n - 1)
    def _():
        wait_slot(slot)


def _dispatch(dest_flat, cnt, start, n_used, x1, meta_rows, mod3, g2, zsrc, n_rows):
    n_tok = x1.shape[0]
    tl = DISP_TILE
    per_b = SEQ // tl
    return pl.pallas_call(
        _dispatch_kernel,
        out_shape=jax.ShapeDtypeStruct((n_rows, SUBLANES, LANES), jnp.uint32),
        grid_spec=pltpu.PrefetchScalarGridSpec(
            num_scalar_prefetch=4,
            grid=(n_tok // tl,),
            in_specs=[pl.BlockSpec((tl, D_MODEL), lambda i, *_: (i, 0)),
                      pl.BlockSpec((tl, LANES), lambda i, *_: (i, 0)),
                      pl.BlockSpec((1, N_MOD, D_MODEL), lambda i, *_: (i // per_b, 0, 0)),
                      pl.BlockSpec((1, D_MODEL), lambda i, *_: (0, 0)),
                      pl.BlockSpec((MOE_BM, SUBLANES, LANES), lambda i, *_: (0, 0, 0))],
            out_specs=pl.BlockSpec(memory_space=pl.ANY),
            scratch_shapes=[pltpu.VMEM((2, tl, SUBLANES, LANES), jnp.uint32),
                            pltpu.SemaphoreType.DMA((2,)), pltpu.SemaphoreType.DMA]),
        compiler_params=pltpu.CompilerParams(dimension_semantics=("arbitrary",),
                                             vmem_limit_bytes=VMEM_LIMIT),
        name="dispatch",
    )(dest_flat, cnt, start, n_used, x1, meta_rows, mod3, g2, zsrc)


def _expert_rows_kernel(be_ref, nu_ref, xs_ref, wgu_hbm, bgu_ref, wd_hbm, bd_ref, ys_ref,
                        wgu_raw, wd_raw, wgu_bf, wd_bf, w_sem):
    j = pl.program_id(0)
    nu = nu_ref[0]
    n_lt = D_MODEL // LANES

    def weight_copies(e):
        return (pltpu.make_async_copy(wgu_hbm.at[e], wgu_raw, w_sem.at[0]),
                pltpu.make_async_copy(wd_hbm.at[e], wd_raw, w_sem.at[1]))

    @pl.when(j == 0)
    def _():
        for w in weight_copies(be_ref[0]):
            w.start()

    e_now = be_ref[j]
    new_expert = jnp.logical_or(j == 0, be_ref[jnp.maximum(j - 1, 0)] != e_now)

    @pl.when(jnp.logical_and(j < nu, new_expert))
    def _():
        for w in weight_copies(e_now):
            w.wait()
        n_cc = 8
        for c in range(n_cc):
            cc = slice(c * (2 * D_FF // n_cc), (c + 1) * (2 * D_FF // n_cc))
            wgu_bf[:, cc] = wgu_raw[:, cc].astype(BF16)
        for c in range(n_cc // 2):
            cc = slice(c * (2 * D_MODEL // n_cc), (c + 1) * (2 * D_MODEL // n_cc))
            wd_bf[:, cc] = wd_raw[:, cc].astype(BF16)
        j_next = lax.while_loop(lambda t: jnp.logical_and(t < nu, be_ref[jnp.minimum(t, pl.num_programs(0) - 1)] == e_now),
                                lambda t: t + 1, j + 1)

        @pl.when(j_next < nu)
        def _():
            for w in weight_copies(be_ref[jnp.minimum(j_next, pl.num_programs(0) - 1)]):
                w.start()

    @pl.when(j < nu)
    def _():
        e_f = e_now.astype(F32)
        pr = MOE_BM // MOE_PARTS
        for p in range(MOE_PARTS):
            words = xs_ref[pl.ds(p * pr, pr)].reshape(pr, D_MODEL)
            packed = words[:, 0:PACK_W]
            meta = pltpu.bitcast(words[:, PACK_W:PACK_W + LANES], F32)
            xb = jnp.concatenate(
                [pltpu.bitcast(packed << 16, F32).astype(BF16),
                 pltpu.bitcast(packed & jnp.uint32(0xFFFF0000), F32).astype(BF16)], axis=1)
            gate = jnp.zeros((pr, 1), F32)
            for k in range(TOP_K):
                mk = meta[:, META_IDX + k:META_IDX + k + 1] == e_f
                gate = gate + jnp.where(mk, meta[:, META_GATE + k:META_GATE + k + 1], 0.0)
            gu = _dot(xb, wgu_bf[...]) + bgu_ref[0]
            glu = jnp.minimum(gu[:, :D_FF], SWIGLU_LIMIT)
            lin = jnp.clip(gu[:, D_FF:], -SWIGLU_LIMIT, SWIGLU_LIMIT)
            act = glu * jax.nn.sigmoid(SWIGLU_ALPHA * glu) * (lin + 1.0)
            y = (_dot(act.astype(BF16), wd_bf[...]) + bd_ref[0]) * gate
            ys_ref[pl.ds(p * pr, pr)] = y.reshape(pr, n_lt, LANES)

    @pl.when(j >= nu)
    def _():
        ys_ref[...] = jnp.zeros_like(ys_ref)


def _expert_rows(blk_exp, n_used, xs, wgu, bgu, wd, bd):
    n_rows = xs.shape[0]
    nb = n_rows // MOE_BM
    n_lt = D_MODEL // LANES
    row_blk = lambda j, be, nu: (jnp.minimum(j, nu[0] - 1), 0, 0)
    per_e = lambda j, be, nu: (be[j], 0, 0)
    return pl.pallas_call(
        _expert_rows_kernel,
        out_shape=jax.ShapeDtypeStruct((n_rows, n_lt, LANES), F32),
        grid_spec=pltpu.PrefetchScalarGridSpec(
            num_scalar_prefetch=2,
            grid=(nb,),
            in_specs=[pl.BlockSpec((MOE_BM, SUBLANES, LANES), row_blk),
                      pl.BlockSpec(memory_space=pl.ANY),
                      pl.BlockSpec((1, 1, 2 * D_FF), per_e),
                      pl.BlockSpec(memory_space=pl.ANY),
                      pl.BlockSpec((1, 1, D_MODEL), per_e)],
            out_specs=pl.BlockSpec((MOE_BM, n_lt, LANES), lambda j, be, nu: (j, 0, 0)),
            scratch_shapes=[pltpu.VMEM((D_MODEL, 2 * D_FF), F32), pltpu.VMEM((D_FF, D_MODEL), F32),
                            pltpu.VMEM((D_MODEL, 2 * D_FF), BF16), pltpu.VMEM((D_FF, D_MODEL), BF16),
                            pltpu.SemaphoreType.DMA((2,))]),
        compiler_params=pltpu.CompilerParams(dimension_semantics=("arbitrary",),
                                             vmem_limit_bytes=VMEM_LIMIT),
        name="experts",
    )(blk_exp, n_used, xs, wgu, bgu, wd, bd)


def _gather_combine_kernel(dest_ref, ys_ref, x1_ref, mod_ref, fg_ref, o_ref, buf, sem):
    i = pl.program_id(0)
    n = pl.num_programs(0)
    tc = GATHER_TILE
    slot = i % 2

    def issue(tile, sl):
        def per_tok(t, c):
            tok = tile * tc + t
            for k in range(TOP_K):
                d = dest_ref[k * (dest_ref.shape[0] // TOP_K) + tok]
                pltpu.make_async_copy(ys_ref.at[d], buf.at[sl, k * tc + t], sem.at[sl]).start(priority=k % 2)
            return c

        lax.fori_loop(0, tc, per_tok, 0, unroll=8)

    @pl.when(i == 0)
    def _():
        issue(0, 0)

    @pl.when(i + 1 < n)
    def _():
        issue(i + 1, 1 - slot)

    for _ in range(TOP_K):
        pltpu.make_async_copy(ys_ref.at[pl.ds(0, tc)], buf.at[slot, pl.ds(0, tc)], sem.at[slot]).wait()

    moe = ((buf[slot, pl.ds(0, tc)] + buf[slot, pl.ds(tc, tc)])
           + (buf[slot, pl.ds(2 * tc, tc)] + buf[slot, pl.ds(3 * tc, tc)])).reshape(tc, D_MODEL)
    m = mod_ref[0]
    x2 = x1_ref[...] + m[5:6] * moe
    o_ref[...] = _rms(x2) * fg_ref[...]


def _gather_combine(dest_flat, ys, x1, mod3, fg):
    n_tok = x1.shape[0]
    tc = GATHER_TILE
    per_b = SEQ // tc
    n_lt = D_MODEL // LANES
    return pl.pallas_call(
        _gather_combine_kernel,
        out_shape=jax.ShapeDtypeStruct((n_tok, D_MODEL), F32),
        grid_spec=pltpu.PrefetchScalarGridSpec(
            num_scalar_prefetch=1,
            grid=(n_tok // tc,),
            in_specs=[pl.BlockSpec(memory_space=pl.ANY),
                      pl.BlockSpec((tc, D_MODEL), lambda i, d: (i, 0)),
                      pl.BlockSpec((1, N_MOD, D_MODEL), lambda i, d: (i // per_b, 0, 0)),
                      pl.BlockSpec((1, D_MODEL), lambda i, d: (0, 0))],
            out_specs=pl.BlockSpec((tc, D_MODEL), lambda i, d: (i, 0)),
            scratch_shapes=[pltpu.VMEM((2, TOP_K * tc, n_lt, LANES), F32), pltpu.SemaphoreType.DMA((2,))]),
        compiler_params=pltpu.CompilerParams(dimension_semantics=("arbitrary",),
                                             vmem_limit_bytes=VMEM_LIMIT),
        name="combine",
    )(dest_flat, ys, x1, mod3, fg)


def _expansion_matrices(src0):
    r = (jnp.arange(2 * LANES) % LANES)[:, None]
    out64, out128 = [], []
    for d in range(2):
        l64 = jnp.arange(D_SSD)[None, :]
        l128 = jnp.arange(HEADS * LANES)[None, :]
        out64.append((l64 // HEAD_DIM == r - src0 - HEADS * d).astype(BF16))
        out128.append((l128 // LANES == r - src0 - HEADS * d).astype(BF16))
    return jnp.stack(out64), jnp.stack(out128)


def _pad_lanes(v):
    return jnp.pad(v, [(0, 0)] * (v.ndim - 1) + [(0, LANES - v.shape[-1])])


def kernel(x, c, ctx, c_ctx, w_mod, b_mod, norm1_g, w_in, ssd_conv_w, ssd_conv_b, ssd_dt_bias, ssd_a_log,
           ssd_d, ssd_norm_g, sc_conv_w, w_out, norm2_g, w_router, b_router, w_gate_up, b_gate_up, w_down,
           b_down, final_g):
    bsz = x.shape[0]
    n_tok = bsz * SEQ
    n_assign = n_tok * TOP_K
    n_blocks = n_assign // MOE_BM + N_EXPERTS
    n_rows = n_blocks * MOE_BM
    li = 0

    cvec = jnp.concatenate([c, c_ctx[None, :], jnp.zeros((7, D_MODEL), F32)], axis=0)
    mod3 = _mod(cvec, w_mod, b_mod[li][None, :], li).reshape(bsz + 8, N_MOD, D_MODEL)

    w = w_in[li]
    wz = w[:, Z0:X0].astype(BF16)
    wxbc = w[:, X0:DT0].astype(BF16)
    wdt = _pad_lanes(w[:, DT0:SC0]).astype(BF16)
    wb = w[:, SC0:SC0 + D_SC].astype(BF16)
    wc = w[:, SC0 + D_SC:SC0 + 2 * D_SC].astype(BF16)
    wu = w[:, SC0 + 2 * D_SC:].astype(BF16)
    g1 = norm1_g[li][None, :]
    cw = ssd_conv_w[li]
    cb = ssd_conv_b[li][None, :]
    dtb = _pad_lanes(ssd_dt_bias[li].reshape(1, 2 * HEADS))
    alog = _pad_lanes(ssd_a_log[li].reshape(1, 2 * HEADS))
    e64_ctx, _ = _expansion_matrices(0)

    h0 = _ctx_states(ctx, mod3, g1, wxbc[:, :XB_W], wdt, cw[:, :XB_W], cb[:, :XB_W], dtb, alog, e64_ctx)

    rep = lambda a: _pad_lanes(jnp.tile(a[..., :2 * HEADS], (1, DT_COPIES)))
    e64, e128 = _expansion_matrices(DT_DA0)
    x2 = x.reshape(n_tok, D_MODEL)
    z, xbc, dtp, scb, v = _inproj(x2, mod3, g1, wz, wxbc, rep(wdt), rep(dtb), rep(alog), wb, wc, wu)

    dsk = jnp.repeat(ssd_d[li], HEAD_DIM)[None, :]
    yssd = _ssd(xbc.reshape(bsz, SEQ, XBC_W), z.reshape(bsz, SEQ, D_SSD), dtp.reshape(bsz, SEQ, LANES), h0,
                cw, cb, dsk, ssd_norm_g[li][None, :], e64, e128)

    wo = w_out[li].astype(BF16)
    g2 = norm2_g[li][None, :]
    wr = w_router[li].T
    wr_hi = wr.astype(BF16)
    wr_lo = (wr - wr_hi.astype(F32)).astype(BF16)
    x1, lgt = _outproj(x2, yssd.reshape(n_tok, D_SSD), scb, v, mod3, sc_conv_w[li], wo[:D_SSD], wo[D_SSD:],
                       g2, jnp.concatenate([wr_hi, wr_hi, wr_lo], axis=1), b_router[li][:, None])

    dest_t, gate_t, idx_t, meta = _route(lgt, n_blocks)
    dest_flat = dest_t.reshape(n_assign)
    cnt = meta[0, :N_EXPERTS]
    start = meta[1, :N_EXPERTS]
    blk_exp = meta[2, :n_blocks]
    n_used = meta[3, :1]

    meta_rows = lax.bitcast_convert_type(_pad_lanes(jnp.concatenate(
        [idx_t.T.astype(F32), gate_t.T], axis=1)), jnp.uint32)
    pad_meta = lax.bitcast_convert_type(_pad_lanes(jnp.concatenate(
        [jnp.full((MOE_BM, TOP_K), -1.0, F32), jnp.zeros((MOE_BM, TOP_K), F32)], axis=1)), jnp.uint32)
    zsrc = jnp.concatenate([jnp.zeros((MOE_BM, PACK_W), jnp.uint32), pad_meta,
                            jnp.zeros((MOE_BM, D_MODEL - PACK_W - LANES), jnp.uint32)],
                           axis=1).reshape(MOE_BM, SUBLANES, LANES)

    xs = _dispatch(dest_flat, cnt, start, n_used, x1, meta_rows, mod3, g2, zsrc, n_rows)
    ys = _expert_rows(blk_exp, n_used, xs, w_gate_up[li], b_gate_up[li][:, None, :],
                      w_down[li], b_down[li][:, None, :])
    out = _gather_combine(dest_flat, ys, x1, mod3, final_g[None, :])
    return out.reshape(bsz, SEQ, D_MODEL)
```

```python
import functools

import jax
import jax.numpy as jnp
from jax import lax
from jax.experimental import pallas as pl
from jax.experimental.pallas import tpu as pltpu

F32 = jnp.float32
BF16 = jnp.bfloat16
HIGHEST = lax.Precision.HIGHEST

D_MODEL = 1024
SEQ = 2048
CTX_LEN = 256
GRID_W = 64
D_SSD = 1024
D_SC = 1024
HEAD_DIM = 64
HEADS = 16
GROUPS = 2
STATE = 128
CHUNK = 128
N_EXPERTS = 32
TOP_K = 4
D_FF = 1024
SWIGLU_LIMIT = 7.0
SWIGLU_ALPHA = 1.702
NORM_EPS = 1e-6
N_MOD = 6
XBC_W = D_SSD + 2 * GROUPS * STATE
XB_W = D_SSD + GROUPS * STATE
LANES = 128

Z0 = 0
X0 = Z0 + D_SSD
B0 = X0 + D_SSD
C0 = B0 + GROUPS * STATE
DT0 = C0 + GROUPS * STATE
SC0 = DT0 + 2 * HEADS

TOK_TILE = 1024
OUT_TILE = 1024
MOE_BM = 512
MOE_PARTS = 2
ZERO_PIECE = 128
RT_TILE = 512
CTX_BATCH = 4
DISP_TILE = 512
GATHER_TILE = 256
DT_COPIES = 3
DT_LOG0 = 2 * HEADS
DT_DA0 = 4 * HEADS
SUBLANES = 8
PACK_W = D_MODEL // 2
META_IDX = 0
META_GATE = TOP_K
VMEM_LIMIT = 56 * 1024 * 1024


def _silu(v):
    return v * jax.nn.sigmoid(v)


def _softplus(v):
    return jnp.maximum(v, 0.0) + jnp.log1p(jnp.exp(-jnp.abs(v)))


def _rms(v):
    return v * lax.rsqrt(jnp.mean(v * v, axis=-1, keepdims=True) + NORM_EPS)


def _dot(a, b):
    return jnp.dot(a, b, preferred_element_type=F32)


def _expand2(v, e2):
    hi = v.astype(BF16)
    lo = (v - hi.astype(F32)).astype(BF16)
    return _dot(jnp.concatenate([hi, lo], axis=1), e2)


def _mod_kernel(c_ref, w_ref, b_ref, o_ref):
    o_ref[...] = jnp.dot(_silu(c_ref[...]), w_ref[0], precision=HIGHEST,
                         preferred_element_type=F32) + b_ref[...]


def _mod(cvec, w_mod, b_mod, layer):
    rows = cvec.shape[0]
    n = w_mod.shape[2]
    tn = 1536
    return pl.pallas_call(
        _mod_kernel,
        out_shape=jax.ShapeDtypeStruct((rows, n), F32),
        grid=(n // tn,),
        in_specs=[pl.BlockSpec((rows, D_MODEL), lambda j: (0, 0)),
                  pl.BlockSpec((1, D_MODEL, tn), lambda j: (layer, 0, j)),
                  pl.BlockSpec((1, tn), lambda j: (0, j))],
        out_specs=pl.BlockSpec((rows, tn), lambda j: (0, j)),
        compiler_params=pltpu.CompilerParams(dimension_semantics=("arbitrary",),
                                             vmem_limit_bytes=VMEM_LIMIT),
        name="mod",
    )(cvec, w_mod, b_mod)


def _ctx_kernel(ctx_ref, mod_ref, g1_ref, wxb_ref, wdt_ref, cw_ref, cb_ref, dtb_ref, alog_ref, e64_ref,
                h0_ref):
    L = CTX_LEN
    nb = ctx_ref.shape[0]
    m = mod_ref[0]
    hc = _rms(ctx_ref[...].reshape(nb * L, D_MODEL)) * g1_ref[...] * (1.0 + m[1:2]) + m[0:1]
    hb = hc.astype(BF16)
    pxb = _dot(hb, wxb_ref[...])
    dtr = _dot(hb, wdt_ref[...])
    rowl = lax.broadcasted_iota(jnp.int32, (nb * L, XB_W), 0) & (L - 1)
    dn = jnp.where(rowl == 0, 0.0, pltpu.roll(pxb, 1, 0))
    up = jnp.where(rowl == L - 1, 0.0, pltpu.roll(pxb, nb * L - 1, 0))
    cw = cw_ref[...]
    xb = _silu(cw[0:1] * dn + cw[1:2] * pxb + cw[2:3] * up + cb_ref[...])
    dt_all = _softplus(dtr + dtb_ref[...])
    da_all = dt_all * (-jnp.exp(alog_ref[...]))
    ri = lax.broadcasted_iota(jnp.int32, (L, L), 0)
    ci = lax.broadcasted_iota(jnp.int32, (L, L), 1)
    for bi in range(nb):
        rs = slice(bi * L, (bi + 1) * L)
        xs = xb[rs, :D_SSD]
        bm = xb[rs, D_SSD:].astype(BF16)
        dt = dt_all[rs]
        da = da_all[rs]
        for d in range(2):
            tri = (ci <= ri) if d == 0 else (ci >= ri)
            cum = jnp.dot(tri.astype(F32), da, precision=HIGHEST, preferred_element_type=F32)
            last = cum[L - 1:L] if d == 0 else cum[0:1]
            w_e = _expand2(jnp.exp(last - cum) * dt, e64_ref[d])
            xw = (xs * w_e).astype(BF16)
            for g in range(GROUPS):
                gw = D_SSD // GROUPS
                st = lax.dot_general(bm[:, g * STATE:(g + 1) * STATE], xw[:, g * gw:(g + 1) * gw],
                                     (((0,), (0,)), ((), ())), preferred_element_type=F32)
                h0_ref[bi, d, :, g * gw:(g + 1) * gw] = st


def _ctx_states(ctx, mod3, g1, wxb, wdt, cw, cb, dtb, alog, e64):
    bsz = ctx.shape[0]
    mod_row = bsz
    nb = CTX_BATCH if bsz % CTX_BATCH == 0 else 1
    const = lambda *shape: pl.BlockSpec(shape, lambda b: (0,) * len(shape))
    return pl.pallas_call(
        _ctx_kernel,
        out_shape=jax.ShapeDtypeStruct((bsz, 2, STATE, D_SSD), F32),
        grid=(bsz // nb,),
        in_specs=[pl.BlockSpec((nb, CTX_LEN, D_MODEL), lambda b: (b, 0, 0)),
                  pl.BlockSpec((1, N_MOD, D_MODEL), lambda b: (mod_row, 0, 0)),
                  const(1, D_MODEL), const(D_MODEL, XB_W), const(D_MODEL, LANES),
                  const(3, XB_W), const(1, XB_W), const(1, LANES), const(1, LANES),
                  const(2, 2 * LANES, D_SSD)],
        out_specs=pl.BlockSpec((nb, 2, STATE, D_SSD), lambda b: (b, 0, 0, 0)),
        compiler_params=pltpu.CompilerParams(dimension_semantics=("arbitrary",),
                                             vmem_limit_bytes=VMEM_LIMIT),
        name="ctx_states",
    )(ctx, mod3, g1, wxb, wdt, cw, cb, dtb, alog, e64)


def _inproj_kernel(x_ref, mod_ref, g1_ref, wz_ref, wxbc_ref, wdt_ref, dtb_ref, alog_ref, wb_ref, wc_ref, wu_ref,
                   z_ref, xbc_ref, dt_ref, scb_ref, v_ref):
    m = mod_ref[0]
    hx = _rms(x_ref[...]) * g1_ref[...] * (1.0 + m[1:2]) + m[0:1]
    hb = hx.astype(BF16)
    z_ref[...] = _dot(hb, wz_ref[...]).astype(BF16)
    xbc_ref[...] = _dot(hb, wxbc_ref[...]).astype(BF16)
    dt = _softplus(_dot(hb, wdt_ref[...]) + dtb_ref[...])
    lane = lax.broadcasted_iota(jnp.int32, dt.shape, 1)
    dt_ref[...] = jnp.where(lane < DT_LOG0, dt,
                            jnp.where(lane < DT_DA0, jnp.log(dt), dt * (-jnp.exp(alog_ref[...]))))
    scb_ref[...] = _dot(hb, wb_ref[...]).astype(BF16)
    v_ref[...] = (_dot(hb, wc_ref[...]) * _dot(hb, wu_ref[...])).astype(BF16)


def _inproj(x2, mod3, g1, wz, wxbc, wdt, dtb, alog, wb, wc, wu):
    t = x2.shape[0]
    tm = TOK_TILE
    per_b = SEQ // tm
    const = lambda *shape: pl.BlockSpec(shape, lambda i: (0,) * len(shape), pipeline_mode=pl.Buffered(1))
    tile = lambda w: pl.BlockSpec((tm, w), lambda i: (i, 0))
    return pl.pallas_call(
        _inproj_kernel,
        out_shape=(jax.ShapeDtypeStruct((t, D_SSD), BF16), jax.ShapeDtypeStruct((t, XBC_W), BF16),
                   jax.ShapeDtypeStruct((t, LANES), F32), jax.ShapeDtypeStruct((t, D_SC), BF16),
                   jax.ShapeDtypeStruct((t, D_SC), BF16)),
        grid=(t // tm,),
        in_specs=[tile(D_MODEL),
                  pl.BlockSpec((1, N_MOD, D_MODEL), lambda i: (i // per_b, 0, 0)),
                  const(1, D_MODEL), const(D_MODEL, D_SSD), const(D_MODEL, XBC_W), const(D_MODEL, LANES),
                  const(1, LANES), const(1, LANES),
                  const(D_MODEL, D_SC), const(D_MODEL, D_SC), const(D_MODEL, D_SC)],
        out_specs=(tile(D_SSD), tile(XBC_W), tile(LANES), tile(D_SC), tile(D_SC)),
        compiler_params=pltpu.CompilerParams(dimension_semantics=("arbitrary",),
                                             vmem_limit_bytes=VMEM_LIMIT),
        name="inproj",
    )(x2, mod3, g1, wz, wxbc, wdt, dtb, alog, wb, wc, wu)


def _ssd_kernel(xbc_ref, z_ref, dt_ref, h0_ref, cw_ref, cb_ref, dsk_ref, g_ref,
                e64_ref, e128_ref, o_ref, xc_ref, y_ref, s_ref):
    Q = CHUNK
    nck = SEQ // Q
    gw = D_SSD // GROUPS

    rowi = lax.broadcasted_iota(jnp.int32, (SUBLANES, XBC_W), 0)

    def conv_body(c, carry):
        r0 = pl.multiple_of(c * Q, Q)
        main = xbc_ref[0, pl.ds(r0, Q), :].astype(F32)
        pstart = pl.multiple_of(jnp.maximum(r0 - 16, 0), 16)
        nstart = pl.multiple_of(jnp.minimum(r0 + Q, SEQ - 16), 16)
        prev = xbc_ref[0, pl.ds(pstart, 16), :].astype(F32)[15:16]
        nxt = xbc_ref[0, pl.ds(nstart, 16), :].astype(F32)[0:1]
        prev = jnp.where(c > 0, prev, 0.0)
        nxt = jnp.where(c < nck - 1, nxt, 0.0)
        dn = pltpu.roll(main, 1, 0)
        up = pltpu.roll(main, Q - 1, 0)
        dn = jnp.concatenate([jnp.where(rowi == 0, prev, dn[0:SUBLANES]), dn[SUBLANES:]], axis=0)
        up = jnp.concatenate([up[:Q - SUBLANES], jnp.where(rowi == SUBLANES - 1, nxt, up[Q - SUBLANES:])], axis=0)
        cw = cw_ref[...]
        conv = cw[0:1] * dn + cw[1:2] * main + cw[2:3] * up + cb_ref[...]
        xc_ref[pl.ds(r0, Q), :] = _silu(conv).astype(BF16)
        return carry

    lax.fori_loop(0, nck, conv_body, 0)

    ri = lax.broadcasted_iota(jnp.int32, (Q, Q), 0)
    ci = lax.broadcasted_iota(jnp.int32, (Q, Q), 1)
    lane = lax.broadcasted_iota(jnp.int32, (Q, LANES), 1)
    da_lanes = jnp.logical_and(lane >= DT_DA0, lane < DT_DA0 + 2 * HEADS)

    def chunk(c, d, first):
        r0 = pl.multiple_of(c * Q, Q)
        rows = pl.ds(r0, Q)
        xs_b = xc_ref[rows, 0:D_SSD]
        xs = xs_b.astype(F32)
        bm = xc_ref[rows, D_SSD:D_SSD + GROUPS * STATE]
        cm = xc_ref[rows, D_SSD + GROUPS * STATE:XBC_W]
        dtp = dt_ref[0, rows, :]
        da = jnp.where(da_lanes, dtp, 0.0)
        dt = pltpu.roll(dtp, DT_DA0, 1)
        log_dt = pltpu.roll(dtp, DT_DA0 - DT_LOG0, 1)
        tri = (ci <= ri) if d == 0 else (ci >= ri)
        p0 = da.astype(BF16)
        r1 = da - p0.astype(F32)
        p1 = r1.astype(BF16)
        p2 = (r1 - p1.astype(F32)).astype(BF16)
        tri_b = jnp.where(tri, 1.0, 0.0).astype(BF16)
        cum = _dot(jnp.concatenate([tri_b, tri_b, tri_b], axis=1),
                   jnp.concatenate([p0, p1, p2], axis=0))
        sub_t = (cum - log_dt).T
        last = cum[Q - 1:Q] if d == 0 else cum[0:1]
        ecum_e = _expand2(jnp.exp(cum), e64_ref[d])
        w_e = _expand2(jnp.where(da_lanes, jnp.exp(last - cum) * dt, 0.0), e64_ref[d])
        colb = _expand2(cum, e128_ref[d])
        decay_e = ecum_e[Q - 1:Q] if d == 0 else ecum_e[0:1]

        gmat = [lax.dot_general(cm[:, g * STATE:(g + 1) * STATE], bm[:, g * STATE:(g + 1) * STATE],
                                (((1,), (1,)), ((), ())), preferred_element_type=F32)
                for g in range(GROUPS)]
        zero_b = jnp.zeros((Q, LANES), BF16)
        y_parts = []
        for p in range(HEADS // 2):
            g = (2 * p) // (HEADS // GROUPS)
            ms = []
            for hh in (2 * p, 2 * p + 1):
                src = DT_DA0 + HEADS * d + hh
                seg = colb[:, hh * LANES:(hh + 1) * LANES] - sub_t[src:src + 1, :]
                ms.append((jnp.where(tri, jnp.exp(seg), 0.0) * gmat[g]).astype(BF16))
            mcat = jnp.concatenate(ms, axis=1)
            xp = xs_b[:, p * LANES:(p + 1) * LANES]
            rhs = jnp.concatenate([jnp.where(lane < HEAD_DIM, xp, zero_b),
                                   jnp.where(lane >= HEAD_DIM, xp, zero_b)], axis=0)
            y_parts.append(_dot(mcat, rhs))
        y_diag = jnp.concatenate(y_parts, axis=1)

        s_old = s_ref[d]
        s_bf = s_old.astype(BF16)
        y_off = jnp.concatenate(
            [_dot(cm[:, g * STATE:(g + 1) * STATE], s_bf[:, g * gw:(g + 1) * gw]) for g in range(GROUPS)],
            axis=1)
        y = y_diag + y_off * ecum_e

        xw = (xs * w_e).astype(BF16)
        upd = jnp.concatenate(
            [lax.dot_general(bm[:, g * STATE:(g + 1) * STATE], xw[:, g * gw:(g + 1) * gw],
                             (((0,), (0,)), ((), ())), preferred_element_type=F32) for g in range(GROUPS)],
            axis=1)
        s_ref[d] = s_old * decay_e + upd

        if first:
            y_ref[rows, :] = y + dsk_ref[...] * xs
        else:
            tot = y_ref[rows, :] + y
            zz = z_ref[0, rows, :].astype(F32)
            gz = tot * _silu(zz)
            outs = []
            for g in range(GROUPS):
                gg = gz[:, g * gw:(g + 1) * gw]
                outs.append(gg * lax.rsqrt(jnp.mean(gg * gg, axis=-1, keepdims=True) + NORM_EPS))
            o_ref[0, rows, :] = (jnp.concatenate(outs, axis=1) * g_ref[...]).astype(BF16)

    s_ref[...] = h0_ref[0]

    def first_half(i, carry):
        chunk(i, 0, True)
        chunk(nck - 1 - i, 1, True)
        return carry

    def second_half(i, carry):
        chunk(i, 0, False)
        chunk(nck - 1 - i, 1, False)
        return carry

    lax.fori_loop(0, nck // 2, first_half, 0, unroll=4)
    lax.fori_loop(nck // 2, nck, second_half, 0, unroll=4)


def _ssd(xbc3, z3, dt3, h0, cw, cb, dsk, g, e64, e128):
    bsz = xbc3.shape[0]
    const = lambda *shape: pl.BlockSpec(shape, lambda b: (0,) * len(shape))
    seq = lambda w: pl.BlockSpec((1, SEQ, w), lambda b: (b, 0, 0))
    return pl.pallas_call(
        _ssd_kernel,
        out_shape=jax.ShapeDtypeStruct((bsz, SEQ, D_SSD), BF16),
        grid=(bsz,),
        in_specs=[seq(XBC_W), seq(D_SSD), seq(LANES),
                  pl.BlockSpec((1, 2, STATE, D_SSD), lambda b: (b, 0, 0, 0)),
                  const(3, XBC_W), const(1, XBC_W),
                  const(1, D_SSD), const(1, D_SSD), const(2, 2 * LANES, D_SSD),
                  const(2, 2 * LANES, HEADS * LANES)],
        out_specs=seq(D_SSD),
        scratch_shapes=[pltpu.VMEM((SEQ, XBC_W), BF16), pltpu.VMEM((SEQ, D_SSD), F32),
                        pltpu.VMEM((2, STATE, D_SSD), F32)],
        compiler_params=pltpu.CompilerParams(dimension_semantics=("arbitrary",),
                                             vmem_limit_bytes=VMEM_LIMIT),
        name="ssd",
    )(xbc3, z3, dt3, h0, cw, cb, dsk, g, e64, e128)


def _outproj_kernel(x_ref, yssd_ref, scb_ref, v_ref, vp_ref, vn_ref, mod_ref, scw_ref, wo1_ref, wo2_ref,
                    g2_ref, wrt_ref, br_ref, x1_ref, lg_ref):
    tm = OUT_TILE
    per_b = SEQ // tm
    i = pl.program_id(0)
    first = (i % per_b) == 0
    last = (i % per_b) == per_b - 1
    m = mod_ref[0]
    v = v_ref[...].astype(F32)
    vp = jnp.where(first, 0.0, vp_ref[...].astype(F32))
    vn = jnp.where(last, 0.0, vn_ref[...].astype(F32))
    dn = jnp.concatenate([vp, v[:tm - GRID_W]], axis=0)
    up = jnp.concatenate([v[GRID_W:], vn], axis=0)
    scw = scw_ref[...]
    ysc = scb_ref[...].astype(F32) * (scw[0:1] * dn + scw[1:2] * v + scw[2:3] * up)
    out = _dot(yssd_ref[...], wo1_ref[...]) + _dot(ysc.astype(BF16), wo2_ref[...])
    x1 = x_ref[...] + m[2:3] * out
    x1_ref[...] = x1
    h2 = _rms(x1) * g2_ref[...] * (1.0 + m[4:5]) + m[3:4]
    h_hi = h2.astype(BF16)
    h_lo = (h2 - h_hi.astype(F32)).astype(BF16)
    lg_ref[...] = lax.dot_general(wrt_ref[...], jnp.concatenate([h_hi, h_lo, h_hi], axis=1),
                                  (((1,), (1,)), ((), ())), preferred_element_type=F32) + br_ref[...]


def _outproj(x2, yssd, scb, v, mod3, scw, wo1, wo2, g2, wrt, br):
    t = x2.shape[0]
    tm = OUT_TILE
    per_b = SEQ // tm
    r = tm // GRID_W
    nrow = t // GRID_W
    const = lambda *shape: pl.BlockSpec(shape, lambda i: (0,) * len(shape))
    tile = lambda w: pl.BlockSpec((tm, w), lambda i: (i, 0))
    return pl.pallas_call(
        _outproj_kernel,
        out_shape=(jax.ShapeDtypeStruct((t, D_MODEL), F32), jax.ShapeDtypeStruct((N_EXPERTS, t), F32)),
        grid=(t // tm,),
        in_specs=[tile(D_MODEL), tile(D_SSD), tile(D_SC), tile(D_SC),
                  pl.BlockSpec((GRID_W, D_SC), lambda i: (jnp.maximum(i * r - 1, 0), 0)),
                  pl.BlockSpec((GRID_W, D_SC), lambda i: (jnp.minimum((i + 1) * r, nrow - 1), 0)),
                  pl.BlockSpec((1, N_MOD, D_MODEL), lambda i: (i // per_b, 0, 0)),
                  const(3, D_SC), const(D_SSD, D_MODEL), const(D_SC, D_MODEL), const(1, D_MODEL),
                  const(N_EXPERTS, 3 * D_MODEL), const(N_EXPERTS, 1)],
        out_specs=(tile(D_MODEL), pl.BlockSpec((N_EXPERTS, tm), lambda i: (0, i))),
        compiler_params=pltpu.CompilerParams(dimension_semantics=("arbitrary",),
                                             vmem_limit_bytes=VMEM_LIMIT),
        name="outproj",
    )(x2, yssd, scb, v, v, v, mod3, scw, wo1, wo2, g2, wrt, br)


def _route_kernel(lg_ref, dest_ref, gate_ref, idx_ref, meta_ref, rank_ref, carry_ref, *, n_tok, n_blocks):
    tt = RT_TILE
    ne = N_EXPERTS
    eio = lax.broadcasted_iota(jnp.int32, (ne, tt), 0)
    si = lax.broadcasted_iota(jnp.int32, (tt, tt), 0)
    ti = lax.broadcasted_iota(jnp.int32, (tt, tt), 1)
    before = (si < ti).astype(BF16)
    carry_ref[...] = jnp.zeros_like(carry_ref)

    def tile_body(j, c):
        t0 = pl.multiple_of(j * tt, tt)
        l = lg_ref[:, pl.ds(t0, tt)]
        onehot = jnp.zeros((ne, tt), F32)
        tops, sels = [], []
        for _ in range(TOP_K):
            mx = jnp.max(l, axis=0, keepdims=True)
            idx = jnp.min(jnp.where(l == mx, eio, ne), axis=0, keepdims=True)
            sel = eio == idx
            l = jnp.where(sel, -jnp.inf, l)
            onehot = onehot + sel.astype(F32)
            tops.append(mx)
            sels.append(sel)
            idx_ref[pl.ds(len(tops) - 1, 1), pl.ds(t0, tt)] = idx
        ex = [jnp.exp(tv - tops[0]) for tv in tops]
        den = ex[0] + ex[1] + ex[2] + ex[3]
        prefix = _dot(onehot.astype(BF16), before) + carry_ref[:, 0:1]
        for k in range(TOP_K):
            gate_ref[pl.ds(k, 1), pl.ds(t0, tt)] = ex[k] / den
            rk = jnp.sum(jnp.where(sels[k], prefix, 0.0), axis=0, keepdims=True)
            rank_ref[pl.ds(k, 1), pl.ds(t0, tt)] = rk.astype(jnp.int32)
        carry_ref[...] = carry_ref[...] + jnp.sum(onehot, axis=1, keepdims=True)
        return c

    lax.fori_loop(0, n_tok // tt, tile_body, 0)

    counts = carry_ref[...]
    padded = jnp.floor((counts + (MOE_BM - 1)) * (1.0 / MOE_BM)) * MOE_BM
    er = lax.broadcasted_iota(jnp.int32, (ne, ne), 0)
    ec = lax.broadcasted_iota(jnp.int32, (ne, ne), 1)
    pad_start = jnp.dot((ec < er).astype(F32), padded, precision=HIGHEST, preferred_element_type=F32)
    pad_end = pad_start + padded

    def dest_body(j, c):
        t0 = pl.multiple_of(j * tt, tt)
        for k in range(TOP_K):
            idx = idx_ref[pl.ds(k, 1), pl.ds(t0, tt)]
            base = jnp.sum(jnp.where(eio == idx, pad_start[:, 0:1], 0.0), axis=0, keepdims=True)
            dest_ref[pl.ds(k, 1), pl.ds(t0, tt)] = base.astype(jnp.int32) + rank_ref[pl.ds(k, 1), pl.ds(t0, tt)]
        return c

    lax.fori_loop(0, n_tok // tt, dest_body, 0)

    width = meta_ref.shape[1]
    sub = lax.broadcasted_iota(jnp.int32, (ne, width), 0)
    lan = lax.broadcasted_iota(jnp.int32, (ne, width), 1)
    diag = sub == lan
    cnt_row = jnp.sum(jnp.where(diag, counts[:, 0:1], 0.0), axis=0, keepdims=True)
    start_row = jnp.sum(jnp.where(diag, pad_start[:, 0:1], 0.0), axis=0, keepdims=True)
    blk_start = (lan * MOE_BM).astype(F32)
    blk_exp = jnp.sum((pad_end[:, 0:1] <= blk_start).astype(F32), axis=0, keepdims=True)
    blk_exp = jnp.minimum(blk_exp, float(ne - 1))
    used = jnp.sum(padded[:, 0:1], axis=0, keepdims=True) * (1.0 / MOE_BM)
    meta_ref[0:1, :] = cnt_row.astype(jnp.int32)
    meta_ref[1:2, :] = start_row.astype(jnp.int32)
    meta_ref[2:3, :] = blk_exp.astype(jnp.int32)
    meta_ref[3:4, :] = jnp.broadcast_to(used, (1, width)).astype(jnp.int32)
    meta_ref[4:8, :] = jnp.zeros((4, width), jnp.int32)


def _route(lgt, n_blocks):
    ne, n_tok = lgt.shape
    width = -(-n_blocks // LANES) * LANES
    full = lambda *shape: pl.BlockSpec(shape, lambda: (0,) * len(shape))
    return pl.pallas_call(
        functools.partial(_route_kernel, n_tok=n_tok, n_blocks=n_blocks),
        out_shape=(jax.ShapeDtypeStruct((TOP_K, n_tok), jnp.int32),
                   jax.ShapeDtypeStruct((TOP_K, n_tok), F32),
                   jax.ShapeDtypeStruct((TOP_K, n_tok), jnp.int32),
                   jax.ShapeDtypeStruct((8, width), jnp.int32)),
        in_specs=[full(ne, n_tok)],
        out_specs=(full(TOP_K, n_tok), full(TOP_K, n_tok), full(TOP_K, n_tok), full(8, width)),
        scratch_shapes=[pltpu.VMEM((TOP_K, n_tok), jnp.int32), pltpu.VMEM((ne, LANES), F32)],
        compiler_params=pltpu.CompilerParams(vmem_limit_bytes=VMEM_LIMIT),
        name="route",
    )(lgt)


def _dispatch_kernel(dest_ref, cnt_ref, start_ref, nu_ref, x1_ref, meta_ref, mod_ref, g2_ref, zsrc_ref, xs_ref,
                     hbuf, sem, zsem):
    i = pl.program_id(0)
    n = pl.num_programs(0)
    tl = DISP_TILE
    slot = i % 2
    nb = xs_ref.shape[0] // MOE_BM

    def zero_block(b):
        return pltpu.make_async_copy(zsrc_ref, xs_ref.at[pl.ds(b * MOE_BM, MOE_BM)], zsem)

    @pl.when(i == 0)
    def _():
        def zero_piece(row0):
            return pltpu.make_async_copy(zsrc_ref.at[pl.ds(0, ZERO_PIECE)], xs_ref.at[pl.ds(row0, ZERO_PIECE)],
                                         zsem)

        def per_expert(e, started):
            rem = cnt_ref[e] & (MOE_BM - 1)
            base = start_ref[e] + cnt_ref[e] - rem
            first = jnp.where(rem > 0, rem // ZERO_PIECE, MOE_BM // ZERO_PIECE)

            def piece(q, c2):
                if started:
                    zero_piece(0).wait()
                else:
                    zero_piece(base + q * ZERO_PIECE).start()
                return c2

            lax.fori_loop(first, MOE_BM // ZERO_PIECE, piece, 0)

        def start_e(e, c):
            per_expert(e, False)
            return c

        def wait_e(e, c):
            per_expert(e, True)
            return c

        def start_t(b, c):
            zero_block(b).start()
            return c

        def wait_t(b, c):
            zero_block(0).wait()
            return c

        lax.fori_loop(0, N_EXPERTS, start_e, 0)
        lax.fori_loop(nu_ref[0], nb, start_t, 0)
        lax.fori_loop(0, N_EXPERTS, wait_e, 0)
        lax.fori_loop(nu_ref[0], nb, wait_t, 0)

    m = mod_ref[0]
    h2 = _rms(x1_ref[...]) * g2_ref[...] * (1.0 + m[4:5]) + m[3:4]
    lo = pltpu.bitcast(h2[:, :PACK_W].astype(BF16).astype(F32), jnp.uint32) >> 16
    hi = pltpu.bitcast(h2[:, PACK_W:].astype(BF16).astype(F32), jnp.uint32) & jnp.uint32(0xFFFF0000)
    row = jnp.concatenate([lo | hi, meta_ref[...], jnp.zeros((tl, D_MODEL - PACK_W - LANES), jnp.uint32)], axis=1)
    hbuf[slot] = row.reshape(tl, SUBLANES, LANES)

    def per_tok(t, c):
        tok = i * tl + t
        for k in range(TOP_K):
            d = dest_ref[k * (dest_ref.shape[0] // TOP_K) + tok]
            pltpu.make_async_copy(hbuf.at[slot, t], xs_ref.at[d], sem.at[slot]).start(priority=k % 2)
        return c

    lax.fori_loop(0, tl, per_tok, 0, unroll=8)

    def wait_slot(sl):
        for _ in range(TOP_K):
            pltpu.make_async_copy(hbuf.at[sl], xs_ref.at[pl.ds(0, tl)], sem.at[sl]).wait()

    @pl.when(i > 0)
    def _():
        wait_slot(1 - slot)

    @pl.when(i == n - 1)
    def _():
        wait_slot(slot)


def _dispatch(dest_flat, cnt, start, n_used, x1, meta_rows, mod3, g2, zsrc, n_rows):
    n_tok = x1.shape[0]
    tl = DISP_TILE
    per_b = SEQ // tl
    return pl.pallas_call(
        _dispatch_kernel,
        out_shape=jax.ShapeDtypeStruct((n_rows, SUBLANES, LANES), jnp.uint32),
        grid_spec=pltpu.PrefetchScalarGridSpec(
            num_scalar_prefetch=4,
            grid=(n_tok // tl,),
            in_specs=[pl.BlockSpec((tl, D_MODEL), lambda i, *_: (i, 0)),
                      pl.BlockSpec((tl, LANES), lambda i, *_: (i, 0)),
                      pl.BlockSpec((1, N_MOD, D_MODEL), lambda i, *_: (i // per_b, 0, 0)),
                      pl.BlockSpec((1, D_MODEL), lambda i, *_: (0, 0)),
                      pl.BlockSpec((MOE_BM, SUBLANES, LANES), lambda i, *_: (0, 0, 0))],
            out_specs=pl.BlockSpec(memory_space=pl.ANY),
            scratch_shapes=[pltpu.VMEM((2, tl, SUBLANES, LANES), jnp.uint32),
                            pltpu.SemaphoreType.DMA((2,)), pltpu.SemaphoreType.DMA]),
        compiler_params=pltpu.CompilerParams(dimension_semantics=("arbitrary",),
                                             vmem_limit_bytes=VMEM_LIMIT),
        name="dispatch",
    )(dest_flat, cnt, start, n_used, x1, meta_rows, mod3, g2, zsrc)


def _expert_rows_kernel(be_ref, nu_ref, xs_ref, wgu_hbm, bgu_ref, wd_hbm, bd_ref, ys_ref,
                        wgu_raw, wd_raw, wgu_bf, wd_bf, w_sem):
    j = pl.program_id(0)
    nu = nu_ref[0]
    n_lt = D_MODEL // LANES

    def weight_copies(e):
        return (pltpu.make_async_copy(wgu_hbm.at[e], wgu_raw, w_sem.at[0]),
                pltpu.make_async_copy(wd_hbm.at[e], wd_raw, w_sem.at[1]))

    @pl.when(j == 0)
    def _():
        for w in weight_copies(be_ref[0]):
            w.start()

    e_now = be_ref[j]
    new_expert = jnp.logical_or(j == 0, be_ref[jnp.maximum(j - 1, 0)] != e_now)

    @pl.when(jnp.logical_and(j < nu, new_expert))
    def _():
        for w in weight_copies(e_now):
            w.wait()
        n_cc = 8
        for c in range(n_cc):
            cc = slice(c * (2 * D_FF // n_cc), (c + 1) * (2 * D_FF // n_cc))
            wgu_bf[:, cc] = wgu_raw[:, cc].astype(BF16)
        for c in range(n_cc // 2):
            cc = slice(c * (2 * D_MODEL // n_cc), (c + 1) * (2 * D_MODEL // n_cc))
            wd_bf[:, cc] = wd_raw[:, cc].astype(BF16)
        j_next = lax.while_loop(lambda t: jnp.logical_and(t < nu, be_ref[jnp.minimum(t, pl.num_programs(0) - 1)] == e_now),
                                lambda t: t + 1, j + 1)

        @pl.when(j_next < nu)
        def _():
            for w in weight_copies(be_ref[jnp.minimum(j_next, pl.num_programs(0) - 1)]):
                w.start()

    @pl.when(j < nu)
    def _():
        e_f = e_now.astype(F32)
        pr = MOE_BM // MOE_PARTS
        for p in range(MOE_PARTS):
            words = xs_ref[pl.ds(p * pr, pr)].reshape(pr, D_MODEL)
            packed = words[:, 0:PACK_W]
            meta = pltpu.bitcast(words[:, PACK_W:PACK_W + LANES], F32)
            xb = jnp.concatenate(
                [pltpu.bitcast(packed << 16, F32).astype(BF16),
                 pltpu.bitcast(packed & jnp.uint32(0xFFFF0000), F32).astype(BF16)], axis=1)
            gate = jnp.zeros((pr, 1), F32)
            for k in range(TOP_K):
                mk = meta[:, META_IDX + k:META_IDX + k + 1] == e_f
                gate = gate + jnp.where(mk, meta[:, META_GATE + k:META_GATE + k + 1], 0.0)
            gu = _dot(xb, wgu_bf[...]) + bgu_ref[0]
            glu = jnp.minimum(gu[:, :D_FF], SWIGLU_LIMIT)
            lin = jnp.clip(gu[:, D_FF:], -SWIGLU_LIMIT, SWIGLU_LIMIT)
            act = glu * jax.nn.sigmoid(SWIGLU_ALPHA * glu) * (lin + 1.0)
            y = (_dot(act.astype(BF16), wd_bf[...]) + bd_ref[0]) * gate
            ys_ref[pl.ds(p * pr, pr)] = y.reshape(pr, n_lt, LANES)

    @pl.when(j >= nu)
    def _():
        ys_ref[...] = jnp.zeros_like(ys_ref)


def _expert_rows(blk_exp, n_used, xs, wgu, bgu, wd, bd):
    n_rows = xs.shape[0]
    nb = n_rows // MOE_BM
    n_lt = D_MODEL // LANES
    row_blk = lambda j, be, nu: (jnp.minimum(j, nu[0] - 1), 0, 0)
    per_e = lambda j, be, nu: (be[j], 0, 0)
    return pl.pallas_call(
        _expert_rows_kernel,
        out_shape=jax.ShapeDtypeStruct((n_rows, n_lt, LANES), F32),
        grid_spec=pltpu.PrefetchScalarGridSpec(
            num_scalar_prefetch=2,
            grid=(nb,),
            in_specs=[pl.BlockSpec((MOE_BM, SUBLANES, LANES), row_blk),
                      pl.BlockSpec(memory_space=pl.ANY),
                      pl.BlockSpec((1, 1, 2 * D_FF), per_e),
                      pl.BlockSpec(memory_space=pl.ANY),
                      pl.BlockSpec((1, 1, D_MODEL), per_e)],
            out_specs=pl.BlockSpec((MOE_BM, n_lt, LANES), lambda j, be, nu: (j, 0, 0)),
            scratch_shapes=[pltpu.VMEM((D_MODEL, 2 * D_FF), F32), pltpu.VMEM((D_FF, D_MODEL), F32),
                            pltpu.VMEM((D_MODEL, 2 * D_FF), BF16), pltpu.VMEM((D_FF, D_MODEL), BF16),
                            pltpu.SemaphoreType.DMA((2,))]),
        compiler_params=pltpu.CompilerParams(dimension_semantics=("arbitrary",),
                                             vmem_limit_bytes=VMEM_LIMIT),
        name="experts",
    )(blk_exp, n_used, xs, wgu, bgu, wd, bd)


def _gather_combine_kernel(dest_ref, ys_ref, x1_ref, mod_ref, fg_ref, o_ref, buf, sem):
    i = pl.program_id(0)
    n = pl.num_programs(0)
    tc = GATHER_TILE
    slot = i % 2

    def issue(tile, sl):
        def per_tok(t, c):
            tok = tile * tc + t
            for k in range(TOP_K):
                d = dest_ref[k * (dest_ref.shape[0] // TOP_K) + tok]
                pltpu.make_async_copy(ys_ref.at[d], buf.at[sl, k * tc + t], sem.at[sl]).start(priority=k % 2)
            return c

        lax.fori_loop(0, tc, per_tok, 0, unroll=8)

    @pl.when(i == 0)
    def _():
        issue(0, 0)

    @pl.when(i + 1 < n)
    def _():
        issue(i + 1, 1 - slot)

    for _ in range(TOP_K):
        pltpu.make_async_copy(ys_ref.at[pl.ds(0, tc)], buf.at[slot, pl.ds(0, tc)], sem.at[slot]).wait()

    moe = ((buf[slot, pl.ds(0, tc)] + buf[slot, pl.ds(tc, tc)])
           + (buf[slot, pl.ds(2 * tc, tc)] + buf[slot, pl.ds(3 * tc, tc)])).reshape(tc, D_MODEL)
    m = mod_ref[0]
    x2 = x1_ref[...] + m[5:6] * moe
    o_ref[...] = _rms(x2) * fg_ref[...]


def _gather_combine(dest_flat, ys, x1, mod3, fg):
    n_tok = x1.shape[0]
    tc = GATHER_TILE
    per_b = SEQ // tc
    n_lt = D_MODEL // LANES
    return pl.pallas_call(
        _gather_combine_kernel,
        out_shape=jax.ShapeDtypeStruct((n_tok, D_MODEL), F32),
        grid_spec=pltpu.PrefetchScalarGridSpec(
            num_scalar_prefetch=1,
            grid=(n_tok // tc,),
            in_specs=[pl.BlockSpec(memory_space=pl.ANY),
                      pl.BlockSpec((tc, D_MODEL), lambda i, d: (i, 0)),
                      pl.BlockSpec((1, N_MOD, D_MODEL), lambda i, d: (i // per_b, 0, 0)),
                      pl.BlockSpec((1, D_MODEL), lambda i, d: (0, 0))],
            out_specs=pl.BlockSpec((tc, D_MODEL), lambda i, d: (i, 0)),
            scratch_shapes=[pltpu.VMEM((2, TOP_K * tc, n_lt, LANES), F32), pltpu.SemaphoreType.DMA((2,))]),
        compiler_params=pltpu.CompilerParams(dimension_semantics=("arbitrary",),
                                             vmem_limit_bytes=VMEM_LIMIT),
        name="combine",
    )(dest_flat, ys, x1, mod3, fg)


def _expansion_matrices(src0):
    r = (jnp.arange(2 * LANES) % LANES)[:, None]
    out64, out128 = [], []
    for d in range(2):
        l64 = jnp.arange(D_SSD)[None, :]
        l128 = jnp.arange(HEADS * LANES)[None, :]
        out64.append((l64 // HEAD_DIM == r - src0 - HEADS * d).astype(BF16))
        out128.append((l128 // LANES == r - src0 - HEADS * d).astype(BF16))
    return jnp.stack(out64), jnp.stack(out128)


def _pad_lanes(v):
    return jnp.pad(v, [(0, 0)] * (v.ndim - 1) + [(0, LANES - v.shape[-1])])


def kernel(x, c, ctx, c_ctx, w_mod, b_mod, norm1_g, w_in, ssd_conv_w, ssd_conv_b, ssd_dt_bias, ssd_a_log,
           ssd_d, ssd_norm_g, sc_conv_w, w_out, norm2_g, w_router, b_router, w_gate_up, b_gate_up, w_down,
           b_down, final_g):
    bsz = x.shape[0]
    n_tok = bsz * SEQ
    n_assign = n_tok * TOP_K
    n_blocks = n_assign // MOE_BM + N_EXPERTS
    n_rows = n_blocks * MOE_BM
    li = 0

    cvec = jnp.concatenate([c, c_ctx[None, :], jnp.zeros((7, D_MODEL), F32)], axis=0)
    mod3 = _mod(cvec, w_mod, b_mod[li][None, :], li).reshape(bsz + 8, N_MOD, D_MODEL)

    w = w_in[li]
    wz = w[:, Z0:X0].astype(BF16)
    wxbc = w[:, X0:DT0].astype(BF16)
    wdt = _pad_lanes(w[:, DT0:SC0]).astype(BF16)
    wb = w[:, SC0:SC0 + D_SC].astype(BF16)
    wc = w[:, SC0 + D_SC:SC0 + 2 * D_SC].astype(BF16)
    wu = w[:, SC0 + 2 * D_SC:].astype(BF16)
    g1 = norm1_g[li][None, :]
    cw = ssd_conv_w[li]
    cb = ssd_conv_b[li][None, :]
    dtb = _pad_lanes(ssd_dt_bias[li].reshape(1, 2 * HEADS))
    alog = _pad_lanes(ssd_a_log[li].reshape(1, 2 * HEADS))
    e64_ctx, _ = _expansion_matrices(0)

    h0 = _ctx_states(ctx, mod3, g1, wxbc[:, :XB_W], wdt, cw[:, :XB_W], cb[:, :XB_W], dtb, alog, e64_ctx)

    rep = lambda a: _pad_lanes(jnp.tile(a[..., :2 * HEADS], (1, DT_COPIES)))
    e64, e128 = _expansion_matrices(DT_DA0)
    x2 = x.reshape(n_tok, D_MODEL)
    z, xbc, dtp, scb, v = _inproj(x2, mod3, g1, wz, wxbc, rep(wdt), rep(dtb), rep(alog), wb, wc, wu)

    dsk = jnp.repeat(ssd_d[li], HEAD_DIM)[None, :]
    yssd = _ssd(xbc.reshape(bsz, SEQ, XBC_W), z.reshape(bsz, SEQ, D_SSD), dtp.reshape(bsz, SEQ, LANES), h0,
                cw, cb, dsk, ssd_norm_g[li][None, :], e64, e128)

    wo = w_out[li].astype(BF16)
    g2 = norm2_g[li][None, :]
    wr = w_router[li].T
    wr_hi = wr.astype(BF16)
    wr_lo = (wr - wr_hi.astype(F32)).astype(BF16)
    x1, lgt = _outproj(x2, yssd.reshape(n_tok, D_SSD), scb, v, mod3, sc_conv_w[li], wo[:D_SSD], wo[D_SSD:],
                       g2, jnp.concatenate([wr_hi, wr_hi, wr_lo], axis=1), b_router[li][:, None])

    dest_t, gate_t, idx_t, meta = _route(lgt, n_blocks)
    dest_flat = dest_t.reshape(n_assign)
    cnt = meta[0, :N_EXPERTS]
    start = meta[1, :N_EXPERTS]
    blk_exp = meta[2, :n_blocks]
    n_used = meta[3, :1]

    meta_rows = lax.bitcast_convert_type(_pad_lanes(jnp.concatenate(
        [idx_t.T.astype(F32), gate_t.T], axis=1)), jnp.uint32)
    pad_meta = lax.bitcast_convert_type(_pad_lanes(jnp.concatenate(
        [jnp.full((MOE_BM, TOP_K), -1.0, F32), jnp.zeros((MOE_BM, TOP_K), F32)], axis=1)), jnp.uint32)
    zsrc = jnp.concatenate([jnp.zeros((MOE_BM, PACK_W), jnp.uint32), pad_meta,
                            jnp.zeros((MOE_BM, D_MODEL - PACK_W - LANES), jnp.uint32)],
                           axis=1).reshape(MOE_BM, SUBLANES, LANES)

    xs = _dispatch(dest_flat, cnt, start, n_used, x1, meta_rows, mod3, g2, zsrc, n_rows)
    ys = _expert_rows(blk_exp, n_used, xs, w_gate_up[li], b_gate_up[li][:, None, :],
                      w_down[li], b_down[li][:, None, :])
    out = _gather_combine(dest_flat, ys, x1, mod3, final_g[None, :])
    return out.reshape(bsz, SEQ, D_MODEL)
```

```python
import functools

import jax
import jax.numpy as jnp
from jax import lax
from jax.experimental import pallas as pl
from jax.experimental.pallas import tpu as pltpu

F32 = jnp.float32
BF16 = jnp.bfloat16
HIGHEST = lax.Precision.HIGHEST

D_MODEL = 1024
SEQ = 2048
CTX_LEN = 256
GRID_W = 64
D_SSD = 1024
D_SC = 1024
HEAD_DIM = 64
HEADS = 16
GROUPS = 2
STATE = 128
CHUNK = 128
N_EXPERTS = 32
TOP_K = 4
D_FF = 1024
SWIGLU_LIMIT = 7.0
SWIGLU_ALPHA = 1.702
NORM_EPS = 1e-6
N_MOD = 6
XBC_W = D_SSD + 2 * GROUPS * STATE
XB_W = D_SSD + GROUPS * STATE
LANES = 128

Z0 = 0
X0 = Z0 + D_SSD
B0 = X0 + D_SSD
C0 = B0 + GROUPS * STATE
DT0 = C0 + GROUPS * STATE
SC0 = DT0 + 2 * HEADS

TOK_TILE = 1024
OUT_TILE = 1024
MOE_BM = 512
MOE_PARTS = 2
ZERO_PIECE = 128
RT_TILE = 512
CTX_BATCH = 4
DISP_TILE = 512
GATHER_TILE = 256
DT_COPIES = 3
DT_LOG0 = 2 * HEADS
DT_DA0 = 4 * HEADS
SUBLANES = 8
PACK_W = D_MODEL // 2
META_IDX = 0
META_GATE = TOP_K
VMEM_LIMIT = 56 * 1024 * 1024


def _silu(v):
    return v * jax.nn.sigmoid(v)


def _softplus(v):
    return jnp.maximum(v, 0.0) + jnp.log1p(jnp.exp(-jnp.abs(v)))


def _rms(v):
    return v * lax.rsqrt(jnp.mean(v * v, axis=-1, keepdims=True) + NORM_EPS)


def _dot(a, b):
    return jnp.dot(a, b, preferred_element_type=F32)


def _expand2(v, e2):
    hi = v.astype(BF16)
    lo = (v - hi.astype(F32)).astype(BF16)
    return _dot(jnp.concatenate([hi, lo], axis=1), e2)


def _mod_kernel(c_ref, w_ref, b_ref, o_ref):
    o_ref[...] = jnp.dot(_silu(c_ref[...]), w_ref[0], precision=HIGHEST,
                         preferred_element_type=F32) + b_ref[...]


def _mod(cvec, w_mod, b_mod, layer):
    rows = cvec.shape[0]
    n = w_mod.shape[2]
    tn = 1536
    return pl.pallas_call(
        _mod_kernel,
        out_shape=jax.ShapeDtypeStruct((rows, n), F32),
        grid=(n // tn,),
        in_specs=[pl.BlockSpec((rows, D_MODEL), lambda j: (0, 0)),
                  pl.BlockSpec((1, D_MODEL, tn), lambda j: (layer, 0, j)),
                  pl.BlockSpec((1, tn), lambda j: (0, j))],
        out_specs=pl.BlockSpec((rows, tn), lambda j: (0, j)),
        compiler_params=pltpu.CompilerParams(dimension_semantics=("arbitrary",),
                                             vmem_limit_bytes=VMEM_LIMIT),
        name="mod",
    )(cvec, w_mod, b_mod)


def _ctx_kernel(ctx_ref, mod_ref, g1_ref, wxb_ref, wdt_ref, cw_ref, cb_ref, dtb_ref, alog_ref, e64_ref,
                h0_ref):
    L = CTX_LEN
    nb = ctx_ref.shape[0]
    m = mod_ref[0]
    hc = _rms(ctx_ref[...].reshape(nb * L, D_MODEL)) * g1_ref[...] * (1.0 + m[1:2]) + m[0:1]
    hb = hc.astype(BF16)
    pxb = _dot(hb, wxb_ref[...])
    dtr = _dot(hb, wdt_ref[...])
    rowl = lax.broadcasted_iota(jnp.int32, (nb * L, XB_W), 0) & (L - 1)
    dn = jnp.where(rowl == 0, 0.0, pltpu.roll(pxb, 1, 0))
    up = jnp.where(rowl == L - 1, 0.0, pltpu.roll(pxb, nb * L - 1, 0))
    cw = cw_ref[...]
    xb = _silu(cw[0:1] * dn + cw[1:2] * pxb + cw[2:3] * up + cb_ref[...])
    dt_all = _softplus(dtr + dtb_ref[...])
    da_all = dt_all * (-jnp.exp(alog_ref[...]))
    ri = lax.broadcasted_iota(jnp.int32, (L, L), 0)
    ci = lax.broadcasted_iota(jnp.int32, (L, L), 1)
    for bi in range(nb):
        rs = slice(bi * L, (bi + 1) * L)
        xs = xb[rs, :D_SSD]
        bm = xb[rs, D_SSD:].astype(BF16)
        dt = dt_all[rs]
        da = da_all[rs]
        for d in range(2):
            tri = (ci <= ri) if d == 0 else (ci >= ri)
            cum = jnp.dot(tri.astype(F32), da, precision=HIGHEST, preferred_element_type=F32)
            last = cum[L - 1:L] if d == 0 else cum[0:1]
            w_e = _expand2(jnp.exp(last - cum) * dt, e64_ref[d])
            xw = (xs * w_e).astype(BF16)
            for g in range(GROUPS):
                gw = D_SSD // GROUPS
                st = lax.dot_general(bm[:, g * STATE:(g + 1) * STATE], xw[:, g * gw:(g + 1) * gw],
                                     (((0,), (0,)), ((), ())), preferred_element_type=F32)
                h0_ref[bi, d, :, g * gw:(g + 1) * gw] = st


def _ctx_states(ctx, mod3, g1, wxb, wdt, cw, cb, dtb, alog, e64):
    bsz = ctx.shape[0]
    mod_row = bsz
    nb = CTX_BATCH if bsz % CTX_BATCH == 0 else 1
    const = lambda *shape: pl.BlockSpec(shape, lambda b: (0,) * len(shape))
    return pl.pallas_call(
        _ctx_kernel,
        out_shape=jax.ShapeDtypeStruct((bsz, 2, STATE, D_SSD), F32),
        grid=(bsz // nb,),
        in_specs=[pl.BlockSpec((nb, CTX_LEN, D_MODEL), lambda b: (b, 0, 0)),
                  pl.BlockSpec((1, N_MOD, D_MODEL), lambda b: (mod_row, 0, 0)),
                  const(1, D_MODEL), const(D_MODEL, XB_W), const(D_MODEL, LANES),
                  const(3, XB_W), const(1, XB_W), const(1, LANES), const(1, LANES),
                  const(2, 2 * LANES, D_SSD)],
        out_specs=pl.BlockSpec((nb, 2, STATE, D_SSD), lambda b: (b, 0, 0, 0)),
        compiler_params=pltpu.CompilerParams(dimension_semantics=("arbitrary",),
                                             vmem_limit_bytes=VMEM_LIMIT),
        name="ctx_states",
    )(ctx, mod3, g1, wxb, wdt, cw, cb, dtb, alog, e64)


def _inproj_kernel(x_ref, mod_ref, g1_ref, wz_ref, wxbc_ref, wdt_ref, dtb_ref, alog_ref, wb_ref, wc_ref, wu_ref,
                   z_ref, xbc_ref, dt_ref, scb_ref, v_ref):
    m = mod_ref[0]
    hx = _rms(x_ref[...]) * g1_ref[...] * (1.0 + m[1:2]) + m[0:1]
    hb = hx.astype(BF16)
    z_ref[...] = _dot(hb, wz_ref[...]).astype(BF16)
    xbc_ref[...] = _dot(hb, wxbc_ref[...]).astype(BF16)
    dt = _softplus(_dot(hb, wdt_ref[...]) + dtb_ref[...])
    lane = lax.broadcasted_iota(jnp.int32, dt.shape, 1)
    dt_ref[...] = jnp.where(lane < DT_LOG0, dt,
                            jnp.where(lane < DT_DA0, jnp.log(dt), dt * (-jnp.exp(alog_ref[...]))))
    scb_ref[...] = _dot(hb, wb_ref[...]).astype(BF16)
    v_ref[...] = (_dot(hb, wc_ref[...]) * _dot(hb, wu_ref[...])).astype(BF16)


def _inproj(x2, mod3, g1, wz, wxbc, wdt, dtb, alog, wb, wc, wu):
    t = x2.shape[0]
    tm = TOK_TILE
    per_b = SEQ // tm
    const = lambda *shape: pl.BlockSpec(shape, lambda i: (0,) * len(shape), pipeline_mode=pl.Buffered(1))
    tile = lambda w: pl.BlockSpec((tm, w), lambda i: (i, 0))
    return pl.pallas_call(
        _inproj_kernel,
        out_shape=(jax.ShapeDtypeStruct((t, D_SSD), BF16), jax.ShapeDtypeStruct((t, XBC_W), BF16),
                   jax.ShapeDtypeStruct((t, LANES), F32), jax.ShapeDtypeStruct((t, D_SC), BF16),
                   jax.ShapeDtypeStruct((t, D_SC), BF16)),
        grid=(t // tm,),
        in_specs=[tile(D_MODEL),
                  pl.BlockSpec((1, N_MOD, D_MODEL), lambda i: (i // per_b, 0, 0)),
                  const(1, D_MODEL), const(D_MODEL, D_SSD), const(D_MODEL, XBC_W), const(D_MODEL, LANES),
                  const(1, LANES), const(1, LANES),
                  const(D_MODEL, D_SC), const(D_MODEL, D_SC), const(D_MODEL, D_SC)],
        out_specs=(tile(D_SSD), tile(XBC_W), tile(LANES), tile(D_SC), tile(D_SC)),
        compiler_params=pltpu.CompilerParams(dimension_semantics=("arbitrary",),
                                             vmem_limit_bytes=VMEM_LIMIT),
        name="inproj",
    )(x2, mod3, g1, wz, wxbc, wdt, dtb, alog, wb, wc, wu)


def _ssd_kernel(xbc_ref, z_ref, dt_ref, h0_ref, cw_ref, cb_ref, dsk_ref, g_ref,
                e64_ref, e128_ref, o_ref, xc_ref, y_ref, s_ref):
    Q = CHUNK
    nck = SEQ // Q
    gw = D_SSD // GROUPS

    rowi = lax.broadcasted_iota(jnp.int32, (SUBLANES, XBC_W), 0)

    def conv_body(c, carry):
        r0 = pl.multiple_of(c * Q, Q)
        main = xbc_ref[0, pl.ds(r0, Q), :].astype(F32)
        pstart = pl.multiple_of(jnp.maximum(r0 - 16, 0), 16)
        nstart = pl.multiple_of(jnp.minimum(r0 + Q, SEQ - 16), 16)
        prev = xbc_ref[0, pl.ds(pstart, 16), :].astype(F32)[15:16]
        nxt = xbc_ref[0, pl.ds(nstart, 16), :].astype(F32)[0:1]
        prev = jnp.where(c > 0, prev, 0.0)
        nxt = jnp.where(c < nck - 1, nxt, 0.0)
        dn = pltpu.roll(main, 1, 0)
        up = pltpu.roll(main, Q - 1, 0)
        dn = jnp.concatenate([jnp.where(rowi == 0, prev, dn[0:SUBLANES]), dn[SUBLANES:]], axis=0)
        up = jnp.concatenate([up[:Q - SUBLANES], jnp.where(rowi == SUBLANES - 1, nxt, up[Q - SUBLANES:])], axis=0)
        cw = cw_ref[...]
        conv = cw[0:1] * dn + cw[1:2] * main + cw[2:3] * up + cb_ref[...]
        xc_ref[pl.ds(r0, Q), :] = _silu(conv).astype(BF16)
        return carry

    lax.fori_loop(0, nck, conv_body, 0)

    ri = lax.broadcasted_iota(jnp.int32, (Q, Q), 0)
    ci = lax.broadcasted_iota(jnp.int32, (Q, Q), 1)
    lane = lax.broadcasted_iota(jnp.int32, (Q, LANES), 1)
    da_lanes = jnp.logical_and(lane >= DT_DA0, lane < DT_DA0 + 2 * HEADS)

    def chunk(c, d, first):
        r0 = pl.multiple_of(c * Q, Q)
        rows = pl.ds(r0, Q)
        xs_b = xc_ref[rows, 0:D_SSD]
        xs = xs_b.astype(F32)
        bm = xc_ref[rows, D_SSD:D_SSD + GROUPS * STATE]
        cm = xc_ref[rows, D_SSD + GROUPS * STATE:XBC_W]
        dtp = dt_ref[0, rows, :]
        da = jnp.where(da_lanes, dtp, 0.0)
        dt = pltpu.roll(dtp, DT_DA0, 1)
        log_dt = pltpu.roll(dtp, DT_DA0 - DT_LOG0, 1)
        tri = (ci <= ri) if d == 0 else (ci >= ri)
        p0 = da.astype(BF16)
        r1 = da - p0.astype(F32)
        p1 = r1.astype(BF16)
        p2 = (r1 - p1.astype(F32)).astype(BF16)
        tri_b = jnp.where(tri, 1.0, 0.0).astype(BF16)
        cum = _dot(jnp.concatenate([tri_b, tri_b, tri_b], axis=1),
                   jnp.concatenate([p0, p1, p2], axis=0))
        sub_t = (cum - log_dt).T
        last = cum[Q - 1:Q] if d == 0 else cum[0:1]
        ecum_e = _expand2(jnp.exp(cum), e64_ref[d])
        w_e = _expand2(jnp.where(da_lanes, jnp.exp(last - cum) * dt, 0.0), e64_ref[d])
        colb = _expand2(cum, e128_ref[d])
        decay_e = ecum_e[Q - 1:Q] if d == 0 else ecum_e[0:1]

        gmat = [lax.dot_general(cm[:, g * STATE:(g + 1) * STATE], bm[:, g * STATE:(g + 1) * STATE],
                                (((1,), (1,)), ((), ())), preferred_element_type=F32)
                for g in range(GROUPS)]
        zero_b = jnp.zeros((Q, LANES), BF16)
        y_parts = []
        for p in range(HEADS // 2):
            g = (2 * p) // (HEADS // GROUPS)
            ms = []
            for hh in (2 * p, 2 * p + 1):
                src = DT_DA0 + HEADS * d + hh
                seg = colb[:, hh * LANES:(hh + 1) * LANES] - sub_t[src:src + 1, :]
                ms.append((jnp.where(tri, jnp.exp(seg), 0.0) * gmat[g]).astype(BF16))
            mcat = jnp.concatenate(ms, axis=1)
            xp = xs_b[:, p * LANES:(p + 1) * LANES]
            rhs = jnp.concatenate([jnp.where(lane < HEAD_DIM, xp, zero_b),
                                   jnp.where(lane >= HEAD_DIM, xp, zero_b)], axis=0)
            y_parts.append(_dot(mcat, rhs))
        y_diag = jnp.concatenate(y_parts, axis=1)

        s_old = s_ref[d]
        s_bf = s_old.astype(BF16)
        y_off = jnp.concatenate(
            [_dot(cm[:, g * STATE:(g + 1) * STATE], s_bf[:, g * gw:(g + 1) * gw]) for g in range(GROUPS)],
            axis=1)
        y = y_diag + y_off * ecum_e

        xw = (xs * w_e).astype(BF16)
        upd = jnp.concatenate(
            [lax.dot_general(bm[:, g * STATE:(g + 1) * STATE], xw[:, g * gw:(g + 1) * gw],
                             (((0,), (0,)), ((), ())), preferred_element_type=F32) for g in range(GROUPS)],
            axis=1)
        s_ref[d] = s_old * decay_e + upd

        if first:
            y_ref[rows, :] = y + dsk_ref[...] * xs
        else:
            tot = y_ref[rows, :] + y
            zz = z_ref[0, rows, :].astype(F32)
            gz = tot * _silu(zz)
            outs = []
            for g in range(GROUPS):
                gg = gz[:, g * gw:(g + 1) * gw]
                outs.append(gg * lax.rsqrt(jnp.mean(gg * gg, axis=-1, keepdims=True) + NORM_EPS))
            o_ref[0, rows, :] = (jnp.concatenate(outs, axis=1) * g_ref[...]).astype(BF16)

    s_ref[...] = h0_ref[0]

    def first_half(i, carry):
        chunk(i, 0, True)
        chunk(nck - 1 - i, 1, True)
        return carry

    def second_half(i, carry):
        chunk(i, 0, False)
        chunk(nck - 1 - i, 1, False)
        return carry

    lax.fori_loop(0, nck // 2, first_half, 0, unroll=4)
    lax.fori_loop(nck // 2, nck, second_half, 0, unroll=4)


def _ssd(xbc3, z3, dt3, h0, cw, cb, dsk, g, e64, e128):
    bsz = xbc3.shape[0]
    const = lambda *shape: pl.BlockSpec(shape, lambda b: (0,) * len(shape))
    seq = lambda w: pl.BlockSpec((1, SEQ, w), lambda b: (b, 0, 0))
    return pl.pallas_call(
        _ssd_kernel,
        out_shape=jax.ShapeDtypeStruct((bsz, SEQ, D_SSD), BF16),
        grid=(bsz,),
        in_specs=[seq(XBC_W), seq(D_SSD), seq(LANES),
                  pl.BlockSpec((1, 2, STATE, D_SSD), lambda b: (b, 0, 0, 0)),
                  const(3, XBC_W), const(1, XBC_W),
                  const(1, D_SSD), const(1, D_SSD), const(2, 2 * LANES, D_SSD),
                  const(2, 2 * LANES, HEADS * LANES)],
        out_specs=seq(D_SSD),
        scratch_shapes=[pltpu.VMEM((SEQ, XBC_W), BF16), pltpu.VMEM((SEQ, D_SSD), F32),
                        pltpu.VMEM((2, STATE, D_SSD), F32)],
        compiler_params=pltpu.CompilerParams(dimension_semantics=("arbitrary",),
                                             vmem_limit_bytes=VMEM_LIMIT),
        name="ssd",
    )(xbc3, z3, dt3, h0, cw, cb, dsk, g, e64, e128)


def _outproj_kernel(x_ref, yssd_ref, scb_ref, v_ref, vp_ref, vn_ref, mod_ref, scw_ref, wo1_ref, wo2_ref,
                    g2_ref, wrt_ref, br_ref, x1_ref, lg_ref):
    tm = OUT_TILE
    per_b = SEQ // tm
    i = pl.program_id(0)
    first = (i % per_b) == 0
    last = (i % per_b) == per_b - 1
    m = mod_ref[0]
    v = v_ref[...].astype(F32)
    vp = jnp.where(first, 0.0, vp_ref[...].astype(F32))
    vn = jnp.where(last, 0.0, vn_ref[...].astype(F32))
    dn = jnp.concatenate([vp, v[:tm - GRID_W]], axis=0)
    up = jnp.concatenate([v[GRID_W:], vn], axis=0)
    scw = scw_ref[...]
    ysc = scb_ref[...].astype(F32) * (scw[0:1] * dn + scw[1:2] * v + scw[2:3] * up)
    out = _dot(yssd_ref[...], wo1_ref[...]) + _dot(ysc.astype(BF16), wo2_ref[...])
    x1 = x_ref[...] + m[2:3] * out
    x1_ref[...] = x1
    h2 = _rms(x1) * g2_ref[...] * (1.0 + m[4:5]) + m[3:4]
    h_hi = h2.astype(BF16)
    h_lo = (h2 - h_hi.astype(F32)).astype(BF16)
    lg_ref[...] = lax.dot_general(wrt_ref[...], jnp.concatenate([h_hi, h_lo, h_hi], axis=1),
                                  (((1,), (1,)), ((), ())), preferred_element_type=F32) + br_ref[...]


def _outproj(x2, yssd, scb, v, mod3, scw, wo1, wo2, g2, wrt, br):
    t = x2.shape[0]
    tm = OUT_TILE
    per_b = SEQ // tm
    r = tm // GRID_W
    nrow = t // GRID_W
    const = lambda *shape: pl.BlockSpec(shape, lambda i: (0,) * len(shape))
    tile = lambda w: pl.BlockSpec((tm, w), lambda i: (i, 0))
    return pl.pallas_call(
        _outproj_kernel,
        out_shape=(jax.ShapeDtypeStruct((t, D_MODEL), F32), jax.ShapeDtypeStruct((N_EXPERTS, t), F32)),
        grid=(t // tm,),
        in_specs=[tile(D_MODEL), tile(D_SSD), tile(D_SC), tile(D_SC),
                  pl.BlockSpec((GRID_W, D_SC), lambda i: (jnp.maximum(i * r - 1, 0), 0)),
                  pl.BlockSpec((GRID_W, D_SC), lambda i: (jnp.minimum((i + 1) * r, nrow - 1), 0)),
                  pl.BlockSpec((1, N_MOD, D_MODEL), lambda i: (i // per_b, 0, 0)),
                  const(3, D_SC), const(D_SSD, D_MODEL), const(D_SC, D_MODEL), const(1, D_MODEL),
                  const(N_EXPERTS, 3 * D_MODEL), const(N_EXPERTS, 1)],
        out_specs=(tile(D_MODEL), pl.BlockSpec((N_EXPERTS, tm), lambda i: (0, i))),
        compiler_params=pltpu.CompilerParams(dimension_semantics=("arbitrary",),
                                             vmem_limit_bytes=VMEM_LIMIT),
        name="outproj",
    )(x2, yssd, scb, v, v, v, mod3, scw, wo1, wo2, g2, wrt, br)


def _route_kernel(lg_ref, dest_ref, gate_ref, idx_ref, meta_ref, rank_ref, carry_ref, *, n_tok, n_blocks):
    tt = RT_TILE
    ne = N_EXPERTS
    eio = lax.broadcasted_iota(jnp.int32, (ne, tt), 0)
    si = lax.broadcasted_iota(jnp.int32, (tt, tt), 0)
    ti = lax.broadcasted_iota(jnp.int32, (tt, tt), 1)
    before = (si < ti).astype(BF16)
    carry_ref[...] = jnp.zeros_like(carry_ref)

    def tile_body(j, c):
        t0 = pl.multiple_of(j * tt, tt)
        l = lg_ref[:, pl.ds(t0, tt)]
        onehot = jnp.zeros((ne, tt), F32)
        tops, sels = [], []
        for _ in range(TOP_K):
            mx = jnp.max(l, axis=0, keepdims=True)
            idx = jnp.min(jnp.where(l == mx, eio, ne), axis=0, keepdims=True)
            sel = eio == idx
            l = jnp.where(sel, -jnp.inf, l)
            onehot = onehot + sel.astype(F32)
            tops.append(mx)
            sels.append(sel)
            idx_ref[pl.ds(len(tops) - 1, 1), pl.ds(t0, tt)] = idx
        ex = [jnp.exp(tv - tops[0]) for tv in tops]
        den = ex[0] + ex[1] + ex[2] + ex[3]
        prefix = _dot(onehot.astype(BF16), before) + carry_ref[:, 0:1]
        for k in range(TOP_K):
            gate_ref[pl.ds(k, 1), pl.ds(t0, tt)] = ex[k] / den
            rk = jnp.sum(jnp.where(sels[k], prefix, 0.0), axis=0, keepdims=True)
            rank_ref[pl.ds(k, 1), pl.ds(t0, tt)] = rk.astype(jnp.int32)
        carry_ref[...] = carry_ref[...] + jnp.sum(onehot, axis=1, keepdims=True)
        return c

    lax.fori_loop(0, n_tok // tt, tile_body, 0)

    counts = carry_ref[...]
    padded = jnp.floor((counts + (MOE_BM - 1)) * (1.0 / MOE_BM)) * MOE_BM
    er = lax.broadcasted_iota(jnp.int32, (ne, ne), 0)
    ec = lax.broadcasted_iota(jnp.int32, (ne, ne), 1)
    pad_start = jnp.dot((ec < er).astype(F32), padded, precision=HIGHEST, preferred_element_type=F32)
    pad_end = pad_start + padded

    def dest_body(j, c):
        t0 = pl.multiple_of(j * tt, tt)
        for k in range(TOP_K):
            idx = idx_ref[pl.ds(k, 1), pl.ds(t0, tt)]
            base = jnp.sum(jnp.where(eio == idx, pad_start[:, 0:1], 0.0), axis=0, keepdims=True)
            dest_ref[pl.ds(k, 1), pl.ds(t0, tt)] = base.astype(jnp.int32) + rank_ref[pl.ds(k, 1), pl.ds(t0, tt)]
        return c

    lax.fori_loop(0, n_tok // tt, dest_body, 0)

    width = meta_ref.shape[1]
    sub = lax.broadcasted_iota(jnp.int32, (ne, width), 0)
    lan = lax.broadcasted_iota(jnp.int32, (ne, width), 1)
    diag = sub == lan
    cnt_row = jnp.sum(jnp.where(diag, counts[:, 0:1], 0.0), axis=0, keepdims=True)
    start_row = jnp.sum(jnp.where(diag, pad_start[:, 0:1], 0.0), axis=0, keepdims=True)
    blk_start = (lan * MOE_BM).astype(F32)
    blk_exp = jnp.sum((pad_end[:, 0:1] <= blk_start).astype(F32), axis=0, keepdims=True)
    blk_exp = jnp.minimum(blk_exp, float(ne - 1))
    used = jnp.sum(padded[:, 0:1], axis=0, keepdims=True) * (1.0 / MOE_BM)
    meta_ref[0:1, :] = cnt_row.astype(jnp.int32)
    meta_ref[1:2, :] = start_row.astype(jnp.int32)
    meta_ref[2:3, :] = blk_exp.astype(jnp.int32)
    meta_ref[3:4, :] = jnp.broadcast_to(used, (1, width)).astype(jnp.int32)
    meta_ref[4:8, :] = jnp.zeros((4, width), jnp.int32)


def _route(lgt, n_blocks):
    ne, n_tok = lgt.shape
    width = -(-n_blocks // LANES) * LANES
    full = lambda *shape: pl.BlockSpec(shape, lambda: (0,) * len(shape))
    return pl.pallas_call(
        functools.partial(_route_kernel, n_tok=n_tok, n_blocks=n_blocks),
        out_shape=(jax.ShapeDtypeStruct((TOP_K, n_tok), jnp.int32),
                   jax.ShapeDtypeStruct((TOP_K, n_tok), F32),
                   jax.ShapeDtypeStruct((TOP_K, n_tok), jnp.int32),
                   jax.ShapeDtypeStruct((8, width), jnp.int32)),
        in_specs=[full(ne, n_tok)],
        out_specs=(full(TOP_K, n_tok), full(TOP_K, n_tok), full(TOP_K, n_tok), full(8, width)),
        scratch_shapes=[pltpu.VMEM((TOP_K, n_tok), jnp.int32), pltpu.VMEM((ne, LANES), F32)],
        compiler_params=pltpu.CompilerParams(vmem_limit_bytes=VMEM_LIMIT),
        name="route",
    )(lgt)


def _dispatch_kernel(dest_ref, cnt_ref, start_ref, nu_ref, x1_ref, meta_ref, mod_ref, g2_ref, zsrc_ref, xs_ref,
                     hbuf, sem, zsem):
    i = pl.program_id(0)
    n = pl.num_programs(0)
    tl = DISP_TILE
    slot = i % 2
    nb = xs_ref.shape[0] // MOE_BM

    def zero_block(b):
        return pltpu.make_async_copy(zsrc_ref, xs_ref.at[pl.ds(b * MOE_BM, MOE_BM)], zsem)

    @pl.when(i == 0)
    def _():
        def zero_piece(row0):
            return pltpu.make_async_copy(zsrc_ref.at[pl.ds(0, ZERO_PIECE)], xs_ref.at[pl.ds(row0, ZERO_PIECE)],
                                         zsem)

        def per_expert(e, started):
            rem = cnt_ref[e] & (MOE_BM - 1)
            base = start_ref[e] + cnt_ref[e] - rem
            first = jnp.where(rem > 0, rem // ZERO_PIECE, MOE_BM // ZERO_PIECE)

            def piece(q, c2):
                if started:
                    zero_piece(0).wait()
                else:
                    zero_piece(base + q * ZERO_PIECE).start()
                return c2

            lax.fori_loop(first, MOE_BM // ZERO_PIECE, piece, 0)

        def start_e(e, c):
            per_expert(e, False)
            return c

        def wait_e(e, c):
            per_expert(e, True)
            return c

        def start_t(b, c):
            zero_block(b).start()
            return c

        def wait_t(b, c):
            zero_block(0).wait()
            return c

        lax.fori_loop(0, N_EXPERTS, start_e, 0)
        lax.fori_loop(nu_ref[0], nb, start_t, 0)
        lax.fori_loop(0, N_EXPERTS, wait_e, 0)
        lax.fori_loop(nu_ref[0], nb, wait_t, 0)

    m = mod_ref[0]
    h2 = _rms(x1_ref[...]) * g2_ref[...] * (1.0 + m[4:5]) + m[3:4]
    lo = pltpu.bitcast(h2[:, :PACK_W].astype(BF16).astype(F32), jnp.uint32) >> 16
    hi = pltpu.bitcast(h2[:, PACK_W:].astype(BF16).astype(F32), jnp.uint32) & jnp.uint32(0xFFFF0000)
    row = jnp.concatenate([lo | hi, meta_ref[...], jnp.zeros((tl, D_MODEL - PACK_W - LANES), jnp.uint32)], axis=1)
    hbuf[slot] = row.reshape(tl, SUBLANES, LANES)

    def per_tok(t, c):
        tok = i * tl + t
        for k in range(TOP_K):
            d = dest_ref[k * (dest_ref.shape[0] // TOP_K) + tok]
            pltpu.make_async_copy(hbuf.at[slot, t], xs_ref.at[d], sem.at[slot]).start(priority=k % 2)
        return c

    lax.fori_loop(0, tl, per_tok, 0, unroll=8)

    def wait_slot(sl):
        for _ in range(TOP_K):
            pltpu.make_async_copy(hbuf.at[sl], xs_ref.at[pl.ds(0, tl)], sem.at[sl]).wait()

    @pl.when(i > 0)
    def _():
        wait_slot(1 - slot)

    @pl.when(i == n - 1)
    def _():
        wait_slot(slot)


def _dispatch(dest_flat, cnt, start, n_used, x1, meta_rows, mod3, g2, zsrc, n_rows):
    n_tok = x1.shape[0]
    tl = DISP_TILE
    per_b = SEQ // tl
    return pl.pallas_call(
        _dispatch_kernel,
        out_shape=jax.ShapeDtypeStruct((n_rows, SUBLANES, LANES), jnp.uint32),
        grid_spec=pltpu.PrefetchScalarGridSpec(
            num_scalar_prefetch=4,
            grid=(n_tok // tl,),
            in_specs=[pl.BlockSpec((tl, D_MODEL), lambda i, *_: (i, 0)),
                      pl.BlockSpec((tl, LANES), lambda i, *_: (i, 0)),
                      pl.BlockSpec((1, N_MOD, D_MODEL), lambda i, *_: (i // per_b, 0, 0)),
                      pl.BlockSpec((1, D_MODEL), lambda i, *_: (0, 0)),
                      pl.BlockSpec((MOE_BM, SUBLANES, LANES), lambda i, *_: (0, 0, 0))],
            out_specs=pl.BlockSpec(memory_space=pl.ANY),
            scratch_shapes=[pltpu.VMEM((2, tl, SUBLANES, LANES), jnp.uint32),
                            pltpu.SemaphoreType.DMA((2,)), pltpu.SemaphoreType.DMA]),
        compiler_params=pltpu.CompilerParams(dimension_semantics=("arbitrary",),
                                             vmem_limit_bytes=VMEM_LIMIT),
        name="dispatch",
    )(dest_flat, cnt, start, n_used, x1, meta_rows, mod3, g2, zsrc)


def _expert_rows_kernel(be_ref, nu_ref, xs_ref, wgu_hbm, bgu_ref, wd_hbm, bd_ref, ys_ref,
                        wgu_raw, wd_raw, wgu_bf, wd_bf, w_sem):
    j = pl.program_id(0)
    nu = nu_ref[0]
    n_lt = D_MODEL // LANES

    def weight_copies(e):
        return (pltpu.make_async_copy(wgu_hbm.at[e], wgu_raw, w_sem.at[0]),
                pltpu.make_async_copy(wd_hbm.at[e], wd_raw, w_sem.at[1]))

    @pl.when(j == 0)
    def _():
        for w in weight_copies(be_ref[0]):
            w.start(priority=1)

    e_now = be_ref[j]
    new_expert = jnp.logical_or(j == 0, be_ref[jnp.maximum(j - 1, 0)] != e_now)

    @pl.when(jnp.logical_and(j < nu, new_expert))
    def _():
        for w in weight_copies(e_now):
            w.wait()
        n_cc = 8
        for c in range(n_cc):
            cc = slice(c * (2 * D_FF // n_cc), (c + 1) * (2 * D_FF // n_cc))
            wgu_bf[:, cc] = wgu_raw[:, cc].astype(BF16)
        for c in range(n_cc // 2):
            cc = slice(c * (2 * D_MODEL // n_cc), (c + 1) * (2 * D_MODEL // n_cc))
            wd_bf[:, cc] = wd_raw[:, cc].astype(BF16)
        j_next = lax.while_loop(lambda t: jnp.logical_and(t < nu, be_ref[jnp.minimum(t, pl.num_programs(0) - 1)] == e_now),
                                lambda t: t + 1, j + 1)

        @pl.when(j_next < nu)
        def _():
            for w in weight_copies(be_ref[jnp.minimum(j_next, pl.num_programs(0) - 1)]):
                w.start(priority=1)

    @pl.when(j < nu)
    def _():
        e_f = e_now.astype(F32)
        pr = MOE_BM // MOE_PARTS
        for p in range(MOE_PARTS):
            words = xs_ref[pl.ds(p * pr, pr)].reshape(pr, D_MODEL)
            packed = words[:, 0:PACK_W]
            meta = pltpu.bitcast(words[:, PACK_W:PACK_W + LANES], F32)
            xb = jnp.concatenate(
                [pltpu.bitcast(packed << 16, F32).astype(BF16),
                 pltpu.bitcast(packed & jnp.uint32(0xFFFF0000), F32).astype(BF16)], axis=1)
            gate = jnp.zeros((pr, 1), F32)
            for k in range(TOP_K):
                mk = meta[:, META_IDX + k:META_IDX + k + 1] == e_f
                gate = gate + jnp.where(mk, meta[:, META_GATE + k:META_GATE + k + 1], 0.0)
            gu = _dot(xb, wgu_bf[...]) + bgu_ref[0]
            glu = jnp.minimum(gu[:, :D_FF], SWIGLU_LIMIT)
            lin = jnp.clip(gu[:, D_FF:], -SWIGLU_LIMIT, SWIGLU_LIMIT)
            act = glu * jax.nn.sigmoid(SWIGLU_ALPHA * glu) * (lin + 1.0)
            y = (_dot(act.astype(BF16), wd_bf[...]) + bd_ref[0]) * gate
            ys_ref[pl.ds(p * pr, pr)] = y.reshape(pr, n_lt, LANES)

    @pl.when(j >= nu)
    def _():
        ys_ref[...] = jnp.zeros_like(ys_ref)


def _expert_rows(blk_exp, n_used, xs, wgu, bgu, wd, bd):
    n_rows = xs.shape[0]
    nb = n_rows // MOE_BM
    n_lt = D_MODEL // LANES
    row_blk = lambda j, be, nu: (jnp.minimum(j, nu[0] - 1), 0, 0)
    per_e = lambda j, be, nu: (be[j], 0, 0)
    return pl.pallas_call(
        _expert_rows_kernel,
        out_shape=jax.ShapeDtypeStruct((n_rows, n_lt, LANES), F32),
        grid_spec=pltpu.PrefetchScalarGridSpec(
            num_scalar_prefetch=2,
            grid=(nb,),
            in_specs=[pl.BlockSpec((MOE_BM, SUBLANES, LANES), row_blk),
                      pl.BlockSpec(memory_space=pl.ANY),
                      pl.BlockSpec((1, 1, 2 * D_FF), per_e),
                      pl.BlockSpec(memory_space=pl.ANY),
                      pl.BlockSpec((1, 1, D_MODEL), per_e)],
            out_specs=pl.BlockSpec((MOE_BM, n_lt, LANES), lambda j, be, nu: (j, 0, 0)),
            scratch_shapes=[pltpu.VMEM((D_MODEL, 2 * D_FF), F32), pltpu.VMEM((D_FF, D_MODEL), F32),
                            pltpu.VMEM((D_MODEL, 2 * D_FF), BF16), pltpu.VMEM((D_FF, D_MODEL), BF16),
                            pltpu.SemaphoreType.DMA((2,))]),
        compiler_params=pltpu.CompilerParams(dimension_semantics=("arbitrary",),
                                             vmem_limit_bytes=VMEM_LIMIT),
        name="experts",
    )(blk_exp, n_used, xs, wgu, bgu, wd, bd)


def _gather_combine_kernel(dest_ref, ys_ref, x1_ref, mod_ref, fg_ref, o_ref, buf, sem):
    i = pl.program_id(0)
    n = pl.num_programs(0)
    tc = GATHER_TILE
    slot = i % 2

    def issue(tile, sl):
        def per_tok(t, c):
            tok = tile * tc + t
            for k in range(TOP_K):
                d = dest_ref[k * (dest_ref.shape[0] // TOP_K) + tok]
                pltpu.make_async_copy(ys_ref.at[d], buf.at[sl, k * tc + t], sem.at[sl]).start(priority=k % 2)
            return c

        lax.fori_loop(0, tc, per_tok, 0, unroll=8)

    @pl.when(i == 0)
    def _():
        issue(0, 0)

    @pl.when(i + 1 < n)
    def _():
        issue(i + 1, 1 - slot)

    for _ in range(TOP_K):
        pltpu.make_async_copy(ys_ref.at[pl.ds(0, tc)], buf.at[slot, pl.ds(0, tc)], sem.at[slot]).wait()

    moe = ((buf[slot, pl.ds(0, tc)] + buf[slot, pl.ds(tc, tc)])
           + (buf[slot, pl.ds(2 * tc, tc)] + buf[slot, pl.ds(3 * tc, tc)])).reshape(tc, D_MODEL)
    m = mod_ref[0]
    x2 = x1_ref[...] + m[5:6] * moe
    o_ref[...] = _rms(x2) * fg_ref[...]


def _gather_combine(dest_flat, ys, x1, mod3, fg):
    n_tok = x1.shape[0]
    tc = GATHER_TILE
    per_b = SEQ // tc
    n_lt = D_MODEL // LANES
    return pl.pallas_call(
        _gather_combine_kernel,
        out_shape=jax.ShapeDtypeStruct((n_tok, D_MODEL), F32),
        grid_spec=pltpu.PrefetchScalarGridSpec(
            num_scalar_prefetch=1,
            grid=(n_tok // tc,),
            in_specs=[pl.BlockSpec(memory_space=pl.ANY),
                      pl.BlockSpec((tc, D_MODEL), lambda i, d: (i, 0)),
                      pl.BlockSpec((1, N_MOD, D_MODEL), lambda i, d: (i // per_b, 0, 0)),
                      pl.BlockSpec((1, D_MODEL), lambda i, d: (0, 0))],
            out_specs=pl.BlockSpec((tc, D_MODEL), lambda i, d: (i, 0)),
            scratch_shapes=[pltpu.VMEM((2, TOP_K * tc, n_lt, LANES), F32), pltpu.SemaphoreType.DMA((2,))]),
        compiler_params=pltpu.CompilerParams(dimension_semantics=("arbitrary",),
                                             vmem_limit_bytes=VMEM_LIMIT),
        name="combine",
    )(dest_flat, ys, x1, mod3, fg)


def _expansion_matrices(src0):
    r = (jnp.arange(2 * LANES) % LANES)[:, None]
    out64, out128 = [], []
    for d in range(2):
        l64 = jnp.arange(D_SSD)[None, :]
        l128 = jnp.arange(HEADS * LANES)[None, :]
        out64.append((l64 // HEAD_DIM == r - src0 - HEADS * d).astype(BF16))
        out128.append((l128 // LANES == r - src0 - HEADS * d).astype(BF16))
    return jnp.stack(out64), jnp.stack(out128)


def _pad_lanes(v):
    return jnp.pad(v, [(0, 0)] * (v.ndim - 1) + [(0, LANES - v.shape[-1])])


def kernel(x, c, ctx, c_ctx, w_mod, b_mod, norm1_g, w_in, ssd_conv_w, ssd_conv_b, ssd_dt_bias, ssd_a_log,
           ssd_d, ssd_norm_g, sc_conv_w, w_out, norm2_g, w_router, b_router, w_gate_up, b_gate_up, w_down,
           b_down, final_g):
    bsz = x.shape[0]
    n_tok = bsz * SEQ
    n_assign = n_tok * TOP_K
    n_blocks = n_assign // MOE_BM + N_EXPERTS
    n_rows = n_blocks * MOE_BM
    li = 0

    cvec = jnp.concatenate([c, c_ctx[None, :], jnp.zeros((7, D_MODEL), F32)], axis=0)
    mod3 = _mod(cvec, w_mod, b_mod[li][None, :], li).reshape(bsz + 8, N_MOD, D_MODEL)

    w = w_in[li]
    wz = w[:, Z0:X0].astype(BF16)
    wxbc = w[:, X0:DT0].astype(BF16)
    wdt = _pad_lanes(w[:, DT0:SC0]).astype(BF16)
    wb = w[:, SC0:SC0 + D_SC].astype(BF16)
    wc = w[:, SC0 + D_SC:SC0 + 2 * D_SC].astype(BF16)
    wu = w[:, SC0 + 2 * D_SC:].astype(BF16)
    g1 = norm1_g[li][None, :]
    cw = ssd_conv_w[li]
    cb = ssd_conv_b[li][None, :]
    dtb = _pad_lanes(ssd_dt_bias[li].reshape(1, 2 * HEADS))
    alog = _pad_lanes(ssd_a_log[li].reshape(1, 2 * HEADS))
    e64_ctx, _ = _expansion_matrices(0)

    h0 = _ctx_states(ctx, mod3, g1, wxbc[:, :XB_W], wdt, cw[:, :XB_W], cb[:, :XB_W], dtb, alog, e64_ctx)

    rep = lambda a: _pad_lanes(jnp.tile(a[..., :2 * HEADS], (1, DT_COPIES)))
    e64, e128 = _expansion_matrices(DT_DA0)
    x2 = x.reshape(n_tok, D_MODEL)
    z, xbc, dtp, scb, v = _inproj(x2, mod3, g1, wz, wxbc, rep(wdt), rep(dtb), rep(alog), wb, wc, wu)

    dsk = jnp.repeat(ssd_d[li], HEAD_DIM)[None, :]
    yssd = _ssd(xbc.reshape(bsz, SEQ, XBC_W), z.reshape(bsz, SEQ, D_SSD), dtp.reshape(bsz, SEQ, LANES), h0,
                cw, cb, dsk, ssd_norm_g[li][None, :], e64, e128)

    wo = w_out[li].astype(BF16)
    g2 = norm2_g[li][None, :]
    wr = w_router[li].T
    wr_hi = wr.astype(BF16)
    wr_lo = (wr - wr_hi.astype(F32)).astype(BF16)
    x1, lgt = _outproj(x2, yssd.reshape(n_tok, D_SSD), scb, v, mod3, sc_conv_w[li], wo[:D_SSD], wo[D_SSD:],
                       g2, jnp.concatenate([wr_hi, wr_hi, wr_lo], axis=1), b_router[li][:, None])

    dest_t, gate_t, idx_t, meta = _route(lgt, n_blocks)
    dest_flat = dest_t.reshape(n_assign)
    cnt = meta[0, :N_EXPERTS]
    start = meta[1, :N_EXPERTS]
    blk_exp = meta[2, :n_blocks]
    n_used = meta[3, :1]

    meta_rows = lax.bitcast_convert_type(_pad_lanes(jnp.concatenate(
        [idx_t.T.astype(F32), gate_t.T], axis=1)), jnp.uint32)
    pad_meta = lax.bitcast_convert_type(_pad_lanes(jnp.concatenate(
        [jnp.full((MOE_BM, TOP_K), -1.0, F32), jnp.zeros((MOE_BM, TOP_K), F32)], axis=1)), jnp.uint32)
    zsrc = jnp.concatenate([jnp.zeros((MOE_BM, PACK_W), jnp.uint32), pad_meta,
                            jnp.zeros((MOE_BM, D_MODEL - PACK_W - LANES), jnp.uint32)],
                           axis=1).reshape(MOE_BM, SUBLANES, LANES)

    xs = _dispatch(dest_flat, cnt, start, n_used, x1, meta_rows, mod3, g2, zsrc, n_rows)
    ys = _expert_rows(blk_exp, n_used, xs, w_gate_up[li], b_gate_up[li][:, None, :],
                      w_down[li], b_down[li][:, None, :])
    out = _gather_combine(dest_flat, ys, x1, mod3, final_g[None, :])
    return out.reshape(bsz, SEQ, D_MODEL)
```

```python
import functools

import jax
import jax.numpy as jnp
from jax import lax
from jax.experimental import pallas as pl
from jax.experimental.pallas import tpu as pltpu

F32 = jnp.float32
BF16 = jnp.bfloat16
HIGHEST = lax.Precision.HIGHEST

D_MODEL = 1024
SEQ = 2048
CTX_LEN = 256
GRID_W = 64
D_SSD = 1024
D_SC = 1024
HEAD_DIM = 64
HEADS = 16
GROUPS = 2
STATE = 128
CHUNK = 128
N_EXPERTS = 32
TOP_K = 4
D_FF = 1024
SWIGLU_LIMIT = 7.0
SWIGLU_ALPHA = 1.702
NORM_EPS = 1e-6
N_MOD = 6
XBC_W = D_SSD + 2 * GROUPS * STATE
XB_W = D_SSD + GROUPS * STATE
LANES = 128

Z0 = 0
X0 = Z0 + D_SSD
B0 = X0 + D_SSD
C0 = B0 + GROUPS * STATE
DT0 = C0 + GROUPS * STATE
SC0 = DT0 + 2 * HEADS

TOK_TILE = 1024
OUT_TILE = 1024
MOE_BM = 512
MOE_PARTS = 2
ZERO_PIECE = 128
RT_TILE = 512
CTX_BATCH = 4
DISP_TILE = 512
GATHER_TILE = 256
DT_COPIES = 3
DT_LOG0 = 2 * HEADS
DT_DA0 = 4 * HEADS
SUBLANES = 8
PACK_W = D_MODEL // 2
META_IDX = 0
META_GATE = TOP_K
VMEM_LIMIT = 56 * 1024 * 1024


def _silu(v):
    return v * jax.nn.sigmoid(v)


def _softplus(v):
    return jnp.maximum(v, 0.0) + jnp.log1p(jnp.exp(-jnp.abs(v)))


def _rms(v):
    return v * lax.rsqrt(jnp.mean(v * v, axis=-1, keepdims=True) + NORM_EPS)


def _dot(a, b):
    return jnp.dot(a, b, preferred_element_type=F32)


def _expand2(v, e2):
    hi = v.astype(BF16)
    lo = (v - hi.astype(F32)).astype(BF16)
    return _dot(jnp.concatenate([hi, lo], axis=1), e2)


def _mod_kernel(c_ref, w_ref, b_ref, o_ref):
    o_ref[...] = jnp.dot(_silu(c_ref[...]), w_ref[0], precision=HIGHEST,
                         preferred_element_type=F32) + b_ref[...]


def _mod(cvec, w_mod, b_mod, layer):
    rows = cvec.shape[0]
    n = w_mod.shape[2]
    tn = 1536
    return pl.pallas_call(
        _mod_kernel,
        out_shape=jax.ShapeDtypeStruct((rows, n), F32),
        grid=(n // tn,),
        in_specs=[pl.BlockSpec((rows, D_MODEL), lambda j: (0, 0)),
                  pl.BlockSpec((1, D_MODEL, tn), lambda j: (layer, 0, j)),
                  pl.BlockSpec((1, tn), lambda j: (0, j))],
        out_specs=pl.BlockSpec((rows, tn), lambda j: (0, j)),
        compiler_params=pltpu.CompilerParams(dimension_semantics=("arbitrary",),
                                             vmem_limit_bytes=VMEM_LIMIT),
        name="mod",
    )(cvec, w_mod, b_mod)


def _ctx_kernel(ctx_ref, mod_ref, g1_ref, wxb_ref, wdt_ref, cw_ref, cb_ref, dtb_ref, alog_ref, e64_ref,
                h0_ref):
    L = CTX_LEN
    nb = ctx_ref.shape[0]
    m = mod_ref[0]
    hc = _rms(ctx_ref[...].reshape(nb * L, D_MODEL)) * g1_ref[...] * (1.0 + m[1:2]) + m[0:1]
    hb = hc.astype(BF16)
    pxb = _dot(hb, wxb_ref[...])
    dtr = _dot(hb, wdt_ref[...])
    rowl = lax.broadcasted_iota(jnp.int32, (nb * L, XB_W), 0) & (L - 1)
    dn = jnp.where(rowl == 0, 0.0, pltpu.roll(pxb, 1, 0))
    up = jnp.where(rowl == L - 1, 0.0, pltpu.roll(pxb, nb * L - 1, 0))
    cw = cw_ref[...]
    xb = _silu(cw[0:1] * dn + cw[1:2] * pxb + cw[2:3] * up + cb_ref[...])
    dt_all = _softplus(dtr + dtb_ref[...])
    da_all = dt_all * (-jnp.exp(alog_ref[...]))
    ri = lax.broadcasted_iota(jnp.int32, (L, L), 0)
    ci = lax.broadcasted_iota(jnp.int32, (L, L), 1)
    for bi in range(nb):
        rs = slice(bi * L, (bi + 1) * L)
        xs = xb[rs, :D_SSD]
        bm = xb[rs, D_SSD:].astype(BF16)
        dt = dt_all[rs]
        da = da_all[rs]
        for d in range(2):
            tri = (ci <= ri) if d == 0 else (ci >= ri)
            cum = jnp.dot(tri.astype(F32), da, precision=HIGHEST, preferred_element_type=F32)
            last = cum[L - 1:L] if d == 0 else cum[0:1]
            w_e = _expand2(jnp.exp(last - cum) * dt, e64_ref[d])
            xw = (xs * w_e).astype(BF16)
            for g in range(GROUPS):
                gw = D_SSD // GROUPS
                st = lax.dot_general(bm[:, g * STATE:(g + 1) * STATE], xw[:, g * gw:(g + 1) * gw],
                                     (((0,), (0,)), ((), ())), preferred_element_type=F32)
                h0_ref[bi, d, :, g * gw:(g + 1) * gw] = st


def _ctx_states(ctx, mod3, g1, wxb, wdt, cw, cb, dtb, alog, e64):
    bsz = ctx.shape[0]
    mod_row = bsz
    nb = CTX_BATCH if bsz % CTX_BATCH == 0 else 1
    const = lambda *shape: pl.BlockSpec(shape, lambda b: (0,) * len(shape))
    return pl.pallas_call(
        _ctx_kernel,
        out_shape=jax.ShapeDtypeStruct((bsz, 2, STATE, D_SSD), F32),
        grid=(bsz // nb,),
        in_specs=[pl.BlockSpec((nb, CTX_LEN, D_MODEL), lambda b: (b, 0, 0)),
                  pl.BlockSpec((1, N_MOD, D_MODEL), lambda b: (mod_row, 0, 0)),
                  const(1, D_MODEL), const(D_MODEL, XB_W), const(D_MODEL, LANES),
                  const(3, XB_W), const(1, XB_W), const(1, LANES), const(1, LANES),
                  const(2, 2 * LANES, D_SSD)],
        out_specs=pl.BlockSpec((nb, 2, STATE, D_SSD), lambda b: (b, 0, 0, 0)),
        compiler_params=pltpu.CompilerParams(dimension_semantics=("arbitrary",),
                                             vmem_limit_bytes=VMEM_LIMIT),
        name="ctx_states",
    )(ctx, mod3, g1, wxb, wdt, cw, cb, dtb, alog, e64)


def _inproj_kernel(x_ref, mod_ref, g1_ref, wz_ref, wxbc_ref, wdt_ref, dtb_ref, alog_ref, wb_ref, wc_ref, wu_ref,
                   z_ref, xbc_ref, dt_ref, scb_ref, v_ref):
    m = mod_ref[0]
    hx = _rms(x_ref[...]) * g1_ref[...] * (1.0 + m[1:2]) + m[0:1]
    hb = hx.astype(BF16)
    z_ref[...] = _dot(hb, wz_ref[...]).astype(BF16)
    xbc_ref[...] = _dot(hb, wxbc_ref[...]).astype(BF16)
    dt = _softplus(_dot(hb, wdt_ref[...]) + dtb_ref[...])
    lane = lax.broadcasted_iota(jnp.int32, dt.shape, 1)
    dt_ref[...] = jnp.where(lane < DT_LOG0, dt,
                            jnp.where(lane < DT_DA0, jnp.log(dt), dt * (-jnp.exp(alog_ref[...]))))
    scb_ref[...] = _dot(hb, wb_ref[...]).astype(BF16)
    v_ref[...] = (_dot(hb, wc_ref[...]) * _dot(hb, wu_ref[...])).astype(BF16)


def _inproj(x2, mod3, g1, wz, wxbc, wdt, dtb, alog, wb, wc, wu):
    t = x2.shape[0]
    tm = TOK_TILE
    per_b = SEQ // tm
    const = lambda *shape: pl.BlockSpec(shape, lambda i: (0,) * len(shape), pipeline_mode=pl.Buffered(1))
    tile = lambda w: pl.BlockSpec((tm, w), lambda i: (i, 0))
    return pl.pallas_call(
        _inproj_kernel,
        out_shape=(jax.ShapeDtypeStruct((t, D_SSD), BF16), jax.ShapeDtypeStruct((t, XBC_W), BF16),
                   jax.ShapeDtypeStruct((t, LANES), F32), jax.ShapeDtypeStruct((t, D_SC), BF16),
                   jax.ShapeDtypeStruct((t, D_SC), BF16)),
        grid=(t // tm,),
        in_specs=[tile(D_MODEL),
                  pl.BlockSpec((1, N_MOD, D_MODEL), lambda i: (i // per_b, 0, 0)),
                  const(1, D_MODEL), const(D_MODEL, D_SSD), const(D_MODEL, XBC_W), const(D_MODEL, LANES),
                  const(1, LANES), const(1, LANES),
                  const(D_MODEL, D_SC), const(D_MODEL, D_SC), const(D_MODEL, D_SC)],
        out_specs=(tile(D_SSD), tile(XBC_W), tile(LANES), tile(D_SC), tile(D_SC)),
        compiler_params=pltpu.CompilerParams(dimension_semantics=("arbitrary",),
                                             vmem_limit_bytes=VMEM_LIMIT),
        name="inproj",
    )(x2, mod3, g1, wz, wxbc, wdt, dtb, alog, wb, wc, wu)


def _ssd_kernel(xbc_ref, z_ref, dt_ref, h0_ref, cw_ref, cb_ref, dsk_ref, g_ref,
                e64_ref, e128_ref, o_ref, xc_ref, y_ref, s_ref):
    Q = CHUNK
    nck = SEQ // Q
    gw = D_SSD // GROUPS

    rowi = lax.broadcasted_iota(jnp.int32, (SUBLANES, XBC_W), 0)

    def conv_body(c, carry):
        r0 = pl.multiple_of(c * Q, Q)
        main = xbc_ref[0, pl.ds(r0, Q), :].astype(F32)
        pstart = pl.multiple_of(jnp.maximum(r0 - 16, 0), 16)
        nstart = pl.multiple_of(jnp.minimum(r0 + Q, SEQ - 16), 16)
        prev = xbc_ref[0, pl.ds(pstart, 16), :].astype(F32)[15:16]
        nxt = xbc_ref[0, pl.ds(nstart, 16), :].astype(F32)[0:1]
        prev = jnp.where(c > 0, prev, 0.0)
        nxt = jnp.where(c < nck - 1, nxt, 0.0)
        dn = pltpu.roll(main, 1, 0)
        up = pltpu.roll(main, Q - 1, 0)
        dn = jnp.concatenate([jnp.where(rowi == 0, prev, dn[0:SUBLANES]), dn[SUBLANES:]], axis=0)
        up = jnp.concatenate([up[:Q - SUBLANES], jnp.where(rowi == SUBLANES - 1, nxt, up[Q - SUBLANES:])], axis=0)
        cw = cw_ref[...]
        conv = cw[0:1] * dn + cw[1:2] * main + cw[2:3] * up + cb_ref[...]
        xc_ref[pl.ds(r0, Q), :] = _silu(conv).astype(BF16)
        return carry

    lax.fori_loop(0, nck, conv_body, 0)

    ri = lax.broadcasted_iota(jnp.int32, (Q, Q), 0)
    ci = lax.broadcasted_iota(jnp.int32, (Q, Q), 1)
    lane = lax.broadcasted_iota(jnp.int32, (Q, LANES), 1)
    da_lanes = jnp.logical_and(lane >= DT_DA0, lane < DT_DA0 + 2 * HEADS)

    def chunk(c, d, first):
        r0 = pl.multiple_of(c * Q, Q)
        rows = pl.ds(r0, Q)
        xs_b = xc_ref[rows, 0:D_SSD]
        xs = xs_b.astype(F32)
        bm = xc_ref[rows, D_SSD:D_SSD + GROUPS * STATE]
        cm = xc_ref[rows, D_SSD + GROUPS * STATE:XBC_W]
        dtp = dt_ref[0, rows, :]
        da = jnp.where(da_lanes, dtp, 0.0)
        dt = pltpu.roll(dtp, DT_DA0, 1)
        log_dt = pltpu.roll(dtp, DT_DA0 - DT_LOG0, 1)
        tri = (ci <= ri) if d == 0 else (ci >= ri)
        p0 = da.astype(BF16)
        r1 = da - p0.astype(F32)
        p1 = r1.astype(BF16)
        p2 = (r1 - p1.astype(F32)).astype(BF16)
        tri_b = jnp.where(tri, 1.0, 0.0).astype(BF16)
        cum = _dot(jnp.concatenate([tri_b, tri_b, tri_b], axis=1),
                   jnp.concatenate([p0, p1, p2], axis=0))
        sub_t = (cum - log_dt).T
        last = cum[Q - 1:Q] if d == 0 else cum[0:1]
        ecum_e = _expand2(jnp.exp(cum), e64_ref[d])
        w_e = _expand2(jnp.where(da_lanes, jnp.exp(last - cum) * dt, 0.0), e64_ref[d])
        colb = _expand2(cum, e128_ref[d])
        decay_e = ecum_e[Q - 1:Q] if d == 0 else ecum_e[0:1]

        gmat = [lax.dot_general(cm[:, g * STATE:(g + 1) * STATE], bm[:, g * STATE:(g + 1) * STATE],
                                (((1,), (1,)), ((), ())), preferred_element_type=F32)
                for g in range(GROUPS)]
        zero_b = jnp.zeros((Q, LANES), BF16)
        y_parts = []
        for p in range(HEADS // 2):
            g = (2 * p) // (HEADS // GROUPS)
            ms = []
            for hh in (2 * p, 2 * p + 1):
                src = DT_DA0 + HEADS * d + hh
                seg = colb[:, hh * LANES:(hh + 1) * LANES] - sub_t[src:src + 1, :]
                ms.append((jnp.where(tri, jnp.exp(seg), 0.0) * gmat[g]).astype(BF16))
            mcat = jnp.concatenate(ms, axis=1)
            xp = xs_b[:, p * LANES:(p + 1) * LANES]
            rhs = jnp.concatenate([jnp.where(lane < HEAD_DIM, xp, zero_b),
                                   jnp.where(lane >= HEAD_DIM, xp, zero_b)], axis=0)
            y_parts.append(_dot(mcat, rhs))
        y_diag = jnp.concatenate(y_parts, axis=1)

        s_old = s_ref[d]
        s_bf = s_old.astype(BF16)
        y_off = jnp.concatenate(
            [_dot(cm[:, g * STATE:(g + 1) * STATE], s_bf[:, g * gw:(g + 1) * gw]) for g in range(GROUPS)],
            axis=1)
        y = y_diag + y_off * ecum_e

        xw = (xs * w_e).astype(BF16)
        upd = jnp.concatenate(
            [lax.dot_general(bm[:, g * STATE:(g + 1) * STATE], xw[:, g * gw:(g + 1) * gw],
                             (((0,), (0,)), ((), ())), preferred_element_type=F32) for g in range(GROUPS)],
            axis=1)
        s_ref[d] = s_old * decay_e + upd

        if first:
            y_ref[rows, :] = y + dsk_ref[...] * xs
        else:
            tot = y_ref[rows, :] + y
            zz = z_ref[0, rows, :].astype(F32)
            gz = tot * _silu(zz)
            outs = []
            for g in range(GROUPS):
                gg = gz[:, g * gw:(g + 1) * gw]
                outs.append(gg * lax.rsqrt(jnp.mean(gg * gg, axis=-1, keepdims=True) + NORM_EPS))
            o_ref[0, rows, :] = (jnp.concatenate(outs, axis=1) * g_ref[...]).astype(BF16)

    s_ref[...] = h0_ref[0]

    def first_half(i, carry):
        chunk(i, 0, True)
        chunk(nck - 1 - i, 1, True)
        return carry

    def second_half(i, carry):
        chunk(i, 0, False)
        chunk(nck - 1 - i, 1, False)
        return carry

    lax.fori_loop(0, nck // 2, first_half, 0, unroll=4)
    lax.fori_loop(nck // 2, nck, second_half, 0, unroll=4)


def _ssd(xbc3, z3, dt3, h0, cw, cb, dsk, g, e64, e128):
    bsz = xbc3.shape[0]
    const = lambda *shape: pl.BlockSpec(shape, lambda b: (0,) * len(shape))
    seq = lambda w: pl.BlockSpec((1, SEQ, w), lambda b: (b, 0, 0))
    return pl.pallas_call(
        _ssd_kernel,
        out_shape=jax.ShapeDtypeStruct((bsz, SEQ, D_SSD), BF16),
        grid=(bsz,),
        in_specs=[seq(XBC_W), seq(D_SSD), seq(LANES),
                  pl.BlockSpec((1, 2, STATE, D_SSD), lambda b: (b, 0, 0, 0)),
                  const(3, XBC_W), const(1, XBC_W),
                  const(1, D_SSD), const(1, D_SSD), const(2, 2 * LANES, D_SSD),
                  const(2, 2 * LANES, HEADS * LANES)],
        out_specs=seq(D_SSD),
        scratch_shapes=[pltpu.VMEM((SEQ, XBC_W), BF16), pltpu.VMEM((SEQ, D_SSD), F32),
                        pltpu.VMEM((2, STATE, D_SSD), F32)],
        compiler_params=pltpu.CompilerParams(dimension_semantics=("arbitrary",),
                                             vmem_limit_bytes=VMEM_LIMIT),
        name="ssd",
    )(xbc3, z3, dt3, h0, cw, cb, dsk, g, e64, e128)


def _outproj_kernel(x_ref, yssd_ref, scb_ref, v_ref, vp_ref, vn_ref, mod_ref, scw_ref, wo1_ref, wo2_ref,
                    g2_ref, wrt_ref, br_ref, x1_ref, lg_ref):
    tm = OUT_TILE
    per_b = SEQ // tm
    i = pl.program_id(0)
    first = (i % per_b) == 0
    last = (i % per_b) == per_b - 1
    m = mod_ref[0]
    v = v_ref[...].astype(F32)
    vp = jnp.where(first, 0.0, vp_ref[...].astype(F32))
    vn = jnp.where(last, 0.0, vn_ref[...].astype(F32))
    dn = jnp.concatenate([vp, v[:tm - GRID_W]], axis=0)
    up = jnp.concatenate([v[GRID_W:], vn], axis=0)
    scw = scw_ref[...]
    ysc = scb_ref[...].astype(F32) * (scw[0:1] * dn + scw[1:2] * v + scw[2:3] * up)
    out = _dot(yssd_ref[...], wo1_ref[...]) + _dot(ysc.astype(BF16), wo2_ref[...])
    x1 = x_ref[...] + m[2:3] * out
    x1_ref[...] = x1
    h2 = _rms(x1) * g2_ref[...] * (1.0 + m[4:5]) + m[3:4]
    h_hi = h2.astype(BF16)
    h_lo = (h2 - h_hi.astype(F32)).astype(BF16)
    lg_ref[...] = lax.dot_general(wrt_ref[...], jnp.concatenate([h_hi, h_lo, h_hi], axis=1),
                                  (((1,), (1,)), ((), ())), preferred_element_type=F32) + br_ref[...]


def _outproj(x2, yssd, scb, v, mod3, scw, wo1, wo2, g2, wrt, br):
    t = x2.shape[0]
    tm = OUT_TILE
    per_b = SEQ // tm
    r = tm // GRID_W
    nrow = t // GRID_W
    const = lambda *shape: pl.BlockSpec(shape, lambda i: (0,) * len(shape))
    tile = lambda w: pl.BlockSpec((tm, w), lambda i: (i, 0))
    return pl.pallas_call(
        _outproj_kernel,
        out_shape=(jax.ShapeDtypeStruct((t, D_MODEL), F32), jax.ShapeDtypeStruct((N_EXPERTS, t), F32)),
        grid=(t // tm,),
        in_specs=[tile(D_MODEL), tile(D_SSD), tile(D_SC), tile(D_SC),
                  pl.BlockSpec((GRID_W, D_SC), lambda i: (jnp.maximum(i * r - 1, 0), 0)),
                  pl.BlockSpec((GRID_W, D_SC), lambda i: (jnp.minimum((i + 1) * r, nrow - 1), 0)),
                  pl.BlockSpec((1, N_MOD, D_MODEL), lambda i: (i // per_b, 0, 0)),
                  const(3, D_SC), const(D_SSD, D_MODEL), const(D_SC, D_MODEL), const(1, D_MODEL),
                  const(N_EXPERTS, 3 * D_MODEL), const(N_EXPERTS, 1)],
        out_specs=(tile(D_MODEL), pl.BlockSpec((N_EXPERTS, tm), lambda i: (0, i))),
        compiler_params=pltpu.CompilerParams(dimension_semantics=("arbitrary",),
                                             vmem_limit_bytes=VMEM_LIMIT),
        name="outproj",
    )(x2, yssd, scb, v, v, v, mod3, scw, wo1, wo2, g2, wrt, br)


def _route_kernel(lg_ref, dest_ref, gate_ref, idx_ref, meta_ref, rank_ref, carry_ref, *, n_tok, n_blocks):
    tt = RT_TILE
    ne = N_EXPERTS
    eio = lax.broadcasted_iota(jnp.int32, (ne, tt), 0)
    si = lax.broadcasted_iota(jnp.int32, (tt, tt), 0)
    ti = lax.broadcasted_iota(jnp.int32, (tt, tt), 1)
    before = (si < ti).astype(BF16)
    carry_ref[...] = jnp.zeros_like(carry_ref)

    def tile_body(j, c):
        t0 = pl.multiple_of(j * tt, tt)
        l = lg_ref[:, pl.ds(t0, tt)]
        onehot = jnp.zeros((ne, tt), F32)
        tops, sels = [], []
        for _ in range(TOP_K):
            mx = jnp.max(l, axis=0, keepdims=True)
            idx = jnp.min(jnp.where(l == mx, eio, ne), axis=0, keepdims=True)
            sel = eio == idx
            l = jnp.where(sel, -jnp.inf, l)
            onehot = onehot + sel.astype(F32)
            tops.append(mx)
            sels.append(sel)
            idx_ref[pl.ds(len(tops) - 1, 1), pl.ds(t0, tt)] = idx
        ex = [jnp.exp(tv - tops[0]) for tv in tops]
        den = ex[0] + ex[1] + ex[2] + ex[3]
        prefix = _dot(onehot.astype(BF16), before) + carry_ref[:, 0:1]
        for k in range(TOP_K):
            gate_ref[pl.ds(k, 1), pl.ds(t0, tt)] = ex[k] / den
            rk = jnp.sum(jnp.where(sels[k], prefix, 0.0), axis=0, keepdims=True)
            rank_ref[pl.ds(k, 1), pl.ds(t0, tt)] = rk.astype(jnp.int32)
        carry_ref[...] = carry_ref[...] + jnp.sum(onehot, axis=1, keepdims=True)
        return c

    lax.fori_loop(0, n_tok // tt, tile_body, 0)

    counts = carry_ref[...]
    padded = jnp.floor((counts + (MOE_BM - 1)) * (1.0 / MOE_BM)) * MOE_BM
    er = lax.broadcasted_iota(jnp.int32, (ne, ne), 0)
    ec = lax.broadcasted_iota(jnp.int32, (ne, ne), 1)
    pad_start = jnp.dot((ec < er).astype(F32), padded, precision=HIGHEST, preferred_element_type=F32)
    pad_end = pad_start + padded

    def dest_body(j, c):
        t0 = pl.multiple_of(j * tt, tt)
        for k in range(TOP_K):
            idx = idx_ref[pl.ds(k, 1), pl.ds(t0, tt)]
            base = jnp.sum(jnp.where(eio == idx, pad_start[:, 0:1], 0.0), axis=0, keepdims=True)
            dest_ref[pl.ds(k, 1), pl.ds(t0, tt)] = base.astype(jnp.int32) + rank_ref[pl.ds(k, 1), pl.ds(t0, tt)]
        return c

    lax.fori_loop(0, n_tok // tt, dest_body, 0)

    width = meta_ref.shape[1]
    sub = lax.broadcasted_iota(jnp.int32, (ne, width), 0)
    lan = lax.broadcasted_iota(jnp.int32, (ne, width), 1)
    diag = sub == lan
    cnt_row = jnp.sum(jnp.where(diag, counts[:, 0:1], 0.0), axis=0, keepdims=True)
    start_row = jnp.sum(jnp.where(diag, pad_start[:, 0:1], 0.0), axis=0, keepdims=True)
    blk_start = (lan * MOE_BM).astype(F32)
    blk_exp = jnp.sum((pad_end[:, 0:1] <= blk_start).astype(F32), axis=0, keepdims=True)
    blk_exp = jnp.minimum(blk_exp, float(ne - 1))
    used = jnp.sum(padded[:, 0:1], axis=0, keepdims=True) * (1.0 / MOE_BM)
    meta_ref[0:1, :] = cnt_row.astype(jnp.int32)
    meta_ref[1:2, :] = start_row.astype(jnp.int32)
    meta_ref[2:3, :] = blk_exp.astype(jnp.int32)
    meta_ref[3:4, :] = jnp.broadcast_to(used, (1, width)).astype(jnp.int32)
    meta_ref[4:8, :] = jnp.zeros((4, width), jnp.int32)


def _route(lgt, n_blocks):
    ne, n_tok = lgt.shape
    width = -(-n_blocks // LANES) * LANES
    full = lambda *shape: pl.BlockSpec(shape, lambda: (0,) * len(shape))
    return pl.pallas_call(
        functools.partial(_route_kernel, n_tok=n_tok, n_blocks=n_blocks),
        out_shape=(jax.ShapeDtypeStruct((TOP_K, n_tok), jnp.int32),
                   jax.ShapeDtypeStruct((TOP_K, n_tok), F32),
                   jax.ShapeDtypeStruct((TOP_K, n_tok), jnp.int32),
                   jax.ShapeDtypeStruct((8, width), jnp.int32)),
        in_specs=[full(ne, n_tok)],
        out_specs=(full(TOP_K, n_tok), full(TOP_K, n_tok), full(TOP_K, n_tok), full(8, width)),
        scratch_shapes=[pltpu.VMEM((TOP_K, n_tok), jnp.int32), pltpu.VMEM((ne, LANES), F32)],
        compiler_params=pltpu.CompilerParams(vmem_limit_bytes=VMEM_LIMIT),
        name="route",
    )(lgt)


def _dispatch_kernel(dest_ref, cnt_ref, start_ref, nu_ref, x1_ref, meta_ref, mod_ref, g2_ref, zsrc_ref, xs_ref,
                     hbuf, sem, zsem):
    i = pl.program_id(0)
    n = pl.num_programs(0)
    tl = DISP_TILE
    slot = i % 2
    nb = xs_ref.shape[0] // MOE_BM

    def zero_block(b):
        return pltpu.make_async_copy(zsrc_ref, xs_ref.at[pl.ds(b * MOE_BM, MOE_BM)], zsem)

    @pl.when(i == 0)
    def _():
        def zero_piece(row0):
            return pltpu.make_async_copy(zsrc_ref.at[pl.ds(0, ZERO_PIECE)], xs_ref.at[pl.ds(row0, ZERO_PIECE)],
                                         zsem)

        def per_expert(e, started):
            rem = cnt_ref[e] & (MOE_BM - 1)
            base = start_ref[e] + cnt_ref[e] - rem
            first = jnp.where(rem > 0, rem // ZERO_PIECE, MOE_BM // ZERO_PIECE)

            def piece(q, c2):
                if started:
                    zero_piece(0).wait()
                else:
                    zero_piece(base + q * ZERO_PIECE).start()
                return c2

            lax.fori_loop(first, MOE_BM // ZERO_PIECE, piece, 0)

        def start_e(e, c):
            per_expert(e, False)
            return c

        def wait_e(e, c):
            per_expert(e, True)
            return c

        def start_t(b, c):
            zero_block(b).start()
            return c

        def wait_t(b, c):
            zero_block(0).wait()
            return c

        lax.fori_loop(0, N_EXPERTS, start_e, 0)
        lax.fori_loop(nu_ref[0], nb, start_t, 0)
        lax.fori_loop(0, N_EXPERTS, wait_e, 0)
        lax.fori_loop(nu_ref[0], nb, wait_t, 0)

    m = mod_ref[0]
    h2 = _rms(x1_ref[...]) * g2_ref[...] * (1.0 + m[4:5]) + m[3:4]
    lo = pltpu.bitcast(h2[:, :PACK_W].astype(BF16).astype(F32), jnp.uint32) >> 16
    hi = pltpu.bitcast(h2[:, PACK_W:].astype(BF16).astype(F32), jnp.uint32) & jnp.uint32(0xFFFF0000)
    row = jnp.concatenate([lo | hi, meta_ref[...], jnp.zeros((tl, D_MODEL - PACK_W - LANES), jnp.uint32)], axis=1)
    hbuf[slot] = row.reshape(tl, SUBLANES, LANES)

    def per_tok(t, c):
        tok = i * tl + t
        for k in range(TOP_K):
            d = dest_ref[k * (dest_ref.shape[0] // TOP_K) + tok]
            pltpu.make_async_copy(hbuf.at[slot, t], xs_ref.at[d], sem.at[slot]).start(priority=k % 2)
        return c

    lax.fori_loop(0, tl, per_tok, 0, unroll=32)

    def wait_slot(sl):
        for _ in range(TOP_K):
            pltpu.make_async_copy(hbuf.at[sl], xs_ref.at[pl.ds(0, tl)], sem.at[sl]).wait()

    @pl.when(i > 0)
    def _():
        wait_slot(1 - slot)

    @pl.when(i == n - 1)
    def _():
        wait_slot(slot)


def _dispatch(dest_flat, cnt, start, n_used, x1, meta_rows, mod3, g2, zsrc, n_rows):
    n_tok = x1.shape[0]
    tl = DISP_TILE
    per_b = SEQ // tl
    return pl.pallas_call(
        _dispatch_kernel,
        out_shape=jax.ShapeDtypeStruct((n_rows, SUBLANES, LANES), jnp.uint32),
        grid_spec=pltpu.PrefetchScalarGridSpec(
            num_scalar_prefetch=4,
            grid=(n_tok // tl,),
            in_specs=[pl.BlockSpec((tl, D_MODEL), lambda i, *_: (i, 0)),
                      pl.BlockSpec((tl, LANES), lambda i, *_: (i, 0)),
                      pl.BlockSpec((1, N_MOD, D_MODEL), lambda i, *_: (i // per_b, 0, 0)),
                      pl.BlockSpec((1, D_MODEL), lambda i, *_: (0, 0)),
                      pl.BlockSpec((MOE_BM, SUBLANES, LANES), lambda i, *_: (0, 0, 0))],
            out_specs=pl.BlockSpec(memory_space=pl.ANY),
            scratch_shapes=[pltpu.VMEM((2, tl, SUBLANES, LANES), jnp.uint32),
                            pltpu.SemaphoreType.DMA((2,)), pltpu.SemaphoreType.DMA]),
        compiler_params=pltpu.CompilerParams(dimension_semantics=("arbitrary",),
                                             vmem_limit_bytes=VMEM_LIMIT),
        name="dispatch",
    )(dest_flat, cnt, start, n_used, x1, meta_rows, mod3, g2, zsrc)


def _expert_rows_kernel(be_ref, nu_ref, xs_ref, wgu_hbm, bgu_ref, wd_hbm, bd_ref, ys_ref,
                        wgu_raw, wd_raw, wgu_bf, wd_bf, w_sem):
    j = pl.program_id(0)
    nu = nu_ref[0]
    n_lt = D_MODEL // LANES

    def weight_copies(e):
        return (pltpu.make_async_copy(wgu_hbm.at[e], wgu_raw, w_sem.at[0]),
                pltpu.make_async_copy(wd_hbm.at[e], wd_raw, w_sem.at[1]))

    @pl.when(j == 0)
    def _():
        for w in weight_copies(be_ref[0]):
            w.start()

    e_now = be_ref[j]
    new_expert = jnp.logical_or(j == 0, be_ref[jnp.maximum(j - 1, 0)] != e_now)

    @pl.when(jnp.logical_and(j < nu, new_expert))
    def _():
        for w in weight_copies(e_now):
            w.wait()
        n_cc = 8
        for c in range(n_cc):
            cc = slice(c * (2 * D_FF // n_cc), (c + 1) * (2 * D_FF // n_cc))
            wgu_bf[:, cc] = wgu_raw[:, cc].astype(BF16)
        for c in range(n_cc // 2):
            cc = slice(c * (2 * D_MODEL // n_cc), (c + 1) * (2 * D_MODEL // n_cc))
            wd_bf[:, cc] = wd_raw[:, cc].astype(BF16)
        j_next = lax.while_loop(lambda t: jnp.logical_and(t < nu, be_ref[jnp.minimum(t, pl.num_programs(0) - 1)] == e_now),
                                lambda t: t + 1, j + 1)

        @pl.when(j_next < nu)
        def _():
            for w in weight_copies(be_ref[jnp.minimum(j_next, pl.num_programs(0) - 1)]):
                w.start()

    @pl.when(j < nu)
    def _():
        e_f = e_now.astype(F32)
        pr = MOE_BM // MOE_PARTS
        for p in range(MOE_PARTS):
            words = xs_ref[pl.ds(p * pr, pr)].reshape(pr, D_MODEL)
            packed = words[:, 0:PACK_W]
            meta = pltpu.bitcast(words[:, PACK_W:PACK_W + LANES], F32)
            xb = jnp.concatenate(
                [pltpu.bitcast(packed << 16, F32).astype(BF16),
                 pltpu.bitcast(packed & jnp.uint32(0xFFFF0000), F32).astype(BF16)], axis=1)
            gate = jnp.zeros((pr, 1), F32)
            for k in range(TOP_K):
                mk = meta[:, META_IDX + k:META_IDX + k + 1] == e_f
                gate = gate + jnp.where(mk, meta[:, META_GATE + k:META_GATE + k + 1], 0.0)
            gu = _dot(xb, wgu_bf[...]) + bgu_ref[0]
            glu = jnp.minimum(gu[:, :D_FF], SWIGLU_LIMIT)
            lin = jnp.clip(gu[:, D_FF:], -SWIGLU_LIMIT, SWIGLU_LIMIT)
            act = glu * jax.nn.sigmoid(SWIGLU_ALPHA * glu) * (lin + 1.0)
            y = (_dot(act.astype(BF16), wd_bf[...]) + bd_ref[0]) * gate
            ys_ref[pl.ds(p * pr, pr)] = y.reshape(pr, n_lt, LANES)

    @pl.when(j >= nu)
    def _():
        ys_ref[...] = jnp.zeros_like(ys_ref)


def _expert_rows(blk_exp, n_used, xs, wgu, bgu, wd, bd):
    n_rows = xs.shape[0]
    nb = n_rows // MOE_BM
    n_lt = D_MODEL // LANES
    row_blk = lambda j, be, nu: (jnp.minimum(j, nu[0] - 1), 0, 0)
    per_e = lambda j, be, nu: (be[j], 0, 0)
    return pl.pallas_call(
        _expert_rows_kernel,
        out_shape=jax.ShapeDtypeStruct((n_rows, n_lt, LANES), F32),
        grid_spec=pltpu.PrefetchScalarGridSpec(
            num_scalar_prefetch=2,
            grid=(nb,),
            in_specs=[pl.BlockSpec((MOE_BM, SUBLANES, LANES), row_blk),
                      pl.BlockSpec(memory_space=pl.ANY),
                      pl.BlockSpec((1, 1, 2 * D_FF), per_e),
                      pl.BlockSpec(memory_space=pl.ANY),
                      pl.BlockSpec((1, 1, D_MODEL), per_e)],
            out_specs=pl.BlockSpec((MOE_BM, n_lt, LANES), lambda j, be, nu: (j, 0, 0)),
            scratch_shapes=[pltpu.VMEM((D_MODEL, 2 * D_FF), F32), pltpu.VMEM((D_FF, D_MODEL), F32),
                            pltpu.VMEM((D_MODEL, 2 * D_FF), BF16), pltpu.VMEM((D_FF, D_MODEL), BF16),
                            pltpu.SemaphoreType.DMA((2,))]),
        compiler_params=pltpu.CompilerParams(dimension_semantics=("arbitrary",),
                                             vmem_limit_bytes=VMEM_LIMIT),
        name="experts",
    )(blk_exp, n_used, xs, wgu, bgu, wd, bd)


def _gather_combine_kernel(dest_ref, ys_ref, x1_ref, mod_ref, fg_ref, o_ref, buf, sem):
    i = pl.program_id(0)
    n = pl.num_programs(0)
    tc = GATHER_TILE
    slot = i % 2

    def issue(tile, sl):
        def per_tok(t, c):
            tok = tile * tc + t
            for k in range(TOP_K):
                d = dest_ref[k * (dest_ref.shape[0] // TOP_K) + tok]
                pltpu.make_async_copy(ys_ref.at[d], buf.at[sl, k * tc + t], sem.at[sl]).start(priority=k % 2)
            return c

        lax.fori_loop(0, tc, per_tok, 0, unroll=32)

    @pl.when(i == 0)
    def _():
        issue(0, 0)

    @pl.when(i + 1 < n)
    def _():
        issue(i + 1, 1 - slot)

    for _ in range(TOP_K):
        pltpu.make_async_copy(ys_ref.at[pl.ds(0, tc)], buf.at[slot, pl.ds(0, tc)], sem.at[slot]).wait()

    moe = ((buf[slot, pl.ds(0, tc)] + buf[slot, pl.ds(tc, tc)])
           + (buf[slot, pl.ds(2 * tc, tc)] + buf[slot, pl.ds(3 * tc, tc)])).reshape(tc, D_MODEL)
    m = mod_ref[0]
    x2 = x1_ref[...] + m[5:6] * moe
    o_ref[...] = _rms(x2) * fg_ref[...]


def _gather_combine(dest_flat, ys, x1, mod3, fg):
    n_tok = x1.shape[0]
    tc = GATHER_TILE
    per_b = SEQ // tc
    n_lt = D_MODEL // LANES
    return pl.pallas_call(
        _gather_combine_kernel,
        out_shape=jax.ShapeDtypeStruct((n_tok, D_MODEL), F32),
        grid_spec=pltpu.PrefetchScalarGridSpec(
            num_scalar_prefetch=1,
            grid=(n_tok // tc,),
            in_specs=[pl.BlockSpec(memory_space=pl.ANY),
                      pl.BlockSpec((tc, D_MODEL), lambda i, d: (i, 0)),
                      pl.BlockSpec((1, N_MOD, D_MODEL), lambda i, d: (i // per_b, 0, 0)),
                      pl.BlockSpec((1, D_MODEL), lambda i, d: (0, 0))],
            out_specs=pl.BlockSpec((tc, D_MODEL), lambda i, d: (i, 0)),
            scratch_shapes=[pltpu.VMEM((2, TOP_K * tc, n_lt, LANES), F32), pltpu.SemaphoreType.DMA((2,))]),
        compiler_params=pltpu.CompilerParams(dimension_semantics=("arbitrary",),
                                             vmem_limit_bytes=VMEM_LIMIT),
        name="combine",
    )(dest_flat, ys, x1, mod3, fg)


def _expansion_matrices(src0):
    r = (jnp.arange(2 * LANES) % LANES)[:, None]
    out64, out128 = [], []
    for d in range(2):
        l64 = jnp.arange(D_SSD)[None, :]
        l128 = jnp.arange(HEADS * LANES)[None, :]
        out64.append((l64 // HEAD_DIM == r - src0 - HEADS * d).astype(BF16))
        out128.append((l128 // LANES == r - src0 - HEADS * d).astype(BF16))
    return jnp.stack(out64), jnp.stack(out128)


def _pad_lanes(v):
    return jnp.pad(v, [(0, 0)] * (v.ndim - 1) + [(0, LANES - v.shape[-1])])


def kernel(x, c, ctx, c_ctx, w_mod, b_mod, norm1_g, w_in, ssd_conv_w, ssd_conv_b, ssd_dt_bias, ssd_a_log,
           ssd_d, ssd_norm_g, sc_conv_w, w_out, norm2_g, w_router, b_router, w_gate_up, b_gate_up, w_down,
           b_down, final_g):
    bsz = x.shape[0]
    n_tok = bsz * SEQ
    n_assign = n_tok * TOP_K
    n_blocks = n_assign // MOE_BM + N_EXPERTS
    n_rows = n_blocks * MOE_BM
    li = 0

    cvec = jnp.concatenate([c, c_ctx[None, :], jnp.zeros((7, D_MODEL), F32)], axis=0)
    mod3 = _mod(cvec, w_mod, b_mod[li][None, :], li).reshape(bsz + 8, N_MOD, D_MODEL)

    w = w_in[li]
    wz = w[:, Z0:X0].astype(BF16)
    wxbc = w[:, X0:DT0].astype(BF16)
    wdt = _pad_lanes(w[:, DT0:SC0]).astype(BF16)
    wb = w[:, SC0:SC0 + D_SC].astype(BF16)
    wc = w[:, SC0 + D_SC:SC0 + 2 * D_SC].astype(BF16)
    wu = w[:, SC0 + 2 * D_SC:].astype(BF16)
    g1 = norm1_g[li][None, :]
    cw = ssd_conv_w[li]
    cb = ssd_conv_b[li][None, :]
    dtb = _pad_lanes(ssd_dt_bias[li].reshape(1, 2 * HEADS))
    alog = _pad_lanes(ssd_a_log[li].reshape(1, 2 * HEADS))
    e64_ctx, _ = _expansion_matrices(0)

    h0 = _ctx_states(ctx, mod3, g1, wxbc[:, :XB_W], wdt, cw[:, :XB_W], cb[:, :XB_W], dtb, alog, e64_ctx)

    rep = lambda a: _pad_lanes(jnp.tile(a[..., :2 * HEADS], (1, DT_COPIES)))
    e64, e128 = _expansion_matrices(DT_DA0)
    x2 = x.reshape(n_tok, D_MODEL)
    z, xbc, dtp, scb, v = _inproj(x2, mod3, g1, wz, wxbc, rep(wdt), rep(dtb), rep(alog), wb, wc, wu)

    dsk = jnp.repeat(ssd_d[li], HEAD_DIM)[None, :]
    yssd = _ssd(xbc.reshape(bsz, SEQ, XBC_W), z.reshape(bsz, SEQ, D_SSD), dtp.reshape(bsz, SEQ, LANES), h0,
                cw, cb, dsk, ssd_norm_g[li][None, :], e64, e128)

    wo = w_out[li].astype(BF16)
    g2 = norm2_g[li][None, :]
    wr = w_router[li].T
    wr_hi = wr.astype(BF16)
    wr_lo = (wr - wr_hi.astype(F32)).astype(BF16)
    x1, lgt = _outproj(x2, yssd.reshape(n_tok, D_SSD), scb, v, mod3, sc_conv_w[li], wo[:D_SSD], wo[D_SSD:],
                       g2, jnp.concatenate([wr_hi, wr_hi, wr_lo], axis=1), b_router[li][:, None])

    dest_t, gate_t, idx_t, meta = _route(lgt, n_blocks)
    dest_flat = dest_t.reshape(n_assign)
    cnt = meta[0, :N_EXPERTS]
    start = meta[1, :N_EXPERTS]
    blk_exp = meta[2, :n_blocks]
    n_used = meta[3, :1]

    meta_rows = lax.bitcast_convert_type(_pad_lanes(jnp.concatenate(
        [idx_t.T.astype(F32), gate_t.T], axis=1)), jnp.uint32)
    pad_meta = lax.bitcast_convert_type(_pad_lanes(jnp.concatenate(
        [jnp.full((MOE_BM, TOP_K), -1.0, F32), jnp.zeros((MOE_BM, TOP_K), F32)], axis=1)), jnp.uint32)
    zsrc = jnp.concatenate([jnp.zeros((MOE_BM, PACK_W), jnp.uint32), pad_meta,
                            jnp.zeros((MOE_BM, D_MODEL - PACK_W - LANES), jnp.uint32)],
                           axis=1).reshape(MOE_BM, SUBLANES, LANES)

    xs = _dispatch(dest_flat, cnt, start, n_used, x1, meta_rows, mod3, g2, zsrc, n_rows)
    ys = _expert_rows(blk_exp, n_used, xs, w_gate_up[li], b_gate_up[li][:, None, :],
                      w_down[li], b_down[li][:, None, :])
    out = _gather_combine(dest_flat, ys, x1, mod3, final_g[None, :])
    return out.reshape(bsz, SEQ, D_MODEL)
```
